```python
import math
import jax, jax.numpy as jnp
from jax import lax
import numpy as np

D_MODEL = 1024
BATCH = 8
SEQ = 16384
DEPTH = 2

HEAD_DIM = 64
CONV_GROUPS = 6
GMLP_HEADS = 6
XATTN_HEADS = 4
CONV_W = CONV_GROUPS * HEAD_DIM
GMLP_W = GMLP_HEADS * HEAD_DIM
XATTN_W = XATTN_HEADS * HEAD_DIM
MIX_W = CONV_W + GMLP_W + XATTN_W
IN_W = 2 * CONV_W + 2 * GMLP_W + XATTN_W
CONV_K = 31
CHUNK = 128
N_MEM = 256
D_FF = 2752
FFN_CONV_K = 3
DEEPNORM_ALPHA = (2.0 * DEPTH) ** 0.25
DEEPNORM_BETA = (8.0 * DEPTH) ** -0.25
LN_EPS = 1e-5

kernel_name = "hybrid_conv_gmlp_xattn_deepnorm"


def _layernorm(x, g, b):
    xf = x.astype(jnp.float32)
    mu = jnp.mean(xf, axis=-1, keepdims=True)
    var = jnp.mean(jnp.square(xf - mu), axis=-1, keepdims=True)
    y = (xf - mu) * lax.rsqrt(var + LN_EPS)
    return (y * g.astype(jnp.float32) + b.astype(jnp.float32)).astype(x.dtype)


def _causal_dwconv(x, w, b):
    k, c = w.shape
    y = lax.conv_general_dilated(
        x, w[:, None, :].astype(x.dtype), window_strides=(1,), padding=[(k - 1, 0)],
        dimension_numbers=("NWC", "WIO", "NWC"), feature_group_count=c)
    return y + b.astype(x.dtype)


def _chunk_spatial_gate(u, v, w_s, b_s, ln_g, ln_b):
    bsz, seq, _ = v.shape
    v = _layernorm(v, ln_g, ln_b)
    vc = v.reshape(bsz, seq // CHUNK, CHUNK, GMLP_HEADS, HEAD_DIM)
    mask = jnp.tril(jnp.ones((CHUNK, CHUNK), dtype=bool))
    ws = jnp.where(mask[None], w_s, jnp.zeros((), w_s.dtype)).astype(v.dtype)
    mixed = jnp.einsum("hts,bnshd->bnthd", ws, vc) + b_s.T[:, :, None].astype(v.dtype)
    return u * mixed.reshape(bsz, seq, GMLP_W)


def _memory_cross_attention(q, mem, w_mk, w_mv):
    bsz, seq, _ = q.shape
    m = mem.shape[1]
    qh = q.reshape(bsz, seq, XATTN_HEADS, HEAD_DIM)
    kh = (mem @ w_mk).reshape(bsz, m, XATTN_HEADS, HEAD_DIM)
    vh = (mem @ w_mv).reshape(bsz, m, XATTN_HEADS, HEAD_DIM)
    s = jnp.einsum("bshd,bmhd->bhsm", qh, kh).astype(jnp.float32) * (1.0 / math.sqrt(HEAD_DIM))
    p = jax.nn.softmax(s, axis=-1).astype(vh.dtype)
    o = jnp.einsum("bhsm,bmhd->bshd", p, vh)
    return o.reshape(bsz, seq, XATTN_W)


def _fwd_setup_inputs(seed: int = 0) -> dict:
    key = jax.random.key(seed)
    ks = jax.random.split(key, 24)
    n = jax.random.normal
    f32 = jnp.float32
    L = DEPTH
    return {
        "x": n(ks[0], (BATCH, SEQ, D_MODEL), f32),
        "mem": n(ks[1], (BATCH, N_MEM, D_MODEL), f32),
        "w_in": n(ks[2], (L, D_MODEL, IN_W), f32) * D_MODEL ** -0.5,
        "conv_a_w": n(ks[3], (L, CONV_K, CONV_W), f32) * CONV_K ** -0.5,
        "conv_a_b": n(ks[4], (L, CONV_W), f32) * 0.02,
        "ln_a_g": 1.0 + 0.05 * n(ks[5], (L, CONV_W), f32),
        "ln_a_b": 0.02 * n(ks[6], (L, CONV_W), f32),
        "ln_v_g": 1.0 + 0.05 * n(ks[7], (L, GMLP_W), f32),
        "ln_v_b": 0.02 * n(ks[8], (L, GMLP_W), f32),
        "w_s": n(ks[9], (L, GMLP_HEADS, CHUNK, CHUNK), f32) * CHUNK ** -0.5,
        "b_s": 1.0 + 0.05 * n(ks[10], (L, GMLP_HEADS, CHUNK), f32),
        "w_mk": n(ks[11], (L, D_MODEL, XATTN_W), f32) * D_MODEL ** -0.5,
        "w_mv": n(ks[12], (L, D_MODEL, XATTN_W), f32) * (D_MODEL ** -0.5 * DEEPNORM_BETA),
        "w_out": n(ks[13], (L, MIX_W, D_MODEL), f32) * (MIX_W ** -0.5 * DEEPNORM_BETA),
        "ln1_g": 1.0 + 0.05 * n(ks[14], (L, D_MODEL), f32),
        "ln1_b": 0.02 * n(ks[15], (L, D_MODEL), f32),
        "w_up": n(ks[16], (L, D_MODEL, 2 * D_FF), f32) * D_MODEL ** -0.5,
        "conv_f_w": n(ks[17], (L, FFN_CONV_K, D_FF), f32) * FFN_CONV_K ** -0.5,
        "conv_f_b": n(ks[18], (L, D_FF), f32) * 0.02,
        "w_down": n(ks[19], (L, D_FF, D_MODEL), f32) * (D_FF ** -0.5 * DEEPNORM_BETA),
        "ln2_g": 1.0 + 0.05 * n(ks[20], (L, D_MODEL), f32),
        "ln2_b": 0.02 * n(ks[21], (L, D_MODEL), f32),
    }


def _fwd_reference(x, mem, w_in, conv_a_w, conv_a_b, ln_a_g, ln_a_b, ln_v_g, ln_v_b, w_s, b_s,
              w_mk, w_mv, w_out, ln1_g, ln1_b, w_up, conv_f_w, conv_f_b, w_down, ln2_g, ln2_b):
    o1 = CONV_W
    o2 = 2 * CONV_W
    o3 = o2 + GMLP_W
    o4 = o3 + GMLP_W
    for l in range(DEPTH):
        h = x @ w_in[l]
        a = h[..., :o1] * jax.nn.sigmoid(h[..., o1:o2])
        a = _causal_dwconv(a, conv_a_w[l], conv_a_b[l])
        a = jax.nn.silu(_layernorm(a, ln_a_g[l], ln_a_b[l]))
        u = jax.nn.gelu(h[..., o2:o3])
        v = jax.nn.gelu(h[..., o3:o4])
        g = _chunk_spatial_gate(u, v, w_s[l], b_s[l], ln_v_g[l], ln_v_b[l])
        c = _memory_cross_attention(h[..., o4:], mem, w_mk[l], w_mv[l])
        mix = jnp.concatenate([a, g, c], axis=-1) @ w_out[l]
        x = _layernorm(DEEPNORM_ALPHA * x + mix, ln1_g[l], ln1_b[l])
        up = x @ w_up[l]
        gate = _causal_dwconv(up[..., :D_FF], conv_f_w[l], conv_f_b[l])
        y = (jax.nn.silu(gate) * up[..., D_FF:]) @ w_down[l]
        x = _layernorm(DEEPNORM_ALPHA * x + y, ln2_g[l], ln2_b[l])
    return x


import jax as _jax
import jax.numpy as _jnp

TWIN_FORMAT = 'train_step'
FWD_PARAMS = ['x', 'mem', 'w_in', 'conv_a_w', 'conv_a_b', 'ln_a_g', 'ln_a_b', 'ln_v_g', 'ln_v_b', 'w_s', 'b_s', 'w_mk', 'w_mv', 'w_out', 'ln1_g', 'ln1_b', 'w_up', 'conv_f_w', 'conv_f_b', 'w_down', 'ln2_g', 'ln2_b']
TWIN_WEIGHTS = ['w_in', 'conv_a_w', 'conv_a_b', 'ln_a_g', 'ln_a_b', 'ln_v_g', 'ln_v_b', 'w_s', 'b_s', 'w_mk', 'w_mv', 'w_out', 'ln1_g', 'ln1_b', 'w_up', 'conv_f_w', 'conv_f_b', 'w_down', 'ln2_g', 'ln2_b']
TWIN_DIFF_INPUT = 'x'
TWIN_INPUTS = ['x', 'mem', 'w_in', 'conv_a_w', 'conv_a_b', 'ln_a_g', 'ln_a_b', 'ln_v_g', 'ln_v_b', 'w_s', 'b_s', 'w_mk', 'w_mv', 'w_out', 'ln1_g', 'ln1_b', 'w_up', 'conv_f_w', 'conv_f_b', 'w_down', 'ln2_g', 'ln2_b', 'loss_target', 'm_w_in', 'm_conv_a_w', 'm_conv_a_b', 'm_ln_a_g', 'm_ln_a_b', 'm_ln_v_g', 'm_ln_v_b', 'm_w_s', 'm_b_s', 'm_w_mk', 'm_w_mv', 'm_w_out', 'm_ln1_g', 'm_ln1_b', 'm_w_up', 'm_conv_f_w', 'm_conv_f_b', 'm_w_down', 'm_ln2_g', 'm_ln2_b', 'v_w_in', 'v_conv_a_w', 'v_conv_a_b', 'v_ln_a_g', 'v_ln_a_b', 'v_ln_v_g', 'v_ln_v_b', 'v_w_s', 'v_b_s', 'v_w_mk', 'v_w_mv', 'v_w_out', 'v_ln1_g', 'v_ln1_b', 'v_w_up', 'v_conv_f_w', 'v_conv_f_b', 'v_w_down', 'v_ln2_g', 'v_ln2_b']
TWIN_OUTPUTS = ['loss', 'grad_x', 'grad_w_in', 'grad_conv_a_w', 'grad_conv_a_b', 'grad_ln_a_g', 'grad_ln_a_b', 'grad_ln_v_g', 'grad_ln_v_b', 'grad_w_s', 'grad_b_s', 'grad_w_mk', 'grad_w_mv', 'grad_w_out', 'grad_ln1_g', 'grad_ln1_b', 'grad_w_up', 'grad_conv_f_w', 'grad_conv_f_b', 'grad_w_down', 'grad_ln2_g', 'grad_ln2_b', 'delta_w_in', 'delta_conv_a_w', 'delta_conv_a_b', 'delta_ln_a_g', 'delta_ln_a_b', 'delta_ln_v_g', 'delta_ln_v_b', 'delta_w_s', 'delta_b_s', 'delta_w_mk', 'delta_w_mv', 'delta_w_out', 'delta_ln1_g', 'delta_ln1_b', 'delta_w_up', 'delta_conv_f_w', 'delta_conv_f_b', 'delta_w_down', 'delta_ln2_g', 'delta_ln2_b', 'new_m_w_in', 'new_m_conv_a_w', 'new_m_conv_a_b', 'new_m_ln_a_g', 'new_m_ln_a_b', 'new_m_ln_v_g', 'new_m_ln_v_b', 'new_m_w_s', 'new_m_b_s', 'new_m_w_mk', 'new_m_w_mv', 'new_m_w_out', 'new_m_ln1_g', 'new_m_ln1_b', 'new_m_w_up', 'new_m_conv_f_w', 'new_m_conv_f_b', 'new_m_w_down', 'new_m_ln2_g', 'new_m_ln2_b', 'new_v_w_in', 'new_v_conv_a_w', 'new_v_conv_a_b', 'new_v_ln_a_g', 'new_v_ln_a_b', 'new_v_ln_v_g', 'new_v_ln_v_b', 'new_v_w_s', 'new_v_b_s', 'new_v_w_mk', 'new_v_w_mv', 'new_v_w_out', 'new_v_ln1_g', 'new_v_ln1_b', 'new_v_w_up', 'new_v_conv_f_w', 'new_v_conv_f_b', 'new_v_w_down', 'new_v_ln2_g', 'new_v_ln2_b']
TWIN_LEAF_KINDS = {'loss': 'loss', 'grad_x': 'grad_x', 'grad_w_in': 'grad_w', 'grad_conv_a_w': 'grad_w', 'grad_conv_a_b': 'grad_w', 'grad_ln_a_g': 'grad_w', 'grad_ln_a_b': 'grad_w', 'grad_ln_v_g': 'grad_w', 'grad_ln_v_b': 'grad_w', 'grad_w_s': 'grad_w', 'grad_b_s': 'grad_w', 'grad_w_mk': 'grad_w', 'grad_w_mv': 'grad_w', 'grad_w_out': 'grad_w', 'grad_ln1_g': 'grad_w', 'grad_ln1_b': 'grad_w', 'grad_w_up': 'grad_w', 'grad_conv_f_w': 'grad_w', 'grad_conv_f_b': 'grad_w', 'grad_w_down': 'grad_w', 'grad_ln2_g': 'grad_w', 'grad_ln2_b': 'grad_w', 'delta_w_in': 'delta_w', 'delta_conv_a_w': 'delta_w', 'delta_conv_a_b': 'delta_w', 'delta_ln_a_g': 'delta_w', 'delta_ln_a_b': 'delta_w', 'delta_ln_v_g': 'delta_w', 'delta_ln_v_b': 'delta_w', 'delta_w_s': 'delta_w', 'delta_b_s': 'delta_w', 'delta_w_mk': 'delta_w', 'delta_w_mv': 'delta_w', 'delta_w_out': 'delta_w', 'delta_ln1_g': 'delta_w', 'delta_ln1_b': 'delta_w', 'delta_w_up': 'delta_w', 'delta_conv_f_w': 'delta_w', 'delta_conv_f_b': 'delta_w', 'delta_w_down': 'delta_w', 'delta_ln2_g': 'delta_w', 'delta_ln2_b': 'delta_w', 'new_m_w_in': 'new_m', 'new_m_conv_a_w': 'new_m', 'new_m_conv_a_b': 'new_m', 'new_m_ln_a_g': 'new_m', 'new_m_ln_a_b': 'new_m', 'new_m_ln_v_g': 'new_m', 'new_m_ln_v_b': 'new_m', 'new_m_w_s': 'new_m', 'new_m_b_s': 'new_m', 'new_m_w_mk': 'new_m', 'new_m_w_mv': 'new_m', 'new_m_w_out': 'new_m', 'new_m_ln1_g': 'new_m', 'new_m_ln1_b': 'new_m', 'new_m_w_up': 'new_m', 'new_m_conv_f_w': 'new_m', 'new_m_conv_f_b': 'new_m', 'new_m_w_down': 'new_m', 'new_m_ln2_g': 'new_m', 'new_m_ln2_b': 'new_m', 'new_v_w_in': 'new_v', 'new_v_conv_a_w': 'new_v', 'new_v_conv_a_b': 'new_v', 'new_v_ln_a_g': 'new_v', 'new_v_ln_a_b': 'new_v', 'new_v_ln_v_g': 'new_v', 'new_v_ln_v_b': 'new_v', 'new_v_w_s': 'new_v', 'new_v_b_s': 'new_v', 'new_v_w_mk': 'new_v', 'new_v_w_mv': 'new_v', 'new_v_w_out': 'new_v', 'new_v_ln1_g': 'new_v', 'new_v_ln1_b': 'new_v', 'new_v_w_up': 'new_v', 'new_v_conv_f_w': 'new_v', 'new_v_conv_f_b': 'new_v', 'new_v_w_down': 'new_v', 'new_v_ln2_g': 'new_v', 'new_v_ln2_b': 'new_v'}


def _forward(args):
    return _fwd_reference(*[args[k] for k in FWD_PARAMS])


def _output_shape():
    def fwd():
        inp = _fwd_setup_inputs(0)
        return _fwd_reference(*[inp[k] for k in FWD_PARAMS])
    out = _jax.eval_shape(fwd)
    return out.shape, out.dtype

N_MICROBATCH = 1
ADAM_LR = 0.001
ADAM_B1 = 0.9
ADAM_B2 = 0.999
ADAM_EPS = 1e-08
ADAM_WD = 0.01
ADAM_STEP = 10
PER_EXAMPLE_BATCH_AXIS = {'x': 0, 'mem': 0, 'loss_target': 0}
SHARED_INPUTS = []
_WEIGHT_DTYPES = {'w_in': _jnp.float32, 'conv_a_w': _jnp.float32, 'conv_a_b': _jnp.float32, 'ln_a_g': _jnp.float32, 'ln_a_b': _jnp.float32, 'ln_v_g': _jnp.float32, 'ln_v_b': _jnp.float32, 'w_s': _jnp.float32, 'b_s': _jnp.float32, 'w_mk': _jnp.float32, 'w_mv': _jnp.float32, 'w_out': _jnp.float32, 'ln1_g': _jnp.float32, 'ln1_b': _jnp.float32, 'w_up': _jnp.float32, 'conv_f_w': _jnp.float32, 'conv_f_b': _jnp.float32, 'w_down': _jnp.float32, 'ln2_g': _jnp.float32, 'ln2_b': _jnp.float32}
MOMENT_SCALE = {'w_in': 7.182122e-02, 'conv_a_w': 8.595267e-02, 'conv_a_b': 5.110699e-01, 'ln_a_g': 1.990806e-01, 'ln_a_b': 3.328939e-01, 'ln_v_g': 5.889752e-02, 'ln_v_b': 6.010252e-02, 'w_s': 4.174693e-02, 'b_s': 5.933062e-02, 'w_mk': 6.849263e-03, 'w_mv': 1.623520e-02, 'w_out': 2.521730e-01, 'ln1_g': 1.185244e+01, 'ln1_b': 2.007051e+00, 'w_up': 4.787141e-02, 'conv_f_w': 4.892720e-02, 'conv_f_b': 5.105931e-02, 'w_down': 1.563616e-01, 'ln2_g': 9.233004e+01, 'ln2_b': 9.868549e+00}


def _to_microbatches(a, axis):
    t = _jnp.moveaxis(a, axis, 0)
    t = t.reshape((N_MICROBATCH, t.shape[0] // N_MICROBATCH) + t.shape[1:])
    return _jnp.moveaxis(t, 1, axis + 1)


def setup_inputs(seed: int = 0) -> dict:
    inp = _fwd_setup_inputs(seed)
    key = _jax.random.fold_in(_jax.random.key(seed), 7919)
    shape, _ = _output_shape()
    out = dict(inp)
    out["loss_target"] = _jax.random.normal(_jax.random.fold_in(key, 0), shape, _jnp.float32)
    for i, name in enumerate(TWIN_WEIGHTS):
        w = inp[name].astype(_jnp.float32)
        if MOMENT_SCALE is None:
            s = _jnp.sqrt(_jnp.mean(_jnp.square(w)) + 1e-30)
        else:
            s = MOMENT_SCALE[name]
        km, kv = _jax.random.split(_jax.random.fold_in(key, i + 1))
        out[name] = w
        out["m_" + name] = s * _jax.random.normal(km, w.shape, _jnp.float32)
        out["v_" + name] = (s * s) * _jax.random.uniform(kv, w.shape, _jnp.float32, 0.5, 1.5)
    if N_MICROBATCH > 1:
        for name, axis in PER_EXAMPLE_BATCH_AXIS.items():
            out[name] = _to_microbatches(out[name], axis)
    return {'x': out['x'], 'mem': out['mem'], 'w_in': out['w_in'], 'conv_a_w': out['conv_a_w'], 'conv_a_b': out['conv_a_b'], 'ln_a_g': out['ln_a_g'], 'ln_a_b': out['ln_a_b'], 'ln_v_g': out['ln_v_g'], 'ln_v_b': out['ln_v_b'], 'w_s': out['w_s'], 'b_s': out['b_s'], 'w_mk': out['w_mk'], 'w_mv': out['w_mv'], 'w_out': out['w_out'], 'ln1_g': out['ln1_g'], 'ln1_b': out['ln1_b'], 'w_up': out['w_up'], 'conv_f_w': out['conv_f_w'], 'conv_f_b': out['conv_f_b'], 'w_down': out['w_down'], 'ln2_g': out['ln2_g'], 'ln2_b': out['ln2_b'], 'loss_target': out['loss_target'], 'm_w_in': out['m_w_in'], 'm_conv_a_w': out['m_conv_a_w'], 'm_conv_a_b': out['m_conv_a_b'], 'm_ln_a_g': out['m_ln_a_g'], 'm_ln_a_b': out['m_ln_a_b'], 'm_ln_v_g': out['m_ln_v_g'], 'm_ln_v_b': out['m_ln_v_b'], 'm_w_s': out['m_w_s'], 'm_b_s': out['m_b_s'], 'm_w_mk': out['m_w_mk'], 'm_w_mv': out['m_w_mv'], 'm_w_out': out['m_w_out'], 'm_ln1_g': out['m_ln1_g'], 'm_ln1_b': out['m_ln1_b'], 'm_w_up': out['m_w_up'], 'm_conv_f_w': out['m_conv_f_w'], 'm_conv_f_b': out['m_conv_f_b'], 'm_w_down': out['m_w_down'], 'm_ln2_g': out['m_ln2_g'], 'm_ln2_b': out['m_ln2_b'], 'v_w_in': out['v_w_in'], 'v_conv_a_w': out['v_conv_a_w'], 'v_conv_a_b': out['v_conv_a_b'], 'v_ln_a_g': out['v_ln_a_g'], 'v_ln_a_b': out['v_ln_a_b'], 'v_ln_v_g': out['v_ln_v_g'], 'v_ln_v_b': out['v_ln_v_b'], 'v_w_s': out['v_w_s'], 'v_b_s': out['v_b_s'], 'v_w_mk': out['v_w_mk'], 'v_w_mv': out['v_w_mv'], 'v_w_out': out['v_w_out'], 'v_ln1_g': out['v_ln1_g'], 'v_ln1_b': out['v_ln1_b'], 'v_w_up': out['v_w_up'], 'v_conv_f_w': out['v_conv_f_w'], 'v_conv_f_b': out['v_conv_f_b'], 'v_w_down': out['v_w_down'], 'v_ln2_g': out['v_ln2_g'], 'v_ln2_b': out['v_ln2_b']}


def _loss(weights, diff, rest, loss_target):
    with _jax.named_scope("forward"):
        args = {**rest, TWIN_DIFF_INPUT: diff, **{k: w.astype(_WEIGHT_DTYPES[k]) for k, w in weights.items()}}
        y = _forward(args)
    with _jax.named_scope("loss_head"):
        err = _jnp.square(y.astype(_jnp.float32) - loss_target)
        return 0.5 * _jnp.sum(_jnp.mean(err, axis=-1)) if err.ndim else 0.5 * err


def _adamw(w, g, m, v):
    m = ADAM_B1 * m + (1.0 - ADAM_B1) * g
    v = ADAM_B2 * v + (1.0 - ADAM_B2) * _jnp.square(g)
    m_hat = m / (1.0 - ADAM_B1 ** ADAM_STEP)
    v_hat = v / (1.0 - ADAM_B2 ** ADAM_STEP)
    delta = -ADAM_LR * (m_hat / (_jnp.sqrt(v_hat) + ADAM_EPS) + ADAM_WD * w)
    return delta, m, v


def reference(x, mem, w_in, conv_a_w, conv_a_b, ln_a_g, ln_a_b, ln_v_g, ln_v_b, w_s, b_s, w_mk, w_mv, w_out, ln1_g, ln1_b, w_up, conv_f_w, conv_f_b, w_down, ln2_g, ln2_b, loss_target, m_w_in, m_conv_a_w, m_conv_a_b, m_ln_a_g, m_ln_a_b, m_ln_v_g, m_ln_v_b, m_w_s, m_b_s, m_w_mk, m_w_mv, m_w_out, m_ln1_g, m_ln1_b, m_w_up, m_conv_f_w, m_conv_f_b, m_w_down, m_ln2_g, m_ln2_b, v_w_in, v_conv_a_w, v_conv_a_b, v_ln_a_g, v_ln_a_b, v_ln_v_g, v_ln_v_b, v_w_s, v_b_s, v_w_mk, v_w_mv, v_w_out, v_ln1_g, v_ln1_b, v_w_up, v_conv_f_w, v_conv_f_b, v_w_down, v_ln2_g, v_ln2_b):
    given = dict(x=x, mem=mem, w_in=w_in, conv_a_w=conv_a_w, conv_a_b=conv_a_b, ln_a_g=ln_a_g, ln_a_b=ln_a_b, ln_v_g=ln_v_g, ln_v_b=ln_v_b, w_s=w_s, b_s=b_s, w_mk=w_mk, w_mv=w_mv, w_out=w_out, ln1_g=ln1_g, ln1_b=ln1_b, w_up=w_up, conv_f_w=conv_f_w, conv_f_b=conv_f_b, w_down=w_down, ln2_g=ln2_g, ln2_b=ln2_b, loss_target=loss_target, m_w_in=m_w_in, m_conv_a_w=m_conv_a_w, m_conv_a_b=m_conv_a_b, m_ln_a_g=m_ln_a_g, m_ln_a_b=m_ln_a_b, m_ln_v_g=m_ln_v_g, m_ln_v_b=m_ln_v_b, m_w_s=m_w_s, m_b_s=m_b_s, m_w_mk=m_w_mk, m_w_mv=m_w_mv, m_w_out=m_w_out, m_ln1_g=m_ln1_g, m_ln1_b=m_ln1_b, m_w_up=m_w_up, m_conv_f_w=m_conv_f_w, m_conv_f_b=m_conv_f_b, m_w_down=m_w_down, m_ln2_g=m_ln2_g, m_ln2_b=m_ln2_b, v_w_in=v_w_in, v_conv_a_w=v_conv_a_w, v_conv_a_b=v_conv_a_b, v_ln_a_g=v_ln_a_g, v_ln_a_b=v_ln_a_b, v_ln_v_g=v_ln_v_g, v_ln_v_b=v_ln_v_b, v_w_s=v_w_s, v_b_s=v_b_s, v_w_mk=v_w_mk, v_w_mv=v_w_mv, v_w_out=v_w_out, v_ln1_g=v_ln1_g, v_ln1_b=v_ln1_b, v_w_up=v_w_up, v_conv_f_w=v_conv_f_w, v_conv_f_b=v_conv_f_b, v_w_down=v_w_down, v_ln2_g=v_ln2_g, v_ln2_b=v_ln2_b)
    weights = {n: given[n] for n in TWIN_WEIGHTS}
    shared = {n: given[n] for n in SHARED_INPUTS}
    per_example = {n: given[n] for n in ['x', 'mem']}
    grad_fn = _jax.value_and_grad(_loss, argnums=(0, 1))

    def one_microbatch(ex, loss_target):
        ex = dict(ex)
        diff = ex.pop(TWIN_DIFF_INPUT)
        return grad_fn(weights, diff, {**shared, **ex}, loss_target)

    if N_MICROBATCH == 1:
        loss, (grad_w, grad_x) = one_microbatch(per_example, given["loss_target"])
    else:
        def body(carry, xs):
            loss_sum, grad_sum = carry
            l_k, (gw_k, gx_k) = one_microbatch(xs[0], xs[1])
            with _jax.named_scope("update"):
                return (loss_sum + l_k, _jax.tree.map(_jnp.add, grad_sum, gw_k)), gx_k

        init = (_jnp.zeros((), _jnp.float32), _jax.tree.map(_jnp.zeros_like, weights))
        (loss, grad_w), grad_x = _jax.lax.scan(body, init, (per_example, given["loss_target"]))
    with _jax.named_scope("update"):
        delta_w, new_m, new_v = {}, {}, {}
        for n in TWIN_WEIGHTS:
            delta_w[n], new_m[n], new_v[n] = _adamw(weights[n], grad_w[n], given["m_" + n], given["v_" + n])
    return (loss, grad_x, *[grad_w[n] for n in TWIN_WEIGHTS], *[delta_w[n] for n in TWIN_WEIGHTS],
            *[new_m[n] for n in TWIN_WEIGHTS], *[new_v[n] for n in TWIN_WEIGHTS])
```

```python
import functools

import jax
import jax.numpy as jnp
from jax import lax
from jax.experimental import pallas as pl
from jax.experimental.pallas import tpu as pltpu

F32 = jnp.float32
BF16 = jnp.bfloat16

D_MODEL = 1024
DEPTH = 2
CONV_W = 384
GMLP_W = 384
XATTN_W = 256
XATTN_HEADS = 4
HEAD_DIM = 64
IN_W = 1792
CONV_K = 31
CHUNK = 128
N_MEM = 256
D_FF = 2752
FFN_CONV_K = 3
ALPHA = (2.0 * DEPTH) ** 0.25
LN_EPS = 1e-5
ATT_SCALE = 1.0 / 8.0
ADAM_LR, ADAM_B1, ADAM_B2, ADAM_EPS, ADAM_WD, ADAM_STEP = 0.001, 0.9, 0.999, 1e-08, 0.01, 10

N_CHIPS = 4
FF_Q = D_FF // N_CHIPS
FF_QP = 704
FF_H = 2 * FF_QP
FF_P = 4 * FF_QP
LANE = 128
CONV_HALO = 32
FFN_HALO = 8
BF16_ROWS = 16
VMEM_LIMIT = 60 * 1024 * 1024

MESH = pl.DeviceIdType.MESH


def _cparams(sem=None, vmem=VMEM_LIMIT):
    kw = {"vmem_limit_bytes": vmem}
    if sem is not None:
        kw["dimension_semantics"] = sem
    return pltpu.CompilerParams(**kw)


def _row_tile(rows, row_bytes, limit=2 << 20):
    if rows * row_bytes <= limit:
        return rows
    best = None
    for cand in range(BF16_ROWS, rows, BF16_ROWS):
        if rows % cand == 0 and cand * row_bytes <= limit:
            best = cand
    assert best is not None, (rows, row_bytes)
    return best


def _const_spec(shape):
    nd = len(shape)
    return pl.BlockSpec(shape, lambda *_: (0,) * nd)


def _w_spec(shape):
    nd = len(shape)
    return pl.BlockSpec(shape, lambda *_: (0,) * nd, pipeline_mode=pl.Buffered(1))


def _sigmoid(x):
    return jax.nn.sigmoid(x)


def _gelu(x):
    return jax.nn.gelu(x)


def _gelu_grad(x):
    c = 0.7978845608028654
    a = 0.044715
    t = jnp.tanh(c * (x + a * x * x * x))
    return 0.5 * (1.0 + t) + 0.5 * x * (1.0 - t * t) * c * (1.0 + 3.0 * a * x * x)


def _ln_fwd(z):
    mu = jnp.mean(z, axis=-1, keepdims=True)
    zc = z - mu
    var = jnp.mean(zc * zc, axis=-1, keepdims=True)
    rstd = lax.rsqrt(var + LN_EPS)
    return zc * rstd, rstd


def _ln_bwd(dxh, xh, rstd):
    m1 = jnp.mean(dxh, axis=-1, keepdims=True)
    m2 = jnp.mean(dxh * xh, axis=-1, keepdims=True)
    return rstd * (dxh - m1 - xh * m2)


def _colsum(a):
    return jnp.sum(a, axis=0, keepdims=True)


def _dot(a, b):
    return jnp.dot(a, b, preferred_element_type=F32)


def _dot_tn(a, b):
    return lax.dot_general(a, b, (((0,), (0,)), ((), ())), preferred_element_type=F32)


def _dot_nt(a, b):
    return lax.dot_general(a, b, (((1,), (1,)), ((), ())), preferred_element_type=F32)


def _softmax_heads(sc):
    ps = []
    for hd in range(XATTN_HEADS):
        s = sc[:, hd * N_MEM:(hd + 1) * N_MEM]
        e = jnp.exp(s - jnp.max(s, axis=-1, keepdims=True))
        ps.append(e / jnp.sum(e, axis=-1, keepdims=True))
    return jnp.concatenate(ps, axis=1)


def _lane_lo(shape):
    return (lax.broadcasted_iota(jnp.int32, shape, len(shape) - 1) % LANE) < HEAD_DIM


def _spatial_mix(vnb, wst_ref, bst_ref, mix_ref, t):
    lo = _lane_lo((CHUNK, LANE))
    for n in range(t // CHUNK):
        rows = slice(n * CHUNK, (n + 1) * CHUNK)
        for j in range(GMLP_W // LANE):
            cols = slice(j * LANE, (j + 1) * LANE)
            r = _dot(wst_ref[j], vnb[rows, cols])
            mix_ref[rows, cols] = jnp.where(lo, r[:CHUNK], r[CHUNK:]) + bst_ref[:, cols]


def _kv_fwd(mem, w_mk, w_mv):
    def body(mem_ref, wk_ref, wv_ref, kt_ref, k_ref, v_ref, vt_ref):
        mb = mem_ref[...].astype(BF16)
        k = _dot(mb, wk_ref[...])
        v = _dot(mb, wv_ref[...])
        col = lax.broadcasted_iota(jnp.int32, (N_MEM, XATTN_W), 1) // HEAD_DIM
        ks = [jnp.where(col == hd, k, 0.0) for hd in range(XATTN_HEADS)]
        vs = [jnp.where(col == hd, v, 0.0) for hd in range(XATTN_HEADS)]
        k_all = jnp.concatenate(ks, axis=0)
        v_all = jnp.concatenate(vs, axis=0)
        k_ref[...] = k_all.astype(BF16)
        v_ref[...] = v_all.astype(BF16)
        kt_ref[...] = jnp.concatenate([x.T for x in ks], axis=1).astype(BF16)
        vt_ref[...] = jnp.concatenate([x.T for x in vs], axis=1).astype(BF16)

    wide = jax.ShapeDtypeStruct((XATTN_W, XATTN_HEADS * N_MEM), BF16)
    tall = jax.ShapeDtypeStruct((XATTN_HEADS * N_MEM, XATTN_W), BF16)
    return pl.pallas_call(body, name="kv_fwd", out_shape=(wide, tall, tall, wide),
                          compiler_params=_cparams())(mem, w_mk, w_mv)


def _kv_bwd(mem, dkt_all, dv_all):
    def body(mem_ref, dkt_ref, dv_ref, gk_ref, gv_ref):
        col = lax.broadcasted_iota(jnp.int32, (N_MEM, XATTN_W), 1) // HEAD_DIM
        dk = jnp.zeros((N_MEM, XATTN_W), F32)
        dv = jnp.zeros((N_MEM, XATTN_W), F32)
        for hd in range(XATTN_HEADS):
            dk = dk + jnp.where(col == hd, dkt_ref[:, hd * N_MEM:(hd + 1) * N_MEM].T, 0.0)
            dv = dv + jnp.where(col == hd, dv_ref[hd * N_MEM:(hd + 1) * N_MEM, :], 0.0)
        mb = mem_ref[...].astype(BF16)
        gk_ref[...] = _dot_tn(mb, dk.astype(BF16))
        gv_ref[...] = _dot_tn(mb, dv.astype(BF16))

    out = jax.ShapeDtypeStruct((D_MODEL, XATTN_W), F32)
    return pl.pallas_call(body, name="kv_bwd", out_shape=(out, out), compiler_params=_cparams())(mem, dkt_all, dv_all)


PA_CONV_B, PA_LNA_G, PA_LNA_B, PA_LNV_G, PA_LNV_B = 0, 1, 2, 3, 4
PD_GIN, PD_BIN, PD_GOUT, PD_BOUT = 0, 1, 2, 3


def _mixer_fwd(xin, pd, w_in, conv_w, pa, wst, bst, kt_all, v_all, w_out, t):
    s = xin.shape[0]
    nt = s // t

    def body(x_ref, pd_ref, win_ref, cw_ref, pa_ref, wst_ref, bst_ref, kt_ref, v_ref, wout_ref,
             xh_ref, rstd_ref, h_ref, ac_ref, cat_ref, cbuf, mixbuf):
        i = pl.program_id(0)
        x = x_ref[...] * pd_ref[PD_GIN:PD_GIN + 1, :] + pd_ref[PD_BIN:PD_BIN + 1, :]
        h = _dot(x.astype(BF16), win_ref[...])
        h_ref[...] = h
        a1, a2 = h[:, 0:CONV_W], h[:, CONV_W:2 * CONV_W]
        hu, hv = h[:, 2 * CONV_W:2 * CONV_W + GMLP_W], h[:, 2 * CONV_W + GMLP_W:2 * CONV_W + 2 * GMLP_W]
        q = h[:, IN_W - XATTN_W:]

        @pl.when(i == 0)
        def _():
            cbuf[0:CONV_HALO, :] = jnp.zeros((CONV_HALO, CONV_W), F32)

        cbuf[CONV_HALO:CONV_HALO + t, :] = a1 * _sigmoid(a2)
        ac = jnp.zeros((t, CONV_W), F32) + pa_ref[PA_CONV_B:PA_CONV_B + 1, :]
        for k in range(CONV_K):
            off = CONV_HALO - (CONV_K - 1) + k
            ac = ac + cbuf[off:off + t, :] * cw_ref[k:k + 1, :]
        ac_ref[...] = ac
        cbuf[0:CONV_HALO, :] = cbuf[t:t + CONV_HALO, :]
        xh_a, _ = _ln_fwd(ac)
        an = xh_a * pa_ref[PA_LNA_G:PA_LNA_G + 1, :] + pa_ref[PA_LNA_B:PA_LNA_B + 1, :]
        a = an * _sigmoid(an)

        u = _gelu(hu)
        xh_v, _ = _ln_fwd(_gelu(hv))
        vn = xh_v * pa_ref[PA_LNV_G:PA_LNV_G + 1, :] + pa_ref[PA_LNV_B:PA_LNV_B + 1, :]
        _spatial_mix(vn.astype(BF16), wst_ref, bst_ref, mixbuf, t)
        g = u * mixbuf[...]

        p = _softmax_heads(_dot(q.astype(BF16), kt_ref[...]) * ATT_SCALE)
        o = _dot(p.astype(BF16), v_ref[...])

        cat = jnp.concatenate([a, g, o], axis=1).astype(BF16)
        cat_ref[...] = cat
        z = ALPHA * x + _dot(cat, wout_ref[...])
        xh, rstd = _ln_fwd(z)
        xh_ref[...] = xh
        rstd_ref[...] = rstd

    tok = lambda w: pl.BlockSpec((t, w), lambda i: (i, 0))
    return pl.pallas_call(
        body, name="mixer_fwd", grid=(nt,),
        in_specs=[tok(D_MODEL), _const_spec((8, D_MODEL)), _w_spec((D_MODEL, IN_W)), _const_spec((CONV_HALO, CONV_W)),
                  _const_spec((8, CONV_W)), _const_spec((3, 2 * CHUNK, CHUNK)), _const_spec((CHUNK, GMLP_W)),
                  _w_spec((XATTN_W, XATTN_HEADS * N_MEM)), _w_spec((XATTN_HEADS * N_MEM, XATTN_W)),
                  _w_spec((D_MODEL, D_MODEL))],
        out_specs=[tok(D_MODEL), tok(1), tok(IN_W), tok(CONV_W), tok(D_MODEL)],
        out_shape=[jax.ShapeDtypeStruct((s, D_MODEL), F32), jax.ShapeDtypeStruct((s, 1), F32),
                   jax.ShapeDtypeStruct((s, IN_W), F32), jax.ShapeDtypeStruct((s, CONV_W), F32),
                   jax.ShapeDtypeStruct((s, D_MODEL), BF16)],
        scratch_shapes=[pltpu.VMEM((t + CONV_HALO, CONV_W), F32), pltpu.VMEM((t, GMLP_W), F32)],
        compiler_params=_cparams(("arbitrary",)),
    )(xin, pd, w_in, conv_w, pa, wst, bst, kt_all, v_all, w_out)


VD_LN_G, VD_LN_B, VD_LOSS = 0, 1, 2
VA_CONV_B, VA_LNA_G, VA_LNA_B, VA_LNV_G, VA_LNV_B = 0, 1, 2, 3, 4


def _mixer_bwd_d(gz, xh1, rstd1, h, ac, pd, conv_w, pa, wst, wstt, bst, kt_all, k_all, vt_all, wout_t, win_t, t):
    s = gz.shape[0]
    nt = s // t

    def body(gz_ref, xh_ref, rstd_ref, h_ref, ac_ref, pd_ref, cw_ref, pa_ref, wst_ref, wstt_ref, bst_ref,
             kt_ref, k_ref, vt_ref, woutt_ref, wint_ref,
             dx_ref, dh_ref, dmix_ref, vd_ref, va_ref, dcw_ref, dws_ref, dbs_ref, dkt_ref, dv_ref,
             ebuf, mixbuf, dvnbuf, dbsacc):
        i = pl.program_id(0)

        @pl.when(i == 0)
        def _():
            vd_ref[...] = jnp.zeros_like(vd_ref)
            va_ref[...] = jnp.zeros_like(va_ref)
            dcw_ref[...] = jnp.zeros_like(dcw_ref)
            dws_ref[...] = jnp.zeros_like(dws_ref)
            dbs_ref[...] = jnp.zeros_like(dbs_ref)
            dkt_ref[...] = jnp.zeros_like(dkt_ref)
            dv_ref[...] = jnp.zeros_like(dv_ref)
            dbsacc[...] = jnp.zeros_like(dbsacc)
            ebuf[t:t + CONV_HALO, :] = jnp.zeros((CONV_HALO, CONV_W), F32)

        gz_v = gz_ref[...]
        xh = xh_ref[...]
        vd_ref[VD_LN_G:VD_LN_G + 1, :] += _colsum(gz_v * xh)
        vd_ref[VD_LN_B:VD_LN_B + 1, :] += _colsum(gz_v)
        dz = _ln_bwd(gz_v * pd_ref[PD_GOUT:PD_GOUT + 1, :], xh, rstd_ref[...])
        dzb = dz.astype(BF16)
        dmix_ref[...] = dzb
        dcat = _dot(dzb, woutt_ref[...])
        d_a, d_g, d_o = dcat[:, 0:CONV_W], dcat[:, CONV_W:CONV_W + GMLP_W], dcat[:, CONV_W + GMLP_W:]

        h = h_ref[...]
        a1, a2 = h[:, 0:CONV_W], h[:, CONV_W:2 * CONV_W]
        hu, hv = h[:, 2 * CONV_W:2 * CONV_W + GMLP_W], h[:, 2 * CONV_W + GMLP_W:2 * CONV_W + 2 * GMLP_W]
        q = h[:, IN_W - XATTN_W:]

        xh_a, rstd_a = _ln_fwd(ac_ref[...])
        an = xh_a * pa_ref[PA_LNA_G:PA_LNA_G + 1, :] + pa_ref[PA_LNA_B:PA_LNA_B + 1, :]
        sig = _sigmoid(an)
        d_an = d_a * (sig * (1.0 + an * (1.0 - sig)))
        va_ref[VA_LNA_G:VA_LNA_G + 1, :] += _colsum(d_an * xh_a)
        va_ref[VA_LNA_B:VA_LNA_B + 1, :] += _colsum(d_an)
        dac = _ln_bwd(d_an * pa_ref[PA_LNA_G:PA_LNA_G + 1, :], xh_a, rstd_a)
        va_ref[VA_CONV_B:VA_CONV_B + 1, :] += _colsum(dac)
        ebuf[0:t, :] = dac
        sg = _sigmoid(a2)
        glu = a1 * sg
        dglu = jnp.zeros((t, CONV_W), F32)
        for k in range(CONV_K):
            off = CONV_K - 1 - k
            ek = ebuf[off:off + t, :]
            dglu = dglu + ek * cw_ref[k:k + 1, :]
            dcw_ref[k:k + 1, :] += _colsum(ek * glu)
        ebuf[t:t + CONV_HALO, :] = ebuf[0:CONV_HALO, :]
        da1 = dglu * sg
        da2 = dglu * a1 * sg * (1.0 - sg)

        u = _gelu(hu)
        xh_v, rstd_v = _ln_fwd(_gelu(hv))
        vn = xh_v * pa_ref[PA_LNV_G:PA_LNV_G + 1, :] + pa_ref[PA_LNV_B:PA_LNV_B + 1, :]
        vnb = vn.astype(BF16)
        _spatial_mix(vnb, wst_ref, bst_ref, mixbuf, t)
        dhu = d_g * mixbuf[...] * _gelu_grad(hu)
        dm = d_g * u
        dmb = dm.astype(BF16)
        lo = _lane_lo((CHUNK, LANE))
        for n in range(t // CHUNK):
            rows = slice(n * CHUNK, (n + 1) * CHUNK)
            dbsacc[...] += dm[rows, :]
            for j in range(GMLP_W // LANE):
                cols = slice(j * LANE, (j + 1) * LANE)
                dm_blk = dmb[rows, cols]
                r = _dot(wstt_ref[j], dm_blk)
                dvnbuf[rows, cols] = jnp.where(lo, r[:CHUNK], r[CHUNK:])
                zero = jnp.zeros_like(dm_blk)
                st = jnp.concatenate([jnp.where(lo, dm_blk, zero), jnp.where(lo, zero, dm_blk)], axis=0)
                dws_ref[j] += _dot_nt(st, vnb[rows, cols])
        dvn = dvnbuf[...]
        va_ref[VA_LNV_G:VA_LNV_G + 1, :] += _colsum(dvn * xh_v)
        va_ref[VA_LNV_B:VA_LNV_B + 1, :] += _colsum(dvn)
        dhv = _ln_bwd(dvn * pa_ref[PA_LNV_G:PA_LNV_G + 1, :], xh_v, rstd_v) * _gelu_grad(hv)

        qb = q.astype(BF16)
        p = _softmax_heads(_dot(qb, kt_ref[...]) * ATT_SCALE)
        dob = d_o.astype(BF16)
        dp = _dot(dob, vt_ref[...])
        dss = []
        for hd in range(XATTN_HEADS):
            cs = slice(hd * N_MEM, (hd + 1) * N_MEM)
            ph, dph = p[:, cs], dp[:, cs]
            dss.append(ph * (dph - jnp.sum(ph * dph, axis=-1, keepdims=True)) * ATT_SCALE)
        dsb = jnp.concatenate(dss, axis=1).astype(BF16)
        dq = _dot(dsb, k_ref[...])
        dkt_ref[...] += _dot_tn(qb, dsb)
        dv_ref[...] += _dot_tn(p.astype(BF16), dob)

        dhb = jnp.concatenate([da1, da2, dhu, dhv, dq], axis=1).astype(BF16)
        dh_ref[...] = dhb
        dx_ref[...] = ALPHA * dz + _dot(dhb, wint_ref[...])

        @pl.when(i == nt - 1)
        def _():
            acc = dbsacc[...]
            head = lax.broadcasted_iota(jnp.int32, (CHUNK, GMLP_W), 1) // HEAD_DIM
            lane = lax.broadcasted_iota(jnp.int32, (CHUNK, LANE), 1)
            out = jnp.zeros((CHUNK, LANE), F32)
            for hd in range(GMLP_W // HEAD_DIM):
                sh = jnp.sum(jnp.where(head == hd, acc, 0.0), axis=1, keepdims=True)
                out = out + jnp.where(lane == hd, sh, 0.0)
            dbs_ref[...] = out

    rev = lambda w: pl.BlockSpec((t, w), lambda i: (nt - 1 - i, 0))
    out_shape = [
        jax.ShapeDtypeStruct((s, D_MODEL), F32), jax.ShapeDtypeStruct((s, IN_W), BF16),
        jax.ShapeDtypeStruct((s, D_MODEL), BF16),
        jax.ShapeDtypeStruct((8, D_MODEL), F32), jax.ShapeDtypeStruct((8, CONV_W), F32),
        jax.ShapeDtypeStruct((CONV_HALO, CONV_W), F32), jax.ShapeDtypeStruct((3, 2 * CHUNK, CHUNK), F32),
        jax.ShapeDtypeStruct((CHUNK, LANE), F32),
        jax.ShapeDtypeStruct((XATTN_W, XATTN_HEADS * N_MEM), F32), jax.ShapeDtypeStruct((XATTN_HEADS * N_MEM, XATTN_W), F32),
    ]
    out_specs = [rev(D_MODEL), rev(IN_W), rev(D_MODEL)] + [_const_spec(o.shape) for o in out_shape[3:]]
    return pl.pallas_call(
        body, name="mixer_bwd_d", grid=(nt,),
        in_specs=[rev(D_MODEL), rev(D_MODEL), rev(1), rev(IN_W), rev(CONV_W),
                  _const_spec((8, D_MODEL)), _const_spec((CONV_HALO, CONV_W)), _const_spec((8, CONV_W)),
                  _const_spec((3, 2 * CHUNK, CHUNK)), _const_spec((3, 2 * CHUNK, CHUNK)), _const_spec((CHUNK, GMLP_W)),
                  _w_spec((XATTN_W, XATTN_HEADS * N_MEM)), _w_spec((XATTN_HEADS * N_MEM, XATTN_W)),
                  _w_spec((XATTN_W, XATTN_HEADS * N_MEM)), _w_spec((D_MODEL, D_MODEL)), _w_spec((IN_W, D_MODEL))],
        out_specs=out_specs, out_shape=out_shape,
        scratch_shapes=[pltpu.VMEM((t + CONV_HALO, CONV_W), F32), pltpu.VMEM((t, GMLP_W), F32),
                        pltpu.VMEM((t, GMLP_W), F32), pltpu.VMEM((CHUNK, GMLP_W), F32)],
        compiler_params=_cparams(("arbitrary",)),
    )(gz, xh1, rstd1, h, ac, pd, conv_w, pa, wst, wstt, bst, kt_all, k_all, vt_all, wout_t, win_t)


def _mixer_bwd_w(xin, pd, dh, cat, dmix, t):
    s = xin.shape[0]
    nt = s // t

    def body(x_ref, pd_ref, dh_ref, cat_ref, dmix_ref, dwin_ref, dwout_ref):
        @pl.when(pl.program_id(0) == 0)
        def _():
            dwin_ref[...] = jnp.zeros_like(dwin_ref)
            dwout_ref[...] = jnp.zeros_like(dwout_ref)

        xb = (x_ref[...] * pd_ref[PD_GIN:PD_GIN + 1, :] + pd_ref[PD_BIN:PD_BIN + 1, :]).astype(BF16)
        dwin_ref[...] += _dot_tn(xb, dh_ref[...])
        dwout_ref[...] += _dot_tn(cat_ref[...], dmix_ref[...])

    tok = lambda w: pl.BlockSpec((t, w), lambda i: (i, 0))
    return pl.pallas_call(
        body, name="mixer_bwd_w", grid=(nt,),
        in_specs=[tok(D_MODEL), _const_spec((8, D_MODEL)), tok(IN_W), tok(D_MODEL), tok(D_MODEL)],
        out_specs=[_const_spec((D_MODEL, IN_W)), _const_spec((D_MODEL, D_MODEL))],
        out_shape=[jax.ShapeDtypeStruct((D_MODEL, IN_W), F32), jax.ShapeDtypeStruct((D_MODEL, D_MODEL), F32)],
        compiler_params=_cparams(("arbitrary",)),
    )(xin, pd, dh, cat, dmix)


PF_W0, PF_B = 0, 3


def _ffn_fwd(xh1, pd, w_g, w_v, pf, w_d, t):
    s = xh1.shape[0]
    nt = s // t

    def body(xh_ref, pd_ref, wg_ref, wv_ref, pf_ref, wd_ref, xh2_ref, rstd_ref, upg_ref, upv_ref, fbuf):
        i = pl.program_id(0)

        @pl.when(i == 0)
        def _():
            fbuf[0:FFN_HALO, :] = jnp.zeros((FFN_HALO, FF_P), F32)

        x1 = xh_ref[...] * pd_ref[PD_GIN:PD_GIN + 1, :] + pd_ref[PD_BIN:PD_BIN + 1, :]
        xb = x1.astype(BF16)
        y = jnp.zeros((t, D_MODEL), F32)
        for hf in range(2):
            cs = slice(hf * FF_H, (hf + 1) * FF_H)
            ug = _dot(xb, wg_ref[:, cs])
            uv = _dot(xb, wv_ref[:, cs])
            upg_ref[:, cs] = ug.astype(BF16)
            upv_ref[:, cs] = uv.astype(BF16)
            fbuf[FFN_HALO:FFN_HALO + t, cs] = ug
            gate = jnp.zeros((t, FF_H), F32) + pf_ref[PF_B:PF_B + 1, cs]
            for k in range(FFN_CONV_K):
                off = FFN_HALO - (FFN_CONV_K - 1) + k
                gate = gate + fbuf[off:off + t, cs] * pf_ref[PF_W0 + k:PF_W0 + k + 1, cs]
            fbuf[0:FFN_HALO, cs] = fbuf[t:t + FFN_HALO, cs]
            hm = gate * _sigmoid(gate) * uv
            y = y + _dot(hm.astype(BF16), wd_ref[cs, :])
        xh2, rstd = _ln_fwd(ALPHA * x1 + y)
        xh2_ref[...] = xh2
        rstd_ref[...] = rstd

    tok = lambda w: pl.BlockSpec((t, w), lambda i: (i, 0))
    return pl.pallas_call(
        body, name="ffn_fwd", grid=(nt,),
        in_specs=[tok(D_MODEL), _const_spec((8, D_MODEL)), _w_spec((D_MODEL, FF_P)), _w_spec((D_MODEL, FF_P)),
                  _const_spec((8, FF_P)), _w_spec((FF_P, D_MODEL))],
        out_specs=[tok(D_MODEL), tok(1), tok(FF_P), tok(FF_P)],
        out_shape=[jax.ShapeDtypeStruct((s, D_MODEL), F32), jax.ShapeDtypeStruct((s, 1), F32),
                   jax.ShapeDtypeStruct((s, FF_P), BF16), jax.ShapeDtypeStruct((s, FF_P), BF16)],
        scratch_shapes=[pltpu.VMEM((t + FFN_HALO, FF_P), F32)],
        compiler_params=_cparams(("arbitrary",)),
    )(xh1, pd, w_g, w_v, pf, w_d)


VF_W0, VF_B = 0, 3


def _ffn_bwd_d(gz_or_target, xh2, rstd2, upg, upv, pd, pf, wd_t, wg_t, wv_t, t, last):
    s = xh2.shape[0]
    nt = s // t
    hb = t // BF16_ROWS

    def body(gz_ref, xh2_ref, rstd_ref, upg_ref, halo_ref, upv_ref, pd_ref, pf_ref, wdt_ref, wgt_ref, wvt_ref,
             dx_ref, dy_ref, dug_ref, duv_ref, hm_ref, vd_ref, vf_ref, gbuf, ebuf):
        i = pl.program_id(0)
        first_tile = i == nt - 1

        @pl.when(i == 0)
        def _():
            vd_ref[...] = jnp.zeros_like(vd_ref)
            vf_ref[...] = jnp.zeros_like(vf_ref)
            ebuf[t:t + FFN_HALO, :] = jnp.zeros((FFN_HALO, FF_P), F32)

        xh2_v = xh2_ref[...]
        if last:
            diff = xh2_v * pd_ref[PD_GOUT:PD_GOUT + 1, :] + pd_ref[PD_BOUT:PD_BOUT + 1, :] - gz_ref[...]
            vd_ref[VD_LOSS:VD_LOSS + 1, :] += _colsum(diff * diff)
            gz_v = diff * (1.0 / D_MODEL)
        else:
            gz_v = gz_ref[...]
        vd_ref[VD_LN_G:VD_LN_G + 1, :] += _colsum(gz_v * xh2_v)
        vd_ref[VD_LN_B:VD_LN_B + 1, :] += _colsum(gz_v)
        dz = _ln_bwd(gz_v * pd_ref[PD_GOUT:PD_GOUT + 1, :], xh2_v, rstd_ref[...])
        dyb = dz.astype(BF16)
        dy_ref[...] = dyb
        dx = ALPHA * dz
        for hf in range(2):
            cs = slice(hf * FF_H, (hf + 1) * FF_H)
            ug = upg_ref[:, cs].astype(F32)
            uv = upv_ref[:, cs].astype(F32)
            halo = halo_ref[:, cs].astype(F32)
            gbuf[0:BF16_ROWS, :] = jnp.where(first_tile, jnp.zeros_like(halo), halo)
            gbuf[BF16_ROWS:BF16_ROWS + t, :] = ug
            gate = jnp.zeros((t, FF_H), F32) + pf_ref[PF_B:PF_B + 1, cs]
            for k in range(FFN_CONV_K):
                off = BF16_ROWS - (FFN_CONV_K - 1) + k
                gate = gate + gbuf[off:off + t, :] * pf_ref[PF_W0 + k:PF_W0 + k + 1, cs]
            sig = _sigmoid(gate)
            sl = gate * sig
            hm_ref[:, cs] = (sl * uv).astype(BF16)
            dhm = _dot(dyb, wdt_ref[:, cs])
            duv = dhm * sl
            dgate = dhm * uv * (sig * (1.0 + gate * (1.0 - sig)))
            vf_ref[VF_B:VF_B + 1, cs] += _colsum(dgate)
            ebuf[0:t, cs] = dgate
            dug = jnp.zeros((t, FF_H), F32)
            for k in range(FFN_CONV_K):
                off = FFN_CONV_K - 1 - k
                ek = ebuf[off:off + t, cs]
                dug = dug + ek * pf_ref[PF_W0 + k:PF_W0 + k + 1, cs]
                vf_ref[VF_W0 + k:VF_W0 + k + 1, cs] += _colsum(ek * ug)
            ebuf[t:t + FFN_HALO, cs] = ebuf[0:FFN_HALO, cs]
            dugb = dug.astype(BF16)
            duvb = duv.astype(BF16)
            dug_ref[:, cs] = dugb
            duv_ref[:, cs] = duvb
            dx = dx + _dot(dugb, wgt_ref[cs, :]) + _dot(duvb, wvt_ref[cs, :])
        dx_ref[...] = dx

        if last:
            @pl.when(i == nt - 1)
            def _():
                tot = jnp.sum(vd_ref[VD_LOSS:VD_LOSS + 1, :], axis=1, keepdims=True)
                vd_ref[VD_LOSS:VD_LOSS + 1, :] = jnp.zeros((1, D_MODEL), F32) + tot

    rev = lambda w: pl.BlockSpec((t, w), lambda i: (nt - 1 - i, 0))
    halo_spec = pl.BlockSpec((BF16_ROWS, FF_P), lambda i: (jnp.maximum((nt - 1 - i) * hb - 1, 0), 0))
    out_shape = [jax.ShapeDtypeStruct((s, D_MODEL), F32), jax.ShapeDtypeStruct((s, D_MODEL), BF16),
                 jax.ShapeDtypeStruct((s, FF_P), BF16), jax.ShapeDtypeStruct((s, FF_P), BF16),
                 jax.ShapeDtypeStruct((s, FF_P), BF16),
                 jax.ShapeDtypeStruct((8, D_MODEL), F32), jax.ShapeDtypeStruct((8, FF_P), F32)]
    return pl.pallas_call(
        body, name="ffn_bwd_d_last" if last else "ffn_bwd_d", grid=(nt,),
        in_specs=[rev(D_MODEL), rev(D_MODEL), rev(1), rev(FF_P), halo_spec, rev(FF_P),
                  _const_spec((8, D_MODEL)), _const_spec((8, FF_P)),
                  _w_spec((D_MODEL, FF_P)), _w_spec((FF_P, D_MODEL)), _w_spec((FF_P, D_MODEL))],
        out_specs=[rev(D_MODEL), rev(D_MODEL), rev(FF_P), rev(FF_P), rev(FF_P),
                   _const_spec((8, D_MODEL)), _const_spec((8, FF_P))],
        out_shape=out_shape,
        scratch_shapes=[pltpu.VMEM((t + BF16_ROWS, FF_H), F32), pltpu.VMEM((t + FFN_HALO, FF_P), F32)],
        compiler_params=_cparams(("arbitrary",)),
    )(gz_or_target, xh2, rstd2, upg, upg, upv, pd, pf, wd_t, wg_t, wv_t)


def _ffn_bwd_w(xh1, pd, dy, dug, duv, hm, t):
    s = xh1.shape[0]
    nt = s // t

    def body(xh_ref, pd_ref, dy_ref, dug_ref, duv_ref, hm_ref, dwg_ref, dwv_ref, dwdt_ref):
        @pl.when(pl.program_id(1) == 0)
        def _():
            dwg_ref[...] = jnp.zeros_like(dwg_ref)
            dwv_ref[...] = jnp.zeros_like(dwv_ref)
            dwdt_ref[...] = jnp.zeros_like(dwdt_ref)

        xb = (xh_ref[...] * pd_ref[PD_GIN:PD_GIN + 1, :] + pd_ref[PD_BIN:PD_BIN + 1, :]).astype(BF16)
        dwg_ref[...] += _dot_tn(xb, dug_ref[...])
        dwv_ref[...] += _dot_tn(xb, duv_ref[...])
        dwdt_ref[...] += _dot_tn(dy_ref[...], hm_ref[...])

    tok = lambda w: pl.BlockSpec((t, w), lambda c, i: (i, 0))
    half = pl.BlockSpec((t, FF_H), lambda c, i: (i, c))
    acc = pl.BlockSpec((D_MODEL, FF_H), lambda c, i: (0, c))
    wshape = jax.ShapeDtypeStruct((D_MODEL, FF_P), F32)
    return pl.pallas_call(
        body, name="ffn_bwd_w", grid=(2, nt),
        in_specs=[tok(D_MODEL), pl.BlockSpec((8, D_MODEL), lambda c, i: (0, 0)), tok(D_MODEL), half, half, half],
        out_specs=[acc, acc, acc], out_shape=[wshape, wshape, wshape],
        compiler_params=_cparams(("arbitrary", "arbitrary")),
    )(xh1, pd, dy, dug, duv, hm)


def _adamw(w, g, m, v, name):
    rows, cols = w.shape
    tr = _row_tile(rows, cols * 4)

    def body(w_ref, g_ref, m_ref, v_ref, d_ref, nm_ref, nv_ref):
        gv = g_ref[...]
        nm = ADAM_B1 * m_ref[...] + (1.0 - ADAM_B1) * gv
        nv = ADAM_B2 * v_ref[...] + (1.0 - ADAM_B2) * (gv * gv)
        m_hat = nm / (1.0 - ADAM_B1 ** ADAM_STEP)
        v_hat = nv / (1.0 - ADAM_B2 ** ADAM_STEP)
        d_ref[...] = -ADAM_LR * (m_hat / (jnp.sqrt(v_hat) + ADAM_EPS) + ADAM_WD * w_ref[...])
        nm_ref[...] = nm
        nv_ref[...] = nv

    blk = pl.BlockSpec((tr, cols), lambda i: (i, 0))
    sh = jax.ShapeDtypeStruct((rows, cols), F32)
    return pl.pallas_call(body, name=name, grid=(rows // tr,), in_specs=[blk] * 4, out_specs=[blk] * 3,
                          out_shape=[sh, sh, sh], compiler_params=_cparams(("arbitrary",)))(w, g, m, v)


ANY = pl.BlockSpec(memory_space=pl.ANY)


def _my_place():
    x, y, c = lax.axis_index("x"), lax.axis_index("y"), lax.axis_index("c")
    chips = [(1 - x, y), (x, 1 - y), (1 - x, 1 - y)]
    return x, y, c, chips


def _all_gather_chips(xs, name):
    r, cdim = xs.shape
    hh = r // 2

    def body(x_ref, out_ref, send_sems, recv_sems, local_sem):
        x, y, c, chips = _my_place()
        sibling = (x, y, 1 - c)

        def rows(px, py, pc):
            return out_ref.at[2 * px + py, pl.ds(pc * hh, hh), :]

        def copy(k, block, to, src=None):
            return pltpu.make_async_remote_copy(
                src_ref=rows(*block) if src is None else src, dst_ref=rows(*block),
                send_sem=send_sems.at[k], recv_sem=recv_sems.at[k], device_id=to, device_id_type=MESH)

        me = (x, y, c)
        mine_src = x_ref.at[pl.ds(c * hh, hh), :]
        mine = pltpu.make_async_copy(mine_src, rows(*me), local_sem)
        mine.start()
        first = [copy(0, me, sibling, src=mine_src)]
        first += [copy(1 + j, me, (*chip, c), src=mine_src) for j, chip in enumerate(chips)]
        for cp in first:
            cp.start()
        passed = [copy(4 + j, (*chip, c), sibling) for j, chip in enumerate(chips)]
        for j, chip in enumerate(chips):
            copy(1 + j, (*chip, c), me).wait_recv()
            passed[j].start()
        copy(0, sibling, me).wait_recv()
        for j, chip in enumerate(chips):
            copy(4 + j, (*chip, 1 - c), me).wait_recv()
        for cp in first + passed:
            cp.wait_send()
        mine.wait()

    return pl.pallas_call(
        body, name=name, out_shape=jax.ShapeDtypeStruct((N_CHIPS, r, cdim), xs.dtype),
        in_specs=[ANY], out_specs=ANY,
        scratch_shapes=[pltpu.SemaphoreType.DMA((7,)), pltpu.SemaphoreType.DMA((7,)), pltpu.SemaphoreType.DMA],
    )(xs)


def _sibling_swap_halves(g):
    _, _, hh, cdim = g.shape

    def body(g_ref, out_ref, send_sems, recv_sems):
        x, y, c, _ = _my_place()
        cps = [pltpu.make_async_remote_copy(
            src_ref=g_ref.at[j, 1 - c], dst_ref=out_ref.at[j], send_sem=send_sems.at[j], recv_sem=recv_sems.at[j],
            device_id=(x, y, 1 - c), device_id_type=MESH) for j in range(N_CHIPS)]
        for cp in cps:
            cp.start()
        for cp in cps:
            cp.wait()

    return pl.pallas_call(
        body, name="rs_sibling", out_shape=jax.ShapeDtypeStruct((N_CHIPS, hh, cdim), g.dtype),
        in_specs=[ANY], out_specs=ANY,
        scratch_shapes=[pltpu.SemaphoreType.DMA((N_CHIPS,)), pltpu.SemaphoreType.DMA((N_CHIPS,))],
    )(g)


def _chip_scatter(cbuf):
    _, hh, cdim = cbuf.shape

    def body(c_ref, out_ref, send_sems, recv_sems, local_sem):
        x, y, c, chips = _my_place()
        mej = 2 * x + y
        mine = pltpu.make_async_copy(c_ref.at[mej], out_ref.at[mej], local_sem)
        mine.start()
        cps = []
        for k, (px, py) in enumerate(chips):
            pj = 2 * px + py
            cps.append((pltpu.make_async_remote_copy(
                src_ref=c_ref.at[pj], dst_ref=out_ref.at[mej], send_sem=send_sems.at[k], recv_sem=recv_sems.at[k],
                device_id=(px, py, c), device_id_type=MESH),
                pltpu.make_async_remote_copy(
                src_ref=c_ref.at[pj], dst_ref=out_ref.at[pj], send_sem=send_sems.at[k], recv_sem=recv_sems.at[k],
                device_id=(px, py, c), device_id_type=MESH)))
        for snd, _ in cps:
            snd.start()
        for snd, rcv in cps:
            rcv.wait_recv()
            snd.wait_send()
        mine.wait()

    return pl.pallas_call(
        body, name="rs_chips", out_shape=jax.ShapeDtypeStruct(cbuf.shape, cbuf.dtype),
        in_specs=[ANY], out_specs=ANY,
        scratch_shapes=[pltpu.SemaphoreType.DMA((3,)), pltpu.SemaphoreType.DMA((3,)), pltpu.SemaphoreType.DMA],
    )(cbuf)


def _sibling_gather(f):
    hh, cdim = f.shape

    def body(f_ref, out_ref, send_sem, recv_sem, local_sem):
        x, y, c, _ = _my_place()
        mine = pltpu.make_async_copy(f_ref, out_ref.at[c], local_sem)
        mine.start()
        snd = pltpu.make_async_remote_copy(src_ref=f_ref, dst_ref=out_ref.at[c], send_sem=send_sem, recv_sem=recv_sem,
                                           device_id=(x, y, 1 - c), device_id_type=MESH)
        rcv = pltpu.make_async_remote_copy(src_ref=f_ref, dst_ref=out_ref.at[1 - c], send_sem=send_sem, recv_sem=recv_sem,
                                           device_id=(x, y, 1 - c), device_id_type=MESH)
        snd.start()
        rcv.wait_recv()
        snd.wait_send()
        mine.wait()

    return pl.pallas_call(
        body, name="rs_sibling_gather", out_shape=jax.ShapeDtypeStruct((2, hh, cdim), f.dtype),
        in_specs=[ANY], out_specs=ANY,
        scratch_shapes=[pltpu.SemaphoreType.DMA, pltpu.SemaphoreType.DMA, pltpu.SemaphoreType.DMA],
    )(f)


def _all_gather_devices(xs):
    m_per, n = xs.shape

    def body(x_ref, out_ref, send_sems, recv_sems, local_sem):
        x, y, c, chips = _my_place()
        me, sibling = (x, y, c), (x, y, 1 - c)

        def rows(px, py, pc):
            return out_ref.at[pl.ds((4 * px + 2 * py + pc) * m_per, m_per), :]

        def copy(k, block, to, src=None):
            return pltpu.make_async_remote_copy(
                src_ref=rows(*block) if src is None else src, dst_ref=rows(*block),
                send_sem=send_sems.at[k], recv_sem=recv_sems.at[k], device_id=to, device_id_type=MESH)

        mine = pltpu.make_async_copy(x_ref, rows(*me), local_sem)
        mine.start()
        first = [copy(0, me, sibling, src=x_ref)]
        first += [copy(1 + j, me, (*chip, c), src=x_ref) for j, chip in enumerate(chips)]
        for cp in first:
            cp.start()
        passed = [copy(4 + j, (*chip, c), sibling) for j, chip in enumerate(chips)]
        for j, chip in enumerate(chips):
            copy(1 + j, (*chip, c), me).wait_recv()
            passed[j].start()
        copy(0, sibling, me).wait_recv()
        for j, chip in enumerate(chips):
            copy(4 + j, (*chip, 1 - c), me).wait_recv()
        for cp in first + passed:
            cp.wait_send()
        mine.wait()

    return pl.pallas_call(
        body, name="ag_small_grads", out_shape=jax.ShapeDtypeStruct((8 * m_per, n), xs.dtype),
        in_specs=[pl.BlockSpec(memory_space=pltpu.VMEM)], out_specs=pl.BlockSpec(memory_space=pltpu.VMEM),
        scratch_shapes=[pltpu.SemaphoreType.DMA((7,)), pltpu.SemaphoreType.DMA((7,)), pltpu.SemaphoreType.DMA],
        compiler_params=_cparams(),
    )(xs)


def _add_halves(g, recv, c_idx):
    _, _, hh, cdim = g.shape
    tr = _row_tile(hh, cdim * 4)

    def body(c_ref, g_ref, r_ref, o_ref):
        o_ref[...] = (g_ref[...] + r_ref[...]).astype(BF16)

    grid_spec = pltpu.PrefetchScalarGridSpec(
        num_scalar_prefetch=1, grid=(N_CHIPS, hh // tr),
        in_specs=[pl.BlockSpec((None, None, tr, cdim), lambda j, i, c: (j, c[0], i, 0)),
                  pl.BlockSpec((None, tr, cdim), lambda j, i, c: (j, i, 0))],
        out_specs=pl.BlockSpec((None, tr, cdim), lambda j, i, c: (j, i, 0)))
    return pl.pallas_call(body, name="rs_add", grid_spec=grid_spec,
                          out_shape=jax.ShapeDtypeStruct((N_CHIPS, hh, cdim), BF16),
                          compiler_params=_cparams(("arbitrary", "arbitrary")))(c_idx, g, recv)


def _sum_slots(slots):
    _, hh, cdim = slots.shape
    tr = _row_tile(hh, cdim * 4, limit=1 << 20)

    def body(s_ref, o_ref):
        acc = s_ref[0].astype(F32)
        for j in range(1, N_CHIPS):
            acc = acc + s_ref[j].astype(F32)
        o_ref[...] = acc

    return pl.pallas_call(body, name="rs_sum", grid=(hh // tr,),
                          in_specs=[pl.BlockSpec((N_CHIPS, tr, cdim), lambda i: (0, i, 0))],
                          out_specs=pl.BlockSpec((tr, cdim), lambda i: (i, 0)),
                          out_shape=jax.ShapeDtypeStruct((hh, cdim), F32),
                          compiler_params=_cparams(("arbitrary",)))(slots)


def _sum_devices(gathered, m_per):
    def body(g_ref, o_ref):
        acc = g_ref[0:m_per, :]
        for d in range(1, 8):
            acc = acc + g_ref[d * m_per:(d + 1) * m_per, :]
        o_ref[...] = acc

    return pl.pallas_call(body, name="small_sum", out_shape=jax.ShapeDtypeStruct((m_per, LANE), F32),
                          compiler_params=_cparams())(gathered)


SMALL_ROWS = 1000


def _pad_ff_cols(a):
    lead = a.shape[:-1]
    n = a.shape[-1] // FF_Q
    a = a.reshape(*lead, n, FF_Q)
    a = jnp.pad(a, [(0, 0)] * len(lead) + [(0, 0), (0, FF_QP - FF_Q)])
    return a.reshape(*lead, n * FF_QP)


def _unpad_ff_cols(a):
    lead = a.shape[:-1]
    n = a.shape[-1] // FF_QP
    return a.reshape(*lead, n, FF_QP)[..., :FF_Q].reshape(*lead, n * FF_Q)


def _pack_weight_shard(l, w_in, w_mk, w_mv, w_out, w_up, w_down):
    up = _pad_ff_cols(w_up[l])
    down = jnp.pad(w_down[l], ((0, FF_QP - FF_Q), (0, 0)))
    parts = [w_in[l].reshape(-1, D_MODEL), w_mk[l].reshape(-1, D_MODEL), w_mv[l].reshape(-1, D_MODEL),
             w_out[l], up.reshape(-1, D_MODEL), down]
    return jnp.concatenate([p.astype(BF16) for p in parts], axis=0)


W_ROWS = [IN_W // 4, 64, 64, 256, FF_H, FF_QP]
W_ROWS_LAYER = sum(W_ROWS)


def _unpack_weights(gathered, l):
    offs = [l * W_ROWS_LAYER]
    for r in W_ROWS:
        offs.append(offs[-1] + r)
    seg = [gathered[:, offs[k]:offs[k + 1], :] for k in range(len(W_ROWS))]
    w_in = seg[0].reshape(N_CHIPS, D_MODEL, IN_W // 4).transpose(1, 0, 2).reshape(D_MODEL, IN_W)
    w_mk = seg[1].reshape(D_MODEL, XATTN_W)
    w_mv = seg[2].reshape(D_MODEL, XATTN_W)
    w_out = seg[3].reshape(D_MODEL, D_MODEL)
    up = seg[4].reshape(N_CHIPS, D_MODEL, FF_H)
    w_g = jnp.concatenate([up[0], up[1]], axis=1)
    w_v = jnp.concatenate([up[2], up[3]], axis=1)
    w_d = seg[5].reshape(FF_P, D_MODEL)
    return w_in, w_mk, w_mv, w_out, w_g, w_v, w_d


def _pack_big_grads(dw_in, dw_mk, dw_mv, dw_out, dw_g, dw_v, dw_dt):
    g_in = dw_in.reshape(D_MODEL, N_CHIPS, IN_W // 4).transpose(1, 0, 2).reshape(N_CHIPS, -1, D_MODEL)
    g_mk = dw_mk.reshape(N_CHIPS, -1, D_MODEL)
    g_mv = dw_mv.reshape(N_CHIPS, -1, D_MODEL)
    g_out = dw_out.reshape(N_CHIPS, -1, D_MODEL)
    up = jnp.concatenate([dw_g, dw_v], axis=1)
    g_up = up.reshape(D_MODEL, N_CHIPS, FF_H).transpose(1, 0, 2).reshape(N_CHIPS, -1, D_MODEL)
    g_down = dw_dt.T.reshape(N_CHIPS, FF_QP, D_MODEL)
    return jnp.concatenate([g_in, g_mk, g_mv, g_out, g_up, g_down], axis=1)


def _unpack_big_grads(f):
    shapes = [(D_MODEL, IN_W // 4), (D_MODEL // 4, XATTN_W), (D_MODEL // 4, XATTN_W), (D_MODEL // 4, D_MODEL),
              (D_MODEL, FF_H), (FF_QP, D_MODEL)]
    outs = []
    for k, shp in enumerate(shapes):
        per_layer = []
        for l in range(DEPTH):
            o = l * W_ROWS_LAYER + sum(W_ROWS[:k])
            per_layer.append(f[o:o + W_ROWS[k], :].reshape(shp))
        outs.append(jnp.stack(per_layer))
    outs[4] = _unpad_ff_cols(outs[4])
    outs[5] = outs[5][:, :FF_Q, :]
    return outs


def _pack_small(parts):
    flat = jnp.concatenate([p.reshape(-1) for p in parts])
    return flat.reshape(-1, LANE)


SMALL_SHAPES = [("conv_a_w", (CONV_K, CONV_W)), ("conv_a_b", (CONV_W,)), ("ln_a_g", (CONV_W,)), ("ln_a_b", (CONV_W,)),
                ("ln_v_g", (GMLP_W,)), ("ln_v_b", (GMLP_W,)), ("w_s", (6, CHUNK, CHUNK)), ("b_s", (6, CHUNK)),
                ("ln1_g", (D_MODEL,)), ("ln1_b", (D_MODEL,)), ("conv_f_w", (FFN_CONV_K, D_FF)), ("conv_f_b", (D_FF,)),
                ("ln2_g", (D_MODEL,)), ("ln2_b", (D_MODEL,))]


def _unpack_small(flat2d):
    flat = flat2d.reshape(DEPTH, -1)
    out, o = {}, 0
    for name, shp in SMALL_SHAPES:
        n = 1
        for d in shp:
            n *= d
        out[name] = flat[:, o:o + n].reshape((DEPTH,) + shp)
        o += n
    return out


def _rows8(rows, width):
    blk = jnp.stack([jnp.pad(v.astype(F32), (0, width - v.shape[0])) for v in rows])
    return jnp.pad(blk, ((0, 8 - len(rows)), (0, 0)))


def kernel(x, mem, w_in, conv_a_w, conv_a_b, ln_a_g, ln_a_b, ln_v_g, ln_v_b, w_s, b_s, w_mk, w_mv, w_out, ln1_g, ln1_b, w_up, conv_f_w, conv_f_b, w_down, ln2_g, ln2_b, loss_target, m_w_in, m_conv_a_w, m_conv_a_b, m_ln_a_g, m_ln_a_b, m_ln_v_g, m_ln_v_b, m_w_s, m_b_s, m_w_mk, m_w_mv, m_w_out, m_ln1_g, m_ln1_b, m_w_up, m_conv_f_w, m_conv_f_b, m_w_down, m_ln2_g, m_ln2_b, v_w_in, v_conv_a_w, v_conv_a_b, v_ln_a_g, v_ln_a_b, v_ln_v_g, v_ln_v_b, v_w_s, v_b_s, v_w_mk, v_w_mv, v_w_out, v_ln1_g, v_ln1_b, v_w_up, v_conv_f_w, v_conv_f_b, v_w_down, v_ln2_g, v_ln2_b):
    seq = x.shape[1]
    t_fwd = min(512, seq)
    t_bwd = min(256, seq)
    chip = 2 * lax.axis_index("x") + lax.axis_index("y")
    core = lax.axis_index("c")
    x0 = x[0]
    mem0 = mem[0]
    target = loss_target[0]

    shard = jnp.concatenate([_pack_weight_shard(l, w_in, w_mk, w_mv, w_out, w_up, w_down) for l in range(DEPTH)], axis=0)
    gathered = _all_gather_chips(shard, "ag_weights")
    small_w = _pack_small([conv_a_w, conv_f_w, jnp.zeros((2 * 80 * LANE - conv_a_w.size - conv_f_w.size,), F32)])
    small_g = _all_gather_chips(small_w, "ag_conv_weights").reshape(N_CHIPS, -1)
    n_ca = DEPTH * CONV_K * (CONV_W // 4)
    conv_a_full = small_g[:, :n_ca].reshape(N_CHIPS, DEPTH, CONV_K, CONV_W // 4).transpose(1, 2, 0, 3).reshape(DEPTH, CONV_K, CONV_W)
    conv_f_full = small_g[:, n_ca:n_ca + DEPTH * FFN_CONV_K * FF_Q].reshape(N_CHIPS, DEPTH, FFN_CONV_K, FF_Q).transpose(1, 2, 0, 3).reshape(DEPTH, FFN_CONV_K, D_FF)

    tril = jnp.tril(jnp.ones((CHUNK, CHUNK), dtype=bool))
    ones = jnp.ones((D_MODEL,), F32)
    zeros = jnp.zeros((D_MODEL,), F32)

    lay = []
    for l in range(DEPTH):
        wi, wk, wv, wo, wg, wvv, wd = _unpack_weights(gathered, l)
        ws_m = jnp.where(tril[None], w_s[l], 0.0)
        p = dict(
            w_in=wi, w_in_t=wi.T, w_mk=wk, w_mv=wv, w_out=wo, w_out_t=wo.T,
            w_g=wg, w_v=wvv, w_d=wd, w_g_t=wg.T, w_v_t=wvv.T, w_d_t=wd.T,
            conv_w=jnp.pad(conv_a_full[l], ((0, CONV_HALO - CONV_K), (0, 0))),
            pa=_rows8([conv_a_b[l], ln_a_g[l], ln_a_b[l], ln_v_g[l], ln_v_b[l]], CONV_W),
            wst=ws_m.reshape(3, 2 * CHUNK, CHUNK).astype(BF16),
            wstt=ws_m.transpose(0, 2, 1).reshape(3, 2 * CHUNK, CHUNK).astype(BF16),
            bst=jnp.repeat(b_s[l].T, HEAD_DIM, axis=1),
            pf=_rows8([_pad_ff_cols(conv_f_full[l][k]) for k in range(FFN_CONV_K)] + [_pad_ff_cols(conv_f_b[l])], FF_P),
        )
        gin, bin_ = (ones, zeros) if l == 0 else (ln2_g[l - 1], ln2_b[l - 1])
        p["pd_mix"] = _rows8([gin, bin_, ln1_g[l]], D_MODEL)
        p["pd_ffn"] = _rows8([ln1_g[l], ln1_b[l], ln2_g[l], ln2_b[l]], D_MODEL)
        lay.append(p)

    acts = []
    xin = x0
    for l in range(DEPTH):
        p = lay[l]
        kt_all, k_all, v_all, vt_all = _kv_fwd(mem0, p["w_mk"], p["w_mv"])
        xh1, rstd1, h, ac, cat = _mixer_fwd(xin, p["pd_mix"], p["w_in"], p["conv_w"], p["pa"], p["wst"], p["bst"],
                                            kt_all, v_all, p["w_out"], t_fwd)
        xh2, rstd2, upg, upv = _ffn_fwd(xh1, p["pd_ffn"], p["w_g"], p["w_v"], p["pf"], p["w_d"], t_fwd)
        acts.append(dict(xin=xin, kt_all=kt_all, k_all=k_all, vt_all=vt_all, xh1=xh1, rstd1=rstd1, h=h, ac=ac, cat=cat,
                         xh2=xh2, rstd2=rstd2, upg=upg, upv=upv))
        xin = xh2

    big, small = [None] * DEPTH, [None] * DEPTH
    gz = target
    loss_sum = None
    for l in reversed(range(DEPTH)):
        p, a = lay[l], acts[l]
        last = l == DEPTH - 1
        dx1, dy, dug, duv, hm, vd2, vf = _ffn_bwd_d(gz, a["xh2"], a["rstd2"], a["upg"], a["upv"], p["pd_ffn"], p["pf"],
                                                    p["w_d_t"], p["w_g_t"], p["w_v_t"], t_bwd, last)
        if last:
            loss_sum = vd2[VD_LOSS, 0]
        dw_g, dw_v, dw_dt = _ffn_bwd_w(a["xh1"], p["pd_ffn"], dy, dug, duv, hm, t_fwd)
        dx0, dh, dmix, vd1, va, dcw, dws, dbs, dkt, dv = _mixer_bwd_d(
            dx1, a["xh1"], a["rstd1"], a["h"], a["ac"], p["pd_mix"], p["conv_w"], p["pa"], p["wst"], p["wstt"], p["bst"],
            a["kt_all"], a["k_all"], a["vt_all"], p["w_out_t"], p["w_in_t"], t_bwd)
        dw_in, dw_out = _mixer_bwd_w(a["xin"], p["pd_mix"], dh, a["cat"], dmix, t_fwd)
        dw_mk, dw_mv = _kv_bwd(mem0, dkt, dv)
        big[l] = _pack_big_grads(dw_in, dw_mk, dw_mv, dw_out, dw_g, dw_v, dw_dt)
        dws6 = jnp.where(tril[None], dws.reshape(6, CHUNK, CHUNK), 0.0)
        small[l] = [dcw[:CONV_K], va[VA_CONV_B], va[VA_LNA_G], va[VA_LNA_B], va[VA_LNV_G], va[VA_LNV_B], dws6,
                    dbs[:, :6].T, vd1[VD_LN_G], vd1[VD_LN_B],
                    jnp.stack([_unpad_ff_cols(vf[VF_W0 + k]) for k in range(FFN_CONV_K)]), _unpad_ff_cols(vf[VF_B]),
                    vd2[VD_LN_G], vd2[VD_LN_B]]
        gz = dx0
    grad_x = gz[None]

    g_all = jnp.concatenate(big, axis=1)
    rows = g_all.shape[1]
    g4 = g_all.reshape(N_CHIPS, 2, rows // 2, D_MODEL)
    recv = _sibling_swap_halves(g4)
    partial = _add_halves(g4, recv, jnp.reshape(core, (1,)).astype(jnp.int32))
    slots = _chip_scatter(partial)
    reduced_half = _sum_slots(slots)
    reduced = _sibling_gather(reduced_half).reshape(rows, D_MODEL)
    g_w_in, g_w_mk, g_w_mv, g_w_out, g_w_up, g_w_down = _unpack_big_grads(reduced)

    small_local = _pack_small([q for l in range(DEPTH) for q in small[l]])
    m_small = small_local.shape[0]
    small_red = _sum_devices(_all_gather_devices(small_local), m_small)
    sg = _unpack_small(small_red)
    g_conv_a_w = lax.dynamic_slice_in_dim(sg["conv_a_w"], chip * (CONV_W // 4), CONV_W // 4, axis=2)
    g_conv_f_w = lax.dynamic_slice_in_dim(sg["conv_f_w"], chip * FF_Q, FF_Q, axis=2)

    loss = 0.5 / D_MODEL * lax.psum(loss_sum, ("x", "y", "c"))

    grads = dict(w_in=g_w_in, conv_a_w=g_conv_a_w, conv_a_b=sg["conv_a_b"], ln_a_g=sg["ln_a_g"], ln_a_b=sg["ln_a_b"],
                 ln_v_g=sg["ln_v_g"], ln_v_b=sg["ln_v_b"], w_s=sg["w_s"], b_s=sg["b_s"], w_mk=g_w_mk, w_mv=g_w_mv,
                 w_out=g_w_out, ln1_g=sg["ln1_g"], ln1_b=sg["ln1_b"], w_up=g_w_up, conv_f_w=g_conv_f_w,
                 conv_f_b=sg["conv_f_b"], w_down=g_w_down, ln2_g=sg["ln2_g"], ln2_b=sg["ln2_b"])
    weights = dict(w_in=w_in, conv_a_w=conv_a_w, conv_a_b=conv_a_b, ln_a_g=ln_a_g, ln_a_b=ln_a_b, ln_v_g=ln_v_g,
                   ln_v_b=ln_v_b, w_s=w_s, b_s=b_s, w_mk=w_mk, w_mv=w_mv, w_out=w_out, ln1_g=ln1_g, ln1_b=ln1_b,
                   w_up=w_up, conv_f_w=conv_f_w, conv_f_b=conv_f_b, w_down=w_down, ln2_g=ln2_g, ln2_b=ln2_b)
    mom_m = dict(w_in=m_w_in, conv_a_w=m_conv_a_w, conv_a_b=m_conv_a_b, ln_a_g=m_ln_a_g, ln_a_b=m_ln_a_b, ln_v_g=m_ln_v_g,
                 ln_v_b=m_ln_v_b, w_s=m_w_s, b_s=m_b_s, w_mk=m_w_mk, w_mv=m_w_mv, w_out=m_w_out, ln1_g=m_ln1_g,
                 ln1_b=m_ln1_b, w_up=m_w_up, conv_f_w=m_conv_f_w, conv_f_b=m_conv_f_b, w_down=m_w_down, ln2_g=m_ln2_g,
                 ln2_b=m_ln2_b)
    mom_v = dict(w_in=v_w_in, conv_a_w=v_conv_a_w, conv_a_b=v_conv_a_b, ln_a_g=v_ln_a_g, ln_a_b=v_ln_a_b, ln_v_g=v_ln_v_g,
                 ln_v_b=v_ln_v_b, w_s=v_w_s, b_s=v_b_s, w_mk=v_w_mk, w_mv=v_w_mv, w_out=v_w_out, ln1_g=v_ln1_g,
                 ln1_b=v_ln1_b, w_up=v_w_up, conv_f_w=v_conv_f_w, conv_f_b=v_conv_f_b, w_down=v_w_down, ln2_g=v_ln2_g,
                 ln2_b=v_ln2_b)
    names = list(weights)
    big_names = ["w_in", "w_mk", "w_mv", "w_out", "w_up", "w_down"]
    delta, new_m, new_v = {}, {}, {}
    for n in big_names:
        shp = weights[n].shape
        as2d = lambda a: a.reshape(-1, shp[-1])
        d, nm, nv = _adamw(as2d(weights[n]), as2d(grads[n]), as2d(mom_m[n]), as2d(mom_v[n]), "adamw_" + n)
        delta[n], new_m[n], new_v[n] = d.reshape(shp), nm.reshape(shp), nv.reshape(shp)
    small_names = [n for n in names if n not in big_names]
    sizes = [weights[n].size for n in small_names]
    pad = (-sum(sizes)) % (8 * LANE)

    def pack(dct, fill):
        return _pack_small([dct[n] for n in small_names] + [jnp.full((pad,), fill, F32)])

    d, nm, nv = _adamw(pack(weights, 0.0), pack(grads, 0.0), pack(mom_m, 0.0), pack(mom_v, 1.0), "adamw_small")
    o = 0
    for n, sz in zip(small_names, sizes):
        shp = weights[n].shape
        delta[n] = d.reshape(-1)[o:o + sz].reshape(shp)
        new_m[n] = nm.reshape(-1)[o:o + sz].reshape(shp)
        new_v[n] = nv.reshape(-1)[o:o + sz].reshape(shp)
        o += sz

    return (loss, grad_x, *[grads[n] for n in names], *[delta[n] for n in names],
            *[new_m[n] for n in names], *[new_v[n] for n in names])
```

```python
import jax
import jax.numpy as jnp
from jax import lax
from jax.experimental import pallas as pl
from jax.experimental.pallas import tpu as pltpu

F32 = jnp.float32
BF16 = jnp.bfloat16

D_MODEL = 1024
DEPTH = 2
CONV_W = 384
GMLP_W = 384
XATTN_W = 256
XATTN_HEADS = 4
HEAD_DIM = 64
IN_W = 1792
CONV_K = 31
CHUNK = 128
N_MEM = 256
D_FF = 2752
FFN_CONV_K = 3
ALPHA = (2.0 * DEPTH) ** 0.25
LN_EPS = 1e-5
ATT_SCALE = 1.0 / 8.0
ADAM_LR, ADAM_B1, ADAM_B2, ADAM_EPS, ADAM_WD, ADAM_STEP = 0.001, 0.9, 0.999, 1e-08, 0.01, 10

N_CHIPS = 4
FF_Q = D_FF // N_CHIPS
FF_QP = 704
FF_H = 2 * FF_QP
FF_P = 4 * FF_QP
LANE = 128
CONV_HALO = 32
FFN_HALO = 8
BF16_ROWS = 16
VMEM_LIMIT = 60 * 1024 * 1024

MESH = pl.DeviceIdType.MESH
ANY = pl.BlockSpec(memory_space=pl.ANY)


def _cparams(sem=None, vmem=VMEM_LIMIT):
    kw = {"vmem_limit_bytes": vmem}
    if sem is not None:
        kw["dimension_semantics"] = sem
    return pltpu.CompilerParams(**kw)


def _row_tile(rows, row_bytes, limit=2 << 20):
    if rows * row_bytes <= limit:
        return rows
    best = None
    for cand in range(BF16_ROWS, rows, BF16_ROWS):
        if rows % cand == 0 and cand * row_bytes <= limit:
            best = cand
    assert best is not None, (rows, row_bytes)
    return best


def _const_spec(shape):
    nd = len(shape)
    return pl.BlockSpec(shape, lambda *_: (0,) * nd)


def _layer_spec(shape, *lead, resident=False):
    nd = len(shape)
    kw = {"pipeline_mode": pl.Buffered(1)} if resident else {}
    return pl.BlockSpec((None,) * len(lead) + tuple(shape), lambda *_: tuple(lead) + (0,) * nd, **kw)


def _sigmoid(x):
    return jax.nn.sigmoid(x)


def _gelu(x):
    return jax.nn.gelu(x)


def _gelu_grad(x):
    c = 0.7978845608028654
    a = 0.044715
    t = jnp.tanh(c * (x + a * x * x * x))
    return 0.5 * (1.0 + t) + 0.5 * x * (1.0 - t * t) * c * (1.0 + 3.0 * a * x * x)


def _ln_fwd(z):
    mu = jnp.mean(z, axis=-1, keepdims=True)
    zc = z - mu
    var = jnp.mean(zc * zc, axis=-1, keepdims=True)
    rstd = lax.rsqrt(var + LN_EPS)
    return zc * rstd, rstd


def _ln_bwd(dxh, xh, rstd):
    m1 = jnp.mean(dxh, axis=-1, keepdims=True)
    m2 = jnp.mean(dxh * xh, axis=-1, keepdims=True)
    return rstd * (dxh - m1 - xh * m2)


def _colsum(a):
    return jnp.sum(a, axis=0, keepdims=True)


def _dot(a, b):
    return jnp.dot(a, b, preferred_element_type=F32)


def _dot_tn(a, b):
    return lax.dot_general(a, b, (((0,), (0,)), ((), ())), preferred_element_type=F32)


def _dot_nt(a, b):
    return lax.dot_general(a, b, (((1,), (1,)), ((), ())), preferred_element_type=F32)


def _softmax_heads(sc):
    ps = []
    for hd in range(XATTN_HEADS):
        s = sc[:, hd * N_MEM:(hd + 1) * N_MEM]
        e = jnp.exp(s - jnp.max(s, axis=-1, keepdims=True))
        ps.append(e / jnp.sum(e, axis=-1, keepdims=True))
    return jnp.concatenate(ps, axis=1)


def _lane_lo(shape):
    return (lax.broadcasted_iota(jnp.int32, shape, len(shape) - 1) % LANE) < HEAD_DIM


def _spatial_mix(vnb, wst_ref, bst_ref, mix_ref, t):
    lo = _lane_lo((CHUNK, LANE))
    for n in range(t // CHUNK):
        rows = slice(n * CHUNK, (n + 1) * CHUNK)
        for j in range(GMLP_W // LANE):
            cols = slice(j * LANE, (j + 1) * LANE)
            r = _dot(wst_ref[j], vnb[rows, cols])
            mix_ref[rows, cols] = jnp.where(lo, r[:CHUNK], r[CHUNK:]) + bst_ref[:, cols]


def _stacked_out(prev, shapes, specs, n_in):
    extra_specs = [] if prev is None else [ANY] * len(shapes)
    aliases = {} if prev is None else {n_in + k: k for k in range(len(shapes))}
    return extra_specs, aliases


def _kv_fwd(mem, w_mk, w_mv, l):
    def body(mem_ref, wk_ref, wv_ref, kt_ref, k_ref, v_ref, vt_ref):
        mb = mem_ref[...].astype(BF16)
        k = _dot(mb, wk_ref[...])
        v = _dot(mb, wv_ref[...])
        col = lax.broadcasted_iota(jnp.int32, (N_MEM, XATTN_W), 1) // HEAD_DIM
        ks = [jnp.where(col == hd, k, 0.0) for hd in range(XATTN_HEADS)]
        vs = [jnp.where(col == hd, v, 0.0) for hd in range(XATTN_HEADS)]
        k_ref[...] = jnp.concatenate(ks, axis=0).astype(BF16)
        v_ref[...] = jnp.concatenate(vs, axis=0).astype(BF16)
        kt_ref[...] = jnp.concatenate([x.T for x in ks], axis=1).astype(BF16)
        vt_ref[...] = jnp.concatenate([x.T for x in vs], axis=1).astype(BF16)

    wide = jax.ShapeDtypeStruct((XATTN_W, XATTN_HEADS * N_MEM), BF16)
    tall = jax.ShapeDtypeStruct((XATTN_HEADS * N_MEM, XATTN_W), BF16)
    wspec = _layer_spec((D_MODEL, XATTN_W), l)
    return pl.pallas_call(body, name="kv_fwd", grid=(1,),
                          in_specs=[_const_spec((N_MEM, D_MODEL)), wspec, wspec],
                          out_specs=[_const_spec(wide.shape), _const_spec(tall.shape), _const_spec(tall.shape),
                                     _const_spec(wide.shape)],
                          out_shape=(wide, tall, tall, wide), compiler_params=_cparams(("arbitrary",)))(mem, w_mk, w_mv)


def _kv_bwd(mem, dkt_all, dv_all, l, prev):
    def body(mem_ref, dkt_ref, dv_ref, *rest):
        gk_ref, gv_ref = rest[-2:]
        col = lax.broadcasted_iota(jnp.int32, (N_MEM, XATTN_W), 1) // HEAD_DIM
        dk = jnp.zeros((N_MEM, XATTN_W), F32)
        dv = jnp.zeros((N_MEM, XATTN_W), F32)
        for hd in range(XATTN_HEADS):
            dk = dk + jnp.where(col == hd, dkt_ref[:, hd * N_MEM:(hd + 1) * N_MEM].T, 0.0)
            dv = dv + jnp.where(col == hd, dv_ref[hd * N_MEM:(hd + 1) * N_MEM, :], 0.0)
        mb = mem_ref[...].astype(BF16)
        gk_ref[...] = _dot_tn(mb, dk.astype(BF16))
        gv_ref[...] = _dot_tn(mb, dv.astype(BF16))

    out = jax.ShapeDtypeStruct((DEPTH, D_MODEL, XATTN_W), F32)
    ospec = _layer_spec((D_MODEL, XATTN_W), l)
    extra, aliases = _stacked_out(prev, [out, out], None, 3)
    return pl.pallas_call(body, name="kv_bwd", grid=(1,),
                          in_specs=[_const_spec((N_MEM, D_MODEL)), _const_spec(dkt_all.shape), _const_spec(dv_all.shape)] + extra,
                          out_specs=[ospec, ospec], out_shape=(out, out), input_output_aliases=aliases,
                          compiler_params=_cparams(("arbitrary",)))(mem, dkt_all, dv_all, *(prev or ()))


PA_CONV_B, PA_LNA_G, PA_LNA_B, PA_LNV_G, PA_LNV_B = 0, 1, 2, 3, 4
PD_GIN, PD_BIN, PD_G1, PD_B1, PD_G2, PD_B2 = 0, 1, 2, 3, 4, 5


def _row(ref, r):
    return ref[r:r + 1, :]


def _mixer_fwd(xin, pd, win_t, conv_w, pa, wst, bst, kt_all, v_all, w_out, l, t):
    s = xin.shape[0]
    nt = s // t

    def body(x_ref, pd_ref, wint_ref, cw_ref, pa_ref, wst_ref, bst_ref, kt_ref, v_ref, wout_ref,
             xh_ref, rstd_ref, h_ref, ac_ref, cat_ref, cbuf, mixbuf):
        i = pl.program_id(0)
        x = x_ref[...] * _row(pd_ref, PD_GIN) + _row(pd_ref, PD_BIN)
        h = _dot_nt(x.astype(BF16), wint_ref[...])
        h_ref[...] = h
        a1, a2 = h[:, 0:CONV_W], h[:, CONV_W:2 * CONV_W]
        hu, hv = h[:, 2 * CONV_W:2 * CONV_W + GMLP_W], h[:, 2 * CONV_W + GMLP_W:2 * CONV_W + 2 * GMLP_W]
        q = h[:, IN_W - XATTN_W:]

        @pl.when(i == 0)
        def _():
            cbuf[0:CONV_HALO, :] = jnp.zeros((CONV_HALO, CONV_W), F32)

        cbuf[CONV_HALO:CONV_HALO + t, :] = a1 * _sigmoid(a2)
        ac = jnp.zeros((t, CONV_W), F32) + _row(pa_ref, PA_CONV_B)
        for k in range(CONV_K):
            off = CONV_HALO - (CONV_K - 1) + k
            ac = ac + cbuf[off:off + t, :] * cw_ref[k:k + 1, :]
        ac_ref[...] = ac
        cbuf[0:CONV_HALO, :] = cbuf[t:t + CONV_HALO, :]
        xh_a, _ = _ln_fwd(ac)
        an = xh_a * _row(pa_ref, PA_LNA_G) + _row(pa_ref, PA_LNA_B)
        a = an * _sigmoid(an)

        u = _gelu(hu)
        xh_v, _ = _ln_fwd(_gelu(hv))
        vn = xh_v * _row(pa_ref, PA_LNV_G) + _row(pa_ref, PA_LNV_B)
        _spatial_mix(vn.astype(BF16), wst_ref, bst_ref, mixbuf, t)
        g = u * mixbuf[...]

        p = _softmax_heads(_dot(q.astype(BF16), kt_ref[...]) * ATT_SCALE)
        o = _dot(p.astype(BF16), v_ref[...])

        cat = jnp.concatenate([a, g, o], axis=1).astype(BF16)
        cat_ref[...] = cat
        z = ALPHA * x + _dot(cat, wout_ref[...])
        xh, rstd = _ln_fwd(z)
        xh_ref[...] = xh
        rstd_ref[...] = rstd

    tok = lambda w: pl.BlockSpec((t, w), lambda i: (i, 0))
    return pl.pallas_call(
        body, name="mixer_fwd", grid=(nt,),
        in_specs=[tok(D_MODEL), _layer_spec((8, D_MODEL), l), _layer_spec((IN_W, D_MODEL), l, resident=True),
                  _layer_spec((CONV_HALO, CONV_W), l), _layer_spec((8, CONV_W), l),
                  _layer_spec((3, 2 * CHUNK, CHUNK), l), _layer_spec((CHUNK, GMLP_W), l),
                  _const_spec((XATTN_W, XATTN_HEADS * N_MEM)), _const_spec((XATTN_HEADS * N_MEM, XATTN_W)),
                  _layer_spec((D_MODEL, D_MODEL), l, resident=True)],
        out_specs=[tok(D_MODEL), tok(1), tok(IN_W), tok(CONV_W), tok(D_MODEL)],
        out_shape=[jax.ShapeDtypeStruct((s, D_MODEL), F32), jax.ShapeDtypeStruct((s, 1), F32),
                   jax.ShapeDtypeStruct((s, IN_W), F32), jax.ShapeDtypeStruct((s, CONV_W), F32),
                   jax.ShapeDtypeStruct((s, D_MODEL), BF16)],
        scratch_shapes=[pltpu.VMEM((t + CONV_HALO, CONV_W), F32), pltpu.VMEM((t, GMLP_W), F32)],
        compiler_params=_cparams(("arbitrary",)),
    )(xin, pd, win_t, conv_w, pa, wst, bst, kt_all, v_all, w_out)


VD_LN_G, VD_LN_B, VD_LOSS = 0, 1, 2
VA_CONV_B, VA_LNA_G, VA_LNA_B, VA_LNV_G, VA_LNV_B = 0, 1, 2, 3, 4


def _mixer_bwd_d(gz, xh1, rstd1, h, ac, pd, conv_w, pa, wst, wstt, bst, kt_all, k_all, vt_all, w_out, win_t, l, t):
    s = gz.shape[0]
    nt = s // t

    def body(gz_ref, xh_ref, rstd_ref, h_ref, ac_ref, pd_ref, cw_ref, pa_ref, wst_ref, wstt_ref, bst_ref,
             kt_ref, k_ref, vt_ref, wout_ref, wint_ref,
             dx_ref, dh_ref, dmix_ref, vd_ref, va_ref, dcw_ref, dws_ref, dbs_ref, dkt_ref, dv_ref,
             ebuf, mixbuf, dvnbuf, dbsacc):
        i = pl.program_id(0)

        @pl.when(i == 0)
        def _():
            vd_ref[...] = jnp.zeros_like(vd_ref)
            va_ref[...] = jnp.zeros_like(va_ref)
            dcw_ref[...] = jnp.zeros_like(dcw_ref)
            dws_ref[...] = jnp.zeros_like(dws_ref)
            dbs_ref[...] = jnp.zeros_like(dbs_ref)
            dkt_ref[...] = jnp.zeros_like(dkt_ref)
            dv_ref[...] = jnp.zeros_like(dv_ref)
            dbsacc[...] = jnp.zeros_like(dbsacc)
            ebuf[t:t + CONV_HALO, :] = jnp.zeros((CONV_HALO, CONV_W), F32)

        gz_v = gz_ref[...]
        xh = xh_ref[...]
        vd_ref[VD_LN_G:VD_LN_G + 1, :] += _colsum(gz_v * xh)
        vd_ref[VD_LN_B:VD_LN_B + 1, :] += _colsum(gz_v)
        dz = _ln_bwd(gz_v * _row(pd_ref, PD_G1), xh, rstd_ref[...])
        dzb = dz.astype(BF16)
        dmix_ref[...] = dzb
        dcat = _dot_nt(dzb, wout_ref[...])
        d_a, d_g, d_o = dcat[:, 0:CONV_W], dcat[:, CONV_W:CONV_W + GMLP_W], dcat[:, CONV_W + GMLP_W:]

        h = h_ref[...]
        a1, a2 = h[:, 0:CONV_W], h[:, CONV_W:2 * CONV_W]
        hu, hv = h[:, 2 * CONV_W:2 * CONV_W + GMLP_W], h[:, 2 * CONV_W + GMLP_W:2 * CONV_W + 2 * GMLP_W]
        q = h[:, IN_W - XATTN_W:]

        xh_a, rstd_a = _ln_fwd(ac_ref[...])
        an = xh_a * _row(pa_ref, PA_LNA_G) + _row(pa_ref, PA_LNA_B)
        sig = _sigmoid(an)
        d_an = d_a * (sig * (1.0 + an * (1.0 - sig)))
        va_ref[VA_LNA_G:VA_LNA_G + 1, :] += _colsum(d_an * xh_a)
        va_ref[VA_LNA_B:VA_LNA_B + 1, :] += _colsum(d_an)
        dac = _ln_bwd(d_an * _row(pa_ref, PA_LNA_G), xh_a, rstd_a)
        va_ref[VA_CONV_B:VA_CONV_B + 1, :] += _colsum(dac)
        ebuf[0:t, :] = dac
        sg = _sigmoid(a2)
        glu = a1 * sg
        dglu = jnp.zeros((t, CONV_W), F32)
        for k in range(CONV_K):
            off = CONV_K - 1 - k
            ek = ebuf[off:off + t, :]
            dglu = dglu + ek * cw_ref[k:k + 1, :]
            dcw_ref[k:k + 1, :] += _colsum(ek * glu)
        ebuf[t:t + CONV_HALO, :] = ebuf[0:CONV_HALO, :]
        da1 = dglu * sg
        da2 = dglu * a1 * sg * (1.0 - sg)

        u = _gelu(hu)
        xh_v, rstd_v = _ln_fwd(_gelu(hv))
        vn = xh_v * _row(pa_ref, PA_LNV_G) + _row(pa_ref, PA_LNV_B)
        vnb = vn.astype(BF16)
        _spatial_mix(vnb, wst_ref, bst_ref, mixbuf, t)
        dhu = d_g * mixbuf[...] * _gelu_grad(hu)
        dm = d_g * u
        dmb = dm.astype(BF16)
        lo = _lane_lo((CHUNK, LANE))
        for n in range(t // CHUNK):
            rows = slice(n * CHUNK, (n + 1) * CHUNK)
            dbsacc[...] += dm[rows, :]
            for j in range(GMLP_W // LANE):
                cols = slice(j * LANE, (j + 1) * LANE)
                dm_blk = dmb[rows, cols]
                r = _dot(wstt_ref[j], dm_blk)
                dvnbuf[rows, cols] = jnp.where(lo, r[:CHUNK], r[CHUNK:])
                zero = jnp.zeros_like(dm_blk)
                st = jnp.concatenate([jnp.where(lo, dm_blk, zero), jnp.where(lo, zero, dm_blk)], axis=0)
                dws_ref[j] += _dot_nt(st, vnb[rows, cols])
        dvn = dvnbuf[...]
        va_ref[VA_LNV_G:VA_LNV_G + 1, :] += _colsum(dvn * xh_v)
        va_ref[VA_LNV_B:VA_LNV_B + 1, :] += _colsum(dvn)
        dhv = _ln_bwd(dvn * _row(pa_ref, PA_LNV_G), xh_v, rstd_v) * _gelu_grad(hv)

        qb = q.astype(BF16)
        p = _softmax_heads(_dot(qb, kt_ref[...]) * ATT_SCALE)
        dob = d_o.astype(BF16)
        dp = _dot(dob, vt_ref[...])
        dss = []
        for hd in range(XATTN_HEADS):
            cs = slice(hd * N_MEM, (hd + 1) * N_MEM)
            ph, dph = p[:, cs], dp[:, cs]
            dss.append(ph * (dph - jnp.sum(ph * dph, axis=-1, keepdims=True)) * ATT_SCALE)
        dsb = jnp.concatenate(dss, axis=1).astype(BF16)
        dq = _dot(dsb, k_ref[...])
        dkt_ref[...] += _dot_tn(qb, dsb)
        dv_ref[...] += _dot_tn(p.astype(BF16), dob)

        dhb = jnp.concatenate([da1, da2, dhu, dhv, dq], axis=1).astype(BF16)
        dh_ref[...] = dhb
        dx_ref[...] = ALPHA * dz + _dot(dhb, wint_ref[...])

        @pl.when(i == nt - 1)
        def _():
            acc = dbsacc[...]
            head = lax.broadcasted_iota(jnp.int32, (CHUNK, GMLP_W), 1) // HEAD_DIM
            lane = lax.broadcasted_iota(jnp.int32, (CHUNK, LANE), 1)
            out = jnp.zeros((CHUNK, LANE), F32)
            for hd in range(GMLP_W // HEAD_DIM):
                sh = jnp.sum(jnp.where(head == hd, acc, 0.0), axis=1, keepdims=True)
                out = out + jnp.where(lane == hd, sh, 0.0)
            dbs_ref[...] = out

    rev = lambda w: pl.BlockSpec((t, w), lambda i: (nt - 1 - i, 0))
    out_shape = [
        jax.ShapeDtypeStruct((s, D_MODEL), F32), jax.ShapeDtypeStruct((s, IN_W), BF16),
        jax.ShapeDtypeStruct((s, D_MODEL), BF16),
        jax.ShapeDtypeStruct((8, D_MODEL), F32), jax.ShapeDtypeStruct((8, CONV_W), F32),
        jax.ShapeDtypeStruct((CONV_HALO, CONV_W), F32), jax.ShapeDtypeStruct((3, 2 * CHUNK, CHUNK), F32),
        jax.ShapeDtypeStruct((CHUNK, LANE), F32),
        jax.ShapeDtypeStruct((XATTN_W, XATTN_HEADS * N_MEM), F32), jax.ShapeDtypeStruct((XATTN_HEADS * N_MEM, XATTN_W), F32),
    ]
    out_specs = [rev(D_MODEL), rev(IN_W), rev(D_MODEL)] + [_const_spec(o.shape) for o in out_shape[3:]]
    return pl.pallas_call(
        body, name="mixer_bwd_d", grid=(nt,),
        in_specs=[rev(D_MODEL), rev(D_MODEL), rev(1), rev(IN_W), rev(CONV_W),
                  _layer_spec((8, D_MODEL), l), _layer_spec((CONV_HALO, CONV_W), l), _layer_spec((8, CONV_W), l),
                  _layer_spec((3, 2 * CHUNK, CHUNK), l), _layer_spec((3, 2 * CHUNK, CHUNK), l),
                  _layer_spec((CHUNK, GMLP_W), l),
                  _const_spec((XATTN_W, XATTN_HEADS * N_MEM)), _const_spec((XATTN_HEADS * N_MEM, XATTN_W)),
                  _const_spec((XATTN_W, XATTN_HEADS * N_MEM)),
                  _layer_spec((D_MODEL, D_MODEL), l, resident=True), _layer_spec((IN_W, D_MODEL), l, resident=True)],
        out_specs=out_specs, out_shape=out_shape,
        scratch_shapes=[pltpu.VMEM((t + CONV_HALO, CONV_W), F32), pltpu.VMEM((t, GMLP_W), F32),
                        pltpu.VMEM((t, GMLP_W), F32), pltpu.VMEM((CHUNK, GMLP_W), F32)],
        compiler_params=_cparams(("arbitrary",)),
    )(gz, xh1, rstd1, h, ac, pd, conv_w, pa, wst, wstt, bst, kt_all, k_all, vt_all, w_out, win_t)


def _mixer_bwd_w(xin, pd, dh, cat, dmix, l, t, prev):
    s = xin.shape[0]
    nt = s // t

    def body(x_ref, pd_ref, dh_ref, cat_ref, dmix_ref, *rest):
        dwin_ref, dwout_ref = rest[-2:]

        @pl.when(pl.program_id(0) == 0)
        def _():
            dwin_ref[...] = jnp.zeros_like(dwin_ref)
            dwout_ref[...] = jnp.zeros_like(dwout_ref)

        xb = (x_ref[...] * _row(pd_ref, PD_GIN) + _row(pd_ref, PD_BIN)).astype(BF16)
        dwin_ref[...] += _dot_tn(dh_ref[...], xb)
        dwout_ref[...] += _dot_tn(cat_ref[...], dmix_ref[...])

    tok = lambda w: pl.BlockSpec((t, w), lambda i: (i, 0))
    shapes = [jax.ShapeDtypeStruct((DEPTH, IN_W, D_MODEL), F32), jax.ShapeDtypeStruct((DEPTH, D_MODEL, D_MODEL), F32)]
    extra, aliases = _stacked_out(prev, shapes, None, 5)
    return pl.pallas_call(
        body, name="mixer_bwd_w", grid=(nt,),
        in_specs=[tok(D_MODEL), _layer_spec((8, D_MODEL), l), tok(IN_W), tok(D_MODEL), tok(D_MODEL)] + extra,
        out_specs=[_layer_spec((IN_W, D_MODEL), l), _layer_spec((D_MODEL, D_MODEL), l)],
        out_shape=shapes, input_output_aliases=aliases,
        compiler_params=_cparams(("arbitrary",)),
    )(xin, pd, dh, cat, dmix, *(prev or ()))


PF_W0, PF_B = 0, 3


def _ffn_fwd(xh1, pd, wup_t, pf, w_d, l, t):
    s = xh1.shape[0]
    nt = s // t

    def body(xh_ref, pd_ref, wg_ref, wv_ref, pf_ref, wd_ref, xh2_ref, rstd_ref, upg_ref, upv_ref, fbuf):
        i = pl.program_id(0)

        @pl.when(i == 0)
        def _():
            fbuf[0:FFN_HALO, :] = jnp.zeros((FFN_HALO, FF_P), F32)

        x1 = xh_ref[...] * _row(pd_ref, PD_G1) + _row(pd_ref, PD_B1)
        xb = x1.astype(BF16)
        y = jnp.zeros((t, D_MODEL), F32)
        for hf in range(2):
            cs = slice(hf * FF_H, (hf + 1) * FF_H)
            ug = _dot_nt(xb, wg_ref[cs, :])
            uv = _dot_nt(xb, wv_ref[cs, :])
            upg_ref[:, cs] = ug.astype(BF16)
            upv_ref[:, cs] = uv.astype(BF16)
            fbuf[FFN_HALO:FFN_HALO + t, cs] = ug
            gate = jnp.zeros((t, FF_H), F32) + pf_ref[PF_B:PF_B + 1, cs]
            for k in range(FFN_CONV_K):
                off = FFN_HALO - (FFN_CONV_K - 1) + k
                gate = gate + fbuf[off:off + t, cs] * pf_ref[PF_W0 + k:PF_W0 + k + 1, cs]
            fbuf[0:FFN_HALO, cs] = fbuf[t:t + FFN_HALO, cs]
            hm = gate * _sigmoid(gate) * uv
            y = y + _dot(hm.astype(BF16), wd_ref[cs, :])
        xh2, rstd = _ln_fwd(ALPHA * x1 + y)
        xh2_ref[...] = xh2
        rstd_ref[...] = rstd

    tok = lambda w: pl.BlockSpec((t, w), lambda i: (i, 0))
    return pl.pallas_call(
        body, name="ffn_fwd", grid=(nt,),
        in_specs=[tok(D_MODEL), _layer_spec((8, D_MODEL), l),
                  _layer_spec((FF_P, D_MODEL), l, 0, resident=True), _layer_spec((FF_P, D_MODEL), l, 1, resident=True),
                  _layer_spec((8, FF_P), l), _layer_spec((FF_P, D_MODEL), l, resident=True)],
        out_specs=[tok(D_MODEL), tok(1), tok(FF_P), tok(FF_P)],
        out_shape=[jax.ShapeDtypeStruct((s, D_MODEL), F32), jax.ShapeDtypeStruct((s, 1), F32),
                   jax.ShapeDtypeStruct((s, FF_P), BF16), jax.ShapeDtypeStruct((s, FF_P), BF16)],
        scratch_shapes=[pltpu.VMEM((t + FFN_HALO, FF_P), F32)],
        compiler_params=_cparams(("arbitrary",)),
    )(xh1, pd, wup_t, wup_t, pf, w_d)


VF_W0, VF_B = 0, 3


def _ffn_bwd_d(gz_or_target, xh2, rstd2, upg, upv, pd, pf, w_d, wup_t, l, t, last):
    s = xh2.shape[0]
    nt = s // t
    hb = t // BF16_ROWS

    def body(gz_ref, xh2_ref, rstd_ref, upg_ref, halo_ref, upv_ref, pd_ref, pf_ref, wd_ref, wg_ref, wv_ref,
             dx_ref, dy_ref, dug_ref, duv_ref, hm_ref, vd_ref, vf_ref, gbuf, ebuf):
        i = pl.program_id(0)
        first_tile = i == nt - 1

        @pl.when(i == 0)
        def _():
            vd_ref[...] = jnp.zeros_like(vd_ref)
            vf_ref[...] = jnp.zeros_like(vf_ref)
            ebuf[t:t + FFN_HALO, :] = jnp.zeros((FFN_HALO, FF_P), F32)

        xh2_v = xh2_ref[...]
        if last:
            diff = xh2_v * _row(pd_ref, PD_G2) + _row(pd_ref, PD_B2) - gz_ref[...]
            vd_ref[VD_LOSS:VD_LOSS + 1, :] += _colsum(diff * diff)
            gz_v = diff * (1.0 / D_MODEL)
        else:
            gz_v = gz_ref[...]
        vd_ref[VD_LN_G:VD_LN_G + 1, :] += _colsum(gz_v * xh2_v)
        vd_ref[VD_LN_B:VD_LN_B + 1, :] += _colsum(gz_v)
        dz = _ln_bwd(gz_v * _row(pd_ref, PD_G2), xh2_v, rstd_ref[...])
        dyb = dz.astype(BF16)
        dy_ref[...] = dyb
        dx = ALPHA * dz
        for hf in range(2):
            cs = slice(hf * FF_H, (hf + 1) * FF_H)
            ug = upg_ref[:, cs].astype(F32)
            uv = upv_ref[:, cs].astype(F32)
            halo = halo_ref[:, cs].astype(F32)
            gbuf[0:BF16_ROWS, :] = jnp.where(first_tile, jnp.zeros_like(halo), halo)
            gbuf[BF16_ROWS:BF16_ROWS + t, :] = ug
            gate = jnp.zeros((t, FF_H), F32) + pf_ref[PF_B:PF_B + 1, cs]
            for k in range(FFN_CONV_K):
                off = BF16_ROWS - (FFN_CONV_K - 1) + k
                gate = gate + gbuf[off:off + t, :] * pf_ref[PF_W0 + k:PF_W0 + k + 1, cs]
            sig = _sigmoid(gate)
            sl = gate * sig
            hm_ref[:, cs] = (sl * uv).astype(BF16)
            dhm = _dot_nt(dyb, wd_ref[cs, :])
            duv = dhm * sl
            dgate = dhm * uv * (sig * (1.0 + gate * (1.0 - sig)))
            vf_ref[VF_B:VF_B + 1, cs] += _colsum(dgate)
            ebuf[0:t, cs] = dgate
            dug = jnp.zeros((t, FF_H), F32)
            for k in range(FFN_CONV_K):
                off = FFN_CONV_K - 1 - k
                ek = ebuf[off:off + t, cs]
                dug = dug + ek * pf_ref[PF_W0 + k:PF_W0 + k + 1, cs]
                vf_ref[VF_W0 + k:VF_W0 + k + 1, cs] += _colsum(ek * ug)
            ebuf[t:t + FFN_HALO, cs] = ebuf[0:FFN_HALO, cs]
            dugb = dug.astype(BF16)
            duvb = duv.astype(BF16)
            dug_ref[:, cs] = dugb
            duv_ref[:, cs] = duvb
            dx = dx + _dot(dugb, wg_ref[cs, :]) + _dot(duvb, wv_ref[cs, :])
        dx_ref[...] = dx

        if last:
            @pl.when(i == nt - 1)
            def _():
                tot = jnp.sum(vd_ref[VD_LOSS:VD_LOSS + 1, :], axis=1, keepdims=True)
                vd_ref[VD_LOSS:VD_LOSS + 1, :] = jnp.zeros((1, D_MODEL), F32) + tot

    rev = lambda w: pl.BlockSpec((t, w), lambda i: (nt - 1 - i, 0))
    halo_spec = pl.BlockSpec((BF16_ROWS, FF_P), lambda i: (jnp.maximum((nt - 1 - i) * hb - 1, 0), 0))
    out_shape = [jax.ShapeDtypeStruct((s, D_MODEL), F32), jax.ShapeDtypeStruct((s, D_MODEL), BF16),
                 jax.ShapeDtypeStruct((s, FF_P), BF16), jax.ShapeDtypeStruct((s, FF_P), BF16),
                 jax.ShapeDtypeStruct((s, FF_P), BF16),
                 jax.ShapeDtypeStruct((8, D_MODEL), F32), jax.ShapeDtypeStruct((8, FF_P), F32)]
    return pl.pallas_call(
        body, name="ffn_bwd_d_last" if last else "ffn_bwd_d", grid=(nt,),
        in_specs=[rev(D_MODEL), rev(D_MODEL), rev(1), rev(FF_P), halo_spec, rev(FF_P),
                  _layer_spec((8, D_MODEL), l), _layer_spec((8, FF_P), l),
                  _layer_spec((FF_P, D_MODEL), l, resident=True),
                  _layer_spec((FF_P, D_MODEL), l, 0, resident=True), _layer_spec((FF_P, D_MODEL), l, 1, resident=True)],
        out_specs=[rev(D_MODEL), rev(D_MODEL), rev(FF_P), rev(FF_P), rev(FF_P),
                   _const_spec((8, D_MODEL)), _const_spec((8, FF_P))],
        out_shape=out_shape,
        scratch_shapes=[pltpu.VMEM((t + BF16_ROWS, FF_H), F32), pltpu.VMEM((t + FFN_HALO, FF_P), F32)],
        compiler_params=_cparams(("arbitrary",)),
    )(gz_or_target, xh2, rstd2, upg, upg, upv, pd, pf, w_d, wup_t, wup_t)


def _ffn_bwd_w(xh1, pd, dy, dug, duv, hm, l, t, prev):
    s = xh1.shape[0]
    nt = s // t

    def body(xh_ref, pd_ref, dy_ref, dug_ref, duv_ref, hm_ref, *rest):
        dwup_ref, dwd_ref = rest[-2:]

        @pl.when(pl.program_id(1) == 0)
        def _():
            dwup_ref[...] = jnp.zeros_like(dwup_ref)
            dwd_ref[...] = jnp.zeros_like(dwd_ref)

        xb = (xh_ref[...] * _row(pd_ref, PD_G1) + _row(pd_ref, PD_B1)).astype(BF16)
        dwup_ref[0] += _dot_tn(dug_ref[...], xb)
        dwup_ref[1] += _dot_tn(duv_ref[...], xb)
        dwd_ref[...] += _dot_tn(hm_ref[...], dy_ref[...])

    tok = lambda w: pl.BlockSpec((t, w), lambda c, i: (i, 0))
    half = pl.BlockSpec((t, FF_H), lambda c, i: (i, c))
    shapes = [jax.ShapeDtypeStruct((DEPTH, 2, FF_P, D_MODEL), F32), jax.ShapeDtypeStruct((DEPTH, FF_P, D_MODEL), F32)]
    extra, aliases = _stacked_out(prev, shapes, None, 6)
    return pl.pallas_call(
        body, name="ffn_bwd_w", grid=(2, nt),
        in_specs=[tok(D_MODEL), pl.BlockSpec((None, 8, D_MODEL), lambda c, i: (l, 0, 0)), tok(D_MODEL), half, half, half] + extra,
        out_specs=[pl.BlockSpec((None, 2, FF_H, D_MODEL), lambda c, i: (l, 0, c, 0)),
                   pl.BlockSpec((None, FF_H, D_MODEL), lambda c, i: (l, c, 0))],
        out_shape=shapes, input_output_aliases=aliases,
        compiler_params=_cparams(("arbitrary", "arbitrary")),
    )(xh1, pd, dy, dug, duv, hm, *(prev or ()))


def _adamw(w, g, m, v, name):
    rows, cols = w.shape
    tr = _row_tile(rows, cols * 4)

    def body(w_ref, g_ref, m_ref, v_ref, d_ref, nm_ref, nv_ref):
        gv = g_ref[...]
        nm = ADAM_B1 * m_ref[...] + (1.0 - ADAM_B1) * gv
        nv = ADAM_B2 * v_ref[...] + (1.0 - ADAM_B2) * (gv * gv)
        m_hat = nm / (1.0 - ADAM_B1 ** ADAM_STEP)
        v_hat = nv / (1.0 - ADAM_B2 ** ADAM_STEP)
        d_ref[...] = -ADAM_LR * (m_hat / (jnp.sqrt(v_hat) + ADAM_EPS) + ADAM_WD * w_ref[...])
        nm_ref[...] = nm
        nv_ref[...] = nv

    blk = pl.BlockSpec((tr, cols), lambda i: (i, 0))
    sh = jax.ShapeDtypeStruct((rows, cols), F32)
    return pl.pallas_call(body, name=name, grid=(rows // tr,), in_specs=[blk] * 4, out_specs=[blk] * 3,
                          out_shape=[sh, sh, sh], compiler_params=_cparams(("arbitrary",)))(w, g, m, v)


def _my_place():
    x, y, c = lax.axis_index("x"), lax.axis_index("y"), lax.axis_index("c")
    chips = [(1 - x, y), (x, 1 - y), (1 - x, 1 - y)]
    return x, y, c, chips


def _all_gather_chips(tensors, name):
    n = len(tensors)
    halves = [a.shape[1] // 2 for a in tensors]

    def body(*refs):
        x_refs, out_refs = refs[:n], refs[n:2 * n]
        send_sems, recv_sems, local_sems = refs[2 * n:]
        x, y, c, chips = _my_place()
        me, sibling, mej = (x, y, c), (x, y, 1 - c), 2 * x + y

        def rows(tn, px, py, pc):
            return out_refs[tn].at[:, 2 * px + py, pl.ds(pc * halves[tn], halves[tn]), :]

        def copy(tn, k, block, to, src=None):
            return pltpu.make_async_remote_copy(
                src_ref=rows(tn, *block) if src is None else src, dst_ref=rows(tn, *block),
                send_sem=send_sems.at[tn, k], recv_sem=recv_sems.at[tn, k], device_id=to, device_id_type=MESH)

        mine_src = [x_refs[tn].at[:, pl.ds(c * halves[tn], halves[tn]), :] for tn in range(n)]
        mine = [pltpu.make_async_copy(mine_src[tn], rows(tn, *me), local_sems.at[tn]) for tn in range(n)]
        first = []
        for j, chip in enumerate(chips):
            first += [copy(tn, 1 + j, me, (*chip, c), src=mine_src[tn]) for tn in range(n)]
        first += [copy(tn, 0, me, sibling, src=mine_src[tn]) for tn in range(n)]
        for cp in first + mine:
            cp.start()
        passed = []
        for j, chip in enumerate(chips):
            for tn in range(n):
                copy(tn, 1 + j, (*chip, c), me).wait_recv()
                fwd = copy(tn, 4 + j, (*chip, c), sibling)
                fwd.start()
                passed.append(fwd)
        for tn in range(n):
            copy(tn, 0, sibling, me).wait_recv()
            for j, chip in enumerate(chips):
                copy(tn, 4 + j, (*chip, 1 - c), me).wait_recv()
        for cp in first + passed:
            cp.wait_send()
        for cp in mine:
            cp.wait()

    return pl.pallas_call(
        body, name=name,
        out_shape=[jax.ShapeDtypeStruct((a.shape[0], N_CHIPS) + a.shape[1:], a.dtype) for a in tensors],
        in_specs=[ANY] * n, out_specs=[ANY] * n,
        scratch_shapes=[pltpu.SemaphoreType.DMA((n, 7)), pltpu.SemaphoreType.DMA((n, 7)), pltpu.SemaphoreType.DMA((n,))],
    )(*tensors)


def _sibling_swap_halves(gs):
    n = len(gs)

    def body(*refs):
        g_refs, out_refs = refs[:n], refs[n:2 * n]
        send_sems, recv_sems = refs[2 * n:]
        x, y, c, _ = _my_place()
        cps = [pltpu.make_async_remote_copy(
            src_ref=g_refs[tn].at[:, :, 1 - c], dst_ref=out_refs[tn], send_sem=send_sems.at[tn], recv_sem=recv_sems.at[tn],
            device_id=(x, y, 1 - c), device_id_type=MESH) for tn in range(n)]
        for cp in cps:
            cp.start()
        for cp in cps:
            cp.wait()

    return pl.pallas_call(
        body, name="rs_sibling",
        out_shape=[jax.ShapeDtypeStruct(g.shape[:2] + g.shape[3:], g.dtype) for g in gs],
        in_specs=[ANY] * n, out_specs=[ANY] * n,
        scratch_shapes=[pltpu.SemaphoreType.DMA((n,)), pltpu.SemaphoreType.DMA((n,))],
    )(*gs)


def _chip_scatter(parts):
    n = len(parts)

    def body(*refs):
        p_refs, out_refs = refs[:n], refs[n:2 * n]
        send_sems, recv_sems = refs[2 * n:]
        x, y, c, chips = _my_place()
        cps = []
        for k, (px, py) in enumerate(chips):
            for tn in range(n):
                cps.append(pltpu.make_async_remote_copy(
                    src_ref=p_refs[tn].at[:, 2 * px + py], dst_ref=out_refs[tn].at[:, k],
                    send_sem=send_sems.at[tn, k], recv_sem=recv_sems.at[tn, k],
                    device_id=(px, py, c), device_id_type=MESH))
        for cp in cps:
            cp.start()
        for cp in cps:
            cp.wait()

    return pl.pallas_call(
        body, name="rs_chips",
        out_shape=[jax.ShapeDtypeStruct((p.shape[0], 3) + p.shape[2:], p.dtype) for p in parts],
        in_specs=[ANY] * n, out_specs=[ANY] * n,
        scratch_shapes=[pltpu.SemaphoreType.DMA((n, 3)), pltpu.SemaphoreType.DMA((n, 3))],
    )(*parts)


def _sibling_gather(fs):
    n = len(fs)

    def body(*refs):
        out_refs = refs[n:2 * n]
        send_sems, recv_sems = refs[2 * n:]
        x, y, c, _ = _my_place()
        snd = [pltpu.make_async_remote_copy(
            src_ref=out_refs[tn].at[:, c], dst_ref=out_refs[tn].at[:, c], send_sem=send_sems.at[tn], recv_sem=recv_sems.at[tn],
            device_id=(x, y, 1 - c), device_id_type=MESH) for tn in range(n)]
        rcv = [pltpu.make_async_remote_copy(
            src_ref=out_refs[tn].at[:, c], dst_ref=out_refs[tn].at[:, 1 - c], send_sem=send_sems.at[tn], recv_sem=recv_sems.at[tn],
            device_id=(x, y, 1 - c), device_id_type=MESH) for tn in range(n)]
        for cp in snd:
            cp.start()
        for tn in range(n):
            rcv[tn].wait_recv()
            snd[tn].wait_send()

    return pl.pallas_call(
        body, name="rs_sibling_gather",
        out_shape=[jax.ShapeDtypeStruct(f.shape, f.dtype) for f in fs],
        in_specs=[ANY] * n, out_specs=[ANY] * n, input_output_aliases={tn: tn for tn in range(n)},
        scratch_shapes=[pltpu.SemaphoreType.DMA((n,)), pltpu.SemaphoreType.DMA((n,))],
    )(*fs)


def _all_gather_devices(xs):
    m_per, n = xs.shape

    def body(x_ref, out_ref, send_sems, recv_sems, local_sem):
        x, y, c, chips = _my_place()
        me, sibling = (x, y, c), (x, y, 1 - c)

        def rows(px, py, pc):
            return out_ref.at[pl.ds((4 * px + 2 * py + pc) * m_per, m_per), :]

        def copy(k, block, to, src=None):
            return pltpu.make_async_remote_copy(
                src_ref=rows(*block) if src is None else src, dst_ref=rows(*block),
                send_sem=send_sems.at[k], recv_sem=recv_sems.at[k], device_id=to, device_id_type=MESH)

        mine = pltpu.make_async_copy(x_ref, rows(*me), local_sem)
        mine.start()
        first = [copy(0, me, sibling, src=x_ref)]
        first += [copy(1 + j, me, (*chip, c), src=x_ref) for j, chip in enumerate(chips)]
        for cp in first:
            cp.start()
        passed = [copy(4 + j, (*chip, c), sibling) for j, chip in enumerate(chips)]
        for j, chip in enumerate(chips):
            copy(1 + j, (*chip, c), me).wait_recv()
            passed[j].start()
        copy(0, sibling, me).wait_recv()
        for j, chip in enumerate(chips):
            copy(4 + j, (*chip, 1 - c), me).wait_recv()
        for cp in first + passed:
            cp.wait_send()
        mine.wait()

    return pl.pallas_call(
        body, name="ag_small_grads", out_shape=jax.ShapeDtypeStruct((8 * m_per, n), xs.dtype),
        in_specs=[pl.BlockSpec(memory_space=pltpu.VMEM)], out_specs=pl.BlockSpec(memory_space=pltpu.VMEM),
        scratch_shapes=[pltpu.SemaphoreType.DMA((7,)), pltpu.SemaphoreType.DMA((7,)), pltpu.SemaphoreType.DMA],
        compiler_params=_cparams(),
    )(xs)


def _add_halves(gs, recvs, place):
    n = len(gs)

    def body(place_ref, *refs):
        g_refs, r_refs, o_refs = refs[:n], refs[n:2 * n], refs[2 * n:]
        for tn in range(n):
            o_refs[tn][...] = (g_refs[tn][...] + r_refs[tn][...]).astype(BF16)

    def gspec(g):
        return pl.BlockSpec((None, None, None) + g.shape[3:], lambda l, j, p: (l, j, p[1], 0, 0))

    def rspec(r):
        return pl.BlockSpec((None, None) + r.shape[2:], lambda l, j, p: (l, j, 0, 0))

    grid_spec = pltpu.PrefetchScalarGridSpec(
        num_scalar_prefetch=1, grid=(DEPTH, N_CHIPS),
        in_specs=[gspec(g) for g in gs] + [rspec(r) for r in recvs], out_specs=[rspec(r) for r in recvs])
    return pl.pallas_call(body, name="rs_add", grid_spec=grid_spec,
                          out_shape=[jax.ShapeDtypeStruct(r.shape, BF16) for r in recvs],
                          compiler_params=_cparams(("arbitrary", "arbitrary")))(place, *gs, *recvs)


def _sum_slots(parts, slots, place):
    n = len(parts)

    def body(place_ref, *refs):
        p_refs, s_refs, o_refs = refs[:n], refs[n:2 * n], refs[2 * n:]
        for tn in range(n):
            acc = p_refs[tn][...].astype(F32)
            for k in range(3):
                acc = acc + s_refs[tn][k].astype(F32)
            o_refs[tn][...] = acc

    def pspec(p):
        return pl.BlockSpec((None, None) + p.shape[2:], lambda l, pl_: (l, pl_[0], 0, 0))

    def sspec(sl):
        return pl.BlockSpec((None,) + sl.shape[1:], lambda l, pl_: (l, 0, 0, 0))

    def ospec(p):
        return pl.BlockSpec((None, None) + p.shape[2:], lambda l, pl_: (l, pl_[1], 0, 0))

    grid_spec = pltpu.PrefetchScalarGridSpec(
        num_scalar_prefetch=1, grid=(DEPTH,),
        in_specs=[pspec(p) for p in parts] + [sspec(sl) for sl in slots], out_specs=[ospec(p) for p in parts])
    return pl.pallas_call(body, name="rs_sum", grid_spec=grid_spec,
                          out_shape=[jax.ShapeDtypeStruct((p.shape[0], 2) + p.shape[2:], F32) for p in parts],
                          compiler_params=_cparams(("arbitrary",)))(place, *parts, *slots)


def _sum_devices(gathered, m_per):
    def body(g_ref, o_ref):
        acc = g_ref[0:m_per, :]
        for d in range(1, 8):
            acc = acc + g_ref[d * m_per:(d + 1) * m_per, :]
        o_ref[...] = acc

    return pl.pallas_call(body, name="small_sum", out_shape=jax.ShapeDtypeStruct((m_per, LANE), F32),
                          compiler_params=_cparams())(gathered)


def _pad_ff_cols(a):
    lead = a.shape[:-1]
    n = a.shape[-1] // FF_Q
    a = a.reshape(*lead, n, FF_Q)
    a = jnp.pad(a, [(0, 0)] * len(lead) + [(0, 0), (0, FF_QP - FF_Q)])
    return a.reshape(*lead, n * FF_QP)


def _unpad_ff_cols(a):
    lead = a.shape[:-1]
    n = a.shape[-1] // FF_QP
    return a.reshape(*lead, n, FF_QP)[..., :FF_Q].reshape(*lead, n * FF_Q)


def _pack_small(parts):
    flat = jnp.concatenate([p.reshape(-1) for p in parts])
    return flat.reshape(-1, LANE)


SMALL_SHAPES = [("conv_a_w", (CONV_K, CONV_W)), ("conv_a_b", (CONV_W,)), ("ln_a_g", (CONV_W,)), ("ln_a_b", (CONV_W,)),
                ("ln_v_g", (GMLP_W,)), ("ln_v_b", (GMLP_W,)), ("w_s", (6, CHUNK, CHUNK)), ("b_s", (6, CHUNK)),
                ("ln1_g", (D_MODEL,)), ("ln1_b", (D_MODEL,)), ("conv_f_w", (FFN_CONV_K, D_FF)), ("conv_f_b", (D_FF,)),
                ("ln2_g", (D_MODEL,)), ("ln2_b", (D_MODEL,))]


def _unpack_small(flat2d):
    flat = flat2d.reshape(DEPTH, -1)
    out, o = {}, 0
    for name, shp in SMALL_SHAPES:
        n = 1
        for d in shp:
            n *= d
        out[name] = flat[:, o:o + n].reshape((DEPTH,) + shp)
        o += n
    return out


def _rows8(rows):
    blk = jnp.stack(rows, axis=1)
    return jnp.pad(blk, ((0, 0), (0, 8 - len(rows)), (0, 0)))


def kernel(x, mem, w_in, conv_a_w, conv_a_b, ln_a_g, ln_a_b, ln_v_g, ln_v_b, w_s, b_s, w_mk, w_mv, w_out, ln1_g, ln1_b, w_up, conv_f_w, conv_f_b, w_down, ln2_g, ln2_b, loss_target, m_w_in, m_conv_a_w, m_conv_a_b, m_ln_a_g, m_ln_a_b, m_ln_v_g, m_ln_v_b, m_w_s, m_b_s, m_w_mk, m_w_mv, m_w_out, m_ln1_g, m_ln1_b, m_w_up, m_conv_f_w, m_conv_f_b, m_w_down, m_ln2_g, m_ln2_b, v_w_in, v_conv_a_w, v_conv_a_b, v_ln_a_g, v_ln_a_b, v_ln_v_g, v_ln_v_b, v_w_s, v_b_s, v_w_mk, v_w_mv, v_w_out, v_ln1_g, v_ln1_b, v_w_up, v_conv_f_w, v_conv_f_b, v_w_down, v_ln2_g, v_ln2_b):
    seq = x.shape[1]
    t_fwd = min(512, seq)
    t_bwd = min(256, seq)
    chip = 2 * lax.axis_index("x") + lax.axis_index("y")
    core = lax.axis_index("c")
    place = jnp.stack([chip, core]).astype(jnp.int32)
    x0 = x[0]
    mem0 = mem[0]
    target = loss_target[0]

    shards = [
        w_in.transpose(0, 2, 1).astype(BF16),
        w_mk.astype(BF16), w_mv.astype(BF16),
        w_out.astype(BF16),
        _pad_ff_cols(w_up).transpose(0, 2, 1).astype(BF16),
        jnp.pad(w_down, ((0, 0), (0, FF_QP - FF_Q), (0, 0))).astype(BF16),
    ]
    g_in, g_mk, g_mv, g_out, g_up, g_down = _all_gather_chips(shards, "ag_weights")
    win_t = g_in.reshape(DEPTH, IN_W, D_MODEL)
    wmk = g_mk.reshape(DEPTH, D_MODEL, XATTN_W)
    wmv = g_mv.reshape(DEPTH, D_MODEL, XATTN_W)
    wout = g_out.reshape(DEPTH, D_MODEL, D_MODEL)
    wup_t = g_up.reshape(DEPTH, 2, FF_P, D_MODEL)
    wdown = g_down.reshape(DEPTH, FF_P, D_MODEL)

    n_ca = conv_a_w.size
    small_w = _pack_small([conv_a_w, conv_f_w, jnp.zeros((2 * 80 * LANE - n_ca - conv_f_w.size,), F32)])[None]
    small_g = _all_gather_chips([small_w], "ag_conv_weights")[0].reshape(N_CHIPS, -1)
    conv_a_full = small_g[:, :n_ca].reshape(N_CHIPS, DEPTH, CONV_K, CONV_W // 4).transpose(1, 2, 0, 3).reshape(DEPTH, CONV_K, CONV_W)
    conv_f_full = small_g[:, n_ca:n_ca + conv_f_w.size].reshape(N_CHIPS, DEPTH, FFN_CONV_K, FF_Q).transpose(1, 2, 0, 3).reshape(DEPTH, FFN_CONV_K, D_FF)

    tril = jnp.tril(jnp.ones((CHUNK, CHUNK), dtype=bool))
    ws_m = jnp.where(tril, w_s, 0.0)
    wst = ws_m.reshape(DEPTH, 3, 2 * CHUNK, CHUNK).astype(BF16)
    wstt = ws_m.transpose(0, 1, 3, 2).reshape(DEPTH, 3, 2 * CHUNK, CHUNK).astype(BF16)
    bst = jnp.repeat(b_s.transpose(0, 2, 1), HEAD_DIM, axis=2)
    conv_w = jnp.pad(conv_a_full, ((0, 0), (0, CONV_HALO - CONV_K), (0, 0)))
    pa = _rows8([conv_a_b, ln_a_g, ln_a_b, ln_v_g, ln_v_b])
    gin = jnp.concatenate([jnp.ones((1, D_MODEL), F32), ln2_g[:DEPTH - 1]], axis=0)
    bin_ = jnp.concatenate([jnp.zeros((1, D_MODEL), F32), ln2_b[:DEPTH - 1]], axis=0)
    pd = _rows8([gin, bin_, ln1_g, ln1_b, ln2_g, ln2_b])
    pf = jnp.concatenate([_pad_ff_cols(conv_f_full), _pad_ff_cols(conv_f_b)[:, None, :],
                          jnp.zeros((DEPTH, 8 - FFN_CONV_K - 1, FF_P), F32)], axis=1)

    acts = []
    xin = x0
    for l in range(DEPTH):
        kt_all, k_all, v_all, vt_all = _kv_fwd(mem0, wmk, wmv, l)
        xh1, rstd1, h, ac, cat = _mixer_fwd(xin, pd, win_t, conv_w, pa, wst, bst, kt_all, v_all, wout, l, t_fwd)
        xh2, rstd2, upg, upv = _ffn_fwd(xh1, pd, wup_t, pf, wdown, l, t_fwd)
        acts.append(dict(xin=xin, kt_all=kt_all, k_all=k_all, vt_all=vt_all, xh1=xh1, rstd1=rstd1, h=h, ac=ac, cat=cat,
                         xh2=xh2, rstd2=rstd2, upg=upg, upv=upv))
        xin = xh2

    small = [None] * DEPTH
    gz = target
    loss_sum = None
    g_ffn = g_mix = g_kv = None
    for l in reversed(range(DEPTH)):
        a = acts[l]
        last = l == DEPTH - 1
        dx1, dy, dug, duv, hm, vd2, vf = _ffn_bwd_d(gz, a["xh2"], a["rstd2"], a["upg"], a["upv"], pd, pf, wdown, wup_t,
                                                    l, t_bwd, last)
        if last:
            loss_sum = vd2[VD_LOSS, 0]
        g_ffn = _ffn_bwd_w(a["xh1"], pd, dy, dug, duv, hm, l, t_fwd, g_ffn)
        dx0, dh, dmix, vd1, va, dcw, dws, dbs, dkt, dv = _mixer_bwd_d(
            dx1, a["xh1"], a["rstd1"], a["h"], a["ac"], pd, conv_w, pa, wst, wstt, bst,
            a["kt_all"], a["k_all"], a["vt_all"], wout, win_t, l, t_bwd)
        g_mix = _mixer_bwd_w(a["xin"], pd, dh, a["cat"], dmix, l, t_fwd, g_mix)
        g_kv = _kv_bwd(mem0, dkt, dv, l, g_kv)
        dws6 = jnp.where(tril, dws.reshape(6, CHUNK, CHUNK), 0.0)
        small[l] = [dcw[:CONV_K], va[VA_CONV_B], va[VA_LNA_G], va[VA_LNA_B], va[VA_LNV_G], va[VA_LNV_B], dws6,
                    dbs[:, :6].T, vd1[VD_LN_G], vd1[VD_LN_B],
                    _unpad_ff_cols(vf[VF_W0:VF_W0 + FFN_CONV_K]), _unpad_ff_cols(vf[VF_B]),
                    vd2[VD_LN_G], vd2[VD_LN_B]]
        gz = dx0
    grad_x = gz[None]

    gw_up_t, gw_down = g_ffn
    gw_in_t, gw_out = g_mix
    gw_mk, gw_mv = g_kv
    tensors = [gw_in_t, gw_mk, gw_mv, gw_out, gw_up_t.reshape(DEPTH, 2 * FF_P, D_MODEL), gw_down]
    g5 = [g.reshape(DEPTH, N_CHIPS, 2, g.shape[1] // (2 * N_CHIPS), g.shape[2]) for g in tensors]
    recvs = _sibling_swap_halves(g5)
    parts = _add_halves(g5, recvs, place)
    slots = _chip_scatter(parts)
    halves = _sum_slots(parts, slots, place)
    red = [f.reshape(DEPTH, -1, f.shape[-1]) for f in _sibling_gather(halves)]
    g_w_in = red[0].transpose(0, 2, 1)
    g_w_mk, g_w_mv, g_w_out = red[1], red[2], red[3]
    g_w_up = _unpad_ff_cols(red[4].transpose(0, 2, 1))
    g_w_down = red[5][:, :FF_Q, :]

    small_local = _pack_small([q for l in range(DEPTH) for q in small[l]])
    m_small = small_local.shape[0]
    small_red = _sum_devices(_all_gather_devices(small_local), m_small)
    sg = _unpack_small(small_red)
    g_conv_a_w = lax.dynamic_slice_in_dim(sg["conv_a_w"], chip * (CONV_W // 4), CONV_W // 4, axis=2)
    g_conv_f_w = lax.dynamic_slice_in_dim(sg["conv_f_w"], chip * FF_Q, FF_Q, axis=2)

    loss = 0.5 / D_MODEL * lax.psum(loss_sum, ("x", "y", "c"))

    grads = dict(w_in=g_w_in, conv_a_w=g_conv_a_w, conv_a_b=sg["conv_a_b"], ln_a_g=sg["ln_a_g"], ln_a_b=sg["ln_a_b"],
                 ln_v_g=sg["ln_v_g"], ln_v_b=sg["ln_v_b"], w_s=sg["w_s"], b_s=sg["b_s"], w_mk=g_w_mk, w_mv=g_w_mv,
                 w_out=g_w_out, ln1_g=sg["ln1_g"], ln1_b=sg["ln1_b"], w_up=g_w_up, conv_f_w=g_conv_f_w,
                 conv_f_b=sg["conv_f_b"], w_down=g_w_down, ln2_g=sg["ln2_g"], ln2_b=sg["ln2_b"])
    weights = dict(w_in=w_in, conv_a_w=conv_a_w, conv_a_b=conv_a_b, ln_a_g=ln_a_g, ln_a_b=ln_a_b, ln_v_g=ln_v_g,
                   ln_v_b=ln_v_b, w_s=w_s, b_s=b_s, w_mk=w_mk, w_mv=w_mv, w_out=w_out, ln1_g=ln1_g, ln1_b=ln1_b,
                   w_up=w_up, conv_f_w=conv_f_w, conv_f_b=conv_f_b, w_down=w_down, ln2_g=ln2_g, ln2_b=ln2_b)
    mom_m = dict(w_in=m_w_in, conv_a_w=m_conv_a_w, conv_a_b=m_conv_a_b, ln_a_g=m_ln_a_g, ln_a_b=m_ln_a_b, ln_v_g=m_ln_v_g,
                 ln_v_b=m_ln_v_b, w_s=m_w_s, b_s=m_b_s, w_mk=m_w_mk, w_mv=m_w_mv, w_out=m_w_out, ln1_g=m_ln1_g,
                 ln1_b=m_ln1_b, w_up=m_w_up, conv_f_w=m_conv_f_w, conv_f_b=m_conv_f_b, w_down=m_w_down, ln2_g=m_ln2_g,
                 ln2_b=m_ln2_b)
    mom_v = dict(w_in=v_w_in, conv_a_w=v_conv_a_w, conv_a_b=v_conv_a_b, ln_a_g=v_ln_a_g, ln_a_b=v_ln_a_b, ln_v_g=v_ln_v_g,
                 ln_v_b=v_ln_v_b, w_s=v_w_s, b_s=v_b_s, w_mk=v_w_mk, w_mv=v_w_mv, w_out=v_w_out, ln1_g=v_ln1_g,
                 ln1_b=v_ln1_b, w_up=v_w_up, conv_f_w=v_conv_f_w, conv_f_b=v_conv_f_b, w_down=v_w_down, ln2_g=v_ln2_g,
                 ln2_b=v_ln2_b)
    names = list(weights)
    big_names = ["w_in", "w_mk", "w_mv", "w_out", "w_up", "w_down"]
    delta, new_m, new_v = {}, {}, {}
    for n in big_names:
        shp = weights[n].shape
        as2d = lambda a: a.reshape(-1, shp[-1])
        d, nm, nv = _adamw(as2d(weights[n]), as2d(grads[n]), as2d(mom_m[n]), as2d(mom_v[n]), "adamw_" + n)
        delta[n], new_m[n], new_v[n] = d.reshape(shp), nm.reshape(shp), nv.reshape(shp)
    small_names = [n for n in names if n not in big_names]
    sizes = [weights[n].size for n in small_names]
    pad = (-sum(sizes)) % (8 * LANE)

    def pack(dct, fill):
        return _pack_small([dct[n] for n in small_names] + [jnp.full((pad,), fill, F32)])

    d, nm, nv = _adamw(pack(weights, 0.0), pack(grads, 0.0), pack(mom_m, 0.0), pack(mom_v, 1.0), "adamw_small")
    o = 0
    for n, sz in zip(small_names, sizes):
        shp = weights[n].shape
        delta[n] = d.reshape(-1)[o:o + sz].reshape(shp)
        new_m[n] = nm.reshape(-1)[o:o + sz].reshape(shp)
        new_v[n] = nv.reshape(-1)[o:o + sz].reshape(shp)
        o += sz

    return (loss, grad_x, *[grads[n] for n in names], *[delta[n] for n in names],
            *[new_m[n] for n in names], *[new_v[n] for n in names])
```

```python
import jax
import jax.numpy as jnp
from jax import lax
from jax.experimental import pallas as pl
from jax.experimental.pallas import tpu as pltpu

F32 = jnp.float32
BF16 = jnp.bfloat16

D_MODEL = 1024
DEPTH = 2
CONV_W = 384
GMLP_W = 384
XATTN_W = 256
XATTN_HEADS = 4
HEAD_DIM = 64
IN_W = 1792
CONV_K = 31
CHUNK = 128
N_MEM = 256
D_FF = 2752
FFN_CONV_K = 3
ALPHA = (2.0 * DEPTH) ** 0.25
LN_EPS = 1e-5
ATT_SCALE = 1.0 / 8.0
ADAM_LR, ADAM_B1, ADAM_B2, ADAM_EPS, ADAM_WD, ADAM_STEP = 0.001, 0.9, 0.999, 1e-08, 0.01, 10

N_CHIPS = 4
FF_Q = D_FF // N_CHIPS
FF_QP = 704
FF_H = 2 * FF_QP
FF_P = 4 * FF_QP
LANE = 128
CONV_HALO = 32
FFN_HALO = 8
BF16_ROWS = 16
VMEM_LIMIT = 60 * 1024 * 1024

MESH = pl.DeviceIdType.MESH
ANY = pl.BlockSpec(memory_space=pl.ANY)


def _cparams(sem=None, vmem=VMEM_LIMIT):
    kw = {"vmem_limit_bytes": vmem}
    if sem is not None:
        kw["dimension_semantics"] = sem
    return pltpu.CompilerParams(**kw)


def _row_tile(rows, row_bytes, limit=2 << 20):
    if rows * row_bytes <= limit:
        return rows
    best = None
    for cand in range(BF16_ROWS, rows, BF16_ROWS):
        if rows % cand == 0 and cand * row_bytes <= limit:
            best = cand
    assert best is not None, (rows, row_bytes)
    return best


def _const_spec(shape):
    nd = len(shape)
    return pl.BlockSpec(shape, lambda *_: (0,) * nd)


def _layer_spec(shape, *lead, resident=False):
    nd = len(shape)
    kw = {"pipeline_mode": pl.Buffered(1)} if resident else {}
    return pl.BlockSpec((None,) * len(lead) + tuple(shape), lambda *_: tuple(lead) + (0,) * nd, **kw)


def _sigmoid(x):
    return jax.nn.sigmoid(x)


def _gelu(x):
    return jax.nn.gelu(x)


def _gelu_grad(x):
    c = 0.7978845608028654
    a = 0.044715
    t = jnp.tanh(c * (x + a * x * x * x))
    return 0.5 * (1.0 + t) + 0.5 * x * (1.0 - t * t) * c * (1.0 + 3.0 * a * x * x)


def _ln_fwd(z):
    mu = jnp.mean(z, axis=-1, keepdims=True)
    zc = z - mu
    var = jnp.mean(zc * zc, axis=-1, keepdims=True)
    rstd = lax.rsqrt(var + LN_EPS)
    return zc * rstd, rstd


def _ln_bwd(dxh, xh, rstd):
    m1 = jnp.mean(dxh, axis=-1, keepdims=True)
    m2 = jnp.mean(dxh * xh, axis=-1, keepdims=True)
    return rstd * (dxh - m1 - xh * m2)


def _colsum(a):
    return jnp.sum(a, axis=0, keepdims=True)


def _dot(a, b):
    return jnp.dot(a, b, preferred_element_type=F32)


def _dot_tn(a, b):
    return lax.dot_general(a, b, (((0,), (0,)), ((), ())), preferred_element_type=F32)


def _dot_nt(a, b):
    return lax.dot_general(a, b, (((1,), (1,)), ((), ())), preferred_element_type=F32)


def _softmax_heads(sc):
    ps = []
    for hd in range(XATTN_HEADS):
        s = sc[:, hd * N_MEM:(hd + 1) * N_MEM]
        e = jnp.exp(s - jnp.max(s, axis=-1, keepdims=True))
        ps.append(e / jnp.sum(e, axis=-1, keepdims=True))
    return jnp.concatenate(ps, axis=1)


def _lane_lo(shape):
    return (lax.broadcasted_iota(jnp.int32, shape, len(shape) - 1) % LANE) < HEAD_DIM


def _spatial_mix(vnb, wst_ref, bst_ref, mix_ref, t):
    lo = _lane_lo((CHUNK, LANE))
    for n in range(t // CHUNK):
        rows = slice(n * CHUNK, (n + 1) * CHUNK)
        for j in range(GMLP_W // LANE):
            cols = slice(j * LANE, (j + 1) * LANE)
            r = _dot(wst_ref[j], vnb[rows, cols])
            mix_ref[rows, cols] = jnp.where(lo, r[:CHUNK], r[CHUNK:]) + bst_ref[:, cols]


def _stacked_out(prev, shapes, specs, n_in):
    extra_specs = [] if prev is None else [ANY] * len(shapes)
    aliases = {} if prev is None else {n_in + k: k for k in range(len(shapes))}
    return extra_specs, aliases


def _kv_fwd(mem, w_mk, w_mv, l):
    def body(mem_ref, wk_ref, wv_ref, kt_ref, k_ref, v_ref, vt_ref):
        mb = mem_ref[...].astype(BF16)
        k = _dot(mb, wk_ref[...])
        v = _dot(mb, wv_ref[...])
        col = lax.broadcasted_iota(jnp.int32, (N_MEM, XATTN_W), 1) // HEAD_DIM
        ks = [jnp.where(col == hd, k, 0.0) for hd in range(XATTN_HEADS)]
        vs = [jnp.where(col == hd, v, 0.0) for hd in range(XATTN_HEADS)]
        k_ref[...] = jnp.concatenate(ks, axis=0).astype(BF16)
        v_ref[...] = jnp.concatenate(vs, axis=0).astype(BF16)
        kt_ref[...] = jnp.concatenate([x.T for x in ks], axis=1).astype(BF16)
        vt_ref[...] = jnp.concatenate([x.T for x in vs], axis=1).astype(BF16)

    wide = jax.ShapeDtypeStruct((XATTN_W, XATTN_HEADS * N_MEM), BF16)
    tall = jax.ShapeDtypeStruct((XATTN_HEADS * N_MEM, XATTN_W), BF16)
    wspec = _layer_spec((D_MODEL, XATTN_W), l)
    return pl.pallas_call(body, name="kv_fwd", grid=(1,),
                          in_specs=[_const_spec((N_MEM, D_MODEL)), wspec, wspec],
                          out_specs=[_const_spec(wide.shape), _const_spec(tall.shape), _const_spec(tall.shape),
                                     _const_spec(wide.shape)],
                          out_shape=(wide, tall, tall, wide), compiler_params=_cparams(("arbitrary",)))(mem, w_mk, w_mv)


def _kv_bwd(mem, dkt_all, dv_all, l, prev):
    def body(mem_ref, dkt_ref, dv_ref, *rest):
        gk_ref, gv_ref = rest[-2:]
        col = lax.broadcasted_iota(jnp.int32, (N_MEM, XATTN_W), 1) // HEAD_DIM
        dk = jnp.zeros((N_MEM, XATTN_W), F32)
        dv = jnp.zeros((N_MEM, XATTN_W), F32)
        for hd in range(XATTN_HEADS):
            dk = dk + jnp.where(col == hd, dkt_ref[:, hd * N_MEM:(hd + 1) * N_MEM].T, 0.0)
            dv = dv + jnp.where(col == hd, dv_ref[hd * N_MEM:(hd + 1) * N_MEM, :], 0.0)
        mb = mem_ref[...].astype(BF16)
        gk_ref[...] = _dot_tn(mb, dk.astype(BF16))
        gv_ref[...] = _dot_tn(mb, dv.astype(BF16))

    out = jax.ShapeDtypeStruct((DEPTH, D_MODEL, XATTN_W), F32)
    ospec = _layer_spec((D_MODEL, XATTN_W), l)
    extra, aliases = _stacked_out(prev, [out, out], None, 3)
    return pl.pallas_call(body, name="kv_bwd", grid=(1,),
                          in_specs=[_const_spec((N_MEM, D_MODEL)), _const_spec(dkt_all.shape), _const_spec(dv_all.shape)] + extra,
                          out_specs=[ospec, ospec], out_shape=(out, out), input_output_aliases=aliases,
                          compiler_params=_cparams(("arbitrary",)))(mem, dkt_all, dv_all, *(prev or ()))


PA_CONV_B, PA_LNA_G, PA_LNA_B, PA_LNV_G, PA_LNV_B = 0, 1, 2, 3, 4
PD_GIN, PD_BIN, PD_G1, PD_B1, PD_G2, PD_B2 = 0, 1, 2, 3, 4, 5


def _row(ref, r):
    return ref[r:r + 1, :]


def _mixer_fwd(xin, pd, win_t, conv_w, pa, wst, bst, kt_all, v_all, w_out, l, t):
    s = xin.shape[0]
    nt = s // t

    def body(x_ref, pd_ref, wint_ref, cw_ref, pa_ref, wst_ref, bst_ref, kt_ref, v_ref, wout_ref,
             xh_ref, rstd_ref, h_ref, ac_ref, cat_ref, cbuf, mixbuf, zbuf):
        i = pl.program_id(0)
        x = x_ref[...] * _row(pd_ref, PD_GIN) + _row(pd_ref, PD_BIN)
        h = _dot_nt(x.astype(BF16), wint_ref[...])
        h_ref[...] = h
        a1, a2 = h[:, 0:CONV_W], h[:, CONV_W:2 * CONV_W]
        hu, hv = h[:, 2 * CONV_W:2 * CONV_W + GMLP_W], h[:, 2 * CONV_W + GMLP_W:2 * CONV_W + 2 * GMLP_W]
        q = h[:, IN_W - XATTN_W:]

        @pl.when(i == 0)
        def _():
            cbuf[0:CONV_HALO, :] = jnp.zeros((CONV_HALO, CONV_W), F32)

        cbuf[CONV_HALO:CONV_HALO + t, :] = a1 * _sigmoid(a2)
        ac = jnp.zeros((t, CONV_W), F32) + _row(pa_ref, PA_CONV_B)
        for r in range(8):
            zr = jnp.zeros((t + 8, CONV_W), F32)
            for a in range(4):
                o = 8 * a + r
                if o < CONV_K:
                    k = CONV_K - 1 - o
                    zr = zr + cbuf[CONV_HALO - 8 - 8 * a:CONV_HALO - 8 - 8 * a + t + 8, :] * cw_ref[k:k + 1, :]
            if r == 0:
                ac = ac + zr[8:, :]
            else:
                zbuf[...] = zr
                ac = ac + zbuf[8 - r:8 - r + t, :]
        ac_ref[...] = ac
        cbuf[0:CONV_HALO, :] = cbuf[t:t + CONV_HALO, :]
        xh_a, _ = _ln_fwd(ac)
        an = xh_a * _row(pa_ref, PA_LNA_G) + _row(pa_ref, PA_LNA_B)
        a = an * _sigmoid(an)

        u = _gelu(hu)
        xh_v, _ = _ln_fwd(_gelu(hv))
        vn = xh_v * _row(pa_ref, PA_LNV_G) + _row(pa_ref, PA_LNV_B)
        _spatial_mix(vn.astype(BF16), wst_ref, bst_ref, mixbuf, t)
        g = u * mixbuf[...]

        p = _softmax_heads(_dot(q.astype(BF16), kt_ref[...]) * ATT_SCALE)
        o = _dot(p.astype(BF16), v_ref[...])

        cat = jnp.concatenate([a, g, o], axis=1).astype(BF16)
        cat_ref[...] = cat
        z = ALPHA * x + _dot(cat, wout_ref[...])
        xh, rstd = _ln_fwd(z)
        xh_ref[...] = xh
        rstd_ref[...] = rstd

    tok = lambda w: pl.BlockSpec((t, w), lambda i: (i, 0))
    return pl.pallas_call(
        body, name="mixer_fwd", grid=(nt,),
        in_specs=[tok(D_MODEL), _layer_spec((8, D_MODEL), l), _layer_spec((IN_W, D_MODEL), l, resident=True),
                  _layer_spec((CONV_HALO, CONV_W), l), _layer_spec((8, CONV_W), l),
                  _layer_spec((3, 2 * CHUNK, CHUNK), l), _layer_spec((CHUNK, GMLP_W), l),
                  _const_spec((XATTN_W, XATTN_HEADS * N_MEM)), _const_spec((XATTN_HEADS * N_MEM, XATTN_W)),
                  _layer_spec((D_MODEL, D_MODEL), l, resident=True)],
        out_specs=[tok(D_MODEL), tok(1), tok(IN_W), tok(CONV_W), tok(D_MODEL)],
        out_shape=[jax.ShapeDtypeStruct((s, D_MODEL), F32), jax.ShapeDtypeStruct((s, 1), F32),
                   jax.ShapeDtypeStruct((s, IN_W), F32), jax.ShapeDtypeStruct((s, CONV_W), F32),
                   jax.ShapeDtypeStruct((s, D_MODEL), BF16)],
        scratch_shapes=[pltpu.VMEM((t + CONV_HALO, CONV_W), F32), pltpu.VMEM((t, GMLP_W), F32),
                        pltpu.VMEM((t + 8, CONV_W), F32)],
        compiler_params=_cparams(("arbitrary",)),
    )(xin, pd, win_t, conv_w, pa, wst, bst, kt_all, v_all, w_out)


VD_LN_G, VD_LN_B, VD_LOSS = 0, 1, 2
VA_CONV_B, VA_LNA_G, VA_LNA_B, VA_LNV_G, VA_LNV_B = 0, 1, 2, 3, 4


def _mixer_bwd_d(gz, xh1, rstd1, h, ac, pd, conv_w, pa, wst, wstt, bst, kt_all, k_all, vt_all, w_out, win_t, l, t):
    s = gz.shape[0]
    nt = s // t

    def body(gz_ref, xh_ref, rstd_ref, h_ref, ac_ref, pd_ref, cw_ref, pa_ref, wst_ref, wstt_ref, bst_ref,
             kt_ref, k_ref, vt_ref, wout_ref, wint_ref,
             dx_ref, dh_ref, dmix_ref, vd_ref, va_ref, dcw_ref, dws_ref, dbs_ref, dkt_ref, dv_ref,
             ebuf, mixbuf, dvnbuf, dbsacc, erbuf):
        i = pl.program_id(0)

        @pl.when(i == 0)
        def _():
            vd_ref[...] = jnp.zeros_like(vd_ref)
            va_ref[...] = jnp.zeros_like(va_ref)
            dcw_ref[...] = jnp.zeros_like(dcw_ref)
            dws_ref[...] = jnp.zeros_like(dws_ref)
            dbs_ref[...] = jnp.zeros_like(dbs_ref)
            dkt_ref[...] = jnp.zeros_like(dkt_ref)
            dv_ref[...] = jnp.zeros_like(dv_ref)
            dbsacc[...] = jnp.zeros_like(dbsacc)
            ebuf[t:t + CONV_HALO, :] = jnp.zeros((CONV_HALO, CONV_W), F32)

        gz_v = gz_ref[...]
        xh = xh_ref[...]
        vd_ref[VD_LN_G:VD_LN_G + 1, :] += _colsum(gz_v * xh)
        vd_ref[VD_LN_B:VD_LN_B + 1, :] += _colsum(gz_v)
        dz = _ln_bwd(gz_v * _row(pd_ref, PD_G1), xh, rstd_ref[...])
        dzb = dz.astype(BF16)
        dmix_ref[...] = dzb
        dcat = _dot_nt(dzb, wout_ref[...])
        d_a, d_g, d_o = dcat[:, 0:CONV_W], dcat[:, CONV_W:CONV_W + GMLP_W], dcat[:, CONV_W + GMLP_W:]

        h = h_ref[...]
        a1, a2 = h[:, 0:CONV_W], h[:, CONV_W:2 * CONV_W]
        hu, hv = h[:, 2 * CONV_W:2 * CONV_W + GMLP_W], h[:, 2 * CONV_W + GMLP_W:2 * CONV_W + 2 * GMLP_W]
        q = h[:, IN_W - XATTN_W:]

        xh_a, rstd_a = _ln_fwd(ac_ref[...])
        an = xh_a * _row(pa_ref, PA_LNA_G) + _row(pa_ref, PA_LNA_B)
        sig = _sigmoid(an)
        d_an = d_a * (sig * (1.0 + an * (1.0 - sig)))
        va_ref[VA_LNA_G:VA_LNA_G + 1, :] += _colsum(d_an * xh_a)
        va_ref[VA_LNA_B:VA_LNA_B + 1, :] += _colsum(d_an)
        dac = _ln_bwd(d_an * _row(pa_ref, PA_LNA_G), xh_a, rstd_a)
        va_ref[VA_CONV_B:VA_CONV_B + 1, :] += _colsum(dac)
        ebuf[0:t, :] = dac
        sg = _sigmoid(a2)
        glu = a1 * sg
        dglu = jnp.zeros((t, CONV_W), F32)
        for r in range(8):
            if r > 0:
                erbuf[...] = ebuf[r:r + t + 24, :]
            src = ebuf if r == 0 else erbuf
            for a in range(4):
                o = 8 * a + r
                if o < CONV_K:
                    k = CONV_K - 1 - o
                    ek = src[8 * a:8 * a + t, :]
                    dglu = dglu + ek * cw_ref[k:k + 1, :]
                    dcw_ref[k:k + 1, :] += _colsum(ek * glu)
        ebuf[t:t + CONV_HALO, :] = ebuf[0:CONV_HALO, :]
        da1 = dglu * sg
        da2 = dglu * a1 * sg * (1.0 - sg)

        u = _gelu(hu)
        xh_v, rstd_v = _ln_fwd(_gelu(hv))
        vn = xh_v * _row(pa_ref, PA_LNV_G) + _row(pa_ref, PA_LNV_B)
        vnb = vn.astype(BF16)
        _spatial_mix(vnb, wst_ref, bst_ref, mixbuf, t)
        dhu = d_g * mixbuf[...] * _gelu_grad(hu)
        dm = d_g * u
        dmb = dm.astype(BF16)
        lo = _lane_lo((CHUNK, LANE))
        for n in range(t // CHUNK):
            rows = slice(n * CHUNK, (n + 1) * CHUNK)
            dbsacc[...] += dm[rows, :]
            for j in range(GMLP_W // LANE):
                cols = slice(j * LANE, (j + 1) * LANE)
                dm_blk = dmb[rows, cols]
                r = _dot(wstt_ref[j], dm_blk)
                dvnbuf[rows, cols] = jnp.where(lo, r[:CHUNK], r[CHUNK:])
                zero = jnp.zeros_like(dm_blk)
                st = jnp.concatenate([jnp.where(lo, dm_blk, zero), jnp.where(lo, zero, dm_blk)], axis=0)
                dws_ref[j] += _dot_nt(st, vnb[rows, cols])
        dvn = dvnbuf[...]
        va_ref[VA_LNV_G:VA_LNV_G + 1, :] += _colsum(dvn * xh_v)
        va_ref[VA_LNV_B:VA_LNV_B + 1, :] += _colsum(dvn)
        dhv = _ln_bwd(dvn * _row(pa_ref, PA_LNV_G), xh_v, rstd_v) * _gelu_grad(hv)

        qb = q.astype(BF16)
        p = _softmax_heads(_dot(qb, kt_ref[...]) * ATT_SCALE)
        dob = d_o.astype(BF16)
        dp = _dot(dob, vt_ref[...])
        dss = []
        for hd in range(XATTN_HEADS):
            cs = slice(hd * N_MEM, (hd + 1) * N_MEM)
            ph, dph = p[:, cs], dp[:, cs]
            dss.append(ph * (dph - jnp.sum(ph * dph, axis=-1, keepdims=True)) * ATT_SCALE)
        dsb = jnp.concatenate(dss, axis=1).astype(BF16)
        dq = _dot(dsb, k_ref[...])
        dkt_ref[...] += _dot_tn(qb, dsb)
        dv_ref[...] += _dot_tn(p.astype(BF16), dob)

        dhb = jnp.concatenate([da1, da2, dhu, dhv, dq], axis=1).astype(BF16)
        dh_ref[...] = dhb
        dx_ref[...] = ALPHA * dz + _dot(dhb, wint_ref[...])

        @pl.when(i == nt - 1)
        def _():
            acc = dbsacc[...]
            head = lax.broadcasted_iota(jnp.int32, (CHUNK, GMLP_W), 1) // HEAD_DIM
            lane = lax.broadcasted_iota(jnp.int32, (CHUNK, LANE), 1)
            out = jnp.zeros((CHUNK, LANE), F32)
            for hd in range(GMLP_W // HEAD_DIM):
                sh = jnp.sum(jnp.where(head == hd, acc, 0.0), axis=1, keepdims=True)
                out = out + jnp.where(lane == hd, sh, 0.0)
            dbs_ref[...] = out

    rev = lambda w: pl.BlockSpec((t, w), lambda i: (nt - 1 - i, 0))
    out_shape = [
        jax.ShapeDtypeStruct((s, D_MODEL), F32), jax.ShapeDtypeStruct((s, IN_W), BF16),
        jax.ShapeDtypeStruct((s, D_MODEL), BF16),
        jax.ShapeDtypeStruct((8, D_MODEL), F32), jax.ShapeDtypeStruct((8, CONV_W), F32),
        jax.ShapeDtypeStruct((CONV_HALO, CONV_W), F32), jax.ShapeDtypeStruct((3, 2 * CHUNK, CHUNK), F32),
        jax.ShapeDtypeStruct((CHUNK, LANE), F32),
        jax.ShapeDtypeStruct((XATTN_W, XATTN_HEADS * N_MEM), F32), jax.ShapeDtypeStruct((XATTN_HEADS * N_MEM, XATTN_W), F32),
    ]
    out_specs = [rev(D_MODEL), rev(IN_W), rev(D_MODEL)] + [_const_spec(o.shape) for o in out_shape[3:]]
    return pl.pallas_call(
        body, name="mixer_bwd_d", grid=(nt,),
        in_specs=[rev(D_MODEL), rev(D_MODEL), rev(1), rev(IN_W), rev(CONV_W),
                  _layer_spec((8, D_MODEL), l), _layer_spec((CONV_HALO, CONV_W), l), _layer_spec((8, CONV_W), l),
                  _layer_spec((3, 2 * CHUNK, CHUNK), l), _layer_spec((3, 2 * CHUNK, CHUNK), l),
                  _layer_spec((CHUNK, GMLP_W), l),
                  _const_spec((XATTN_W, XATTN_HEADS * N_MEM)), _const_spec((XATTN_HEADS * N_MEM, XATTN_W)),
                  _const_spec((XATTN_W, XATTN_HEADS * N_MEM)),
                  _layer_spec((D_MODEL, D_MODEL), l, resident=True), _layer_spec((IN_W, D_MODEL), l, resident=True)],
        out_specs=out_specs, out_shape=out_shape,
        scratch_shapes=[pltpu.VMEM((t + CONV_HALO, CONV_W), F32), pltpu.VMEM((t, GMLP_W), F32),
                        pltpu.VMEM((t, GMLP_W), F32), pltpu.VMEM((CHUNK, GMLP_W), F32),
                        pltpu.VMEM((t + 24, CONV_W), F32)],
        compiler_params=_cparams(("arbitrary",)),
    )(gz, xh1, rstd1, h, ac, pd, conv_w, pa, wst, wstt, bst, kt_all, k_all, vt_all, w_out, win_t)


def _mixer_bwd_w(xin, pd, dh, cat, dmix, l, t, prev):
    s = xin.shape[0]
    nt = s // t

    def body(x_ref, pd_ref, dh_ref, cat_ref, dmix_ref, *rest):
        dwin_ref, dwout_ref = rest[-2:]

        @pl.when(pl.program_id(0) == 0)
        def _():
            dwin_ref[...] = jnp.zeros_like(dwin_ref)
            dwout_ref[...] = jnp.zeros_like(dwout_ref)

        xb = (x_ref[...] * _row(pd_ref, PD_GIN) + _row(pd_ref, PD_BIN)).astype(BF16)
        dwin_ref[...] += _dot_tn(dh_ref[...], xb)
        dwout_ref[...] += _dot_tn(cat_ref[...], dmix_ref[...])

    tok = lambda w: pl.BlockSpec((t, w), lambda i: (i, 0))
    shapes = [jax.ShapeDtypeStruct((DEPTH, IN_W, D_MODEL), F32), jax.ShapeDtypeStruct((DEPTH, D_MODEL, D_MODEL), F32)]
    extra, aliases = _stacked_out(prev, shapes, None, 5)
    return pl.pallas_call(
        body, name="mixer_bwd_w", grid=(nt,),
        in_specs=[tok(D_MODEL), _layer_spec((8, D_MODEL), l), tok(IN_W), tok(D_MODEL), tok(D_MODEL)] + extra,
        out_specs=[_layer_spec((IN_W, D_MODEL), l), _layer_spec((D_MODEL, D_MODEL), l)],
        out_shape=shapes, input_output_aliases=aliases,
        compiler_params=_cparams(("arbitrary",)),
    )(xin, pd, dh, cat, dmix, *(prev or ()))


PF_W0, PF_B = 0, 3


def _ffn_fwd(xh1, pd, wup_t, pf, w_d, l, t):
    s = xh1.shape[0]
    nt = s // t

    def body(xh_ref, pd_ref, wg_ref, wv_ref, pf_ref, wd_ref, xh2_ref, rstd_ref, upg_ref, upv_ref, fbuf):
        i = pl.program_id(0)

        @pl.when(i == 0)
        def _():
            fbuf[0:FFN_HALO, :] = jnp.zeros((FFN_HALO, FF_P), F32)

        x1 = xh_ref[...] * _row(pd_ref, PD_G1) + _row(pd_ref, PD_B1)
        xb = x1.astype(BF16)
        y = jnp.zeros((t, D_MODEL), F32)
        for hf in range(2):
            cs = slice(hf * FF_H, (hf + 1) * FF_H)
            ug = _dot_nt(xb, wg_ref[cs, :])
            uv = _dot_nt(xb, wv_ref[cs, :])
            upg_ref[:, cs] = ug
            upv_ref[:, cs] = uv
            fbuf[FFN_HALO:FFN_HALO + t, cs] = ug
            gate = jnp.zeros((t, FF_H), F32) + pf_ref[PF_B:PF_B + 1, cs]
            for k in range(FFN_CONV_K):
                off = FFN_HALO - (FFN_CONV_K - 1) + k
                gate = gate + fbuf[off:off + t, cs] * pf_ref[PF_W0 + k:PF_W0 + k + 1, cs]
            fbuf[0:FFN_HALO, cs] = fbuf[t:t + FFN_HALO, cs]
            hm = gate * _sigmoid(gate) * uv
            y = y + _dot(hm.astype(BF16), wd_ref[cs, :])
        xh2, rstd = _ln_fwd(ALPHA * x1 + y)
        xh2_ref[...] = xh2
        rstd_ref[...] = rstd

    tok = lambda w: pl.BlockSpec((t, w), lambda i: (i, 0))
    return pl.pallas_call(
        body, name="ffn_fwd", grid=(nt,),
        in_specs=[tok(D_MODEL), _layer_spec((8, D_MODEL), l),
                  _layer_spec((FF_P, D_MODEL), l, 0, resident=True), _layer_spec((FF_P, D_MODEL), l, 1, resident=True),
                  _layer_spec((8, FF_P), l), _layer_spec((FF_P, D_MODEL), l, resident=True)],
        out_specs=[tok(D_MODEL), tok(1), tok(FF_P), tok(FF_P)],
        out_shape=[jax.ShapeDtypeStruct((s, D_MODEL), F32), jax.ShapeDtypeStruct((s, 1), F32),
                   jax.ShapeDtypeStruct((s, FF_P), F32), jax.ShapeDtypeStruct((s, FF_P), F32)],
        scratch_shapes=[pltpu.VMEM((t + FFN_HALO, FF_P), F32)],
        compiler_params=_cparams(("arbitrary",)),
    )(xh1, pd, wup_t, wup_t, pf, w_d)


VF_W0, VF_B = 0, 3


def _ffn_bwd_d(gz_or_target, xh2, rstd2, upg, upv, pd, pf, w_d, wup_t, l, t, last):
    s = xh2.shape[0]
    nt = s // t
    hb = t // FFN_HALO

    def body(gz_ref, xh2_ref, rstd_ref, upg_ref, halo_ref, upv_ref, pd_ref, pf_ref, wd_ref, wg_ref, wv_ref,
             dx_ref, dy_ref, dug_ref, duv_ref, hm_ref, vd_ref, vf_ref, gbuf, ebuf, s1buf, s2buf):
        i = pl.program_id(0)
        first_tile = i == nt - 1

        @pl.when(i == 0)
        def _():
            vd_ref[...] = jnp.zeros_like(vd_ref)
            vf_ref[...] = jnp.zeros_like(vf_ref)
            ebuf[t:t + FFN_HALO, :] = jnp.zeros((FFN_HALO, FF_P), F32)

        xh2_v = xh2_ref[...]
        if last:
            diff = xh2_v * _row(pd_ref, PD_G2) + _row(pd_ref, PD_B2) - gz_ref[...]
            vd_ref[VD_LOSS:VD_LOSS + 1, :] += _colsum(diff * diff)
            gz_v = diff * (1.0 / D_MODEL)
        else:
            gz_v = gz_ref[...]
        vd_ref[VD_LN_G:VD_LN_G + 1, :] += _colsum(gz_v * xh2_v)
        vd_ref[VD_LN_B:VD_LN_B + 1, :] += _colsum(gz_v)
        dz = _ln_bwd(gz_v * _row(pd_ref, PD_G2), xh2_v, rstd_ref[...])
        dyb = dz.astype(BF16)
        dy_ref[...] = dyb
        dx = ALPHA * dz
        for hf in range(2):
            cs = slice(hf * FF_H, (hf + 1) * FF_H)
            ug = upg_ref[:, cs]
            uv = upv_ref[:, cs]
            halo = halo_ref[:, cs]
            gbuf[0:FFN_HALO, :] = jnp.where(first_tile, jnp.zeros_like(halo), halo)
            gbuf[FFN_HALO:FFN_HALO + t, :] = ug
            s1buf[...] = gbuf[FFN_HALO - 1:FFN_HALO - 1 + t, :]
            s2buf[...] = gbuf[FFN_HALO - 2:FFN_HALO - 2 + t, :]
            ug1 = s1buf[...]
            ug2 = s2buf[...]
            gate = (pf_ref[PF_B:PF_B + 1, cs] + ug2 * pf_ref[PF_W0:PF_W0 + 1, cs] + ug1 * pf_ref[PF_W0 + 1:PF_W0 + 2, cs]
                    + ug * pf_ref[PF_W0 + 2:PF_W0 + 3, cs])
            sig = _sigmoid(gate)
            sl = gate * sig
            hm_ref[:, cs] = sl * uv
            dhm = _dot_nt(dyb, wd_ref[cs, :])
            duv = dhm * sl
            dgate = dhm * uv * (sig * (1.0 + gate * (1.0 - sig)))
            vf_ref[VF_B:VF_B + 1, cs] += _colsum(dgate)
            vf_ref[VF_W0:VF_W0 + 1, cs] += _colsum(dgate * ug2)
            vf_ref[VF_W0 + 1:VF_W0 + 2, cs] += _colsum(dgate * ug1)
            vf_ref[VF_W0 + 2:VF_W0 + 3, cs] += _colsum(dgate * ug)
            ebuf[0:t, cs] = dgate
            dug = (ebuf[2:2 + t, cs] * pf_ref[PF_W0:PF_W0 + 1, cs] + ebuf[1:1 + t, cs] * pf_ref[PF_W0 + 1:PF_W0 + 2, cs]
                   + dgate * pf_ref[PF_W0 + 2:PF_W0 + 3, cs])
            ebuf[t:t + FFN_HALO, cs] = ebuf[0:FFN_HALO, cs]
            dugb = dug.astype(BF16)
            duvb = duv.astype(BF16)
            dug_ref[:, cs] = dugb
            duv_ref[:, cs] = duvb
            dx = dx + _dot(dugb, wg_ref[cs, :]) + _dot(duvb, wv_ref[cs, :])
        dx_ref[...] = dx

        if last:
            @pl.when(i == nt - 1)
            def _():
                tot = jnp.sum(vd_ref[VD_LOSS:VD_LOSS + 1, :], axis=1, keepdims=True)
                vd_ref[VD_LOSS:VD_LOSS + 1, :] = jnp.zeros((1, D_MODEL), F32) + tot

    rev = lambda w: pl.BlockSpec((t, w), lambda i: (nt - 1 - i, 0))
    halo_spec = pl.BlockSpec((FFN_HALO, FF_P), lambda i: (jnp.maximum((nt - 1 - i) * hb - 1, 0), 0))
    out_shape = [jax.ShapeDtypeStruct((s, D_MODEL), F32), jax.ShapeDtypeStruct((s, D_MODEL), BF16),
                 jax.ShapeDtypeStruct((s, FF_P), BF16), jax.ShapeDtypeStruct((s, FF_P), BF16),
                 jax.ShapeDtypeStruct((s, FF_P), F32),
                 jax.ShapeDtypeStruct((8, D_MODEL), F32), jax.ShapeDtypeStruct((8, FF_P), F32)]
    return pl.pallas_call(
        body, name="ffn_bwd_d_last" if last else "ffn_bwd_d", grid=(nt,),
        in_specs=[rev(D_MODEL), rev(D_MODEL), rev(1), rev(FF_P), halo_spec, rev(FF_P),
                  _layer_spec((8, D_MODEL), l), _layer_spec((8, FF_P), l),
                  _layer_spec((FF_P, D_MODEL), l, resident=True),
                  _layer_spec((FF_P, D_MODEL), l, 0, resident=True), _layer_spec((FF_P, D_MODEL), l, 1, resident=True)],
        out_specs=[rev(D_MODEL), rev(D_MODEL), rev(FF_P), rev(FF_P), rev(FF_P),
                   _const_spec((8, D_MODEL)), _const_spec((8, FF_P))],
        out_shape=out_shape,
        scratch_shapes=[pltpu.VMEM((t + FFN_HALO, FF_H), F32), pltpu.VMEM((t + FFN_HALO, FF_P), F32),
                        pltpu.VMEM((t, FF_H), F32), pltpu.VMEM((t, FF_H), F32)],
        compiler_params=_cparams(("arbitrary",)),
    )(gz_or_target, xh2, rstd2, upg, upg, upv, pd, pf, w_d, wup_t, wup_t)


def _ffn_bwd_w(xh1, pd, dy, dug, duv, hm, l, t, prev):
    s = xh1.shape[0]
    nt = s // t

    def body(xh_ref, pd_ref, dy_ref, dug_ref, duv_ref, hm_ref, *rest):
        dwup_ref, dwd_ref = rest[-2:]

        @pl.when(pl.program_id(1) == 0)
        def _():
            dwup_ref[...] = jnp.zeros_like(dwup_ref)
            dwd_ref[...] = jnp.zeros_like(dwd_ref)

        xb = (xh_ref[...] * _row(pd_ref, PD_G1) + _row(pd_ref, PD_B1)).astype(BF16)
        dwup_ref[0] += _dot_tn(dug_ref[...], xb)
        dwup_ref[1] += _dot_tn(duv_ref[...], xb)
        dwd_ref[...] += _dot_tn(hm_ref[...].astype(BF16), dy_ref[...])

    tok = lambda w: pl.BlockSpec((t, w), lambda c, i: (i, 0))
    half = pl.BlockSpec((t, FF_H), lambda c, i: (i, c))
    shapes = [jax.ShapeDtypeStruct((DEPTH, 2, FF_P, D_MODEL), F32), jax.ShapeDtypeStruct((DEPTH, FF_P, D_MODEL), F32)]
    extra, aliases = _stacked_out(prev, shapes, None, 6)
    return pl.pallas_call(
        body, name="ffn_bwd_w", grid=(2, nt),
        in_specs=[tok(D_MODEL), pl.BlockSpec((None, 8, D_MODEL), lambda c, i: (l, 0, 0)), tok(D_MODEL), half, half, half] + extra,
        out_specs=[pl.BlockSpec((None, 2, FF_H, D_MODEL), lambda c, i: (l, 0, c, 0)),
                   pl.BlockSpec((None, FF_H, D_MODEL), lambda c, i: (l, c, 0))],
        out_shape=shapes, input_output_aliases=aliases,
        compiler_params=_cparams(("arbitrary", "arbitrary")),
    )(xh1, pd, dy, dug, duv, hm, *(prev or ()))


def _adamw(w, g, m, v, name):
    rows, cols = w.shape
    tr = _row_tile(rows, cols * 4)

    def body(w_ref, g_ref, m_ref, v_ref, d_ref, nm_ref, nv_ref):
        gv = g_ref[...]
        nm = ADAM_B1 * m_ref[...] + (1.0 - ADAM_B1) * gv
        nv = ADAM_B2 * v_ref[...] + (1.0 - ADAM_B2) * (gv * gv)
        m_hat = nm / (1.0 - ADAM_B1 ** ADAM_STEP)
        v_hat = nv / (1.0 - ADAM_B2 ** ADAM_STEP)
        d_ref[...] = -ADAM_LR * (m_hat / (jnp.sqrt(v_hat) + ADAM_EPS) + ADAM_WD * w_ref[...])
        nm_ref[...] = nm
        nv_ref[...] = nv

    blk = pl.BlockSpec((tr, cols), lambda i: (i, 0))
    sh = jax.ShapeDtypeStruct((rows, cols), F32)
    return pl.pallas_call(body, name=name, grid=(rows // tr,), in_specs=[blk] * 4, out_specs=[blk] * 3,
                          out_shape=[sh, sh, sh], compiler_params=_cparams(("arbitrary",)))(w, g, m, v)


def _my_place():
    x, y, c = lax.axis_index("x"), lax.axis_index("y"), lax.axis_index("c")
    chips = [(1 - x, y), (x, 1 - y), (1 - x, 1 - y)]
    return x, y, c, chips


def _all_gather_chips(tensors, name):
    n = len(tensors)
    halves = [a.shape[1] // 2 for a in tensors]

    def body(*refs):
        x_refs, out_refs = refs[:n], refs[n:2 * n]
        send_sems, recv_sems, local_sems = refs[2 * n:]
        x, y, c, chips = _my_place()
        me, sibling, mej = (x, y, c), (x, y, 1 - c), 2 * x + y

        def rows(tn, px, py, pc):
            return out_refs[tn].at[:, 2 * px + py, pl.ds(pc * halves[tn], halves[tn]), :]

        def copy(tn, k, block, to, src=None):
            return pltpu.make_async_remote_copy(
                src_ref=rows(tn, *block) if src is None else src, dst_ref=rows(tn, *block),
                send_sem=send_sems.at[tn, k], recv_sem=recv_sems.at[tn, k], device_id=to, device_id_type=MESH)

        mine_src = [x_refs[tn].at[:, pl.ds(c * halves[tn], halves[tn]), :] for tn in range(n)]
        mine = [pltpu.make_async_copy(mine_src[tn], rows(tn, *me), local_sems.at[tn]) for tn in range(n)]
        first = []
        for j, chip in enumerate(chips):
            first += [copy(tn, 1 + j, me, (*chip, c), src=mine_src[tn]) for tn in range(n)]
        first += [copy(tn, 0, me, sibling, src=mine_src[tn]) for tn in range(n)]
        for cp in first + mine:
            cp.start()
        passed = []
        for j, chip in enumerate(chips):
            for tn in range(n):
                copy(tn, 1 + j, (*chip, c), me).wait_recv()
                fwd = copy(tn, 4 + j, (*chip, c), sibling)
                fwd.start()
                passed.append(fwd)
        for tn in range(n):
            copy(tn, 0, sibling, me).wait_recv()
            for j, chip in enumerate(chips):
                copy(tn, 4 + j, (*chip, 1 - c), me).wait_recv()
        for cp in first + passed:
            cp.wait_send()
        for cp in mine:
            cp.wait()

    return pl.pallas_call(
        body, name=name,
        out_shape=[jax.ShapeDtypeStruct((a.shape[0], N_CHIPS) + a.shape[1:], a.dtype) for a in tensors],
        in_specs=[ANY] * n, out_specs=[ANY] * n,
        scratch_shapes=[pltpu.SemaphoreType.DMA((n, 7)), pltpu.SemaphoreType.DMA((n, 7)), pltpu.SemaphoreType.DMA((n,))],
    )(*tensors)


def _sibling_swap_halves(gs):
    n = len(gs)

    def body(*refs):
        g_refs, out_refs = refs[:n], refs[n:2 * n]
        send_sems, recv_sems = refs[2 * n:]
        x, y, c, _ = _my_place()
        cps = [pltpu.make_async_remote_copy(
            src_ref=g_refs[tn].at[:, :, 1 - c], dst_ref=out_refs[tn], send_sem=send_sems.at[tn], recv_sem=recv_sems.at[tn],
            device_id=(x, y, 1 - c), device_id_type=MESH) for tn in range(n)]
        for cp in cps:
            cp.start()
        for cp in cps:
            cp.wait()

    return pl.pallas_call(
        body, name="rs_sibling",
        out_shape=[jax.ShapeDtypeStruct(g.shape[:2] + g.shape[3:], g.dtype) for g in gs],
        in_specs=[ANY] * n, out_specs=[ANY] * n,
        scratch_shapes=[pltpu.SemaphoreType.DMA((n,)), pltpu.SemaphoreType.DMA((n,))],
    )(*gs)


def _chip_scatter(parts):
    n = len(parts)

    def body(*refs):
        p_refs, out_refs = refs[:n], refs[n:2 * n]
        send_sems, recv_sems = refs[2 * n:]
        x, y, c, chips = _my_place()
        cps = []
        for k, (px, py) in enumerate(chips):
            for tn in range(n):
                cps.append(pltpu.make_async_remote_copy(
                    src_ref=p_refs[tn].at[:, 2 * px + py], dst_ref=out_refs[tn].at[:, k],
                    send_sem=send_sems.at[tn, k], recv_sem=recv_sems.at[tn, k],
                    device_id=(px, py, c), device_id_type=MESH))
        for cp in cps:
            cp.start()
        for cp in cps:
            cp.wait()

    return pl.pallas_call(
        body, name="rs_chips",
        out_shape=[jax.ShapeDtypeStruct((p.shape[0], 3) + p.shape[2:], p.dtype) for p in parts],
        in_specs=[ANY] * n, out_specs=[ANY] * n,
        scratch_shapes=[pltpu.SemaphoreType.DMA((n, 3)), pltpu.SemaphoreType.DMA((n, 3))],
    )(*parts)


def _sibling_gather(fs):
    n = len(fs)

    def body(*refs):
        out_refs = refs[n:2 * n]
        send_sems, recv_sems = refs[2 * n:]
        x, y, c, _ = _my_place()
        snd = [pltpu.make_async_remote_copy(
            src_ref=out_refs[tn].at[:, c], dst_ref=out_refs[tn].at[:, c], send_sem=send_sems.at[tn], recv_sem=recv_sems.at[tn],
            device_id=(x, y, 1 - c), device_id_type=MESH) for tn in range(n)]
        rcv = [pltpu.make_async_remote_copy(
            src_ref=out_refs[tn].at[:, c], dst_ref=out_refs[tn].at[:, 1 - c], send_sem=send_sems.at[tn], recv_sem=recv_sems.at[tn],
            device_id=(x, y, 1 - c), device_id_type=MESH) for tn in range(n)]
        for cp in snd:
            cp.start()
        for tn in range(n):
            rcv[tn].wait_recv()
            snd[tn].wait_send()

    return pl.pallas_call(
        body, name="rs_sibling_gather",
        out_shape=[jax.ShapeDtypeStruct(f.shape, f.dtype) for f in fs],
        in_specs=[ANY] * n, out_specs=[ANY] * n, input_output_aliases={tn: tn for tn in range(n)},
        scratch_shapes=[pltpu.SemaphoreType.DMA((n,)), pltpu.SemaphoreType.DMA((n,))],
    )(*fs)


def _all_gather_devices(xs):
    m_per, n = xs.shape

    def body(x_ref, out_ref, send_sems, recv_sems, local_sem):
        x, y, c, chips = _my_place()
        me, sibling = (x, y, c), (x, y, 1 - c)

        def rows(px, py, pc):
            return out_ref.at[pl.ds((4 * px + 2 * py + pc) * m_per, m_per), :]

        def copy(k, block, to, src=None):
            return pltpu.make_async_remote_copy(
                src_ref=rows(*block) if src is None else src, dst_ref=rows(*block),
                send_sem=send_sems.at[k], recv_sem=recv_sems.at[k], device_id=to, device_id_type=MESH)

        mine = pltpu.make_async_copy(x_ref, rows(*me), local_sem)
        mine.start()
        first = [copy(0, me, sibling, src=x_ref)]
        first += [copy(1 + j, me, (*chip, c), src=x_ref) for j, chip in enumerate(chips)]
        for cp in first:
            cp.start()
        passed = [copy(4 + j, (*chip, c), sibling) for j, chip in enumerate(chips)]
        for j, chip in enumerate(chips):
            copy(1 + j, (*chip, c), me).wait_recv()
            passed[j].start()
        copy(0, sibling, me).wait_recv()
        for j, chip in enumerate(chips):
            copy(4 + j, (*chip, 1 - c), me).wait_recv()
        for cp in first + passed:
            cp.wait_send()
        mine.wait()

    return pl.pallas_call(
        body, name="ag_small_grads", out_shape=jax.ShapeDtypeStruct((8 * m_per, n), xs.dtype),
        in_specs=[pl.BlockSpec(memory_space=pltpu.VMEM)], out_specs=pl.BlockSpec(memory_space=pltpu.VMEM),
        scratch_shapes=[pltpu.SemaphoreType.DMA((7,)), pltpu.SemaphoreType.DMA((7,)), pltpu.SemaphoreType.DMA],
        compiler_params=_cparams(),
    )(xs)


def _add_halves(gs, recvs, place):
    n = len(gs)

    def body(place_ref, *refs):
        g_refs, r_refs, o_refs = refs[:n], refs[n:2 * n], refs[2 * n:]
        for tn in range(n):
            o_refs[tn][...] = (g_refs[tn][...] + r_refs[tn][...]).astype(BF16)

    def gspec(g):
        return pl.BlockSpec((None, None, None) + g.shape[3:], lambda l, j, p: (l, j, p[1], 0, 0))

    def rspec(r):
        return pl.BlockSpec((None, None) + r.shape[2:], lambda l, j, p: (l, j, 0, 0))

    grid_spec = pltpu.PrefetchScalarGridSpec(
        num_scalar_prefetch=1, grid=(DEPTH, N_CHIPS),
        in_specs=[gspec(g) for g in gs] + [rspec(r) for r in recvs], out_specs=[rspec(r) for r in recvs])
    return pl.pallas_call(body, name="rs_add", grid_spec=grid_spec,
                          out_shape=[jax.ShapeDtypeStruct(r.shape, BF16) for r in recvs],
                          compiler_params=_cparams(("arbitrary", "arbitrary")))(place, *gs, *recvs)


def _sum_slots(parts, slots, place):
    n = len(parts)

    def body(place_ref, *refs):
        p_refs, s_refs, o_refs = refs[:n], refs[n:2 * n], refs[2 * n:]
        for tn in range(n):
            acc = p_refs[tn][...].astype(F32)
            for k in range(3):
                acc = acc + s_refs[tn][k].astype(F32)
            o_refs[tn][...] = acc

    def pspec(p):
        return pl.BlockSpec((None, None) + p.shape[2:], lambda l, pl_: (l, pl_[0], 0, 0))

    def sspec(sl):
        return pl.BlockSpec((None,) + sl.shape[1:], lambda l, pl_: (l, 0, 0, 0))

    def ospec(p):
        return pl.BlockSpec((None, None) + p.shape[2:], lambda l, pl_: (l, pl_[1], 0, 0))

    grid_spec = pltpu.PrefetchScalarGridSpec(
        num_scalar_prefetch=1, grid=(DEPTH,),
        in_specs=[pspec(p) for p in parts] + [sspec(sl) for sl in slots], out_specs=[ospec(p) for p in parts])
    return pl.pallas_call(body, name="rs_sum", grid_spec=grid_spec,
                          out_shape=[jax.ShapeDtypeStruct((p.shape[0], 2) + p.shape[2:], F32) for p in parts],
                          compiler_params=_cparams(("arbitrary",)))(place, *parts, *slots)


def _sum_devices(gathered, m_per):
    def body(g_ref, o_ref):
        acc = g_ref[0:m_per, :]
        for d in range(1, 8):
            acc = acc + g_ref[d * m_per:(d + 1) * m_per, :]
        o_ref[...] = acc

    return pl.pallas_call(body, name="small_sum", out_shape=jax.ShapeDtypeStruct((m_per, LANE), F32),
                          compiler_params=_cparams())(gathered)


def _pad_ff_cols(a):
    lead = a.shape[:-1]
    n = a.shape[-1] // FF_Q
    a = a.reshape(*lead, n, FF_Q)
    a = jnp.pad(a, [(0, 0)] * len(lead) + [(0, 0), (0, FF_QP - FF_Q)])
    return a.reshape(*lead, n * FF_QP)


def _unpad_ff_cols(a):
    lead = a.shape[:-1]
    n = a.shape[-1] // FF_QP
    return a.reshape(*lead, n, FF_QP)[..., :FF_Q].reshape(*lead, n * FF_Q)


def _pack_small(parts):
    flat = jnp.concatenate([p.reshape(-1) for p in parts])
    return flat.reshape(-1, LANE)


SMALL_SHAPES = [("conv_a_w", (CONV_K, CONV_W)), ("conv_a_b", (CONV_W,)), ("ln_a_g", (CONV_W,)), ("ln_a_b", (CONV_W,)),
                ("ln_v_g", (GMLP_W,)), ("ln_v_b", (GMLP_W,)), ("w_s", (6, CHUNK, CHUNK)), ("b_s", (6, CHUNK)),
                ("ln1_g", (D_MODEL,)), ("ln1_b", (D_MODEL,)), ("conv_f_w", (FFN_CONV_K, D_FF)), ("conv_f_b", (D_FF,)),
                ("ln2_g", (D_MODEL,)), ("ln2_b", (D_MODEL,))]


def _unpack_small(flat2d):
    flat = flat2d.reshape(DEPTH, -1)
    out, o = {}, 0
    for name, shp in SMALL_SHAPES:
        n = 1
        for d in shp:
            n *= d
        out[name] = flat[:, o:o + n].reshape((DEPTH,) + shp)
        o += n
    return out


def _rows8(rows):
    blk = jnp.stack(rows, axis=1)
    return jnp.pad(blk, ((0, 0), (0, 8 - len(rows)), (0, 0)))


def kernel(x, mem, w_in, conv_a_w, conv_a_b, ln_a_g, ln_a_b, ln_v_g, ln_v_b, w_s, b_s, w_mk, w_mv, w_out, ln1_g, ln1_b, w_up, conv_f_w, conv_f_b, w_down, ln2_g, ln2_b, loss_target, m_w_in, m_conv_a_w, m_conv_a_b, m_ln_a_g, m_ln_a_b, m_ln_v_g, m_ln_v_b, m_w_s, m_b_s, m_w_mk, m_w_mv, m_w_out, m_ln1_g, m_ln1_b, m_w_up, m_conv_f_w, m_conv_f_b, m_w_down, m_ln2_g, m_ln2_b, v_w_in, v_conv_a_w, v_conv_a_b, v_ln_a_g, v_ln_a_b, v_ln_v_g, v_ln_v_b, v_w_s, v_b_s, v_w_mk, v_w_mv, v_w_out, v_ln1_g, v_ln1_b, v_w_up, v_conv_f_w, v_conv_f_b, v_w_down, v_ln2_g, v_ln2_b):
    seq = x.shape[1]
    t_fwd = min(512, seq)
    t_bwd = min(256, seq)
    chip = 2 * lax.axis_index("x") + lax.axis_index("y")
    core = lax.axis_index("c")
    place = jnp.stack([chip, core]).astype(jnp.int32)
    x0 = x[0]
    mem0 = mem[0]
    target = loss_target[0]

    shards = [
        w_in.transpose(0, 2, 1).astype(BF16),
        w_mk.astype(BF16), w_mv.astype(BF16),
        w_out.astype(BF16),
        _pad_ff_cols(w_up).transpose(0, 2, 1).astype(BF16),
        jnp.pad(w_down, ((0, 0), (0, FF_QP - FF_Q), (0, 0))).astype(BF16),
    ]
    g_in, g_mk, g_mv, g_out, g_up, g_down = _all_gather_chips(shards, "ag_weights")
    win_t = g_in.reshape(DEPTH, IN_W, D_MODEL)
    wmk = g_mk.reshape(DEPTH, D_MODEL, XATTN_W)
    wmv = g_mv.reshape(DEPTH, D_MODEL, XATTN_W)
    wout = g_out.reshape(DEPTH, D_MODEL, D_MODEL)
    wup_t = g_up.reshape(DEPTH, 2, FF_P, D_MODEL)
    wdown = g_down.reshape(DEPTH, FF_P, D_MODEL)

    n_ca = conv_a_w.size
    small_w = _pack_small([conv_a_w, conv_f_w, jnp.zeros((2 * 80 * LANE - n_ca - conv_f_w.size,), F32)])[None]
    small_g = _all_gather_chips([small_w], "ag_conv_weights")[0].reshape(N_CHIPS, -1)
    conv_a_full = small_g[:, :n_ca].reshape(N_CHIPS, DEPTH, CONV_K, CONV_W // 4).transpose(1, 2, 0, 3).reshape(DEPTH, CONV_K, CONV_W)
    conv_f_full = small_g[:, n_ca:n_ca + conv_f_w.size].reshape(N_CHIPS, DEPTH, FFN_CONV_K, FF_Q).transpose(1, 2, 0, 3).reshape(DEPTH, FFN_CONV_K, D_FF)

    tril = jnp.tril(jnp.ones((CHUNK, CHUNK), dtype=bool))
    ws_m = jnp.where(tril, w_s, 0.0)
    wst = ws_m.reshape(DEPTH, 3, 2 * CHUNK, CHUNK).astype(BF16)
    wstt = ws_m.transpose(0, 1, 3, 2).reshape(DEPTH, 3, 2 * CHUNK, CHUNK).astype(BF16)
    bst = jnp.repeat(b_s.transpose(0, 2, 1), HEAD_DIM, axis=2)
    conv_w = jnp.pad(conv_a_full, ((0, 0), (0, CONV_HALO - CONV_K), (0, 0)))
    pa = _rows8([conv_a_b, ln_a_g, ln_a_b, ln_v_g, ln_v_b])
    gin = jnp.concatenate([jnp.ones((1, D_MODEL), F32), ln2_g[:DEPTH - 1]], axis=0)
    bin_ = jnp.concatenate([jnp.zeros((1, D_MODEL), F32), ln2_b[:DEPTH - 1]], axis=0)
    pd = _rows8([gin, bin_, ln1_g, ln1_b, ln2_g, ln2_b])
    pf = jnp.concatenate([_pad_ff_cols(conv_f_full), _pad_ff_cols(conv_f_b)[:, None, :],
                          jnp.zeros((DEPTH, 8 - FFN_CONV_K - 1, FF_P), F32)], axis=1)

    acts = []
    xin = x0
    for l in range(DEPTH):
        kt_all, k_all, v_all, vt_all = _kv_fwd(mem0, wmk, wmv, l)
        xh1, rstd1, h, ac, cat = _mixer_fwd(xin, pd, win_t, conv_w, pa, wst, bst, kt_all, v_all, wout, l, t_fwd)
        xh2, rstd2, upg, upv = _ffn_fwd(xh1, pd, wup_t, pf, wdown, l, t_bwd)
        acts.append(dict(xin=xin, kt_all=kt_all, k_all=k_all, vt_all=vt_all, xh1=xh1, rstd1=rstd1, h=h, ac=ac, cat=cat,
                         xh2=xh2, rstd2=rstd2, upg=upg, upv=upv))
        xin = xh2

    small = [None] * DEPTH
    gz = target
    loss_sum = None
    g_ffn = g_mix = g_kv = None
    for l in reversed(range(DEPTH)):
        a = acts[l]
        last = l == DEPTH - 1
        dx1, dy, dug, duv, hm, vd2, vf = _ffn_bwd_d(gz, a["xh2"], a["rstd2"], a["upg"], a["upv"], pd, pf, wdown, wup_t,
                                                    l, t_bwd, last)
        if last:
            loss_sum = vd2[VD_LOSS, 0]
        g_ffn = _ffn_bwd_w(a["xh1"], pd, dy, dug, duv, hm, l, t_fwd, g_ffn)
        dx0, dh, dmix, vd1, va, dcw, dws, dbs, dkt, dv = _mixer_bwd_d(
            dx1, a["xh1"], a["rstd1"], a["h"], a["ac"], pd, conv_w, pa, wst, wstt, bst,
            a["kt_all"], a["k_all"], a["vt_all"], wout, win_t, l, t_bwd)
        g_mix = _mixer_bwd_w(a["xin"], pd, dh, a["cat"], dmix, l, t_fwd, g_mix)
        g_kv = _kv_bwd(mem0, dkt, dv, l, g_kv)
        dws6 = jnp.where(tril, dws.reshape(6, CHUNK, CHUNK), 0.0)
        small[l] = [dcw[:CONV_K], va[VA_CONV_B], va[VA_LNA_G], va[VA_LNA_B], va[VA_LNV_G], va[VA_LNV_B], dws6,
                    dbs[:, :6].T, vd1[VD_LN_G], vd1[VD_LN_B],
                    _unpad_ff_cols(vf[VF_W0:VF_W0 + FFN_CONV_K]), _unpad_ff_cols(vf[VF_B]),
                    vd2[VD_LN_G], vd2[VD_LN_B]]
        gz = dx0
    grad_x = gz[None]

    gw_up_t, gw_down = g_ffn
    gw_in_t, gw_out = g_mix
    gw_mk, gw_mv = g_kv
    tensors = [gw_in_t, gw_mk, gw_mv, gw_out, gw_up_t.reshape(DEPTH, 2 * FF_P, D_MODEL), gw_down]
    g5 = [g.reshape(DEPTH, N_CHIPS, 2, g.shape[1] // (2 * N_CHIPS), g.shape[2]) for g in tensors]
    recvs = _sibling_swap_halves(g5)
    parts = _add_halves(g5, recvs, place)
    slots = _chip_scatter(parts)
    halves = _sum_slots(parts, slots, place)
    red = [f.reshape(DEPTH, -1, f.shape[-1]) for f in _sibling_gather(halves)]
    g_w_in = red[0].transpose(0, 2, 1)
    g_w_mk, g_w_mv, g_w_out = red[1], red[2], red[3]
    g_w_up = _unpad_ff_cols(red[4].transpose(0, 2, 1))
    g_w_down = red[5][:, :FF_Q, :]

    small_local = _pack_small([q for l in range(DEPTH) for q in small[l]])
    m_small = small_local.shape[0]
    small_red = _sum_devices(_all_gather_devices(small_local), m_small)
    sg = _unpack_small(small_red)
    g_conv_a_w = lax.dynamic_slice_in_dim(sg["conv_a_w"], chip * (CONV_W // 4), CONV_W // 4, axis=2)
    g_conv_f_w = lax.dynamic_slice_in_dim(sg["conv_f_w"], chip * FF_Q, FF_Q, axis=2)

    loss = 0.5 / D_MODEL * lax.psum(loss_sum, ("x", "y", "c"))

    grads = dict(w_in=g_w_in, conv_a_w=g_conv_a_w, conv_a_b=sg["conv_a_b"], ln_a_g=sg["ln_a_g"], ln_a_b=sg["ln_a_b"],
                 ln_v_g=sg["ln_v_g"], ln_v_b=sg["ln_v_b"], w_s=sg["w_s"], b_s=sg["b_s"], w_mk=g_w_mk, w_mv=g_w_mv,
                 w_out=g_w_out, ln1_g=sg["ln1_g"], ln1_b=sg["ln1_b"], w_up=g_w_up, conv_f_w=g_conv_f_w,
                 conv_f_b=sg["conv_f_b"], w_down=g_w_down, ln2_g=sg["ln2_g"], ln2_b=sg["ln2_b"])
    weights = dict(w_in=w_in, conv_a_w=conv_a_w, conv_a_b=conv_a_b, ln_a_g=ln_a_g, ln_a_b=ln_a_b, ln_v_g=ln_v_g,
                   ln_v_b=ln_v_b, w_s=w_s, b_s=b_s, w_mk=w_mk, w_mv=w_mv, w_out=w_out, ln1_g=ln1_g, ln1_b=ln1_b,
                   w_up=w_up, conv_f_w=conv_f_w, conv_f_b=conv_f_b, w_down=w_down, ln2_g=ln2_g, ln2_b=ln2_b)
    mom_m = dict(w_in=m_w_in, conv_a_w=m_conv_a_w, conv_a_b=m_conv_a_b, ln_a_g=m_ln_a_g, ln_a_b=m_ln_a_b, ln_v_g=m_ln_v_g,
                 ln_v_b=m_ln_v_b, w_s=m_w_s, b_s=m_b_s, w_mk=m_w_mk, w_mv=m_w_mv, w_out=m_w_out, ln1_g=m_ln1_g,
                 ln1_b=m_ln1_b, w_up=m_w_up, conv_f_w=m_conv_f_w, conv_f_b=m_conv_f_b, w_down=m_w_down, ln2_g=m_ln2_g,
                 ln2_b=m_ln2_b)
    mom_v = dict(w_in=v_w_in, conv_a_w=v_conv_a_w, conv_a_b=v_conv_a_b, ln_a_g=v_ln_a_g, ln_a_b=v_ln_a_b, ln_v_g=v_ln_v_g,
                 ln_v_b=v_ln_v_b, w_s=v_w_s, b_s=v_b_s, w_mk=v_w_mk, w_mv=v_w_mv, w_out=v_w_out, ln1_g=v_ln1_g,
                 ln1_b=v_ln1_b, w_up=v_w_up, conv_f_w=v_conv_f_w, conv_f_b=v_conv_f_b, w_down=v_w_down, ln2_g=v_ln2_g,
                 ln2_b=v_ln2_b)
    names = list(weights)
    big_names = ["w_in", "w_mk", "w_mv", "w_out", "w_up", "w_down"]
    delta, new_m, new_v = {}, {}, {}
    for n in big_names:
        shp = weights[n].shape
        as2d = lambda a: a.reshape(-1, shp[-1])
        d, nm, nv = _adamw(as2d(weights[n]), as2d(grads[n]), as2d(mom_m[n]), as2d(mom_v[n]), "adamw_" + n)
        delta[n], new_m[n], new_v[n] = d.reshape(shp), nm.reshape(shp), nv.reshape(shp)
    small_names = [n for n in names if n not in big_names]
    sizes = [weights[n].size for n in small_names]
    pad = (-sum(sizes)) % (8 * LANE)

    def pack(dct, fill):
        return _pack_small([dct[n] for n in small_names] + [jnp.full((pad,), fill, F32)])

    d, nm, nv = _adamw(pack(weights, 0.0), pack(grads, 0.0), pack(mom_m, 0.0), pack(mom_v, 1.0), "adamw_small")
    o = 0
    for n, sz in zip(small_names, sizes):
        shp = weights[n].shape
        delta[n] = d.reshape(-1)[o:o + sz].reshape(shp)
        new_m[n] = nm.reshape(-1)[o:o + sz].reshape(shp)
        new_v[n] = nv.reshape(-1)[o:o + sz].reshape(shp)
        o += sz

    return (loss, grad_x, *[grads[n] for n in names], *[delta[n] for n in names],
            *[new_m[n] for n in names], *[new_v[n] for n in names])
```

```python
import jax
import jax.numpy as jnp
from jax import lax
from jax.experimental import pallas as pl
from jax.experimental.pallas import tpu as pltpu

F32 = jnp.float32
BF16 = jnp.bfloat16

D_MODEL = 1024
DEPTH = 2
CONV_W = 384
GMLP_W = 384
XATTN_W = 256
XATTN_HEADS = 4
HEAD_DIM = 64
IN_W = 1792
CONV_K = 31
CHUNK = 128
N_MEM = 256
D_FF = 2752
FFN_CONV_K = 3
ALPHA = (2.0 * DEPTH) ** 0.25
LN_EPS = 1e-5
ATT_SCALE = 1.0 / 8.0
ADAM_LR, ADAM_B1, ADAM_B2, ADAM_EPS, ADAM_WD, ADAM_STEP = 0.001, 0.9, 0.999, 1e-08, 0.01, 10

N_CHIPS = 4
FF_Q = D_FF // N_CHIPS
FF_QP = 704
FF_H = 2 * FF_QP
FF_P = 4 * FF_QP
LANE = 128
CONV_HALO = 32
FFN_HALO = 8
BF16_ROWS = 16
VMEM_LIMIT = 60 * 1024 * 1024

MESH = pl.DeviceIdType.MESH
ANY = pl.BlockSpec(memory_space=pl.ANY)


def _cparams(sem=None, vmem=VMEM_LIMIT):
    kw = {"vmem_limit_bytes": vmem}
    if sem is not None:
        kw["dimension_semantics"] = sem
    return pltpu.CompilerParams(**kw)


def _row_tile(rows, row_bytes, limit=2 << 20):
    if rows * row_bytes <= limit:
        return rows
    best = None
    for cand in range(BF16_ROWS, rows, BF16_ROWS):
        if rows % cand == 0 and cand * row_bytes <= limit:
            best = cand
    assert best is not None, (rows, row_bytes)
    return best


def _const_spec(shape):
    nd = len(shape)
    return pl.BlockSpec(shape, lambda *_: (0,) * nd)


def _layer_spec(shape, *lead, resident=False):
    nd = len(shape)
    kw = {"pipeline_mode": pl.Buffered(1)} if resident else {}
    return pl.BlockSpec((None,) * len(lead) + tuple(shape), lambda *_: tuple(lead) + (0,) * nd, **kw)


def _sigmoid(x):
    return jax.nn.sigmoid(x)


def _gelu(x):
    return jax.nn.gelu(x)


def _gelu_grad(x):
    c = 0.7978845608028654
    a = 0.044715
    t = jnp.tanh(c * (x + a * x * x * x))
    return 0.5 * (1.0 + t) + 0.5 * x * (1.0 - t * t) * c * (1.0 + 3.0 * a * x * x)


def _ln_fwd(z):
    mu = jnp.mean(z, axis=-1, keepdims=True)
    zc = z - mu
    var = jnp.mean(zc * zc, axis=-1, keepdims=True)
    rstd = lax.rsqrt(var + LN_EPS)
    return zc * rstd, rstd


def _ln_bwd(dxh, xh, rstd):
    m1 = jnp.mean(dxh, axis=-1, keepdims=True)
    m2 = jnp.mean(dxh * xh, axis=-1, keepdims=True)
    return rstd * (dxh - m1 - xh * m2)


def _colsum(a):
    return jnp.sum(a, axis=0, keepdims=True)


def _dot(a, b):
    return jnp.dot(a, b, preferred_element_type=F32)


def _dot_tn(a, b):
    return lax.dot_general(a, b, (((0,), (0,)), ((), ())), preferred_element_type=F32)


def _dot_nt(a, b):
    return lax.dot_general(a, b, (((1,), (1,)), ((), ())), preferred_element_type=F32)


def _softmax_heads(sc):
    ps = []
    for hd in range(XATTN_HEADS):
        s = sc[:, hd * N_MEM:(hd + 1) * N_MEM]
        e = jnp.exp(s - jnp.max(s, axis=-1, keepdims=True))
        ps.append(e / jnp.sum(e, axis=-1, keepdims=True))
    return jnp.concatenate(ps, axis=1)


def _lane_lo(shape):
    return (lax.broadcasted_iota(jnp.int32, shape, len(shape) - 1) % LANE) < HEAD_DIM


def _spatial_mix(vnb, wst_ref, bst_ref, mix_ref, t):
    lo = _lane_lo((CHUNK, LANE))
    for n in range(t // CHUNK):
        rows = slice(n * CHUNK, (n + 1) * CHUNK)
        for j in range(GMLP_W // LANE):
            cols = slice(j * LANE, (j + 1) * LANE)
            r = _dot(wst_ref[j], vnb[rows, cols])
            mix_ref[rows, cols] = jnp.where(lo, r[:CHUNK], r[CHUNK:]) + bst_ref[:, cols]


def _stacked_out(prev, shapes, specs, n_in):
    extra_specs = [] if prev is None else [ANY] * len(shapes)
    aliases = {} if prev is None else {n_in + k: k for k in range(len(shapes))}
    return extra_specs, aliases


def _kv_fwd(mem, w_mk, w_mv):
    def body(mem_ref, wk_ref, wv_ref, kt_ref, k_ref, v_ref, vt_ref):
        mb = mem_ref[...].astype(BF16)
        k = _dot(mb, wk_ref[...])
        v = _dot(mb, wv_ref[...])
        col = lax.broadcasted_iota(jnp.int32, (N_MEM, XATTN_W), 1) // HEAD_DIM
        ks = [jnp.where(col == hd, k, 0.0) for hd in range(XATTN_HEADS)]
        vs = [jnp.where(col == hd, v, 0.0) for hd in range(XATTN_HEADS)]
        k_ref[...] = jnp.concatenate(ks, axis=0).astype(BF16)
        v_ref[...] = jnp.concatenate(vs, axis=0).astype(BF16)
        kt_ref[...] = jnp.concatenate([x.T for x in ks], axis=1).astype(BF16)
        vt_ref[...] = jnp.concatenate([x.T for x in vs], axis=1).astype(BF16)

    wide = jax.ShapeDtypeStruct((XATTN_W, XATTN_HEADS * N_MEM), BF16)
    tall = jax.ShapeDtypeStruct((XATTN_HEADS * N_MEM, XATTN_W), BF16)
    wspec = _layer_spec((D_MODEL, XATTN_W), 0)
    return pl.pallas_call(body, name="kv_fwd", grid=(1,),
                          in_specs=[_const_spec((N_MEM, D_MODEL)), wspec, wspec],
                          out_specs=[_const_spec(wide.shape), _const_spec(tall.shape), _const_spec(tall.shape),
                                     _const_spec(wide.shape)],
                          out_shape=(wide, tall, tall, wide), compiler_params=_cparams(("arbitrary",)))(mem, w_mk, w_mv)


def _kv_bwd(mem, dkt_all, dv_all, l, prev):
    def body(mem_ref, dkt_ref, dv_ref, *rest):
        gk_ref, gv_ref = rest[-2:]
        col = lax.broadcasted_iota(jnp.int32, (N_MEM, XATTN_W), 1) // HEAD_DIM
        dk = jnp.zeros((N_MEM, XATTN_W), F32)
        dv = jnp.zeros((N_MEM, XATTN_W), F32)
        for hd in range(XATTN_HEADS):
            dk = dk + jnp.where(col == hd, dkt_ref[:, hd * N_MEM:(hd + 1) * N_MEM].T, 0.0)
            dv = dv + jnp.where(col == hd, dv_ref[hd * N_MEM:(hd + 1) * N_MEM, :], 0.0)
        mb = mem_ref[...].astype(BF16)
        gk_ref[...] = _dot_tn(mb, dk.astype(BF16))
        gv_ref[...] = _dot_tn(mb, dv.astype(BF16))

    out = jax.ShapeDtypeStruct((DEPTH, D_MODEL, XATTN_W), F32)
    ospec = _layer_spec((D_MODEL, XATTN_W), l)
    extra, aliases = _stacked_out(prev, [out, out], None, 3)
    return pl.pallas_call(body, name="kv_bwd", grid=(1,),
                          in_specs=[_const_spec((N_MEM, D_MODEL)), _const_spec(dkt_all.shape), _const_spec(dv_all.shape)] + extra,
                          out_specs=[ospec, ospec], out_shape=(out, out), input_output_aliases=aliases,
                          compiler_params=_cparams(("arbitrary",)))(mem, dkt_all, dv_all, *(prev or ()))


def _my_place():
    x, y, c = lax.axis_index("x"), lax.axis_index("y"), lax.axis_index("c")
    chips = [(1 - x, y), (x, 1 - y), (1 - x, 1 - y)]
    return x, y, c, chips


def _side_gather(shard_refs, land_refs, send_sems, recv_sems, local_sems):
    n = len(shard_refs)
    x, y, c, chips = _my_place()
    mej = 2 * x + y

    def remote(tn, k, slot):
        px, py = chips[k]
        return pltpu.make_async_remote_copy(
            src_ref=shard_refs[tn], dst_ref=land_refs[tn].at[slot], send_sem=send_sems.at[tn, k],
            recv_sem=recv_sems.at[tn, k], device_id=(px, py, c), device_id_type=MESH)

    def local(tn):
        return pltpu.make_async_copy(shard_refs[tn], land_refs[tn].at[mej], local_sems.at[tn])

    def start():
        for k in range(3):
            for tn in range(n):
                remote(tn, k, mej).start()
        for tn in range(n):
            local(tn).start()

    def wait():
        for k, (px, py) in enumerate(chips):
            for tn in range(n):
                remote(tn, k, 2 * px + py).wait_recv()
                remote(tn, k, mej).wait_send()
        for tn in range(n):
            local(tn).wait()

    return start, wait


def _side_specs(side):
    side = list(side or ())
    n = len(side)
    shapes = [jax.ShapeDtypeStruct((N_CHIPS,) + a.shape, a.dtype) for a in side]
    scratch = [pltpu.SemaphoreType.DMA((n, 3)), pltpu.SemaphoreType.DMA((n, 3)), pltpu.SemaphoreType.DMA((n,))] if n else []
    return side, [ANY] * n, [ANY] * n, shapes, scratch


PA_CONV_B, PA_LNA_G, PA_LNA_B, PA_LNV_G, PA_LNV_B = 0, 1, 2, 3, 4
PD_GIN, PD_BIN, PD_G1, PD_B1, PD_G2, PD_B2 = 0, 1, 2, 3, 4, 5


def _row(ref, r):
    return ref[r:r + 1, :]


def _mixer_fwd(xin, pd, win_t, conv_w, pa, wst, bst, kt_all, v_all, w_out, l, t, side=None):
    s = xin.shape[0]
    nt = s // t
    side, side_in, side_out, side_shapes, side_scratch = _side_specs(side)
    ns = len(side)

    def body(x_ref, pd_ref, wint_ref, cw_ref, pa_ref, wst_ref, bst_ref, kt_ref, v_ref, wout_ref, *rest):
        shard_refs, rest = rest[:ns], rest[ns:]
        xh_ref, rstd_ref, h_ref, ac_ref, cat_ref = rest[:5]
        land_refs, rest = rest[5:5 + ns], rest[5 + ns:]
        cbuf, mixbuf, zbuf = rest[:3]
        i = pl.program_id(0)
        if ns:
            gather_start, gather_wait = _side_gather(shard_refs, land_refs, *rest[3:])
            pl.when(i == 0)(gather_start)
        x = x_ref[...] * _row(pd_ref, PD_GIN) + _row(pd_ref, PD_BIN)
        h = _dot_nt(x.astype(BF16), wint_ref[...])
        h_ref[...] = h
        a1, a2 = h[:, 0:CONV_W], h[:, CONV_W:2 * CONV_W]
        hu, hv = h[:, 2 * CONV_W:2 * CONV_W + GMLP_W], h[:, 2 * CONV_W + GMLP_W:2 * CONV_W + 2 * GMLP_W]
        q = h[:, IN_W - XATTN_W:]

        @pl.when(i == 0)
        def _():
            cbuf[0:CONV_HALO, :] = jnp.zeros((CONV_HALO, CONV_W), F32)

        cbuf[CONV_HALO:CONV_HALO + t, :] = a1 * _sigmoid(a2)
        ac = jnp.zeros((t, CONV_W), F32) + _row(pa_ref, PA_CONV_B)
        for r in range(8):
            zr = jnp.zeros((t + 8, CONV_W), F32)
            for a in range(4):
                o = 8 * a + r
                if o < CONV_K:
                    k = CONV_K - 1 - o
                    zr = zr + cbuf[CONV_HALO - 8 - 8 * a:CONV_HALO - 8 - 8 * a + t + 8, :] * cw_ref[k:k + 1, :]
            if r == 0:
                ac = ac + zr[8:, :]
            else:
                zbuf[...] = zr
                ac = ac + zbuf[8 - r:8 - r + t, :]
        ac_ref[...] = ac
        cbuf[0:CONV_HALO, :] = cbuf[t:t + CONV_HALO, :]
        xh_a, _ = _ln_fwd(ac)
        an = xh_a * _row(pa_ref, PA_LNA_G) + _row(pa_ref, PA_LNA_B)
        a = an * _sigmoid(an)

        u = _gelu(hu)
        xh_v, _ = _ln_fwd(_gelu(hv))
        vn = xh_v * _row(pa_ref, PA_LNV_G) + _row(pa_ref, PA_LNV_B)
        _spatial_mix(vn.astype(BF16), wst_ref, bst_ref, mixbuf, t)
        g = u * mixbuf[...]

        p = _softmax_heads(_dot(q.astype(BF16), kt_ref[...]) * ATT_SCALE)
        o = _dot(p.astype(BF16), v_ref[...])

        cat = jnp.concatenate([a, g, o], axis=1).astype(BF16)
        cat_ref[...] = cat
        z = ALPHA * x + _dot(cat, wout_ref[...])
        xh, rstd = _ln_fwd(z)
        xh_ref[...] = xh
        rstd_ref[...] = rstd
        if ns:
            pl.when(i == nt - 1)(gather_wait)

    tok = lambda w: pl.BlockSpec((t, w), lambda i: (i, 0))
    return pl.pallas_call(
        body, name="mixer_fwd_gather" if ns else "mixer_fwd", grid=(nt,),
        in_specs=[tok(D_MODEL), _layer_spec((8, D_MODEL), l), _layer_spec((IN_W, D_MODEL), 0, resident=True),
                  _layer_spec((CONV_HALO, CONV_W), l), _layer_spec((8, CONV_W), l),
                  _layer_spec((3, 2 * CHUNK, CHUNK), l), _layer_spec((CHUNK, GMLP_W), l),
                  _const_spec((XATTN_W, XATTN_HEADS * N_MEM)), _const_spec((XATTN_HEADS * N_MEM, XATTN_W)),
                  _layer_spec((D_MODEL, D_MODEL), 0, resident=True)] + side_in,
        out_specs=[tok(D_MODEL), tok(1), tok(IN_W), tok(CONV_W), tok(D_MODEL)] + side_out,
        out_shape=[jax.ShapeDtypeStruct((s, D_MODEL), F32), jax.ShapeDtypeStruct((s, 1), F32),
                   jax.ShapeDtypeStruct((s, IN_W), F32), jax.ShapeDtypeStruct((s, CONV_W), F32),
                   jax.ShapeDtypeStruct((s, D_MODEL), BF16)] + side_shapes,
        scratch_shapes=[pltpu.VMEM((t + CONV_HALO, CONV_W), F32), pltpu.VMEM((t, GMLP_W), F32),
                        pltpu.VMEM((t + 8, CONV_W), F32)] + side_scratch,
        compiler_params=_cparams(("arbitrary",)),
    )(xin, pd, win_t, conv_w, pa, wst, bst, kt_all, v_all, w_out, *side)


VD_LN_G, VD_LN_B, VD_LOSS = 0, 1, 2
VA_CONV_B, VA_LNA_G, VA_LNA_B, VA_LNV_G, VA_LNV_B = 0, 1, 2, 3, 4


def _mixer_bwd_d(gz, xh1, rstd1, h, ac, pd, conv_w, pa, wst, wstt, bst, kt_all, k_all, vt_all, w_out, win_t, l, t):
    s = gz.shape[0]
    nt = s // t

    def body(gz_ref, xh_ref, rstd_ref, h_ref, ac_ref, pd_ref, cw_ref, pa_ref, wst_ref, wstt_ref, bst_ref,
             kt_ref, k_ref, vt_ref, wout_ref, wint_ref,
             dx_ref, dh_ref, dmix_ref, vd_ref, va_ref, dcw_ref, dws_ref, dbs_ref, dkt_ref, dv_ref,
             ebuf, mixbuf, dvnbuf, dbsacc, erbuf):
        i = pl.program_id(0)

        @pl.when(i == 0)
        def _():
            vd_ref[...] = jnp.zeros_like(vd_ref)
            va_ref[...] = jnp.zeros_like(va_ref)
            dcw_ref[...] = jnp.zeros_like(dcw_ref)
            dws_ref[...] = jnp.zeros_like(dws_ref)
            dbs_ref[...] = jnp.zeros_like(dbs_ref)
            dkt_ref[...] = jnp.zeros_like(dkt_ref)
            dv_ref[...] = jnp.zeros_like(dv_ref)
            dbsacc[...] = jnp.zeros_like(dbsacc)
            ebuf[t:t + CONV_HALO, :] = jnp.zeros((CONV_HALO, CONV_W), F32)

        gz_v = gz_ref[...]
        xh = xh_ref[...]
        vd_ref[VD_LN_G:VD_LN_G + 1, :] += _colsum(gz_v * xh)
        vd_ref[VD_LN_B:VD_LN_B + 1, :] += _colsum(gz_v)
        dz = _ln_bwd(gz_v * _row(pd_ref, PD_G1), xh, rstd_ref[...])
        dzb = dz.astype(BF16)
        dmix_ref[...] = dzb
        dcat = _dot_nt(dzb, wout_ref[...])
        d_a, d_g, d_o = dcat[:, 0:CONV_W], dcat[:, CONV_W:CONV_W + GMLP_W], dcat[:, CONV_W + GMLP_W:]

        h = h_ref[...]
        a1, a2 = h[:, 0:CONV_W], h[:, CONV_W:2 * CONV_W]
        hu, hv = h[:, 2 * CONV_W:2 * CONV_W + GMLP_W], h[:, 2 * CONV_W + GMLP_W:2 * CONV_W + 2 * GMLP_W]
        q = h[:, IN_W - XATTN_W:]

        xh_a, rstd_a = _ln_fwd(ac_ref[...])
        an = xh_a * _row(pa_ref, PA_LNA_G) + _row(pa_ref, PA_LNA_B)
        sig = _sigmoid(an)
        d_an = d_a * (sig * (1.0 + an * (1.0 - sig)))
        va_ref[VA_LNA_G:VA_LNA_G + 1, :] += _colsum(d_an * xh_a)
        va_ref[VA_LNA_B:VA_LNA_B + 1, :] += _colsum(d_an)
        dac = _ln_bwd(d_an * _row(pa_ref, PA_LNA_G), xh_a, rstd_a)
        va_ref[VA_CONV_B:VA_CONV_B + 1, :] += _colsum(dac)
        ebuf[0:t, :] = dac
        sg = _sigmoid(a2)
        glu = a1 * sg
        dglu = jnp.zeros((t, CONV_W), F32)
        for r in range(8):
            if r > 0:
                erbuf[...] = ebuf[r:r + t + 24, :]
            src = ebuf if r == 0 else erbuf
            for a in range(4):
                o = 8 * a + r
                if o < CONV_K:
                    k = CONV_K - 1 - o
                    ek = src[8 * a:8 * a + t, :]
                    dglu = dglu + ek * cw_ref[k:k + 1, :]
                    dcw_ref[k:k + 1, :] += _colsum(ek * glu)
        ebuf[t:t + CONV_HALO, :] = ebuf[0:CONV_HALO, :]
        da1 = dglu * sg
        da2 = dglu * a1 * sg * (1.0 - sg)

        u = _gelu(hu)
        xh_v, rstd_v = _ln_fwd(_gelu(hv))
        vn = xh_v * _row(pa_ref, PA_LNV_G) + _row(pa_ref, PA_LNV_B)
        vnb = vn.astype(BF16)
        _spatial_mix(vnb, wst_ref, bst_ref, mixbuf, t)
        dhu = d_g * mixbuf[...] * _gelu_grad(hu)
        dm = d_g * u
        dmb = dm.astype(BF16)
        lo = _lane_lo((CHUNK, LANE))
        for n in range(t // CHUNK):
            rows = slice(n * CHUNK, (n + 1) * CHUNK)
            dbsacc[...] += dm[rows, :]
            for j in range(GMLP_W // LANE):
                cols = slice(j * LANE, (j + 1) * LANE)
                dm_blk = dmb[rows, cols]
                r = _dot(wstt_ref[j], dm_blk)
                dvnbuf[rows, cols] = jnp.where(lo, r[:CHUNK], r[CHUNK:])
                zero = jnp.zeros_like(dm_blk)
                st = jnp.concatenate([jnp.where(lo, dm_blk, zero), jnp.where(lo, zero, dm_blk)], axis=0)
                dws_ref[j] += _dot_nt(st, vnb[rows, cols])
        dvn = dvnbuf[...]
        va_ref[VA_LNV_G:VA_LNV_G + 1, :] += _colsum(dvn * xh_v)
        va_ref[VA_LNV_B:VA_LNV_B + 1, :] += _colsum(dvn)
        dhv = _ln_bwd(dvn * _row(pa_ref, PA_LNV_G), xh_v, rstd_v) * _gelu_grad(hv)

        qb = q.astype(BF16)
        p = _softmax_heads(_dot(qb, kt_ref[...]) * ATT_SCALE)
        dob = d_o.astype(BF16)
        dp = _dot(dob, vt_ref[...])
        dss = []
        for hd in range(XATTN_HEADS):
            cs = slice(hd * N_MEM, (hd + 1) * N_MEM)
            ph, dph = p[:, cs], dp[:, cs]
            dss.append(ph * (dph - jnp.sum(ph * dph, axis=-1, keepdims=True)) * ATT_SCALE)
        dsb = jnp.concatenate(dss, axis=1).astype(BF16)
        dq = _dot(dsb, k_ref[...])
        dkt_ref[...] += _dot_tn(qb, dsb)
        dv_ref[...] += _dot_tn(p.astype(BF16), dob)

        dhb = jnp.concatenate([da1, da2, dhu, dhv, dq], axis=1).astype(BF16)
        dh_ref[...] = dhb
        dx_ref[...] = ALPHA * dz + _dot(dhb, wint_ref[...])

        @pl.when(i == nt - 1)
        def _():
            acc = dbsacc[...]
            head = lax.broadcasted_iota(jnp.int32, (CHUNK, GMLP_W), 1) // HEAD_DIM
            lane = lax.broadcasted_iota(jnp.int32, (CHUNK, LANE), 1)
            out = jnp.zeros((CHUNK, LANE), F32)
            for hd in range(GMLP_W // HEAD_DIM):
                sh = jnp.sum(jnp.where(head == hd, acc, 0.0), axis=1, keepdims=True)
                out = out + jnp.where(lane == hd, sh, 0.0)
            dbs_ref[...] = out

    rev = lambda w: pl.BlockSpec((t, w), lambda i: (nt - 1 - i, 0))
    out_shape = [
        jax.ShapeDtypeStruct((s, D_MODEL), F32), jax.ShapeDtypeStruct((s, IN_W), BF16),
        jax.ShapeDtypeStruct((s, D_MODEL), BF16),
        jax.ShapeDtypeStruct((8, D_MODEL), F32), jax.ShapeDtypeStruct((8, CONV_W), F32),
        jax.ShapeDtypeStruct((CONV_HALO, CONV_W), F32), jax.ShapeDtypeStruct((3, 2 * CHUNK, CHUNK), F32),
        jax.ShapeDtypeStruct((CHUNK, LANE), F32),
        jax.ShapeDtypeStruct((XATTN_W, XATTN_HEADS * N_MEM), F32), jax.ShapeDtypeStruct((XATTN_HEADS * N_MEM, XATTN_W), F32),
    ]
    out_specs = [rev(D_MODEL), rev(IN_W), rev(D_MODEL)] + [_const_spec(o.shape) for o in out_shape[3:]]
    return pl.pallas_call(
        body, name="mixer_bwd_d", grid=(nt,),
        in_specs=[rev(D_MODEL), rev(D_MODEL), rev(1), rev(IN_W), rev(CONV_W),
                  _layer_spec((8, D_MODEL), l), _layer_spec((CONV_HALO, CONV_W), l), _layer_spec((8, CONV_W), l),
                  _layer_spec((3, 2 * CHUNK, CHUNK), l), _layer_spec((3, 2 * CHUNK, CHUNK), l),
                  _layer_spec((CHUNK, GMLP_W), l),
                  _const_spec((XATTN_W, XATTN_HEADS * N_MEM)), _const_spec((XATTN_HEADS * N_MEM, XATTN_W)),
                  _const_spec((XATTN_W, XATTN_HEADS * N_MEM)),
                  _layer_spec((D_MODEL, D_MODEL), 0, resident=True), _layer_spec((IN_W, D_MODEL), 0, resident=True)],
        out_specs=out_specs, out_shape=out_shape,
        scratch_shapes=[pltpu.VMEM((t + CONV_HALO, CONV_W), F32), pltpu.VMEM((t, GMLP_W), F32),
                        pltpu.VMEM((t, GMLP_W), F32), pltpu.VMEM((CHUNK, GMLP_W), F32),
                        pltpu.VMEM((t + 24, CONV_W), F32)],
        compiler_params=_cparams(("arbitrary",)),
    )(gz, xh1, rstd1, h, ac, pd, conv_w, pa, wst, wstt, bst, kt_all, k_all, vt_all, w_out, win_t)


def _mixer_bwd_w(xin, pd, dh, cat, dmix, l, t, prev):
    s = xin.shape[0]
    nt = s // t

    def body(x_ref, pd_ref, dh_ref, cat_ref, dmix_ref, *rest):
        dwin_ref, dwout_ref = rest[-2:]

        @pl.when(pl.program_id(0) == 0)
        def _():
            dwin_ref[...] = jnp.zeros_like(dwin_ref)
            dwout_ref[...] = jnp.zeros_like(dwout_ref)

        xb = (x_ref[...] * _row(pd_ref, PD_GIN) + _row(pd_ref, PD_BIN)).astype(BF16)
        dwin_ref[...] += _dot_tn(dh_ref[...], xb)
        dwout_ref[...] += _dot_tn(cat_ref[...], dmix_ref[...])

    tok = lambda w: pl.BlockSpec((t, w), lambda i: (i, 0))
    shapes = [jax.ShapeDtypeStruct((DEPTH, IN_W, D_MODEL), F32), jax.ShapeDtypeStruct((DEPTH, D_MODEL, D_MODEL), F32)]
    extra, aliases = _stacked_out(prev, shapes, None, 5)
    return pl.pallas_call(
        body, name="mixer_bwd_w", grid=(nt,),
        in_specs=[tok(D_MODEL), _layer_spec((8, D_MODEL), l), tok(IN_W), tok(D_MODEL), tok(D_MODEL)] + extra,
        out_specs=[_layer_spec((IN_W, D_MODEL), l), _layer_spec((D_MODEL, D_MODEL), l)],
        out_shape=shapes, input_output_aliases=aliases,
        compiler_params=_cparams(("arbitrary",)),
    )(xin, pd, dh, cat, dmix, *(prev or ()))


PF_W0, PF_B = 0, 3


def _ffn_fwd(xh1, pd, wup_t, pf, w_d, l, t, side=None):
    s = xh1.shape[0]
    nt = s // t
    side, side_in, side_out, side_shapes, side_scratch = _side_specs(side)
    ns = len(side)

    def body(xh_ref, pd_ref, wg_ref, wv_ref, pf_ref, wd_ref, *rest):
        shard_refs, rest = rest[:ns], rest[ns:]
        xh2_ref, rstd_ref, upg_ref, upv_ref = rest[:4]
        land_refs, rest = rest[4:4 + ns], rest[4 + ns:]
        fbuf = rest[0]
        i = pl.program_id(0)
        if ns:
            gather_start, gather_wait = _side_gather(shard_refs, land_refs, *rest[1:])
            pl.when(i == 0)(gather_start)

        @pl.when(i == 0)
        def _():
            fbuf[0:FFN_HALO, :] = jnp.zeros((FFN_HALO, FF_P), F32)

        x1 = xh_ref[...] * _row(pd_ref, PD_G1) + _row(pd_ref, PD_B1)
        xb = x1.astype(BF16)
        y = jnp.zeros((t, D_MODEL), F32)
        for hf in range(2):
            cs = slice(hf * FF_H, (hf + 1) * FF_H)
            ug = _dot_nt(xb, wg_ref[cs, :])
            uv = _dot_nt(xb, wv_ref[cs, :])
            upg_ref[:, cs] = ug
            upv_ref[:, cs] = uv
            fbuf[FFN_HALO:FFN_HALO + t, cs] = ug
            gate = jnp.zeros((t, FF_H), F32) + pf_ref[PF_B:PF_B + 1, cs]
            for k in range(FFN_CONV_K):
                off = FFN_HALO - (FFN_CONV_K - 1) + k
                gate = gate + fbuf[off:off + t, cs] * pf_ref[PF_W0 + k:PF_W0 + k + 1, cs]
            fbuf[0:FFN_HALO, cs] = fbuf[t:t + FFN_HALO, cs]
            hm = gate * _sigmoid(gate) * uv
            y = y + _dot(hm.astype(BF16), wd_ref[cs, :])
        xh2, rstd = _ln_fwd(ALPHA * x1 + y)
        xh2_ref[...] = xh2
        rstd_ref[...] = rstd
        if ns:
            pl.when(i == nt - 1)(gather_wait)

    tok = lambda w: pl.BlockSpec((t, w), lambda i: (i, 0))
    return pl.pallas_call(
        body, name="ffn_fwd_gather" if ns else "ffn_fwd", grid=(nt,),
        in_specs=[tok(D_MODEL), _layer_spec((8, D_MODEL), l),
                  _layer_spec((FF_P, D_MODEL), 0, 0, resident=True), _layer_spec((FF_P, D_MODEL), 0, 1, resident=True),
                  _layer_spec((8, FF_P), l), _layer_spec((FF_P, D_MODEL), 0, resident=True)] + side_in,
        out_specs=[tok(D_MODEL), tok(1), tok(FF_P), tok(FF_P)] + side_out,
        out_shape=[jax.ShapeDtypeStruct((s, D_MODEL), F32), jax.ShapeDtypeStruct((s, 1), F32),
                   jax.ShapeDtypeStruct((s, FF_P), F32), jax.ShapeDtypeStruct((s, FF_P), F32)] + side_shapes,
        scratch_shapes=[pltpu.VMEM((t + FFN_HALO, FF_P), F32)] + side_scratch,
        compiler_params=_cparams(("arbitrary",)),
    )(xh1, pd, wup_t, wup_t, pf, w_d, *side)


VF_W0, VF_B = 0, 3


def _ffn_bwd_d(gz_or_target, xh2, rstd2, upg, upv, pd, pf, w_d, wup_t, l, t, last):
    s = xh2.shape[0]
    nt = s // t
    hb = t // FFN_HALO

    def body(gz_ref, xh2_ref, rstd_ref, upg_ref, halo_ref, upv_ref, pd_ref, pf_ref, wd_ref, wg_ref, wv_ref,
             dx_ref, dy_ref, dug_ref, duv_ref, hm_ref, vd_ref, vf_ref, gbuf, ebuf, s1buf, s2buf):
        i = pl.program_id(0)
        first_tile = i == nt - 1

        @pl.when(i == 0)
        def _():
            vd_ref[...] = jnp.zeros_like(vd_ref)
            vf_ref[...] = jnp.zeros_like(vf_ref)
            ebuf[t:t + FFN_HALO, :] = jnp.zeros((FFN_HALO, FF_P), F32)

        xh2_v = xh2_ref[...]
        if last:
            diff = xh2_v * _row(pd_ref, PD_G2) + _row(pd_ref, PD_B2) - gz_ref[...]
            vd_ref[VD_LOSS:VD_LOSS + 1, :] += _colsum(diff * diff)
            gz_v = diff * (1.0 / D_MODEL)
        else:
            gz_v = gz_ref[...]
        vd_ref[VD_LN_G:VD_LN_G + 1, :] += _colsum(gz_v * xh2_v)
        vd_ref[VD_LN_B:VD_LN_B + 1, :] += _colsum(gz_v)
        dz = _ln_bwd(gz_v * _row(pd_ref, PD_G2), xh2_v, rstd_ref[...])
        dyb = dz.astype(BF16)
        dy_ref[...] = dyb
        dx = ALPHA * dz
        for hf in range(2):
            cs = slice(hf * FF_H, (hf + 1) * FF_H)
            ug = upg_ref[:, cs]
            uv = upv_ref[:, cs]
            halo = halo_ref[:, cs]
            gbuf[0:FFN_HALO, :] = jnp.where(first_tile, jnp.zeros_like(halo), halo)
            gbuf[FFN_HALO:FFN_HALO + t, :] = ug
            s1buf[...] = gbuf[FFN_HALO - 1:FFN_HALO - 1 + t, :]
            s2buf[...] = gbuf[FFN_HALO - 2:FFN_HALO - 2 + t, :]
            ug1 = s1buf[...]
            ug2 = s2buf[...]
            gate = (pf_ref[PF_B:PF_B + 1, cs] + ug2 * pf_ref[PF_W0:PF_W0 + 1, cs] + ug1 * pf_ref[PF_W0 + 1:PF_W0 + 2, cs]
                    + ug * pf_ref[PF_W0 + 2:PF_W0 + 3, cs])
            sig = _sigmoid(gate)
            sl = gate * sig
            hm_ref[:, cs] = sl * uv
            dhm = _dot_nt(dyb, wd_ref[cs, :])
            duv = dhm * sl
            dgate = dhm * uv * (sig * (1.0 + gate * (1.0 - sig)))
            vf_ref[VF_B:VF_B + 1, cs] += _colsum(dgate)
            vf_ref[VF_W0:VF_W0 + 1, cs] += _colsum(dgate * ug2)
            vf_ref[VF_W0 + 1:VF_W0 + 2, cs] += _colsum(dgate * ug1)
            vf_ref[VF_W0 + 2:VF_W0 + 3, cs] += _colsum(dgate * ug)
            ebuf[0:t, cs] = dgate
            dug = (ebuf[2:2 + t, cs] * pf_ref[PF_W0:PF_W0 + 1, cs] + ebuf[1:1 + t, cs] * pf_ref[PF_W0 + 1:PF_W0 + 2, cs]
                   + dgate * pf_ref[PF_W0 + 2:PF_W0 + 3, cs])
            ebuf[t:t + FFN_HALO, cs] = ebuf[0:FFN_HALO, cs]
            dugb = dug.astype(BF16)
            duvb = duv.astype(BF16)
            dug_ref[:, cs] = dugb
            duv_ref[:, cs] = duvb
            dx = dx + _dot(dugb, wg_ref[cs, :]) + _dot(duvb, wv_ref[cs, :])
        dx_ref[...] = dx

        if last:
            @pl.when(i == nt - 1)
            def _():
                tot = jnp.sum(vd_ref[VD_LOSS:VD_LOSS + 1, :], axis=1, keepdims=True)
                vd_ref[VD_LOSS:VD_LOSS + 1, :] = jnp.zeros((1, D_MODEL), F32) + tot

    rev = lambda w: pl.BlockSpec((t, w), lambda i: (nt - 1 - i, 0))
    halo_spec = pl.BlockSpec((FFN_HALO, FF_P), lambda i: (jnp.maximum((nt - 1 - i) * hb - 1, 0), 0))
    out_shape = [jax.ShapeDtypeStruct((s, D_MODEL), F32), jax.ShapeDtypeStruct((s, D_MODEL), BF16),
                 jax.ShapeDtypeStruct((s, FF_P), BF16), jax.ShapeDtypeStruct((s, FF_P), BF16),
                 jax.ShapeDtypeStruct((s, FF_P), F32),
                 jax.ShapeDtypeStruct((8, D_MODEL), F32), jax.ShapeDtypeStruct((8, FF_P), F32)]
    return pl.pallas_call(
        body, name="ffn_bwd_d_last" if last else "ffn_bwd_d", grid=(nt,),
        in_specs=[rev(D_MODEL), rev(D_MODEL), rev(1), rev(FF_P), halo_spec, rev(FF_P),
                  _layer_spec((8, D_MODEL), l), _layer_spec((8, FF_P), l),
                  _layer_spec((FF_P, D_MODEL), 0, resident=True),
                  _layer_spec((FF_P, D_MODEL), 0, 0, resident=True), _layer_spec((FF_P, D_MODEL), 0, 1, resident=True)],
        out_specs=[rev(D_MODEL), rev(D_MODEL), rev(FF_P), rev(FF_P), rev(FF_P),
                   _const_spec((8, D_MODEL)), _const_spec((8, FF_P))],
        out_shape=out_shape,
        scratch_shapes=[pltpu.VMEM((t + FFN_HALO, FF_H), F32), pltpu.VMEM((t + FFN_HALO, FF_P), F32),
                        pltpu.VMEM((t, FF_H), F32), pltpu.VMEM((t, FF_H), F32)],
        compiler_params=_cparams(("arbitrary",)),
    )(gz_or_target, xh2, rstd2, upg, upg, upv, pd, pf, w_d, wup_t, wup_t)


def _ffn_bwd_w(xh1, pd, dy, dug, duv, hm, l, t, prev):
    s = xh1.shape[0]
    nt = s // t

    def body(xh_ref, pd_ref, dy_ref, dug_ref, duv_ref, hm_ref, *rest):
        dwup_ref, dwd_ref = rest[-2:]

        @pl.when(pl.program_id(1) == 0)
        def _():
            dwup_ref[...] = jnp.zeros_like(dwup_ref)
            dwd_ref[...] = jnp.zeros_like(dwd_ref)

        xb = (xh_ref[...] * _row(pd_ref, PD_G1) + _row(pd_ref, PD_B1)).astype(BF16)
        dwup_ref[0] += _dot_tn(dug_ref[...], xb)
        dwup_ref[1] += _dot_tn(duv_ref[...], xb)
        dwd_ref[...] += _dot_tn(hm_ref[...].astype(BF16), dy_ref[...])

    tok = lambda w: pl.BlockSpec((t, w), lambda c, i: (i, 0))
    half = pl.BlockSpec((t, FF_H), lambda c, i: (i, c))
    shapes = [jax.ShapeDtypeStruct((DEPTH, 2, FF_P, D_MODEL), F32), jax.ShapeDtypeStruct((DEPTH, FF_P, D_MODEL), F32)]
    extra, aliases = _stacked_out(prev, shapes, None, 6)
    return pl.pallas_call(
        body, name="ffn_bwd_w", grid=(2, nt),
        in_specs=[tok(D_MODEL), pl.BlockSpec((None, 8, D_MODEL), lambda c, i: (l, 0, 0)), tok(D_MODEL), half, half, half] + extra,
        out_specs=[pl.BlockSpec((None, 2, FF_H, D_MODEL), lambda c, i: (l, 0, c, 0)),
                   pl.BlockSpec((None, FF_H, D_MODEL), lambda c, i: (l, c, 0))],
        out_shape=shapes, input_output_aliases=aliases,
        compiler_params=_cparams(("arbitrary", "arbitrary")),
    )(xh1, pd, dy, dug, duv, hm, *(prev or ()))


def _adamw(w, g, m, v, name):
    rows, cols = w.shape
    tr = _row_tile(rows, cols * 4)

    def body(w_ref, g_ref, m_ref, v_ref, d_ref, nm_ref, nv_ref):
        gv = g_ref[...]
        nm = ADAM_B1 * m_ref[...] + (1.0 - ADAM_B1) * gv
        nv = ADAM_B2 * v_ref[...] + (1.0 - ADAM_B2) * (gv * gv)
        m_hat = nm / (1.0 - ADAM_B1 ** ADAM_STEP)
        v_hat = nv / (1.0 - ADAM_B2 ** ADAM_STEP)
        d_ref[...] = -ADAM_LR * (m_hat / (jnp.sqrt(v_hat) + ADAM_EPS) + ADAM_WD * w_ref[...])
        nm_ref[...] = nm
        nv_ref[...] = nv

    blk = pl.BlockSpec((tr, cols), lambda i: (i, 0))
    sh = jax.ShapeDtypeStruct((rows, cols), F32)
    return pl.pallas_call(body, name=name, grid=(rows // tr,), in_specs=[blk] * 4, out_specs=[blk] * 3,
                          out_shape=[sh, sh, sh], compiler_params=_cparams(("arbitrary",)))(w, g, m, v)


def _all_gather_chips(tensors, name):
    n = len(tensors)
    halves = [a.shape[1] // 2 for a in tensors]

    def body(*refs):
        x_refs, out_refs = refs[:n], refs[n:2 * n]
        send_sems, recv_sems, local_sems = refs[2 * n:]
        x, y, c, chips = _my_place()
        me, sibling, mej = (x, y, c), (x, y, 1 - c), 2 * x + y

        def rows(tn, px, py, pc):
            return out_refs[tn].at[:, 2 * px + py, pl.ds(pc * halves[tn], halves[tn]), :]

        def copy(tn, k, block, to, src=None):
            return pltpu.make_async_remote_copy(
                src_ref=rows(tn, *block) if src is None else src, dst_ref=rows(tn, *block),
                send_sem=send_sems.at[tn, k], recv_sem=recv_sems.at[tn, k], device_id=to, device_id_type=MESH)

        mine_src = [x_refs[tn].at[:, pl.ds(c * halves[tn], halves[tn]), :] for tn in range(n)]
        mine = [pltpu.make_async_copy(mine_src[tn], rows(tn, *me), local_sems.at[tn]) for tn in range(n)]
        first = []
        for j, chip in enumerate(chips):
            first += [copy(tn, 1 + j, me, (*chip, c), src=mine_src[tn]) for tn in range(n)]
        first += [copy(tn, 0, me, sibling, src=mine_src[tn]) for tn in range(n)]
        for cp in first + mine:
            cp.start()
        passed = []
        for j, chip in enumerate(chips):
            for tn in range(n):
                copy(tn, 1 + j, (*chip, c), me).wait_recv()
                fwd = copy(tn, 4 + j, (*chip, c), sibling)
                fwd.start()
                passed.append(fwd)
        for tn in range(n):
            copy(tn, 0, sibling, me).wait_recv()
            for j, chip in enumerate(chips):
                copy(tn, 4 + j, (*chip, 1 - c), me).wait_recv()
        for cp in first + passed:
            cp.wait_send()
        for cp in mine:
            cp.wait()

    return pl.pallas_call(
        body, name=name,
        out_shape=[jax.ShapeDtypeStruct((a.shape[0], N_CHIPS) + a.shape[1:], a.dtype) for a in tensors],
        in_specs=[ANY] * n, out_specs=[ANY] * n,
        scratch_shapes=[pltpu.SemaphoreType.DMA((n, 7)), pltpu.SemaphoreType.DMA((n, 7)), pltpu.SemaphoreType.DMA((n,))],
    )(*tensors)


def _sibling_swap_halves(gs):
    n = len(gs)

    def body(*refs):
        g_refs, out_refs = refs[:n], refs[n:2 * n]
        send_sems, recv_sems = refs[2 * n:]
        x, y, c, _ = _my_place()
        cps = [pltpu.make_async_remote_copy(
            src_ref=g_refs[tn].at[:, :, 1 - c], dst_ref=out_refs[tn], send_sem=send_sems.at[tn], recv_sem=recv_sems.at[tn],
            device_id=(x, y, 1 - c), device_id_type=MESH) for tn in range(n)]
        for cp in cps:
            cp.start()
        for cp in cps:
            cp.wait()

    return pl.pallas_call(
        body, name="rs_sibling",
        out_shape=[jax.ShapeDtypeStruct(g.shape[:2] + g.shape[3:], g.dtype) for g in gs],
        in_specs=[ANY] * n, out_specs=[ANY] * n,
        scratch_shapes=[pltpu.SemaphoreType.DMA((n,)), pltpu.SemaphoreType.DMA((n,))],
    )(*gs)


def _chip_scatter(parts):
    n = len(parts)

    def body(*refs):
        p_refs, out_refs = refs[:n], refs[n:2 * n]
        send_sems, recv_sems = refs[2 * n:]
        x, y, c, chips = _my_place()
        cps = []
        for k, (px, py) in enumerate(chips):
            for tn in range(n):
                cps.append(pltpu.make_async_remote_copy(
                    src_ref=p_refs[tn].at[:, 2 * px + py], dst_ref=out_refs[tn].at[:, k],
                    send_sem=send_sems.at[tn, k], recv_sem=recv_sems.at[tn, k],
                    device_id=(px, py, c), device_id_type=MESH))
        for cp in cps:
            cp.start()
        for cp in cps:
            cp.wait()

    return pl.pallas_call(
        body, name="rs_chips",
        out_shape=[jax.ShapeDtypeStruct((p.shape[0], 3) + p.shape[2:], p.dtype) for p in parts],
        in_specs=[ANY] * n, out_specs=[ANY] * n,
        scratch_shapes=[pltpu.SemaphoreType.DMA((n, 3)), pltpu.SemaphoreType.DMA((n, 3))],
    )(*parts)


def _sibling_gather(fs):
    n = len(fs)

    def body(*refs):
        out_refs = refs[n:2 * n]
        send_sems, recv_sems = refs[2 * n:]
        x, y, c, _ = _my_place()
        snd = [pltpu.make_async_remote_copy(
            src_ref=out_refs[tn].at[:, c], dst_ref=out_refs[tn].at[:, c], send_sem=send_sems.at[tn], recv_sem=recv_sems.at[tn],
            device_id=(x, y, 1 - c), device_id_type=MESH) for tn in range(n)]
        rcv = [pltpu.make_async_remote_copy(
            src_ref=out_refs[tn].at[:, c], dst_ref=out_refs[tn].at[:, 1 - c], send_sem=send_sems.at[tn], recv_sem=recv_sems.at[tn],
            device_id=(x, y, 1 - c), device_id_type=MESH) for tn in range(n)]
        for cp in snd:
            cp.start()
        for tn in range(n):
            rcv[tn].wait_recv()
            snd[tn].wait_send()

    return pl.pallas_call(
        body, name="rs_sibling_gather",
        out_shape=[jax.ShapeDtypeStruct(f.shape, f.dtype) for f in fs],
        in_specs=[ANY] * n, out_specs=[ANY] * n, input_output_aliases={tn: tn for tn in range(n)},
        scratch_shapes=[pltpu.SemaphoreType.DMA((n,)), pltpu.SemaphoreType.DMA((n,))],
    )(*fs)


def _all_gather_devices(xs):
    m_per, n = xs.shape

    def body(x_ref, out_ref, send_sems, recv_sems, local_sem):
        x, y, c, chips = _my_place()
        me, sibling = (x, y, c), (x, y, 1 - c)

        def rows(px, py, pc):
            return out_ref.at[pl.ds((4 * px + 2 * py + pc) * m_per, m_per), :]

        def copy(k, block, to, src=None):
            return pltpu.make_async_remote_copy(
                src_ref=rows(*block) if src is None else src, dst_ref=rows(*block),
                send_sem=send_sems.at[k], recv_sem=recv_sems.at[k], device_id=to, device_id_type=MESH)

        mine = pltpu.make_async_copy(x_ref, rows(*me), local_sem)
        mine.start()
        first = [copy(0, me, sibling, src=x_ref)]
        first += [copy(1 + j, me, (*chip, c), src=x_ref) for j, chip in enumerate(chips)]
        for cp in first:
            cp.start()
        passed = [copy(4 + j, (*chip, c), sibling) for j, chip in enumerate(chips)]
        for j, chip in enumerate(chips):
            copy(1 + j, (*chip, c), me).wait_recv()
            passed[j].start()
        copy(0, sibling, me).wait_recv()
        for j, chip in enumerate(chips):
            copy(4 + j, (*chip, 1 - c), me).wait_recv()
        for cp in first + passed:
            cp.wait_send()
        mine.wait()

    return pl.pallas_call(
        body, name="ag_small_grads", out_shape=jax.ShapeDtypeStruct((8 * m_per, n), xs.dtype),
        in_specs=[pl.BlockSpec(memory_space=pltpu.VMEM)], out_specs=pl.BlockSpec(memory_space=pltpu.VMEM),
        scratch_shapes=[pltpu.SemaphoreType.DMA((7,)), pltpu.SemaphoreType.DMA((7,)), pltpu.SemaphoreType.DMA],
        compiler_params=_cparams(),
    )(xs)


def _add_halves(gs, recvs, place):
    n = len(gs)

    def body(place_ref, *refs):
        g_refs, r_refs, o_refs = refs[:n], refs[n:2 * n], refs[2 * n:]
        for tn in range(n):
            o_refs[tn][...] = (g_refs[tn][...] + r_refs[tn][...]).astype(BF16)

    def gspec(g):
        return pl.BlockSpec((None, None, None) + g.shape[3:], lambda l, j, p: (l, j, p[1], 0, 0))

    def rspec(r):
        return pl.BlockSpec((None, None) + r.shape[2:], lambda l, j, p: (l, j, 0, 0))

    grid_spec = pltpu.PrefetchScalarGridSpec(
        num_scalar_prefetch=1, grid=(DEPTH, N_CHIPS),
        in_specs=[gspec(g) for g in gs] + [rspec(r) for r in recvs], out_specs=[rspec(r) for r in recvs])
    return pl.pallas_call(body, name="rs_add", grid_spec=grid_spec,
                          out_shape=[jax.ShapeDtypeStruct(r.shape, BF16) for r in recvs],
                          compiler_params=_cparams(("arbitrary", "arbitrary")))(place, *gs, *recvs)


def _sum_slots(parts, slots, place):
    n = len(parts)

    def body(place_ref, *refs):
        p_refs, s_refs, o_refs = refs[:n], refs[n:2 * n], refs[2 * n:]
        for tn in range(n):
            acc = p_refs[tn][...].astype(F32)
            for k in range(3):
                acc = acc + s_refs[tn][k].astype(F32)
            o_refs[tn][...] = acc

    def pspec(p):
        return pl.BlockSpec((None, None) + p.shape[2:], lambda l, pl_: (l, pl_[0], 0, 0))

    def sspec(sl):
        return pl.BlockSpec((None,) + sl.shape[1:], lambda l, pl_: (l, 0, 0, 0))

    def ospec(p):
        return pl.BlockSpec((None, None) + p.shape[2:], lambda l, pl_: (l, pl_[1], 0, 0))

    grid_spec = pltpu.PrefetchScalarGridSpec(
        num_scalar_prefetch=1, grid=(DEPTH,),
        in_specs=[pspec(p) for p in parts] + [sspec(sl) for sl in slots], out_specs=[ospec(p) for p in parts])
    return pl.pallas_call(body, name="rs_sum", grid_spec=grid_spec,
                          out_shape=[jax.ShapeDtypeStruct((p.shape[0], 2) + p.shape[2:], F32) for p in parts],
                          compiler_params=_cparams(("arbitrary",)))(place, *parts, *slots)


def _sum_devices(gathered, m_per):
    def body(g_ref, o_ref):
        acc = g_ref[0:m_per, :]
        for d in range(1, 8):
            acc = acc + g_ref[d * m_per:(d + 1) * m_per, :]
        o_ref[...] = acc

    return pl.pallas_call(body, name="small_sum", out_shape=jax.ShapeDtypeStruct((m_per, LANE), F32),
                          compiler_params=_cparams())(gathered)


def _pad_ff_cols(a):
    lead = a.shape[:-1]
    n = a.shape[-1] // FF_Q
    a = a.reshape(*lead, n, FF_Q)
    a = jnp.pad(a, [(0, 0)] * len(lead) + [(0, 0), (0, FF_QP - FF_Q)])
    return a.reshape(*lead, n * FF_QP)


def _unpad_ff_cols(a):
    lead = a.shape[:-1]
    n = a.shape[-1] // FF_QP
    return a.reshape(*lead, n, FF_QP)[..., :FF_Q].reshape(*lead, n * FF_Q)


def _pack_small(parts):
    flat = jnp.concatenate([p.reshape(-1) for p in parts])
    return flat.reshape(-1, LANE)


SMALL_SHAPES = [("conv_a_w", (CONV_K, CONV_W)), ("conv_a_b", (CONV_W,)), ("ln_a_g", (CONV_W,)), ("ln_a_b", (CONV_W,)),
                ("ln_v_g", (GMLP_W,)), ("ln_v_b", (GMLP_W,)), ("w_s", (6, CHUNK, CHUNK)), ("b_s", (6, CHUNK)),
                ("ln1_g", (D_MODEL,)), ("ln1_b", (D_MODEL,)), ("conv_f_w", (FFN_CONV_K, D_FF)), ("conv_f_b", (D_FF,)),
                ("ln2_g", (D_MODEL,)), ("ln2_b", (D_MODEL,))]


def _unpack_small(flat2d):
    flat = flat2d.reshape(DEPTH, -1)
    out, o = {}, 0
    for name, shp in SMALL_SHAPES:
        n = 1
        for d in shp:
            n *= d
        out[name] = flat[:, o:o + n].reshape((DEPTH,) + shp)
        o += n
    return out


def _rows8(rows):
    blk = jnp.stack(rows, axis=1)
    return jnp.pad(blk, ((0, 0), (0, 8 - len(rows)), (0, 0)))


def kernel(x, mem, w_in, conv_a_w, conv_a_b, ln_a_g, ln_a_b, ln_v_g, ln_v_b, w_s, b_s, w_mk, w_mv, w_out, ln1_g, ln1_b, w_up, conv_f_w, conv_f_b, w_down, ln2_g, ln2_b, loss_target, m_w_in, m_conv_a_w, m_conv_a_b, m_ln_a_g, m_ln_a_b, m_ln_v_g, m_ln_v_b, m_w_s, m_b_s, m_w_mk, m_w_mv, m_w_out, m_ln1_g, m_ln1_b, m_w_up, m_conv_f_w, m_conv_f_b, m_w_down, m_ln2_g, m_ln2_b, v_w_in, v_conv_a_w, v_conv_a_b, v_ln_a_g, v_ln_a_b, v_ln_v_g, v_ln_v_b, v_w_s, v_b_s, v_w_mk, v_w_mv, v_w_out, v_ln1_g, v_ln1_b, v_w_up, v_conv_f_w, v_conv_f_b, v_w_down, v_ln2_g, v_ln2_b):
    seq = x.shape[1]
    t_fwd = min(512, seq)
    t_bwd = min(256, seq)
    chip = 2 * lax.axis_index("x") + lax.axis_index("y")
    core = lax.axis_index("c")
    place = jnp.stack([chip, core]).astype(jnp.int32)
    x0 = x[0]
    mem0 = mem[0]
    target = loss_target[0]

    sh_in = w_in.transpose(0, 2, 1).astype(BF16)
    sh_mk, sh_mv, sh_out = w_mk.astype(BF16), w_mv.astype(BF16), w_out.astype(BF16)
    sh_up = _pad_ff_cols(w_up).transpose(0, 2, 1).astype(BF16)
    sh_dn = jnp.pad(w_down, ((0, 0), (0, FF_QP - FF_Q), (0, 0))).astype(BF16)

    def mixer_weights(g_in, g_mk, g_mv, g_out):
        return dict(win_t=g_in.reshape(1, IN_W, D_MODEL), wmk=g_mk.reshape(1, D_MODEL, XATTN_W),
                    wmv=g_mv.reshape(1, D_MODEL, XATTN_W), wout=g_out.reshape(1, D_MODEL, D_MODEL))

    def ffn_weights(g_up, g_dn):
        return dict(wup_t=g_up.reshape(1, 2, FF_P, D_MODEL), wdown=g_dn.reshape(1, FF_P, D_MODEL))

    n_ca = conv_a_w.size
    small_w = _pack_small([conv_a_w, conv_f_w, jnp.zeros((2 * 80 * LANE - n_ca - conv_f_w.size,), F32)])[None]
    *g_mixer0, small_g = _all_gather_chips([sh_in[:1], sh_mk[:1], sh_mv[:1], sh_out[:1], small_w], "ag_mixer0")
    wts = [mixer_weights(*g_mixer0), None]
    small_g = small_g.reshape(N_CHIPS, -1)
    conv_a_full = small_g[:, :n_ca].reshape(N_CHIPS, DEPTH, CONV_K, CONV_W // 4).transpose(1, 2, 0, 3).reshape(DEPTH, CONV_K, CONV_W)
    conv_f_full = small_g[:, n_ca:n_ca + conv_f_w.size].reshape(N_CHIPS, DEPTH, FFN_CONV_K, FF_Q).transpose(1, 2, 0, 3).reshape(DEPTH, FFN_CONV_K, D_FF)

    tril = jnp.tril(jnp.ones((CHUNK, CHUNK), dtype=bool))
    ws_m = jnp.where(tril, w_s, 0.0)
    wst = ws_m.reshape(DEPTH, 3, 2 * CHUNK, CHUNK).astype(BF16)
    wstt = ws_m.transpose(0, 1, 3, 2).reshape(DEPTH, 3, 2 * CHUNK, CHUNK).astype(BF16)
    bst = jnp.repeat(b_s.transpose(0, 2, 1), HEAD_DIM, axis=2)
    conv_w = jnp.pad(conv_a_full, ((0, 0), (0, CONV_HALO - CONV_K), (0, 0)))
    pa = _rows8([conv_a_b, ln_a_g, ln_a_b, ln_v_g, ln_v_b])
    gin = jnp.concatenate([jnp.ones((1, D_MODEL), F32), ln2_g[:DEPTH - 1]], axis=0)
    bin_ = jnp.concatenate([jnp.zeros((1, D_MODEL), F32), ln2_b[:DEPTH - 1]], axis=0)
    pd = _rows8([gin, bin_, ln1_g, ln1_b, ln2_g, ln2_b])
    pf = jnp.concatenate([_pad_ff_cols(conv_f_full), _pad_ff_cols(conv_f_b)[:, None, :],
                          jnp.zeros((DEPTH, 8 - FFN_CONV_K - 1, FF_P), F32)], axis=1)

    acts = []
    xin = x0
    for l in range(DEPTH):
        w = wts[l]
        kt_all, k_all, v_all, vt_all = _kv_fwd(mem0, w["wmk"], w["wmv"])
        side = [sh_up[0], sh_dn[0]] if l == 0 else None
        xh1, rstd1, h, ac, cat, *landed = _mixer_fwd(xin, pd, w["win_t"], conv_w, pa, wst, bst, kt_all, v_all, w["wout"],
                                                     l, t_fwd, side)
        if l == 0:
            w.update(ffn_weights(*landed))
        side = [sh_in[1], sh_mk[1], sh_mv[1], sh_out[1], sh_up[1], sh_dn[1]] if l == 0 else None
        xh2, rstd2, upg, upv, *landed = _ffn_fwd(xh1, pd, w["wup_t"], pf, w["wdown"], l, t_bwd, side)
        if l == 0:
            wts[1] = {**mixer_weights(*landed[:4]), **ffn_weights(*landed[4:])}
        acts.append(dict(xin=xin, kt_all=kt_all, k_all=k_all, vt_all=vt_all, xh1=xh1, rstd1=rstd1, h=h, ac=ac, cat=cat,
                         xh2=xh2, rstd2=rstd2, upg=upg, upv=upv))
        xin = xh2

    small = [None] * DEPTH
    gz = target
    loss_sum = None
    g_ffn = g_mix = g_kv = None
    for l in reversed(range(DEPTH)):
        a, w = acts[l], wts[l]
        last = l == DEPTH - 1
        dx1, dy, dug, duv, hm, vd2, vf = _ffn_bwd_d(gz, a["xh2"], a["rstd2"], a["upg"], a["upv"], pd, pf, w["wdown"], w["wup_t"],
                                                    l, t_bwd, last)
        if last:
            loss_sum = vd2[VD_LOSS, 0]
        g_ffn = _ffn_bwd_w(a["xh1"], pd, dy, dug, duv, hm, l, t_fwd, g_ffn)
        dx0, dh, dmix, vd1, va, dcw, dws, dbs, dkt, dv = _mixer_bwd_d(
            dx1, a["xh1"], a["rstd1"], a["h"], a["ac"], pd, conv_w, pa, wst, wstt, bst,
            a["kt_all"], a["k_all"], a["vt_all"], w["wout"], w["win_t"], l, t_bwd)
        g_mix = _mixer_bwd_w(a["xin"], pd, dh, a["cat"], dmix, l, t_fwd, g_mix)
        g_kv = _kv_bwd(mem0, dkt, dv, l, g_kv)
        dws6 = jnp.where(tril, dws.reshape(6, CHUNK, CHUNK), 0.0)
        small[l] = [dcw[:CONV_K], va[VA_CONV_B], va[VA_LNA_G], va[VA_LNA_B], va[VA_LNV_G], va[VA_LNV_B], dws6,
                    dbs[:, :6].T, vd1[VD_LN_G], vd1[VD_LN_B],
                    _unpad_ff_cols(vf[VF_W0:VF_W0 + FFN_CONV_K]), _unpad_ff_cols(vf[VF_B]),
                    vd2[VD_LN_G], vd2[VD_LN_B]]
        gz = dx0
    grad_x = gz[None]

    gw_up_t, gw_down = g_ffn
    gw_in_t, gw_out = g_mix
    gw_mk, gw_mv = g_kv
    tensors = [gw_in_t, gw_mk, gw_mv, gw_out, gw_up_t.reshape(DEPTH, 2 * FF_P, D_MODEL), gw_down]
    g5 = [g.reshape(DEPTH, N_CHIPS, 2, g.shape[1] // (2 * N_CHIPS), g.shape[2]) for g in tensors]
    recvs = _sibling_swap_halves(g5)
    parts = _add_halves(g5, recvs, place)
    slots = _chip_scatter(parts)
    halves = _sum_slots(parts, slots, place)
    red = [f.reshape(DEPTH, -1, f.shape[-1]) for f in _sibling_gather(halves)]
    g_w_in = red[0].transpose(0, 2, 1)
    g_w_mk, g_w_mv, g_w_out = red[1], red[2], red[3]
    g_w_up = _unpad_ff_cols(red[4].transpose(0, 2, 1))
    g_w_down = red[5][:, :FF_Q, :]

    small_local = _pack_small([q for l in range(DEPTH) for q in small[l]])
    m_small = small_local.shape[0]
    small_red = _sum_devices(_all_gather_devices(small_local), m_small)
    sg = _unpack_small(small_red)
    g_conv_a_w = lax.dynamic_slice_in_dim(sg["conv_a_w"], chip * (CONV_W // 4), CONV_W // 4, axis=2)
    g_conv_f_w = lax.dynamic_slice_in_dim(sg["conv_f_w"], chip * FF_Q, FF_Q, axis=2)

    loss = 0.5 / D_MODEL * lax.psum(loss_sum, ("x", "y", "c"))

    grads = dict(w_in=g_w_in, conv_a_w=g_conv_a_w, conv_a_b=sg["conv_a_b"], ln_a_g=sg["ln_a_g"], ln_a_b=sg["ln_a_b"],
                 ln_v_g=sg["ln_v_g"], ln_v_b=sg["ln_v_b"], w_s=sg["w_s"], b_s=sg["b_s"], w_mk=g_w_mk, w_mv=g_w_mv,
                 w_out=g_w_out, ln1_g=sg["ln1_g"], ln1_b=sg["ln1_b"], w_up=g_w_up, conv_f_w=g_conv_f_w,
                 conv_f_b=sg["conv_f_b"], w_down=g_w_down, ln2_g=sg["ln2_g"], ln2_b=sg["ln2_b"])
    weights = dict(w_in=w_in, conv_a_w=conv_a_w, conv_a_b=conv_a_b, ln_a_g=ln_a_g, ln_a_b=ln_a_b, ln_v_g=ln_v_g,
                   ln_v_b=ln_v_b, w_s=w_s, b_s=b_s, w_mk=w_mk, w_mv=w_mv, w_out=w_out, ln1_g=ln1_g, ln1_b=ln1_b,
                   w_up=w_up, conv_f_w=conv_f_w, conv_f_b=conv_f_b, w_down=w_down, ln2_g=ln2_g, ln2_b=ln2_b)
    mom_m = dict(w_in=m_w_in, conv_a_w=m_conv_a_w, conv_a_b=m_conv_a_b, ln_a_g=m_ln_a_g, ln_a_b=m_ln_a_b, ln_v_g=m_ln_v_g,
                 ln_v_b=m_ln_v_b, w_s=m_w_s, b_s=m_b_s, w_mk=m_w_mk, w_mv=m_w_mv, w_out=m_w_out, ln1_g=m_ln1_g,
                 ln1_b=m_ln1_b, w_up=m_w_up, conv_f_w=m_conv_f_w, conv_f_b=m_conv_f_b, w_down=m_w_down, ln2_g=m_ln2_g,
                 ln2_b=m_ln2_b)
    mom_v = dict(w_in=v_w_in, conv_a_w=v_conv_a_w, conv_a_b=v_conv_a_b, ln_a_g=v_ln_a_g, ln_a_b=v_ln_a_b, ln_v_g=v_ln_v_g,
                 ln_v_b=v_ln_v_b, w_s=v_w_s, b_s=v_b_s, w_mk=v_w_mk, w_mv=v_w_mv, w_out=v_w_out, ln1_g=v_ln1_g,
                 ln1_b=v_ln1_b, w_up=v_w_up, conv_f_w=v_conv_f_w, conv_f_b=v_conv_f_b, w_down=v_w_down, ln2_g=v_ln2_g,
                 ln2_b=v_ln2_b)
    names = list(weights)
    big_names = ["w_in", "w_mk", "w_mv", "w_out", "w_up", "w_down"]
    delta, new_m, new_v = {}, {}, {}
    for n in big_names:
        shp = weights[n].shape
        as2d = lambda a: a.reshape(-1, shp[-1])
        d, nm, nv = _adamw(as2d(weights[n]), as2d(grads[n]), as2d(mom_m[n]), as2d(mom_v[n]), "adamw_" + n)
        delta[n], new_m[n], new_v[n] = d.reshape(shp), nm.reshape(shp), nv.reshape(shp)
    small_names = [n for n in names if n not in big_names]
    sizes = [weights[n].size for n in small_names]
    pad = (-sum(sizes)) % (8 * LANE)

    def pack(dct, fill):
        return _pack_small([dct[n] for n in small_names] + [jnp.full((pad,), fill, F32)])

    d, nm, nv = _adamw(pack(weights, 0.0), pack(grads, 0.0), pack(mom_m, 0.0), pack(mom_v, 1.0), "adamw_small")
    o = 0
    for n, sz in zip(small_names, sizes):
        shp = weights[n].shape
        delta[n] = d.reshape(-1)[o:o + sz].reshape(shp)
        new_m[n] = nm.reshape(-1)[o:o + sz].reshape(shp)
        new_v[n] = nv.reshape(-1)[o:o + sz].reshape(shp)
        o += sz

    return (loss, grad_x, *[grads[n] for n in names], *[delta[n] for n in names],
            *[new_m[n] for n in names], *[new_v[n] for n in names])
```

```python
import jax
import jax.numpy as jnp
from jax import lax
from jax.experimental import pallas as pl
from jax.experimental.pallas import tpu as pltpu

F32 = jnp.float32
BF16 = jnp.bfloat16

D_MODEL = 1024
DEPTH = 2
CONV_W = 384
GMLP_W = 384
XATTN_W = 256
XATTN_HEADS = 4
HEAD_DIM = 64
IN_W = 1792
CONV_K = 31
CHUNK = 128
N_MEM = 256
D_FF = 2752
FFN_CONV_K = 3
ALPHA = (2.0 * DEPTH) ** 0.25
LN_EPS = 1e-5
ATT_SCALE = 1.0 / 8.0
ADAM_LR, ADAM_B1, ADAM_B2, ADAM_EPS, ADAM_WD, ADAM_STEP = 0.001, 0.9, 0.999, 1e-08, 0.01, 10

N_CHIPS = 4
FF_Q = D_FF // N_CHIPS
FF_QP = 704
FF_H = 2 * FF_QP
FF_P = 4 * FF_QP
LANE = 128
CONV_HALO = 32
FFN_HALO = 8
BF16_ROWS = 16
VMEM_LIMIT = 60 * 1024 * 1024

MESH = pl.DeviceIdType.MESH
ANY = pl.BlockSpec(memory_space=pl.ANY)


def _cparams(sem=None, vmem=VMEM_LIMIT):
    kw = {"vmem_limit_bytes": vmem}
    if sem is not None:
        kw["dimension_semantics"] = sem
    return pltpu.CompilerParams(**kw)


def _row_tile(rows, row_bytes, limit=2 << 20):
    if rows * row_bytes <= limit:
        return rows
    best = None
    for cand in range(BF16_ROWS, rows, BF16_ROWS):
        if rows % cand == 0 and cand * row_bytes <= limit:
            best = cand
    assert best is not None, (rows, row_bytes)
    return best


def _const_spec(shape):
    nd = len(shape)
    return pl.BlockSpec(shape, lambda *_: (0,) * nd)


def _layer_spec(shape, *lead, resident=False):
    nd = len(shape)
    kw = {"pipeline_mode": pl.Buffered(1)} if resident else {}
    return pl.BlockSpec((None,) * len(lead) + tuple(shape), lambda *_: tuple(lead) + (0,) * nd, **kw)


def _sigmoid(x):
    return jax.nn.sigmoid(x)


def _gelu(x):
    return jax.nn.gelu(x)


def _gelu_grad(x):
    c = 0.7978845608028654
    a = 0.044715
    t = jnp.tanh(c * (x + a * x * x * x))
    return 0.5 * (1.0 + t) + 0.5 * x * (1.0 - t * t) * c * (1.0 + 3.0 * a * x * x)


def _ln_fwd(z):
    mu = jnp.mean(z, axis=-1, keepdims=True)
    zc = z - mu
    var = jnp.mean(zc * zc, axis=-1, keepdims=True)
    rstd = lax.rsqrt(var + LN_EPS)
    return zc * rstd, rstd


def _ln_bwd(dxh, xh, rstd):
    m1 = jnp.mean(dxh, axis=-1, keepdims=True)
    m2 = jnp.mean(dxh * xh, axis=-1, keepdims=True)
    return rstd * (dxh - m1 - xh * m2)


def _colsum(a):
    return jnp.sum(a, axis=0, keepdims=True)


def _dot(a, b):
    return jnp.dot(a, b, preferred_element_type=F32)


def _dot_tn(a, b):
    return lax.dot_general(a, b, (((0,), (0,)), ((), ())), preferred_element_type=F32)


def _dot_nt(a, b):
    return lax.dot_general(a, b, (((1,), (1,)), ((), ())), preferred_element_type=F32)


def _softmax_heads(sc):
    ps = []
    for hd in range(XATTN_HEADS):
        s = sc[:, hd * N_MEM:(hd + 1) * N_MEM]
        e = jnp.exp(s - jnp.max(s, axis=-1, keepdims=True))
        ps.append(e / jnp.sum(e, axis=-1, keepdims=True))
    return jnp.concatenate(ps, axis=1)


def _lane_lo(shape):
    return (lax.broadcasted_iota(jnp.int32, shape, len(shape) - 1) % LANE) < HEAD_DIM


def _spatial_mix(vnb, wst_ref, bst_ref, mix_ref, t):
    lo = _lane_lo((CHUNK, LANE))
    for n in range(t // CHUNK):
        rows = slice(n * CHUNK, (n + 1) * CHUNK)
        for j in range(GMLP_W // LANE):
            cols = slice(j * LANE, (j + 1) * LANE)
            r = _dot(wst_ref[j], vnb[rows, cols])
            mix_ref[rows, cols] = jnp.where(lo, r[:CHUNK], r[CHUNK:]) + bst_ref[:, cols]


def _kv_fwd(mem, w_mk, w_mv):
    def body(mem_ref, wk_ref, wv_ref, kt_ref, k_ref, v_ref, vt_ref):
        mb = mem_ref[...].astype(BF16)
        k = _dot(mb, wk_ref[...])
        v = _dot(mb, wv_ref[...])
        col = lax.broadcasted_iota(jnp.int32, (N_MEM, XATTN_W), 1) // HEAD_DIM
        ks = [jnp.where(col == hd, k, 0.0) for hd in range(XATTN_HEADS)]
        vs = [jnp.where(col == hd, v, 0.0) for hd in range(XATTN_HEADS)]
        k_ref[...] = jnp.concatenate(ks, axis=0).astype(BF16)
        v_ref[...] = jnp.concatenate(vs, axis=0).astype(BF16)
        kt_ref[...] = jnp.concatenate([x.T for x in ks], axis=1).astype(BF16)
        vt_ref[...] = jnp.concatenate([x.T for x in vs], axis=1).astype(BF16)

    wide = jax.ShapeDtypeStruct((XATTN_W, XATTN_HEADS * N_MEM), BF16)
    tall = jax.ShapeDtypeStruct((XATTN_HEADS * N_MEM, XATTN_W), BF16)
    wspec = _layer_spec((D_MODEL, XATTN_W), 0)
    return pl.pallas_call(body, name="kv_fwd", grid=(1,),
                          in_specs=[_const_spec((N_MEM, D_MODEL)), wspec, wspec],
                          out_specs=[_const_spec(wide.shape), _const_spec(tall.shape), _const_spec(tall.shape),
                                     _const_spec(wide.shape)],
                          out_shape=(wide, tall, tall, wide), compiler_params=_cparams(("arbitrary",)))(mem, w_mk, w_mv)


def _kv_bwd(mem, dkt_all, dv_all):
    def body(mem_ref, dkt_ref, dv_ref, gk_ref, gv_ref):
        col = lax.broadcasted_iota(jnp.int32, (N_MEM, XATTN_W), 1) // HEAD_DIM
        dk = jnp.zeros((N_MEM, XATTN_W), F32)
        dv = jnp.zeros((N_MEM, XATTN_W), F32)
        for hd in range(XATTN_HEADS):
            dk = dk + jnp.where(col == hd, dkt_ref[:, hd * N_MEM:(hd + 1) * N_MEM].T, 0.0)
            dv = dv + jnp.where(col == hd, dv_ref[hd * N_MEM:(hd + 1) * N_MEM, :], 0.0)
        mb = mem_ref[...].astype(BF16)
        gk_ref[0] = _dot_tn(mb, dk.astype(BF16))
        gv_ref[0] = _dot_tn(mb, dv.astype(BF16))

    out = jax.ShapeDtypeStruct((1, D_MODEL, XATTN_W), F32)
    return pl.pallas_call(body, name="kv_bwd", out_shape=(out, out), compiler_params=_cparams())(mem, dkt_all, dv_all)


def _my_place():
    x, y, c = lax.axis_index("x"), lax.axis_index("y"), lax.axis_index("c")
    chips = [(1 - x, y), (x, 1 - y), (1 - x, 1 - y)]
    return x, y, c, chips


def _side_gather(shard_refs, land_refs, send_sems, recv_sems, local_sems):
    n = len(shard_refs)
    x, y, c, chips = _my_place()
    mej = 2 * x + y

    def remote(tn, k, slot):
        px, py = chips[k]
        return pltpu.make_async_remote_copy(
            src_ref=shard_refs[tn], dst_ref=land_refs[tn].at[slot], send_sem=send_sems.at[tn, k],
            recv_sem=recv_sems.at[tn, k], device_id=(px, py, c), device_id_type=MESH)

    def local(tn):
        return pltpu.make_async_copy(shard_refs[tn], land_refs[tn].at[mej], local_sems.at[tn])

    def start():
        for k in range(3):
            for tn in range(n):
                remote(tn, k, mej).start()
        for tn in range(n):
            local(tn).start()

    def wait():
        for k, (px, py) in enumerate(chips):
            for tn in range(n):
                remote(tn, k, 2 * px + py).wait_recv()
                remote(tn, k, mej).wait_send()
        for tn in range(n):
            local(tn).wait()

    return start, wait


def _side_specs(side):
    side = list(side or ())
    n = len(side)
    shapes = [jax.ShapeDtypeStruct((N_CHIPS,) + a.shape, a.dtype) for a in side]
    scratch = [pltpu.SemaphoreType.DMA((n, 3)), pltpu.SemaphoreType.DMA((n, 3)), pltpu.SemaphoreType.DMA((n,))] if n else []
    return side, [ANY] * n, [ANY] * n, shapes, scratch


class _SideOp:
    def __init__(self, ins, out_shapes, scratch, make, aliases=None):
        self.ins, self.out_shapes, self.scratch, self.make, self.aliases = list(ins), list(out_shapes), list(scratch), make, dict(aliases or {})


def _call_with_side(body, side_ops, *, name, grid, in_specs, out_specs, out_shape, scratch_shapes, operands, semantics):
    side_ops = list(side_ops or ())
    n_in, n_out, n_scr = len(in_specs), len(out_specs), len(scratch_shapes)
    s_ins = [a for op in side_ops for a in op.ins]
    s_outs = [o for op in side_ops for o in op.out_shapes]
    s_scr = [x for op in side_ops for x in op.scratch]
    aliases, oi, oo = {}, 0, 0
    for op in side_ops:
        for a, b in op.aliases.items():
            aliases[n_in + oi + a] = n_out + oo + b
        oi, oo = oi + len(op.ins), oo + len(op.out_shapes)

    def wrapped(*refs):
        ins, sins = refs[:n_in], refs[n_in:n_in + len(s_ins)]
        base = n_in + len(s_ins)
        outs, souts = refs[base:base + n_out], refs[base + n_out:base + n_out + len(s_outs)]
        base += n_out + len(s_outs)
        scr, sscr = refs[base:base + n_scr], refs[base + n_scr:]
        if side_ops:
            first = pl.program_id(0) == 0
            last = pl.program_id(0) == grid[0] - 1
            for d in range(1, len(grid)):
                first = jnp.logical_and(first, pl.program_id(d) == 0)
                last = jnp.logical_and(last, pl.program_id(d) == grid[d] - 1)
            hooks, a, b, c = [], 0, 0, 0
            for op in side_ops:
                hooks.append(op.make(sins[a:a + len(op.ins)], souts[b:b + len(op.out_shapes)], sscr[c:c + len(op.scratch)]))
                a, b, c = a + len(op.ins), b + len(op.out_shapes), c + len(op.scratch)

            @pl.when(first)
            def _():
                for start, _w in hooks:
                    start()

        body(*ins, *outs, *scr)
        if side_ops:
            @pl.when(last)
            def _():
                for _s, wait in hooks:
                    wait()

    res = pl.pallas_call(
        wrapped, name=name, grid=grid, in_specs=list(in_specs) + [ANY] * len(s_ins),
        out_specs=list(out_specs) + [ANY] * len(s_outs), out_shape=list(out_shape) + s_outs,
        scratch_shapes=list(scratch_shapes) + s_scr, input_output_aliases=aliases,
        compiler_params=_cparams(semantics),
    )(*operands, *s_ins)
    side_res, k = [], n_out
    for op in side_ops:
        side_res.append(list(res[k:k + len(op.out_shapes)]))
        k += len(op.out_shapes)
    return list(res[:n_out]), side_res


PA_CONV_B, PA_LNA_G, PA_LNA_B, PA_LNV_G, PA_LNV_B = 0, 1, 2, 3, 4
PD_GIN, PD_BIN, PD_G1, PD_B1, PD_G2, PD_B2 = 0, 1, 2, 3, 4, 5


def _row(ref, r):
    return ref[r:r + 1, :]


def _mixer_fwd(xin, pd, win_t, conv_w, pa, wst, bst, kt_all, v_all, w_out, l, t, side=None):
    s = xin.shape[0]
    nt = s // t
    side, side_in, side_out, side_shapes, side_scratch = _side_specs(side)
    ns = len(side)

    def body(x_ref, pd_ref, wint_ref, cw_ref, pa_ref, wst_ref, bst_ref, kt_ref, v_ref, wout_ref, *rest):
        shard_refs, rest = rest[:ns], rest[ns:]
        xh_ref, rstd_ref, h_ref, ac_ref, cat_ref = rest[:5]
        land_refs, rest = rest[5:5 + ns], rest[5 + ns:]
        cbuf, mixbuf, zbuf = rest[:3]
        i = pl.program_id(0)
        if ns:
            gather_start, gather_wait = _side_gather(shard_refs, land_refs, *rest[3:])
            pl.when(i == 0)(gather_start)
        x = x_ref[...] * _row(pd_ref, PD_GIN) + _row(pd_ref, PD_BIN)
        h = _dot_nt(x.astype(BF16), wint_ref[...])
        h_ref[...] = h
        a1, a2 = h[:, 0:CONV_W], h[:, CONV_W:2 * CONV_W]
        hu, hv = h[:, 2 * CONV_W:2 * CONV_W + GMLP_W], h[:, 2 * CONV_W + GMLP_W:2 * CONV_W + 2 * GMLP_W]
        q = h[:, IN_W - XATTN_W:]

        @pl.when(i == 0)
        def _():
            cbuf[0:CONV_HALO, :] = jnp.zeros((CONV_HALO, CONV_W), F32)

        cbuf[CONV_HALO:CONV_HALO + t, :] = a1 * _sigmoid(a2)
        ac = jnp.zeros((t, CONV_W), F32) + _row(pa_ref, PA_CONV_B)
        for r in range(8):
            zr = jnp.zeros((t + 8, CONV_W), F32)
            for a in range(4):
                o = 8 * a + r
                if o < CONV_K:
                    k = CONV_K - 1 - o
                    zr = zr + cbuf[CONV_HALO - 8 - 8 * a:CONV_HALO - 8 - 8 * a + t + 8, :] * cw_ref[k:k + 1, :]
            if r == 0:
                ac = ac + zr[8:, :]
            else:
                zbuf[...] = zr
                ac = ac + zbuf[8 - r:8 - r + t, :]
        ac_ref[...] = ac
        cbuf[0:CONV_HALO, :] = cbuf[t:t + CONV_HALO, :]
        xh_a, _ = _ln_fwd(ac)
        an = xh_a * _row(pa_ref, PA_LNA_G) + _row(pa_ref, PA_LNA_B)
        a = an * _sigmoid(an)

        u = _gelu(hu)
        xh_v, _ = _ln_fwd(_gelu(hv))
        vn = xh_v * _row(pa_ref, PA_LNV_G) + _row(pa_ref, PA_LNV_B)
        _spatial_mix(vn.astype(BF16), wst_ref, bst_ref, mixbuf, t)
        g = u * mixbuf[...]

        p = _softmax_heads(_dot(q.astype(BF16), kt_ref[...]) * ATT_SCALE)
        o = _dot(p.astype(BF16), v_ref[...])

        cat = jnp.concatenate([a, g, o], axis=1).astype(BF16)
        cat_ref[...] = cat
        z = ALPHA * x + _dot(cat, wout_ref[...])
        xh, rstd = _ln_fwd(z)
        xh_ref[...] = xh
        rstd_ref[...] = rstd
        if ns:
            pl.when(i == nt - 1)(gather_wait)

    tok = lambda w: pl.BlockSpec((t, w), lambda i: (i, 0))
    return pl.pallas_call(
        body, name="mixer_fwd_gather" if ns else "mixer_fwd", grid=(nt,),
        in_specs=[tok(D_MODEL), _layer_spec((8, D_MODEL), l), _layer_spec((IN_W, D_MODEL), 0, resident=True),
                  _layer_spec((CONV_HALO, CONV_W), l), _layer_spec((8, CONV_W), l),
                  _layer_spec((3, 2 * CHUNK, CHUNK), l), _layer_spec((CHUNK, GMLP_W), l),
                  _const_spec((XATTN_W, XATTN_HEADS * N_MEM)), _const_spec((XATTN_HEADS * N_MEM, XATTN_W)),
                  _layer_spec((D_MODEL, D_MODEL), 0, resident=True)] + side_in,
        out_specs=[tok(D_MODEL), tok(1), tok(IN_W), tok(CONV_W), tok(D_MODEL)] + side_out,
        out_shape=[jax.ShapeDtypeStruct((s, D_MODEL), F32), jax.ShapeDtypeStruct((s, 1), F32),
                   jax.ShapeDtypeStruct((s, IN_W), F32), jax.ShapeDtypeStruct((s, CONV_W), F32),
                   jax.ShapeDtypeStruct((s, D_MODEL), BF16)] + side_shapes,
        scratch_shapes=[pltpu.VMEM((t + CONV_HALO, CONV_W), F32), pltpu.VMEM((t, GMLP_W), F32),
                        pltpu.VMEM((t + 8, CONV_W), F32)] + side_scratch,
        compiler_params=_cparams(("arbitrary",)),
    )(xin, pd, win_t, conv_w, pa, wst, bst, kt_all, v_all, w_out, *side)


VD_LN_G, VD_LN_B, VD_LOSS = 0, 1, 2
VA_CONV_B, VA_LNA_G, VA_LNA_B, VA_LNV_G, VA_LNV_B = 0, 1, 2, 3, 4


def _mixer_bwd_d(gz, xh1, rstd1, h, ac, pd, conv_w, pa, wst, wstt, bst, kt_all, k_all, vt_all, w_out, win_t, l, t, side_ops=None):
    s = gz.shape[0]
    nt = s // t

    def body(gz_ref, xh_ref, rstd_ref, h_ref, ac_ref, pd_ref, cw_ref, pa_ref, wst_ref, wstt_ref, bst_ref,
             kt_ref, k_ref, vt_ref, wout_ref, wint_ref,
             dx_ref, dh_ref, dmix_ref, vd_ref, va_ref, dcw_ref, dws_ref, dbs_ref, dkt_ref, dv_ref,
             ebuf, mixbuf, dvnbuf, dbsacc, erbuf):
        i = pl.program_id(0)

        @pl.when(i == 0)
        def _():
            vd_ref[...] = jnp.zeros_like(vd_ref)
            va_ref[...] = jnp.zeros_like(va_ref)
            dcw_ref[...] = jnp.zeros_like(dcw_ref)
            dws_ref[...] = jnp.zeros_like(dws_ref)
            dbs_ref[...] = jnp.zeros_like(dbs_ref)
            dkt_ref[...] = jnp.zeros_like(dkt_ref)
            dv_ref[...] = jnp.zeros_like(dv_ref)
            dbsacc[...] = jnp.zeros_like(dbsacc)
            ebuf[t:t + CONV_HALO, :] = jnp.zeros((CONV_HALO, CONV_W), F32)

        gz_v = gz_ref[...]
        xh = xh_ref[...]
        vd_ref[VD_LN_G:VD_LN_G + 1, :] += _colsum(gz_v * xh)
        vd_ref[VD_LN_B:VD_LN_B + 1, :] += _colsum(gz_v)
        dz = _ln_bwd(gz_v * _row(pd_ref, PD_G1), xh, rstd_ref[...])
        dzb = dz.astype(BF16)
        dmix_ref[...] = dzb
        dcat = _dot_nt(dzb, wout_ref[...])
        d_a, d_g, d_o = dcat[:, 0:CONV_W], dcat[:, CONV_W:CONV_W + GMLP_W], dcat[:, CONV_W + GMLP_W:]

        h = h_ref[...]
        a1, a2 = h[:, 0:CONV_W], h[:, CONV_W:2 * CONV_W]
        hu, hv = h[:, 2 * CONV_W:2 * CONV_W + GMLP_W], h[:, 2 * CONV_W + GMLP_W:2 * CONV_W + 2 * GMLP_W]
        q = h[:, IN_W - XATTN_W:]

        xh_a, rstd_a = _ln_fwd(ac_ref[...])
        an = xh_a * _row(pa_ref, PA_LNA_G) + _row(pa_ref, PA_LNA_B)
        sig = _sigmoid(an)
        d_an = d_a * (sig * (1.0 + an * (1.0 - sig)))
        va_ref[VA_LNA_G:VA_LNA_G + 1, :] += _colsum(d_an * xh_a)
        va_ref[VA_LNA_B:VA_LNA_B + 1, :] += _colsum(d_an)
        dac = _ln_bwd(d_an * _row(pa_ref, PA_LNA_G), xh_a, rstd_a)
        va_ref[VA_CONV_B:VA_CONV_B + 1, :] += _colsum(dac)
        ebuf[0:t, :] = dac
        sg = _sigmoid(a2)
        glu = a1 * sg
        dglu = jnp.zeros((t, CONV_W), F32)
        for r in range(8):
            if r > 0:
                erbuf[...] = ebuf[r:r + t + 24, :]
            src = ebuf if r == 0 else erbuf
            for a in range(4):
                o = 8 * a + r
                if o < CONV_K:
                    k = CONV_K - 1 - o
                    ek = src[8 * a:8 * a + t, :]
                    dglu = dglu + ek * cw_ref[k:k + 1, :]
                    dcw_ref[k:k + 1, :] += _colsum(ek * glu)
        ebuf[t:t + CONV_HALO, :] = ebuf[0:CONV_HALO, :]
        da1 = dglu * sg
        da2 = dglu * a1 * sg * (1.0 - sg)

        u = _gelu(hu)
        xh_v, rstd_v = _ln_fwd(_gelu(hv))
        vn = xh_v * _row(pa_ref, PA_LNV_G) + _row(pa_ref, PA_LNV_B)
        vnb = vn.astype(BF16)
        _spatial_mix(vnb, wst_ref, bst_ref, mixbuf, t)
        dhu = d_g * mixbuf[...] * _gelu_grad(hu)
        dm = d_g * u
        dmb = dm.astype(BF16)
        lo = _lane_lo((CHUNK, LANE))
        for n in range(t // CHUNK):
            rows = slice(n * CHUNK, (n + 1) * CHUNK)
            dbsacc[...] += dm[rows, :]
            for j in range(GMLP_W // LANE):
                cols = slice(j * LANE, (j + 1) * LANE)
                dm_blk = dmb[rows, cols]
                r = _dot(wstt_ref[j], dm_blk)
                dvnbuf[rows, cols] = jnp.where(lo, r[:CHUNK], r[CHUNK:])
                zero = jnp.zeros_like(dm_blk)
                st = jnp.concatenate([jnp.where(lo, dm_blk, zero), jnp.where(lo, zero, dm_blk)], axis=0)
                dws_ref[j] += _dot_nt(st, vnb[rows, cols])
        dvn = dvnbuf[...]
        va_ref[VA_LNV_G:VA_LNV_G + 1, :] += _colsum(dvn * xh_v)
        va_ref[VA_LNV_B:VA_LNV_B + 1, :] += _colsum(dvn)
        dhv = _ln_bwd(dvn * _row(pa_ref, PA_LNV_G), xh_v, rstd_v) * _gelu_grad(hv)

        qb = q.astype(BF16)
        p = _softmax_heads(_dot(qb, kt_ref[...]) * ATT_SCALE)
        dob = d_o.astype(BF16)
        dp = _dot(dob, vt_ref[...])
        dss = []
        for hd in range(XATTN_HEADS):
            cs = slice(hd * N_MEM, (hd + 1) * N_MEM)
            ph, dph = p[:, cs], dp[:, cs]
            dss.append(ph * (dph - jnp.sum(ph * dph, axis=-1, keepdims=True)) * ATT_SCALE)
        dsb = jnp.concatenate(dss, axis=1).astype(BF16)
        dq = _dot(dsb, k_ref[...])
        dkt_ref[...] += _dot_tn(qb, dsb)
        dv_ref[...] += _dot_tn(p.astype(BF16), dob)

        dhb = jnp.concatenate([da1, da2, dhu, dhv, dq], axis=1).astype(BF16)
        dh_ref[...] = dhb
        dx_ref[...] = ALPHA * dz + _dot(dhb, wint_ref[...])

        @pl.when(i == nt - 1)
        def _():
            acc = dbsacc[...]
            head = lax.broadcasted_iota(jnp.int32, (CHUNK, GMLP_W), 1) // HEAD_DIM
            lane = lax.broadcasted_iota(jnp.int32, (CHUNK, LANE), 1)
            out = jnp.zeros((CHUNK, LANE), F32)
            for hd in range(GMLP_W // HEAD_DIM):
                sh = jnp.sum(jnp.where(head == hd, acc, 0.0), axis=1, keepdims=True)
                out = out + jnp.where(lane == hd, sh, 0.0)
            dbs_ref[...] = out

    rev = lambda w: pl.BlockSpec((t, w), lambda i: (nt - 1 - i, 0))
    out_shape = [
        jax.ShapeDtypeStruct((s, D_MODEL), F32), jax.ShapeDtypeStruct((s, IN_W), BF16),
        jax.ShapeDtypeStruct((s, D_MODEL), BF16),
        jax.ShapeDtypeStruct((8, D_MODEL), F32), jax.ShapeDtypeStruct((8, CONV_W), F32),
        jax.ShapeDtypeStruct((CONV_HALO, CONV_W), F32), jax.ShapeDtypeStruct((3, 2 * CHUNK, CHUNK), F32),
        jax.ShapeDtypeStruct((CHUNK, LANE), F32),
        jax.ShapeDtypeStruct((XATTN_W, XATTN_HEADS * N_MEM), F32), jax.ShapeDtypeStruct((XATTN_HEADS * N_MEM, XATTN_W), F32),
    ]
    out_specs = [rev(D_MODEL), rev(IN_W), rev(D_MODEL)] + [_const_spec(o.shape) for o in out_shape[3:]]
    return _call_with_side(
        body, side_ops, name="mixer_bwd_d", grid=(nt,),
        in_specs=[rev(D_MODEL), rev(D_MODEL), rev(1), rev(IN_W), rev(CONV_W),
                  _layer_spec((8, D_MODEL), l), _layer_spec((CONV_HALO, CONV_W), l), _layer_spec((8, CONV_W), l),
                  _layer_spec((3, 2 * CHUNK, CHUNK), l), _layer_spec((3, 2 * CHUNK, CHUNK), l),
                  _layer_spec((CHUNK, GMLP_W), l),
                  _const_spec((XATTN_W, XATTN_HEADS * N_MEM)), _const_spec((XATTN_HEADS * N_MEM, XATTN_W)),
                  _const_spec((XATTN_W, XATTN_HEADS * N_MEM)),
                  _layer_spec((D_MODEL, D_MODEL), 0, resident=True), _layer_spec((IN_W, D_MODEL), 0, resident=True)],
        out_specs=out_specs, out_shape=out_shape,
        scratch_shapes=[pltpu.VMEM((t + CONV_HALO, CONV_W), F32), pltpu.VMEM((t, GMLP_W), F32),
                        pltpu.VMEM((t, GMLP_W), F32), pltpu.VMEM((CHUNK, GMLP_W), F32),
                        pltpu.VMEM((t + 24, CONV_W), F32)],
        operands=(gz, xh1, rstd1, h, ac, pd, conv_w, pa, wst, wstt, bst, kt_all, k_all, vt_all, w_out, win_t),
        semantics=("arbitrary",))


def _mixer_bwd_w(xin, pd, dh, cat, dmix, l, t, side_ops=None):
    s = xin.shape[0]
    nt = s // t

    def body(x_ref, pd_ref, dh_ref, cat_ref, dmix_ref, dwin_ref, dwout_ref):
        @pl.when(pl.program_id(0) == 0)
        def _():
            dwin_ref[...] = jnp.zeros_like(dwin_ref)
            dwout_ref[...] = jnp.zeros_like(dwout_ref)

        xb = (x_ref[...] * _row(pd_ref, PD_GIN) + _row(pd_ref, PD_BIN)).astype(BF16)
        dwin_ref[...] += _dot_tn(dh_ref[...], xb)
        dwout_ref[...] += _dot_tn(cat_ref[...], dmix_ref[...])

    tok = lambda w: pl.BlockSpec((t, w), lambda i: (i, 0))
    return _call_with_side(
        body, side_ops, name="mixer_bwd_w", grid=(nt,),
        in_specs=[tok(D_MODEL), _layer_spec((8, D_MODEL), l), tok(IN_W), tok(D_MODEL), tok(D_MODEL)],
        out_specs=[_layer_spec((IN_W, D_MODEL), 0), _layer_spec((D_MODEL, D_MODEL), 0)],
        out_shape=[jax.ShapeDtypeStruct((1, IN_W, D_MODEL), F32), jax.ShapeDtypeStruct((1, D_MODEL, D_MODEL), F32)],
        scratch_shapes=[], operands=(xin, pd, dh, cat, dmix), semantics=("arbitrary",))


PF_W0, PF_B = 0, 3


def _ffn_fwd(xh1, pd, wup_t, pf, w_d, l, t, side=None):
    s = xh1.shape[0]
    nt = s // t
    side, side_in, side_out, side_shapes, side_scratch = _side_specs(side)
    ns = len(side)

    def body(xh_ref, pd_ref, wg_ref, wv_ref, pf_ref, wd_ref, *rest):
        shard_refs, rest = rest[:ns], rest[ns:]
        xh2_ref, rstd_ref, upg_ref, upv_ref = rest[:4]
        land_refs, rest = rest[4:4 + ns], rest[4 + ns:]
        fbuf = rest[0]
        i = pl.program_id(0)
        if ns:
            gather_start, gather_wait = _side_gather(shard_refs, land_refs, *rest[1:])
            pl.when(i == 0)(gather_start)

        @pl.when(i == 0)
        def _():
            fbuf[0:FFN_HALO, :] = jnp.zeros((FFN_HALO, FF_P), F32)

        x1 = xh_ref[...] * _row(pd_ref, PD_G1) + _row(pd_ref, PD_B1)
        xb = x1.astype(BF16)
        y = jnp.zeros((t, D_MODEL), F32)
        for hf in range(2):
            cs = slice(hf * FF_H, (hf + 1) * FF_H)
            ug = _dot_nt(xb, wg_ref[cs, :])
            uv = _dot_nt(xb, wv_ref[cs, :])
            upg_ref[:, cs] = ug
            upv_ref[:, cs] = uv
            fbuf[FFN_HALO:FFN_HALO + t, cs] = ug
            gate = jnp.zeros((t, FF_H), F32) + pf_ref[PF_B:PF_B + 1, cs]
            for k in range(FFN_CONV_K):
                off = FFN_HALO - (FFN_CONV_K - 1) + k
                gate = gate + fbuf[off:off + t, cs] * pf_ref[PF_W0 + k:PF_W0 + k + 1, cs]
            fbuf[0:FFN_HALO, cs] = fbuf[t:t + FFN_HALO, cs]
            hm = gate * _sigmoid(gate) * uv
            y = y + _dot(hm.astype(BF16), wd_ref[cs, :])
        xh2, rstd = _ln_fwd(ALPHA * x1 + y)
        xh2_ref[...] = xh2
        rstd_ref[...] = rstd
        if ns:
            pl.when(i == nt - 1)(gather_wait)

    tok = lambda w: pl.BlockSpec((t, w), lambda i: (i, 0))
    return pl.pallas_call(
        body, name="ffn_fwd_gather" if ns else "ffn_fwd", grid=(nt,),
        in_specs=[tok(D_MODEL), _layer_spec((8, D_MODEL), l),
                  _layer_spec((FF_P, D_MODEL), 0, 0, resident=True), _layer_spec((FF_P, D_MODEL), 0, 1, resident=True),
                  _layer_spec((8, FF_P), l), _layer_spec((FF_P, D_MODEL), 0, resident=True)] + side_in,
        out_specs=[tok(D_MODEL), tok(1), tok(FF_P), tok(FF_P)] + side_out,
        out_shape=[jax.ShapeDtypeStruct((s, D_MODEL), F32), jax.ShapeDtypeStruct((s, 1), F32),
                   jax.ShapeDtypeStruct((s, FF_P), F32), jax.ShapeDtypeStruct((s, FF_P), F32)] + side_shapes,
        scratch_shapes=[pltpu.VMEM((t + FFN_HALO, FF_P), F32)] + side_scratch,
        compiler_params=_cparams(("arbitrary",)),
    )(xh1, pd, wup_t, wup_t, pf, w_d, *side)


VF_W0, VF_B = 0, 3


def _ffn_bwd_d(gz_or_target, xh2, rstd2, upg, upv, pd, pf, w_d, wup_t, l, t, last, side_ops=None):
    s = xh2.shape[0]
    nt = s // t
    hb = t // FFN_HALO

    def body(gz_ref, xh2_ref, rstd_ref, upg_ref, halo_ref, upv_ref, pd_ref, pf_ref, wd_ref, wg_ref, wv_ref,
             dx_ref, dy_ref, dug_ref, duv_ref, hm_ref, vd_ref, vf_ref, gbuf, ebuf, s1buf, s2buf):
        i = pl.program_id(0)
        first_tile = i == nt - 1

        @pl.when(i == 0)
        def _():
            vd_ref[...] = jnp.zeros_like(vd_ref)
            vf_ref[...] = jnp.zeros_like(vf_ref)
            ebuf[t:t + FFN_HALO, :] = jnp.zeros((FFN_HALO, FF_P), F32)

        xh2_v = xh2_ref[...]
        if last:
            diff = xh2_v * _row(pd_ref, PD_G2) + _row(pd_ref, PD_B2) - gz_ref[...]
            vd_ref[VD_LOSS:VD_LOSS + 1, :] += _colsum(diff * diff)
            gz_v = diff * (1.0 / D_MODEL)
        else:
            gz_v = gz_ref[...]
        vd_ref[VD_LN_G:VD_LN_G + 1, :] += _colsum(gz_v * xh2_v)
        vd_ref[VD_LN_B:VD_LN_B + 1, :] += _colsum(gz_v)
        dz = _ln_bwd(gz_v * _row(pd_ref, PD_G2), xh2_v, rstd_ref[...])
        dyb = dz.astype(BF16)
        dy_ref[...] = dyb
        dx = ALPHA * dz
        for hf in range(2):
            cs = slice(hf * FF_H, (hf + 1) * FF_H)
            ug = upg_ref[:, cs]
            uv = upv_ref[:, cs]
            halo = halo_ref[:, cs]
            gbuf[0:FFN_HALO, :] = jnp.where(first_tile, jnp.zeros_like(halo), halo)
            gbuf[FFN_HALO:FFN_HALO + t, :] = ug
            s1buf[...] = gbuf[FFN_HALO - 1:FFN_HALO - 1 + t, :]
            s2buf[...] = gbuf[FFN_HALO - 2:FFN_HALO - 2 + t, :]
            ug1 = s1buf[...]
            ug2 = s2buf[...]
            gate = (pf_ref[PF_B:PF_B + 1, cs] + ug2 * pf_ref[PF_W0:PF_W0 + 1, cs] + ug1 * pf_ref[PF_W0 + 1:PF_W0 + 2, cs]
                    + ug * pf_ref[PF_W0 + 2:PF_W0 + 3, cs])
            sig = _sigmoid(gate)
            sl = gate * sig
            hm_ref[:, cs] = sl * uv
            dhm = _dot_nt(dyb, wd_ref[cs, :])
            duv = dhm * sl
            dgate = dhm * uv * (sig * (1.0 + gate * (1.0 - sig)))
            vf_ref[VF_B:VF_B + 1, cs] += _colsum(dgate)
            vf_ref[VF_W0:VF_W0 + 1, cs] += _colsum(dgate * ug2)
            vf_ref[VF_W0 + 1:VF_W0 + 2, cs] += _colsum(dgate * ug1)
            vf_ref[VF_W0 + 2:VF_W0 + 3, cs] += _colsum(dgate * ug)
            ebuf[0:t, cs] = dgate
            dug = (ebuf[2:2 + t, cs] * pf_ref[PF_W0:PF_W0 + 1, cs] + ebuf[1:1 + t, cs] * pf_ref[PF_W0 + 1:PF_W0 + 2, cs]
                   + dgate * pf_ref[PF_W0 + 2:PF_W0 + 3, cs])
            ebuf[t:t + FFN_HALO, cs] = ebuf[0:FFN_HALO, cs]
            dugb = dug.astype(BF16)
            duvb = duv.astype(BF16)
            dug_ref[:, cs] = dugb
            duv_ref[:, cs] = duvb
            dx = dx + _dot(dugb, wg_ref[cs, :]) + _dot(duvb, wv_ref[cs, :])
        dx_ref[...] = dx

        if last:
            @pl.when(i == nt - 1)
            def _():
                tot = jnp.sum(vd_ref[VD_LOSS:VD_LOSS + 1, :], axis=1, keepdims=True)
                vd_ref[VD_LOSS:VD_LOSS + 1, :] = jnp.zeros((1, D_MODEL), F32) + tot

    rev = lambda w: pl.BlockSpec((t, w), lambda i: (nt - 1 - i, 0))
    halo_spec = pl.BlockSpec((FFN_HALO, FF_P), lambda i: (jnp.maximum((nt - 1 - i) * hb - 1, 0), 0))
    out_shape = [jax.ShapeDtypeStruct((s, D_MODEL), F32), jax.ShapeDtypeStruct((s, D_MODEL), BF16),
                 jax.ShapeDtypeStruct((s, FF_P), BF16), jax.ShapeDtypeStruct((s, FF_P), BF16),
                 jax.ShapeDtypeStruct((s, FF_P), F32),
                 jax.ShapeDtypeStruct((8, D_MODEL), F32), jax.ShapeDtypeStruct((8, FF_P), F32)]
    return _call_with_side(
        body, side_ops, name="ffn_bwd_d_last" if last else "ffn_bwd_d", grid=(nt,),
        in_specs=[rev(D_MODEL), rev(D_MODEL), rev(1), rev(FF_P), halo_spec, rev(FF_P),
                  _layer_spec((8, D_MODEL), l), _layer_spec((8, FF_P), l),
                  _layer_spec((FF_P, D_MODEL), 0, resident=True),
                  _layer_spec((FF_P, D_MODEL), 0, 0, resident=True), _layer_spec((FF_P, D_MODEL), 0, 1, resident=True)],
        out_specs=[rev(D_MODEL), rev(D_MODEL), rev(FF_P), rev(FF_P), rev(FF_P),
                   _const_spec((8, D_MODEL)), _const_spec((8, FF_P))],
        out_shape=out_shape,
        scratch_shapes=[pltpu.VMEM((t + FFN_HALO, FF_H), F32), pltpu.VMEM((t + FFN_HALO, FF_P), F32),
                        pltpu.VMEM((t, FF_H), F32), pltpu.VMEM((t, FF_H), F32)],
        operands=(gz_or_target, xh2, rstd2, upg, upg, upv, pd, pf, w_d, wup_t, wup_t), semantics=("arbitrary",))


def _ffn_bwd_w(xh1, pd, dy, dug, duv, hm, l, t, side_ops=None):
    s = xh1.shape[0]
    nt = s // t

    def body(xh_ref, pd_ref, dy_ref, dug_ref, duv_ref, hm_ref, dwup_ref, dwd_ref):
        @pl.when(pl.program_id(1) == 0)
        def _():
            dwup_ref[...] = jnp.zeros_like(dwup_ref)
            dwd_ref[...] = jnp.zeros_like(dwd_ref)

        xb = (xh_ref[...] * _row(pd_ref, PD_G1) + _row(pd_ref, PD_B1)).astype(BF16)
        dwup_ref[0] += _dot_tn(dug_ref[...], xb)
        dwup_ref[1] += _dot_tn(duv_ref[...], xb)
        dwd_ref[...] += _dot_tn(hm_ref[...].astype(BF16), dy_ref[...])

    tok = lambda w: pl.BlockSpec((t, w), lambda c, i: (i, 0))
    half = pl.BlockSpec((t, FF_H), lambda c, i: (i, c))
    return _call_with_side(
        body, side_ops, name="ffn_bwd_w", grid=(2, nt),
        in_specs=[tok(D_MODEL), pl.BlockSpec((None, 8, D_MODEL), lambda c, i: (l, 0, 0)), tok(D_MODEL), half, half, half],
        out_specs=[pl.BlockSpec((None, 2, FF_H, D_MODEL), lambda c, i: (0, 0, c, 0)),
                   pl.BlockSpec((None, FF_H, D_MODEL), lambda c, i: (0, c, 0))],
        out_shape=[jax.ShapeDtypeStruct((1, 2, FF_P, D_MODEL), F32), jax.ShapeDtypeStruct((1, FF_P, D_MODEL), F32)],
        scratch_shapes=[], operands=(xh1, pd, dy, dug, duv, hm), semantics=("arbitrary", "arbitrary"))


def _adamw_math(w, g, m, v):
    nm = ADAM_B1 * m + (1.0 - ADAM_B1) * g
    nv = ADAM_B2 * v + (1.0 - ADAM_B2) * (g * g)
    m_hat = nm / (1.0 - ADAM_B1 ** ADAM_STEP)
    v_hat = nv / (1.0 - ADAM_B2 ** ADAM_STEP)
    return -ADAM_LR * (m_hat / (jnp.sqrt(v_hat) + ADAM_EPS) + ADAM_WD * w), nm, nv


def _adamw(w, g, m, v, name):
    rows, cols = w.shape
    tr = _row_tile(rows, cols * 4)

    def body(w_ref, g_ref, m_ref, v_ref, d_ref, nm_ref, nv_ref):
        d_ref[...], nm_ref[...], nv_ref[...] = _adamw_math(w_ref[...], g_ref[...], m_ref[...], v_ref[...])

    blk = pl.BlockSpec((tr, cols), lambda i: (i, 0))
    sh = jax.ShapeDtypeStruct((rows, cols), F32)
    return pl.pallas_call(body, name=name, grid=(rows // tr,), in_specs=[blk] * 4, out_specs=[blk] * 3,
                          out_shape=[sh, sh, sh], compiler_params=_cparams(("arbitrary",)))(w, g, m, v)


def _adamw_layers(w, gs, m, v, name):
    _, rows, cols = w.shape
    tr = _row_tile(rows, cols * 4)

    def body(w_ref, g0_ref, g1_ref, m_ref, v_ref, g_ref, d_ref, nm_ref, nv_ref):
        g = jnp.where(pl.program_id(0) == 0, g0_ref[...], g1_ref[...])
        g_ref[...] = g
        d_ref[...], nm_ref[...], nv_ref[...] = _adamw_math(w_ref[...], g, m_ref[...], v_ref[...])

    blk3 = pl.BlockSpec((None, tr, cols), lambda l, i: (l, i, 0))
    blk2 = pl.BlockSpec((tr, cols), lambda l, i: (i, 0))
    sh = jax.ShapeDtypeStruct(w.shape, F32)
    return pl.pallas_call(body, name=name, grid=(DEPTH, rows // tr), in_specs=[blk3, blk2, blk2, blk3, blk3],
                          out_specs=[blk3] * 4, out_shape=[sh] * 4,
                          compiler_params=_cparams(("arbitrary", "arbitrary")))(w, gs[0], gs[1], m, v)


def _all_gather_chips(tensors, name):
    n = len(tensors)
    halves = [a.shape[1] // 2 for a in tensors]

    def body(*refs):
        x_refs, out_refs = refs[:n], refs[n:2 * n]
        send_sems, recv_sems, local_sems = refs[2 * n:]
        x, y, c, chips = _my_place()
        me, sibling, mej = (x, y, c), (x, y, 1 - c), 2 * x + y

        def rows(tn, px, py, pc):
            return out_refs[tn].at[:, 2 * px + py, pl.ds(pc * halves[tn], halves[tn]), :]

        def copy(tn, k, block, to, src=None):
            return pltpu.make_async_remote_copy(
                src_ref=rows(tn, *block) if src is None else src, dst_ref=rows(tn, *block),
                send_sem=send_sems.at[tn, k], recv_sem=recv_sems.at[tn, k], device_id=to, device_id_type=MESH)

        mine_src = [x_refs[tn].at[:, pl.ds(c * halves[tn], halves[tn]), :] for tn in range(n)]
        mine = [pltpu.make_async_copy(mine_src[tn], rows(tn, *me), local_sems.at[tn]) for tn in range(n)]
        first = []
        for j, chip in enumerate(chips):
            first += [copy(tn, 1 + j, me, (*chip, c), src=mine_src[tn]) for tn in range(n)]
        first += [copy(tn, 0, me, sibling, src=mine_src[tn]) for tn in range(n)]
        for cp in first + mine:
            cp.start()
        passed = []
        for j, chip in enumerate(chips):
            for tn in range(n):
                copy(tn, 1 + j, (*chip, c), me).wait_recv()
                fwd = copy(tn, 4 + j, (*chip, c), sibling)
                fwd.start()
                passed.append(fwd)
        for tn in range(n):
            copy(tn, 0, sibling, me).wait_recv()
            for j, chip in enumerate(chips):
                copy(tn, 4 + j, (*chip, 1 - c), me).wait_recv()
        for cp in first + passed:
            cp.wait_send()
        for cp in mine:
            cp.wait()

    return pl.pallas_call(
        body, name=name,
        out_shape=[jax.ShapeDtypeStruct((a.shape[0], N_CHIPS) + a.shape[1:], a.dtype) for a in tensors],
        in_specs=[ANY] * n, out_specs=[ANY] * n,
        scratch_shapes=[pltpu.SemaphoreType.DMA((n, 7)), pltpu.SemaphoreType.DMA((n, 7)), pltpu.SemaphoreType.DMA((n,))],
    )(*tensors)


def _swap_op(g5s):
    n = len(g5s)

    def make(ins, outs, scr):
        send_sems, recv_sems = scr
        x, y, c, _ = _my_place()

        def copies():
            return [pltpu.make_async_remote_copy(
                src_ref=ins[tn].at[:, :, 1 - c], dst_ref=outs[tn], send_sem=send_sems.at[tn], recv_sem=recv_sems.at[tn],
                device_id=(x, y, 1 - c), device_id_type=MESH) for tn in range(n)]

        def start():
            for cp in copies():
                cp.start()

        def wait():
            for cp in copies():
                cp.wait()

        return start, wait

    return _SideOp(g5s, [jax.ShapeDtypeStruct(g.shape[:2] + g.shape[3:], g.dtype) for g in g5s],
                   [pltpu.SemaphoreType.DMA((n,)), pltpu.SemaphoreType.DMA((n,))], make)


def _scatter_op(parts):
    n = len(parts)

    def make(ins, outs, scr):
        send_sems, recv_sems = scr
        x, y, c, chips = _my_place()

        def copies():
            return [pltpu.make_async_remote_copy(
                src_ref=ins[tn].at[:, 2 * px + py], dst_ref=outs[tn].at[:, k],
                send_sem=send_sems.at[tn, k], recv_sem=recv_sems.at[tn, k],
                device_id=(px, py, c), device_id_type=MESH) for k, (px, py) in enumerate(chips) for tn in range(n)]

        def start():
            for cp in copies():
                cp.start()

        def wait():
            for cp in copies():
                cp.wait()

        return start, wait

    return _SideOp(parts, [jax.ShapeDtypeStruct((p.shape[0], 3) + p.shape[2:], p.dtype) for p in parts],
                   [pltpu.SemaphoreType.DMA((n, 3)), pltpu.SemaphoreType.DMA((n, 3))], make)


def _sgather_op(fs):
    n = len(fs)

    def make(ins, outs, scr):
        send_sems, recv_sems = scr
        x, y, c, _ = _my_place()

        def copy(tn, dst_half):
            return pltpu.make_async_remote_copy(
                src_ref=outs[tn].at[:, c], dst_ref=outs[tn].at[:, dst_half], send_sem=send_sems.at[tn],
                recv_sem=recv_sems.at[tn], device_id=(x, y, 1 - c), device_id_type=MESH)

        def start():
            for tn in range(n):
                copy(tn, c).start()

        def wait():
            for tn in range(n):
                copy(tn, 1 - c).wait_recv()
                copy(tn, c).wait_send()

        return start, wait

    return _SideOp(fs, [jax.ShapeDtypeStruct(f.shape, f.dtype) for f in fs],
                   [pltpu.SemaphoreType.DMA((n,)), pltpu.SemaphoreType.DMA((n,))], make, aliases={tn: tn for tn in range(n)})


def _run_side_ops(ops, name):
    return _call_with_side(lambda: None, ops, name=name, grid=(1,), in_specs=[], out_specs=[], out_shape=[],
                           scratch_shapes=[], operands=(), semantics=("arbitrary",))[1]


def _all_gather_devices(xs):
    m_per, n = xs.shape

    def body(x_ref, out_ref, send_sems, recv_sems, local_sem):
        x, y, c, chips = _my_place()
        me, sibling = (x, y, c), (x, y, 1 - c)

        def rows(px, py, pc):
            return out_ref.at[pl.ds((4 * px + 2 * py + pc) * m_per, m_per), :]

        def copy(k, block, to, src=None):
            return pltpu.make_async_remote_copy(
                src_ref=rows(*block) if src is None else src, dst_ref=rows(*block),
                send_sem=send_sems.at[k], recv_sem=recv_sems.at[k], device_id=to, device_id_type=MESH)

        mine = pltpu.make_async_copy(x_ref, rows(*me), local_sem)
        mine.start()
        first = [copy(0, me, sibling, src=x_ref)]
        first += [copy(1 + j, me, (*chip, c), src=x_ref) for j, chip in enumerate(chips)]
        for cp in first:
            cp.start()
        passed = [copy(4 + j, (*chip, c), sibling) for j, chip in enumerate(chips)]
        for j, chip in enumerate(chips):
            copy(1 + j, (*chip, c), me).wait_recv()
            passed[j].start()
        copy(0, sibling, me).wait_recv()
        for j, chip in enumerate(chips):
            copy(4 + j, (*chip, 1 - c), me).wait_recv()
        for cp in first + passed:
            cp.wait_send()
        mine.wait()

    return pl.pallas_call(
        body, name="ag_small_grads", out_shape=jax.ShapeDtypeStruct((8 * m_per, n), xs.dtype),
        in_specs=[pl.BlockSpec(memory_space=pltpu.VMEM)], out_specs=pl.BlockSpec(memory_space=pltpu.VMEM),
        scratch_shapes=[pltpu.SemaphoreType.DMA((7,)), pltpu.SemaphoreType.DMA((7,)), pltpu.SemaphoreType.DMA],
        compiler_params=_cparams(),
    )(xs)


def _add_halves(gs, recvs, place):
    n = len(gs)

    def body(place_ref, *refs):
        g_refs, r_refs, o_refs = refs[:n], refs[n:2 * n], refs[2 * n:]
        for tn in range(n):
            o_refs[tn][...] = (g_refs[tn][...] + r_refs[tn][...]).astype(BF16)

    def gspec(g):
        return pl.BlockSpec((None, None, None) + g.shape[3:], lambda l, j, p: (l, j, p[1], 0, 0))

    def rspec(r):
        return pl.BlockSpec((None, None) + r.shape[2:], lambda l, j, p: (l, j, 0, 0))

    grid_spec = pltpu.PrefetchScalarGridSpec(
        num_scalar_prefetch=1, grid=(gs[0].shape[0], N_CHIPS),
        in_specs=[gspec(g) for g in gs] + [rspec(r) for r in recvs], out_specs=[rspec(r) for r in recvs])
    return pl.pallas_call(body, name="rs_add", grid_spec=grid_spec,
                          out_shape=[jax.ShapeDtypeStruct(r.shape, BF16) for r in recvs],
                          compiler_params=_cparams(("arbitrary", "arbitrary")))(place, *gs, *recvs)


def _sum_slots(parts, slots, place):
    n = len(parts)

    def body(place_ref, *refs):
        p_refs, s_refs, o_refs = refs[:n], refs[n:2 * n], refs[2 * n:]
        for tn in range(n):
            acc = p_refs[tn][...].astype(F32)
            for k in range(3):
                acc = acc + s_refs[tn][k].astype(F32)
            o_refs[tn][...] = acc

    def pspec(p):
        return pl.BlockSpec((None, None) + p.shape[2:], lambda l, pl_: (l, pl_[0], 0, 0))

    def sspec(sl):
        return pl.BlockSpec((None,) + sl.shape[1:], lambda l, pl_: (l, 0, 0, 0))

    def ospec(p):
        return pl.BlockSpec((None, None) + p.shape[2:], lambda l, pl_: (l, pl_[1], 0, 0))

    grid_spec = pltpu.PrefetchScalarGridSpec(
        num_scalar_prefetch=1, grid=(parts[0].shape[0],),
        in_specs=[pspec(p) for p in parts] + [sspec(sl) for sl in slots], out_specs=[ospec(p) for p in parts])
    return pl.pallas_call(body, name="rs_sum", grid_spec=grid_spec,
                          out_shape=[jax.ShapeDtypeStruct((p.shape[0], 2) + p.shape[2:], F32) for p in parts],
                          compiler_params=_cparams(("arbitrary",)))(place, *parts, *slots)


def _sum_devices(gathered, m_per):
    def body(g_ref, o_ref):
        acc = g_ref[0:m_per, :]
        for d in range(1, 8):
            acc = acc + g_ref[d * m_per:(d + 1) * m_per, :]
        o_ref[...] = acc

    return pl.pallas_call(body, name="small_sum", out_shape=jax.ShapeDtypeStruct((m_per, LANE), F32),
                          compiler_params=_cparams())(gathered)


def _pad_ff_cols(a):
    lead = a.shape[:-1]
    n = a.shape[-1] // FF_Q
    a = a.reshape(*lead, n, FF_Q)
    a = jnp.pad(a, [(0, 0)] * len(lead) + [(0, 0), (0, FF_QP - FF_Q)])
    return a.reshape(*lead, n * FF_QP)


def _unpad_ff_cols(a):
    lead = a.shape[:-1]
    n = a.shape[-1] // FF_QP
    return a.reshape(*lead, n, FF_QP)[..., :FF_Q].reshape(*lead, n * FF_Q)


def _pack_small(parts):
    flat = jnp.concatenate([p.reshape(-1) for p in parts])
    return flat.reshape(-1, LANE)


SMALL_SHAPES = [("conv_a_w", (CONV_K, CONV_W)), ("conv_a_b", (CONV_W,)), ("ln_a_g", (CONV_W,)), ("ln_a_b", (CONV_W,)),
                ("ln_v_g", (GMLP_W,)), ("ln_v_b", (GMLP_W,)), ("w_s", (6, CHUNK, CHUNK)), ("b_s", (6, CHUNK)),
                ("ln1_g", (D_MODEL,)), ("ln1_b", (D_MODEL,)), ("conv_f_w", (FFN_CONV_K, D_FF)), ("conv_f_b", (D_FF,)),
                ("ln2_g", (D_MODEL,)), ("ln2_b", (D_MODEL,))]


def _unpack_small(flat2d):
    flat = flat2d.reshape(DEPTH, -1)
    out, o = {}, 0
    for name, shp in SMALL_SHAPES:
        n = 1
        for d in shp:
            n *= d
        out[name] = flat[:, o:o + n].reshape((DEPTH,) + shp)
        o += n
    return out


def _rows8(rows):
    blk = jnp.stack(rows, axis=1)
    return jnp.pad(blk, ((0, 0), (0, 8 - len(rows)), (0, 0)))


def kernel(x, mem, w_in, conv_a_w, conv_a_b, ln_a_g, ln_a_b, ln_v_g, ln_v_b, w_s, b_s, w_mk, w_mv, w_out, ln1_g, ln1_b, w_up, conv_f_w, conv_f_b, w_down, ln2_g, ln2_b, loss_target, m_w_in, m_conv_a_w, m_conv_a_b, m_ln_a_g, m_ln_a_b, m_ln_v_g, m_ln_v_b, m_w_s, m_b_s, m_w_mk, m_w_mv, m_w_out, m_ln1_g, m_ln1_b, m_w_up, m_conv_f_w, m_conv_f_b, m_w_down, m_ln2_g, m_ln2_b, v_w_in, v_conv_a_w, v_conv_a_b, v_ln_a_g, v_ln_a_b, v_ln_v_g, v_ln_v_b, v_w_s, v_b_s, v_w_mk, v_w_mv, v_w_out, v_ln1_g, v_ln1_b, v_w_up, v_conv_f_w, v_conv_f_b, v_w_down, v_ln2_g, v_ln2_b):
    seq = x.shape[1]
    t_fwd = min(512, seq)
    t_bwd = min(256, seq)
    chip = 2 * lax.axis_index("x") + lax.axis_index("y")
    core = lax.axis_index("c")
    place = jnp.stack([chip, core]).astype(jnp.int32)
    x0 = x[0]
    mem0 = mem[0]
    target = loss_target[0]

    sh_in = w_in.transpose(0, 2, 1).astype(BF16)
    sh_mk, sh_mv, sh_out = w_mk.astype(BF16), w_mv.astype(BF16), w_out.astype(BF16)
    sh_up = _pad_ff_cols(w_up).transpose(0, 2, 1).astype(BF16)
    sh_dn = jnp.pad(w_down, ((0, 0), (0, FF_QP - FF_Q), (0, 0))).astype(BF16)

    def mixer_weights(g_in, g_mk, g_mv, g_out):
        return dict(win_t=g_in.reshape(1, IN_W, D_MODEL), wmk=g_mk.reshape(1, D_MODEL, XATTN_W),
                    wmv=g_mv.reshape(1, D_MODEL, XATTN_W), wout=g_out.reshape(1, D_MODEL, D_MODEL))

    def ffn_weights(g_up, g_dn):
        return dict(wup_t=g_up.reshape(1, 2, FF_P, D_MODEL), wdown=g_dn.reshape(1, FF_P, D_MODEL))

    n_ca = conv_a_w.size
    small_w = _pack_small([conv_a_w, conv_f_w, jnp.zeros((2 * 80 * LANE - n_ca - conv_f_w.size,), F32)])[None]
    *g_mixer0, small_g = _all_gather_chips([sh_in[:1], sh_mk[:1], sh_mv[:1], sh_out[:1], small_w], "ag_mixer0")
    wts = [mixer_weights(*g_mixer0), None]
    small_g = small_g.reshape(N_CHIPS, -1)
    conv_a_full = small_g[:, :n_ca].reshape(N_CHIPS, DEPTH, CONV_K, CONV_W // 4).transpose(1, 2, 0, 3).reshape(DEPTH, CONV_K, CONV_W)
    conv_f_full = small_g[:, n_ca:n_ca + conv_f_w.size].reshape(N_CHIPS, DEPTH, FFN_CONV_K, FF_Q).transpose(1, 2, 0, 3).reshape(DEPTH, FFN_CONV_K, D_FF)

    tril = jnp.tril(jnp.ones((CHUNK, CHUNK), dtype=bool))
    ws_m = jnp.where(tril, w_s, 0.0)
    wst = ws_m.reshape(DEPTH, 3, 2 * CHUNK, CHUNK).astype(BF16)
    wstt = ws_m.transpose(0, 1, 3, 2).reshape(DEPTH, 3, 2 * CHUNK, CHUNK).astype(BF16)
    bst = jnp.repeat(b_s.transpose(0, 2, 1), HEAD_DIM, axis=2)
    conv_w = jnp.pad(conv_a_full, ((0, 0), (0, CONV_HALO - CONV_K), (0, 0)))
    pa = _rows8([conv_a_b, ln_a_g, ln_a_b, ln_v_g, ln_v_b])
    gin = jnp.concatenate([jnp.ones((1, D_MODEL), F32), ln2_g[:DEPTH - 1]], axis=0)
    bin_ = jnp.concatenate([jnp.zeros((1, D_MODEL), F32), ln2_b[:DEPTH - 1]], axis=0)
    pd = _rows8([gin, bin_, ln1_g, ln1_b, ln2_g, ln2_b])
    pf = jnp.concatenate([_pad_ff_cols(conv_f_full), _pad_ff_cols(conv_f_b)[:, None, :],
                          jnp.zeros((DEPTH, 8 - FFN_CONV_K - 1, FF_P), F32)], axis=1)

    acts = []
    xin = x0
    for l in range(DEPTH):
        w = wts[l]
        kt_all, k_all, v_all, vt_all = _kv_fwd(mem0, w["wmk"], w["wmv"])
        side = [sh_up[0], sh_dn[0]] if l == 0 else None
        xh1, rstd1, h, ac, cat, *landed = _mixer_fwd(xin, pd, w["win_t"], conv_w, pa, wst, bst, kt_all, v_all, w["wout"],
                                                     l, t_fwd, side)
        if l == 0:
            w.update(ffn_weights(*landed))
        side = [sh_in[1], sh_mk[1], sh_mv[1], sh_out[1], sh_up[1], sh_dn[1]] if l == 0 else None
        xh2, rstd2, upg, upv, *landed = _ffn_fwd(xh1, pd, w["wup_t"], pf, w["wdown"], l, t_bwd, side)
        if l == 0:
            wts[1] = {**mixer_weights(*landed[:4]), **ffn_weights(*landed[4:])}
        acts.append(dict(xin=xin, kt_all=kt_all, k_all=k_all, vt_all=vt_all, xh1=xh1, rstd1=rstd1, h=h, ac=ac, cat=cat,
                         xh2=xh2, rstd2=rstd2, upg=upg, upv=upv))
        xin = xh2

    assert DEPTH == 2

    def halves_view(gs):
        return [g.reshape(1, N_CHIPS, 2, g.shape[1] // (2 * N_CHIPS), g.shape[2]) for g in gs]

    small = [None] * DEPTH
    red_layers = [None] * DEPTH
    gz = target
    loss_sum = None
    g5_prev = None
    for l in reversed(range(DEPTH)):
        a, w = acts[l], wts[l]
        last = l == DEPTH - 1
        (dx1, dy, dug, duv, hm, vd2, vf), side = _ffn_bwd_d(
            gz, a["xh2"], a["rstd2"], a["upg"], a["upv"], pd, pf, w["wdown"], w["wup_t"], l, t_bwd, last,
            [_swap_op(g5_prev)] if g5_prev else None)
        if last:
            loss_sum = vd2[VD_LOSS, 0]
        parts_prev = _add_halves(g5_prev, side[0], place) if g5_prev else None
        (gw_up_t, gw_down), side = _ffn_bwd_w(a["xh1"], pd, dy, dug, duv, hm, l, t_fwd,
                                              [_scatter_op(parts_prev)] if g5_prev else None)
        halves_prev = _sum_slots(parts_prev, side[0], place) if g5_prev else None
        g5_ffn = halves_view([gw_up_t.reshape(1, 2 * FF_P, D_MODEL), gw_down])
        ops = ([_sgather_op(halves_prev)] if g5_prev else []) + ([_swap_op(g5_ffn)] if l == 0 else [])
        (dx0, dh, dmix, vd1, va, dcw, dws, dbs, dkt, dv), side = _mixer_bwd_d(
            dx1, a["xh1"], a["rstd1"], a["h"], a["ac"], pd, conv_w, pa, wst, wstt, bst,
            a["kt_all"], a["k_all"], a["vt_all"], w["wout"], w["win_t"], l, t_bwd, ops)
        if g5_prev:
            red_layers[l + 1] = side[0]
        parts_ffn = _add_halves(g5_ffn, side[-1], place) if l == 0 else None
        (gw_in_t, gw_out), side = _mixer_bwd_w(a["xin"], pd, dh, a["cat"], dmix, l, t_fwd,
                                               [_scatter_op(parts_ffn)] if l == 0 else None)
        gw_mk, gw_mv = _kv_bwd(mem0, dkt, dv)
        g5_mix = halves_view([gw_in_t, gw_mk, gw_mv, gw_out])
        if l == 0:
            halves_ffn = _sum_slots(parts_ffn, side[0], place)
            red_ffn, recv_mix = _run_side_ops([_sgather_op(halves_ffn), _swap_op(g5_mix)], "rs_tail_swap")
            parts_mix = _add_halves(g5_mix, recv_mix, place)
            halves_mix = _sum_slots(parts_mix, _run_side_ops([_scatter_op(parts_mix)], "rs_tail_chips")[0], place)
            red_mix = _run_side_ops([_sgather_op(halves_mix)], "rs_tail_gather")[0]
            red_layers[0] = red_mix + red_ffn
        else:
            g5_prev = g5_mix + g5_ffn
        dws6 = jnp.where(tril, dws.reshape(6, CHUNK, CHUNK), 0.0)
        small[l] = [dcw[:CONV_K], va[VA_CONV_B], va[VA_LNA_G], va[VA_LNA_B], va[VA_LNV_G], va[VA_LNV_B], dws6,
                    dbs[:, :6].T, vd1[VD_LN_G], vd1[VD_LN_B],
                    _unpad_ff_cols(vf[VF_W0:VF_W0 + FFN_CONV_K]), _unpad_ff_cols(vf[VF_B]),
                    vd2[VD_LN_G], vd2[VD_LN_B]]
        gz = dx0
    grad_x = gz[None]

    def shard_grads(red):
        r = [f.reshape(-1, f.shape[-1]) for f in red]
        return dict(w_in=r[0].T, w_mk=r[1], w_mv=r[2], w_out=r[3], w_up=_unpad_ff_cols(r[4].T), w_down=r[5][:FF_Q])

    big_grads = [shard_grads(red_layers[l]) for l in range(DEPTH)]

    small_local = _pack_small([q for l in range(DEPTH) for q in small[l]])
    m_small = small_local.shape[0]
    small_red = _sum_devices(_all_gather_devices(small_local), m_small)
    sg = _unpack_small(small_red)
    g_conv_a_w = lax.dynamic_slice_in_dim(sg["conv_a_w"], chip * (CONV_W // 4), CONV_W // 4, axis=2)
    g_conv_f_w = lax.dynamic_slice_in_dim(sg["conv_f_w"], chip * FF_Q, FF_Q, axis=2)

    loss = 0.5 / D_MODEL * lax.psum(loss_sum, ("x", "y", "c"))

    grads = dict(conv_a_w=g_conv_a_w, conv_a_b=sg["conv_a_b"], ln_a_g=sg["ln_a_g"], ln_a_b=sg["ln_a_b"],
                 ln_v_g=sg["ln_v_g"], ln_v_b=sg["ln_v_b"], w_s=sg["w_s"], b_s=sg["b_s"], ln1_g=sg["ln1_g"], ln1_b=sg["ln1_b"],
                 conv_f_w=g_conv_f_w, conv_f_b=sg["conv_f_b"], ln2_g=sg["ln2_g"], ln2_b=sg["ln2_b"])
    weights = dict(w_in=w_in, conv_a_w=conv_a_w, conv_a_b=conv_a_b, ln_a_g=ln_a_g, ln_a_b=ln_a_b, ln_v_g=ln_v_g,
                   ln_v_b=ln_v_b, w_s=w_s, b_s=b_s, w_mk=w_mk, w_mv=w_mv, w_out=w_out, ln1_g=ln1_g, ln1_b=ln1_b,
                   w_up=w_up, conv_f_w=conv_f_w, conv_f_b=conv_f_b, w_down=w_down, ln2_g=ln2_g, ln2_b=ln2_b)
    mom_m = dict(w_in=m_w_in, conv_a_w=m_conv_a_w, conv_a_b=m_conv_a_b, ln_a_g=m_ln_a_g, ln_a_b=m_ln_a_b, ln_v_g=m_ln_v_g,
                 ln_v_b=m_ln_v_b, w_s=m_w_s, b_s=m_b_s, w_mk=m_w_mk, w_mv=m_w_mv, w_out=m_w_out, ln1_g=m_ln1_g,
                 ln1_b=m_ln1_b, w_up=m_w_up, conv_f_w=m_conv_f_w, conv_f_b=m_conv_f_b, w_down=m_w_down, ln2_g=m_ln2_g,
                 ln2_b=m_ln2_b)
    mom_v = dict(w_in=v_w_in, conv_a_w=v_conv_a_w, conv_a_b=v_conv_a_b, ln_a_g=v_ln_a_g, ln_a_b=v_ln_a_b, ln_v_g=v_ln_v_g,
                 ln_v_b=v_ln_v_b, w_s=v_w_s, b_s=v_b_s, w_mk=v_w_mk, w_mv=v_w_mv, w_out=v_w_out, ln1_g=v_ln1_g,
                 ln1_b=v_ln1_b, w_up=v_w_up, conv_f_w=v_conv_f_w, conv_f_b=v_conv_f_b, w_down=v_w_down, ln2_g=v_ln2_g,
                 ln2_b=v_ln2_b)
    names = list(weights)
    big_names = ["w_in", "w_mk", "w_mv", "w_out", "w_up", "w_down"]
    delta, new_m, new_v = {}, {}, {}
    for n in big_names:
        grads[n], delta[n], new_m[n], new_v[n] = _adamw_layers(weights[n], [big_grads[l][n] for l in range(DEPTH)],
                                                               mom_m[n], mom_v[n], "adamw_" + n)
    small_names = [n for n in names if n not in big_names]
    sizes = [weights[n].size for n in small_names]
    pad = (-sum(sizes)) % (8 * LANE)

    def pack(dct, fill):
        return _pack_small([dct[n] for n in small_names] + [jnp.full((pad,), fill, F32)])

    d, nm, nv = _adamw(pack(weights, 0.0), pack(grads, 0.0), pack(mom_m, 0.0), pack(mom_v, 1.0), "adamw_small")
    o = 0
    for n, sz in zip(small_names, sizes):
        shp = weights[n].shape
        delta[n] = d.reshape(-1)[o:o + sz].reshape(shp)
        new_m[n] = nm.reshape(-1)[o:o + sz].reshape(shp)
        new_v[n] = nv.reshape(-1)[o:o + sz].reshape(shp)
        o += sz

    return (loss, grad_x, *[grads[n] for n in names], *[delta[n] for n in names],
            *[new_m[n] for n in names], *[new_v[n] for n in names])
```

```python
import jax
import jax.numpy as jnp
from jax import lax
from jax.experimental import pallas as pl
from jax.experimental.pallas import tpu as pltpu

F32 = jnp.float32
BF16 = jnp.bfloat16

D_MODEL = 1024
DEPTH = 2
CONV_W = 384
GMLP_W = 384
XATTN_W = 256
XATTN_HEADS = 4
HEAD_DIM = 64
IN_W = 1792
CONV_K = 31
CHUNK = 128
N_MEM = 256
D_FF = 2752
FFN_CONV_K = 3
ALPHA = (2.0 * DEPTH) ** 0.25
LN_EPS = 1e-5
ATT_SCALE = 1.0 / 8.0
ADAM_LR, ADAM_B1, ADAM_B2, ADAM_EPS, ADAM_WD, ADAM_STEP = 0.001, 0.9, 0.999, 1e-08, 0.01, 10

N_CHIPS = 4
FF_Q = D_FF // N_CHIPS
FF_QP = 704
FF_H = 2 * FF_QP
FF_P = 4 * FF_QP
LANE = 128
CONV_HALO = 32
FFN_HALO = 8
BF16_ROWS = 16
VMEM_LIMIT = 60 * 1024 * 1024

MESH = pl.DeviceIdType.MESH
ANY = pl.BlockSpec(memory_space=pl.ANY)


def _cparams(sem=None, vmem=VMEM_LIMIT):
    kw = {"vmem_limit_bytes": vmem}
    if sem is not None:
        kw["dimension_semantics"] = sem
    return pltpu.CompilerParams(**kw)


def _row_tile(rows, row_bytes, limit=2 << 20):
    if rows * row_bytes <= limit:
        return rows
    best = None
    for cand in range(BF16_ROWS, rows, BF16_ROWS):
        if rows % cand == 0 and cand * row_bytes <= limit:
            best = cand
    assert best is not None, (rows, row_bytes)
    return best


def _const_spec(shape):
    nd = len(shape)
    return pl.BlockSpec(shape, lambda *_: (0,) * nd)


def _layer_spec(shape, *lead, resident=False):
    nd = len(shape)
    kw = {"pipeline_mode": pl.Buffered(1)} if resident else {}
    return pl.BlockSpec((None,) * len(lead) + tuple(shape), lambda *_: tuple(lead) + (0,) * nd, **kw)


def _sigmoid(x):
    return jax.nn.sigmoid(x)


def _gelu(x):
    return jax.nn.gelu(x)


def _gelu_grad(x):
    c = 0.7978845608028654
    a = 0.044715
    t = jnp.tanh(c * (x + a * x * x * x))
    return 0.5 * (1.0 + t) + 0.5 * x * (1.0 - t * t) * c * (1.0 + 3.0 * a * x * x)


def _ln_fwd(z):
    mu = jnp.mean(z, axis=-1, keepdims=True)
    zc = z - mu
    var = jnp.mean(zc * zc, axis=-1, keepdims=True)
    rstd = lax.rsqrt(var + LN_EPS)
    return zc * rstd, rstd


def _ln_bwd(dxh, xh, rstd):
    m1 = jnp.mean(dxh, axis=-1, keepdims=True)
    m2 = jnp.mean(dxh * xh, axis=-1, keepdims=True)
    return rstd * (dxh - m1 - xh * m2)


def _colsum(a):
    return jnp.sum(a, axis=0, keepdims=True)


def _dot(a, b):
    return jnp.dot(a, b, preferred_element_type=F32)


def _dot_tn(a, b):
    return lax.dot_general(a, b, (((0,), (0,)), ((), ())), preferred_element_type=F32)


def _dot_nt(a, b):
    return lax.dot_general(a, b, (((1,), (1,)), ((), ())), preferred_element_type=F32)


def _softmax_heads(sc):
    ps = []
    for hd in range(XATTN_HEADS):
        s = sc[:, hd * N_MEM:(hd + 1) * N_MEM]
        e = jnp.exp(s - jnp.max(s, axis=-1, keepdims=True))
        ps.append(e / jnp.sum(e, axis=-1, keepdims=True))
    return jnp.concatenate(ps, axis=1)


def _lane_lo(shape):
    return (lax.broadcasted_iota(jnp.int32, shape, len(shape) - 1) % LANE) < HEAD_DIM


def _spatial_mix(vnb, wst_ref, bst_ref, mix_ref, t):
    lo = _lane_lo((CHUNK, LANE))
    for n in range(t // CHUNK):
        rows = slice(n * CHUNK, (n + 1) * CHUNK)
        for j in range(GMLP_W // LANE):
            cols = slice(j * LANE, (j + 1) * LANE)
            r = _dot(wst_ref[j], vnb[rows, cols])
            mix_ref[rows, cols] = jnp.where(lo, r[:CHUNK], r[CHUNK:]) + bst_ref[:, cols]


def _kv_fwd(mem, w_mk, w_mv):
    def body(mem_ref, wk_ref, wv_ref, kt_ref, k_ref, v_ref, vt_ref):
        mb = mem_ref[...].astype(BF16)
        k = _dot(mb, wk_ref[...])
        v = _dot(mb, wv_ref[...])
        col = lax.broadcasted_iota(jnp.int32, (N_MEM, XATTN_W), 1) // HEAD_DIM
        ks = [jnp.where(col == hd, k, 0.0) for hd in range(XATTN_HEADS)]
        vs = [jnp.where(col == hd, v, 0.0) for hd in range(XATTN_HEADS)]
        k_ref[...] = jnp.concatenate(ks, axis=0).astype(BF16)
        v_ref[...] = jnp.concatenate(vs, axis=0).astype(BF16)
        kt_ref[...] = jnp.concatenate([x.T for x in ks], axis=1).astype(BF16)
        vt_ref[...] = jnp.concatenate([x.T for x in vs], axis=1).astype(BF16)

    wide = jax.ShapeDtypeStruct((XATTN_W, XATTN_HEADS * N_MEM), BF16)
    tall = jax.ShapeDtypeStruct((XATTN_HEADS * N_MEM, XATTN_W), BF16)
    wspec = _layer_spec((D_MODEL, XATTN_W), 0)
    return pl.pallas_call(body, name="kv_fwd", grid=(1,),
                          in_specs=[_const_spec((N_MEM, D_MODEL)), wspec, wspec],
                          out_specs=[_const_spec(wide.shape), _const_spec(tall.shape), _const_spec(tall.shape),
                                     _const_spec(wide.shape)],
                          out_shape=(wide, tall, tall, wide), compiler_params=_cparams(("arbitrary",)))(mem, w_mk, w_mv)


def _kv_bwd(mem, dkt_all, dv_all):
    def body(mem_ref, dkt_ref, dv_ref, gk_ref, gv_ref):
        col = lax.broadcasted_iota(jnp.int32, (N_MEM, XATTN_W), 1) // HEAD_DIM
        dk = jnp.zeros((N_MEM, XATTN_W), F32)
        dv = jnp.zeros((N_MEM, XATTN_W), F32)
        for hd in range(XATTN_HEADS):
            dk = dk + jnp.where(col == hd, dkt_ref[:, hd * N_MEM:(hd + 1) * N_MEM].T, 0.0)
            dv = dv + jnp.where(col == hd, dv_ref[hd * N_MEM:(hd + 1) * N_MEM, :], 0.0)
        mb = mem_ref[...].astype(BF16)
        gk_ref[0] = _dot_tn(mb, dk.astype(BF16))
        gv_ref[0] = _dot_tn(mb, dv.astype(BF16))

    out = jax.ShapeDtypeStruct((1, D_MODEL, XATTN_W), F32)
    return pl.pallas_call(body, name="kv_bwd", out_shape=(out, out), compiler_params=_cparams())(mem, dkt_all, dv_all)


def _my_place():
    x, y, c = lax.axis_index("x"), lax.axis_index("y"), lax.axis_index("c")
    chips = [(1 - x, y), (x, 1 - y), (1 - x, 1 - y)]
    return x, y, c, chips


def _side_gather(shard_refs, land_refs, send_sems, recv_sems, local_sems):
    n = len(shard_refs)
    x, y, c, chips = _my_place()
    mej = 2 * x + y

    def remote(tn, k, slot):
        px, py = chips[k]
        return pltpu.make_async_remote_copy(
            src_ref=shard_refs[tn], dst_ref=land_refs[tn].at[slot], send_sem=send_sems.at[tn, k],
            recv_sem=recv_sems.at[tn, k], device_id=(px, py, c), device_id_type=MESH)

    def local(tn):
        return pltpu.make_async_copy(shard_refs[tn], land_refs[tn].at[mej], local_sems.at[tn])

    def start():
        for k in range(3):
            for tn in range(n):
                remote(tn, k, mej).start()
        for tn in range(n):
            local(tn).start()

    def wait():
        for k, (px, py) in enumerate(chips):
            for tn in range(n):
                remote(tn, k, 2 * px + py).wait_recv()
                remote(tn, k, mej).wait_send()
        for tn in range(n):
            local(tn).wait()

    return start, wait


def _side_specs(side):
    side = list(side or ())
    n = len(side)
    shapes = [jax.ShapeDtypeStruct((N_CHIPS,) + a.shape, a.dtype) for a in side]
    scratch = [pltpu.SemaphoreType.DMA((n, 3)), pltpu.SemaphoreType.DMA((n, 3)), pltpu.SemaphoreType.DMA((n,))] if n else []
    return side, [ANY] * n, [ANY] * n, shapes, scratch


class _SideOp:
    def __init__(self, ins, out_shapes, scratch, make, aliases=None):
        self.ins, self.out_shapes, self.scratch, self.make, self.aliases = list(ins), list(out_shapes), list(scratch), make, dict(aliases or {})


def _call_with_side(body, side_ops, *, name, grid, in_specs, out_specs, out_shape, scratch_shapes, operands, semantics):
    side_ops = list(side_ops or ())
    n_in, n_out, n_scr = len(in_specs), len(out_specs), len(scratch_shapes)
    s_ins = [a for op in side_ops for a in op.ins]
    s_outs = [o for op in side_ops for o in op.out_shapes]
    s_scr = [x for op in side_ops for x in op.scratch]
    aliases, oi, oo = {}, 0, 0
    for op in side_ops:
        for a, b in op.aliases.items():
            aliases[n_in + oi + a] = n_out + oo + b
        oi, oo = oi + len(op.ins), oo + len(op.out_shapes)

    def wrapped(*refs):
        ins, sins = refs[:n_in], refs[n_in:n_in + len(s_ins)]
        base = n_in + len(s_ins)
        outs, souts = refs[base:base + n_out], refs[base + n_out:base + n_out + len(s_outs)]
        base += n_out + len(s_outs)
        scr, sscr = refs[base:base + n_scr], refs[base + n_scr:]
        if side_ops:
            first = pl.program_id(0) == 0
            last = pl.program_id(0) == grid[0] - 1
            for d in range(1, len(grid)):
                first = jnp.logical_and(first, pl.program_id(d) == 0)
                last = jnp.logical_and(last, pl.program_id(d) == grid[d] - 1)
            hooks, a, b, c = [], 0, 0, 0
            for op in side_ops:
                hooks.append(op.make(sins[a:a + len(op.ins)], souts[b:b + len(op.out_shapes)], sscr[c:c + len(op.scratch)]))
                a, b, c = a + len(op.ins), b + len(op.out_shapes), c + len(op.scratch)

            @pl.when(first)
            def _():
                for start, _w in hooks:
                    start()

        body(*ins, *outs, *scr)
        if side_ops:
            @pl.when(last)
            def _():
                for _s, wait in hooks:
                    wait()

    res = pl.pallas_call(
        wrapped, name=name, grid=grid, in_specs=list(in_specs) + [ANY] * len(s_ins),
        out_specs=list(out_specs) + [ANY] * len(s_outs), out_shape=list(out_shape) + s_outs,
        scratch_shapes=list(scratch_shapes) + s_scr, input_output_aliases=aliases,
        compiler_params=_cparams(semantics),
    )(*operands, *s_ins)
    side_res, k = [], n_out
    for op in side_ops:
        side_res.append(list(res[k:k + len(op.out_shapes)]))
        k += len(op.out_shapes)
    return list(res[:n_out]), side_res


PA_CONV_B, PA_LNA_G, PA_LNA_B, PA_LNV_G, PA_LNV_B = 0, 1, 2, 3, 4
PD_GIN, PD_BIN, PD_G1, PD_B1, PD_G2, PD_B2 = 0, 1, 2, 3, 4, 5


def _row(ref, r):
    return ref[r:r + 1, :]


def _mixer_fwd(xin, pd, win_t, conv_w, pa, wst, bst, kt_all, v_all, w_out, l, t, side=None):
    s = xin.shape[0]
    nt = s // t
    side, side_in, side_out, side_shapes, side_scratch = _side_specs(side)
    ns = len(side)

    def body(x_ref, pd_ref, wint_ref, cw_ref, pa_ref, wst_ref, bst_ref, kt_ref, v_ref, wout_ref, *rest):
        shard_refs, rest = rest[:ns], rest[ns:]
        xh_ref, rstd_ref, h_ref, ac_ref, cat_ref = rest[:5]
        land_refs, rest = rest[5:5 + ns], rest[5 + ns:]
        cbuf, mixbuf, zbuf = rest[:3]
        i = pl.program_id(0)
        if ns:
            gather_start, gather_wait = _side_gather(shard_refs, land_refs, *rest[3:])
            pl.when(i == 0)(gather_start)
        x = x_ref[...] * _row(pd_ref, PD_GIN) + _row(pd_ref, PD_BIN)
        h = _dot_nt(x.astype(BF16), wint_ref[...])
        h_ref[...] = h
        a1, a2 = h[:, 0:CONV_W], h[:, CONV_W:2 * CONV_W]
        hu, hv = h[:, 2 * CONV_W:2 * CONV_W + GMLP_W], h[:, 2 * CONV_W + GMLP_W:2 * CONV_W + 2 * GMLP_W]
        q = h[:, IN_W - XATTN_W:]

        @pl.when(i == 0)
        def _():
            cbuf[0:CONV_HALO, :] = jnp.zeros((CONV_HALO, CONV_W), F32)

        cbuf[CONV_HALO:CONV_HALO + t, :] = a1 * _sigmoid(a2)
        ac = jnp.zeros((t, CONV_W), F32) + _row(pa_ref, PA_CONV_B)
        for r in range(8):
            zr = jnp.zeros((t + 8, CONV_W), F32)
            for a in range(4):
                o = 8 * a + r
                if o < CONV_K:
                    k = CONV_K - 1 - o
                    zr = zr + cbuf[CONV_HALO - 8 - 8 * a:CONV_HALO - 8 - 8 * a + t + 8, :] * cw_ref[k:k + 1, :]
            if r == 0:
                ac = ac + zr[8:, :]
            else:
                zbuf[...] = zr
                ac = ac + zbuf[8 - r:8 - r + t, :]
        ac_ref[...] = ac
        cbuf[0:CONV_HALO, :] = cbuf[t:t + CONV_HALO, :]
        xh_a, _ = _ln_fwd(ac)
        an = xh_a * _row(pa_ref, PA_LNA_G) + _row(pa_ref, PA_LNA_B)
        a = an * _sigmoid(an)

        u = _gelu(hu)
        xh_v, _ = _ln_fwd(_gelu(hv))
        vn = xh_v * _row(pa_ref, PA_LNV_G) + _row(pa_ref, PA_LNV_B)
        _spatial_mix(vn.astype(BF16), wst_ref, bst_ref, mixbuf, t)
        g = u * mixbuf[...]

        p = _softmax_heads(_dot(q.astype(BF16), kt_ref[...]) * ATT_SCALE)
        o = _dot(p.astype(BF16), v_ref[...])

        cat = jnp.concatenate([a, g, o], axis=1).astype(BF16)
        cat_ref[...] = cat
        z = ALPHA * x + _dot(cat, wout_ref[...])
        xh, rstd = _ln_fwd(z)
        xh_ref[...] = xh
        rstd_ref[...] = rstd
        if ns:
            pl.when(i == nt - 1)(gather_wait)

    tok = lambda w: pl.BlockSpec((t, w), lambda i: (i, 0))
    return pl.pallas_call(
        body, name="mixer_fwd_gather" if ns else "mixer_fwd", grid=(nt,),
        in_specs=[tok(D_MODEL), _layer_spec((8, D_MODEL), l), _layer_spec((IN_W, D_MODEL), 0, resident=True),
                  _layer_spec((CONV_HALO, CONV_W), l), _layer_spec((8, CONV_W), l),
                  _layer_spec((3, 2 * CHUNK, CHUNK), l), _layer_spec((CHUNK, GMLP_W), l),
                  _const_spec((XATTN_W, XATTN_HEADS * N_MEM)), _const_spec((XATTN_HEADS * N_MEM, XATTN_W)),
                  _layer_spec((D_MODEL, D_MODEL), 0, resident=True)] + side_in,
        out_specs=[tok(D_MODEL), tok(1), tok(IN_W), tok(CONV_W), tok(D_MODEL)] + side_out,
        out_shape=[jax.ShapeDtypeStruct((s, D_MODEL), F32), jax.ShapeDtypeStruct((s, 1), F32),
                   jax.ShapeDtypeStruct((s, IN_W), F32), jax.ShapeDtypeStruct((s, CONV_W), F32),
                   jax.ShapeDtypeStruct((s, D_MODEL), BF16)] + side_shapes,
        scratch_shapes=[pltpu.VMEM((t + CONV_HALO, CONV_W), F32), pltpu.VMEM((t, GMLP_W), F32),
                        pltpu.VMEM((t + 8, CONV_W), F32)] + side_scratch,
        compiler_params=_cparams(("arbitrary",)),
    )(xin, pd, win_t, conv_w, pa, wst, bst, kt_all, v_all, w_out, *side)


VD_LN_G, VD_LN_B, VD_LOSS = 0, 1, 2
VA_CONV_B, VA_LNA_G, VA_LNA_B, VA_LNV_G, VA_LNV_B = 0, 1, 2, 3, 4


def _mixer_bwd_d(gz, xh1, rstd1, h, ac, pd, conv_w, pa, wst, wstt, bst, kt_all, k_all, vt_all, w_out, win_t, l, t, side_ops=None):
    s = gz.shape[0]
    nt = s // t

    def body(gz_ref, xh_ref, rstd_ref, h_ref, ac_ref, pd_ref, cw_ref, pa_ref, wst_ref, wstt_ref, bst_ref,
             kt_ref, k_ref, vt_ref, wout_ref, wint_ref,
             dx_ref, dh_ref, dmix_ref, vd_ref, va_ref, dcw_ref, dws_ref, dbs_ref, dkt_ref, dv_ref,
             ebuf, mixbuf, dvnbuf, dbsacc, erbuf):
        i = pl.program_id(0)

        @pl.when(i == 0)
        def _():
            vd_ref[...] = jnp.zeros_like(vd_ref)
            va_ref[...] = jnp.zeros_like(va_ref)
            dcw_ref[...] = jnp.zeros_like(dcw_ref)
            dws_ref[...] = jnp.zeros_like(dws_ref)
            dbs_ref[...] = jnp.zeros_like(dbs_ref)
            dkt_ref[...] = jnp.zeros_like(dkt_ref)
            dv_ref[...] = jnp.zeros_like(dv_ref)
            dbsacc[...] = jnp.zeros_like(dbsacc)
            ebuf[t:t + CONV_HALO, :] = jnp.zeros((CONV_HALO, CONV_W), F32)

        gz_v = gz_ref[...]
        xh = xh_ref[...]
        vd_ref[VD_LN_G:VD_LN_G + 1, :] += _colsum(gz_v * xh)
        vd_ref[VD_LN_B:VD_LN_B + 1, :] += _colsum(gz_v)
        dz = _ln_bwd(gz_v * _row(pd_ref, PD_G1), xh, rstd_ref[...])
        dzb = dz.astype(BF16)
        dmix_ref[...] = dzb
        dcat = _dot_nt(dzb, wout_ref[...])
        d_a, d_g, d_o = dcat[:, 0:CONV_W], dcat[:, CONV_W:CONV_W + GMLP_W], dcat[:, CONV_W + GMLP_W:]

        h = h_ref[...]
        a1, a2 = h[:, 0:CONV_W], h[:, CONV_W:2 * CONV_W]
        hu, hv = h[:, 2 * CONV_W:2 * CONV_W + GMLP_W], h[:, 2 * CONV_W + GMLP_W:2 * CONV_W + 2 * GMLP_W]
        q = h[:, IN_W - XATTN_W:]

        xh_a, rstd_a = _ln_fwd(ac_ref[...])
        an = xh_a * _row(pa_ref, PA_LNA_G) + _row(pa_ref, PA_LNA_B)
        sig = _sigmoid(an)
        d_an = d_a * (sig * (1.0 + an * (1.0 - sig)))
        va_ref[VA_LNA_G:VA_LNA_G + 1, :] += _colsum(d_an * xh_a)
        va_ref[VA_LNA_B:VA_LNA_B + 1, :] += _colsum(d_an)
        dac = _ln_bwd(d_an * _row(pa_ref, PA_LNA_G), xh_a, rstd_a)
        va_ref[VA_CONV_B:VA_CONV_B + 1, :] += _colsum(dac)
        ebuf[0:t, :] = dac
        sg = _sigmoid(a2)
        glu = a1 * sg
        dglu = jnp.zeros((t, CONV_W), F32)
        for r in range(8):
            if r > 0:
                erbuf[...] = ebuf[r:r + t + 24, :]
            src = ebuf if r == 0 else erbuf
            for a in range(4):
                o = 8 * a + r
                if o < CONV_K:
                    k = CONV_K - 1 - o
                    ek = src[8 * a:8 * a + t, :]
                    dglu = dglu + ek * cw_ref[k:k + 1, :]
                    dcw_ref[k:k + 1, :] += _colsum(ek * glu)
        ebuf[t:t + CONV_HALO, :] = ebuf[0:CONV_HALO, :]
        da1 = dglu * sg
        da2 = dglu * a1 * sg * (1.0 - sg)

        u = _gelu(hu)
        xh_v, rstd_v = _ln_fwd(_gelu(hv))
        vn = xh_v * _row(pa_ref, PA_LNV_G) + _row(pa_ref, PA_LNV_B)
        vnb = vn.astype(BF16)
        _spatial_mix(vnb, wst_ref, bst_ref, mixbuf, t)
        dhu = d_g * mixbuf[...] * _gelu_grad(hu)
        dm = d_g * u
        dmb = dm.astype(BF16)
        lo = _lane_lo((CHUNK, LANE))
        for n in range(t // CHUNK):
            rows = slice(n * CHUNK, (n + 1) * CHUNK)
            dbsacc[...] += dm[rows, :]
            for j in range(GMLP_W // LANE):
                cols = slice(j * LANE, (j + 1) * LANE)
                dm_blk = dmb[rows, cols]
                r = _dot(wstt_ref[j], dm_blk)
                dvnbuf[rows, cols] = jnp.where(lo, r[:CHUNK], r[CHUNK:])
                zero = jnp.zeros_like(dm_blk)
                st = jnp.concatenate([jnp.where(lo, dm_blk, zero), jnp.where(lo, zero, dm_blk)], axis=0)
                dws_ref[j] += _dot_nt(st, vnb[rows, cols])
        dvn = dvnbuf[...]
        va_ref[VA_LNV_G:VA_LNV_G + 1, :] += _colsum(dvn * xh_v)
        va_ref[VA_LNV_B:VA_LNV_B + 1, :] += _colsum(dvn)
        dhv = _ln_bwd(dvn * _row(pa_ref, PA_LNV_G), xh_v, rstd_v) * _gelu_grad(hv)

        qb = q.astype(BF16)
        p = _softmax_heads(_dot(qb, kt_ref[...]) * ATT_SCALE)
        dob = d_o.astype(BF16)
        dp = _dot(dob, vt_ref[...])
        dss = []
        for hd in range(XATTN_HEADS):
            cs = slice(hd * N_MEM, (hd + 1) * N_MEM)
            ph, dph = p[:, cs], dp[:, cs]
            dss.append(ph * (dph - jnp.sum(ph * dph, axis=-1, keepdims=True)) * ATT_SCALE)
        dsb = jnp.concatenate(dss, axis=1).astype(BF16)
        dq = _dot(dsb, k_ref[...])
        dkt_ref[...] += _dot_tn(qb, dsb)
        dv_ref[...] += _dot_tn(p.astype(BF16), dob)

        dhb = jnp.concatenate([da1, da2, dhu, dhv, dq], axis=1).astype(BF16)
        dh_ref[...] = dhb
        dx_ref[...] = ALPHA * dz + _dot(dhb, wint_ref[...])

        @pl.when(i == nt - 1)
        def _():
            acc = dbsacc[...]
            head = lax.broadcasted_iota(jnp.int32, (CHUNK, GMLP_W), 1) // HEAD_DIM
            lane = lax.broadcasted_iota(jnp.int32, (CHUNK, LANE), 1)
            out = jnp.zeros((CHUNK, LANE), F32)
            for hd in range(GMLP_W // HEAD_DIM):
                sh = jnp.sum(jnp.where(head == hd, acc, 0.0), axis=1, keepdims=True)
                out = out + jnp.where(lane == hd, sh, 0.0)
            dbs_ref[...] = out

    rev = lambda w: pl.BlockSpec((t, w), lambda i: (nt - 1 - i, 0))
    out_shape = [
        jax.ShapeDtypeStruct((s, D_MODEL), F32), jax.ShapeDtypeStruct((s, IN_W), BF16),
        jax.ShapeDtypeStruct((s, D_MODEL), BF16),
        jax.ShapeDtypeStruct((8, D_MODEL), F32), jax.ShapeDtypeStruct((8, CONV_W), F32),
        jax.ShapeDtypeStruct((CONV_HALO, CONV_W), F32), jax.ShapeDtypeStruct((3, 2 * CHUNK, CHUNK), F32),
        jax.ShapeDtypeStruct((CHUNK, LANE), F32),
        jax.ShapeDtypeStruct((XATTN_W, XATTN_HEADS * N_MEM), F32), jax.ShapeDtypeStruct((XATTN_HEADS * N_MEM, XATTN_W), F32),
    ]
    out_specs = [rev(D_MODEL), rev(IN_W), rev(D_MODEL)] + [_const_spec(o.shape) for o in out_shape[3:]]
    return _call_with_side(
        body, side_ops, name="mixer_bwd_d", grid=(nt,),
        in_specs=[rev(D_MODEL), rev(D_MODEL), rev(1), rev(IN_W), rev(CONV_W),
                  _layer_spec((8, D_MODEL), l), _layer_spec((CONV_HALO, CONV_W), l), _layer_spec((8, CONV_W), l),
                  _layer_spec((3, 2 * CHUNK, CHUNK), l), _layer_spec((3, 2 * CHUNK, CHUNK), l),
                  _layer_spec((CHUNK, GMLP_W), l),
                  _const_spec((XATTN_W, XATTN_HEADS * N_MEM)), _const_spec((XATTN_HEADS * N_MEM, XATTN_W)),
                  _const_spec((XATTN_W, XATTN_HEADS * N_MEM)),
                  _layer_spec((D_MODEL, D_MODEL), 0, resident=True), _layer_spec((IN_W, D_MODEL), 0, resident=True)],
        out_specs=out_specs, out_shape=out_shape,
        scratch_shapes=[pltpu.VMEM((t + CONV_HALO, CONV_W), F32), pltpu.VMEM((t, GMLP_W), F32),
                        pltpu.VMEM((t, GMLP_W), F32), pltpu.VMEM((CHUNK, GMLP_W), F32),
                        pltpu.VMEM((t + 24, CONV_W), F32)],
        operands=(gz, xh1, rstd1, h, ac, pd, conv_w, pa, wst, wstt, bst, kt_all, k_all, vt_all, w_out, win_t),
        semantics=("arbitrary",))


def _mixer_bwd_w(xin, pd, dh, cat, dmix, l, t, side_ops=None):
    s = xin.shape[0]
    nt = s // t

    def body(x_ref, pd_ref, dh_ref, cat_ref, dmix_ref, dwin_ref, dwout_ref):
        @pl.when(pl.program_id(0) == 0)
        def _():
            dwin_ref[...] = jnp.zeros_like(dwin_ref)
            dwout_ref[...] = jnp.zeros_like(dwout_ref)

        xb = (x_ref[...] * _row(pd_ref, PD_GIN) + _row(pd_ref, PD_BIN)).astype(BF16)
        dwin_ref[...] += _dot_tn(dh_ref[...], xb)
        dwout_ref[...] += _dot_tn(cat_ref[...], dmix_ref[...])

    tok = lambda w: pl.BlockSpec((t, w), lambda i: (i, 0))
    return _call_with_side(
        body, side_ops, name="mixer_bwd_w", grid=(nt,),
        in_specs=[tok(D_MODEL), _layer_spec((8, D_MODEL), l), tok(IN_W), tok(D_MODEL), tok(D_MODEL)],
        out_specs=[_layer_spec((IN_W, D_MODEL), 0), _layer_spec((D_MODEL, D_MODEL), 0)],
        out_shape=[jax.ShapeDtypeStruct((1, IN_W, D_MODEL), F32), jax.ShapeDtypeStruct((1, D_MODEL, D_MODEL), F32)],
        scratch_shapes=[], operands=(xin, pd, dh, cat, dmix), semantics=("arbitrary",))


PF_W0, PF_B = 0, 3


def _ffn_fwd(xh1, pd, wup_t, pf, w_d, l, t, side=None):
    s = xh1.shape[0]
    nt = s // t
    side, side_in, side_out, side_shapes, side_scratch = _side_specs(side)
    ns = len(side)

    def body(xh_ref, pd_ref, wg_ref, wv_ref, pf_ref, wd_ref, *rest):
        shard_refs, rest = rest[:ns], rest[ns:]
        xh2_ref, rstd_ref, upg_ref, upv_ref = rest[:4]
        land_refs, rest = rest[4:4 + ns], rest[4 + ns:]
        fbuf = rest[0]
        i = pl.program_id(0)
        if ns:
            gather_start, gather_wait = _side_gather(shard_refs, land_refs, *rest[1:])
            pl.when(i == 0)(gather_start)

        @pl.when(i == 0)
        def _():
            fbuf[0:FFN_HALO, :] = jnp.zeros((FFN_HALO, FF_P), F32)

        x1 = xh_ref[...] * _row(pd_ref, PD_G1) + _row(pd_ref, PD_B1)
        xb = x1.astype(BF16)
        y = jnp.zeros((t, D_MODEL), F32)
        for hf in range(2):
            cs = slice(hf * FF_H, (hf + 1) * FF_H)
            ug = _dot_nt(xb, wg_ref[cs, :])
            uv = _dot_nt(xb, wv_ref[cs, :])
            upg_ref[:, cs] = ug
            upv_ref[:, cs] = uv
            fbuf[FFN_HALO:FFN_HALO + t, cs] = ug
            gate = jnp.zeros((t, FF_H), F32) + pf_ref[PF_B:PF_B + 1, cs]
            for k in range(FFN_CONV_K):
                off = FFN_HALO - (FFN_CONV_K - 1) + k
                gate = gate + fbuf[off:off + t, cs] * pf_ref[PF_W0 + k:PF_W0 + k + 1, cs]
            fbuf[0:FFN_HALO, cs] = fbuf[t:t + FFN_HALO, cs]
            hm = gate * _sigmoid(gate) * uv
            y = y + _dot(hm.astype(BF16), wd_ref[cs, :])
        xh2, rstd = _ln_fwd(ALPHA * x1 + y)
        xh2_ref[...] = xh2
        rstd_ref[...] = rstd
        if ns:
            pl.when(i == nt - 1)(gather_wait)

    tok = lambda w: pl.BlockSpec((t, w), lambda i: (i, 0))
    return pl.pallas_call(
        body, name="ffn_fwd_gather" if ns else "ffn_fwd", grid=(nt,),
        in_specs=[tok(D_MODEL), _layer_spec((8, D_MODEL), l),
                  _layer_spec((FF_P, D_MODEL), 0, 0, resident=True), _layer_spec((FF_P, D_MODEL), 0, 1, resident=True),
                  _layer_spec((8, FF_P), l), _layer_spec((FF_P, D_MODEL), 0, resident=True)] + side_in,
        out_specs=[tok(D_MODEL), tok(1), tok(FF_P), tok(FF_P)] + side_out,
        out_shape=[jax.ShapeDtypeStruct((s, D_MODEL), F32), jax.ShapeDtypeStruct((s, 1), F32),
                   jax.ShapeDtypeStruct((s, FF_P), F32), jax.ShapeDtypeStruct((s, FF_P), F32)] + side_shapes,
        scratch_shapes=[pltpu.VMEM((t + FFN_HALO, FF_P), F32)] + side_scratch,
        compiler_params=_cparams(("arbitrary",)),
    )(xh1, pd, wup_t, wup_t, pf, w_d, *side)


VF_W0, VF_B = 0, 3


def _ffn_bwd_d(gz_or_target, xh2, rstd2, upg, upv, pd, pf, w_d, wup_t, l, t, last, side_ops=None):
    s = xh2.shape[0]
    nt = s // t
    hb = t // FFN_HALO

    def body(gz_ref, xh2_ref, rstd_ref, upg_ref, halo_ref, upv_ref, pd_ref, pf_ref, wd_ref, wg_ref, wv_ref,
             dx_ref, dy_ref, dug_ref, duv_ref, hm_ref, vd_ref, vf_ref, gbuf, ebuf, s1buf, s2buf):
        i = pl.program_id(0)
        first_tile = i == nt - 1

        @pl.when(i == 0)
        def _():
            vd_ref[...] = jnp.zeros_like(vd_ref)
            vf_ref[...] = jnp.zeros_like(vf_ref)
            ebuf[t:t + FFN_HALO, :] = jnp.zeros((FFN_HALO, FF_P), F32)

        xh2_v = xh2_ref[...]
        if last:
            diff = xh2_v * _row(pd_ref, PD_G2) + _row(pd_ref, PD_B2) - gz_ref[...]
            vd_ref[VD_LOSS:VD_LOSS + 1, :] += _colsum(diff * diff)
            gz_v = diff * (1.0 / D_MODEL)
        else:
            gz_v = gz_ref[...]
        vd_ref[VD_LN_G:VD_LN_G + 1, :] += _colsum(gz_v * xh2_v)
        vd_ref[VD_LN_B:VD_LN_B + 1, :] += _colsum(gz_v)
        dz = _ln_bwd(gz_v * _row(pd_ref, PD_G2), xh2_v, rstd_ref[...])
        dyb = dz.astype(BF16)
        dy_ref[...] = dyb
        dx = ALPHA * dz
        for hf in range(2):
            cs = slice(hf * FF_H, (hf + 1) * FF_H)
            ug = upg_ref[:, cs]
            uv = upv_ref[:, cs]
            halo = halo_ref[:, cs]
            gbuf[0:FFN_HALO, :] = jnp.where(first_tile, jnp.zeros_like(halo), halo)
            gbuf[FFN_HALO:FFN_HALO + t, :] = ug
            s1buf[...] = gbuf[FFN_HALO - 1:FFN_HALO - 1 + t, :]
            s2buf[...] = gbuf[FFN_HALO - 2:FFN_HALO - 2 + t, :]
            ug1 = s1buf[...]
            ug2 = s2buf[...]
            gate = (pf_ref[PF_B:PF_B + 1, cs] + ug2 * pf_ref[PF_W0:PF_W0 + 1, cs] + ug1 * pf_ref[PF_W0 + 1:PF_W0 + 2, cs]
                    + ug * pf_ref[PF_W0 + 2:PF_W0 + 3, cs])
            sig = _sigmoid(gate)
            sl = gate * sig
            hm_ref[:, cs] = sl * uv
            dhm = _dot_nt(dyb, wd_ref[cs, :])
            duv = dhm * sl
            dgate = dhm * uv * (sig * (1.0 + gate * (1.0 - sig)))
            vf_ref[VF_B:VF_B + 1, cs] += _colsum(dgate)
            vf_ref[VF_W0:VF_W0 + 1, cs] += _colsum(dgate * ug2)
            vf_ref[VF_W0 + 1:VF_W0 + 2, cs] += _colsum(dgate * ug1)
            vf_ref[VF_W0 + 2:VF_W0 + 3, cs] += _colsum(dgate * ug)
            ebuf[0:t, cs] = dgate
            dug = (ebuf[2:2 + t, cs] * pf_ref[PF_W0:PF_W0 + 1, cs] + ebuf[1:1 + t, cs] * pf_ref[PF_W0 + 1:PF_W0 + 2, cs]
                   + dgate * pf_ref[PF_W0 + 2:PF_W0 + 3, cs])
            ebuf[t:t + FFN_HALO, cs] = ebuf[0:FFN_HALO, cs]
            dugb = dug.astype(BF16)
            duvb = duv.astype(BF16)
            dug_ref[:, cs] = dugb
            duv_ref[:, cs] = duvb
            dx = dx + _dot(dugb, wg_ref[cs, :]) + _dot(duvb, wv_ref[cs, :])
        dx_ref[...] = dx

        if last:
            @pl.when(i == nt - 1)
            def _():
                tot = jnp.sum(vd_ref[VD_LOSS:VD_LOSS + 1, :], axis=1, keepdims=True)
                vd_ref[VD_LOSS:VD_LOSS + 1, :] = jnp.zeros((1, D_MODEL), F32) + tot

    rev = lambda w: pl.BlockSpec((t, w), lambda i: (nt - 1 - i, 0))
    halo_spec = pl.BlockSpec((FFN_HALO, FF_P), lambda i: (jnp.maximum((nt - 1 - i) * hb - 1, 0), 0))
    out_shape = [jax.ShapeDtypeStruct((s, D_MODEL), F32), jax.ShapeDtypeStruct((s, D_MODEL), BF16),
                 jax.ShapeDtypeStruct((s, FF_P), BF16), jax.ShapeDtypeStruct((s, FF_P), BF16),
                 jax.ShapeDtypeStruct((s, FF_P), F32),
                 jax.ShapeDtypeStruct((8, D_MODEL), F32), jax.ShapeDtypeStruct((8, FF_P), F32)]
    return _call_with_side(
        body, side_ops, name="ffn_bwd_d_last" if last else "ffn_bwd_d", grid=(nt,),
        in_specs=[rev(D_MODEL), rev(D_MODEL), rev(1), rev(FF_P), halo_spec, rev(FF_P),
                  _layer_spec((8, D_MODEL), l), _layer_spec((8, FF_P), l),
                  _layer_spec((FF_P, D_MODEL), 0, resident=True),
                  _layer_spec((FF_P, D_MODEL), 0, 0, resident=True), _layer_spec((FF_P, D_MODEL), 0, 1, resident=True)],
        out_specs=[rev(D_MODEL), rev(D_MODEL), rev(FF_P), rev(FF_P), rev(FF_P),
                   _const_spec((8, D_MODEL)), _const_spec((8, FF_P))],
        out_shape=out_shape,
        scratch_shapes=[pltpu.VMEM((t + FFN_HALO, FF_H), F32), pltpu.VMEM((t + FFN_HALO, FF_P), F32),
                        pltpu.VMEM((t, FF_H), F32), pltpu.VMEM((t, FF_H), F32)],
        operands=(gz_or_target, xh2, rstd2, upg, upg, upv, pd, pf, w_d, wup_t, wup_t), semantics=("arbitrary",))


def _ffn_bwd_w(xh1, pd, dy, dug, duv, hm, l, t, side_ops=None):
    s = xh1.shape[0]
    nt = s // t

    def body(xh_ref, pd_ref, dy_ref, dug_ref, duv_ref, hm_ref, dwup_ref, dwd_ref):
        @pl.when(pl.program_id(1) == 0)
        def _():
            dwup_ref[...] = jnp.zeros_like(dwup_ref)
            dwd_ref[...] = jnp.zeros_like(dwd_ref)

        xb = (xh_ref[...] * _row(pd_ref, PD_G1) + _row(pd_ref, PD_B1)).astype(BF16)
        dwup_ref[0] += _dot_tn(dug_ref[...], xb)
        dwup_ref[1] += _dot_tn(duv_ref[...], xb)
        dwd_ref[...] += _dot_tn(hm_ref[...].astype(BF16), dy_ref[...])

    tok = lambda w: pl.BlockSpec((t, w), lambda c, i: (i, 0))
    half = pl.BlockSpec((t, FF_H), lambda c, i: (i, c))
    return _call_with_side(
        body, side_ops, name="ffn_bwd_w", grid=(2, nt),
        in_specs=[tok(D_MODEL), pl.BlockSpec((None, 8, D_MODEL), lambda c, i: (l, 0, 0)), tok(D_MODEL), half, half, half],
        out_specs=[pl.BlockSpec((None, 2, FF_H, D_MODEL), lambda c, i: (0, 0, c, 0)),
                   pl.BlockSpec((None, FF_H, D_MODEL), lambda c, i: (0, c, 0))],
        out_shape=[jax.ShapeDtypeStruct((1, 2, FF_P, D_MODEL), F32), jax.ShapeDtypeStruct((1, FF_P, D_MODEL), F32)],
        scratch_shapes=[], operands=(xh1, pd, dy, dug, duv, hm), semantics=("arbitrary", "arbitrary"))


def _adamw_math(w, g, m, v):
    nm = ADAM_B1 * m + (1.0 - ADAM_B1) * g
    nv = ADAM_B2 * v + (1.0 - ADAM_B2) * (g * g)
    m_hat = nm / (1.0 - ADAM_B1 ** ADAM_STEP)
    v_hat = nv / (1.0 - ADAM_B2 ** ADAM_STEP)
    return -ADAM_LR * (m_hat / (jnp.sqrt(v_hat) + ADAM_EPS) + ADAM_WD * w), nm, nv


def _adamw_layers(w, gs, m, v, name):
    shp = w.shape
    _, rows, cols = shp
    tr = _row_tile(rows, cols * 4)
    nb = rows // tr

    def body(w_ref, g0_ref, g1_ref, m_ref, v_ref, g_ref, d_ref, nm_ref, nv_ref):
        g = jnp.where(pl.program_id(0) == 0, g0_ref[...], g1_ref[...])
        g_ref[...] = g
        d_ref[...], nm_ref[...], nv_ref[...] = _adamw_math(w_ref[...], g, m_ref[...], v_ref[...])

    stacked = pl.BlockSpec((tr, cols), lambda l, i: (l * nb + i, 0))
    single = pl.BlockSpec((tr, cols), lambda l, i: (i, 0))
    sh = jax.ShapeDtypeStruct((DEPTH * rows, cols), F32)
    flat = lambda a: a.reshape(DEPTH * rows, cols)
    outs = pl.pallas_call(body, name=name, grid=(DEPTH, nb), in_specs=[stacked, single, single, stacked, stacked],
                          out_specs=[stacked] * 4, out_shape=[sh] * 4,
                          compiler_params=_cparams(("arbitrary", "arbitrary")))(flat(w), gs[0], gs[1], flat(m), flat(v))
    return [o.reshape(shp) for o in outs]


def _adamw_small(ws, gs, ms, vs):
    n = len(ws)

    def body(*refs):
        w_refs, g_refs, m_refs, v_refs = refs[:n], refs[n:2 * n], refs[2 * n:3 * n], refs[3 * n:4 * n]
        d_refs, nm_refs, nv_refs = refs[4 * n:5 * n], refs[5 * n:6 * n], refs[6 * n:7 * n]
        for k in range(n):
            d_refs[k][...], nm_refs[k][...], nv_refs[k][...] = _adamw_math(w_refs[k][...], g_refs[k][...], m_refs[k][...],
                                                                             v_refs[k][...])

    shapes = [jax.ShapeDtypeStruct(w.shape, F32) for w in ws]
    outs = pl.pallas_call(body, name="adamw_small", out_shape=shapes * 3, compiler_params=_cparams())(*ws, *gs, *ms, *vs)
    return outs[:n], outs[n:2 * n], outs[2 * n:]


def _all_gather_chips(tensors, name):
    n = len(tensors)
    halves = [a.shape[1] // 2 for a in tensors]

    def body(*refs):
        x_refs, out_refs = refs[:n], refs[n:2 * n]
        send_sems, recv_sems, local_sems = refs[2 * n:]
        x, y, c, chips = _my_place()
        me, sibling, mej = (x, y, c), (x, y, 1 - c), 2 * x + y

        def rows(tn, px, py, pc):
            return out_refs[tn].at[:, 2 * px + py, pl.ds(pc * halves[tn], halves[tn]), :]

        def copy(tn, k, block, to, src=None):
            return pltpu.make_async_remote_copy(
                src_ref=rows(tn, *block) if src is None else src, dst_ref=rows(tn, *block),
                send_sem=send_sems.at[tn, k], recv_sem=recv_sems.at[tn, k], device_id=to, device_id_type=MESH)

        mine_src = [x_refs[tn].at[:, pl.ds(c * halves[tn], halves[tn]), :] for tn in range(n)]
        mine = [pltpu.make_async_copy(mine_src[tn], rows(tn, *me), local_sems.at[tn]) for tn in range(n)]
        first = []
        for j, chip in enumerate(chips):
            first += [copy(tn, 1 + j, me, (*chip, c), src=mine_src[tn]) for tn in range(n)]
        first += [copy(tn, 0, me, sibling, src=mine_src[tn]) for tn in range(n)]
        for cp in first + mine:
            cp.start()
        passed = []
        for j, chip in enumerate(chips):
            for tn in range(n):
                copy(tn, 1 + j, (*chip, c), me).wait_recv()
                fwd = copy(tn, 4 + j, (*chip, c), sibling)
                fwd.start()
                passed.append(fwd)
        for tn in range(n):
            copy(tn, 0, sibling, me).wait_recv()
            for j, chip in enumerate(chips):
                copy(tn, 4 + j, (*chip, 1 - c), me).wait_recv()
        for cp in first + passed:
            cp.wait_send()
        for cp in mine:
            cp.wait()

    return pl.pallas_call(
        body, name=name,
        out_shape=[jax.ShapeDtypeStruct((a.shape[0], N_CHIPS) + a.shape[1:], a.dtype) for a in tensors],
        in_specs=[ANY] * n, out_specs=[ANY] * n,
        scratch_shapes=[pltpu.SemaphoreType.DMA((n, 7)), pltpu.SemaphoreType.DMA((n, 7)), pltpu.SemaphoreType.DMA((n,))],
    )(*tensors)


def _swap_op(g5s):
    n = len(g5s)

    def make(ins, outs, scr):
        send_sems, recv_sems = scr
        x, y, c, _ = _my_place()

        def copies():
            return [pltpu.make_async_remote_copy(
                src_ref=ins[tn].at[:, :, 1 - c], dst_ref=outs[tn], send_sem=send_sems.at[tn], recv_sem=recv_sems.at[tn],
                device_id=(x, y, 1 - c), device_id_type=MESH) for tn in range(n)]

        def start():
            for cp in copies():
                cp.start()

        def wait():
            for cp in copies():
                cp.wait()

        return start, wait

    return _SideOp(g5s, [jax.ShapeDtypeStruct(g.shape[:2] + g.shape[3:], g.dtype) for g in g5s],
                   [pltpu.SemaphoreType.DMA((n,)), pltpu.SemaphoreType.DMA((n,))], make)


def _scatter_op(parts):
    n = len(parts)

    def make(ins, outs, scr):
        send_sems, recv_sems = scr
        x, y, c, chips = _my_place()

        def copies():
            return [pltpu.make_async_remote_copy(
                src_ref=ins[tn].at[:, 2 * px + py], dst_ref=outs[tn].at[:, k],
                send_sem=send_sems.at[tn, k], recv_sem=recv_sems.at[tn, k],
                device_id=(px, py, c), device_id_type=MESH) for k, (px, py) in enumerate(chips) for tn in range(n)]

        def start():
            for cp in copies():
                cp.start()

        def wait():
            for cp in copies():
                cp.wait()

        return start, wait

    return _SideOp(parts, [jax.ShapeDtypeStruct((p.shape[0], 3) + p.shape[2:], p.dtype) for p in parts],
                   [pltpu.SemaphoreType.DMA((n, 3)), pltpu.SemaphoreType.DMA((n, 3))], make)


def _sgather_op(fs):
    n = len(fs)

    def make(ins, outs, scr):
        send_sems, recv_sems = scr
        x, y, c, _ = _my_place()

        def copy(tn, dst_half):
            return pltpu.make_async_remote_copy(
                src_ref=outs[tn].at[:, c], dst_ref=outs[tn].at[:, dst_half], send_sem=send_sems.at[tn],
                recv_sem=recv_sems.at[tn], device_id=(x, y, 1 - c), device_id_type=MESH)

        def start():
            for tn in range(n):
                copy(tn, c).start()

        def wait():
            for tn in range(n):
                copy(tn, 1 - c).wait_recv()
                copy(tn, c).wait_send()

        return start, wait

    return _SideOp(fs, [jax.ShapeDtypeStruct(f.shape, f.dtype) for f in fs],
                   [pltpu.SemaphoreType.DMA((n,)), pltpu.SemaphoreType.DMA((n,))], make, aliases={tn: tn for tn in range(n)})


def _run_side_ops(ops, name):
    return _call_with_side(lambda: None, ops, name=name, grid=(1,), in_specs=[], out_specs=[], out_shape=[],
                           scratch_shapes=[], operands=(), semantics=("arbitrary",))[1]


def _gather_devices_op(xs):
    def make(ins, outs, scr):
        send_sems, recv_sems, local_sem = scr
        x, y, c, chips = _my_place()
        peers = [(x, y, 1 - c)] + [(px, py, pc) for (px, py) in chips for pc in (c, 1 - c)]
        me = 4 * x + 2 * y + c

        def copy(k, slot):
            return pltpu.make_async_remote_copy(
                src_ref=ins[0], dst_ref=outs[0].at[slot], send_sem=send_sems.at[k], recv_sem=recv_sems.at[k],
                device_id=peers[k], device_id_type=MESH)

        def local():
            return pltpu.make_async_copy(ins[0], outs[0].at[me], local_sem)

        def start():
            for k in range(7):
                copy(k, me).start()
            local().start()

        def wait():
            for k, (px, py, pc) in enumerate(peers):
                copy(k, 4 * px + 2 * py + pc).wait_recv()
                copy(k, me).wait_send()
            local().wait()

        return start, wait

    return _SideOp([xs], [jax.ShapeDtypeStruct((8,) + xs.shape, xs.dtype)],
                   [pltpu.SemaphoreType.DMA((7,)), pltpu.SemaphoreType.DMA((7,)), pltpu.SemaphoreType.DMA], make)


def _add_halves(gs, recvs, place):
    n = len(gs)

    def body(place_ref, *refs):
        g_refs, r_refs, o_refs = refs[:n], refs[n:2 * n], refs[2 * n:]
        for tn in range(n):
            o_refs[tn][...] = (g_refs[tn][...] + r_refs[tn][...]).astype(BF16)

    def gspec(g):
        return pl.BlockSpec((None, None, None) + g.shape[3:], lambda l, j, p: (l, j, p[1], 0, 0))

    def rspec(r):
        return pl.BlockSpec((None, None) + r.shape[2:], lambda l, j, p: (l, j, 0, 0))

    grid_spec = pltpu.PrefetchScalarGridSpec(
        num_scalar_prefetch=1, grid=(gs[0].shape[0], N_CHIPS),
        in_specs=[gspec(g) for g in gs] + [rspec(r) for r in recvs], out_specs=[rspec(r) for r in recvs])
    return pl.pallas_call(body, name="rs_add", grid_spec=grid_spec,
                          out_shape=[jax.ShapeDtypeStruct(r.shape, BF16) for r in recvs],
                          compiler_params=_cparams(("arbitrary", "arbitrary")))(place, *gs, *recvs)


def _sum_slots(parts, slots, place):
    n = len(parts)

    def body(place_ref, *refs):
        p_refs, s_refs, o_refs = refs[:n], refs[n:2 * n], refs[2 * n:]
        for tn in range(n):
            acc = p_refs[tn][...].astype(F32)
            for k in range(3):
                acc = acc + s_refs[tn][k].astype(F32)
            o_refs[tn][...] = acc

    def pspec(p):
        return pl.BlockSpec((None, None) + p.shape[2:], lambda l, pl_: (l, pl_[0], 0, 0))

    def sspec(sl):
        return pl.BlockSpec((None,) + sl.shape[1:], lambda l, pl_: (l, 0, 0, 0))

    def ospec(p):
        return pl.BlockSpec((None, None) + p.shape[2:], lambda l, pl_: (l, pl_[1], 0, 0))

    grid_spec = pltpu.PrefetchScalarGridSpec(
        num_scalar_prefetch=1, grid=(parts[0].shape[0],),
        in_specs=[pspec(p) for p in parts] + [sspec(sl) for sl in slots], out_specs=[ospec(p) for p in parts])
    return pl.pallas_call(body, name="rs_sum", grid_spec=grid_spec,
                          out_shape=[jax.ShapeDtypeStruct((p.shape[0], 2) + p.shape[2:], F32) for p in parts],
                          compiler_params=_cparams(("arbitrary",)))(place, *parts, *slots)


def _sum_devices(gathered, m_per):
    def body(g_ref, o_ref):
        acc = g_ref[0:m_per, :]
        for d in range(1, 8):
            acc = acc + g_ref[d * m_per:(d + 1) * m_per, :]
        o_ref[...] = acc

    return pl.pallas_call(body, name="small_sum", out_shape=jax.ShapeDtypeStruct((m_per, LANE), F32),
                          compiler_params=_cparams())(gathered)


def _pad_ff_cols(a):
    lead = a.shape[:-1]
    n = a.shape[-1] // FF_Q
    a = a.reshape(*lead, n, FF_Q)
    a = jnp.pad(a, [(0, 0)] * len(lead) + [(0, 0), (0, FF_QP - FF_Q)])
    return a.reshape(*lead, n * FF_QP)


def _unpad_ff_cols(a):
    lead = a.shape[:-1]
    n = a.shape[-1] // FF_QP
    return a.reshape(*lead, n, FF_QP)[..., :FF_Q].reshape(*lead, n * FF_Q)


def _pack_small(parts):
    flat = jnp.concatenate([p.reshape(-1) for p in parts])
    return flat.reshape(-1, LANE)


SMALL_SHAPES = [("conv_a_w", (CONV_K, CONV_W)), ("conv_a_b", (CONV_W,)), ("ln_a_g", (CONV_W,)), ("ln_a_b", (CONV_W,)),
                ("ln_v_g", (GMLP_W,)), ("ln_v_b", (GMLP_W,)), ("w_s", (6, CHUNK, CHUNK)), ("b_s", (6, CHUNK)),
                ("ln1_g", (D_MODEL,)), ("ln1_b", (D_MODEL,)), ("conv_f_w", (FFN_CONV_K, D_FF)), ("conv_f_b", (D_FF,)),
                ("ln2_g", (D_MODEL,)), ("ln2_b", (D_MODEL,))]


def _unpack_small(flat2d):
    flat = flat2d.reshape(DEPTH, -1)
    out, o = {}, 0
    for name, shp in SMALL_SHAPES:
        n = 1
        for d in shp:
            n *= d
        out[name] = flat[:, o:o + n].reshape((DEPTH,) + shp)
        o += n
    return out


def _rows8(rows):
    blk = jnp.stack(rows, axis=1)
    return jnp.pad(blk, ((0, 0), (0, 8 - len(rows)), (0, 0)))


def kernel(x, mem, w_in, conv_a_w, conv_a_b, ln_a_g, ln_a_b, ln_v_g, ln_v_b, w_s, b_s, w_mk, w_mv, w_out, ln1_g, ln1_b, w_up, conv_f_w, conv_f_b, w_down, ln2_g, ln2_b, loss_target, m_w_in, m_conv_a_w, m_conv_a_b, m_ln_a_g, m_ln_a_b, m_ln_v_g, m_ln_v_b, m_w_s, m_b_s, m_w_mk, m_w_mv, m_w_out, m_ln1_g, m_ln1_b, m_w_up, m_conv_f_w, m_conv_f_b, m_w_down, m_ln2_g, m_ln2_b, v_w_in, v_conv_a_w, v_conv_a_b, v_ln_a_g, v_ln_a_b, v_ln_v_g, v_ln_v_b, v_w_s, v_b_s, v_w_mk, v_w_mv, v_w_out, v_ln1_g, v_ln1_b, v_w_up, v_conv_f_w, v_conv_f_b, v_w_down, v_ln2_g, v_ln2_b):
    seq = x.shape[1]
    t_fwd = min(512, seq)
    t_bwd = min(256, seq)
    chip = 2 * lax.axis_index("x") + lax.axis_index("y")
    core = lax.axis_index("c")
    place = jnp.stack([chip, core]).astype(jnp.int32)
    x0 = x[0]
    mem0 = mem[0]
    target = loss_target[0]

    sh_in = w_in.transpose(0, 2, 1).astype(BF16)
    sh_mk, sh_mv, sh_out = w_mk.astype(BF16), w_mv.astype(BF16), w_out.astype(BF16)
    sh_up = _pad_ff_cols(w_up).transpose(0, 2, 1).astype(BF16)
    sh_dn = jnp.pad(w_down, ((0, 0), (0, FF_QP - FF_Q), (0, 0))).astype(BF16)

    def mixer_weights(g_in, g_mk, g_mv, g_out):
        return dict(win_t=g_in.reshape(1, IN_W, D_MODEL), wmk=g_mk.reshape(1, D_MODEL, XATTN_W),
                    wmv=g_mv.reshape(1, D_MODEL, XATTN_W), wout=g_out.reshape(1, D_MODEL, D_MODEL))

    def ffn_weights(g_up, g_dn):
        return dict(wup_t=g_up.reshape(1, 2, FF_P, D_MODEL), wdown=g_dn.reshape(1, FF_P, D_MODEL))

    n_ca = conv_a_w.size
    small_w = _pack_small([conv_a_w, conv_f_w, jnp.zeros((2 * 80 * LANE - n_ca - conv_f_w.size,), F32)])[None]
    *g_mixer0, small_g = _all_gather_chips([sh_in[:1], sh_mk[:1], sh_mv[:1], sh_out[:1], small_w], "ag_mixer0")
    wts = [mixer_weights(*g_mixer0), None]
    small_g = small_g.reshape(N_CHIPS, -1)
    conv_a_full = small_g[:, :n_ca].reshape(N_CHIPS, DEPTH, CONV_K, CONV_W // 4).transpose(1, 2, 0, 3).reshape(DEPTH, CONV_K, CONV_W)
    conv_f_full = small_g[:, n_ca:n_ca + conv_f_w.size].reshape(N_CHIPS, DEPTH, FFN_CONV_K, FF_Q).transpose(1, 2, 0, 3).reshape(DEPTH, FFN_CONV_K, D_FF)

    tril = jnp.tril(jnp.ones((CHUNK, CHUNK), dtype=bool))
    ws_m = jnp.where(tril, w_s, 0.0)
    wst = ws_m.reshape(DEPTH, 3, 2 * CHUNK, CHUNK).astype(BF16)
    wstt = ws_m.transpose(0, 1, 3, 2).reshape(DEPTH, 3, 2 * CHUNK, CHUNK).astype(BF16)
    bst = jnp.repeat(b_s.transpose(0, 2, 1), HEAD_DIM, axis=2)
    conv_w = jnp.pad(conv_a_full, ((0, 0), (0, CONV_HALO - CONV_K), (0, 0)))
    pa = _rows8([conv_a_b, ln_a_g, ln_a_b, ln_v_g, ln_v_b])
    gin = jnp.concatenate([jnp.ones((1, D_MODEL), F32), ln2_g[:DEPTH - 1]], axis=0)
    bin_ = jnp.concatenate([jnp.zeros((1, D_MODEL), F32), ln2_b[:DEPTH - 1]], axis=0)
    pd = _rows8([gin, bin_, ln1_g, ln1_b, ln2_g, ln2_b])
    pf = jnp.concatenate([_pad_ff_cols(conv_f_full), _pad_ff_cols(conv_f_b)[:, None, :],
                          jnp.zeros((DEPTH, 8 - FFN_CONV_K - 1, FF_P), F32)], axis=1)

    acts = []
    xin = x0
    for l in range(DEPTH):
        w = wts[l]
        kt_all, k_all, v_all, vt_all = _kv_fwd(mem0, w["wmk"], w["wmv"])
        side = [sh_up[0], sh_dn[0]] if l == 0 else None
        xh1, rstd1, h, ac, cat, *landed = _mixer_fwd(xin, pd, w["win_t"], conv_w, pa, wst, bst, kt_all, v_all, w["wout"],
                                                     l, t_fwd, side)
        if l == 0:
            w.update(ffn_weights(*landed))
        side = [sh_in[1], sh_mk[1], sh_mv[1], sh_out[1], sh_up[1], sh_dn[1]] if l == 0 else None
        xh2, rstd2, upg, upv, *landed = _ffn_fwd(xh1, pd, w["wup_t"], pf, w["wdown"], l, t_bwd, side)
        if l == 0:
            wts[1] = {**mixer_weights(*landed[:4]), **ffn_weights(*landed[4:])}
        acts.append(dict(xin=xin, kt_all=kt_all, k_all=k_all, vt_all=vt_all, xh1=xh1, rstd1=rstd1, h=h, ac=ac, cat=cat,
                         xh2=xh2, rstd2=rstd2, upg=upg, upv=upv))
        xin = xh2

    assert DEPTH == 2

    def halves_view(gs):
        return [g.reshape(1, N_CHIPS, 2, g.shape[1] // (2 * N_CHIPS), g.shape[2]) for g in gs]

    small = [None] * DEPTH
    red_layers = [None] * DEPTH
    gz = target
    loss_sum = None
    g5_prev = None
    for l in reversed(range(DEPTH)):
        a, w = acts[l], wts[l]
        last = l == DEPTH - 1
        (dx1, dy, dug, duv, hm, vd2, vf), side = _ffn_bwd_d(
            gz, a["xh2"], a["rstd2"], a["upg"], a["upv"], pd, pf, w["wdown"], w["wup_t"], l, t_bwd, last,
            [_swap_op(g5_prev)] if g5_prev else None)
        if last:
            loss_sum = vd2[VD_LOSS, 0]
        parts_prev = _add_halves(g5_prev, side[0], place) if g5_prev else None
        (gw_up_t, gw_down), side = _ffn_bwd_w(a["xh1"], pd, dy, dug, duv, hm, l, t_fwd,
                                              [_scatter_op(parts_prev)] if g5_prev else None)
        halves_prev = _sum_slots(parts_prev, side[0], place) if g5_prev else None
        g5_ffn = halves_view([gw_up_t.reshape(1, 2 * FF_P, D_MODEL), gw_down])
        ops = ([_sgather_op(halves_prev)] if g5_prev else []) + ([_swap_op(g5_ffn)] if l == 0 else [])
        (dx0, dh, dmix, vd1, va, dcw, dws, dbs, dkt, dv), side = _mixer_bwd_d(
            dx1, a["xh1"], a["rstd1"], a["h"], a["ac"], pd, conv_w, pa, wst, wstt, bst,
            a["kt_all"], a["k_all"], a["vt_all"], w["wout"], w["win_t"], l, t_bwd, ops)
        if g5_prev:
            red_layers[l + 1] = side[0]
        parts_ffn = _add_halves(g5_ffn, side[-1], place) if l == 0 else None
        dws6 = jnp.where(tril, dws.reshape(6, CHUNK, CHUNK), 0.0)
        small[l] = [dcw[:CONV_K], va[VA_CONV_B], va[VA_LNA_G], va[VA_LNA_B], va[VA_LNV_G], va[VA_LNV_B], dws6,
                    dbs[:, :6].T, vd1[VD_LN_G], vd1[VD_LN_B],
                    _unpad_ff_cols(vf[VF_W0:VF_W0 + FFN_CONV_K]), _unpad_ff_cols(vf[VF_B]),
                    vd2[VD_LN_G], vd2[VD_LN_B]]
        ops = None
        if l == 0:
            small_local = _pack_small([q for ll in range(DEPTH) for q in small[ll]])
            ops = [_scatter_op(parts_ffn), _gather_devices_op(small_local)]
        (gw_in_t, gw_out), side = _mixer_bwd_w(a["xin"], pd, dh, a["cat"], dmix, l, t_fwd, ops)
        gw_mk, gw_mv = _kv_bwd(mem0, dkt, dv)
        g5_mix = halves_view([gw_in_t, gw_mk, gw_mv, gw_out])
        if l == 0:
            small_gathered = side[1][0]
            halves_ffn = _sum_slots(parts_ffn, side[0], place)
            red_ffn, recv_mix = _run_side_ops([_sgather_op(halves_ffn), _swap_op(g5_mix)], "rs_tail_swap")
            parts_mix = _add_halves(g5_mix, recv_mix, place)
            halves_mix = _sum_slots(parts_mix, _run_side_ops([_scatter_op(parts_mix)], "rs_tail_chips")[0], place)
            red_mix = _run_side_ops([_sgather_op(halves_mix)], "rs_tail_gather")[0]
            red_layers[0] = red_mix + red_ffn
        else:
            g5_prev = g5_mix + g5_ffn
        gz = dx0
    grad_x = gz[None]

    def shard_grads(red):
        r = [f.reshape(-1, f.shape[-1]) for f in red]
        return dict(w_in=r[0].T, w_mk=r[1], w_mv=r[2], w_out=r[3], w_up=_unpad_ff_cols(r[4].T), w_down=r[5][:FF_Q])

    big_grads = [shard_grads(red_layers[l]) for l in range(DEPTH)]

    m_small = small_gathered.shape[1]
    small_red = _sum_devices(small_gathered.reshape(8 * m_small, LANE), m_small)
    sg = _unpack_small(small_red)
    g_conv_a_w = lax.dynamic_slice_in_dim(sg["conv_a_w"], chip * (CONV_W // 4), CONV_W // 4, axis=2)
    g_conv_f_w = lax.dynamic_slice_in_dim(sg["conv_f_w"], chip * FF_Q, FF_Q, axis=2)

    loss = 0.5 / D_MODEL * lax.psum(loss_sum, ("x", "y", "c"))

    grads = dict(conv_a_w=g_conv_a_w, conv_a_b=sg["conv_a_b"], ln_a_g=sg["ln_a_g"], ln_a_b=sg["ln_a_b"],
                 ln_v_g=sg["ln_v_g"], ln_v_b=sg["ln_v_b"], w_s=sg["w_s"], b_s=sg["b_s"], ln1_g=sg["ln1_g"], ln1_b=sg["ln1_b"],
                 conv_f_w=g_conv_f_w, conv_f_b=sg["conv_f_b"], ln2_g=sg["ln2_g"], ln2_b=sg["ln2_b"])
    weights = dict(w_in=w_in, conv_a_w=conv_a_w, conv_a_b=conv_a_b, ln_a_g=ln_a_g, ln_a_b=ln_a_b, ln_v_g=ln_v_g,
                   ln_v_b=ln_v_b, w_s=w_s, b_s=b_s, w_mk=w_mk, w_mv=w_mv, w_out=w_out, ln1_g=ln1_g, ln1_b=ln1_b,
                   w_up=w_up, conv_f_w=conv_f_w, conv_f_b=conv_f_b, w_down=w_down, ln2_g=ln2_g, ln2_b=ln2_b)
    mom_m = dict(w_in=m_w_in, conv_a_w=m_conv_a_w, conv_a_b=m_conv_a_b, ln_a_g=m_ln_a_g, ln_a_b=m_ln_a_b, ln_v_g=m_ln_v_g,
                 ln_v_b=m_ln_v_b, w_s=m_w_s, b_s=m_b_s, w_mk=m_w_mk, w_mv=m_w_mv, w_out=m_w_out, ln1_g=m_ln1_g,
                 ln1_b=m_ln1_b, w_up=m_w_up, conv_f_w=m_conv_f_w, conv_f_b=m_conv_f_b, w_down=m_w_down, ln2_g=m_ln2_g,
                 ln2_b=m_ln2_b)
    mom_v = dict(w_in=v_w_in, conv_a_w=v_conv_a_w, conv_a_b=v_conv_a_b, ln_a_g=v_ln_a_g, ln_a_b=v_ln_a_b, ln_v_g=v_ln_v_g,
                 ln_v_b=v_ln_v_b, w_s=v_w_s, b_s=v_b_s, w_mk=v_w_mk, w_mv=v_w_mv, w_out=v_w_out, ln1_g=v_ln1_g,
                 ln1_b=v_ln1_b, w_up=v_w_up, conv_f_w=v_conv_f_w, conv_f_b=v_conv_f_b, w_down=v_w_down, ln2_g=v_ln2_g,
                 ln2_b=v_ln2_b)
    names = list(weights)
    big_names = ["w_in", "w_mk", "w_mv", "w_out", "w_up", "w_down"]
    delta, new_m, new_v = {}, {}, {}
    for n in big_names:
        grads[n], delta[n], new_m[n], new_v[n] = _adamw_layers(weights[n], [big_grads[l][n] for l in range(DEPTH)],
                                                               mom_m[n], mom_v[n], "adamw_" + n)
    small_names = [n for n in names if n not in big_names]
    ds, nms, nvs = _adamw_small([weights[n] for n in small_names], [grads[n] for n in small_names],
                                [mom_m[n] for n in small_names], [mom_v[n] for n in small_names])
    for n, d, nm, nv in zip(small_names, ds, nms, nvs):
        delta[n], new_m[n], new_v[n] = d, nm, nv

    return (loss, grad_x, *[grads[n] for n in names], *[delta[n] for n in names],
            *[new_m[n] for n in names], *[new_v[n] for n in names])
```

```python
import jax
import jax.numpy as jnp
from jax import lax
from jax.experimental import pallas as pl
from jax.experimental.pallas import tpu as pltpu

F32 = jnp.float32
BF16 = jnp.bfloat16

D_MODEL = 1024
DEPTH = 2
CONV_W = 384
GMLP_W = 384
XATTN_W = 256
XATTN_HEADS = 4
HEAD_DIM = 64
IN_W = 1792
CONV_K = 31
CHUNK = 128
N_MEM = 256
D_FF = 2752
FFN_CONV_K = 3
ALPHA = (2.0 * DEPTH) ** 0.25
LN_EPS = 1e-5
ATT_SCALE = 1.0 / 8.0
ADAM_LR, ADAM_B1, ADAM_B2, ADAM_EPS, ADAM_WD, ADAM_STEP = 0.001, 0.9, 0.999, 1e-08, 0.01, 10

N_CHIPS = 4
FF_Q = D_FF // N_CHIPS
FF_QP = 704
FF_H = 2 * FF_QP
FF_P = 4 * FF_QP
LANE = 128
CONV_HALO = 32
FFN_HALO = 8
BF16_ROWS = 16
VMEM_LIMIT = 60 * 1024 * 1024

MESH = pl.DeviceIdType.MESH
ANY = pl.BlockSpec(memory_space=pl.ANY)


def _cparams(sem=None, vmem=VMEM_LIMIT):
    kw = {"vmem_limit_bytes": vmem}
    if sem is not None:
        kw["dimension_semantics"] = sem
    return pltpu.CompilerParams(**kw)


def _row_tile(rows, row_bytes, limit=2 << 20, mult=BF16_ROWS):
    if rows * row_bytes <= limit:
        return rows
    best = None
    for cand in range(mult, rows, mult):
        if rows % cand == 0 and cand * row_bytes <= limit:
            best = cand
    assert best is not None, (rows, row_bytes)
    return best


def _const_spec(shape):
    nd = len(shape)
    return pl.BlockSpec(shape, lambda *_: (0,) * nd)


def _layer_spec(shape, *lead, resident=False):
    nd = len(shape)
    kw = {"pipeline_mode": pl.Buffered(1)} if resident else {}
    return pl.BlockSpec((None,) * len(lead) + tuple(shape), lambda *_: tuple(lead) + (0,) * nd, **kw)


def _sigmoid(x):
    return jax.nn.sigmoid(x)


def _gelu(x):
    return jax.nn.gelu(x)


def _gelu_grad(x):
    c = 0.7978845608028654
    a = 0.044715
    t = jnp.tanh(c * (x + a * x * x * x))
    return 0.5 * (1.0 + t) + 0.5 * x * (1.0 - t * t) * c * (1.0 + 3.0 * a * x * x)


def _ln_fwd(z):
    mu = jnp.mean(z, axis=-1, keepdims=True)
    zc = z - mu
    var = jnp.mean(zc * zc, axis=-1, keepdims=True)
    rstd = lax.rsqrt(var + LN_EPS)
    return zc * rstd, rstd


def _ln_bwd(dxh, xh, rstd):
    m1 = jnp.mean(dxh, axis=-1, keepdims=True)
    m2 = jnp.mean(dxh * xh, axis=-1, keepdims=True)
    return rstd * (dxh - m1 - xh * m2)


def _colsum(a):
    return jnp.sum(a, axis=0, keepdims=True)


def _dot(a, b):
    return jnp.dot(a, b, preferred_element_type=F32)


def _dot_tn(a, b):
    return lax.dot_general(a, b, (((0,), (0,)), ((), ())), preferred_element_type=F32)


def _dot_nt(a, b):
    return lax.dot_general(a, b, (((1,), (1,)), ((), ())), preferred_element_type=F32)


def _softmax_heads(sc):
    ps = []
    for hd in range(XATTN_HEADS):
        s = sc[:, hd * N_MEM:(hd + 1) * N_MEM]
        e = jnp.exp(s - jnp.max(s, axis=-1, keepdims=True))
        ps.append(e / jnp.sum(e, axis=-1, keepdims=True))
    return jnp.concatenate(ps, axis=1)


def _lane_lo(shape):
    return (lax.broadcasted_iota(jnp.int32, shape, len(shape) - 1) % LANE) < HEAD_DIM


def _spatial_mix(vnb, wst_ref, bst_ref, mix_ref, t):
    lo = _lane_lo((CHUNK, LANE))
    for n in range(t // CHUNK):
        rows = slice(n * CHUNK, (n + 1) * CHUNK)
        for j in range(GMLP_W // LANE):
            cols = slice(j * LANE, (j + 1) * LANE)
            r = _dot(wst_ref[j], vnb[rows, cols])
            mix_ref[rows, cols] = jnp.where(lo, r[:CHUNK], r[CHUNK:]) + bst_ref[:, cols]


def _kv_fwd(mem, w_mk, w_mv):
    def body(mem_ref, wk_ref, wv_ref, kt_ref, k_ref, v_ref, vt_ref):
        mb = mem_ref[...].astype(BF16)
        k = _dot(mb, wk_ref[...])
        v = _dot(mb, wv_ref[...])
        col = lax.broadcasted_iota(jnp.int32, (N_MEM, XATTN_W), 1) // HEAD_DIM
        ks = [jnp.where(col == hd, k, 0.0) for hd in range(XATTN_HEADS)]
        vs = [jnp.where(col == hd, v, 0.0) for hd in range(XATTN_HEADS)]
        k_ref[...] = jnp.concatenate(ks, axis=0).astype(BF16)
        v_ref[...] = jnp.concatenate(vs, axis=0).astype(BF16)
        kt_ref[...] = jnp.concatenate([x.T for x in ks], axis=1).astype(BF16)
        vt_ref[...] = jnp.concatenate([x.T for x in vs], axis=1).astype(BF16)

    wide = jax.ShapeDtypeStruct((XATTN_W, XATTN_HEADS * N_MEM), BF16)
    tall = jax.ShapeDtypeStruct((XATTN_HEADS * N_MEM, XATTN_W), BF16)
    wspec = _layer_spec((D_MODEL, XATTN_W), 0)
    return pl.pallas_call(body, name="kv_fwd", grid=(1,),
                          in_specs=[_const_spec((N_MEM, D_MODEL)), wspec, wspec],
                          out_specs=[_const_spec(wide.shape), _const_spec(tall.shape), _const_spec(tall.shape),
                                     _const_spec(wide.shape)],
                          out_shape=(wide, tall, tall, wide), compiler_params=_cparams(("arbitrary",)))(mem, w_mk, w_mv)


def _kv_bwd(mem, dkt_all, dv_all):
    def body(mem_ref, dkt_ref, dv_ref, gk_ref, gv_ref):
        col = lax.broadcasted_iota(jnp.int32, (N_MEM, XATTN_W), 1) // HEAD_DIM
        dk = jnp.zeros((N_MEM, XATTN_W), F32)
        dv = jnp.zeros((N_MEM, XATTN_W), F32)
        for hd in range(XATTN_HEADS):
            dk = dk + jnp.where(col == hd, dkt_ref[:, hd * N_MEM:(hd + 1) * N_MEM].T, 0.0)
            dv = dv + jnp.where(col == hd, dv_ref[hd * N_MEM:(hd + 1) * N_MEM, :], 0.0)
        mb = mem_ref[...].astype(BF16)
        gk_ref[0] = _dot_tn(mb, dk.astype(BF16))
        gv_ref[0] = _dot_tn(mb, dv.astype(BF16))

    out = jax.ShapeDtypeStruct((1, D_MODEL, XATTN_W), F32)
    return pl.pallas_call(body, name="kv_bwd", out_shape=(out, out), compiler_params=_cparams())(mem, dkt_all, dv_all)


def _my_place():
    x, y, c = lax.axis_index("x"), lax.axis_index("y"), lax.axis_index("c")
    chips = [(1 - x, y), (x, 1 - y), (1 - x, 1 - y)]
    return x, y, c, chips


def _side_gather(shard_refs, land_refs, send_sems, recv_sems, local_sems):
    n = len(shard_refs)
    x, y, c, chips = _my_place()
    mej = 2 * x + y

    def remote(tn, k, slot):
        px, py = chips[k]
        return pltpu.make_async_remote_copy(
            src_ref=shard_refs[tn], dst_ref=land_refs[tn].at[slot], send_sem=send_sems.at[tn, k],
            recv_sem=recv_sems.at[tn, k], device_id=(px, py, c), device_id_type=MESH)

    def local(tn):
        return pltpu.make_async_copy(shard_refs[tn], land_refs[tn].at[mej], local_sems.at[tn])

    def start():
        for k in range(3):
            for tn in range(n):
                remote(tn, k, mej).start()
        for tn in range(n):
            local(tn).start()

    def wait():
        for k, (px, py) in enumerate(chips):
            for tn in range(n):
                remote(tn, k, 2 * px + py).wait_recv()
                remote(tn, k, mej).wait_send()
        for tn in range(n):
            local(tn).wait()

    return start, wait


def _side_specs(side):
    side = list(side or ())
    n = len(side)
    shapes = [jax.ShapeDtypeStruct((N_CHIPS,) + a.shape, a.dtype) for a in side]
    scratch = [pltpu.SemaphoreType.DMA((n, 3)), pltpu.SemaphoreType.DMA((n, 3)), pltpu.SemaphoreType.DMA((n,))] if n else []
    return side, [ANY] * n, [ANY] * n, shapes, scratch


class _SideOp:
    def __init__(self, ins, out_shapes, scratch, make, aliases=None):
        self.ins, self.out_shapes, self.scratch, self.make, self.aliases = list(ins), list(out_shapes), list(scratch), make, dict(aliases or {})


def _call_with_side(body, side_ops, *, name, grid, in_specs, out_specs, out_shape, scratch_shapes, operands, semantics):
    side_ops = list(side_ops or ())
    n_in, n_out, n_scr = len(in_specs), len(out_specs), len(scratch_shapes)
    s_ins = [a for op in side_ops for a in op.ins]
    s_outs = [o for op in side_ops for o in op.out_shapes]
    s_scr = [x for op in side_ops for x in op.scratch]
    aliases, oi, oo = {}, 0, 0
    for op in side_ops:
        for a, b in op.aliases.items():
            aliases[n_in + oi + a] = n_out + oo + b
        oi, oo = oi + len(op.ins), oo + len(op.out_shapes)

    def wrapped(*refs):
        ins, sins = refs[:n_in], refs[n_in:n_in + len(s_ins)]
        base = n_in + len(s_ins)
        outs, souts = refs[base:base + n_out], refs[base + n_out:base + n_out + len(s_outs)]
        base += n_out + len(s_outs)
        scr, sscr = refs[base:base + n_scr], refs[base + n_scr:]
        if side_ops:
            first = pl.program_id(0) == 0
            last = pl.program_id(0) == grid[0] - 1
            for d in range(1, len(grid)):
                first = jnp.logical_and(first, pl.program_id(d) == 0)
                last = jnp.logical_and(last, pl.program_id(d) == grid[d] - 1)
            hooks, a, b, c = [], 0, 0, 0
            for op in side_ops:
                hooks.append(op.make(sins[a:a + len(op.ins)], souts[b:b + len(op.out_shapes)], sscr[c:c + len(op.scratch)]))
                a, b, c = a + len(op.ins), b + len(op.out_shapes), c + len(op.scratch)

            @pl.when(first)
            def _():
                for start, _w in hooks:
                    start()

        body(*ins, *outs, *scr)
        if side_ops:
            @pl.when(last)
            def _():
                for _s, wait in hooks:
                    wait()

    res = pl.pallas_call(
        wrapped, name=name, grid=grid, in_specs=list(in_specs) + [ANY] * len(s_ins),
        out_specs=list(out_specs) + [ANY] * len(s_outs), out_shape=list(out_shape) + s_outs,
        scratch_shapes=list(scratch_shapes) + s_scr, input_output_aliases=aliases,
        compiler_params=_cparams(semantics),
    )(*operands, *s_ins)
    side_res, k = [], n_out
    for op in side_ops:
        side_res.append(list(res[k:k + len(op.out_shapes)]))
        k += len(op.out_shapes)
    return list(res[:n_out]), side_res


PA_CONV_B, PA_LNA_G, PA_LNA_B, PA_LNV_G, PA_LNV_B = 0, 1, 2, 3, 4
PD_GIN, PD_BIN, PD_G1, PD_B1, PD_G2, PD_B2 = 0, 1, 2, 3, 4, 5


def _row(ref, r):
    return ref[r:r + 1, :]


def _mixer_fwd(xin, pd, win_t, conv_w, pa, wst, bst, kt_all, v_all, w_out, l, t, side=None):
    s = xin.shape[0]
    nt = s // t
    side, side_in, side_out, side_shapes, side_scratch = _side_specs(side)
    ns = len(side)

    def body(x_ref, pd_ref, wint_ref, cw_ref, pa_ref, wst_ref, bst_ref, kt_ref, v_ref, wout_ref, *rest):
        shard_refs, rest = rest[:ns], rest[ns:]
        xh_ref, rstd_ref, h_ref, ac_ref, cat_ref = rest[:5]
        land_refs, rest = rest[5:5 + ns], rest[5 + ns:]
        cbuf, mixbuf, zbuf = rest[:3]
        i = pl.program_id(0)
        if ns:
            gather_start, gather_wait = _side_gather(shard_refs, land_refs, *rest[3:])
            pl.when(i == 0)(gather_start)
        x = x_ref[...] * _row(pd_ref, PD_GIN) + _row(pd_ref, PD_BIN)
        h = _dot_nt(x.astype(BF16), wint_ref[...])
        h_ref[...] = h
        a1, a2 = h[:, 0:CONV_W], h[:, CONV_W:2 * CONV_W]
        hu, hv = h[:, 2 * CONV_W:2 * CONV_W + GMLP_W], h[:, 2 * CONV_W + GMLP_W:2 * CONV_W + 2 * GMLP_W]
        q = h[:, IN_W - XATTN_W:]

        @pl.when(i == 0)
        def _():
            cbuf[0:CONV_HALO, :] = jnp.zeros((CONV_HALO, CONV_W), F32)

        cbuf[CONV_HALO:CONV_HALO + t, :] = a1 * _sigmoid(a2)
        ac = jnp.zeros((t, CONV_W), F32) + _row(pa_ref, PA_CONV_B)
        for r in range(8):
            zr = jnp.zeros((t + 8, CONV_W), F32)
            for a in range(4):
                o = 8 * a + r
                if o < CONV_K:
                    k = CONV_K - 1 - o
                    zr = zr + cbuf[CONV_HALO - 8 - 8 * a:CONV_HALO - 8 - 8 * a + t + 8, :] * cw_ref[k:k + 1, :]
            if r == 0:
                ac = ac + zr[8:, :]
            else:
                zbuf[...] = zr
                ac = ac + zbuf[8 - r:8 - r + t, :]
        ac_ref[...] = ac
        cbuf[0:CONV_HALO, :] = cbuf[t:t + CONV_HALO, :]
        xh_a, _ = _ln_fwd(ac)
        an = xh_a * _row(pa_ref, PA_LNA_G) + _row(pa_ref, PA_LNA_B)
        a = an * _sigmoid(an)

        u = _gelu(hu)
        xh_v, _ = _ln_fwd(_gelu(hv))
        vn = xh_v * _row(pa_ref, PA_LNV_G) + _row(pa_ref, PA_LNV_B)
        _spatial_mix(vn.astype(BF16), wst_ref, bst_ref, mixbuf, t)
        g = u * mixbuf[...]

        p = _softmax_heads(_dot(q.astype(BF16), kt_ref[...]) * ATT_SCALE)
        o = _dot(p.astype(BF16), v_ref[...])

        cat = jnp.concatenate([a, g, o], axis=1).astype(BF16)
        cat_ref[...] = cat
        z = ALPHA * x + _dot(cat, wout_ref[...])
        xh, rstd = _ln_fwd(z)
        xh_ref[...] = xh
        rstd_ref[...] = rstd
        if ns:
            pl.when(i == nt - 1)(gather_wait)

    tok = lambda w: pl.BlockSpec((t, w), lambda i: (i, 0))
    return pl.pallas_call(
        body, name="mixer_fwd_gather" if ns else "mixer_fwd", grid=(nt,),
        in_specs=[tok(D_MODEL), _layer_spec((8, D_MODEL), l), _layer_spec((IN_W, D_MODEL), 0, resident=True),
                  _layer_spec((CONV_HALO, CONV_W), l), _layer_spec((8, CONV_W), l),
                  _layer_spec((3, 2 * CHUNK, CHUNK), l), _layer_spec((CHUNK, GMLP_W), l),
                  _const_spec((XATTN_W, XATTN_HEADS * N_MEM)), _const_spec((XATTN_HEADS * N_MEM, XATTN_W)),
                  _layer_spec((D_MODEL, D_MODEL), 0, resident=True)] + side_in,
        out_specs=[tok(D_MODEL), tok(1), tok(IN_W), tok(CONV_W), tok(D_MODEL)] + side_out,
        out_shape=[jax.ShapeDtypeStruct((s, D_MODEL), F32), jax.ShapeDtypeStruct((s, 1), F32),
                   jax.ShapeDtypeStruct((s, IN_W), F32), jax.ShapeDtypeStruct((s, CONV_W), F32),
                   jax.ShapeDtypeStruct((s, D_MODEL), BF16)] + side_shapes,
        scratch_shapes=[pltpu.VMEM((t + CONV_HALO, CONV_W), F32), pltpu.VMEM((t, GMLP_W), F32),
                        pltpu.VMEM((t + 8, CONV_W), F32)] + side_scratch,
        compiler_params=_cparams(("arbitrary",)),
    )(xin, pd, win_t, conv_w, pa, wst, bst, kt_all, v_all, w_out, *side)


VD_LN_G, VD_LN_B, VD_LOSS = 0, 1, 2
VA_CONV_B, VA_LNA_G, VA_LNA_B, VA_LNV_G, VA_LNV_B = 0, 1, 2, 3, 4


def _mixer_bwd_d(gz, xh1, rstd1, h, ac, pd, conv_w, pa, wst, wstt, bst, kt_all, k_all, vt_all, w_out, win_t, l, t, side_ops=None):
    s = gz.shape[0]
    nt = s // t

    def body(gz_ref, xh_ref, rstd_ref, h_ref, ac_ref, pd_ref, cw_ref, pa_ref, wst_ref, wstt_ref, bst_ref,
             kt_ref, k_ref, vt_ref, wout_ref, wint_ref,
             dx_ref, dh_ref, dmix_ref, vd_ref, va_ref, dcw_ref, dws_ref, dbs_ref, dkt_ref, dv_ref,
             ebuf, mixbuf, dvnbuf, dbsacc, erbuf):
        i = pl.program_id(0)

        @pl.when(i == 0)
        def _():
            vd_ref[...] = jnp.zeros_like(vd_ref)
            va_ref[...] = jnp.zeros_like(va_ref)
            dcw_ref[...] = jnp.zeros_like(dcw_ref)
            dws_ref[...] = jnp.zeros_like(dws_ref)
            dbs_ref[...] = jnp.zeros_like(dbs_ref)
            dkt_ref[...] = jnp.zeros_like(dkt_ref)
            dv_ref[...] = jnp.zeros_like(dv_ref)
            dbsacc[...] = jnp.zeros_like(dbsacc)
            ebuf[t:t + CONV_HALO, :] = jnp.zeros((CONV_HALO, CONV_W), F32)

        gz_v = gz_ref[...]
        xh = xh_ref[...]
        vd_ref[VD_LN_G:VD_LN_G + 1, :] += _colsum(gz_v * xh)
        vd_ref[VD_LN_B:VD_LN_B + 1, :] += _colsum(gz_v)
        dz = _ln_bwd(gz_v * _row(pd_ref, PD_G1), xh, rstd_ref[...])
        dzb = dz.astype(BF16)
        dmix_ref[...] = dzb
        dcat = _dot_nt(dzb, wout_ref[...])
        d_a, d_g, d_o = dcat[:, 0:CONV_W], dcat[:, CONV_W:CONV_W + GMLP_W], dcat[:, CONV_W + GMLP_W:]

        h = h_ref[...]
        a1, a2 = h[:, 0:CONV_W], h[:, CONV_W:2 * CONV_W]
        hu, hv = h[:, 2 * CONV_W:2 * CONV_W + GMLP_W], h[:, 2 * CONV_W + GMLP_W:2 * CONV_W + 2 * GMLP_W]
        q = h[:, IN_W - XATTN_W:]

        xh_a, rstd_a = _ln_fwd(ac_ref[...])
        an = xh_a * _row(pa_ref, PA_LNA_G) + _row(pa_ref, PA_LNA_B)
        sig = _sigmoid(an)
        d_an = d_a * (sig * (1.0 + an * (1.0 - sig)))
        va_ref[VA_LNA_G:VA_LNA_G + 1, :] += _colsum(d_an * xh_a)
        va_ref[VA_LNA_B:VA_LNA_B + 1, :] += _colsum(d_an)
        dac = _ln_bwd(d_an * _row(pa_ref, PA_LNA_G), xh_a, rstd_a)
        va_ref[VA_CONV_B:VA_CONV_B + 1, :] += _colsum(dac)
        ebuf[0:t, :] = dac
        sg = _sigmoid(a2)
        glu = a1 * sg
        dglu = jnp.zeros((t, CONV_W), F32)
        for r in range(8):
            if r > 0:
                erbuf[...] = ebuf[r:r + t + 24, :]
            src = ebuf if r == 0 else erbuf
            for a in range(4):
                o = 8 * a + r
                if o < CONV_K:
                    k = CONV_K - 1 - o
                    ek = src[8 * a:8 * a + t, :]
                    dglu = dglu + ek * cw_ref[k:k + 1, :]
                    dcw_ref[k:k + 1, :] += _colsum(ek * glu)
        ebuf[t:t + CONV_HALO, :] = ebuf[0:CONV_HALO, :]
        da1 = dglu * sg
        da2 = dglu * a1 * sg * (1.0 - sg)

        u = _gelu(hu)
        xh_v, rstd_v = _ln_fwd(_gelu(hv))
        vn = xh_v * _row(pa_ref, PA_LNV_G) + _row(pa_ref, PA_LNV_B)
        vnb = vn.astype(BF16)
        _spatial_mix(vnb, wst_ref, bst_ref, mixbuf, t)
        dhu = d_g * mixbuf[...] * _gelu_grad(hu)
        dm = d_g * u
        dmb = dm.astype(BF16)
        lo = _lane_lo((CHUNK, LANE))
        for n in range(t // CHUNK):
            rows = slice(n * CHUNK, (n + 1) * CHUNK)
            dbsacc[...] += dm[rows, :]
            for j in range(GMLP_W // LANE):
                cols = slice(j * LANE, (j + 1) * LANE)
                dm_blk = dmb[rows, cols]
                r = _dot(wstt_ref[j], dm_blk)
                dvnbuf[rows, cols] = jnp.where(lo, r[:CHUNK], r[CHUNK:])
                zero = jnp.zeros_like(dm_blk)
                st = jnp.concatenate([jnp.where(lo, dm_blk, zero), jnp.where(lo, zero, dm_blk)], axis=0)
                dws_ref[j] += _dot_nt(st, vnb[rows, cols])
        dvn = dvnbuf[...]
        va_ref[VA_LNV_G:VA_LNV_G + 1, :] += _colsum(dvn * xh_v)
        va_ref[VA_LNV_B:VA_LNV_B + 1, :] += _colsum(dvn)
        dhv = _ln_bwd(dvn * _row(pa_ref, PA_LNV_G), xh_v, rstd_v) * _gelu_grad(hv)

        qb = q.astype(BF16)
        p = _softmax_heads(_dot(qb, kt_ref[...]) * ATT_SCALE)
        dob = d_o.astype(BF16)
        dp = _dot(dob, vt_ref[...])
        dss = []
        for hd in range(XATTN_HEADS):
            cs = slice(hd * N_MEM, (hd + 1) * N_MEM)
            ph, dph = p[:, cs], dp[:, cs]
            dss.append(ph * (dph - jnp.sum(ph * dph, axis=-1, keepdims=True)) * ATT_SCALE)
        dsb = jnp.concatenate(dss, axis=1).astype(BF16)
        dq = _dot(dsb, k_ref[...])
        dkt_ref[...] += _dot_tn(qb, dsb)
        dv_ref[...] += _dot_tn(p.astype(BF16), dob)

        dhb = jnp.concatenate([da1, da2, dhu, dhv, dq], axis=1).astype(BF16)
        dh_ref[...] = dhb
        dx_ref[...] = ALPHA * dz + _dot(dhb, wint_ref[...])

        @pl.when(i == nt - 1)
        def _():
            acc = dbsacc[...]
            head = lax.broadcasted_iota(jnp.int32, (CHUNK, GMLP_W), 1) // HEAD_DIM
            lane = lax.broadcasted_iota(jnp.int32, (CHUNK, LANE), 1)
            out = jnp.zeros((CHUNK, LANE), F32)
            for hd in range(GMLP_W // HEAD_DIM):
                sh = jnp.sum(jnp.where(head == hd, acc, 0.0), axis=1, keepdims=True)
                out = out + jnp.where(lane == hd, sh, 0.0)
            dbs_ref[...] = out

    rev = lambda w: pl.BlockSpec((t, w), lambda i: (nt - 1 - i, 0))
    out_shape = [
        jax.ShapeDtypeStruct((s, D_MODEL), F32), jax.ShapeDtypeStruct((s, IN_W), BF16),
        jax.ShapeDtypeStruct((s, D_MODEL), BF16),
        jax.ShapeDtypeStruct((8, D_MODEL), F32), jax.ShapeDtypeStruct((8, CONV_W), F32),
        jax.ShapeDtypeStruct((CONV_HALO, CONV_W), F32), jax.ShapeDtypeStruct((3, 2 * CHUNK, CHUNK), F32),
        jax.ShapeDtypeStruct((CHUNK, LANE), F32),
        jax.ShapeDtypeStruct((XATTN_W, XATTN_HEADS * N_MEM), F32), jax.ShapeDtypeStruct((XATTN_HEADS * N_MEM, XATTN_W), F32),
    ]
    out_specs = [rev(D_MODEL), rev(IN_W), rev(D_MODEL)] + [_const_spec(o.shape) for o in out_shape[3:]]
    return _call_with_side(
        body, side_ops, name="mixer_bwd_d", grid=(nt,),
        in_specs=[rev(D_MODEL), rev(D_MODEL), rev(1), rev(IN_W), rev(CONV_W),
                  _layer_spec((8, D_MODEL), l), _layer_spec((CONV_HALO, CONV_W), l), _layer_spec((8, CONV_W), l),
                  _layer_spec((3, 2 * CHUNK, CHUNK), l), _layer_spec((3, 2 * CHUNK, CHUNK), l),
                  _layer_spec((CHUNK, GMLP_W), l),
                  _const_spec((XATTN_W, XATTN_HEADS * N_MEM)), _const_spec((XATTN_HEADS * N_MEM, XATTN_W)),
                  _const_spec((XATTN_W, XATTN_HEADS * N_MEM)),
                  _layer_spec((D_MODEL, D_MODEL), 0, resident=True), _layer_spec((IN_W, D_MODEL), 0, resident=True)],
        out_specs=out_specs, out_shape=out_shape,
        scratch_shapes=[pltpu.VMEM((t + CONV_HALO, CONV_W), F32), pltpu.VMEM((t, GMLP_W), F32),
                        pltpu.VMEM((t, GMLP_W), F32), pltpu.VMEM((CHUNK, GMLP_W), F32),
                        pltpu.VMEM((t + 24, CONV_W), F32)],
        operands=(gz, xh1, rstd1, h, ac, pd, conv_w, pa, wst, wstt, bst, kt_all, k_all, vt_all, w_out, win_t),
        semantics=("arbitrary",))


def _mixer_bwd_w(xin, pd, dh, cat, dmix, l, t, side_ops=None):
    s = xin.shape[0]
    nt = s // t

    def body(x_ref, pd_ref, dh_ref, cat_ref, dmix_ref, dwin_ref, dwout_ref):
        @pl.when(pl.program_id(0) == 0)
        def _():
            dwin_ref[...] = jnp.zeros_like(dwin_ref)
            dwout_ref[...] = jnp.zeros_like(dwout_ref)

        xb = (x_ref[...] * _row(pd_ref, PD_GIN) + _row(pd_ref, PD_BIN)).astype(BF16)
        dwin_ref[...] += _dot_tn(dh_ref[...], xb)
        dwout_ref[...] += _dot_tn(cat_ref[...], dmix_ref[...])

    tok = lambda w: pl.BlockSpec((t, w), lambda i: (i, 0))
    return _call_with_side(
        body, side_ops, name="mixer_bwd_w", grid=(nt,),
        in_specs=[tok(D_MODEL), _layer_spec((8, D_MODEL), l), tok(IN_W), tok(D_MODEL), tok(D_MODEL)],
        out_specs=[_layer_spec((IN_W, D_MODEL), 0), _layer_spec((D_MODEL, D_MODEL), 0)],
        out_shape=[jax.ShapeDtypeStruct((1, IN_W, D_MODEL), F32), jax.ShapeDtypeStruct((1, D_MODEL, D_MODEL), F32)],
        scratch_shapes=[], operands=(xin, pd, dh, cat, dmix), semantics=("arbitrary",))


PF_W0, PF_B = 0, 3


def _ffn_fwd(xh1, pd, wup_t, pf, w_d, l, t, side=None):
    s = xh1.shape[0]
    nt = s // t
    side, side_in, side_out, side_shapes, side_scratch = _side_specs(side)
    ns = len(side)

    def body(xh_ref, pd_ref, wg_ref, wv_ref, pf_ref, wd_ref, *rest):
        shard_refs, rest = rest[:ns], rest[ns:]
        xh2_ref, rstd_ref, upg_ref, upv_ref = rest[:4]
        land_refs, rest = rest[4:4 + ns], rest[4 + ns:]
        fbuf = rest[0]
        i = pl.program_id(0)
        if ns:
            gather_start, gather_wait = _side_gather(shard_refs, land_refs, *rest[1:])
            pl.when(i == 0)(gather_start)

        @pl.when(i == 0)
        def _():
            fbuf[0:FFN_HALO, :] = jnp.zeros((FFN_HALO, FF_P), F32)

        x1 = xh_ref[...] * _row(pd_ref, PD_G1) + _row(pd_ref, PD_B1)
        xb = x1.astype(BF16)
        y = jnp.zeros((t, D_MODEL), F32)
        for hf in range(2):
            cs = slice(hf * FF_H, (hf + 1) * FF_H)
            ug = _dot_nt(xb, wg_ref[cs, :])
            uv = _dot_nt(xb, wv_ref[cs, :])
            upg_ref[:, cs] = ug
            upv_ref[:, cs] = uv
            fbuf[FFN_HALO:FFN_HALO + t, cs] = ug
            gate = jnp.zeros((t, FF_H), F32) + pf_ref[PF_B:PF_B + 1, cs]
            for k in range(FFN_CONV_K):
                off = FFN_HALO - (FFN_CONV_K - 1) + k
                gate = gate + fbuf[off:off + t, cs] * pf_ref[PF_W0 + k:PF_W0 + k + 1, cs]
            fbuf[0:FFN_HALO, cs] = fbuf[t:t + FFN_HALO, cs]
            hm = gate * _sigmoid(gate) * uv
            y = y + _dot(hm.astype(BF16), wd_ref[cs, :])
        xh2, rstd = _ln_fwd(ALPHA * x1 + y)
        xh2_ref[...] = xh2
        rstd_ref[...] = rstd
        if ns:
            pl.when(i == nt - 1)(gather_wait)

    tok = lambda w: pl.BlockSpec((t, w), lambda i: (i, 0))
    return pl.pallas_call(
        body, name="ffn_fwd_gather" if ns else "ffn_fwd", grid=(nt,),
        in_specs=[tok(D_MODEL), _layer_spec((8, D_MODEL), l),
                  _layer_spec((FF_P, D_MODEL), 0, 0, resident=True), _layer_spec((FF_P, D_MODEL), 0, 1, resident=True),
                  _layer_spec((8, FF_P), l), _layer_spec((FF_P, D_MODEL), 0, resident=True)] + side_in,
        out_specs=[tok(D_MODEL), tok(1), tok(FF_P), tok(FF_P)] + side_out,
        out_shape=[jax.ShapeDtypeStruct((s, D_MODEL), F32), jax.ShapeDtypeStruct((s, 1), F32),
                   jax.ShapeDtypeStruct((s, FF_P), F32), jax.ShapeDtypeStruct((s, FF_P), F32)] + side_shapes,
        scratch_shapes=[pltpu.VMEM((t + FFN_HALO, FF_P), F32)] + side_scratch,
        compiler_params=_cparams(("arbitrary",)),
    )(xh1, pd, wup_t, wup_t, pf, w_d, *side)


VF_W0, VF_B = 0, 3


def _ffn_bwd_d(gz_or_target, xh2, rstd2, upg, upv, pd, pf, w_d, wup_t, l, t, last, side_ops=None):
    s = xh2.shape[0]
    nt = s // t
    hb = t // FFN_HALO

    def body(gz_ref, xh2_ref, rstd_ref, upg_ref, halo_ref, upv_ref, pd_ref, pf_ref, wd_ref, wg_ref, wv_ref,
             dx_ref, dy_ref, dug_ref, duv_ref, hm_ref, vd_ref, vf_ref, gbuf, ebuf, s1buf, s2buf):
        i = pl.program_id(0)
        first_tile = i == nt - 1

        @pl.when(i == 0)
        def _():
            vd_ref[...] = jnp.zeros_like(vd_ref)
            vf_ref[...] = jnp.zeros_like(vf_ref)
            ebuf[t:t + FFN_HALO, :] = jnp.zeros((FFN_HALO, FF_P), F32)

        xh2_v = xh2_ref[...]
        if last:
            diff = xh2_v * _row(pd_ref, PD_G2) + _row(pd_ref, PD_B2) - gz_ref[...]
            vd_ref[VD_LOSS:VD_LOSS + 1, :] += _colsum(diff * diff)
            gz_v = diff * (1.0 / D_MODEL)
        else:
            gz_v = gz_ref[...]
        vd_ref[VD_LN_G:VD_LN_G + 1, :] += _colsum(gz_v * xh2_v)
        vd_ref[VD_LN_B:VD_LN_B + 1, :] += _colsum(gz_v)
        dz = _ln_bwd(gz_v * _row(pd_ref, PD_G2), xh2_v, rstd_ref[...])
        dyb = dz.astype(BF16)
        dy_ref[...] = dyb
        dx = ALPHA * dz
        for hf in range(2):
            cs = slice(hf * FF_H, (hf + 1) * FF_H)
            ug = upg_ref[:, cs]
            uv = upv_ref[:, cs]
            halo = halo_ref[:, cs]
            gbuf[0:FFN_HALO, :] = jnp.where(first_tile, jnp.zeros_like(halo), halo)
            gbuf[FFN_HALO:FFN_HALO + t, :] = ug
            s1buf[...] = gbuf[FFN_HALO - 1:FFN_HALO - 1 + t, :]
            s2buf[...] = gbuf[FFN_HALO - 2:FFN_HALO - 2 + t, :]
            ug1 = s1buf[...]
            ug2 = s2buf[...]
            gate = (pf_ref[PF_B:PF_B + 1, cs] + ug2 * pf_ref[PF_W0:PF_W0 + 1, cs] + ug1 * pf_ref[PF_W0 + 1:PF_W0 + 2, cs]
                    + ug * pf_ref[PF_W0 + 2:PF_W0 + 3, cs])
            sig = _sigmoid(gate)
            sl = gate * sig
            hm_ref[:, cs] = sl * uv
            dhm = _dot_nt(dyb, wd_ref[cs, :])
            duv = dhm * sl
            dgate = dhm * uv * (sig * (1.0 + gate * (1.0 - sig)))
            vf_ref[VF_B:VF_B + 1, cs] += _colsum(dgate)
            vf_ref[VF_W0:VF_W0 + 1, cs] += _colsum(dgate * ug2)
            vf_ref[VF_W0 + 1:VF_W0 + 2, cs] += _colsum(dgate * ug1)
            vf_ref[VF_W0 + 2:VF_W0 + 3, cs] += _colsum(dgate * ug)
            ebuf[0:t, cs] = dgate
            dug = (ebuf[2:2 + t, cs] * pf_ref[PF_W0:PF_W0 + 1, cs] + ebuf[1:1 + t, cs] * pf_ref[PF_W0 + 1:PF_W0 + 2, cs]
                   + dgate * pf_ref[PF_W0 + 2:PF_W0 + 3, cs])
            ebuf[t:t + FFN_HALO, cs] = ebuf[0:FFN_HALO, cs]
            dugb = dug.astype(BF16)
            duvb = duv.astype(BF16)
            dug_ref[:, cs] = dugb
            duv_ref[:, cs] = duvb
            dx = dx + _dot(dugb, wg_ref[cs, :]) + _dot(duvb, wv_ref[cs, :])
        dx_ref[...] = dx

        if last:
            @pl.when(i == nt - 1)
            def _():
                tot = jnp.sum(vd_ref[VD_LOSS:VD_LOSS + 1, :], axis=1, keepdims=True)
                vd_ref[VD_LOSS:VD_LOSS + 1, :] = jnp.zeros((1, D_MODEL), F32) + tot

    rev = lambda w: pl.BlockSpec((t, w), lambda i: (nt - 1 - i, 0))
    halo_spec = pl.BlockSpec((FFN_HALO, FF_P), lambda i: (jnp.maximum((nt - 1 - i) * hb - 1, 0), 0))
    out_shape = [jax.ShapeDtypeStruct((s, D_MODEL), F32), jax.ShapeDtypeStruct((s, D_MODEL), BF16),
                 jax.ShapeDtypeStruct((s, FF_P), BF16), jax.ShapeDtypeStruct((s, FF_P), BF16),
                 jax.ShapeDtypeStruct((s, FF_P), F32),
                 jax.ShapeDtypeStruct((8, D_MODEL), F32), jax.ShapeDtypeStruct((8, FF_P), F32)]
    return _call_with_side(
        body, side_ops, name="ffn_bwd_d_last" if last else "ffn_bwd_d", grid=(nt,),
        in_specs=[rev(D_MODEL), rev(D_MODEL), rev(1), rev(FF_P), halo_spec, rev(FF_P),
                  _layer_spec((8, D_MODEL), l), _layer_spec((8, FF_P), l),
                  _layer_spec((FF_P, D_MODEL), 0, resident=True),
                  _layer_spec((FF_P, D_MODEL), 0, 0, resident=True), _layer_spec((FF_P, D_MODEL), 0, 1, resident=True)],
        out_specs=[rev(D_MODEL), rev(D_MODEL), rev(FF_P), rev(FF_P), rev(FF_P),
                   _const_spec((8, D_MODEL)), _const_spec((8, FF_P))],
        out_shape=out_shape,
        scratch_shapes=[pltpu.VMEM((t + FFN_HALO, FF_H), F32), pltpu.VMEM((t + FFN_HALO, FF_P), F32),
                        pltpu.VMEM((t, FF_H), F32), pltpu.VMEM((t, FF_H), F32)],
        operands=(gz_or_target, xh2, rstd2, upg, upg, upv, pd, pf, w_d, wup_t, wup_t), semantics=("arbitrary",))


def _ffn_bwd_w(xh1, pd, dy, dug, duv, hm, l, t, side_ops=None):
    s = xh1.shape[0]
    nt = s // t

    def body(xh_ref, pd_ref, dy_ref, dug_ref, duv_ref, hm_ref, dwup_ref, dwd_ref):
        @pl.when(pl.program_id(1) == 0)
        def _():
            dwup_ref[...] = jnp.zeros_like(dwup_ref)
            dwd_ref[...] = jnp.zeros_like(dwd_ref)

        xb = (xh_ref[...] * _row(pd_ref, PD_G1) + _row(pd_ref, PD_B1)).astype(BF16)
        dwup_ref[0] += _dot_tn(dug_ref[...], xb)
        dwup_ref[1] += _dot_tn(duv_ref[...], xb)
        dwd_ref[...] += _dot_tn(hm_ref[...].astype(BF16), dy_ref[...])

    tok = lambda w: pl.BlockSpec((t, w), lambda c, i: (i, 0))
    half = pl.BlockSpec((t, FF_H), lambda c, i: (i, c))
    return _call_with_side(
        body, side_ops, name="ffn_bwd_w", grid=(2, nt),
        in_specs=[tok(D_MODEL), pl.BlockSpec((None, 8, D_MODEL), lambda c, i: (l, 0, 0)), tok(D_MODEL), half, half, half],
        out_specs=[pl.BlockSpec((None, 2, FF_H, D_MODEL), lambda c, i: (0, 0, c, 0)),
                   pl.BlockSpec((None, FF_H, D_MODEL), lambda c, i: (0, c, 0))],
        out_shape=[jax.ShapeDtypeStruct((1, 2, FF_P, D_MODEL), F32), jax.ShapeDtypeStruct((1, FF_P, D_MODEL), F32)],
        scratch_shapes=[], operands=(xh1, pd, dy, dug, duv, hm), semantics=("arbitrary", "arbitrary"))


def _adamw_math(w, g, m, v):
    nm = ADAM_B1 * m + (1.0 - ADAM_B1) * g
    nv = ADAM_B2 * v + (1.0 - ADAM_B2) * (g * g)
    m_hat = nm / (1.0 - ADAM_B1 ** ADAM_STEP)
    v_hat = nv / (1.0 - ADAM_B2 ** ADAM_STEP)
    return -ADAM_LR * (m_hat / (jnp.sqrt(v_hat) + ADAM_EPS) + ADAM_WD * w), nm, nv


def _adamw_layers(w, gs, m, v, name):
    shp = w.shape
    _, rows, cols = shp
    tr = _row_tile(rows, cols * 4, mult=8)
    nb = rows // tr

    def body(w_ref, g0_ref, g1_ref, m_ref, v_ref, g_ref, d_ref, nm_ref, nv_ref):
        g = jnp.where(pl.program_id(0) == 0, g0_ref[...], g1_ref[...])
        g_ref[...] = g
        d_ref[...], nm_ref[...], nv_ref[...] = _adamw_math(w_ref[...], g, m_ref[...], v_ref[...])

    stacked = pl.BlockSpec((tr, cols), lambda l, i: (l * nb + i, 0))
    single = pl.BlockSpec((tr, cols), lambda l, i: (i, 0))
    sh = jax.ShapeDtypeStruct((DEPTH * rows, cols), F32)
    flat = lambda a: a.reshape(DEPTH * rows, cols)
    outs = pl.pallas_call(body, name=name, grid=(DEPTH, nb), in_specs=[stacked, single, single, stacked, stacked],
                          out_specs=[stacked] * 4, out_shape=[sh] * 4,
                          compiler_params=_cparams(("arbitrary", "arbitrary")))(flat(w), gs[0], gs[1], flat(m), flat(v))
    return [o.reshape(shp) for o in outs]


def _adamw_small(ws, gs, ms, vs):
    n = len(ws)

    def body(*refs):
        w_refs, g_refs, m_refs, v_refs = refs[:n], refs[n:2 * n], refs[2 * n:3 * n], refs[3 * n:4 * n]
        d_refs, nm_refs, nv_refs = refs[4 * n:5 * n], refs[5 * n:6 * n], refs[6 * n:7 * n]
        for k in range(n):
            d_refs[k][...], nm_refs[k][...], nv_refs[k][...] = _adamw_math(w_refs[k][...], g_refs[k][...], m_refs[k][...],
                                                                             v_refs[k][...])

    shapes = [jax.ShapeDtypeStruct(w.shape, F32) for w in ws]
    outs = pl.pallas_call(body, name="adamw_small", out_shape=shapes * 3, compiler_params=_cparams())(*ws, *gs, *ms, *vs)
    return outs[:n], outs[n:2 * n], outs[2 * n:]


def _all_gather_chips(tensors, name):
    n = len(tensors)
    halves = [a.shape[1] // 2 for a in tensors]

    def body(*refs):
        x_refs, out_refs = refs[:n], refs[n:2 * n]
        send_sems, recv_sems, local_sems = refs[2 * n:]
        x, y, c, chips = _my_place()
        me, sibling, mej = (x, y, c), (x, y, 1 - c), 2 * x + y

        def rows(tn, px, py, pc):
            return out_refs[tn].at[:, 2 * px + py, pl.ds(pc * halves[tn], halves[tn]), :]

        def copy(tn, k, block, to, src=None):
            return pltpu.make_async_remote_copy(
                src_ref=rows(tn, *block) if src is None else src, dst_ref=rows(tn, *block),
                send_sem=send_sems.at[tn, k], recv_sem=recv_sems.at[tn, k], device_id=to, device_id_type=MESH)

        mine_src = [x_refs[tn].at[:, pl.ds(c * halves[tn], halves[tn]), :] for tn in range(n)]
        mine = [pltpu.make_async_copy(mine_src[tn], rows(tn, *me), local_sems.at[tn]) for tn in range(n)]
        first = []
        for j, chip in enumerate(chips):
            first += [copy(tn, 1 + j, me, (*chip, c), src=mine_src[tn]) for tn in range(n)]
        first += [copy(tn, 0, me, sibling, src=mine_src[tn]) for tn in range(n)]
        for cp in first + mine:
            cp.start()
        passed = []
        for j, chip in enumerate(chips):
            for tn in range(n):
                copy(tn, 1 + j, (*chip, c), me).wait_recv()
                fwd = copy(tn, 4 + j, (*chip, c), sibling)
                fwd.start()
                passed.append(fwd)
        for tn in range(n):
            copy(tn, 0, sibling, me).wait_recv()
            for j, chip in enumerate(chips):
                copy(tn, 4 + j, (*chip, 1 - c), me).wait_recv()
        for cp in first + passed:
            cp.wait_send()
        for cp in mine:
            cp.wait()

    return pl.pallas_call(
        body, name=name,
        out_shape=[jax.ShapeDtypeStruct((a.shape[0], N_CHIPS) + a.shape[1:], a.dtype) for a in tensors],
        in_specs=[ANY] * n, out_specs=[ANY] * n,
        scratch_shapes=[pltpu.SemaphoreType.DMA((n, 7)), pltpu.SemaphoreType.DMA((n, 7)), pltpu.SemaphoreType.DMA((n,))],
    )(*tensors)


def _swap_op(g5s):
    n = len(g5s)

    def make(ins, outs, scr):
        send_sems, recv_sems = scr
        x, y, c, _ = _my_place()

        def copies():
            return [pltpu.make_async_remote_copy(
                src_ref=ins[tn].at[:, :, 1 - c], dst_ref=outs[tn], send_sem=send_sems.at[tn], recv_sem=recv_sems.at[tn],
                device_id=(x, y, 1 - c), device_id_type=MESH) for tn in range(n)]

        def start():
            for cp in copies():
                cp.start()

        def wait():
            for cp in copies():
                cp.wait()

        return start, wait

    return _SideOp(g5s, [jax.ShapeDtypeStruct(g.shape[:2] + g.shape[3:], g.dtype) for g in g5s],
                   [pltpu.SemaphoreType.DMA((n,)), pltpu.SemaphoreType.DMA((n,))], make)


def _scatter_op(parts):
    n = len(parts)

    def make(ins, outs, scr):
        send_sems, recv_sems = scr
        x, y, c, chips = _my_place()

        def copies():
            return [pltpu.make_async_remote_copy(
                src_ref=ins[tn].at[:, 2 * px + py], dst_ref=outs[tn].at[:, k],
                send_sem=send_sems.at[tn, k], recv_sem=recv_sems.at[tn, k],
                device_id=(px, py, c), device_id_type=MESH) for k, (px, py) in enumerate(chips) for tn in range(n)]

        def start():
            for cp in copies():
                cp.start()

        def wait():
            for cp in copies():
                cp.wait()

        return start, wait

    return _SideOp(parts, [jax.ShapeDtypeStruct((p.shape[0], 3) + p.shape[2:], p.dtype) for p in parts],
                   [pltpu.SemaphoreType.DMA((n, 3)), pltpu.SemaphoreType.DMA((n, 3))], make)


def _sgather_op(fs):
    n = len(fs)

    def make(ins, outs, scr):
        send_sems, recv_sems = scr
        x, y, c, _ = _my_place()

        def copy(tn, dst_half):
            return pltpu.make_async_remote_copy(
                src_ref=outs[tn].at[:, c], dst_ref=outs[tn].at[:, dst_half], send_sem=send_sems.at[tn],
                recv_sem=recv_sems.at[tn], device_id=(x, y, 1 - c), device_id_type=MESH)

        def start():
            for tn in range(n):
                copy(tn, c).start()

        def wait():
            for tn in range(n):
                copy(tn, 1 - c).wait_recv()
                copy(tn, c).wait_send()

        return start, wait

    return _SideOp(fs, [jax.ShapeDtypeStruct(f.shape, f.dtype) for f in fs],
                   [pltpu.SemaphoreType.DMA((n,)), pltpu.SemaphoreType.DMA((n,))], make, aliases={tn: tn for tn in range(n)})


def _run_side_ops(ops, name):
    return _call_with_side(lambda: None, ops, name=name, grid=(1,), in_specs=[], out_specs=[], out_shape=[],
                           scratch_shapes=[], operands=(), semantics=("arbitrary",))[1]


def _gather_devices_op(xs):
    def make(ins, outs, scr):
        send_sems, recv_sems, local_sem = scr
        x, y, c, chips = _my_place()
        peers = [(x, y, 1 - c)] + [(px, py, pc) for (px, py) in chips for pc in (c, 1 - c)]
        me = 4 * x + 2 * y + c

        def copy(k, slot):
            return pltpu.make_async_remote_copy(
                src_ref=ins[0], dst_ref=outs[0].at[slot], send_sem=send_sems.at[k], recv_sem=recv_sems.at[k],
                device_id=peers[k], device_id_type=MESH)

        def local():
            return pltpu.make_async_copy(ins[0], outs[0].at[me], local_sem)

        def start():
            for k in range(7):
                copy(k, me).start()
            local().start()

        def wait():
            for k, (px, py, pc) in enumerate(peers):
                copy(k, 4 * px + 2 * py + pc).wait_recv()
                copy(k, me).wait_send()
            local().wait()

        return start, wait

    return _SideOp([xs], [jax.ShapeDtypeStruct((8,) + xs.shape, xs.dtype)],
                   [pltpu.SemaphoreType.DMA((7,)), pltpu.SemaphoreType.DMA((7,)), pltpu.SemaphoreType.DMA], make)


def _add_halves(gs, recvs, place):
    n = len(gs)

    def body(place_ref, *refs):
        g_refs, r_refs, o_refs = refs[:n], refs[n:2 * n], refs[2 * n:]
        for tn in range(n):
            o_refs[tn][...] = (g_refs[tn][...] + r_refs[tn][...]).astype(BF16)

    def gspec(g):
        return pl.BlockSpec((None, None, None) + g.shape[3:], lambda l, j, p: (l, j, p[1], 0, 0))

    def rspec(r):
        return pl.BlockSpec((None, None) + r.shape[2:], lambda l, j, p: (l, j, 0, 0))

    grid_spec = pltpu.PrefetchScalarGridSpec(
        num_scalar_prefetch=1, grid=(gs[0].shape[0], N_CHIPS),
        in_specs=[gspec(g) for g in gs] + [rspec(r) for r in recvs], out_specs=[rspec(r) for r in recvs])
    return pl.pallas_call(body, name="rs_add", grid_spec=grid_spec,
                          out_shape=[jax.ShapeDtypeStruct(r.shape, BF16) for r in recvs],
                          compiler_params=_cparams(("arbitrary", "arbitrary")))(place, *gs, *recvs)


def _sum_slots(parts, slots, place):
    n = len(parts)

    def body(place_ref, *refs):
        p_refs, s_refs, o_refs = refs[:n], refs[n:2 * n], refs[2 * n:]
        for tn in range(n):
            acc = p_refs[tn][...].astype(F32)
            for k in range(3):
                acc = acc + s_refs[tn][k].astype(F32)
            o_refs[tn][...] = acc

    def pspec(p):
        return pl.BlockSpec((None, None) + p.shape[2:], lambda l, pl_: (l, pl_[0], 0, 0))

    def sspec(sl):
        return pl.BlockSpec((None,) + sl.shape[1:], lambda l, pl_: (l, 0, 0, 0))

    def ospec(p):
        return pl.BlockSpec((None, None) + p.shape[2:], lambda l, pl_: (l, pl_[1], 0, 0))

    grid_spec = pltpu.PrefetchScalarGridSpec(
        num_scalar_prefetch=1, grid=(parts[0].shape[0],),
        in_specs=[pspec(p) for p in parts] + [sspec(sl) for sl in slots], out_specs=[ospec(p) for p in parts])
    return pl.pallas_call(body, name="rs_sum", grid_spec=grid_spec,
                          out_shape=[jax.ShapeDtypeStruct((p.shape[0], 2) + p.shape[2:], F32) for p in parts],
                          compiler_params=_cparams(("arbitrary",)))(place, *parts, *slots)


def _sum_devices(gathered, m_per):
    def body(g_ref, o_ref):
        acc = g_ref[0:m_per, :]
        for d in range(1, 8):
            acc = acc + g_ref[d * m_per:(d + 1) * m_per, :]
        o_ref[...] = acc

    return pl.pallas_call(body, name="small_sum", out_shape=jax.ShapeDtypeStruct((m_per, LANE), F32),
                          compiler_params=_cparams())(gathered)


def _pad_ff_cols(a):
    lead = a.shape[:-1]
    n = a.shape[-1] // FF_Q
    a = a.reshape(*lead, n, FF_Q)
    a = jnp.pad(a, [(0, 0)] * len(lead) + [(0, 0), (0, FF_QP - FF_Q)])
    return a.reshape(*lead, n * FF_QP)


def _unpad_ff_cols(a):
    lead = a.shape[:-1]
    n = a.shape[-1] // FF_QP
    return a.reshape(*lead, n, FF_QP)[..., :FF_Q].reshape(*lead, n * FF_Q)


def _pack_small(parts):
    flat = jnp.concatenate([p.reshape(-1) for p in parts])
    return flat.reshape(-1, LANE)


SMALL_SHAPES = [("conv_a_w", (CONV_K, CONV_W)), ("conv_a_b", (CONV_W,)), ("ln_a_g", (CONV_W,)), ("ln_a_b", (CONV_W,)),
                ("ln_v_g", (GMLP_W,)), ("ln_v_b", (GMLP_W,)), ("w_s", (6, CHUNK, CHUNK)), ("b_s", (6, CHUNK)),
                ("ln1_g", (D_MODEL,)), ("ln1_b", (D_MODEL,)), ("conv_f_w", (FFN_CONV_K, D_FF)), ("conv_f_b", (D_FF,)),
                ("ln2_g", (D_MODEL,)), ("ln2_b", (D_MODEL,))]


def _unpack_small(flat2d):
    flat = flat2d.reshape(DEPTH, -1)
    out, o = {}, 0
    for name, shp in SMALL_SHAPES:
        n = 1
        for d in shp:
            n *= d
        out[name] = flat[:, o:o + n].reshape((DEPTH,) + shp)
        o += n
    return out


def _rows8(rows):
    blk = jnp.stack(rows, axis=1)
    return jnp.pad(blk, ((0, 0), (0, 8 - len(rows)), (0, 0)))


def kernel(x, mem, w_in, conv_a_w, conv_a_b, ln_a_g, ln_a_b, ln_v_g, ln_v_b, w_s, b_s, w_mk, w_mv, w_out, ln1_g, ln1_b, w_up, conv_f_w, conv_f_b, w_down, ln2_g, ln2_b, loss_target, m_w_in, m_conv_a_w, m_conv_a_b, m_ln_a_g, m_ln_a_b, m_ln_v_g, m_ln_v_b, m_w_s, m_b_s, m_w_mk, m_w_mv, m_w_out, m_ln1_g, m_ln1_b, m_w_up, m_conv_f_w, m_conv_f_b, m_w_down, m_ln2_g, m_ln2_b, v_w_in, v_conv_a_w, v_conv_a_b, v_ln_a_g, v_ln_a_b, v_ln_v_g, v_ln_v_b, v_w_s, v_b_s, v_w_mk, v_w_mv, v_w_out, v_ln1_g, v_ln1_b, v_w_up, v_conv_f_w, v_conv_f_b, v_w_down, v_ln2_g, v_ln2_b):
    seq = x.shape[1]
    t_fwd = min(512, seq)
    t_bwd = min(256, seq)
    chip = 2 * lax.axis_index("x") + lax.axis_index("y")
    core = lax.axis_index("c")
    place = jnp.stack([chip, core]).astype(jnp.int32)
    x0 = x[0]
    mem0 = mem[0]
    target = loss_target[0]

    sh_in = w_in.transpose(0, 2, 1).astype(BF16)
    sh_mk, sh_mv, sh_out = w_mk.astype(BF16), w_mv.astype(BF16), w_out.astype(BF16)
    sh_up = _pad_ff_cols(w_up).transpose(0, 2, 1).astype(BF16)
    sh_dn = jnp.pad(w_down, ((0, 0), (0, FF_QP - FF_Q), (0, 0))).astype(BF16)

    def mixer_weights(g_in, g_mk, g_mv, g_out):
        return dict(win_t=g_in.reshape(1, IN_W, D_MODEL), wmk=g_mk.reshape(1, D_MODEL, XATTN_W),
                    wmv=g_mv.reshape(1, D_MODEL, XATTN_W), wout=g_out.reshape(1, D_MODEL, D_MODEL))

    def ffn_weights(g_up, g_dn):
        return dict(wup_t=g_up.reshape(1, 2, FF_P, D_MODEL), wdown=g_dn.reshape(1, FF_P, D_MODEL))

    n_ca = conv_a_w.size
    small_w = _pack_small([conv_a_w, conv_f_w, jnp.zeros((2 * 80 * LANE - n_ca - conv_f_w.size,), F32)])[None]
    *g_mixer0, small_g = _all_gather_chips([sh_in[:1], sh_mk[:1], sh_mv[:1], sh_out[:1], small_w], "ag_mixer0")
    wts = [mixer_weights(*g_mixer0), None]
    small_g = small_g.reshape(N_CHIPS, -1)
    conv_a_full = small_g[:, :n_ca].reshape(N_CHIPS, DEPTH, CONV_K, CONV_W // 4).transpose(1, 2, 0, 3).reshape(DEPTH, CONV_K, CONV_W)
    conv_f_full = small_g[:, n_ca:n_ca + conv_f_w.size].reshape(N_CHIPS, DEPTH, FFN_CONV_K, FF_Q).transpose(1, 2, 0, 3).reshape(DEPTH, FFN_CONV_K, D_FF)

    tril = jnp.tril(jnp.ones((CHUNK, CHUNK), dtype=bool))
    ws_m = jnp.where(tril, w_s, 0.0)
    wst = ws_m.reshape(DEPTH, 3, 2 * CHUNK, CHUNK).astype(BF16)
    wstt = ws_m.transpose(0, 1, 3, 2).reshape(DEPTH, 3, 2 * CHUNK, CHUNK).astype(BF16)
    bst = jnp.repeat(b_s.transpose(0, 2, 1), HEAD_DIM, axis=2)
    conv_w = jnp.pad(conv_a_full, ((0, 0), (0, CONV_HALO - CONV_K), (0, 0)))
    pa = _rows8([conv_a_b, ln_a_g, ln_a_b, ln_v_g, ln_v_b])
    gin = jnp.concatenate([jnp.ones((1, D_MODEL), F32), ln2_g[:DEPTH - 1]], axis=0)
    bin_ = jnp.concatenate([jnp.zeros((1, D_MODEL), F32), ln2_b[:DEPTH - 1]], axis=0)
    pd = _rows8([gin, bin_, ln1_g, ln1_b, ln2_g, ln2_b])
    pf = jnp.concatenate([_pad_ff_cols(conv_f_full), _pad_ff_cols(conv_f_b)[:, None, :],
                          jnp.zeros((DEPTH, 8 - FFN_CONV_K - 1, FF_P), F32)], axis=1)

    acts = []
    xin = x0
    for l in range(DEPTH):
        w = wts[l]
        kt_all, k_all, v_all, vt_all = _kv_fwd(mem0, w["wmk"], w["wmv"])
        side = [sh_up[0], sh_dn[0]] if l == 0 else None
        xh1, rstd1, h, ac, cat, *landed = _mixer_fwd(xin, pd, w["win_t"], conv_w, pa, wst, bst, kt_all, v_all, w["wout"],
                                                     l, t_fwd, side)
        if l == 0:
            w.update(ffn_weights(*landed))
        side = [sh_in[1], sh_mk[1], sh_mv[1], sh_out[1], sh_up[1], sh_dn[1]] if l == 0 else None
        xh2, rstd2, upg, upv, *landed = _ffn_fwd(xh1, pd, w["wup_t"], pf, w["wdown"], l, t_fwd, side)
        if l == 0:
            wts[1] = {**mixer_weights(*landed[:4]), **ffn_weights(*landed[4:])}
        acts.append(dict(xin=xin, kt_all=kt_all, k_all=k_all, vt_all=vt_all, xh1=xh1, rstd1=rstd1, h=h, ac=ac, cat=cat,
                         xh2=xh2, rstd2=rstd2, upg=upg, upv=upv))
        xin = xh2

    assert DEPTH == 2

    def halves_view(gs):
        return [g.reshape(1, N_CHIPS, 2, g.shape[1] // (2 * N_CHIPS), g.shape[2]) for g in gs]

    small = [None] * DEPTH
    small_packed = [None] * DEPTH
    small_gathered = [None] * DEPTH
    red_layers = [None] * DEPTH
    gz = target
    loss_sum = None
    g5_prev = None
    for l in reversed(range(DEPTH)):
        a, w = acts[l], wts[l]
        last = l == DEPTH - 1
        (dx1, dy, dug, duv, hm, vd2, vf), side = _ffn_bwd_d(
            gz, a["xh2"], a["rstd2"], a["upg"], a["upv"], pd, pf, w["wdown"], w["wup_t"], l, t_bwd, last,
            [_swap_op(g5_prev), _gather_devices_op(small_packed[l + 1])] if g5_prev else None)
        if last:
            loss_sum = vd2[VD_LOSS, 0]
        if g5_prev:
            small_gathered[l + 1] = side[1][0]
        parts_prev = _add_halves(g5_prev, side[0], place) if g5_prev else None
        (gw_up_t, gw_down), side = _ffn_bwd_w(a["xh1"], pd, dy, dug, duv, hm, l, t_fwd,
                                              [_scatter_op(parts_prev)] if g5_prev else None)
        halves_prev = _sum_slots(parts_prev, side[0], place) if g5_prev else None
        g5_ffn = halves_view([gw_up_t.reshape(1, 2 * FF_P, D_MODEL), gw_down])
        ops = ([_sgather_op(halves_prev)] if g5_prev else []) + ([_swap_op(g5_ffn)] if l == 0 else [])
        (dx0, dh, dmix, vd1, va, dcw, dws, dbs, dkt, dv), side = _mixer_bwd_d(
            dx1, a["xh1"], a["rstd1"], a["h"], a["ac"], pd, conv_w, pa, wst, wstt, bst,
            a["kt_all"], a["k_all"], a["vt_all"], w["wout"], w["win_t"], l, t_fwd, ops)
        if g5_prev:
            red_layers[l + 1] = side[0]
        parts_ffn = _add_halves(g5_ffn, side[-1], place) if l == 0 else None
        dws6 = jnp.where(tril, dws.reshape(6, CHUNK, CHUNK), 0.0)
        small[l] = [dcw[:CONV_K], va[VA_CONV_B], va[VA_LNA_G], va[VA_LNA_B], va[VA_LNV_G], va[VA_LNV_B], dws6,
                    dbs[:, :6].T, vd1[VD_LN_G], vd1[VD_LN_B],
                    _unpad_ff_cols(vf[VF_W0:VF_W0 + FFN_CONV_K]), _unpad_ff_cols(vf[VF_B]),
                    vd2[VD_LN_G], vd2[VD_LN_B]]
        small_packed[l] = _pack_small(small[l])
        ops = [_scatter_op(parts_ffn), _gather_devices_op(small_packed[l])] if l == 0 else None
        (gw_in_t, gw_out), side = _mixer_bwd_w(a["xin"], pd, dh, a["cat"], dmix, l, t_fwd, ops)
        gw_mk, gw_mv = _kv_bwd(mem0, dkt, dv)
        g5_mix = halves_view([gw_in_t, gw_mk, gw_mv, gw_out])
        if l == 0:
            small_gathered[l] = side[1][0]
            halves_ffn = _sum_slots(parts_ffn, side[0], place)
            red_ffn, recv_mix = _run_side_ops([_sgather_op(halves_ffn), _swap_op(g5_mix)], "rs_tail_swap")
            parts_mix = _add_halves(g5_mix, recv_mix, place)
            halves_mix = _sum_slots(parts_mix, _run_side_ops([_scatter_op(parts_mix)], "rs_tail_chips")[0], place)
            red_mix = _run_side_ops([_sgather_op(halves_mix)], "rs_tail_gather")[0]
            red_layers[0] = red_mix + red_ffn
        else:
            g5_prev = g5_mix + g5_ffn
        gz = dx0
    grad_x = gz[None]

    def shard_grads(red):
        r = [f.reshape(-1, f.shape[-1]) for f in red]
        return dict(w_in=r[0].T, w_mk=r[1], w_mv=r[2], w_out=r[3], w_up=_unpad_ff_cols(r[4].T), w_down=r[5][:FF_Q])

    big_grads = [shard_grads(red_layers[l]) for l in range(DEPTH)]

    m_small = small_gathered[0].shape[1]
    small_red = jnp.concatenate([_sum_devices(g.reshape(8 * m_small, LANE), m_small) for g in small_gathered], axis=0)
    sg = _unpack_small(small_red)
    g_conv_a_w = lax.dynamic_slice_in_dim(sg["conv_a_w"], chip * (CONV_W // 4), CONV_W // 4, axis=2)
    g_conv_f_w = lax.dynamic_slice_in_dim(sg["conv_f_w"], chip * FF_Q, FF_Q, axis=2)

    loss = 0.5 / D_MODEL * lax.psum(loss_sum, ("x", "y", "c"))

    grads = dict(conv_a_w=g_conv_a_w, conv_a_b=sg["conv_a_b"], ln_a_g=sg["ln_a_g"], ln_a_b=sg["ln_a_b"],
                 ln_v_g=sg["ln_v_g"], ln_v_b=sg["ln_v_b"], w_s=sg["w_s"], b_s=sg["b_s"], ln1_g=sg["ln1_g"], ln1_b=sg["ln1_b"],
                 conv_f_w=g_conv_f_w, conv_f_b=sg["conv_f_b"], ln2_g=sg["ln2_g"], ln2_b=sg["ln2_b"])
    weights = dict(w_in=w_in, conv_a_w=conv_a_w, conv_a_b=conv_a_b, ln_a_g=ln_a_g, ln_a_b=ln_a_b, ln_v_g=ln_v_g,
                   ln_v_b=ln_v_b, w_s=w_s, b_s=b_s, w_mk=w_mk, w_mv=w_mv, w_out=w_out, ln1_g=ln1_g, ln1_b=ln1_b,
                   w_up=w_up, conv_f_w=conv_f_w, conv_f_b=conv_f_b, w_down=w_down, ln2_g=ln2_g, ln2_b=ln2_b)
    mom_m = dict(w_in=m_w_in, conv_a_w=m_conv_a_w, conv_a_b=m_conv_a_b, ln_a_g=m_ln_a_g, ln_a_b=m_ln_a_b, ln_v_g=m_ln_v_g,
                 ln_v_b=m_ln_v_b, w_s=m_w_s, b_s=m_b_s, w_mk=m_w_mk, w_mv=m_w_mv, w_out=m_w_out, ln1_g=m_ln1_g,
                 ln1_b=m_ln1_b, w_up=m_w_up, conv_f_w=m_conv_f_w, conv_f_b=m_conv_f_b, w_down=m_w_down, ln2_g=m_ln2_g,
                 ln2_b=m_ln2_b)
    mom_v = dict(w_in=v_w_in, conv_a_w=v_conv_a_w, conv_a_b=v_conv_a_b, ln_a_g=v_ln_a_g, ln_a_b=v_ln_a_b, ln_v_g=v_ln_v_g,
                 ln_v_b=v_ln_v_b, w_s=v_w_s, b_s=v_b_s, w_mk=v_w_mk, w_mv=v_w_mv, w_out=v_w_out, ln1_g=v_ln1_g,
                 ln1_b=v_ln1_b, w_up=v_w_up, conv_f_w=v_conv_f_w, conv_f_b=v_conv_f_b, w_down=v_w_down, ln2_g=v_ln2_g,
                 ln2_b=v_ln2_b)
    names = list(weights)
    big_names = ["w_in", "w_mk", "w_mv", "w_out", "w_up", "w_down"]
    delta, new_m, new_v = {}, {}, {}
    for n in big_names:
        grads[n], delta[n], new_m[n], new_v[n] = _adamw_layers(weights[n], [big_grads[l][n] for l in range(DEPTH)],
                                                               mom_m[n], mom_v[n], "adamw_" + n)
    small_names = [n for n in names if n not in big_names]
    ds, nms, nvs = _adamw_small([weights[n] for n in small_names], [grads[n] for n in small_names],
                                [mom_m[n] for n in small_names], [mom_v[n] for n in small_names])
    for n, d, nm, nv in zip(small_names, ds, nms, nvs):
        delta[n], new_m[n], new_v[n] = d, nm, nv

    return (loss, grad_x, *[grads[n] for n in names], *[delta[n] for n in names],
            *[new_m[n] for n in names], *[new_v[n] for n in names])
```

```python
import jax
import jax.numpy as jnp
from jax import lax
from jax.experimental import pallas as pl
from jax.experimental.pallas import tpu as pltpu

F32 = jnp.float32
BF16 = jnp.bfloat16

D_MODEL = 1024
DEPTH = 2
CONV_W = 384
GMLP_W = 384
XATTN_W = 256
XATTN_HEADS = 4
HEAD_DIM = 64
IN_W = 1792
CONV_K = 31
CHUNK = 128
N_MEM = 256
D_FF = 2752
FFN_CONV_K = 3
ALPHA = (2.0 * DEPTH) ** 0.25
LN_EPS = 1e-5
ATT_SCALE = 1.0 / 8.0
ADAM_LR, ADAM_B1, ADAM_B2, ADAM_EPS, ADAM_WD, ADAM_STEP = 0.001, 0.9, 0.999, 1e-08, 0.01, 10

N_CHIPS = 4
FF_Q = D_FF // N_CHIPS
FF_QP = 704
FF_H = 2 * FF_QP
FF_P = 4 * FF_QP
LANE = 128
CONV_HALO = 32
FFN_HALO = 8
BF16_ROWS = 16
VMEM_LIMIT = 60 * 1024 * 1024

MESH = pl.DeviceIdType.MESH
ANY = pl.BlockSpec(memory_space=pl.ANY)


def _cparams(sem=None, vmem=VMEM_LIMIT):
    kw = {"vmem_limit_bytes": vmem}
    if sem is not None:
        kw["dimension_semantics"] = sem
    return pltpu.CompilerParams(**kw)


def _row_tile(rows, row_bytes, limit=2 << 20, mult=BF16_ROWS):
    if rows * row_bytes <= limit:
        return rows
    best = None
    for cand in range(mult, rows, mult):
        if rows % cand == 0 and cand * row_bytes <= limit:
            best = cand
    assert best is not None, (rows, row_bytes)
    return best


def _const_spec(shape):
    nd = len(shape)
    return pl.BlockSpec(shape, lambda *_: (0,) * nd)


def _layer_spec(shape, *lead, resident=False):
    nd = len(shape)
    kw = {"pipeline_mode": pl.Buffered(1)} if resident else {}
    return pl.BlockSpec((None,) * len(lead) + tuple(shape), lambda *_: tuple(lead) + (0,) * nd, **kw)


def _sigmoid(x):
    return jax.nn.sigmoid(x)


def _gelu(x):
    return jax.nn.gelu(x)


def _gelu_grad(x):
    c = 0.7978845608028654
    a = 0.044715
    t = jnp.tanh(c * (x + a * x * x * x))
    return 0.5 * (1.0 + t) + 0.5 * x * (1.0 - t * t) * c * (1.0 + 3.0 * a * x * x)


def _ln_fwd(z):
    mu = jnp.mean(z, axis=-1, keepdims=True)
    zc = z - mu
    var = jnp.mean(zc * zc, axis=-1, keepdims=True)
    rstd = lax.rsqrt(var + LN_EPS)
    return zc * rstd, rstd


def _ln_bwd(dxh, xh, rstd):
    m1 = jnp.mean(dxh, axis=-1, keepdims=True)
    m2 = jnp.mean(dxh * xh, axis=-1, keepdims=True)
    return rstd * (dxh - m1 - xh * m2)


def _colsum(a):
    return jnp.sum(a, axis=0, keepdims=True)


def _dot(a, b):
    return jnp.dot(a, b, preferred_element_type=F32)


def _dot_tn(a, b):
    return lax.dot_general(a, b, (((0,), (0,)), ((), ())), preferred_element_type=F32)


def _dot_nt(a, b):
    return lax.dot_general(a, b, (((1,), (1,)), ((), ())), preferred_element_type=F32)


def _softmax_heads(sc):
    ps = []
    for hd in range(XATTN_HEADS):
        s = sc[:, hd * N_MEM:(hd + 1) * N_MEM]
        e = jnp.exp(s - jnp.max(s, axis=-1, keepdims=True))
        ps.append(e / jnp.sum(e, axis=-1, keepdims=True))
    return jnp.concatenate(ps, axis=1)


def _lane_lo(shape):
    return (lax.broadcasted_iota(jnp.int32, shape, len(shape) - 1) % LANE) < HEAD_DIM


def _spatial_mix(vnb, wst_ref, bst_ref, mix_ref, t):
    lo = _lane_lo((CHUNK, LANE))
    for n in range(t // CHUNK):
        rows = slice(n * CHUNK, (n + 1) * CHUNK)
        for j in range(GMLP_W // LANE):
            cols = slice(j * LANE, (j + 1) * LANE)
            r = _dot(wst_ref[j], vnb[rows, cols])
            mix_ref[rows, cols] = jnp.where(lo, r[:CHUNK], r[CHUNK:]) + bst_ref[:, cols]


def _kv_fwd(mem, w_mk, w_mv):
    def body(mem_ref, wk_ref, wv_ref, kt_ref, k_ref, v_ref, vt_ref):
        mb = mem_ref[...].astype(BF16)
        k = _dot(mb, wk_ref[...])
        v = _dot(mb, wv_ref[...])
        col = lax.broadcasted_iota(jnp.int32, (N_MEM, XATTN_W), 1) // HEAD_DIM
        ks = [jnp.where(col == hd, k, 0.0) for hd in range(XATTN_HEADS)]
        vs = [jnp.where(col == hd, v, 0.0) for hd in range(XATTN_HEADS)]
        k_ref[...] = jnp.concatenate(ks, axis=0).astype(BF16)
        v_ref[...] = jnp.concatenate(vs, axis=0).astype(BF16)
        kt_ref[...] = jnp.concatenate([x.T for x in ks], axis=1).astype(BF16)
        vt_ref[...] = jnp.concatenate([x.T for x in vs], axis=1).astype(BF16)

    wide = jax.ShapeDtypeStruct((XATTN_W, XATTN_HEADS * N_MEM), BF16)
    tall = jax.ShapeDtypeStruct((XATTN_HEADS * N_MEM, XATTN_W), BF16)
    wspec = _layer_spec((D_MODEL, XATTN_W), 0)
    return pl.pallas_call(body, name="kv_fwd", grid=(1,),
                          in_specs=[_const_spec((N_MEM, D_MODEL)), wspec, wspec],
                          out_specs=[_const_spec(wide.shape), _const_spec(tall.shape), _const_spec(tall.shape),
                                     _const_spec(wide.shape)],
                          out_shape=(wide, tall, tall, wide), compiler_params=_cparams(("arbitrary",)))(mem, w_mk, w_mv)


def _kv_bwd(mem, dkt_all, dv_all):
    def body(mem_ref, dkt_ref, dv_ref, gk_ref, gv_ref):
        col = lax.broadcasted_iota(jnp.int32, (N_MEM, XATTN_W), 1) // HEAD_DIM
        dk = jnp.zeros((N_MEM, XATTN_W), F32)
        dv = jnp.zeros((N_MEM, XATTN_W), F32)
        for hd in range(XATTN_HEADS):
            dk = dk + jnp.where(col == hd, dkt_ref[:, hd * N_MEM:(hd + 1) * N_MEM].T, 0.0)
            dv = dv + jnp.where(col == hd, dv_ref[hd * N_MEM:(hd + 1) * N_MEM, :], 0.0)
        mb = mem_ref[...].astype(BF16)
        gk_ref[0] = _dot_tn(mb, dk.astype(BF16))
        gv_ref[0] = _dot_tn(mb, dv.astype(BF16))

    out = jax.ShapeDtypeStruct((1, D_MODEL, XATTN_W), F32)
    return pl.pallas_call(body, name="kv_bwd", out_shape=(out, out), compiler_params=_cparams())(mem, dkt_all, dv_all)


def _my_place():
    x, y, c = lax.axis_index("x"), lax.axis_index("y"), lax.axis_index("c")
    chips = [(1 - x, y), (x, 1 - y), (1 - x, 1 - y)]
    return x, y, c, chips


def _side_gather(shard_refs, land_refs, send_sems, recv_sems, local_sems):
    n = len(shard_refs)
    x, y, c, chips = _my_place()
    mej = 2 * x + y

    def remote(tn, k, slot):
        px, py = chips[k]
        return pltpu.make_async_remote_copy(
            src_ref=shard_refs[tn], dst_ref=land_refs[tn].at[slot], send_sem=send_sems.at[tn, k],
            recv_sem=recv_sems.at[tn, k], device_id=(px, py, c), device_id_type=MESH)

    def local(tn):
        return pltpu.make_async_copy(shard_refs[tn], land_refs[tn].at[mej], local_sems.at[tn])

    def start():
        for k in range(3):
            for tn in range(n):
                remote(tn, k, mej).start()
        for tn in range(n):
            local(tn).start()

    def wait():
        for k, (px, py) in enumerate(chips):
            for tn in range(n):
                remote(tn, k, 2 * px + py).wait_recv()
                remote(tn, k, mej).wait_send()
        for tn in range(n):
            local(tn).wait()

    return start, wait


def _side_specs(side):
    side = list(side or ())
    n = len(side)
    shapes = [jax.ShapeDtypeStruct((N_CHIPS,) + a.shape, a.dtype) for a in side]
    scratch = [pltpu.SemaphoreType.DMA((n, 3)), pltpu.SemaphoreType.DMA((n, 3)), pltpu.SemaphoreType.DMA((n,))] if n else []
    return side, [ANY] * n, [ANY] * n, shapes, scratch


class _SideOp:
    def __init__(self, ins, out_shapes, scratch, make, aliases=None):
        self.ins, self.out_shapes, self.scratch, self.make, self.aliases = list(ins), list(out_shapes), list(scratch), make, dict(aliases or {})


def _call_with_side(body, side_ops, *, name, grid, in_specs, out_specs, out_shape, scratch_shapes, operands, semantics):
    side_ops = list(side_ops or ())
    n_in, n_out, n_scr = len(in_specs), len(out_specs), len(scratch_shapes)
    s_ins = [a for op in side_ops for a in op.ins]
    s_outs = [o for op in side_ops for o in op.out_shapes]
    s_scr = [x for op in side_ops for x in op.scratch]
    aliases, oi, oo = {}, 0, 0
    for op in side_ops:
        for a, b in op.aliases.items():
            aliases[n_in + oi + a] = n_out + oo + b
        oi, oo = oi + len(op.ins), oo + len(op.out_shapes)

    def wrapped(*refs):
        ins, sins = refs[:n_in], refs[n_in:n_in + len(s_ins)]
        base = n_in + len(s_ins)
        outs, souts = refs[base:base + n_out], refs[base + n_out:base + n_out + len(s_outs)]
        base += n_out + len(s_outs)
        scr, sscr = refs[base:base + n_scr], refs[base + n_scr:]
        if side_ops:
            first = pl.program_id(0) == 0
            last = pl.program_id(0) == grid[0] - 1
            for d in range(1, len(grid)):
                first = jnp.logical_and(first, pl.program_id(d) == 0)
                last = jnp.logical_and(last, pl.program_id(d) == grid[d] - 1)
            hooks, a, b, c = [], 0, 0, 0
            for op in side_ops:
                hooks.append(op.make(sins[a:a + len(op.ins)], souts[b:b + len(op.out_shapes)], sscr[c:c + len(op.scratch)]))
                a, b, c = a + len(op.ins), b + len(op.out_shapes), c + len(op.scratch)

            @pl.when(first)
            def _():
                for start, _w in hooks:
                    start()

        body(*ins, *outs, *scr)
        if side_ops:
            @pl.when(last)
            def _():
                for _s, wait in hooks:
                    wait()

    res = pl.pallas_call(
        wrapped, name=name, grid=grid, in_specs=list(in_specs) + [ANY] * len(s_ins),
        out_specs=list(out_specs) + [ANY] * len(s_outs), out_shape=list(out_shape) + s_outs,
        scratch_shapes=list(scratch_shapes) + s_scr, input_output_aliases=aliases,
        compiler_params=_cparams(semantics),
    )(*operands, *s_ins)
    side_res, k = [], n_out
    for op in side_ops:
        side_res.append(list(res[k:k + len(op.out_shapes)]))
        k += len(op.out_shapes)
    return list(res[:n_out]), side_res


PA_CONV_B, PA_LNA_G, PA_LNA_B, PA_LNV_G, PA_LNV_B = 0, 1, 2, 3, 4
PD_GIN, PD_BIN, PD_G1, PD_B1, PD_G2, PD_B2 = 0, 1, 2, 3, 4, 5


def _row(ref, r):
    return ref[r:r + 1, :]


def _mixer_fwd(xin, pd, win_t, conv_w, pa, wst, bst, kt_all, v_all, w_out, l, t, side=None):
    s = xin.shape[0]
    nt = s // t
    side, side_in, side_out, side_shapes, side_scratch = _side_specs(side)
    ns = len(side)

    def body(x_ref, pd_ref, wint_ref, cw_ref, pa_ref, wst_ref, bst_ref, kt_ref, v_ref, wout_ref, *rest):
        shard_refs, rest = rest[:ns], rest[ns:]
        xh_ref, rstd_ref, h_ref, ac_ref, cat_ref = rest[:5]
        land_refs, rest = rest[5:5 + ns], rest[5 + ns:]
        cbuf, mixbuf, zbuf = rest[:3]
        i = pl.program_id(0)
        if ns:
            gather_start, gather_wait = _side_gather(shard_refs, land_refs, *rest[3:])
            pl.when(i == 0)(gather_start)
        x = x_ref[...] * _row(pd_ref, PD_GIN) + _row(pd_ref, PD_BIN)
        h = _dot_nt(x.astype(BF16), wint_ref[...])
        h_ref[...] = h
        a1, a2 = h[:, 0:CONV_W], h[:, CONV_W:2 * CONV_W]
        hu, hv = h[:, 2 * CONV_W:2 * CONV_W + GMLP_W], h[:, 2 * CONV_W + GMLP_W:2 * CONV_W + 2 * GMLP_W]
        q = h[:, IN_W - XATTN_W:]

        @pl.when(i == 0)
        def _():
            cbuf[0:CONV_HALO, :] = jnp.zeros((CONV_HALO, CONV_W), F32)

        cbuf[CONV_HALO:CONV_HALO + t, :] = a1 * _sigmoid(a2)
        ac = jnp.zeros((t, CONV_W), F32) + _row(pa_ref, PA_CONV_B)
        for r in range(8):
            zr = jnp.zeros((t + 8, CONV_W), F32)
            for a in range(4):
                o = 8 * a + r
                if o < CONV_K:
                    k = CONV_K - 1 - o
                    zr = zr + cbuf[CONV_HALO - 8 - 8 * a:CONV_HALO - 8 - 8 * a + t + 8, :] * cw_ref[k:k + 1, :]
            if r == 0:
                ac = ac + zr[8:, :]
            else:
                zbuf[...] = zr
                ac = ac + zbuf[8 - r:8 - r + t, :]
        ac_ref[...] = ac
        cbuf[0:CONV_HALO, :] = cbuf[t:t + CONV_HALO, :]
        xh_a, _ = _ln_fwd(ac)
        an = xh_a * _row(pa_ref, PA_LNA_G) + _row(pa_ref, PA_LNA_B)
        a = an * _sigmoid(an)

        u = _gelu(hu)
        xh_v, _ = _ln_fwd(_gelu(hv))
        vn = xh_v * _row(pa_ref, PA_LNV_G) + _row(pa_ref, PA_LNV_B)
        _spatial_mix(vn.astype(BF16), wst_ref, bst_ref, mixbuf, t)
        g = u * mixbuf[...]

        p = _softmax_heads(_dot(q.astype(BF16), kt_ref[...]) * ATT_SCALE)
        o = _dot(p.astype(BF16), v_ref[...])

        cat = jnp.concatenate([a, g, o], axis=1).astype(BF16)
        cat_ref[...] = cat
        z = ALPHA * x + _dot(cat, wout_ref[...])
        xh, rstd = _ln_fwd(z)
        xh_ref[...] = xh
        rstd_ref[...] = rstd
        if ns:
            pl.when(i == nt - 1)(gather_wait)

    tok = lambda w: pl.BlockSpec((t, w), lambda i: (i, 0))
    return pl.pallas_call(
        body, name="mixer_fwd_gather" if ns else "mixer_fwd", grid=(nt,),
        in_specs=[tok(D_MODEL), _layer_spec((8, D_MODEL), l), _layer_spec((IN_W, D_MODEL), 0, resident=True),
                  _layer_spec((CONV_HALO, CONV_W), l), _layer_spec((8, CONV_W), l),
                  _layer_spec((3, 2 * CHUNK, CHUNK), l), _layer_spec((CHUNK, GMLP_W), l),
                  _const_spec((XATTN_W, XATTN_HEADS * N_MEM)), _const_spec((XATTN_HEADS * N_MEM, XATTN_W)),
                  _layer_spec((D_MODEL, D_MODEL), 0, resident=True)] + side_in,
        out_specs=[tok(D_MODEL), tok(1), tok(IN_W), tok(CONV_W), tok(D_MODEL)] + side_out,
        out_shape=[jax.ShapeDtypeStruct((s, D_MODEL), F32), jax.ShapeDtypeStruct((s, 1), F32),
                   jax.ShapeDtypeStruct((s, IN_W), F32), jax.ShapeDtypeStruct((s, CONV_W), F32),
                   jax.ShapeDtypeStruct((s, D_MODEL), BF16)] + side_shapes,
        scratch_shapes=[pltpu.VMEM((t + CONV_HALO, CONV_W), F32), pltpu.VMEM((t, GMLP_W), F32),
                        pltpu.VMEM((t + 8, CONV_W), F32)] + side_scratch,
        compiler_params=_cparams(("arbitrary",)),
    )(xin, pd, win_t, conv_w, pa, wst, bst, kt_all, v_all, w_out, *side)


VD_LN_G, VD_LN_B, VD_LOSS = 0, 1, 2
VA_CONV_B, VA_LNA_G, VA_LNA_B, VA_LNV_G, VA_LNV_B = 0, 1, 2, 3, 4


def _mixer_bwd_d(gz, xh1, rstd1, h, ac, pd, conv_w, pa, wst, wstt, bst, kt_all, k_all, vt_all, w_out, win_t, l, t, side_ops=None):
    s = gz.shape[0]
    nt = s // t

    def body(gz_ref, xh_ref, rstd_ref, h_ref, ac_ref, pd_ref, cw_ref, pa_ref, wst_ref, wstt_ref, bst_ref,
             kt_ref, k_ref, vt_ref, wout_ref, wint_ref,
             dx_ref, dh_ref, dmix_ref, vd_ref, va_ref, dcw_ref, dws_ref, dbs_ref, dkt_ref, dv_ref,
             ebuf, mixbuf, dvnbuf, dbsacc, erbuf):
        i = pl.program_id(0)

        @pl.when(i == 0)
        def _():
            vd_ref[...] = jnp.zeros_like(vd_ref)
            va_ref[...] = jnp.zeros_like(va_ref)
            dcw_ref[...] = jnp.zeros_like(dcw_ref)
            dws_ref[...] = jnp.zeros_like(dws_ref)
            dbs_ref[...] = jnp.zeros_like(dbs_ref)
            dkt_ref[...] = jnp.zeros_like(dkt_ref)
            dv_ref[...] = jnp.zeros_like(dv_ref)
            dbsacc[...] = jnp.zeros_like(dbsacc)
            ebuf[t:t + CONV_HALO, :] = jnp.zeros((CONV_HALO, CONV_W), F32)

        gz_v = gz_ref[...]
        xh = xh_ref[...]
        vd_ref[VD_LN_G:VD_LN_G + 1, :] += _colsum(gz_v * xh)
        vd_ref[VD_LN_B:VD_LN_B + 1, :] += _colsum(gz_v)
        dz = _ln_bwd(gz_v * _row(pd_ref, PD_G1), xh, rstd_ref[...])
        dzb = dz.astype(BF16)
        dmix_ref[...] = dzb
        dcat = _dot_nt(dzb, wout_ref[...])
        d_a, d_g, d_o = dcat[:, 0:CONV_W], dcat[:, CONV_W:CONV_W + GMLP_W], dcat[:, CONV_W + GMLP_W:]

        h = h_ref[...]
        a1, a2 = h[:, 0:CONV_W], h[:, CONV_W:2 * CONV_W]
        hu, hv = h[:, 2 * CONV_W:2 * CONV_W + GMLP_W], h[:, 2 * CONV_W + GMLP_W:2 * CONV_W + 2 * GMLP_W]
        q = h[:, IN_W - XATTN_W:]

        xh_a, rstd_a = _ln_fwd(ac_ref[...])
        an = xh_a * _row(pa_ref, PA_LNA_G) + _row(pa_ref, PA_LNA_B)
        sig = _sigmoid(an)
        d_an = d_a * (sig * (1.0 + an * (1.0 - sig)))
        va_ref[VA_LNA_G:VA_LNA_G + 1, :] += _colsum(d_an * xh_a)
        va_ref[VA_LNA_B:VA_LNA_B + 1, :] += _colsum(d_an)
        dac = _ln_bwd(d_an * _row(pa_ref, PA_LNA_G), xh_a, rstd_a)
        va_ref[VA_CONV_B:VA_CONV_B + 1, :] += _colsum(dac)
        ebuf[0:t, :] = dac
        sg = _sigmoid(a2)
        glu = a1 * sg
        dglu = jnp.zeros((t, CONV_W), F32)
        for r in range(8):
            if r > 0:
                erbuf[...] = ebuf[r:r + t + 24, :]
            src = ebuf if r == 0 else erbuf
            for a in range(4):
                o = 8 * a + r
                if o < CONV_K:
                    k = CONV_K - 1 - o
                    ek = src[8 * a:8 * a + t, :]
                    dglu = dglu + ek * cw_ref[k:k + 1, :]
                    dcw_ref[k:k + 1, :] += _colsum(ek * glu)
        ebuf[t:t + CONV_HALO, :] = ebuf[0:CONV_HALO, :]
        da1 = dglu * sg
        da2 = dglu * a1 * sg * (1.0 - sg)

        u = _gelu(hu)
        xh_v, rstd_v = _ln_fwd(_gelu(hv))
        vn = xh_v * _row(pa_ref, PA_LNV_G) + _row(pa_ref, PA_LNV_B)
        vnb = vn.astype(BF16)
        _spatial_mix(vnb, wst_ref, bst_ref, mixbuf, t)
        dhu = d_g * mixbuf[...] * _gelu_grad(hu)
        dm = d_g * u
        dmb = dm.astype(BF16)
        lo = _lane_lo((CHUNK, LANE))
        for n in range(t // CHUNK):
            rows = slice(n * CHUNK, (n + 1) * CHUNK)
            dbsacc[...] += dm[rows, :]
            for j in range(GMLP_W // LANE):
                cols = slice(j * LANE, (j + 1) * LANE)
                dm_blk = dmb[rows, cols]
                r = _dot(wstt_ref[j], dm_blk)
                dvnbuf[rows, cols] = jnp.where(lo, r[:CHUNK], r[CHUNK:])
                zero = jnp.zeros_like(dm_blk)
                st = jnp.concatenate([jnp.where(lo, dm_blk, zero), jnp.where(lo, zero, dm_blk)], axis=0)
                dws_ref[j] += _dot_nt(st, vnb[rows, cols])
        dvn = dvnbuf[...]
        va_ref[VA_LNV_G:VA_LNV_G + 1, :] += _colsum(dvn * xh_v)
        va_ref[VA_LNV_B:VA_LNV_B + 1, :] += _colsum(dvn)
        dhv = _ln_bwd(dvn * _row(pa_ref, PA_LNV_G), xh_v, rstd_v) * _gelu_grad(hv)

        qb = q.astype(BF16)
        p = _softmax_heads(_dot(qb, kt_ref[...]) * ATT_SCALE)
        dob = d_o.astype(BF16)
        dp = _dot(dob, vt_ref[...])
        dss = []
        for hd in range(XATTN_HEADS):
            cs = slice(hd * N_MEM, (hd + 1) * N_MEM)
            ph, dph = p[:, cs], dp[:, cs]
            dss.append(ph * (dph - jnp.sum(ph * dph, axis=-1, keepdims=True)) * ATT_SCALE)
        dsb = jnp.concatenate(dss, axis=1).astype(BF16)
        dq = _dot(dsb, k_ref[...])
        dkt_ref[...] += _dot_tn(qb, dsb)
        dv_ref[...] += _dot_tn(p.astype(BF16), dob)

        dhb = jnp.concatenate([da1, da2, dhu, dhv, dq], axis=1).astype(BF16)
        dh_ref[...] = dhb
        dx_ref[...] = ALPHA * dz + _dot(dhb, wint_ref[...])

        @pl.when(i == nt - 1)
        def _():
            acc = dbsacc[...]
            head = lax.broadcasted_iota(jnp.int32, (CHUNK, GMLP_W), 1) // HEAD_DIM
            lane = lax.broadcasted_iota(jnp.int32, (CHUNK, LANE), 1)
            out = jnp.zeros((CHUNK, LANE), F32)
            for hd in range(GMLP_W // HEAD_DIM):
                sh = jnp.sum(jnp.where(head == hd, acc, 0.0), axis=1, keepdims=True)
                out = out + jnp.where(lane == hd, sh, 0.0)
            dbs_ref[...] = out

    rev = lambda w: pl.BlockSpec((t, w), lambda i: (nt - 1 - i, 0))
    out_shape = [
        jax.ShapeDtypeStruct((s, D_MODEL), F32), jax.ShapeDtypeStruct((s, IN_W), BF16),
        jax.ShapeDtypeStruct((s, D_MODEL), BF16),
        jax.ShapeDtypeStruct((8, D_MODEL), F32), jax.ShapeDtypeStruct((8, CONV_W), F32),
        jax.ShapeDtypeStruct((CONV_HALO, CONV_W), F32), jax.ShapeDtypeStruct((3, 2 * CHUNK, CHUNK), F32),
        jax.ShapeDtypeStruct((CHUNK, LANE), F32),
        jax.ShapeDtypeStruct((XATTN_W, XATTN_HEADS * N_MEM), F32), jax.ShapeDtypeStruct((XATTN_HEADS * N_MEM, XATTN_W), F32),
    ]
    out_specs = [rev(D_MODEL), rev(IN_W), rev(D_MODEL)] + [_const_spec(o.shape) for o in out_shape[3:]]
    return _call_with_side(
        body, side_ops, name="mixer_bwd_d", grid=(nt,),
        in_specs=[rev(D_MODEL), rev(D_MODEL), rev(1), rev(IN_W), rev(CONV_W),
                  _layer_spec((8, D_MODEL), l), _layer_spec((CONV_HALO, CONV_W), l), _layer_spec((8, CONV_W), l),
                  _layer_spec((3, 2 * CHUNK, CHUNK), l), _layer_spec((3, 2 * CHUNK, CHUNK), l),
                  _layer_spec((CHUNK, GMLP_W), l),
                  _const_spec((XATTN_W, XATTN_HEADS * N_MEM)), _const_spec((XATTN_HEADS * N_MEM, XATTN_W)),
                  _const_spec((XATTN_W, XATTN_HEADS * N_MEM)),
                  _layer_spec((D_MODEL, D_MODEL), 0, resident=True), _layer_spec((IN_W, D_MODEL), 0, resident=True)],
        out_specs=out_specs, out_shape=out_shape,
        scratch_shapes=[pltpu.VMEM((t + CONV_HALO, CONV_W), F32), pltpu.VMEM((t, GMLP_W), F32),
                        pltpu.VMEM((t, GMLP_W), F32), pltpu.VMEM((CHUNK, GMLP_W), F32),
                        pltpu.VMEM((t + 24, CONV_W), F32)],
        operands=(gz, xh1, rstd1, h, ac, pd, conv_w, pa, wst, wstt, bst, kt_all, k_all, vt_all, w_out, win_t),
        semantics=("arbitrary",))


def _mixer_bwd_w(xin, pd, dh, cat, dmix, l, t, side_ops=None):
    s = xin.shape[0]
    nt = s // t

    def body(x_ref, pd_ref, dh_ref, cat_ref, dmix_ref, dwin_ref, dwout_ref):
        @pl.when(pl.program_id(0) == 0)
        def _():
            dwin_ref[...] = jnp.zeros_like(dwin_ref)
            dwout_ref[...] = jnp.zeros_like(dwout_ref)

        xb = (x_ref[...] * _row(pd_ref, PD_GIN) + _row(pd_ref, PD_BIN)).astype(BF16)
        dwin_ref[...] += _dot_tn(dh_ref[...], xb)
        dwout_ref[...] += _dot_tn(cat_ref[...], dmix_ref[...])

    tok = lambda w: pl.BlockSpec((t, w), lambda i: (i, 0))
    return _call_with_side(
        body, side_ops, name="mixer_bwd_w", grid=(nt,),
        in_specs=[tok(D_MODEL), _layer_spec((8, D_MODEL), l), tok(IN_W), tok(D_MODEL), tok(D_MODEL)],
        out_specs=[_layer_spec((IN_W, D_MODEL), 0, resident=True), _layer_spec((D_MODEL, D_MODEL), 0, resident=True)],
        out_shape=[jax.ShapeDtypeStruct((1, IN_W, D_MODEL), F32), jax.ShapeDtypeStruct((1, D_MODEL, D_MODEL), F32)],
        scratch_shapes=[], operands=(xin, pd, dh, cat, dmix), semantics=("arbitrary",))


PF_W0, PF_B = 0, 3


def _ffn_fwd(xh1, pd, wup_t, pf, w_d, l, t, side=None):
    s = xh1.shape[0]
    nt = s // t
    side, side_in, side_out, side_shapes, side_scratch = _side_specs(side)
    ns = len(side)

    def body(xh_ref, pd_ref, wg_ref, wv_ref, pf_ref, wd_ref, *rest):
        shard_refs, rest = rest[:ns], rest[ns:]
        xh2_ref, rstd_ref, upg_ref, upv_ref = rest[:4]
        land_refs, rest = rest[4:4 + ns], rest[4 + ns:]
        fbuf = rest[0]
        i = pl.program_id(0)
        if ns:
            gather_start, gather_wait = _side_gather(shard_refs, land_refs, *rest[1:])
            pl.when(i == 0)(gather_start)

        @pl.when(i == 0)
        def _():
            fbuf[0:FFN_HALO, :] = jnp.zeros((FFN_HALO, FF_P), F32)

        x1 = xh_ref[...] * _row(pd_ref, PD_G1) + _row(pd_ref, PD_B1)
        xb = x1.astype(BF16)
        y = jnp.zeros((t, D_MODEL), F32)
        for hf in range(2):
            cs = slice(hf * FF_H, (hf + 1) * FF_H)
            ug = _dot_nt(xb, wg_ref[cs, :])
            uv = _dot_nt(xb, wv_ref[cs, :])
            upg_ref[:, cs] = ug
            upv_ref[:, cs] = uv
            fbuf[FFN_HALO:FFN_HALO + t, cs] = ug
            gate = jnp.zeros((t, FF_H), F32) + pf_ref[PF_B:PF_B + 1, cs]
            for k in range(FFN_CONV_K):
                off = FFN_HALO - (FFN_CONV_K - 1) + k
                gate = gate + fbuf[off:off + t, cs] * pf_ref[PF_W0 + k:PF_W0 + k + 1, cs]
            fbuf[0:FFN_HALO, cs] = fbuf[t:t + FFN_HALO, cs]
            hm = gate * _sigmoid(gate) * uv
            y = y + _dot(hm.astype(BF16), wd_ref[cs, :])
        xh2, rstd = _ln_fwd(ALPHA * x1 + y)
        xh2_ref[...] = xh2
        rstd_ref[...] = rstd
        if ns:
            pl.when(i == nt - 1)(gather_wait)

    tok = lambda w: pl.BlockSpec((t, w), lambda i: (i, 0))
    return pl.pallas_call(
        body, name="ffn_fwd_gather" if ns else "ffn_fwd", grid=(nt,),
        in_specs=[tok(D_MODEL), _layer_spec((8, D_MODEL), l),
                  _layer_spec((FF_P, D_MODEL), 0, 0, resident=True), _layer_spec((FF_P, D_MODEL), 0, 1, resident=True),
                  _layer_spec((8, FF_P), l), _layer_spec((FF_P, D_MODEL), 0, resident=True)] + side_in,
        out_specs=[tok(D_MODEL), tok(1), tok(FF_P), tok(FF_P)] + side_out,
        out_shape=[jax.ShapeDtypeStruct((s, D_MODEL), F32), jax.ShapeDtypeStruct((s, 1), F32),
                   jax.ShapeDtypeStruct((s, FF_P), F32), jax.ShapeDtypeStruct((s, FF_P), F32)] + side_shapes,
        scratch_shapes=[pltpu.VMEM((t + FFN_HALO, FF_P), F32)] + side_scratch,
        compiler_params=_cparams(("arbitrary",)),
    )(xh1, pd, wup_t, wup_t, pf, w_d, *side)


VF_W0, VF_B = 0, 3


def _ffn_bwd_d(gz_or_target, xh2, rstd2, upg, upv, pd, pf, w_d, wup_t, l, t, last, side_ops=None):
    s = xh2.shape[0]
    nt = s // t
    hb = t // FFN_HALO

    def body(gz_ref, xh2_ref, rstd_ref, upg_ref, halo_ref, upv_ref, pd_ref, pf_ref, wd_ref, wg_ref, wv_ref,
             dx_ref, dy_ref, dug_ref, duv_ref, hm_ref, vd_ref, vf_ref, gbuf, ebuf, s1buf, s2buf):
        i = pl.program_id(0)
        first_tile = i == nt - 1

        @pl.when(i == 0)
        def _():
            vd_ref[...] = jnp.zeros_like(vd_ref)
            vf_ref[...] = jnp.zeros_like(vf_ref)
            ebuf[t:t + FFN_HALO, :] = jnp.zeros((FFN_HALO, FF_P), F32)

        xh2_v = xh2_ref[...]
        if last:
            diff = xh2_v * _row(pd_ref, PD_G2) + _row(pd_ref, PD_B2) - gz_ref[...]
            vd_ref[VD_LOSS:VD_LOSS + 1, :] += _colsum(diff * diff)
            gz_v = diff * (1.0 / D_MODEL)
        else:
            gz_v = gz_ref[...]
        vd_ref[VD_LN_G:VD_LN_G + 1, :] += _colsum(gz_v * xh2_v)
        vd_ref[VD_LN_B:VD_LN_B + 1, :] += _colsum(gz_v)
        dz = _ln_bwd(gz_v * _row(pd_ref, PD_G2), xh2_v, rstd_ref[...])
        dyb = dz.astype(BF16)
        dy_ref[...] = dyb
        dx = ALPHA * dz
        for hf in range(2):
            cs = slice(hf * FF_H, (hf + 1) * FF_H)
            ug = upg_ref[:, cs]
            uv = upv_ref[:, cs]
            halo = halo_ref[:, cs]
            gbuf[0:FFN_HALO, :] = jnp.where(first_tile, jnp.zeros_like(halo), halo)
            gbuf[FFN_HALO:FFN_HALO + t, :] = ug
            s1buf[...] = gbuf[FFN_HALO - 1:FFN_HALO - 1 + t, :]
            s2buf[...] = gbuf[FFN_HALO - 2:FFN_HALO - 2 + t, :]
            ug1 = s1buf[...]
            ug2 = s2buf[...]
            gate = (pf_ref[PF_B:PF_B + 1, cs] + ug2 * pf_ref[PF_W0:PF_W0 + 1, cs] + ug1 * pf_ref[PF_W0 + 1:PF_W0 + 2, cs]
                    + ug * pf_ref[PF_W0 + 2:PF_W0 + 3, cs])
            sig = _sigmoid(gate)
            sl = gate * sig
            hm_ref[:, cs] = sl * uv
            dhm = _dot_nt(dyb, wd_ref[cs, :])
            duv = dhm * sl
            dgate = dhm * uv * (sig * (1.0 + gate * (1.0 - sig)))
            vf_ref[VF_B:VF_B + 1, cs] += _colsum(dgate)
            vf_ref[VF_W0:VF_W0 + 1, cs] += _colsum(dgate * ug2)
            vf_ref[VF_W0 + 1:VF_W0 + 2, cs] += _colsum(dgate * ug1)
            vf_ref[VF_W0 + 2:VF_W0 + 3, cs] += _colsum(dgate * ug)
            ebuf[0:t, cs] = dgate
            dug = (ebuf[2:2 + t, cs] * pf_ref[PF_W0:PF_W0 + 1, cs] + ebuf[1:1 + t, cs] * pf_ref[PF_W0 + 1:PF_W0 + 2, cs]
                   + dgate * pf_ref[PF_W0 + 2:PF_W0 + 3, cs])
            ebuf[t:t + FFN_HALO, cs] = ebuf[0:FFN_HALO, cs]
            dugb = dug.astype(BF16)
            duvb = duv.astype(BF16)
            dug_ref[:, cs] = dugb
            duv_ref[:, cs] = duvb
            dx = dx + _dot(dugb, wg_ref[cs, :]) + _dot(duvb, wv_ref[cs, :])
        dx_ref[...] = dx

        if last:
            @pl.when(i == nt - 1)
            def _():
                tot = jnp.sum(vd_ref[VD_LOSS:VD_LOSS + 1, :], axis=1, keepdims=True)
                vd_ref[VD_LOSS:VD_LOSS + 1, :] = jnp.zeros((1, D_MODEL), F32) + tot

    rev = lambda w: pl.BlockSpec((t, w), lambda i: (nt - 1 - i, 0))
    halo_spec = pl.BlockSpec((FFN_HALO, FF_P), lambda i: (jnp.maximum((nt - 1 - i) * hb - 1, 0), 0))
    out_shape = [jax.ShapeDtypeStruct((s, D_MODEL), F32), jax.ShapeDtypeStruct((s, D_MODEL), BF16),
                 jax.ShapeDtypeStruct((s, FF_P), BF16), jax.ShapeDtypeStruct((s, FF_P), BF16),
                 jax.ShapeDtypeStruct((s, FF_P), F32),
                 jax.ShapeDtypeStruct((8, D_MODEL), F32), jax.ShapeDtypeStruct((8, FF_P), F32)]
    return _call_with_side(
        body, side_ops, name="ffn_bwd_d_last" if last else "ffn_bwd_d", grid=(nt,),
        in_specs=[rev(D_MODEL), rev(D_MODEL), rev(1), rev(FF_P), halo_spec, rev(FF_P),
                  _layer_spec((8, D_MODEL), l), _layer_spec((8, FF_P), l),
                  _layer_spec((FF_P, D_MODEL), 0, resident=True),
                  _layer_spec((FF_P, D_MODEL), 0, 0, resident=True), _layer_spec((FF_P, D_MODEL), 0, 1, resident=True)],
        out_specs=[rev(D_MODEL), rev(D_MODEL), rev(FF_P), rev(FF_P), rev(FF_P),
                   _const_spec((8, D_MODEL)), _const_spec((8, FF_P))],
        out_shape=out_shape,
        scratch_shapes=[pltpu.VMEM((t + FFN_HALO, FF_H), F32), pltpu.VMEM((t + FFN_HALO, FF_P), F32),
                        pltpu.VMEM((t, FF_H), F32), pltpu.VMEM((t, FF_H), F32)],
        operands=(gz_or_target, xh2, rstd2, upg, upg, upv, pd, pf, w_d, wup_t, wup_t), semantics=("arbitrary",))


def _ffn_bwd_w(xh1, pd, dy, dug, duv, hm, l, t, side_ops=None):
    s = xh1.shape[0]
    nt = s // t

    def body(xh_ref, pd_ref, dy_ref, dug_ref, duv_ref, hm_ref, dwup_ref, dwd_ref):
        @pl.when(pl.program_id(1) == 0)
        def _():
            dwup_ref[...] = jnp.zeros_like(dwup_ref)
            dwd_ref[...] = jnp.zeros_like(dwd_ref)

        xb = (xh_ref[...] * _row(pd_ref, PD_G1) + _row(pd_ref, PD_B1)).astype(BF16)
        dwup_ref[0] += _dot_tn(dug_ref[...], xb)
        dwup_ref[1] += _dot_tn(duv_ref[...], xb)
        dwd_ref[...] += _dot_tn(hm_ref[...].astype(BF16), dy_ref[...])

    tok = lambda w: pl.BlockSpec((t, w), lambda c, i: (i, 0))
    half = pl.BlockSpec((t, FF_H), lambda c, i: (i, c))
    return _call_with_side(
        body, side_ops, name="ffn_bwd_w", grid=(2, nt),
        in_specs=[tok(D_MODEL), pl.BlockSpec((None, 8, D_MODEL), lambda c, i: (l, 0, 0)), tok(D_MODEL), half, half, half],
        out_specs=[pl.BlockSpec((None, 2, FF_H, D_MODEL), lambda c, i: (0, 0, c, 0), pipeline_mode=pl.Buffered(1)),
                   pl.BlockSpec((None, FF_H, D_MODEL), lambda c, i: (0, c, 0), pipeline_mode=pl.Buffered(1))],
        out_shape=[jax.ShapeDtypeStruct((1, 2, FF_P, D_MODEL), F32), jax.ShapeDtypeStruct((1, FF_P, D_MODEL), F32)],
        scratch_shapes=[], operands=(xh1, pd, dy, dug, duv, hm), semantics=("arbitrary", "arbitrary"))


def _adamw_math(w, g, m, v):
    nm = ADAM_B1 * m + (1.0 - ADAM_B1) * g
    nv = ADAM_B2 * v + (1.0 - ADAM_B2) * (g * g)
    m_hat = nm / (1.0 - ADAM_B1 ** ADAM_STEP)
    v_hat = nv / (1.0 - ADAM_B2 ** ADAM_STEP)
    return -ADAM_LR * (m_hat / (jnp.sqrt(v_hat) + ADAM_EPS) + ADAM_WD * w), nm, nv


def _adamw_layers(w, gs, m, v, name):
    shp = w.shape
    _, rows, cols = shp
    tr = _row_tile(rows, cols * 4, mult=8)
    nb = rows // tr

    def body(w_ref, g0_ref, g1_ref, m_ref, v_ref, g_ref, d_ref, nm_ref, nv_ref):
        g = jnp.where(pl.program_id(0) == 0, g0_ref[...], g1_ref[...])
        g_ref[...] = g
        d_ref[...], nm_ref[...], nv_ref[...] = _adamw_math(w_ref[...], g, m_ref[...], v_ref[...])

    stacked = pl.BlockSpec((tr, cols), lambda l, i: (l * nb + i, 0))
    single = pl.BlockSpec((tr, cols), lambda l, i: (i, 0))
    sh = jax.ShapeDtypeStruct((DEPTH * rows, cols), F32)
    flat = lambda a: a.reshape(DEPTH * rows, cols)
    outs = pl.pallas_call(body, name=name, grid=(DEPTH, nb), in_specs=[stacked, single, single, stacked, stacked],
                          out_specs=[stacked] * 4, out_shape=[sh] * 4,
                          compiler_params=_cparams(("arbitrary", "arbitrary")))(flat(w), gs[0], gs[1], flat(m), flat(v))
    return [o.reshape(shp) for o in outs]


def _adamw_small(ws, gs, ms, vs):
    n = len(ws)

    def body(*refs):
        w_refs, g_refs, m_refs, v_refs = refs[:n], refs[n:2 * n], refs[2 * n:3 * n], refs[3 * n:4 * n]
        d_refs, nm_refs, nv_refs = refs[4 * n:5 * n], refs[5 * n:6 * n], refs[6 * n:7 * n]
        for k in range(n):
            d_refs[k][...], nm_refs[k][...], nv_refs[k][...] = _adamw_math(w_refs[k][...], g_refs[k][...], m_refs[k][...],
                                                                             v_refs[k][...])

    shapes = [jax.ShapeDtypeStruct(w.shape, F32) for w in ws]
    outs = pl.pallas_call(body, name="adamw_small", out_shape=shapes * 3, compiler_params=_cparams())(*ws, *gs, *ms, *vs)
    return outs[:n], outs[n:2 * n], outs[2 * n:]


def _all_gather_chips(tensors, name):
    n = len(tensors)
    halves = [a.shape[1] // 2 for a in tensors]

    def body(*refs):
        x_refs, out_refs = refs[:n], refs[n:2 * n]
        send_sems, recv_sems, local_sems = refs[2 * n:]
        x, y, c, chips = _my_place()
        me, sibling, mej = (x, y, c), (x, y, 1 - c), 2 * x + y

        def rows(tn, px, py, pc):
            return out_refs[tn].at[:, 2 * px + py, pl.ds(pc * halves[tn], halves[tn]), :]

        def copy(tn, k, block, to, src=None):
            return pltpu.make_async_remote_copy(
                src_ref=rows(tn, *block) if src is None else src, dst_ref=rows(tn, *block),
                send_sem=send_sems.at[tn, k], recv_sem=recv_sems.at[tn, k], device_id=to, device_id_type=MESH)

        mine_src = [x_refs[tn].at[:, pl.ds(c * halves[tn], halves[tn]), :] for tn in range(n)]
        mine = [pltpu.make_async_copy(mine_src[tn], rows(tn, *me), local_sems.at[tn]) for tn in range(n)]
        first = []
        for j, chip in enumerate(chips):
            first += [copy(tn, 1 + j, me, (*chip, c), src=mine_src[tn]) for tn in range(n)]
        first += [copy(tn, 0, me, sibling, src=mine_src[tn]) for tn in range(n)]
        for cp in first + mine:
            cp.start()
        passed = []
        for j, chip in enumerate(chips):
            for tn in range(n):
                copy(tn, 1 + j, (*chip, c), me).wait_recv()
                fwd = copy(tn, 4 + j, (*chip, c), sibling)
                fwd.start()
                passed.append(fwd)
        for tn in range(n):
            copy(tn, 0, sibling, me).wait_recv()
            for j, chip in enumerate(chips):
                copy(tn, 4 + j, (*chip, 1 - c), me).wait_recv()
        for cp in first + passed:
            cp.wait_send()
        for cp in mine:
            cp.wait()

    return pl.pallas_call(
        body, name=name,
        out_shape=[jax.ShapeDtypeStruct((a.shape[0], N_CHIPS) + a.shape[1:], a.dtype) for a in tensors],
        in_specs=[ANY] * n, out_specs=[ANY] * n,
        scratch_shapes=[pltpu.SemaphoreType.DMA((n, 7)), pltpu.SemaphoreType.DMA((n, 7)), pltpu.SemaphoreType.DMA((n,))],
    )(*tensors)


def _swap_op(g5s):
    n = len(g5s)

    def make(ins, outs, scr):
        send_sems, recv_sems = scr
        x, y, c, _ = _my_place()

        def copies():
            return [pltpu.make_async_remote_copy(
                src_ref=ins[tn].at[:, :, 1 - c], dst_ref=outs[tn], send_sem=send_sems.at[tn], recv_sem=recv_sems.at[tn],
                device_id=(x, y, 1 - c), device_id_type=MESH) for tn in range(n)]

        def start():
            for cp in copies():
                cp.start()

        def wait():
            for cp in copies():
                cp.wait()

        return start, wait

    return _SideOp(g5s, [jax.ShapeDtypeStruct(g.shape[:2] + g.shape[3:], g.dtype) for g in g5s],
                   [pltpu.SemaphoreType.DMA((n,)), pltpu.SemaphoreType.DMA((n,))], make)


def _scatter_op(parts):
    n = len(parts)

    def make(ins, outs, scr):
        send_sems, recv_sems = scr
        x, y, c, chips = _my_place()

        def copies():
            return [pltpu.make_async_remote_copy(
                src_ref=ins[tn].at[:, 2 * px + py], dst_ref=outs[tn].at[:, k],
                send_sem=send_sems.at[tn, k], recv_sem=recv_sems.at[tn, k],
                device_id=(px, py, c), device_id_type=MESH) for k, (px, py) in enumerate(chips) for tn in range(n)]

        def start():
            for cp in copies():
                cp.start()

        def wait():
            for cp in copies():
                cp.wait()

        return start, wait

    return _SideOp(parts, [jax.ShapeDtypeStruct((p.shape[0], 3) + p.shape[2:], p.dtype) for p in parts],
                   [pltpu.SemaphoreType.DMA((n, 3)), pltpu.SemaphoreType.DMA((n, 3))], make)


def _sgather_op(fs):
    n = len(fs)

    def make(ins, outs, scr):
        send_sems, recv_sems = scr
        x, y, c, _ = _my_place()

        def copy(tn, dst_half):
            return pltpu.make_async_remote_copy(
                src_ref=outs[tn].at[:, c], dst_ref=outs[tn].at[:, dst_half], send_sem=send_sems.at[tn],
                recv_sem=recv_sems.at[tn], device_id=(x, y, 1 - c), device_id_type=MESH)

        def start():
            for tn in range(n):
                copy(tn, c).start()

        def wait():
            for tn in range(n):
                copy(tn, 1 - c).wait_recv()
                copy(tn, c).wait_send()

        return start, wait

    return _SideOp(fs, [jax.ShapeDtypeStruct(f.shape, f.dtype) for f in fs],
                   [pltpu.SemaphoreType.DMA((n,)), pltpu.SemaphoreType.DMA((n,))], make, aliases={tn: tn for tn in range(n)})


def _run_side_ops(ops, name):
    return _call_with_side(lambda: None, ops, name=name, grid=(1,), in_specs=[], out_specs=[], out_shape=[],
                           scratch_shapes=[], operands=(), semantics=("arbitrary",))[1]


def _gather_devices_op(xs):
    def make(ins, outs, scr):
        send_sems, recv_sems, local_sem = scr
        x, y, c, chips = _my_place()
        peers = [(x, y, 1 - c)] + [(px, py, pc) for (px, py) in chips for pc in (c, 1 - c)]
        me = 4 * x + 2 * y + c

        def copy(k, slot):
            return pltpu.make_async_remote_copy(
                src_ref=ins[0], dst_ref=outs[0].at[slot], send_sem=send_sems.at[k], recv_sem=recv_sems.at[k],
                device_id=peers[k], device_id_type=MESH)

        def local():
            return pltpu.make_async_copy(ins[0], outs[0].at[me], local_sem)

        def start():
            for k in range(7):
                copy(k, me).start()
            local().start()

        def wait():
            for k, (px, py, pc) in enumerate(peers):
                copy(k, 4 * px + 2 * py + pc).wait_recv()
                copy(k, me).wait_send()
            local().wait()

        return start, wait

    return _SideOp([xs], [jax.ShapeDtypeStruct((8,) + xs.shape, xs.dtype)],
                   [pltpu.SemaphoreType.DMA((7,)), pltpu.SemaphoreType.DMA((7,)), pltpu.SemaphoreType.DMA], make)


def _add_halves(gs, recvs, place):
    n = len(gs)

    def body(place_ref, *refs):
        g_refs, r_refs, o_refs = refs[:n], refs[n:2 * n], refs[2 * n:]
        for tn in range(n):
            o_refs[tn][...] = (g_refs[tn][...] + r_refs[tn][...]).astype(BF16)

    def gspec(g):
        return pl.BlockSpec((None, None, None) + g.shape[3:], lambda l, j, p: (l, j, p[1], 0, 0))

    def rspec(r):
        return pl.BlockSpec((None, None) + r.shape[2:], lambda l, j, p: (l, j, 0, 0))

    grid_spec = pltpu.PrefetchScalarGridSpec(
        num_scalar_prefetch=1, grid=(gs[0].shape[0], N_CHIPS),
        in_specs=[gspec(g) for g in gs] + [rspec(r) for r in recvs], out_specs=[rspec(r) for r in recvs])
    return pl.pallas_call(body, name="rs_add", grid_spec=grid_spec,
                          out_shape=[jax.ShapeDtypeStruct(r.shape, BF16) for r in recvs],
                          compiler_params=_cparams(("arbitrary", "arbitrary")))(place, *gs, *recvs)


def _sum_slots(parts, slots, place):
    n = len(parts)

    def body(place_ref, *refs):
        p_refs, s_refs, o_refs = refs[:n], refs[n:2 * n], refs[2 * n:]
        for tn in range(n):
            acc = p_refs[tn][...].astype(F32)
            for k in range(3):
                acc = acc + s_refs[tn][k].astype(F32)
            o_refs[tn][...] = acc

    def pspec(p):
        return pl.BlockSpec((None, None) + p.shape[2:], lambda l, pl_: (l, pl_[0], 0, 0))

    def sspec(sl):
        return pl.BlockSpec((None,) + sl.shape[1:], lambda l, pl_: (l, 0, 0, 0))

    def ospec(p):
        return pl.BlockSpec((None, None) + p.shape[2:], lambda l, pl_: (l, pl_[1], 0, 0))

    grid_spec = pltpu.PrefetchScalarGridSpec(
        num_scalar_prefetch=1, grid=(parts[0].shape[0],),
        in_specs=[pspec(p) for p in parts] + [sspec(sl) for sl in slots], out_specs=[ospec(p) for p in parts])
    return pl.pallas_call(body, name="rs_sum", grid_spec=grid_spec,
                          out_shape=[jax.ShapeDtypeStruct((p.shape[0], 2) + p.shape[2:], F32) for p in parts],
                          compiler_params=_cparams(("arbitrary",)))(place, *parts, *slots)


def _sum_devices(gathered, m_per):
    def body(g_ref, o_ref):
        acc = g_ref[0:m_per, :]
        for d in range(1, 8):
            acc = acc + g_ref[d * m_per:(d + 1) * m_per, :]
        o_ref[...] = acc

    return pl.pallas_call(body, name="small_sum", out_shape=jax.ShapeDtypeStruct((m_per, LANE), F32),
                          compiler_params=_cparams())(gathered)


def _pad_ff_cols(a):
    lead = a.shape[:-1]
    n = a.shape[-1] // FF_Q
    a = a.reshape(*lead, n, FF_Q)
    a = jnp.pad(a, [(0, 0)] * len(lead) + [(0, 0), (0, FF_QP - FF_Q)])
    return a.reshape(*lead, n * FF_QP)


def _unpad_ff_cols(a):
    lead = a.shape[:-1]
    n = a.shape[-1] // FF_QP
    return a.reshape(*lead, n, FF_QP)[..., :FF_Q].reshape(*lead, n * FF_Q)


def _pack_small(parts):
    flat = jnp.concatenate([p.reshape(-1) for p in parts])
    return flat.reshape(-1, LANE)


SMALL_SHAPES = [("conv_a_w", (CONV_K, CONV_W)), ("conv_a_b", (CONV_W,)), ("ln_a_g", (CONV_W,)), ("ln_a_b", (CONV_W,)),
                ("ln_v_g", (GMLP_W,)), ("ln_v_b", (GMLP_W,)), ("w_s", (6, CHUNK, CHUNK)), ("b_s", (6, CHUNK)),
                ("ln1_g", (D_MODEL,)), ("ln1_b", (D_MODEL,)), ("conv_f_w", (FFN_CONV_K, D_FF)), ("conv_f_b", (D_FF,)),
                ("ln2_g", (D_MODEL,)), ("ln2_b", (D_MODEL,))]


def _unpack_small(flat2d):
    flat = flat2d.reshape(DEPTH, -1)
    out, o = {}, 0
    for name, shp in SMALL_SHAPES:
        n = 1
        for d in shp:
            n *= d
        out[name] = flat[:, o:o + n].reshape((DEPTH,) + shp)
        o += n
    return out


def _rows8(rows):
    blk = jnp.stack(rows, axis=1)
    return jnp.pad(blk, ((0, 0), (0, 8 - len(rows)), (0, 0)))


def kernel(x, mem, w_in, conv_a_w, conv_a_b, ln_a_g, ln_a_b, ln_v_g, ln_v_b, w_s, b_s, w_mk, w_mv, w_out, ln1_g, ln1_b, w_up, conv_f_w, conv_f_b, w_down, ln2_g, ln2_b, loss_target, m_w_in, m_conv_a_w, m_conv_a_b, m_ln_a_g, m_ln_a_b, m_ln_v_g, m_ln_v_b, m_w_s, m_b_s, m_w_mk, m_w_mv, m_w_out, m_ln1_g, m_ln1_b, m_w_up, m_conv_f_w, m_conv_f_b, m_w_down, m_ln2_g, m_ln2_b, v_w_in, v_conv_a_w, v_conv_a_b, v_ln_a_g, v_ln_a_b, v_ln_v_g, v_ln_v_b, v_w_s, v_b_s, v_w_mk, v_w_mv, v_w_out, v_ln1_g, v_ln1_b, v_w_up, v_conv_f_w, v_conv_f_b, v_w_down, v_ln2_g, v_ln2_b):
    seq = x.shape[1]
    t_fwd = min(512, seq)
    t_bwd = min(256, seq)
    t_wg = min(1024, seq)
    chip = 2 * lax.axis_index("x") + lax.axis_index("y")
    core = lax.axis_index("c")
    place = jnp.stack([chip, core]).astype(jnp.int32)
    x0 = x[0]
    mem0 = mem[0]
    target = loss_target[0]

    sh_in = w_in.transpose(0, 2, 1).astype(BF16)
    sh_mk, sh_mv, sh_out = w_mk.astype(BF16), w_mv.astype(BF16), w_out.astype(BF16)
    sh_up = _pad_ff_cols(w_up).transpose(0, 2, 1).astype(BF16)
    sh_dn = jnp.pad(w_down, ((0, 0), (0, FF_QP - FF_Q), (0, 0))).astype(BF16)

    def mixer_weights(g_in, g_mk, g_mv, g_out):
        return dict(win_t=g_in.reshape(1, IN_W, D_MODEL), wmk=g_mk.reshape(1, D_MODEL, XATTN_W),
                    wmv=g_mv.reshape(1, D_MODEL, XATTN_W), wout=g_out.reshape(1, D_MODEL, D_MODEL))

    def ffn_weights(g_up, g_dn):
        return dict(wup_t=g_up.reshape(1, 2, FF_P, D_MODEL), wdown=g_dn.reshape(1, FF_P, D_MODEL))

    n_ca = conv_a_w.size
    small_w = _pack_small([conv_a_w, conv_f_w, jnp.zeros((2 * 80 * LANE - n_ca - conv_f_w.size,), F32)])[None]
    *g_mixer0, small_g = _all_gather_chips([sh_in[:1], sh_mk[:1], sh_mv[:1], sh_out[:1], small_w], "ag_mixer0")
    wts = [mixer_weights(*g_mixer0), None]
    small_g = small_g.reshape(N_CHIPS, -1)
    conv_a_full = small_g[:, :n_ca].reshape(N_CHIPS, DEPTH, CONV_K, CONV_W // 4).transpose(1, 2, 0, 3).reshape(DEPTH, CONV_K, CONV_W)
    conv_f_full = small_g[:, n_ca:n_ca + conv_f_w.size].reshape(N_CHIPS, DEPTH, FFN_CONV_K, FF_Q).transpose(1, 2, 0, 3).reshape(DEPTH, FFN_CONV_K, D_FF)

    tril = jnp.tril(jnp.ones((CHUNK, CHUNK), dtype=bool))
    ws_m = jnp.where(tril, w_s, 0.0)
    wst = ws_m.reshape(DEPTH, 3, 2 * CHUNK, CHUNK).astype(BF16)
    wstt = ws_m.transpose(0, 1, 3, 2).reshape(DEPTH, 3, 2 * CHUNK, CHUNK).astype(BF16)
    bst = jnp.repeat(b_s.transpose(0, 2, 1), HEAD_DIM, axis=2)
    conv_w = jnp.pad(conv_a_full, ((0, 0), (0, CONV_HALO - CONV_K), (0, 0)))
    pa = _rows8([conv_a_b, ln_a_g, ln_a_b, ln_v_g, ln_v_b])
    gin = jnp.concatenate([jnp.ones((1, D_MODEL), F32), ln2_g[:DEPTH - 1]], axis=0)
    bin_ = jnp.concatenate([jnp.zeros((1, D_MODEL), F32), ln2_b[:DEPTH - 1]], axis=0)
    pd = _rows8([gin, bin_, ln1_g, ln1_b, ln2_g, ln2_b])
    pf = jnp.concatenate([_pad_ff_cols(conv_f_full), _pad_ff_cols(conv_f_b)[:, None, :],
                          jnp.zeros((DEPTH, 8 - FFN_CONV_K - 1, FF_P), F32)], axis=1)

    acts = []
    xin = x0
    for l in range(DEPTH):
        w = wts[l]
        kt_all, k_all, v_all, vt_all = _kv_fwd(mem0, w["wmk"], w["wmv"])
        side = [sh_up[0], sh_dn[0]] if l == 0 else None
        xh1, rstd1, h, ac, cat, *landed = _mixer_fwd(xin, pd, w["win_t"], conv_w, pa, wst, bst, kt_all, v_all, w["wout"],
                                                     l, t_fwd, side)
        if l == 0:
            w.update(ffn_weights(*landed))
        side = [sh_in[1], sh_mk[1], sh_mv[1], sh_out[1], sh_up[1], sh_dn[1]] if l == 0 else None
        xh2, rstd2, upg, upv, *landed = _ffn_fwd(xh1, pd, w["wup_t"], pf, w["wdown"], l, t_fwd, side)
        if l == 0:
            wts[1] = {**mixer_weights(*landed[:4]), **ffn_weights(*landed[4:])}
        acts.append(dict(xin=xin, kt_all=kt_all, k_all=k_all, vt_all=vt_all, xh1=xh1, rstd1=rstd1, h=h, ac=ac, cat=cat,
                         xh2=xh2, rstd2=rstd2, upg=upg, upv=upv))
        xin = xh2

    assert DEPTH == 2

    def halves_view(gs):
        return [g.reshape(1, N_CHIPS, 2, g.shape[1] // (2 * N_CHIPS), g.shape[2]) for g in gs]

    small = [None] * DEPTH
    small_packed = [None] * DEPTH
    small_gathered = [None] * DEPTH
    red_layers = [None] * DEPTH
    gz = target
    loss_sum = None
    g5_prev = None
    for l in reversed(range(DEPTH)):
        a, w = acts[l], wts[l]
        last = l == DEPTH - 1
        (dx1, dy, dug, duv, hm, vd2, vf), side = _ffn_bwd_d(
            gz, a["xh2"], a["rstd2"], a["upg"], a["upv"], pd, pf, w["wdown"], w["wup_t"], l, t_bwd, last,
            [_swap_op(g5_prev), _gather_devices_op(small_packed[l + 1])] if g5_prev else None)
        if last:
            loss_sum = vd2[VD_LOSS, 0]
        if g5_prev:
            small_gathered[l + 1] = side[1][0]
        parts_prev = _add_halves(g5_prev, side[0], place) if g5_prev else None
        (gw_up_t, gw_down), side = _ffn_bwd_w(a["xh1"], pd, dy, dug, duv, hm, l, t_wg,
                                              [_scatter_op(parts_prev)] if g5_prev else None)
        halves_prev = _sum_slots(parts_prev, side[0], place) if g5_prev else None
        g5_ffn = halves_view([gw_up_t.reshape(1, 2 * FF_P, D_MODEL), gw_down])
        ops = ([_sgather_op(halves_prev)] if g5_prev else []) + ([_swap_op(g5_ffn)] if l == 0 else [])
        (dx0, dh, dmix, vd1, va, dcw, dws, dbs, dkt, dv), side = _mixer_bwd_d(
            dx1, a["xh1"], a["rstd1"], a["h"], a["ac"], pd, conv_w, pa, wst, wstt, bst,
            a["kt_all"], a["k_all"], a["vt_all"], w["wout"], w["win_t"], l, t_fwd, ops)
        if g5_prev:
            red_layers[l + 1] = side[0]
        parts_ffn = _add_halves(g5_ffn, side[-1], place) if l == 0 else None
        dws6 = jnp.where(tril, dws.reshape(6, CHUNK, CHUNK), 0.0)
        small[l] = [dcw[:CONV_K], va[VA_CONV_B], va[VA_LNA_G], va[VA_LNA_B], va[VA_LNV_G], va[VA_LNV_B], dws6,
                    dbs[:, :6].T, vd1[VD_LN_G], vd1[VD_LN_B],
                    _unpad_ff_cols(vf[VF_W0:VF_W0 + FFN_CONV_K]), _unpad_ff_cols(vf[VF_B]),
                    vd2[VD_LN_G], vd2[VD_LN_B]]
        small_packed[l] = _pack_small(small[l])
        ops = [_scatter_op(parts_ffn), _gather_devices_op(small_packed[l])] if l == 0 else None
        (gw_in_t, gw_out), side = _mixer_bwd_w(a["xin"], pd, dh, a["cat"], dmix, l, t_wg, ops)
        gw_mk, gw_mv = _kv_bwd(mem0, dkt, dv)
        g5_mix = halves_view([gw_in_t, gw_mk, gw_mv, gw_out])
        if l == 0:
            small_gathered[l] = side[1][0]
            halves_ffn = _sum_slots(parts_ffn, side[0], place)
            red_ffn, recv_mix = _run_side_ops([_sgather_op(halves_ffn), _swap_op(g5_mix)], "rs_tail_swap")
            parts_mix = _add_halves(g5_mix, recv_mix, place)
            halves_mix = _sum_slots(parts_mix, _run_side_ops([_scatter_op(parts_mix)], "rs_tail_chips")[0], place)
            red_mix = _run_side_ops([_sgather_op(halves_mix)], "rs_tail_gather")[0]
            red_layers[0] = red_mix + red_ffn
        else:
            g5_prev = g5_mix + g5_ffn
        gz = dx0
    grad_x = gz[None]

    def shard_grads(red):
        r = [f.reshape(-1, f.shape[-1]) for f in red]
        return dict(w_in=r[0].T, w_mk=r[1], w_mv=r[2], w_out=r[3], w_up=_unpad_ff_cols(r[4].T), w_down=r[5][:FF_Q])

    big_grads = [shard_grads(red_layers[l]) for l in range(DEPTH)]

    m_small = small_gathered[0].shape[1]
    small_red = jnp.concatenate([_sum_devices(g.reshape(8 * m_small, LANE), m_small) for g in small_gathered], axis=0)
    sg = _unpack_small(small_red)
    g_conv_a_w = lax.dynamic_slice_in_dim(sg["conv_a_w"], chip * (CONV_W // 4), CONV_W // 4, axis=2)
    g_conv_f_w = lax.dynamic_slice_in_dim(sg["conv_f_w"], chip * FF_Q, FF_Q, axis=2)

    loss = 0.5 / D_MODEL * lax.psum(loss_sum, ("x", "y", "c"))

    grads = dict(conv_a_w=g_conv_a_w, conv_a_b=sg["conv_a_b"], ln_a_g=sg["ln_a_g"], ln_a_b=sg["ln_a_b"],
                 ln_v_g=sg["ln_v_g"], ln_v_b=sg["ln_v_b"], w_s=sg["w_s"], b_s=sg["b_s"], ln1_g=sg["ln1_g"], ln1_b=sg["ln1_b"],
                 conv_f_w=g_conv_f_w, conv_f_b=sg["conv_f_b"], ln2_g=sg["ln2_g"], ln2_b=sg["ln2_b"])
    weights = dict(w_in=w_in, conv_a_w=conv_a_w, conv_a_b=conv_a_b, ln_a_g=ln_a_g, ln_a_b=ln_a_b, ln_v_g=ln_v_g,
                   ln_v_b=ln_v_b, w_s=w_s, b_s=b_s, w_mk=w_mk, w_mv=w_mv, w_out=w_out, ln1_g=ln1_g, ln1_b=ln1_b,
                   w_up=w_up, conv_f_w=conv_f_w, conv_f_b=conv_f_b, w_down=w_down, ln2_g=ln2_g, ln2_b=ln2_b)
    mom_m = dict(w_in=m_w_in, conv_a_w=m_conv_a_w, conv_a_b=m_conv_a_b, ln_a_g=m_ln_a_g, ln_a_b=m_ln_a_b, ln_v_g=m_ln_v_g,
                 ln_v_b=m_ln_v_b, w_s=m_w_s, b_s=m_b_s, w_mk=m_w_mk, w_mv=m_w_mv, w_out=m_w_out, ln1_g=m_ln1_g,
                 ln1_b=m_ln1_b, w_up=m_w_up, conv_f_w=m_conv_f_w, conv_f_b=m_conv_f_b, w_down=m_w_down, ln2_g=m_ln2_g,
                 ln2_b=m_ln2_b)
    mom_v = dict(w_in=v_w_in, conv_a_w=v_conv_a_w, conv_a_b=v_conv_a_b, ln_a_g=v_ln_a_g, ln_a_b=v_ln_a_b, ln_v_g=v_ln_v_g,
                 ln_v_b=v_ln_v_b, w_s=v_w_s, b_s=v_b_s, w_mk=v_w_mk, w_mv=v_w_mv, w_out=v_w_out, ln1_g=v_ln1_g,
                 ln1_b=v_ln1_b, w_up=v_w_up, conv_f_w=v_conv_f_w, conv_f_b=v_conv_f_b, w_down=v_w_down, ln2_g=v_ln2_g,
                 ln2_b=v_ln2_b)
    names = list(weights)
    big_names = ["w_in", "w_mk", "w_mv", "w_out", "w_up", "w_down"]
    delta, new_m, new_v = {}, {}, {}
    for n in big_names:
        grads[n], delta[n], new_m[n], new_v[n] = _adamw_layers(weights[n], [big_grads[l][n] for l in range(DEPTH)],
                                                               mom_m[n], mom_v[n], "adamw_" + n)
    small_names = [n for n in names if n not in big_names]
    ds, nms, nvs = _adamw_small([weights[n] for n in small_names], [grads[n] for n in small_names],
                                [mom_m[n] for n in small_names], [mom_v[n] for n in small_names])
    for n, d, nm, nv in zip(small_names, ds, nms, nvs):
        delta[n], new_m[n], new_v[n] = d, nm, nv

    return (loss, grad_x, *[grads[n] for n in names], *[delta[n] for n in names],
            *[new_m[n] for n in names], *[new_v[n] for n in names])
```

```python
import jax
import jax.numpy as jnp
from jax import lax
from jax.experimental import pallas as pl
from jax.experimental.pallas import tpu as pltpu

F32 = jnp.float32
BF16 = jnp.bfloat16

D_MODEL = 1024
DEPTH = 2
CONV_W = 384
GMLP_W = 384
XATTN_W = 256
XATTN_HEADS = 4
HEAD_DIM = 64
IN_W = 1792
CONV_K = 31
CHUNK = 128
N_MEM = 256
D_FF = 2752
FFN_CONV_K = 3
ALPHA = (2.0 * DEPTH) ** 0.25
LN_EPS = 1e-5
ATT_SCALE = 1.0 / 8.0
ADAM_LR, ADAM_B1, ADAM_B2, ADAM_EPS, ADAM_WD, ADAM_STEP = 0.001, 0.9, 0.999, 1e-08, 0.01, 10

N_CHIPS = 4
FF_Q = D_FF // N_CHIPS
FF_QP = 704
FF_H = 2 * FF_QP
FF_P = 4 * FF_QP
LANE = 128
CONV_HALO = 32
FFN_HALO = 8
BF16_ROWS = 16
VMEM_LIMIT = 60 * 1024 * 1024

MESH = pl.DeviceIdType.MESH
ANY = pl.BlockSpec(memory_space=pl.ANY)


def _cparams(sem=None, vmem=VMEM_LIMIT):
    kw = {"vmem_limit_bytes": vmem}
    if sem is not None:
        kw["dimension_semantics"] = sem
    return pltpu.CompilerParams(**kw)


def _row_tile(rows, row_bytes, limit=2 << 20, mult=BF16_ROWS):
    if rows * row_bytes <= limit:
        return rows
    best = None
    for cand in range(mult, rows, mult):
        if rows % cand == 0 and cand * row_bytes <= limit:
            best = cand
    assert best is not None, (rows, row_bytes)
    return best


def _const_spec(shape):
    nd = len(shape)
    return pl.BlockSpec(shape, lambda *_: (0,) * nd)


def _layer_spec(shape, *lead, resident=False):
    nd = len(shape)
    kw = {"pipeline_mode": pl.Buffered(1)} if resident else {}
    return pl.BlockSpec((None,) * len(lead) + tuple(shape), lambda *_: tuple(lead) + (0,) * nd, **kw)


def _sigmoid(x):
    return jax.nn.sigmoid(x)


def _gelu(x):
    return jax.nn.gelu(x)


def _gelu_grad(x):
    c = 0.7978845608028654
    a = 0.044715
    t = jnp.tanh(c * (x + a * x * x * x))
    return 0.5 * (1.0 + t) + 0.5 * x * (1.0 - t * t) * c * (1.0 + 3.0 * a * x * x)


def _ln_fwd(z):
    mu = jnp.mean(z, axis=-1, keepdims=True)
    zc = z - mu
    var = jnp.mean(zc * zc, axis=-1, keepdims=True)
    rstd = lax.rsqrt(var + LN_EPS)
    return zc * rstd, rstd


def _ln_bwd(dxh, xh, rstd):
    m1 = jnp.mean(dxh, axis=-1, keepdims=True)
    m2 = jnp.mean(dxh * xh, axis=-1, keepdims=True)
    return rstd * (dxh - m1 - xh * m2)


def _colsum(a):
    return jnp.sum(a, axis=0, keepdims=True)


def _dot(a, b):
    return jnp.dot(a, b, preferred_element_type=F32)


def _dot_tn(a, b):
    return lax.dot_general(a, b, (((0,), (0,)), ((), ())), preferred_element_type=F32)


def _dot_nt(a, b):
    return lax.dot_general(a, b, (((1,), (1,)), ((), ())), preferred_element_type=F32)


def _softmax_heads(sc):
    ps = []
    for hd in range(XATTN_HEADS):
        s = sc[:, hd * N_MEM:(hd + 1) * N_MEM]
        e = jnp.exp(s - jnp.max(s, axis=-1, keepdims=True))
        ps.append(e / jnp.sum(e, axis=-1, keepdims=True))
    return jnp.concatenate(ps, axis=1)


def _lane_lo(shape):
    return (lax.broadcasted_iota(jnp.int32, shape, len(shape) - 1) % LANE) < HEAD_DIM


def _spatial_mix(vnb, wst_ref, bst_ref, mix_ref, t):
    lo = _lane_lo((CHUNK, LANE))
    for n in range(t // CHUNK):
        rows = slice(n * CHUNK, (n + 1) * CHUNK)
        for j in range(GMLP_W // LANE):
            cols = slice(j * LANE, (j + 1) * LANE)
            r = _dot(wst_ref[j], vnb[rows, cols])
            mix_ref[rows, cols] = jnp.where(lo, r[:CHUNK], r[CHUNK:]) + bst_ref[:, cols]


def _kv_fwd(mem, w_mk, w_mv):
    def body(mem_ref, wk_ref, wv_ref, kt_ref, k_ref, v_ref, vt_ref):
        mb = mem_ref[...].astype(BF16)
        k = _dot(mb, wk_ref[...])
        v = _dot(mb, wv_ref[...])
        col = lax.broadcasted_iota(jnp.int32, (N_MEM, XATTN_W), 1) // HEAD_DIM
        ks = [jnp.where(col == hd, k, 0.0) for hd in range(XATTN_HEADS)]
        vs = [jnp.where(col == hd, v, 0.0) for hd in range(XATTN_HEADS)]
        k_ref[...] = jnp.concatenate(ks, axis=0).astype(BF16)
        v_ref[...] = jnp.concatenate(vs, axis=0).astype(BF16)
        kt_ref[...] = jnp.concatenate([x.T for x in ks], axis=1).astype(BF16)
        vt_ref[...] = jnp.concatenate([x.T for x in vs], axis=1).astype(BF16)

    wide = jax.ShapeDtypeStruct((XATTN_W, XATTN_HEADS * N_MEM), BF16)
    tall = jax.ShapeDtypeStruct((XATTN_HEADS * N_MEM, XATTN_W), BF16)
    wspec = _layer_spec((D_MODEL, XATTN_W), 0)
    return pl.pallas_call(body, name="kv_fwd", grid=(1,),
                          in_specs=[_const_spec((N_MEM, D_MODEL)), wspec, wspec],
                          out_specs=[_const_spec(wide.shape), _const_spec(tall.shape), _const_spec(tall.shape),
                                     _const_spec(wide.shape)],
                          out_shape=(wide, tall, tall, wide), compiler_params=_cparams(("arbitrary",)))(mem, w_mk, w_mv)


def _kv_bwd(mem, dkt_all, dv_all):
    def body(mem_ref, dkt_ref, dv_ref, gk_ref, gv_ref):
        col = lax.broadcasted_iota(jnp.int32, (N_MEM, XATTN_W), 1) // HEAD_DIM
        dk = jnp.zeros((N_MEM, XATTN_W), F32)
        dv = jnp.zeros((N_MEM, XATTN_W), F32)
        for hd in range(XATTN_HEADS):
            dk = dk + jnp.where(col == hd, dkt_ref[:, hd * N_MEM:(hd + 1) * N_MEM].T, 0.0)
            dv = dv + jnp.where(col == hd, dv_ref[hd * N_MEM:(hd + 1) * N_MEM, :], 0.0)
        mb = mem_ref[...].astype(BF16)
        gk_ref[0] = _dot_tn(mb, dk.astype(BF16))
        gv_ref[0] = _dot_tn(mb, dv.astype(BF16))

    out = jax.ShapeDtypeStruct((1, D_MODEL, XATTN_W), F32)
    return pl.pallas_call(body, name="kv_bwd", out_shape=(out, out), compiler_params=_cparams())(mem, dkt_all, dv_all)


def _my_place():
    x, y, c = lax.axis_index("x"), lax.axis_index("y"), lax.axis_index("c")
    chips = [(1 - x, y), (x, 1 - y), (1 - x, 1 - y)]
    return x, y, c, chips


def _side_gather(shard_refs, land_refs, send_sems, recv_sems, local_sems):
    n = len(shard_refs)
    x, y, c, chips = _my_place()
    mej = 2 * x + y

    def remote(tn, k, slot):
        px, py = chips[k]
        return pltpu.make_async_remote_copy(
            src_ref=shard_refs[tn], dst_ref=land_refs[tn].at[slot], send_sem=send_sems.at[tn, k],
            recv_sem=recv_sems.at[tn, k], device_id=(px, py, c), device_id_type=MESH)

    def local(tn):
        return pltpu.make_async_copy(shard_refs[tn], land_refs[tn].at[mej], local_sems.at[tn])

    def start():
        for k in range(3):
            for tn in range(n):
                remote(tn, k, mej).start()
        for tn in range(n):
            local(tn).start()

    def wait():
        for k, (px, py) in enumerate(chips):
            for tn in range(n):
                remote(tn, k, 2 * px + py).wait_recv()
                remote(tn, k, mej).wait_send()
        for tn in range(n):
            local(tn).wait()

    return start, wait


def _side_specs(side):
    side = list(side or ())
    n = len(side)
    shapes = [jax.ShapeDtypeStruct((N_CHIPS,) + a.shape, a.dtype) for a in side]
    scratch = [pltpu.SemaphoreType.DMA((n, 3)), pltpu.SemaphoreType.DMA((n, 3)), pltpu.SemaphoreType.DMA((n,))] if n else []
    return side, [ANY] * n, [ANY] * n, shapes, scratch


class _SideOp:
    def __init__(self, ins, out_shapes, scratch, make, aliases=None):
        self.ins, self.out_shapes, self.scratch, self.make, self.aliases = list(ins), list(out_shapes), list(scratch), make, dict(aliases or {})


def _call_with_side(body, side_ops, *, name, grid, in_specs, out_specs, out_shape, scratch_shapes, operands, semantics):
    side_ops = list(side_ops or ())
    n_in, n_out, n_scr = len(in_specs), len(out_specs), len(scratch_shapes)
    s_ins = [a for op in side_ops for a in op.ins]
    s_outs = [o for op in side_ops for o in op.out_shapes]
    s_scr = [x for op in side_ops for x in op.scratch]
    aliases, oi, oo = {}, 0, 0
    for op in side_ops:
        for a, b in op.aliases.items():
            aliases[n_in + oi + a] = n_out + oo + b
        oi, oo = oi + len(op.ins), oo + len(op.out_shapes)

    def wrapped(*refs):
        ins, sins = refs[:n_in], refs[n_in:n_in + len(s_ins)]
        base = n_in + len(s_ins)
        outs, souts = refs[base:base + n_out], refs[base + n_out:base + n_out + len(s_outs)]
        base += n_out + len(s_outs)
        scr, sscr = refs[base:base + n_scr], refs[base + n_scr:]
        if side_ops:
            first = pl.program_id(0) == 0
            last = pl.program_id(0) == grid[0] - 1
            for d in range(1, len(grid)):
                first = jnp.logical_and(first, pl.program_id(d) == 0)
                last = jnp.logical_and(last, pl.program_id(d) == grid[d] - 1)
            hooks, a, b, c = [], 0, 0, 0
            for op in side_ops:
                hooks.append(op.make(sins[a:a + len(op.ins)], souts[b:b + len(op.out_shapes)], sscr[c:c + len(op.scratch)]))
                a, b, c = a + len(op.ins), b + len(op.out_shapes), c + len(op.scratch)

            @pl.when(first)
            def _():
                for start, _w in hooks:
                    start()

        body(*ins, *outs, *scr)
        if side_ops:
            @pl.when(last)
            def _():
                for _s, wait in hooks:
                    wait()

    res = pl.pallas_call(
        wrapped, name=name, grid=grid, in_specs=list(in_specs) + [ANY] * len(s_ins),
        out_specs=list(out_specs) + [ANY] * len(s_outs), out_shape=list(out_shape) + s_outs,
        scratch_shapes=list(scratch_shapes) + s_scr, input_output_aliases=aliases,
        compiler_params=_cparams(semantics),
    )(*operands, *s_ins)
    side_res, k = [], n_out
    for op in side_ops:
        side_res.append(list(res[k:k + len(op.out_shapes)]))
        k += len(op.out_shapes)
    return list(res[:n_out]), side_res


PA_CONV_B, PA_LNA_G, PA_LNA_B, PA_LNV_G, PA_LNV_B = 0, 1, 2, 3, 4
PD_GIN, PD_BIN, PD_G1, PD_B1, PD_G2, PD_B2 = 0, 1, 2, 3, 4, 5


def _row(ref, r):
    return ref[r:r + 1, :]


def _mixer_fwd(xin, pd, win_t, conv_w, pa, wst, bst, kt_all, v_all, w_out, l, t, side=None):
    s = xin.shape[0]
    nt = s // t
    side, side_in, side_out, side_shapes, side_scratch = _side_specs(side)
    ns = len(side)

    def body(x_ref, pd_ref, wint_ref, cw_ref, pa_ref, wst_ref, bst_ref, kt_ref, v_ref, wout_ref, *rest):
        shard_refs, rest = rest[:ns], rest[ns:]
        xh_ref, rstd_ref, h_ref, ac_ref, cat_ref = rest[:5]
        land_refs, rest = rest[5:5 + ns], rest[5 + ns:]
        cbuf, mixbuf, zbuf = rest[:3]
        i = pl.program_id(0)
        if ns:
            gather_start, gather_wait = _side_gather(shard_refs, land_refs, *rest[3:])
            pl.when(i == 0)(gather_start)
        x = x_ref[...] * _row(pd_ref, PD_GIN) + _row(pd_ref, PD_BIN)
        h = _dot_nt(x.astype(BF16), wint_ref[...])
        h_ref[...] = h
        a1, a2 = h[:, 0:CONV_W], h[:, CONV_W:2 * CONV_W]
        hu, hv = h[:, 2 * CONV_W:2 * CONV_W + GMLP_W], h[:, 2 * CONV_W + GMLP_W:2 * CONV_W + 2 * GMLP_W]
        q = h[:, IN_W - XATTN_W:]

        @pl.when(i == 0)
        def _():
            cbuf[0:CONV_HALO, :] = jnp.zeros((CONV_HALO, CONV_W), F32)

        cbuf[CONV_HALO:CONV_HALO + t, :] = a1 * _sigmoid(a2)
        ac = jnp.zeros((t, CONV_W), F32) + _row(pa_ref, PA_CONV_B)
        for r in range(8):
            zr = jnp.zeros((t + 8, CONV_W), F32)
            for a in range(4):
                o = 8 * a + r
                if o < CONV_K:
                    k = CONV_K - 1 - o
                    zr = zr + cbuf[CONV_HALO - 8 - 8 * a:CONV_HALO - 8 - 8 * a + t + 8, :] * cw_ref[k:k + 1, :]
            if r == 0:
                ac = ac + zr[8:, :]
            else:
                zbuf[...] = zr
                ac = ac + zbuf[8 - r:8 - r + t, :]
        ac_ref[...] = ac
        cbuf[0:CONV_HALO, :] = cbuf[t:t + CONV_HALO, :]
        xh_a, _ = _ln_fwd(ac)
        an = xh_a * _row(pa_ref, PA_LNA_G) + _row(pa_ref, PA_LNA_B)
        a = an * _sigmoid(an)

        u = _gelu(hu)
        xh_v, _ = _ln_fwd(_gelu(hv))
        vn = xh_v * _row(pa_ref, PA_LNV_G) + _row(pa_ref, PA_LNV_B)
        _spatial_mix(vn.astype(BF16), wst_ref, bst_ref, mixbuf, t)
        g = u * mixbuf[...]

        p = _softmax_heads(_dot(q.astype(BF16), kt_ref[...]) * ATT_SCALE)
        o = _dot(p.astype(BF16), v_ref[...])

        cat = jnp.concatenate([a, g, o], axis=1).astype(BF16)
        cat_ref[...] = cat
        z = ALPHA * x + _dot(cat, wout_ref[...])
        xh, rstd = _ln_fwd(z)
        xh_ref[...] = xh
        rstd_ref[...] = rstd
        if ns:
            pl.when(i == nt - 1)(gather_wait)

    tok = lambda w: pl.BlockSpec((t, w), lambda i: (i, 0))
    return pl.pallas_call(
        body, name="mixer_fwd_gather" if ns else "mixer_fwd", grid=(nt,),
        in_specs=[tok(D_MODEL), _layer_spec((8, D_MODEL), l), _layer_spec((IN_W, D_MODEL), 0, resident=True),
                  _layer_spec((CONV_HALO, CONV_W), l), _layer_spec((8, CONV_W), l),
                  _layer_spec((3, 2 * CHUNK, CHUNK), l), _layer_spec((CHUNK, GMLP_W), l),
                  _const_spec((XATTN_W, XATTN_HEADS * N_MEM)), _const_spec((XATTN_HEADS * N_MEM, XATTN_W)),
                  _layer_spec((D_MODEL, D_MODEL), 0, resident=True)] + side_in,
        out_specs=[tok(D_MODEL), tok(1), tok(IN_W), tok(CONV_W), tok(D_MODEL)] + side_out,
        out_shape=[jax.ShapeDtypeStruct((s, D_MODEL), F32), jax.ShapeDtypeStruct((s, 1), F32),
                   jax.ShapeDtypeStruct((s, IN_W), F32), jax.ShapeDtypeStruct((s, CONV_W), F32),
                   jax.ShapeDtypeStruct((s, D_MODEL), BF16)] + side_shapes,
        scratch_shapes=[pltpu.VMEM((t + CONV_HALO, CONV_W), F32), pltpu.VMEM((t, GMLP_W), F32),
                        pltpu.VMEM((t + 8, CONV_W), F32)] + side_scratch,
        compiler_params=_cparams(("arbitrary",)),
    )(xin, pd, win_t, conv_w, pa, wst, bst, kt_all, v_all, w_out, *side)


VD_LN_G, VD_LN_B, VD_LOSS = 0, 1, 2
VA_CONV_B, VA_LNA_G, VA_LNA_B, VA_LNV_G, VA_LNV_B = 0, 1, 2, 3, 4


def _mixer_bwd_d(gz, xh1, rstd1, h, ac, pd, conv_w, pa, wst, wstt, bst, kt_all, k_all, vt_all, w_out, win_t, l, t, side_ops=None):
    s = gz.shape[0]
    nt = s // t

    def body(gz_ref, xh_ref, rstd_ref, h_ref, ac_ref, pd_ref, cw_ref, pa_ref, wst_ref, wstt_ref, bst_ref,
             kt_ref, k_ref, vt_ref, wout_ref, wint_ref,
             dx_ref, dh_ref, dmix_ref, vd_ref, va_ref, dcw_ref, dws_ref, dbs_ref, dkt_ref, dv_ref,
             ebuf, mixbuf, dvnbuf, dbsacc, erbuf):
        i = pl.program_id(0)

        @pl.when(i == 0)
        def _():
            vd_ref[...] = jnp.zeros_like(vd_ref)
            va_ref[...] = jnp.zeros_like(va_ref)
            dcw_ref[...] = jnp.zeros_like(dcw_ref)
            dws_ref[...] = jnp.zeros_like(dws_ref)
            dbs_ref[...] = jnp.zeros_like(dbs_ref)
            dkt_ref[...] = jnp.zeros_like(dkt_ref)
            dv_ref[...] = jnp.zeros_like(dv_ref)
            dbsacc[...] = jnp.zeros_like(dbsacc)
            ebuf[t:t + CONV_HALO, :] = jnp.zeros((CONV_HALO, CONV_W), F32)

        gz_v = gz_ref[...]
        xh = xh_ref[...]
        vd_ref[VD_LN_G:VD_LN_G + 1, :] += _colsum(gz_v * xh)
        vd_ref[VD_LN_B:VD_LN_B + 1, :] += _colsum(gz_v)
        dz = _ln_bwd(gz_v * _row(pd_ref, PD_G1), xh, rstd_ref[...])
        dzb = dz.astype(BF16)
        dmix_ref[...] = dzb
        dcat = _dot_nt(dzb, wout_ref[...])
        d_a, d_g, d_o = dcat[:, 0:CONV_W], dcat[:, CONV_W:CONV_W + GMLP_W], dcat[:, CONV_W + GMLP_W:]

        h = h_ref[...]
        a1, a2 = h[:, 0:CONV_W], h[:, CONV_W:2 * CONV_W]
        hu, hv = h[:, 2 * CONV_W:2 * CONV_W + GMLP_W], h[:, 2 * CONV_W + GMLP_W:2 * CONV_W + 2 * GMLP_W]
        q = h[:, IN_W - XATTN_W:]

        xh_a, rstd_a = _ln_fwd(ac_ref[...])
        an = xh_a * _row(pa_ref, PA_LNA_G) + _row(pa_ref, PA_LNA_B)
        sig = _sigmoid(an)
        d_an = d_a * (sig * (1.0 + an * (1.0 - sig)))
        va_ref[VA_LNA_G:VA_LNA_G + 1, :] += _colsum(d_an * xh_a)
        va_ref[VA_LNA_B:VA_LNA_B + 1, :] += _colsum(d_an)
        dac = _ln_bwd(d_an * _row(pa_ref, PA_LNA_G), xh_a, rstd_a)
        va_ref[VA_CONV_B:VA_CONV_B + 1, :] += _colsum(dac)
        ebuf[0:t, :] = dac
        sg = _sigmoid(a2)
        glu = a1 * sg
        dglu = jnp.zeros((t, CONV_W), F32)
        for r in range(8):
            if r > 0:
                erbuf[...] = ebuf[r:r + t + 24, :]
            src = ebuf if r == 0 else erbuf
            for a in range(4):
                o = 8 * a + r
                if o < CONV_K:
                    k = CONV_K - 1 - o
                    ek = src[8 * a:8 * a + t, :]
                    dglu = dglu + ek * cw_ref[k:k + 1, :]
                    dcw_ref[k:k + 1, :] += _colsum(ek * glu)
        ebuf[t:t + CONV_HALO, :] = ebuf[0:CONV_HALO, :]
        da1 = dglu * sg
        da2 = dglu * a1 * sg * (1.0 - sg)

        u = _gelu(hu)
        xh_v, rstd_v = _ln_fwd(_gelu(hv))
        vn = xh_v * _row(pa_ref, PA_LNV_G) + _row(pa_ref, PA_LNV_B)
        vnb = vn.astype(BF16)
        _spatial_mix(vnb, wst_ref, bst_ref, mixbuf, t)
        dhu = d_g * mixbuf[...] * _gelu_grad(hu)
        dm = d_g * u
        dmb = dm.astype(BF16)
        lo = _lane_lo((CHUNK, LANE))
        for n in range(t // CHUNK):
            rows = slice(n * CHUNK, (n + 1) * CHUNK)
            dbsacc[...] += dm[rows, :]
            for j in range(GMLP_W // LANE):
                cols = slice(j * LANE, (j + 1) * LANE)
                dm_blk = dmb[rows, cols]
                r = _dot(wstt_ref[j], dm_blk)
                dvnbuf[rows, cols] = jnp.where(lo, r[:CHUNK], r[CHUNK:])
                zero = jnp.zeros_like(dm_blk)
                st = jnp.concatenate([jnp.where(lo, dm_blk, zero), jnp.where(lo, zero, dm_blk)], axis=0)
                dws_ref[j] += _dot_nt(st, vnb[rows, cols])
        dvn = dvnbuf[...]
        va_ref[VA_LNV_G:VA_LNV_G + 1, :] += _colsum(dvn * xh_v)
        va_ref[VA_LNV_B:VA_LNV_B + 1, :] += _colsum(dvn)
        dhv = _ln_bwd(dvn * _row(pa_ref, PA_LNV_G), xh_v, rstd_v) * _gelu_grad(hv)

        qb = q.astype(BF16)
        p = _softmax_heads(_dot(qb, kt_ref[...]) * ATT_SCALE)
        dob = d_o.astype(BF16)
        dp = _dot(dob, vt_ref[...])
        dss = []
        for hd in range(XATTN_HEADS):
            cs = slice(hd * N_MEM, (hd + 1) * N_MEM)
            ph, dph = p[:, cs], dp[:, cs]
            dss.append(ph * (dph - jnp.sum(ph * dph, axis=-1, keepdims=True)) * ATT_SCALE)
        dsb = jnp.concatenate(dss, axis=1).astype(BF16)
        dq = _dot(dsb, k_ref[...])
        dkt_ref[...] += _dot_tn(qb, dsb)
        dv_ref[...] += _dot_tn(p.astype(BF16), dob)

        dhb = jnp.concatenate([da1, da2, dhu, dhv, dq], axis=1).astype(BF16)
        dh_ref[...] = dhb
        dx_ref[...] = ALPHA * dz + _dot(dhb, wint_ref[...])

        @pl.when(i == nt - 1)
        def _():
            acc = dbsacc[...]
            head = lax.broadcasted_iota(jnp.int32, (CHUNK, GMLP_W), 1) // HEAD_DIM
            lane = lax.broadcasted_iota(jnp.int32, (CHUNK, LANE), 1)
            out = jnp.zeros((CHUNK, LANE), F32)
            for hd in range(GMLP_W // HEAD_DIM):
                sh = jnp.sum(jnp.where(head == hd, acc, 0.0), axis=1, keepdims=True)
                out = out + jnp.where(lane == hd, sh, 0.0)
            dbs_ref[...] = out

    rev = lambda w: pl.BlockSpec((t, w), lambda i: (nt - 1 - i, 0))
    out_shape = [
        jax.ShapeDtypeStruct((s, D_MODEL), F32), jax.ShapeDtypeStruct((s, IN_W), BF16),
        jax.ShapeDtypeStruct((s, D_MODEL), BF16),
        jax.ShapeDtypeStruct((8, D_MODEL), F32), jax.ShapeDtypeStruct((8, CONV_W), F32),
        jax.ShapeDtypeStruct((CONV_HALO, CONV_W), F32), jax.ShapeDtypeStruct((3, 2 * CHUNK, CHUNK), F32),
        jax.ShapeDtypeStruct((CHUNK, LANE), F32),
        jax.ShapeDtypeStruct((XATTN_W, XATTN_HEADS * N_MEM), F32), jax.ShapeDtypeStruct((XATTN_HEADS * N_MEM, XATTN_W), F32),
    ]
    out_specs = [rev(D_MODEL), rev(IN_W), rev(D_MODEL)] + [_const_spec(o.shape) for o in out_shape[3:]]
    return _call_with_side(
        body, side_ops, name="mixer_bwd_d", grid=(nt,),
        in_specs=[rev(D_MODEL), rev(D_MODEL), rev(1), rev(IN_W), rev(CONV_W),
                  _layer_spec((8, D_MODEL), l), _layer_spec((CONV_HALO, CONV_W), l), _layer_spec((8, CONV_W), l),
                  _layer_spec((3, 2 * CHUNK, CHUNK), l), _layer_spec((3, 2 * CHUNK, CHUNK), l),
                  _layer_spec((CHUNK, GMLP_W), l),
                  _const_spec((XATTN_W, XATTN_HEADS * N_MEM)), _const_spec((XATTN_HEADS * N_MEM, XATTN_W)),
                  _const_spec((XATTN_W, XATTN_HEADS * N_MEM)),
                  _layer_spec((D_MODEL, D_MODEL), 0, resident=True), _layer_spec((IN_W, D_MODEL), 0, resident=True)],
        out_specs=out_specs, out_shape=out_shape,
        scratch_shapes=[pltpu.VMEM((t + CONV_HALO, CONV_W), F32), pltpu.VMEM((t, GMLP_W), F32),
                        pltpu.VMEM((t, GMLP_W), F32), pltpu.VMEM((CHUNK, GMLP_W), F32),
                        pltpu.VMEM((t + 24, CONV_W), F32)],
        operands=(gz, xh1, rstd1, h, ac, pd, conv_w, pa, wst, wstt, bst, kt_all, k_all, vt_all, w_out, win_t),
        semantics=("arbitrary",))


def _mixer_bwd_w(xin, pd, dh, cat, dmix, l, t, side_ops=None):
    s = xin.shape[0]
    nt = s // t

    def body(x_ref, pd_ref, dh_ref, cat_ref, dmix_ref, dwin_ref, dwout_ref):
        @pl.when(pl.program_id(0) == 0)
        def _():
            dwin_ref[...] = jnp.zeros_like(dwin_ref)
            dwout_ref[...] = jnp.zeros_like(dwout_ref)

        xb = (x_ref[...] * _row(pd_ref, PD_GIN) + _row(pd_ref, PD_BIN)).astype(BF16)
        dwin_ref[...] += _dot_tn(dh_ref[...], xb)
        dwout_ref[...] += _dot_tn(cat_ref[...], dmix_ref[...])

    tok = lambda w: pl.BlockSpec((t, w), lambda i: (i, 0))
    return _call_with_side(
        body, side_ops, name="mixer_bwd_w", grid=(nt,),
        in_specs=[tok(D_MODEL), _layer_spec((8, D_MODEL), l), tok(IN_W), tok(D_MODEL), tok(D_MODEL)],
        out_specs=[_layer_spec((IN_W, D_MODEL), 0, resident=True), _layer_spec((D_MODEL, D_MODEL), 0, resident=True)],
        out_shape=[jax.ShapeDtypeStruct((1, IN_W, D_MODEL), F32), jax.ShapeDtypeStruct((1, D_MODEL, D_MODEL), F32)],
        scratch_shapes=[], operands=(xin, pd, dh, cat, dmix), semantics=("arbitrary",))


PF_W0, PF_B = 0, 3


def _ffn_fwd(xh1, pd, wup_t, pf, w_d, l, t, side=None):
    s = xh1.shape[0]
    nt = s // t
    side, side_in, side_out, side_shapes, side_scratch = _side_specs(side)
    ns = len(side)

    def body(xh_ref, pd_ref, wg_ref, wv_ref, pf_ref, wd_ref, *rest):
        shard_refs, rest = rest[:ns], rest[ns:]
        xh2_ref, rstd_ref, upg_ref, upv_ref, hm_ref, xb_ref = rest[:6]
        land_refs, rest = rest[6:6 + ns], rest[6 + ns:]
        fbuf = rest[0]
        i = pl.program_id(0)
        if ns:
            gather_start, gather_wait = _side_gather(shard_refs, land_refs, *rest[1:])
            pl.when(i == 0)(gather_start)

        @pl.when(i == 0)
        def _():
            fbuf[0:FFN_HALO, :] = jnp.zeros((FFN_HALO, FF_P), F32)

        x1 = xh_ref[...] * _row(pd_ref, PD_G1) + _row(pd_ref, PD_B1)
        xb = x1.astype(BF16)
        xb_ref[...] = xb
        y = jnp.zeros((t, D_MODEL), F32)
        for hf in range(2):
            cs = slice(hf * FF_H, (hf + 1) * FF_H)
            ug = _dot_nt(xb, wg_ref[cs, :])
            uv = _dot_nt(xb, wv_ref[cs, :])
            upg_ref[:, cs] = ug
            upv_ref[:, cs] = uv
            fbuf[FFN_HALO:FFN_HALO + t, cs] = ug
            gate = jnp.zeros((t, FF_H), F32) + pf_ref[PF_B:PF_B + 1, cs]
            for k in range(FFN_CONV_K):
                off = FFN_HALO - (FFN_CONV_K - 1) + k
                gate = gate + fbuf[off:off + t, cs] * pf_ref[PF_W0 + k:PF_W0 + k + 1, cs]
            fbuf[0:FFN_HALO, cs] = fbuf[t:t + FFN_HALO, cs]
            hmb = (gate * _sigmoid(gate) * uv).astype(BF16)
            hm_ref[:, cs] = hmb
            y = y + _dot(hmb, wd_ref[cs, :])
        xh2, rstd = _ln_fwd(ALPHA * x1 + y)
        xh2_ref[...] = xh2
        rstd_ref[...] = rstd
        if ns:
            pl.when(i == nt - 1)(gather_wait)

    tok = lambda w: pl.BlockSpec((t, w), lambda i: (i, 0))
    return pl.pallas_call(
        body, name="ffn_fwd_gather" if ns else "ffn_fwd", grid=(nt,),
        in_specs=[tok(D_MODEL), _layer_spec((8, D_MODEL), l),
                  _layer_spec((FF_P, D_MODEL), 0, 0, resident=True), _layer_spec((FF_P, D_MODEL), 0, 1, resident=True),
                  _layer_spec((8, FF_P), l), _layer_spec((FF_P, D_MODEL), 0, resident=True)] + side_in,
        out_specs=[tok(D_MODEL), tok(1), tok(FF_P), tok(FF_P), tok(FF_P), tok(D_MODEL)] + side_out,
        out_shape=[jax.ShapeDtypeStruct((s, D_MODEL), F32), jax.ShapeDtypeStruct((s, 1), F32),
                   jax.ShapeDtypeStruct((s, FF_P), F32), jax.ShapeDtypeStruct((s, FF_P), F32),
                   jax.ShapeDtypeStruct((s, FF_P), BF16), jax.ShapeDtypeStruct((s, D_MODEL), BF16)] + side_shapes,
        scratch_shapes=[pltpu.VMEM((t + FFN_HALO, FF_P), F32)] + side_scratch,
        compiler_params=_cparams(("arbitrary",)),
    )(xh1, pd, wup_t, wup_t, pf, w_d, *side)


VF_W0, VF_B = 0, 3


def _ffn_bwd_d(gz_or_target, xh2, rstd2, upg, upv, pd, pf, w_d, wup_t, l, t, last, side_ops=None):
    s = xh2.shape[0]
    nt = s // t
    hb = t // FFN_HALO

    def body(gz_ref, xh2_ref, rstd_ref, upg_ref, halo_ref, upv_ref, pd_ref, pf_ref, wd_ref, wg_ref, wv_ref,
             dx_ref, dy_ref, dug_ref, duv_ref, vd_ref, vf_ref, gbuf, ebuf, s1buf, s2buf):
        i = pl.program_id(0)
        first_tile = i == nt - 1

        @pl.when(i == 0)
        def _():
            vd_ref[...] = jnp.zeros_like(vd_ref)
            vf_ref[...] = jnp.zeros_like(vf_ref)
            ebuf[t:t + FFN_HALO, :] = jnp.zeros((FFN_HALO, FF_P), F32)

        xh2_v = xh2_ref[...]
        if last:
            diff = xh2_v * _row(pd_ref, PD_G2) + _row(pd_ref, PD_B2) - gz_ref[...]
            vd_ref[VD_LOSS:VD_LOSS + 1, :] += _colsum(diff * diff)
            gz_v = diff * (1.0 / D_MODEL)
        else:
            gz_v = gz_ref[...]
        vd_ref[VD_LN_G:VD_LN_G + 1, :] += _colsum(gz_v * xh2_v)
        vd_ref[VD_LN_B:VD_LN_B + 1, :] += _colsum(gz_v)
        dz = _ln_bwd(gz_v * _row(pd_ref, PD_G2), xh2_v, rstd_ref[...])
        dyb = dz.astype(BF16)
        dy_ref[...] = dyb
        dx = ALPHA * dz
        for hf in range(2):
            cs = slice(hf * FF_H, (hf + 1) * FF_H)
            ug = upg_ref[:, cs]
            uv = upv_ref[:, cs]
            halo = halo_ref[:, cs]
            gbuf[0:FFN_HALO, :] = jnp.where(first_tile, jnp.zeros_like(halo), halo)
            gbuf[FFN_HALO:FFN_HALO + t, :] = ug
            s1buf[...] = gbuf[FFN_HALO - 1:FFN_HALO - 1 + t, :]
            s2buf[...] = gbuf[FFN_HALO - 2:FFN_HALO - 2 + t, :]
            ug1 = s1buf[...]
            ug2 = s2buf[...]
            gate = (pf_ref[PF_B:PF_B + 1, cs] + ug2 * pf_ref[PF_W0:PF_W0 + 1, cs] + ug1 * pf_ref[PF_W0 + 1:PF_W0 + 2, cs]
                    + ug * pf_ref[PF_W0 + 2:PF_W0 + 3, cs])
            sig = _sigmoid(gate)
            sl = gate * sig
            dhm = _dot_nt(dyb, wd_ref[cs, :])
            duv = dhm * sl
            dgate = dhm * uv * (sig * (1.0 + gate * (1.0 - sig)))
            vf_ref[VF_B:VF_B + 1, cs] += _colsum(dgate)
            vf_ref[VF_W0:VF_W0 + 1, cs] += _colsum(dgate * ug2)
            vf_ref[VF_W0 + 1:VF_W0 + 2, cs] += _colsum(dgate * ug1)
            vf_ref[VF_W0 + 2:VF_W0 + 3, cs] += _colsum(dgate * ug)
            ebuf[0:t, cs] = dgate
            dug = (ebuf[2:2 + t, cs] * pf_ref[PF_W0:PF_W0 + 1, cs] + ebuf[1:1 + t, cs] * pf_ref[PF_W0 + 1:PF_W0 + 2, cs]
                   + dgate * pf_ref[PF_W0 + 2:PF_W0 + 3, cs])
            ebuf[t:t + FFN_HALO, cs] = ebuf[0:FFN_HALO, cs]
            dugb = dug.astype(BF16)
            duvb = duv.astype(BF16)
            dug_ref[:, cs] = dugb
            duv_ref[:, cs] = duvb
            dx = dx + _dot(dugb, wg_ref[cs, :]) + _dot(duvb, wv_ref[cs, :])
        dx_ref[...] = dx

        if last:
            @pl.when(i == nt - 1)
            def _():
                tot = jnp.sum(vd_ref[VD_LOSS:VD_LOSS + 1, :], axis=1, keepdims=True)
                vd_ref[VD_LOSS:VD_LOSS + 1, :] = jnp.zeros((1, D_MODEL), F32) + tot

    rev = lambda w: pl.BlockSpec((t, w), lambda i: (nt - 1 - i, 0))
    halo_spec = pl.BlockSpec((FFN_HALO, FF_P), lambda i: (jnp.maximum((nt - 1 - i) * hb - 1, 0), 0))
    out_shape = [jax.ShapeDtypeStruct((s, D_MODEL), F32), jax.ShapeDtypeStruct((s, D_MODEL), BF16),
                 jax.ShapeDtypeStruct((s, FF_P), BF16), jax.ShapeDtypeStruct((s, FF_P), BF16),
                 jax.ShapeDtypeStruct((8, D_MODEL), F32), jax.ShapeDtypeStruct((8, FF_P), F32)]
    return _call_with_side(
        body, side_ops, name="ffn_bwd_d_last" if last else "ffn_bwd_d", grid=(nt,),
        in_specs=[rev(D_MODEL), rev(D_MODEL), rev(1), rev(FF_P), halo_spec, rev(FF_P),
                  _layer_spec((8, D_MODEL), l), _layer_spec((8, FF_P), l),
                  _layer_spec((FF_P, D_MODEL), 0, resident=True),
                  _layer_spec((FF_P, D_MODEL), 0, 0, resident=True), _layer_spec((FF_P, D_MODEL), 0, 1, resident=True)],
        out_specs=[rev(D_MODEL), rev(D_MODEL), rev(FF_P), rev(FF_P),
                   _const_spec((8, D_MODEL)), _const_spec((8, FF_P))],
        out_shape=out_shape,
        scratch_shapes=[pltpu.VMEM((t + FFN_HALO, FF_H), F32), pltpu.VMEM((t + FFN_HALO, FF_P), F32),
                        pltpu.VMEM((t, FF_H), F32), pltpu.VMEM((t, FF_H), F32)],
        operands=(gz_or_target, xh2, rstd2, upg, upg, upv, pd, pf, w_d, wup_t, wup_t), semantics=("arbitrary",))


def _ffn_bwd_w(xb, dy, dug, duv, hm, t, side_ops=None):
    s = xb.shape[0]
    nt = s // t

    def body(xb_ref, dy_ref, dug_ref, duv_ref, hm_ref, dwup_ref, dwd_ref):
        @pl.when(pl.program_id(1) == 0)
        def _():
            dwup_ref[...] = jnp.zeros_like(dwup_ref)
            dwd_ref[...] = jnp.zeros_like(dwd_ref)

        dwup_ref[0] += _dot_tn(dug_ref[...], xb_ref[...])
        dwup_ref[1] += _dot_tn(duv_ref[...], xb_ref[...])
        dwd_ref[...] += _dot_tn(hm_ref[...], dy_ref[...])

    tok = lambda w: pl.BlockSpec((t, w), lambda c, i: (i, 0))
    half = pl.BlockSpec((t, FF_H), lambda c, i: (i, c))
    return _call_with_side(
        body, side_ops, name="ffn_bwd_w", grid=(2, nt),
        in_specs=[tok(D_MODEL), tok(D_MODEL), half, half, half],
        out_specs=[pl.BlockSpec((None, 2, FF_H, D_MODEL), lambda c, i: (0, 0, c, 0), pipeline_mode=pl.Buffered(1)),
                   pl.BlockSpec((None, FF_H, D_MODEL), lambda c, i: (0, c, 0), pipeline_mode=pl.Buffered(1))],
        out_shape=[jax.ShapeDtypeStruct((1, 2, FF_P, D_MODEL), F32), jax.ShapeDtypeStruct((1, FF_P, D_MODEL), F32)],
        scratch_shapes=[], operands=(xb, dy, dug, duv, hm), semantics=("arbitrary", "arbitrary"))


def _adamw_math(w, g, m, v):
    nm = ADAM_B1 * m + (1.0 - ADAM_B1) * g
    nv = ADAM_B2 * v + (1.0 - ADAM_B2) * (g * g)
    m_hat = nm / (1.0 - ADAM_B1 ** ADAM_STEP)
    v_hat = nv / (1.0 - ADAM_B2 ** ADAM_STEP)
    return -ADAM_LR * (m_hat / (jnp.sqrt(v_hat) + ADAM_EPS) + ADAM_WD * w), nm, nv


def _adamw_layers(w, gs, m, v, name):
    shp = w.shape
    _, rows, cols = shp
    tr = _row_tile(rows, cols * 4, mult=8)
    nb = rows // tr

    def body(w_ref, g0_ref, g1_ref, m_ref, v_ref, g_ref, d_ref, nm_ref, nv_ref):
        g = jnp.where(pl.program_id(0) == 0, g0_ref[...], g1_ref[...])
        g_ref[...] = g
        d_ref[...], nm_ref[...], nv_ref[...] = _adamw_math(w_ref[...], g, m_ref[...], v_ref[...])

    stacked = pl.BlockSpec((tr, cols), lambda l, i: (l * nb + i, 0))
    single = pl.BlockSpec((tr, cols), lambda l, i: (i, 0))
    sh = jax.ShapeDtypeStruct((DEPTH * rows, cols), F32)
    flat = lambda a: a.reshape(DEPTH * rows, cols)
    outs = pl.pallas_call(body, name=name, grid=(DEPTH, nb), in_specs=[stacked, single, single, stacked, stacked],
                          out_specs=[stacked] * 4, out_shape=[sh] * 4,
                          compiler_params=_cparams(("arbitrary", "arbitrary")))(flat(w), gs[0], gs[1], flat(m), flat(v))
    return [o.reshape(shp) for o in outs]


def _adamw_small(ws, gs, ms, vs):
    n = len(ws)

    def body(*refs):
        w_refs, g_refs, m_refs, v_refs = refs[:n], refs[n:2 * n], refs[2 * n:3 * n], refs[3 * n:4 * n]
        d_refs, nm_refs, nv_refs = refs[4 * n:5 * n], refs[5 * n:6 * n], refs[6 * n:7 * n]
        for k in range(n):
            d_refs[k][...], nm_refs[k][...], nv_refs[k][...] = _adamw_math(w_refs[k][...], g_refs[k][...], m_refs[k][...],
                                                                             v_refs[k][...])

    shapes = [jax.ShapeDtypeStruct(w.shape, F32) for w in ws]
    outs = pl.pallas_call(body, name="adamw_small", out_shape=shapes * 3, compiler_params=_cparams())(*ws, *gs, *ms, *vs)
    return outs[:n], outs[n:2 * n], outs[2 * n:]


def _all_gather_chips(tensors, name):
    n = len(tensors)
    halves = [a.shape[1] // 2 for a in tensors]

    def body(*refs):
        x_refs, out_refs = refs[:n], refs[n:2 * n]
        send_sems, recv_sems, local_sems = refs[2 * n:]
        x, y, c, chips = _my_place()
        me, sibling, mej = (x, y, c), (x, y, 1 - c), 2 * x + y

        def rows(tn, px, py, pc):
            return out_refs[tn].at[:, 2 * px + py, pl.ds(pc * halves[tn], halves[tn]), :]

        def copy(tn, k, block, to, src=None):
            return pltpu.make_async_remote_copy(
                src_ref=rows(tn, *block) if src is None else src, dst_ref=rows(tn, *block),
                send_sem=send_sems.at[tn, k], recv_sem=recv_sems.at[tn, k], device_id=to, device_id_type=MESH)

        mine_src = [x_refs[tn].at[:, pl.ds(c * halves[tn], halves[tn]), :] for tn in range(n)]
        mine = [pltpu.make_async_copy(mine_src[tn], rows(tn, *me), local_sems.at[tn]) for tn in range(n)]
        first = []
        for j, chip in enumerate(chips):
            first += [copy(tn, 1 + j, me, (*chip, c), src=mine_src[tn]) for tn in range(n)]
        first += [copy(tn, 0, me, sibling, src=mine_src[tn]) for tn in range(n)]
        for cp in first + mine:
            cp.start()
        passed = []
        for j, chip in enumerate(chips):
            for tn in range(n):
                copy(tn, 1 + j, (*chip, c), me).wait_recv()
                fwd = copy(tn, 4 + j, (*chip, c), sibling)
                fwd.start()
                passed.append(fwd)
        for tn in range(n):
            copy(tn, 0, sibling, me).wait_recv()
            for j, chip in enumerate(chips):
                copy(tn, 4 + j, (*chip, 1 - c), me).wait_recv()
        for cp in first + passed:
            cp.wait_send()
        for cp in mine:
            cp.wait()

    return pl.pallas_call(
        body, name=name,
        out_shape=[jax.ShapeDtypeStruct((a.shape[0], N_CHIPS) + a.shape[1:], a.dtype) for a in tensors],
        in_specs=[ANY] * n, out_specs=[ANY] * n,
        scratch_shapes=[pltpu.SemaphoreType.DMA((n, 7)), pltpu.SemaphoreType.DMA((n, 7)), pltpu.SemaphoreType.DMA((n,))],
    )(*tensors)


def _swap_op(g5s):
    n = len(g5s)

    def make(ins, outs, scr):
        send_sems, recv_sems = scr
        x, y, c, _ = _my_place()

        def copies():
            return [pltpu.make_async_remote_copy(
                src_ref=ins[tn].at[:, :, 1 - c], dst_ref=outs[tn], send_sem=send_sems.at[tn], recv_sem=recv_sems.at[tn],
                device_id=(x, y, 1 - c), device_id_type=MESH) for tn in range(n)]

        def start():
            for cp in copies():
                cp.start()

        def wait():
            for cp in copies():
                cp.wait()

        return start, wait

    return _SideOp(g5s, [jax.ShapeDtypeStruct(g.shape[:2] + g.shape[3:], g.dtype) for g in g5s],
                   [pltpu.SemaphoreType.DMA((n,)), pltpu.SemaphoreType.DMA((n,))], make)


def _scatter_op(parts):
    n = len(parts)

    def make(ins, outs, scr):
        send_sems, recv_sems = scr
        x, y, c, chips = _my_place()

        def copies():
            return [pltpu.make_async_remote_copy(
                src_ref=ins[tn].at[:, 2 * px + py], dst_ref=outs[tn].at[:, k],
                send_sem=send_sems.at[tn, k], recv_sem=recv_sems.at[tn, k],
                device_id=(px, py, c), device_id_type=MESH) for k, (px, py) in enumerate(chips) for tn in range(n)]

        def start():
            for cp in copies():
                cp.start()

        def wait():
            for cp in copies():
                cp.wait()

        return start, wait

    return _SideOp(parts, [jax.ShapeDtypeStruct((p.shape[0], 3) + p.shape[2:], p.dtype) for p in parts],
                   [pltpu.SemaphoreType.DMA((n, 3)), pltpu.SemaphoreType.DMA((n, 3))], make)


def _sgather_op(fs):
    n = len(fs)

    def make(ins, outs, scr):
        send_sems, recv_sems = scr
        x, y, c, _ = _my_place()

        def copy(tn, dst_half):
            return pltpu.make_async_remote_copy(
                src_ref=outs[tn].at[:, c], dst_ref=outs[tn].at[:, dst_half], send_sem=send_sems.at[tn],
                recv_sem=recv_sems.at[tn], device_id=(x, y, 1 - c), device_id_type=MESH)

        def start():
            for tn in range(n):
                copy(tn, c).start()

        def wait():
            for tn in range(n):
                copy(tn, 1 - c).wait_recv()
                copy(tn, c).wait_send()

        return start, wait

    return _SideOp(fs, [jax.ShapeDtypeStruct(f.shape, f.dtype) for f in fs],
                   [pltpu.SemaphoreType.DMA((n,)), pltpu.SemaphoreType.DMA((n,))], make, aliases={tn: tn for tn in range(n)})


def _run_side_ops(ops, name):
    return _call_with_side(lambda: None, ops, name=name, grid=(1,), in_specs=[], out_specs=[], out_shape=[],
                           scratch_shapes=[], operands=(), semantics=("arbitrary",))[1]


def _gather_devices_op(xs):
    def make(ins, outs, scr):
        send_sems, recv_sems, local_sem = scr
        x, y, c, chips = _my_place()
        peers = [(x, y, 1 - c)] + [(px, py, pc) for (px, py) in chips for pc in (c, 1 - c)]
        me = 4 * x + 2 * y + c

        def copy(k, slot):
            return pltpu.make_async_remote_copy(
                src_ref=ins[0], dst_ref=outs[0].at[slot], send_sem=send_sems.at[k], recv_sem=recv_sems.at[k],
                device_id=peers[k], device_id_type=MESH)

        def local():
            return pltpu.make_async_copy(ins[0], outs[0].at[me], local_sem)

        def start():
            for k in range(7):
                copy(k, me).start()
            local().start()

        def wait():
            for k, (px, py, pc) in enumerate(peers):
                copy(k, 4 * px + 2 * py + pc).wait_recv()
                copy(k, me).wait_send()
            local().wait()

        return start, wait

    return _SideOp([xs], [jax.ShapeDtypeStruct((8,) + xs.shape, xs.dtype)],
                   [pltpu.SemaphoreType.DMA((7,)), pltpu.SemaphoreType.DMA((7,)), pltpu.SemaphoreType.DMA], make)


def _add_halves(gs, recvs, place):
    n = len(gs)

    def body(place_ref, *refs):
        g_refs, r_refs, o_refs = refs[:n], refs[n:2 * n], refs[2 * n:]
        for tn in range(n):
            o_refs[tn][...] = (g_refs[tn][...] + r_refs[tn][...]).astype(BF16)

    def gspec(g):
        return pl.BlockSpec((None, None, None) + g.shape[3:], lambda l, j, p: (l, j, p[1], 0, 0))

    def rspec(r):
        return pl.BlockSpec((None, None) + r.shape[2:], lambda l, j, p: (l, j, 0, 0))

    grid_spec = pltpu.PrefetchScalarGridSpec(
        num_scalar_prefetch=1, grid=(gs[0].shape[0], N_CHIPS),
        in_specs=[gspec(g) for g in gs] + [rspec(r) for r in recvs], out_specs=[rspec(r) for r in recvs])
    return pl.pallas_call(body, name="rs_add", grid_spec=grid_spec,
                          out_shape=[jax.ShapeDtypeStruct(r.shape, BF16) for r in recvs],
                          compiler_params=_cparams(("arbitrary", "arbitrary")))(place, *gs, *recvs)


def _sum_slots(parts, slots, place):
    n = len(parts)

    def body(place_ref, *refs):
        p_refs, s_refs, o_refs = refs[:n], refs[n:2 * n], refs[2 * n:]
        for tn in range(n):
            acc = p_refs[tn][...].astype(F32)
            for k in range(3):
                acc = acc + s_refs[tn][k].astype(F32)
            o_refs[tn][...] = acc

    def pspec(p):
        return pl.BlockSpec((None, None) + p.shape[2:], lambda l, pl_: (l, pl_[0], 0, 0))

    def sspec(sl):
        return pl.BlockSpec((None,) + sl.shape[1:], lambda l, pl_: (l, 0, 0, 0))

    def ospec(p):
        return pl.BlockSpec((None, None) + p.shape[2:], lambda l, pl_: (l, pl_[1], 0, 0))

    grid_spec = pltpu.PrefetchScalarGridSpec(
        num_scalar_prefetch=1, grid=(parts[0].shape[0],),
        in_specs=[pspec(p) for p in parts] + [sspec(sl) for sl in slots], out_specs=[ospec(p) for p in parts])
    return pl.pallas_call(body, name="rs_sum", grid_spec=grid_spec,
                          out_shape=[jax.ShapeDtypeStruct((p.shape[0], 2) + p.shape[2:], F32) for p in parts],
                          compiler_params=_cparams(("arbitrary",)))(place, *parts, *slots)


def _sum_devices(gathered, m_per):
    def body(g_ref, o_ref):
        acc = g_ref[0:m_per, :]
        for d in range(1, 8):
            acc = acc + g_ref[d * m_per:(d + 1) * m_per, :]
        o_ref[...] = acc

    return pl.pallas_call(body, name="small_sum", out_shape=jax.ShapeDtypeStruct((m_per, LANE), F32),
                          compiler_params=_cparams())(gathered)


def _pad_ff_cols(a):
    lead = a.shape[:-1]
    n = a.shape[-1] // FF_Q
    a = a.reshape(*lead, n, FF_Q)
    a = jnp.pad(a, [(0, 0)] * len(lead) + [(0, 0), (0, FF_QP - FF_Q)])
    return a.reshape(*lead, n * FF_QP)


def _unpad_ff_cols(a):
    lead = a.shape[:-1]
    n = a.shape[-1] // FF_QP
    return a.reshape(*lead, n, FF_QP)[..., :FF_Q].reshape(*lead, n * FF_Q)


def _pack_small(parts):
    flat = jnp.concatenate([p.reshape(-1) for p in parts])
    return flat.reshape(-1, LANE)


SMALL_SHAPES = [("conv_a_w", (CONV_K, CONV_W)), ("conv_a_b", (CONV_W,)), ("ln_a_g", (CONV_W,)), ("ln_a_b", (CONV_W,)),
                ("ln_v_g", (GMLP_W,)), ("ln_v_b", (GMLP_W,)), ("w_s", (6, CHUNK, CHUNK)), ("b_s", (6, CHUNK)),
                ("ln1_g", (D_MODEL,)), ("ln1_b", (D_MODEL,)), ("conv_f_w", (FFN_CONV_K, D_FF)), ("conv_f_b", (D_FF,)),
                ("ln2_g", (D_MODEL,)), ("ln2_b", (D_MODEL,))]


def _unpack_small(flat2d):
    flat = flat2d.reshape(DEPTH, -1)
    out, o = {}, 0
    for name, shp in SMALL_SHAPES:
        n = 1
        for d in shp:
            n *= d
        out[name] = flat[:, o:o + n].reshape((DEPTH,) + shp)
        o += n
    return out


def _rows8(rows):
    blk = jnp.stack(rows, axis=1)
    return jnp.pad(blk, ((0, 0), (0, 8 - len(rows)), (0, 0)))


def kernel(x, mem, w_in, conv_a_w, conv_a_b, ln_a_g, ln_a_b, ln_v_g, ln_v_b, w_s, b_s, w_mk, w_mv, w_out, ln1_g, ln1_b, w_up, conv_f_w, conv_f_b, w_down, ln2_g, ln2_b, loss_target, m_w_in, m_conv_a_w, m_conv_a_b, m_ln_a_g, m_ln_a_b, m_ln_v_g, m_ln_v_b, m_w_s, m_b_s, m_w_mk, m_w_mv, m_w_out, m_ln1_g, m_ln1_b, m_w_up, m_conv_f_w, m_conv_f_b, m_w_down, m_ln2_g, m_ln2_b, v_w_in, v_conv_a_w, v_conv_a_b, v_ln_a_g, v_ln_a_b, v_ln_v_g, v_ln_v_b, v_w_s, v_b_s, v_w_mk, v_w_mv, v_w_out, v_ln1_g, v_ln1_b, v_w_up, v_conv_f_w, v_conv_f_b, v_w_down, v_ln2_g, v_ln2_b):
    seq = x.shape[1]
    t_fwd = min(512, seq)
    t_bwd = min(256, seq)
    t_wg = min(1024, seq)
    chip = 2 * lax.axis_index("x") + lax.axis_index("y")
    core = lax.axis_index("c")
    place = jnp.stack([chip, core]).astype(jnp.int32)
    x0 = x[0]
    mem0 = mem[0]
    target = loss_target[0]

    sh_in = w_in.transpose(0, 2, 1).astype(BF16)
    sh_mk, sh_mv, sh_out = w_mk.astype(BF16), w_mv.astype(BF16), w_out.astype(BF16)
    sh_up = _pad_ff_cols(w_up).transpose(0, 2, 1).astype(BF16)
    sh_dn = jnp.pad(w_down, ((0, 0), (0, FF_QP - FF_Q), (0, 0))).astype(BF16)

    def mixer_weights(g_in, g_mk, g_mv, g_out):
        return dict(win_t=g_in.reshape(1, IN_W, D_MODEL), wmk=g_mk.reshape(1, D_MODEL, XATTN_W),
                    wmv=g_mv.reshape(1, D_MODEL, XATTN_W), wout=g_out.reshape(1, D_MODEL, D_MODEL))

    def ffn_weights(g_up, g_dn):
        return dict(wup_t=g_up.reshape(1, 2, FF_P, D_MODEL), wdown=g_dn.reshape(1, FF_P, D_MODEL))

    n_ca = conv_a_w.size
    small_w = _pack_small([conv_a_w, conv_f_w, jnp.zeros((2 * 80 * LANE - n_ca - conv_f_w.size,), F32)])[None]
    *g_mixer0, small_g = _all_gather_chips([sh_in[:1], sh_mk[:1], sh_mv[:1], sh_out[:1], small_w], "ag_mixer0")
    wts = [mixer_weights(*g_mixer0), None]
    small_g = small_g.reshape(N_CHIPS, -1)
    conv_a_full = small_g[:, :n_ca].reshape(N_CHIPS, DEPTH, CONV_K, CONV_W // 4).transpose(1, 2, 0, 3).reshape(DEPTH, CONV_K, CONV_W)
    conv_f_full = small_g[:, n_ca:n_ca + conv_f_w.size].reshape(N_CHIPS, DEPTH, FFN_CONV_K, FF_Q).transpose(1, 2, 0, 3).reshape(DEPTH, FFN_CONV_K, D_FF)

    tril = jnp.tril(jnp.ones((CHUNK, CHUNK), dtype=bool))
    ws_m = jnp.where(tril, w_s, 0.0)
    wst = ws_m.reshape(DEPTH, 3, 2 * CHUNK, CHUNK).astype(BF16)
    wstt = ws_m.transpose(0, 1, 3, 2).reshape(DEPTH, 3, 2 * CHUNK, CHUNK).astype(BF16)
    bst = jnp.repeat(b_s.transpose(0, 2, 1), HEAD_DIM, axis=2)
    conv_w = jnp.pad(conv_a_full, ((0, 0), (0, CONV_HALO - CONV_K), (0, 0)))
    pa = _rows8([conv_a_b, ln_a_g, ln_a_b, ln_v_g, ln_v_b])
    gin = jnp.concatenate([jnp.ones((1, D_MODEL), F32), ln2_g[:DEPTH - 1]], axis=0)
    bin_ = jnp.concatenate([jnp.zeros((1, D_MODEL), F32), ln2_b[:DEPTH - 1]], axis=0)
    pd = _rows8([gin, bin_, ln1_g, ln1_b, ln2_g, ln2_b])
    pf = jnp.concatenate([_pad_ff_cols(conv_f_full), _pad_ff_cols(conv_f_b)[:, None, :],
                          jnp.zeros((DEPTH, 8 - FFN_CONV_K - 1, FF_P), F32)], axis=1)

    acts = []
    xin = x0
    for l in range(DEPTH):
        w = wts[l]
        kt_all, k_all, v_all, vt_all = _kv_fwd(mem0, w["wmk"], w["wmv"])
        side = [sh_up[0], sh_dn[0]] if l == 0 else None
        xh1, rstd1, h, ac, cat, *landed = _mixer_fwd(xin, pd, w["win_t"], conv_w, pa, wst, bst, kt_all, v_all, w["wout"],
                                                     l, t_fwd, side)
        if l == 0:
            w.update(ffn_weights(*landed))
        side = [sh_in[1], sh_mk[1], sh_mv[1], sh_out[1], sh_up[1], sh_dn[1]] if l == 0 else None
        xh2, rstd2, upg, upv, hm, xb1, *landed = _ffn_fwd(xh1, pd, w["wup_t"], pf, w["wdown"], l, t_bwd, side)
        if l == 0:
            wts[1] = {**mixer_weights(*landed[:4]), **ffn_weights(*landed[4:])}
        acts.append(dict(xin=xin, kt_all=kt_all, k_all=k_all, vt_all=vt_all, xh1=xh1, rstd1=rstd1, h=h, ac=ac, cat=cat,
                         xh2=xh2, rstd2=rstd2, upg=upg, upv=upv, hm=hm, xb1=xb1))
        xin = xh2

    assert DEPTH == 2

    def halves_view(gs):
        return [g.reshape(1, N_CHIPS, 2, g.shape[1] // (2 * N_CHIPS), g.shape[2]) for g in gs]

    small = [None] * DEPTH
    small_packed = [None] * DEPTH
    small_gathered = [None] * DEPTH
    red_layers = [None] * DEPTH
    gz = target
    loss_sum = None
    g5_prev = None
    for l in reversed(range(DEPTH)):
        a, w = acts[l], wts[l]
        last = l == DEPTH - 1
        (dx1, dy, dug, duv, vd2, vf), side = _ffn_bwd_d(
            gz, a["xh2"], a["rstd2"], a["upg"], a["upv"], pd, pf, w["wdown"], w["wup_t"], l, t_bwd, last,
            [_swap_op(g5_prev), _gather_devices_op(small_packed[l + 1])] if g5_prev else None)
        if last:
            loss_sum = vd2[VD_LOSS, 0]
        if g5_prev:
            small_gathered[l + 1] = side[1][0]
        parts_prev = _add_halves(g5_prev, side[0], place) if g5_prev else None
        (gw_up_t, gw_down), side = _ffn_bwd_w(a["xb1"], dy, dug, duv, a["hm"], t_wg,
                                              [_scatter_op(parts_prev)] if g5_prev else None)
        halves_prev = _sum_slots(parts_prev, side[0], place) if g5_prev else None
        g5_ffn = halves_view([gw_up_t.reshape(1, 2 * FF_P, D_MODEL), gw_down])
        ops = ([_sgather_op(halves_prev)] if g5_prev else []) + ([_swap_op(g5_ffn)] if l == 0 else [])
        (dx0, dh, dmix, vd1, va, dcw, dws, dbs, dkt, dv), side = _mixer_bwd_d(
            dx1, a["xh1"], a["rstd1"], a["h"], a["ac"], pd, conv_w, pa, wst, wstt, bst,
            a["kt_all"], a["k_all"], a["vt_all"], w["wout"], w["win_t"], l, t_fwd, ops)
        if g5_prev:
            red_layers[l + 1] = side[0]
        parts_ffn = _add_halves(g5_ffn, side[-1], place) if l == 0 else None
        dws6 = jnp.where(tril, dws.reshape(6, CHUNK, CHUNK), 0.0)
        small[l] = [dcw[:CONV_K], va[VA_CONV_B], va[VA_LNA_G], va[VA_LNA_B], va[VA_LNV_G], va[VA_LNV_B], dws6,
                    dbs[:, :6].T, vd1[VD_LN_G], vd1[VD_LN_B],
                    _unpad_ff_cols(vf[VF_W0:VF_W0 + FFN_CONV_K]), _unpad_ff_cols(vf[VF_B]),
                    vd2[VD_LN_G], vd2[VD_LN_B]]
        small_packed[l] = _pack_small(small[l])
        ops = [_scatter_op(parts_ffn), _gather_devices_op(small_packed[l])] if l == 0 else None
        (gw_in_t, gw_out), side = _mixer_bwd_w(a["xin"], pd, dh, a["cat"], dmix, l, t_wg, ops)
        gw_mk, gw_mv = _kv_bwd(mem0, dkt, dv)
        g5_mix = halves_view([gw_in_t, gw_mk, gw_mv, gw_out])
        if l == 0:
            small_gathered[l] = side[1][0]
            halves_ffn = _sum_slots(parts_ffn, side[0], place)
            red_ffn, recv_mix = _run_side_ops([_sgather_op(halves_ffn), _swap_op(g5_mix)], "rs_tail_swap")
            parts_mix = _add_halves(g5_mix, recv_mix, place)
            halves_mix = _sum_slots(parts_mix, _run_side_ops([_scatter_op(parts_mix)], "rs_tail_chips")[0], place)
            red_mix = _run_side_ops([_sgather_op(halves_mix)], "rs_tail_gather")[0]
            red_layers[0] = red_mix + red_ffn
        else:
            g5_prev = g5_mix + g5_ffn
        gz = dx0
    grad_x = gz[None]

    def shard_grads(red):
        r = [f.reshape(-1, f.shape[-1]) for f in red]
        return dict(w_in=r[0].T, w_mk=r[1], w_mv=r[2], w_out=r[3], w_up=_unpad_ff_cols(r[4].T), w_down=r[5][:FF_Q])

    big_grads = [shard_grads(red_layers[l]) for l in range(DEPTH)]

    m_small = small_gathered[0].shape[1]
    small_red = jnp.concatenate([_sum_devices(g.reshape(8 * m_small, LANE), m_small) for g in small_gathered], axis=0)
    sg = _unpack_small(small_red)
    g_conv_a_w = lax.dynamic_slice_in_dim(sg["conv_a_w"], chip * (CONV_W // 4), CONV_W // 4, axis=2)
    g_conv_f_w = lax.dynamic_slice_in_dim(sg["conv_f_w"], chip * FF_Q, FF_Q, axis=2)

    loss = 0.5 / D_MODEL * lax.psum(loss_sum, ("x", "y", "c"))

    grads = dict(conv_a_w=g_conv_a_w, conv_a_b=sg["conv_a_b"], ln_a_g=sg["ln_a_g"], ln_a_b=sg["ln_a_b"],
                 ln_v_g=sg["ln_v_g"], ln_v_b=sg["ln_v_b"], w_s=sg["w_s"], b_s=sg["b_s"], ln1_g=sg["ln1_g"], ln1_b=sg["ln1_b"],
                 conv_f_w=g_conv_f_w, conv_f_b=sg["conv_f_b"], ln2_g=sg["ln2_g"], ln2_b=sg["ln2_b"])
    weights = dict(w_in=w_in, conv_a_w=conv_a_w, conv_a_b=conv_a_b, ln_a_g=ln_a_g, ln_a_b=ln_a_b, ln_v_g=ln_v_g,
                   ln_v_b=ln_v_b, w_s=w_s, b_s=b_s, w_mk=w_mk, w_mv=w_mv, w_out=w_out, ln1_g=ln1_g, ln1_b=ln1_b,
                   w_up=w_up, conv_f_w=conv_f_w, conv_f_b=conv_f_b, w_down=w_down, ln2_g=ln2_g, ln2_b=ln2_b)
    mom_m = dict(w_in=m_w_in, conv_a_w=m_conv_a_w, conv_a_b=m_conv_a_b, ln_a_g=m_ln_a_g, ln_a_b=m_ln_a_b, ln_v_g=m_ln_v_g,
                 ln_v_b=m_ln_v_b, w_s=m_w_s, b_s=m_b_s, w_mk=m_w_mk, w_mv=m_w_mv, w_out=m_w_out, ln1_g=m_ln1_g,
                 ln1_b=m_ln1_b, w_up=m_w_up, conv_f_w=m_conv_f_w, conv_f_b=m_conv_f_b, w_down=m_w_down, ln2_g=m_ln2_g,
                 ln2_b=m_ln2_b)
    mom_v = dict(w_in=v_w_in, conv_a_w=v_conv_a_w, conv_a_b=v_conv_a_b, ln_a_g=v_ln_a_g, ln_a_b=v_ln_a_b, ln_v_g=v_ln_v_g,
                 ln_v_b=v_ln_v_b, w_s=v_w_s, b_s=v_b_s, w_mk=v_w_mk, w_mv=v_w_mv, w_out=v_w_out, ln1_g=v_ln1_g,
                 ln1_b=v_ln1_b, w_up=v_w_up, conv_f_w=v_conv_f_w, conv_f_b=v_conv_f_b, w_down=v_w_down, ln2_g=v_ln2_g,
                 ln2_b=v_ln2_b)
    names = list(weights)
    big_names = ["w_in", "w_mk", "w_mv", "w_out", "w_up", "w_down"]
    delta, new_m, new_v = {}, {}, {}
    for n in big_names:
        grads[n], delta[n], new_m[n], new_v[n] = _adamw_layers(weights[n], [big_grads[l][n] for l in range(DEPTH)],
                                                               mom_m[n], mom_v[n], "adamw_" + n)
    small_names = [n for n in names if n not in big_names]
    ds, nms, nvs = _adamw_small([weights[n] for n in small_names], [grads[n] for n in small_names],
                                [mom_m[n] for n in small_names], [mom_v[n] for n in small_names])
    for n, d, nm, nv in zip(small_names, ds, nms, nvs):
        delta[n], new_m[n], new_v[n] = d, nm, nv

    return (loss, grad_x, *[grads[n] for n in names], *[delta[n] for n in names],
            *[new_m[n] for n in names], *[new_v[n] for n in names])
```

```python
import jax
import jax.numpy as jnp
from jax import lax
from jax.experimental import pallas as pl
from jax.experimental.pallas import tpu as pltpu

F32 = jnp.float32
BF16 = jnp.bfloat16

D_MODEL = 1024
DEPTH = 2
CONV_W = 384
GMLP_W = 384
XATTN_W = 256
XATTN_HEADS = 4
HEAD_DIM = 64
IN_W = 1792
CONV_K = 31
CHUNK = 128
N_MEM = 256
D_FF = 2752
FFN_CONV_K = 3
ALPHA = (2.0 * DEPTH) ** 0.25
LN_EPS = 1e-5
ATT_SCALE = 1.0 / 8.0
ADAM_LR, ADAM_B1, ADAM_B2, ADAM_EPS, ADAM_WD, ADAM_STEP = 0.001, 0.9, 0.999, 1e-08, 0.01, 10

N_CHIPS = 4
FF_Q = D_FF // N_CHIPS
FF_QP = 704
FF_H = 2 * FF_QP
FF_P = 4 * FF_QP
LANE = 128
CONV_HALO = 32
FFN_HALO = 8
BF16_ROWS = 16
VMEM_LIMIT = 60 * 1024 * 1024

MESH = pl.DeviceIdType.MESH
ANY = pl.BlockSpec(memory_space=pl.ANY)


def _cparams(sem=None, vmem=VMEM_LIMIT):
    kw = {"vmem_limit_bytes": vmem}
    if sem is not None:
        kw["dimension_semantics"] = sem
    return pltpu.CompilerParams(**kw)


def _row_tile(rows, row_bytes, limit=2 << 20, mult=BF16_ROWS):
    if rows * row_bytes <= limit:
        return rows
    best = None
    for cand in range(mult, rows, mult):
        if rows % cand == 0 and cand * row_bytes <= limit:
            best = cand
    assert best is not None, (rows, row_bytes)
    return best


def _const_spec(shape):
    nd = len(shape)
    return pl.BlockSpec(shape, lambda *_: (0,) * nd)


def _layer_spec(shape, *lead, resident=False):
    nd = len(shape)
    kw = {"pipeline_mode": pl.Buffered(1)} if resident else {}
    return pl.BlockSpec((None,) * len(lead) + tuple(shape), lambda *_: tuple(lead) + (0,) * nd, **kw)


def _sigmoid(x):
    return jax.nn.sigmoid(x)


def _gelu(x):
    return jax.nn.gelu(x)


def _gelu_grad(x):
    c = 0.7978845608028654
    a = 0.044715
    t = jnp.tanh(c * (x + a * x * x * x))
    return 0.5 * (1.0 + t) + 0.5 * x * (1.0 - t * t) * c * (1.0 + 3.0 * a * x * x)


def _ln_fwd(z):
    mu = jnp.mean(z, axis=-1, keepdims=True)
    zc = z - mu
    var = jnp.mean(zc * zc, axis=-1, keepdims=True)
    rstd = lax.rsqrt(var + LN_EPS)
    return zc * rstd, rstd


def _ln_bwd(dxh, xh, rstd):
    m1 = jnp.mean(dxh, axis=-1, keepdims=True)
    m2 = jnp.mean(dxh * xh, axis=-1, keepdims=True)
    return rstd * (dxh - m1 - xh * m2)


def _colsum(a):
    return jnp.sum(a, axis=0, keepdims=True)


def _dot(a, b):
    return jnp.dot(a, b, preferred_element_type=F32)


def _dot_tn(a, b):
    return lax.dot_general(a, b, (((0,), (0,)), ((), ())), preferred_element_type=F32)


def _dot_nt(a, b):
    return lax.dot_general(a, b, (((1,), (1,)), ((), ())), preferred_element_type=F32)


def _softmax_heads(sc):
    ps = []
    for hd in range(XATTN_HEADS):
        s = sc[:, hd * N_MEM:(hd + 1) * N_MEM]
        e = jnp.exp(s - jnp.max(s, axis=-1, keepdims=True))
        ps.append(e / jnp.sum(e, axis=-1, keepdims=True))
    return jnp.concatenate(ps, axis=1)


def _lane_lo(shape):
    return (lax.broadcasted_iota(jnp.int32, shape, len(shape) - 1) % LANE) < HEAD_DIM


def _spatial_mix(vnb, wst_ref, bst_ref, mix_ref, t):
    lo = _lane_lo((CHUNK, LANE))
    for n in range(t // CHUNK):
        rows = slice(n * CHUNK, (n + 1) * CHUNK)
        for j in range(GMLP_W // LANE):
            cols = slice(j * LANE, (j + 1) * LANE)
            r = _dot(wst_ref[j], vnb[rows, cols])
            mix_ref[rows, cols] = jnp.where(lo, r[:CHUNK], r[CHUNK:]) + bst_ref[:, cols]


def _kv_fwd(mem, w_mk, w_mv):
    def body(mem_ref, wk_ref, wv_ref, kt_ref, k_ref, v_ref, vt_ref):
        mb = mem_ref[...].astype(BF16)
        k = _dot(mb, wk_ref[...])
        v = _dot(mb, wv_ref[...])
        col = lax.broadcasted_iota(jnp.int32, (N_MEM, XATTN_W), 1) // HEAD_DIM
        ks = [jnp.where(col == hd, k, 0.0) for hd in range(XATTN_HEADS)]
        vs = [jnp.where(col == hd, v, 0.0) for hd in range(XATTN_HEADS)]
        k_ref[...] = jnp.concatenate(ks, axis=0).astype(BF16)
        v_ref[...] = jnp.concatenate(vs, axis=0).astype(BF16)
        kt_ref[...] = jnp.concatenate([x.T for x in ks], axis=1).astype(BF16)
        vt_ref[...] = jnp.concatenate([x.T for x in vs], axis=1).astype(BF16)

    wide = jax.ShapeDtypeStruct((XATTN_W, XATTN_HEADS * N_MEM), BF16)
    tall = jax.ShapeDtypeStruct((XATTN_HEADS * N_MEM, XATTN_W), BF16)
    wspec = _layer_spec((D_MODEL, XATTN_W), 0)
    return pl.pallas_call(body, name="kv_fwd", grid=(1,),
                          in_specs=[_const_spec((N_MEM, D_MODEL)), wspec, wspec],
                          out_specs=[_const_spec(wide.shape), _const_spec(tall.shape), _const_spec(tall.shape),
                                     _const_spec(wide.shape)],
                          out_shape=(wide, tall, tall, wide), compiler_params=_cparams(("arbitrary",)))(mem, w_mk, w_mv)


def _kv_bwd(mem, dkt_all, dv_all):
    def body(mem_ref, dkt_ref, dv_ref, gk_ref, gv_ref):
        col = lax.broadcasted_iota(jnp.int32, (N_MEM, XATTN_W), 1) // HEAD_DIM
        dk = jnp.zeros((N_MEM, XATTN_W), F32)
        dv = jnp.zeros((N_MEM, XATTN_W), F32)
        for hd in range(XATTN_HEADS):
            dk = dk + jnp.where(col == hd, dkt_ref[:, hd * N_MEM:(hd + 1) * N_MEM].T, 0.0)
            dv = dv + jnp.where(col == hd, dv_ref[hd * N_MEM:(hd + 1) * N_MEM, :], 0.0)
        mb = mem_ref[...].astype(BF16)
        gk_ref[0] = _dot_tn(mb, dk.astype(BF16))
        gv_ref[0] = _dot_tn(mb, dv.astype(BF16))

    out = jax.ShapeDtypeStruct((1, D_MODEL, XATTN_W), F32)
    return pl.pallas_call(body, name="kv_bwd", out_shape=(out, out), compiler_params=_cparams())(mem, dkt_all, dv_all)


def _my_place():
    x, y, c = lax.axis_index("x"), lax.axis_index("y"), lax.axis_index("c")
    chips = [(1 - x, y), (x, 1 - y), (1 - x, 1 - y)]
    return x, y, c, chips


class _SideOp:
    def __init__(self, ins, out_shapes, scratch, make, aliases=None):
        self.ins, self.out_shapes, self.scratch, self.make, self.aliases = list(ins), list(out_shapes), list(scratch), make, dict(aliases or {})


def _call_with_side(body, side_ops, *, name, grid, in_specs, out_specs, out_shape, scratch_shapes, operands, semantics):
    side_ops = list(side_ops or ())
    n_in, n_out, n_scr = len(in_specs), len(out_specs), len(scratch_shapes)
    s_ins = [a for op in side_ops for a in op.ins]
    s_outs = [o for op in side_ops for o in op.out_shapes]
    s_scr = [x for op in side_ops for x in op.scratch]
    aliases, oi, oo = {}, 0, 0
    for op in side_ops:
        for a, b in op.aliases.items():
            aliases[n_in + oi + a] = n_out + oo + b
        oi, oo = oi + len(op.ins), oo + len(op.out_shapes)

    def wrapped(*refs):
        ins, sins = refs[:n_in], refs[n_in:n_in + len(s_ins)]
        base = n_in + len(s_ins)
        outs, souts = refs[base:base + n_out], refs[base + n_out:base + n_out + len(s_outs)]
        base += n_out + len(s_outs)
        scr, sscr = refs[base:base + n_scr], refs[base + n_scr:]
        if side_ops:
            first = pl.program_id(0) == 0
            last = pl.program_id(0) == grid[0] - 1
            for d in range(1, len(grid)):
                first = jnp.logical_and(first, pl.program_id(d) == 0)
                last = jnp.logical_and(last, pl.program_id(d) == grid[d] - 1)
            hooks, a, b, c = [], 0, 0, 0
            for op in side_ops:
                hooks.append(op.make(sins[a:a + len(op.ins)], souts[b:b + len(op.out_shapes)], sscr[c:c + len(op.scratch)]))
                a, b, c = a + len(op.ins), b + len(op.out_shapes), c + len(op.scratch)

            @pl.when(first)
            def _():
                for start, _w in hooks:
                    start()

        body(*ins, *outs, *scr)
        if side_ops:
            @pl.when(last)
            def _():
                for _s, wait in hooks:
                    wait()

    res = pl.pallas_call(
        wrapped, name=name, grid=grid, in_specs=list(in_specs) + [ANY] * len(s_ins),
        out_specs=list(out_specs) + [ANY] * len(s_outs), out_shape=list(out_shape) + s_outs,
        scratch_shapes=list(scratch_shapes) + s_scr, input_output_aliases=aliases,
        compiler_params=_cparams(semantics),
    )(*operands, *s_ins)
    side_res, k = [], n_out
    for op in side_ops:
        side_res.append(list(res[k:k + len(op.out_shapes)]))
        k += len(op.out_shapes)
    return list(res[:n_out]), side_res


def _gather_chips_op(shards):
    n = len(shards)

    def make(ins, outs, scr):
        send_sems, recv_sems, local_sems = scr
        x, y, c, chips = _my_place()
        mej = 2 * x + y

        def remote(tn, k, slot):
            px, py = chips[k]
            return pltpu.make_async_remote_copy(
                src_ref=ins[tn], dst_ref=outs[tn].at[slot], send_sem=send_sems.at[tn, k],
                recv_sem=recv_sems.at[tn, k], device_id=(px, py, c), device_id_type=MESH)

        def local(tn):
            return pltpu.make_async_copy(ins[tn], outs[tn].at[mej], local_sems.at[tn])

        def start():
            for k in range(3):
                for tn in range(n):
                    remote(tn, k, mej).start()
            for tn in range(n):
                local(tn).start()

        def wait():
            for k, (px, py) in enumerate(chips):
                for tn in range(n):
                    remote(tn, k, 2 * px + py).wait_recv()
                    remote(tn, k, mej).wait_send()
            for tn in range(n):
                local(tn).wait()

        return start, wait

    return _SideOp(shards, [jax.ShapeDtypeStruct((N_CHIPS,) + a.shape, a.dtype) for a in shards],
                   [pltpu.SemaphoreType.DMA((n, 3)), pltpu.SemaphoreType.DMA((n, 3)), pltpu.SemaphoreType.DMA((n,))], make)


PA_CONV_B, PA_LNA_G, PA_LNA_B, PA_LNV_G, PA_LNV_B = 0, 1, 2, 3, 4
PD_GIN, PD_BIN, PD_G1, PD_B1, PD_G2, PD_B2 = 0, 1, 2, 3, 4, 5


def _row(ref, r):
    return ref[r:r + 1, :]


def _mixer_fwd(xin, pd, win_t, conv_w, pa, wst, bst, kt_all, v_all, w_out, l, t, side_ops=None):
    s = xin.shape[0]
    nt = s // t

    def body(x_ref, pd_ref, wint_ref, cw_ref, pa_ref, wst_ref, bst_ref, kt_ref, v_ref, wout_ref,
             xh_ref, rstd_ref, h_ref, ac_ref, cat_ref, cbuf, mixbuf, zbuf):
        i = pl.program_id(0)
        x = x_ref[...] * _row(pd_ref, PD_GIN) + _row(pd_ref, PD_BIN)
        h = _dot_nt(x.astype(BF16), wint_ref[...])
        h_ref[...] = h
        a1, a2 = h[:, 0:CONV_W], h[:, CONV_W:2 * CONV_W]
        hu, hv = h[:, 2 * CONV_W:2 * CONV_W + GMLP_W], h[:, 2 * CONV_W + GMLP_W:2 * CONV_W + 2 * GMLP_W]
        q = h[:, IN_W - XATTN_W:]

        @pl.when(i == 0)
        def _():
            cbuf[0:CONV_HALO, :] = jnp.zeros((CONV_HALO, CONV_W), F32)

        cbuf[CONV_HALO:CONV_HALO + t, :] = a1 * _sigmoid(a2)
        ac = jnp.zeros((t, CONV_W), F32) + _row(pa_ref, PA_CONV_B)
        for r in range(8):
            zr = jnp.zeros((t + 8, CONV_W), F32)
            for a in range(4):
                o = 8 * a + r
                if o < CONV_K:
                    k = CONV_K - 1 - o
                    zr = zr + cbuf[CONV_HALO - 8 - 8 * a:CONV_HALO - 8 - 8 * a + t + 8, :] * cw_ref[k:k + 1, :]
            if r == 0:
                ac = ac + zr[8:, :]
            else:
                zbuf[...] = zr
                ac = ac + zbuf[8 - r:8 - r + t, :]
        ac_ref[...] = ac
        cbuf[0:CONV_HALO, :] = cbuf[t:t + CONV_HALO, :]
        xh_a, _ = _ln_fwd(ac)
        an = xh_a * _row(pa_ref, PA_LNA_G) + _row(pa_ref, PA_LNA_B)
        a = an * _sigmoid(an)

        u = _gelu(hu)
        xh_v, _ = _ln_fwd(_gelu(hv))
        vn = xh_v * _row(pa_ref, PA_LNV_G) + _row(pa_ref, PA_LNV_B)
        _spatial_mix(vn.astype(BF16), wst_ref, bst_ref, mixbuf, t)
        g = u * mixbuf[...]

        p = _softmax_heads(_dot(q.astype(BF16), kt_ref[...]) * ATT_SCALE)
        o = _dot(p.astype(BF16), v_ref[...])

        cat = jnp.concatenate([a, g, o], axis=1).astype(BF16)
        cat_ref[...] = cat
        z = ALPHA * x + _dot(cat, wout_ref[...])
        xh, rstd = _ln_fwd(z)
        xh_ref[...] = xh
        rstd_ref[...] = rstd

    tok = lambda w: pl.BlockSpec((t, w), lambda i: (i, 0))
    return _call_with_side(
        body, side_ops, name="mixer_fwd", grid=(nt,),
        in_specs=[tok(D_MODEL), _layer_spec((8, D_MODEL), l), _layer_spec((IN_W, D_MODEL), 0, resident=True),
                  _layer_spec((CONV_HALO, CONV_W), l), _layer_spec((8, CONV_W), l),
                  _layer_spec((3, 2 * CHUNK, CHUNK), l), _layer_spec((CHUNK, GMLP_W), l),
                  _const_spec((XATTN_W, XATTN_HEADS * N_MEM)), _const_spec((XATTN_HEADS * N_MEM, XATTN_W)),
                  _layer_spec((D_MODEL, D_MODEL), 0, resident=True)],
        out_specs=[tok(D_MODEL), tok(1), tok(IN_W), tok(CONV_W), tok(D_MODEL)],
        out_shape=[jax.ShapeDtypeStruct((s, D_MODEL), F32), jax.ShapeDtypeStruct((s, 1), F32),
                   jax.ShapeDtypeStruct((s, IN_W), F32), jax.ShapeDtypeStruct((s, CONV_W), F32),
                   jax.ShapeDtypeStruct((s, D_MODEL), BF16)],
        scratch_shapes=[pltpu.VMEM((t + CONV_HALO, CONV_W), F32), pltpu.VMEM((t, GMLP_W), F32),
                        pltpu.VMEM((t + 8, CONV_W), F32)],
        operands=(xin, pd, win_t, conv_w, pa, wst, bst, kt_all, v_all, w_out), semantics=("arbitrary",))


VD_LN_G, VD_LN_B, VD_LOSS = 0, 1, 2
VA_CONV_B, VA_LNA_G, VA_LNA_B, VA_LNV_G, VA_LNV_B = 0, 1, 2, 3, 4


def _mixer_bwd_d(gz, xh1, rstd1, h, ac, pd, conv_w, pa, wst, wstt, bst, kt_all, k_all, vt_all, w_out, win_t, l, t, side_ops=None):
    s = gz.shape[0]
    nt = s // t

    def body(gz_ref, xh_ref, rstd_ref, h_ref, ac_ref, pd_ref, cw_ref, pa_ref, wst_ref, wstt_ref, bst_ref,
             kt_ref, k_ref, vt_ref, wout_ref, wint_ref,
             dx_ref, dh_ref, dmix_ref, vd_ref, va_ref, dcw_ref, dws_ref, dbs_ref, dkt_ref, dv_ref,
             ebuf, mixbuf, dvnbuf, dbsacc, erbuf):
        i = pl.program_id(0)

        @pl.when(i == 0)
        def _():
            vd_ref[...] = jnp.zeros_like(vd_ref)
            va_ref[...] = jnp.zeros_like(va_ref)
            dcw_ref[...] = jnp.zeros_like(dcw_ref)
            dws_ref[...] = jnp.zeros_like(dws_ref)
            dbs_ref[...] = jnp.zeros_like(dbs_ref)
            dkt_ref[...] = jnp.zeros_like(dkt_ref)
            dv_ref[...] = jnp.zeros_like(dv_ref)
            dbsacc[...] = jnp.zeros_like(dbsacc)
            ebuf[t:t + CONV_HALO, :] = jnp.zeros((CONV_HALO, CONV_W), F32)

        gz_v = gz_ref[...]
        xh = xh_ref[...]
        vd_ref[VD_LN_G:VD_LN_G + 1, :] += _colsum(gz_v * xh)
        vd_ref[VD_LN_B:VD_LN_B + 1, :] += _colsum(gz_v)
        dz = _ln_bwd(gz_v * _row(pd_ref, PD_G1), xh, rstd_ref[...])
        dzb = dz.astype(BF16)
        dmix_ref[...] = dzb
        dcat = _dot_nt(dzb, wout_ref[...])
        d_a, d_g, d_o = dcat[:, 0:CONV_W], dcat[:, CONV_W:CONV_W + GMLP_W], dcat[:, CONV_W + GMLP_W:]

        h = h_ref[...]
        a1, a2 = h[:, 0:CONV_W], h[:, CONV_W:2 * CONV_W]
        hu, hv = h[:, 2 * CONV_W:2 * CONV_W + GMLP_W], h[:, 2 * CONV_W + GMLP_W:2 * CONV_W + 2 * GMLP_W]
        q = h[:, IN_W - XATTN_W:]

        xh_a, rstd_a = _ln_fwd(ac_ref[...])
        an = xh_a * _row(pa_ref, PA_LNA_G) + _row(pa_ref, PA_LNA_B)
        sig = _sigmoid(an)
        d_an = d_a * (sig * (1.0 + an * (1.0 - sig)))
        va_ref[VA_LNA_G:VA_LNA_G + 1, :] += _colsum(d_an * xh_a)
        va_ref[VA_LNA_B:VA_LNA_B + 1, :] += _colsum(d_an)
        dac = _ln_bwd(d_an * _row(pa_ref, PA_LNA_G), xh_a, rstd_a)
        va_ref[VA_CONV_B:VA_CONV_B + 1, :] += _colsum(dac)
        ebuf[0:t, :] = dac
        sg = _sigmoid(a2)
        glu = a1 * sg
        dglu = jnp.zeros((t, CONV_W), F32)
        for r in range(8):
            if r > 0:
                erbuf[...] = ebuf[r:r + t + 24, :]
            src = ebuf if r == 0 else erbuf
            for a in range(4):
                o = 8 * a + r
                if o < CONV_K:
                    k = CONV_K - 1 - o
                    ek = src[8 * a:8 * a + t, :]
                    dglu = dglu + ek * cw_ref[k:k + 1, :]
                    dcw_ref[k:k + 1, :] += _colsum(ek * glu)
        ebuf[t:t + CONV_HALO, :] = ebuf[0:CONV_HALO, :]
        da1 = dglu * sg
        da2 = dglu * a1 * sg * (1.0 - sg)

        u = _gelu(hu)
        xh_v, rstd_v = _ln_fwd(_gelu(hv))
        vn = xh_v * _row(pa_ref, PA_LNV_G) + _row(pa_ref, PA_LNV_B)
        vnb = vn.astype(BF16)
        _spatial_mix(vnb, wst_ref, bst_ref, mixbuf, t)
        dhu = d_g * mixbuf[...] * _gelu_grad(hu)
        dm = d_g * u
        dmb = dm.astype(BF16)
        lo = _lane_lo((CHUNK, LANE))
        for n in range(t // CHUNK):
            rows = slice(n * CHUNK, (n + 1) * CHUNK)
            dbsacc[...] += dm[rows, :]
            for j in range(GMLP_W // LANE):
                cols = slice(j * LANE, (j + 1) * LANE)
                dm_blk = dmb[rows, cols]
                r = _dot(wstt_ref[j], dm_blk)
                dvnbuf[rows, cols] = jnp.where(lo, r[:CHUNK], r[CHUNK:])
                zero = jnp.zeros_like(dm_blk)
                st = jnp.concatenate([jnp.where(lo, dm_blk, zero), jnp.where(lo, zero, dm_blk)], axis=0)
                dws_ref[j] += _dot_nt(st, vnb[rows, cols])
        dvn = dvnbuf[...]
        va_ref[VA_LNV_G:VA_LNV_G + 1, :] += _colsum(dvn * xh_v)
        va_ref[VA_LNV_B:VA_LNV_B + 1, :] += _colsum(dvn)
        dhv = _ln_bwd(dvn * _row(pa_ref, PA_LNV_G), xh_v, rstd_v) * _gelu_grad(hv)

        qb = q.astype(BF16)
        p = _softmax_heads(_dot(qb, kt_ref[...]) * ATT_SCALE)
        dob = d_o.astype(BF16)
        dp = _dot(dob, vt_ref[...])
        dss = []
        for hd in range(XATTN_HEADS):
            cs = slice(hd * N_MEM, (hd + 1) * N_MEM)
            ph, dph = p[:, cs], dp[:, cs]
            dss.append(ph * (dph - jnp.sum(ph * dph, axis=-1, keepdims=True)) * ATT_SCALE)
        dsb = jnp.concatenate(dss, axis=1).astype(BF16)
        dq = _dot(dsb, k_ref[...])
        dkt_ref[...] += _dot_tn(qb, dsb)
        dv_ref[...] += _dot_tn(p.astype(BF16), dob)

        dhb = jnp.concatenate([da1, da2, dhu, dhv, dq], axis=1).astype(BF16)
        dh_ref[...] = dhb
        dx_ref[...] = ALPHA * dz + _dot(dhb, wint_ref[...])

        @pl.when(i == nt - 1)
        def _():
            acc = dbsacc[...]
            head = lax.broadcasted_iota(jnp.int32, (CHUNK, GMLP_W), 1) // HEAD_DIM
            lane = lax.broadcasted_iota(jnp.int32, (CHUNK, LANE), 1)
            out = jnp.zeros((CHUNK, LANE), F32)
            for hd in range(GMLP_W // HEAD_DIM):
                sh = jnp.sum(jnp.where(head == hd, acc, 0.0), axis=1, keepdims=True)
                out = out + jnp.where(lane == hd, sh, 0.0)
            dbs_ref[...] = out

    rev = lambda w: pl.BlockSpec((t, w), lambda i: (nt - 1 - i, 0))
    out_shape = [
        jax.ShapeDtypeStruct((s, D_MODEL), F32), jax.ShapeDtypeStruct((s, IN_W), BF16),
        jax.ShapeDtypeStruct((s, D_MODEL), BF16),
        jax.ShapeDtypeStruct((8, D_MODEL), F32), jax.ShapeDtypeStruct((8, CONV_W), F32),
        jax.ShapeDtypeStruct((CONV_HALO, CONV_W), F32), jax.ShapeDtypeStruct((3, 2 * CHUNK, CHUNK), F32),
        jax.ShapeDtypeStruct((CHUNK, LANE), F32),
        jax.ShapeDtypeStruct((XATTN_W, XATTN_HEADS * N_MEM), F32), jax.ShapeDtypeStruct((XATTN_HEADS * N_MEM, XATTN_W), F32),
    ]
    out_specs = [rev(D_MODEL), rev(IN_W), rev(D_MODEL)] + [_const_spec(o.shape) for o in out_shape[3:]]
    return _call_with_side(
        body, side_ops, name="mixer_bwd_d", grid=(nt,),
        in_specs=[rev(D_MODEL), rev(D_MODEL), rev(1), rev(IN_W), rev(CONV_W),
                  _layer_spec((8, D_MODEL), l), _layer_spec((CONV_HALO, CONV_W), l), _layer_spec((8, CONV_W), l),
                  _layer_spec((3, 2 * CHUNK, CHUNK), l), _layer_spec((3, 2 * CHUNK, CHUNK), l),
                  _layer_spec((CHUNK, GMLP_W), l),
                  _const_spec((XATTN_W, XATTN_HEADS * N_MEM)), _const_spec((XATTN_HEADS * N_MEM, XATTN_W)),
                  _const_spec((XATTN_W, XATTN_HEADS * N_MEM)),
                  _layer_spec((D_MODEL, D_MODEL), 0, resident=True), _layer_spec((IN_W, D_MODEL), 0, resident=True)],
        out_specs=out_specs, out_shape=out_shape,
        scratch_shapes=[pltpu.VMEM((t + CONV_HALO, CONV_W), F32), pltpu.VMEM((t, GMLP_W), F32),
                        pltpu.VMEM((t, GMLP_W), F32), pltpu.VMEM((CHUNK, GMLP_W), F32),
                        pltpu.VMEM((t + 24, CONV_W), F32)],
        operands=(gz, xh1, rstd1, h, ac, pd, conv_w, pa, wst, wstt, bst, kt_all, k_all, vt_all, w_out, win_t),
        semantics=("arbitrary",))


def _mixer_bwd_w(xin, pd, dh, cat, dmix, l, t, side_ops=None):
    s = xin.shape[0]
    nt = s // t

    def body(x_ref, pd_ref, dh_ref, cat_ref, dmix_ref, dwin_ref, dwout_ref):
        @pl.when(pl.program_id(0) == 0)
        def _():
            dwin_ref[...] = jnp.zeros_like(dwin_ref)
            dwout_ref[...] = jnp.zeros_like(dwout_ref)

        xb = (x_ref[...] * _row(pd_ref, PD_GIN) + _row(pd_ref, PD_BIN)).astype(BF16)
        dwin_ref[...] += _dot_tn(dh_ref[...], xb)
        dwout_ref[...] += _dot_tn(cat_ref[...], dmix_ref[...])

    tok = lambda w: pl.BlockSpec((t, w), lambda i: (i, 0))
    return _call_with_side(
        body, side_ops, name="mixer_bwd_w", grid=(nt,),
        in_specs=[tok(D_MODEL), _layer_spec((8, D_MODEL), l), tok(IN_W), tok(D_MODEL), tok(D_MODEL)],
        out_specs=[_layer_spec((IN_W, D_MODEL), 0, resident=True), _layer_spec((D_MODEL, D_MODEL), 0, resident=True)],
        out_shape=[jax.ShapeDtypeStruct((1, IN_W, D_MODEL), F32), jax.ShapeDtypeStruct((1, D_MODEL, D_MODEL), F32)],
        scratch_shapes=[], operands=(xin, pd, dh, cat, dmix), semantics=("arbitrary",))


PF_W0, PF_B = 0, 3


def _ffn_fwd(xh1, pd, wup_t, pf, w_d, l, t, side_ops=None):
    s = xh1.shape[0]
    nt = s // t

    def body(xh_ref, pd_ref, wg_ref, wv_ref, pf_ref, wd_ref, xh2_ref, rstd_ref, upg_ref, upv_ref, fbuf):
        i = pl.program_id(0)

        @pl.when(i == 0)
        def _():
            fbuf[0:FFN_HALO, :] = jnp.zeros((FFN_HALO, FF_P), F32)

        x1 = xh_ref[...] * _row(pd_ref, PD_G1) + _row(pd_ref, PD_B1)
        xb = x1.astype(BF16)
        y = jnp.zeros((t, D_MODEL), F32)
        for hf in range(2):
            cs = slice(hf * FF_H, (hf + 1) * FF_H)
            ug = _dot_nt(xb, wg_ref[cs, :])
            uv = _dot_nt(xb, wv_ref[cs, :])
            upg_ref[:, cs] = ug
            upv_ref[:, cs] = uv
            fbuf[FFN_HALO:FFN_HALO + t, cs] = ug
            gate = jnp.zeros((t, FF_H), F32) + pf_ref[PF_B:PF_B + 1, cs]
            for k in range(FFN_CONV_K):
                off = FFN_HALO - (FFN_CONV_K - 1) + k
                gate = gate + fbuf[off:off + t, cs] * pf_ref[PF_W0 + k:PF_W0 + k + 1, cs]
            fbuf[0:FFN_HALO, cs] = fbuf[t:t + FFN_HALO, cs]
            hm = gate * _sigmoid(gate) * uv
            y = y + _dot(hm.astype(BF16), wd_ref[cs, :])
        xh2, rstd = _ln_fwd(ALPHA * x1 + y)
        xh2_ref[...] = xh2
        rstd_ref[...] = rstd

    tok = lambda w: pl.BlockSpec((t, w), lambda i: (i, 0))
    return _call_with_side(
        body, side_ops, name="ffn_fwd", grid=(nt,),
        in_specs=[tok(D_MODEL), _layer_spec((8, D_MODEL), l),
                  _layer_spec((FF_P, D_MODEL), 0, 0, resident=True), _layer_spec((FF_P, D_MODEL), 0, 1, resident=True),
                  _layer_spec((8, FF_P), l), _layer_spec((FF_P, D_MODEL), 0, resident=True)],
        out_specs=[tok(D_MODEL), tok(1), tok(FF_P), tok(FF_P)],
        out_shape=[jax.ShapeDtypeStruct((s, D_MODEL), F32), jax.ShapeDtypeStruct((s, 1), F32),
                   jax.ShapeDtypeStruct((s, FF_P), F32), jax.ShapeDtypeStruct((s, FF_P), F32)],
        scratch_shapes=[pltpu.VMEM((t + FFN_HALO, FF_P), F32)],
        operands=(xh1, pd, wup_t, wup_t, pf, w_d), semantics=("arbitrary",))


VF_W0, VF_B = 0, 3


def _ffn_bwd_d(gz_or_target, xh2, rstd2, upg, upv, pd, pf, w_d, wup_t, l, t, last, side_ops=None):
    s = xh2.shape[0]
    nt = s // t
    hb = t // FFN_HALO

    def body(gz_ref, xh2_ref, rstd_ref, upg_ref, halo_ref, upv_ref, pd_ref, pf_ref, wd_ref, wg_ref, wv_ref,
             dx_ref, dy_ref, dug_ref, duv_ref, hm_ref, vd_ref, vf_ref, gbuf, ebuf, s1buf, s2buf):
        i = pl.program_id(0)
        first_tile = i == nt - 1

        @pl.when(i == 0)
        def _():
            vd_ref[...] = jnp.zeros_like(vd_ref)
            vf_ref[...] = jnp.zeros_like(vf_ref)
            ebuf[t:t + FFN_HALO, :] = jnp.zeros((FFN_HALO, FF_P), F32)

        xh2_v = xh2_ref[...]
        if last:
            diff = xh2_v * _row(pd_ref, PD_G2) + _row(pd_ref, PD_B2) - gz_ref[...]
            vd_ref[VD_LOSS:VD_LOSS + 1, :] += _colsum(diff * diff)
            gz_v = diff * (1.0 / D_MODEL)
        else:
            gz_v = gz_ref[...]
        vd_ref[VD_LN_G:VD_LN_G + 1, :] += _colsum(gz_v * xh2_v)
        vd_ref[VD_LN_B:VD_LN_B + 1, :] += _colsum(gz_v)
        dz = _ln_bwd(gz_v * _row(pd_ref, PD_G2), xh2_v, rstd_ref[...])
        dyb = dz.astype(BF16)
        dy_ref[...] = dyb
        dx = ALPHA * dz
        for hf in range(2):
            cs = slice(hf * FF_H, (hf + 1) * FF_H)
            ug = upg_ref[:, cs]
            uv = upv_ref[:, cs]
            halo = halo_ref[:, cs]
            gbuf[0:FFN_HALO, :] = jnp.where(first_tile, jnp.zeros_like(halo), halo)
            gbuf[FFN_HALO:FFN_HALO + t, :] = ug
            s1buf[...] = gbuf[FFN_HALO - 1:FFN_HALO - 1 + t, :]
            s2buf[...] = gbuf[FFN_HALO - 2:FFN_HALO - 2 + t, :]
            ug1 = s1buf[...]
            ug2 = s2buf[...]
            gate = (pf_ref[PF_B:PF_B + 1, cs] + ug2 * pf_ref[PF_W0:PF_W0 + 1, cs] + ug1 * pf_ref[PF_W0 + 1:PF_W0 + 2, cs]
                    + ug * pf_ref[PF_W0 + 2:PF_W0 + 3, cs])
            sig = _sigmoid(gate)
            sl = gate * sig
            hm_ref[:, cs] = sl * uv
            dhm = _dot_nt(dyb, wd_ref[cs, :])
            duv = dhm * sl
            dgate = dhm * uv * (sig * (1.0 + gate * (1.0 - sig)))
            vf_ref[VF_B:VF_B + 1, cs] += _colsum(dgate)
            vf_ref[VF_W0:VF_W0 + 1, cs] += _colsum(dgate * ug2)
            vf_ref[VF_W0 + 1:VF_W0 + 2, cs] += _colsum(dgate * ug1)
            vf_ref[VF_W0 + 2:VF_W0 + 3, cs] += _colsum(dgate * ug)
            ebuf[0:t, cs] = dgate
            dug = (ebuf[2:2 + t, cs] * pf_ref[PF_W0:PF_W0 + 1, cs] + ebuf[1:1 + t, cs] * pf_ref[PF_W0 + 1:PF_W0 + 2, cs]
                   + dgate * pf_ref[PF_W0 + 2:PF_W0 + 3, cs])
            ebuf[t:t + FFN_HALO, cs] = ebuf[0:FFN_HALO, cs]
            dugb = dug.astype(BF16)
            duvb = duv.astype(BF16)
            dug_ref[:, cs] = dugb
            duv_ref[:, cs] = duvb
            dx = dx + _dot(dugb, wg_ref[cs, :]) + _dot(duvb, wv_ref[cs, :])
        dx_ref[...] = dx

        if last:
            @pl.when(i == nt - 1)
            def _():
                tot = jnp.sum(vd_ref[VD_LOSS:VD_LOSS + 1, :], axis=1, keepdims=True)
                vd_ref[VD_LOSS:VD_LOSS + 1, :] = jnp.zeros((1, D_MODEL), F32) + tot

    rev = lambda w: pl.BlockSpec((t, w), lambda i: (nt - 1 - i, 0))
    halo_spec = pl.BlockSpec((FFN_HALO, FF_P), lambda i: (jnp.maximum((nt - 1 - i) * hb - 1, 0), 0))
    out_shape = [jax.ShapeDtypeStruct((s, D_MODEL), F32), jax.ShapeDtypeStruct((s, D_MODEL), BF16),
                 jax.ShapeDtypeStruct((s, FF_P), BF16), jax.ShapeDtypeStruct((s, FF_P), BF16),
                 jax.ShapeDtypeStruct((s, FF_P), F32),
                 jax.ShapeDtypeStruct((8, D_MODEL), F32), jax.ShapeDtypeStruct((8, FF_P), F32)]
    return _call_with_side(
        body, side_ops, name="ffn_bwd_d_last" if last else "ffn_bwd_d", grid=(nt,),
        in_specs=[rev(D_MODEL), rev(D_MODEL), rev(1), rev(FF_P), halo_spec, rev(FF_P),
                  _layer_spec((8, D_MODEL), l), _layer_spec((8, FF_P), l),
                  _layer_spec((FF_P, D_MODEL), 0, resident=True),
                  _layer_spec((FF_P, D_MODEL), 0, 0, resident=True), _layer_spec((FF_P, D_MODEL), 0, 1, resident=True)],
        out_specs=[rev(D_MODEL), rev(D_MODEL), rev(FF_P), rev(FF_P), rev(FF_P),
                   _const_spec((8, D_MODEL)), _const_spec((8, FF_P))],
        out_shape=out_shape,
        scratch_shapes=[pltpu.VMEM((t + FFN_HALO, FF_H), F32), pltpu.VMEM((t + FFN_HALO, FF_P), F32),
                        pltpu.VMEM((t, FF_H), F32), pltpu.VMEM((t, FF_H), F32)],
        operands=(gz_or_target, xh2, rstd2, upg, upg, upv, pd, pf, w_d, wup_t, wup_t), semantics=("arbitrary",))


def _ffn_bwd_w(xh1, pd, dy, dug, duv, hm, l, t, side_ops=None):
    s = xh1.shape[0]
    nt = s // t

    def body(xh_ref, pd_ref, dy_ref, dug_ref, duv_ref, hm_ref, dwup_ref, dwd_ref):
        @pl.when(pl.program_id(1) == 0)
        def _():
            dwup_ref[...] = jnp.zeros_like(dwup_ref)
            dwd_ref[...] = jnp.zeros_like(dwd_ref)

        xb = (xh_ref[...] * _row(pd_ref, PD_G1) + _row(pd_ref, PD_B1)).astype(BF16)
        dwup_ref[0] += _dot_tn(dug_ref[...], xb)
        dwup_ref[1] += _dot_tn(duv_ref[...], xb)
        dwd_ref[...] += _dot_tn(hm_ref[...].astype(BF16), dy_ref[...])

    tok = lambda w: pl.BlockSpec((t, w), lambda c, i: (i, 0))
    half = pl.BlockSpec((t, FF_H), lambda c, i: (i, c))
    return _call_with_side(
        body, side_ops, name="ffn_bwd_w", grid=(2, nt),
        in_specs=[tok(D_MODEL), pl.BlockSpec((None, 8, D_MODEL), lambda c, i: (l, 0, 0)), tok(D_MODEL), half, half, half],
        out_specs=[pl.BlockSpec((None, 2, FF_H, D_MODEL), lambda c, i: (0, 0, c, 0), pipeline_mode=pl.Buffered(1)),
                   pl.BlockSpec((None, FF_H, D_MODEL), lambda c, i: (0, c, 0), pipeline_mode=pl.Buffered(1))],
        out_shape=[jax.ShapeDtypeStruct((1, 2, FF_P, D_MODEL), F32), jax.ShapeDtypeStruct((1, FF_P, D_MODEL), F32)],
        scratch_shapes=[], operands=(xh1, pd, dy, dug, duv, hm), semantics=("arbitrary", "arbitrary"))


def _adamw_math(w, g, m, v):
    nm = ADAM_B1 * m + (1.0 - ADAM_B1) * g
    nv = ADAM_B2 * v + (1.0 - ADAM_B2) * (g * g)
    m_hat = nm / (1.0 - ADAM_B1 ** ADAM_STEP)
    v_hat = nv / (1.0 - ADAM_B2 ** ADAM_STEP)
    return -ADAM_LR * (m_hat / (jnp.sqrt(v_hat) + ADAM_EPS) + ADAM_WD * w), nm, nv


def _adamw_layers(w, gs, m, v, name):
    shp = w.shape
    _, rows, cols = shp
    tr = _row_tile(rows, cols * 4, mult=8)
    nb = rows // tr

    def body(w_ref, g0_ref, g1_ref, m_ref, v_ref, g_ref, d_ref, nm_ref, nv_ref):
        g = jnp.where(pl.program_id(0) == 0, g0_ref[...], g1_ref[...])
        g_ref[...] = g
        d_ref[...], nm_ref[...], nv_ref[...] = _adamw_math(w_ref[...], g, m_ref[...], v_ref[...])

    stacked = pl.BlockSpec((tr, cols), lambda l, i: (l * nb + i, 0))
    single = pl.BlockSpec((tr, cols), lambda l, i: (i, 0))
    sh = jax.ShapeDtypeStruct((DEPTH * rows, cols), F32)
    flat = lambda a: a.reshape(DEPTH * rows, cols)
    outs = pl.pallas_call(body, name=name, grid=(DEPTH, nb), in_specs=[stacked, single, single, stacked, stacked],
                          out_specs=[stacked] * 4, out_shape=[sh] * 4,
                          compiler_params=_cparams(("arbitrary", "arbitrary")))(flat(w), gs[0], gs[1], flat(m), flat(v))
    return [o.reshape(shp) for o in outs]


def _adamw_small(ws, gs, ms, vs):
    n = len(ws)

    def body(*refs):
        w_refs, g_refs, m_refs, v_refs = refs[:n], refs[n:2 * n], refs[2 * n:3 * n], refs[3 * n:4 * n]
        d_refs, nm_refs, nv_refs = refs[4 * n:5 * n], refs[5 * n:6 * n], refs[6 * n:7 * n]
        for k in range(n):
            d_refs[k][...], nm_refs[k][...], nv_refs[k][...] = _adamw_math(w_refs[k][...], g_refs[k][...], m_refs[k][...],
                                                                             v_refs[k][...])

    shapes = [jax.ShapeDtypeStruct(w.shape, F32) for w in ws]
    outs = pl.pallas_call(body, name="adamw_small", out_shape=shapes * 3, compiler_params=_cparams())(*ws, *gs, *ms, *vs)
    return outs[:n], outs[n:2 * n], outs[2 * n:]


def _all_gather_chips(tensors, name):
    n = len(tensors)
    halves = [a.shape[1] // 2 for a in tensors]

    def body(*refs):
        x_refs, out_refs = refs[:n], refs[n:2 * n]
        send_sems, recv_sems, local_sems = refs[2 * n:]
        x, y, c, chips = _my_place()
        me, sibling, mej = (x, y, c), (x, y, 1 - c), 2 * x + y

        def rows(tn, px, py, pc):
            return out_refs[tn].at[:, 2 * px + py, pl.ds(pc * halves[tn], halves[tn]), :]

        def copy(tn, k, block, to, src=None):
            return pltpu.make_async_remote_copy(
                src_ref=rows(tn, *block) if src is None else src, dst_ref=rows(tn, *block),
                send_sem=send_sems.at[tn, k], recv_sem=recv_sems.at[tn, k], device_id=to, device_id_type=MESH)

        mine_src = [x_refs[tn].at[:, pl.ds(c * halves[tn], halves[tn]), :] for tn in range(n)]
        mine = [pltpu.make_async_copy(mine_src[tn], rows(tn, *me), local_sems.at[tn]) for tn in range(n)]
        first = []
        for j, chip in enumerate(chips):
            first += [copy(tn, 1 + j, me, (*chip, c), src=mine_src[tn]) for tn in range(n)]
        first += [copy(tn, 0, me, sibling, src=mine_src[tn]) for tn in range(n)]
        for cp in first + mine:
            cp.start()
        passed = []
        for j, chip in enumerate(chips):
            for tn in range(n):
                copy(tn, 1 + j, (*chip, c), me).wait_recv()
                fwd = copy(tn, 4 + j, (*chip, c), sibling)
                fwd.start()
                passed.append(fwd)
        for tn in range(n):
            copy(tn, 0, sibling, me).wait_recv()
            for j, chip in enumerate(chips):
                copy(tn, 4 + j, (*chip, 1 - c), me).wait_recv()
        for cp in first + passed:
            cp.wait_send()
        for cp in mine:
            cp.wait()

    return pl.pallas_call(
        body, name=name,
        out_shape=[jax.ShapeDtypeStruct((a.shape[0], N_CHIPS) + a.shape[1:], a.dtype) for a in tensors],
        in_specs=[ANY] * n, out_specs=[ANY] * n,
        scratch_shapes=[pltpu.SemaphoreType.DMA((n, 7)), pltpu.SemaphoreType.DMA((n, 7)), pltpu.SemaphoreType.DMA((n,))],
    )(*tensors)


def _swap_op(g5s):
    n = len(g5s)

    def make(ins, outs, scr):
        send_sems, recv_sems = scr
        x, y, c, _ = _my_place()

        def copies():
            return [pltpu.make_async_remote_copy(
                src_ref=ins[tn].at[:, :, 1 - c], dst_ref=outs[tn], send_sem=send_sems.at[tn], recv_sem=recv_sems.at[tn],
                device_id=(x, y, 1 - c), device_id_type=MESH) for tn in range(n)]

        def start():
            for cp in copies():
                cp.start()

        def wait():
            for cp in copies():
                cp.wait()

        return start, wait

    return _SideOp(g5s, [jax.ShapeDtypeStruct(g.shape[:2] + g.shape[3:], g.dtype) for g in g5s],
                   [pltpu.SemaphoreType.DMA((n,)), pltpu.SemaphoreType.DMA((n,))], make)


def _scatter_op(parts):
    n = len(parts)

    def make(ins, outs, scr):
        send_sems, recv_sems = scr
        x, y, c, chips = _my_place()

        def copies():
            return [pltpu.make_async_remote_copy(
                src_ref=ins[tn].at[:, 2 * px + py], dst_ref=outs[tn].at[:, k],
                send_sem=send_sems.at[tn, k], recv_sem=recv_sems.at[tn, k],
                device_id=(px, py, c), device_id_type=MESH) for k, (px, py) in enumerate(chips) for tn in range(n)]

        def start():
            for cp in copies():
                cp.start()

        def wait():
            for cp in copies():
                cp.wait()

        return start, wait

    return _SideOp(parts, [jax.ShapeDtypeStruct((p.shape[0], 3) + p.shape[2:], p.dtype) for p in parts],
                   [pltpu.SemaphoreType.DMA((n, 3)), pltpu.SemaphoreType.DMA((n, 3))], make)


def _sgather_op(fs):
    n = len(fs)

    def make(ins, outs, scr):
        send_sems, recv_sems = scr
        x, y, c, _ = _my_place()

        def copy(tn, dst_half):
            return pltpu.make_async_remote_copy(
                src_ref=outs[tn].at[:, c], dst_ref=outs[tn].at[:, dst_half], send_sem=send_sems.at[tn],
                recv_sem=recv_sems.at[tn], device_id=(x, y, 1 - c), device_id_type=MESH)

        def start():
            for tn in range(n):
                copy(tn, c).start()

        def wait():
            for tn in range(n):
                copy(tn, 1 - c).wait_recv()
                copy(tn, c).wait_send()

        return start, wait

    return _SideOp(fs, [jax.ShapeDtypeStruct(f.shape, f.dtype) for f in fs],
                   [pltpu.SemaphoreType.DMA((n,)), pltpu.SemaphoreType.DMA((n,))], make, aliases={tn: tn for tn in range(n)})


def _run_side_ops(ops, name):
    return _call_with_side(lambda: None, ops, name=name, grid=(1,), in_specs=[], out_specs=[], out_shape=[],
                           scratch_shapes=[], operands=(), semantics=("arbitrary",))[1]


def _gather_devices_op(xs):
    def make(ins, outs, scr):
        send_sems, recv_sems, local_sem = scr
        x, y, c, chips = _my_place()
        peers = [(x, y, 1 - c)] + [(px, py, pc) for (px, py) in chips for pc in (c, 1 - c)]
        me = 4 * x + 2 * y + c

        def copy(k, slot):
            return pltpu.make_async_remote_copy(
                src_ref=ins[0], dst_ref=outs[0].at[slot], send_sem=send_sems.at[k], recv_sem=recv_sems.at[k],
                device_id=peers[k], device_id_type=MESH)

        def local():
            return pltpu.make_async_copy(ins[0], outs[0].at[me], local_sem)

        def start():
            for k in range(7):
                copy(k, me).start()
            local().start()

        def wait():
            for k, (px, py, pc) in enumerate(peers):
                copy(k, 4 * px + 2 * py + pc).wait_recv()
                copy(k, me).wait_send()
            local().wait()

        return start, wait

    return _SideOp([xs], [jax.ShapeDtypeStruct((8,) + xs.shape, xs.dtype)],
                   [pltpu.SemaphoreType.DMA((7,)), pltpu.SemaphoreType.DMA((7,)), pltpu.SemaphoreType.DMA], make)


def _add_halves(gs, recvs, place):
    n = len(gs)

    def body(place_ref, *refs):
        g_refs, r_refs, o_refs = refs[:n], refs[n:2 * n], refs[2 * n:]
        for tn in range(n):
            o_refs[tn][...] = (g_refs[tn][...] + r_refs[tn][...]).astype(BF16)

    def gspec(g):
        return pl.BlockSpec((None, None, None) + g.shape[3:], lambda l, j, p: (l, j, p[1], 0, 0))

    def rspec(r):
        return pl.BlockSpec((None, None) + r.shape[2:], lambda l, j, p: (l, j, 0, 0))

    grid_spec = pltpu.PrefetchScalarGridSpec(
        num_scalar_prefetch=1, grid=(gs[0].shape[0], N_CHIPS),
        in_specs=[gspec(g) for g in gs] + [rspec(r) for r in recvs], out_specs=[rspec(r) for r in recvs])
    return pl.pallas_call(body, name="rs_add", grid_spec=grid_spec,
                          out_shape=[jax.ShapeDtypeStruct(r.shape, BF16) for r in recvs],
                          compiler_params=_cparams(("arbitrary", "arbitrary")))(place, *gs, *recvs)


def _sum_slots(parts, slots, place):
    n = len(parts)

    def body(place_ref, *refs):
        p_refs, s_refs, o_refs = refs[:n], refs[n:2 * n], refs[2 * n:]
        for tn in range(n):
            acc = p_refs[tn][...].astype(F32)
            for k in range(3):
                acc = acc + s_refs[tn][k].astype(F32)
            o_refs[tn][...] = acc

    def pspec(p):
        return pl.BlockSpec((None, None) + p.shape[2:], lambda l, pl_: (l, pl_[0], 0, 0))

    def sspec(sl):
        return pl.BlockSpec((None,) + sl.shape[1:], lambda l, pl_: (l, 0, 0, 0))

    def ospec(p):
        return pl.BlockSpec((None, None) + p.shape[2:], lambda l, pl_: (l, pl_[1], 0, 0))

    grid_spec = pltpu.PrefetchScalarGridSpec(
        num_scalar_prefetch=1, grid=(parts[0].shape[0],),
        in_specs=[pspec(p) for p in parts] + [sspec(sl) for sl in slots], out_specs=[ospec(p) for p in parts])
    return pl.pallas_call(body, name="rs_sum", grid_spec=grid_spec,
                          out_shape=[jax.ShapeDtypeStruct((p.shape[0], 2) + p.shape[2:], F32) for p in parts],
                          compiler_params=_cparams(("arbitrary",)))(place, *parts, *slots)


def _sum_devices(gathered, m_per):
    def body(g_ref, o_ref):
        acc = g_ref[0:m_per, :]
        for d in range(1, 8):
            acc = acc + g_ref[d * m_per:(d + 1) * m_per, :]
        o_ref[...] = acc

    return pl.pallas_call(body, name="small_sum", out_shape=jax.ShapeDtypeStruct((m_per, LANE), F32),
                          compiler_params=_cparams())(gathered)


def _pad_ff_cols(a):
    lead = a.shape[:-1]
    n = a.shape[-1] // FF_Q
    a = a.reshape(*lead, n, FF_Q)
    a = jnp.pad(a, [(0, 0)] * len(lead) + [(0, 0), (0, FF_QP - FF_Q)])
    return a.reshape(*lead, n * FF_QP)


def _unpad_ff_cols(a):
    lead = a.shape[:-1]
    n = a.shape[-1] // FF_QP
    return a.reshape(*lead, n, FF_QP)[..., :FF_Q].reshape(*lead, n * FF_Q)


def _pack_small(parts):
    flat = jnp.concatenate([p.reshape(-1) for p in parts])
    return flat.reshape(-1, LANE)


SMALL_SHAPES = [("conv_a_w", (CONV_K, CONV_W)), ("conv_a_b", (CONV_W,)), ("ln_a_g", (CONV_W,)), ("ln_a_b", (CONV_W,)),
                ("ln_v_g", (GMLP_W,)), ("ln_v_b", (GMLP_W,)), ("w_s", (6, CHUNK, CHUNK)), ("b_s", (6, CHUNK)),
                ("ln1_g", (D_MODEL,)), ("ln1_b", (D_MODEL,)), ("conv_f_w", (FFN_CONV_K, D_FF)), ("conv_f_b", (D_FF,)),
                ("ln2_g", (D_MODEL,)), ("ln2_b", (D_MODEL,))]


def _unpack_small(flat2d):
    flat = flat2d.reshape(DEPTH, -1)
    out, o = {}, 0
    for name, shp in SMALL_SHAPES:
        n = 1
        for d in shp:
            n *= d
        out[name] = flat[:, o:o + n].reshape((DEPTH,) + shp)
        o += n
    return out


def _rows8(rows):
    blk = jnp.stack(rows, axis=1)
    return jnp.pad(blk, ((0, 0), (0, 8 - len(rows)), (0, 0)))


def kernel(x, mem, w_in, conv_a_w, conv_a_b, ln_a_g, ln_a_b, ln_v_g, ln_v_b, w_s, b_s, w_mk, w_mv, w_out, ln1_g, ln1_b, w_up, conv_f_w, conv_f_b, w_down, ln2_g, ln2_b, loss_target, m_w_in, m_conv_a_w, m_conv_a_b, m_ln_a_g, m_ln_a_b, m_ln_v_g, m_ln_v_b, m_w_s, m_b_s, m_w_mk, m_w_mv, m_w_out, m_ln1_g, m_ln1_b, m_w_up, m_conv_f_w, m_conv_f_b, m_w_down, m_ln2_g, m_ln2_b, v_w_in, v_conv_a_w, v_conv_a_b, v_ln_a_g, v_ln_a_b, v_ln_v_g, v_ln_v_b, v_w_s, v_b_s, v_w_mk, v_w_mv, v_w_out, v_ln1_g, v_ln1_b, v_w_up, v_conv_f_w, v_conv_f_b, v_w_down, v_ln2_g, v_ln2_b):
    seq = x.shape[1]
    t_fwd = min(512, seq)
    t_bwd = min(256, seq)
    t_wg = min(1024, seq)
    chip = 2 * lax.axis_index("x") + lax.axis_index("y")
    core = lax.axis_index("c")
    place = jnp.stack([chip, core]).astype(jnp.int32)
    x0 = x[0]
    mem0 = mem[0]
    target = loss_target[0]

    sh_in = w_in.transpose(0, 2, 1).astype(BF16)
    sh_mk, sh_mv, sh_out = w_mk.astype(BF16), w_mv.astype(BF16), w_out.astype(BF16)
    sh_up = _pad_ff_cols(w_up).transpose(0, 2, 1).astype(BF16)
    sh_dn = jnp.pad(w_down, ((0, 0), (0, FF_QP - FF_Q), (0, 0))).astype(BF16)

    def mixer_weights(g_in, g_mk, g_mv, g_out):
        return dict(win_t=g_in.reshape(1, IN_W, D_MODEL), wmk=g_mk.reshape(1, D_MODEL, XATTN_W),
                    wmv=g_mv.reshape(1, D_MODEL, XATTN_W), wout=g_out.reshape(1, D_MODEL, D_MODEL))

    def ffn_weights(g_up, g_dn):
        return dict(wup_t=g_up.reshape(1, 2, FF_P, D_MODEL), wdown=g_dn.reshape(1, FF_P, D_MODEL))

    n_ca = conv_a_w.size
    small_w = _pack_small([conv_a_w, conv_f_w, jnp.zeros((2 * 80 * LANE - n_ca - conv_f_w.size,), F32)])[None]
    *g_mixer0, small_g = _all_gather_chips([sh_in[:1], sh_mk[:1], sh_mv[:1], sh_out[:1], small_w], "ag_mixer0")
    wts = [mixer_weights(*g_mixer0), None]
    small_g = small_g.reshape(N_CHIPS, -1)
    conv_a_full = small_g[:, :n_ca].reshape(N_CHIPS, DEPTH, CONV_K, CONV_W // 4).transpose(1, 2, 0, 3).reshape(DEPTH, CONV_K, CONV_W)
    conv_f_full = small_g[:, n_ca:n_ca + conv_f_w.size].reshape(N_CHIPS, DEPTH, FFN_CONV_K, FF_Q).transpose(1, 2, 0, 3).reshape(DEPTH, FFN_CONV_K, D_FF)

    tril = jnp.tril(jnp.ones((CHUNK, CHUNK), dtype=bool))
    ws_m = jnp.where(tril, w_s, 0.0)
    wst = ws_m.reshape(DEPTH, 3, 2 * CHUNK, CHUNK).astype(BF16)
    wstt = ws_m.transpose(0, 1, 3, 2).reshape(DEPTH, 3, 2 * CHUNK, CHUNK).astype(BF16)
    bst = jnp.repeat(b_s.transpose(0, 2, 1), HEAD_DIM, axis=2)
    conv_w = jnp.pad(conv_a_full, ((0, 0), (0, CONV_HALO - CONV_K), (0, 0)))
    pa = _rows8([conv_a_b, ln_a_g, ln_a_b, ln_v_g, ln_v_b])
    gin = jnp.concatenate([jnp.ones((1, D_MODEL), F32), ln2_g[:DEPTH - 1]], axis=0)
    bin_ = jnp.concatenate([jnp.zeros((1, D_MODEL), F32), ln2_b[:DEPTH - 1]], axis=0)
    pd = _rows8([gin, bin_, ln1_g, ln1_b, ln2_g, ln2_b])
    pf = jnp.concatenate([_pad_ff_cols(conv_f_full), _pad_ff_cols(conv_f_b)[:, None, :],
                          jnp.zeros((DEPTH, 8 - FFN_CONV_K - 1, FF_P), F32)], axis=1)

    acts = []
    xin = x0
    for l in range(DEPTH):
        w = wts[l]
        kt_all, k_all, v_all, vt_all = _kv_fwd(mem0, w["wmk"], w["wmv"])
        ops = [_gather_chips_op([sh_up[0], sh_dn[0]])] if l == 0 else None
        (xh1, rstd1, h, ac, cat), side = _mixer_fwd(xin, pd, w["win_t"], conv_w, pa, wst, bst, kt_all, v_all, w["wout"],
                                                    l, t_fwd, ops)
        if l == 0:
            w.update(ffn_weights(*side[0]))
        ops = [_gather_chips_op([sh_in[1], sh_mk[1], sh_mv[1], sh_out[1], sh_up[1], sh_dn[1]])] if l == 0 else None
        (xh2, rstd2, upg, upv), side = _ffn_fwd(xh1, pd, w["wup_t"], pf, w["wdown"], l, t_fwd, ops)
        if l == 0:
            wts[1] = {**mixer_weights(*side[0][:4]), **ffn_weights(*side[0][4:])}
        acts.append(dict(xin=xin, kt_all=kt_all, k_all=k_all, vt_all=vt_all, xh1=xh1, rstd1=rstd1, h=h, ac=ac, cat=cat,
                         xh2=xh2, rstd2=rstd2, upg=upg, upv=upv))
        xin = xh2

    assert DEPTH == 2

    def halves_view(gs):
        return [g.reshape(1, N_CHIPS, 2, g.shape[1] // (2 * N_CHIPS), g.shape[2]) for g in gs]

    small = [None] * DEPTH
    small_packed = [None] * DEPTH
    small_gathered = [None] * DEPTH
    red_layers = [None] * DEPTH
    gz = target
    loss_sum = None
    g5_prev = None
    for l in reversed(range(DEPTH)):
        a, w = acts[l], wts[l]
        last = l == DEPTH - 1
        (dx1, dy, dug, duv, hm, vd2, vf), side = _ffn_bwd_d(
            gz, a["xh2"], a["rstd2"], a["upg"], a["upv"], pd, pf, w["wdown"], w["wup_t"], l, t_bwd, last,
            [_swap_op(g5_prev), _gather_devices_op(small_packed[l + 1])] if g5_prev else None)
        if last:
            loss_sum = vd2[VD_LOSS, 0]
        if g5_prev:
            small_gathered[l + 1] = side[1][0]
        parts_prev = _add_halves(g5_prev, side[0], place) if g5_prev else None
        (gw_up_t, gw_down), side = _ffn_bwd_w(a["xh1"], pd, dy, dug, duv, hm, l, t_wg,
                                              [_scatter_op(parts_prev)] if g5_prev else None)
        halves_prev = _sum_slots(parts_prev, side[0], place) if g5_prev else None
        g5_up = halves_view([gw_up_t.reshape(1, 2 * FF_P, D_MODEL)])
        ops = ([_sgather_op(halves_prev)] if g5_prev else []) + ([_swap_op(g5_up)] if l == 0 else [])
        (dx0, dh, dmix, vd1, va, dcw, dws, dbs, dkt, dv), side = _mixer_bwd_d(
            dx1, a["xh1"], a["rstd1"], a["h"], a["ac"], pd, conv_w, pa, wst, wstt, bst,
            a["kt_all"], a["k_all"], a["vt_all"], w["wout"], w["win_t"], l, t_fwd, ops)
        if g5_prev:
            red_layers[l + 1] = side[0]
        parts_up = _add_halves(g5_up, side[-1], place) if l == 0 else None
        dws6 = jnp.where(tril, dws.reshape(6, CHUNK, CHUNK), 0.0)
        small[l] = [dcw[:CONV_K], va[VA_CONV_B], va[VA_LNA_G], va[VA_LNA_B], va[VA_LNV_G], va[VA_LNV_B], dws6,
                    dbs[:, :6].T, vd1[VD_LN_G], vd1[VD_LN_B],
                    _unpad_ff_cols(vf[VF_W0:VF_W0 + FFN_CONV_K]), _unpad_ff_cols(vf[VF_B]),
                    vd2[VD_LN_G], vd2[VD_LN_B]]
        small_packed[l] = _pack_small(small[l])
        ops = [_scatter_op(parts_up), _gather_devices_op(small_packed[l])] if l == 0 else None
        (gw_in_t, gw_out), side = _mixer_bwd_w(a["xin"], pd, dh, a["cat"], dmix, l, t_wg, ops)
        gw_mk, gw_mv = _kv_bwd(mem0, dkt, dv)
        g5_rest = halves_view([gw_in_t, gw_mk, gw_mv, gw_out, gw_down])
        if l == 0:
            small_gathered[l] = side[1][0]
            halves_up = _sum_slots(parts_up, side[0], place)
            red_up, recv_rest = _run_side_ops([_sgather_op(halves_up), _swap_op(g5_rest)], "rs_tail_swap")
            parts_rest = _add_halves(g5_rest, recv_rest, place)
            halves_rest = _sum_slots(parts_rest, _run_side_ops([_scatter_op(parts_rest)], "rs_tail_chips")[0], place)
            red_rest = _run_side_ops([_sgather_op(halves_rest)], "rs_tail_gather")[0]
            red_layers[0] = red_rest[:4] + red_up + red_rest[4:]
        else:
            g5_prev = g5_rest[:4] + g5_up + g5_rest[4:]
        gz = dx0
    grad_x = gz[None]

    def shard_grads(red):
        r = [f.reshape(-1, f.shape[-1]) for f in red]
        return dict(w_in=r[0].T, w_mk=r[1], w_mv=r[2], w_out=r[3], w_up=_unpad_ff_cols(r[4].T), w_down=r[5][:FF_Q])

    big_grads = [shard_grads(red_layers[l]) for l in range(DEPTH)]

    m_small = small_gathered[0].shape[1]
    small_red = jnp.concatenate([_sum_devices(g.reshape(8 * m_small, LANE), m_small) for g in small_gathered], axis=0)
    sg = _unpack_small(small_red)
    g_conv_a_w = lax.dynamic_slice_in_dim(sg["conv_a_w"], chip * (CONV_W // 4), CONV_W // 4, axis=2)
    g_conv_f_w = lax.dynamic_slice_in_dim(sg["conv_f_w"], chip * FF_Q, FF_Q, axis=2)

    loss = 0.5 / D_MODEL * lax.psum(loss_sum, ("x", "y", "c"))

    grads = dict(conv_a_w=g_conv_a_w, conv_a_b=sg["conv_a_b"], ln_a_g=sg["ln_a_g"], ln_a_b=sg["ln_a_b"],
                 ln_v_g=sg["ln_v_g"], ln_v_b=sg["ln_v_b"], w_s=sg["w_s"], b_s=sg["b_s"], ln1_g=sg["ln1_g"], ln1_b=sg["ln1_b"],
                 conv_f_w=g_conv_f_w, conv_f_b=sg["conv_f_b"], ln2_g=sg["ln2_g"], ln2_b=sg["ln2_b"])
    weights = dict(w_in=w_in, conv_a_w=conv_a_w, conv_a_b=conv_a_b, ln_a_g=ln_a_g, ln_a_b=ln_a_b, ln_v_g=ln_v_g,
                   ln_v_b=ln_v_b, w_s=w_s, b_s=b_s, w_mk=w_mk, w_mv=w_mv, w_out=w_out, ln1_g=ln1_g, ln1_b=ln1_b,
                   w_up=w_up, conv_f_w=conv_f_w, conv_f_b=conv_f_b, w_down=w_down, ln2_g=ln2_g, ln2_b=ln2_b)
    mom_m = dict(w_in=m_w_in, conv_a_w=m_conv_a_w, conv_a_b=m_conv_a_b, ln_a_g=m_ln_a_g, ln_a_b=m_ln_a_b, ln_v_g=m_ln_v_g,
                 ln_v_b=m_ln_v_b, w_s=m_w_s, b_s=m_b_s, w_mk=m_w_mk, w_mv=m_w_mv, w_out=m_w_out, ln1_g=m_ln1_g,
                 ln1_b=m_ln1_b, w_up=m_w_up, conv_f_w=m_conv_f_w, conv_f_b=m_conv_f_b, w_down=m_w_down, ln2_g=m_ln2_g,
                 ln2_b=m_ln2_b)
    mom_v = dict(w_in=v_w_in, conv_a_w=v_conv_a_w, conv_a_b=v_conv_a_b, ln_a_g=v_ln_a_g, ln_a_b=v_ln_a_b, ln_v_g=v_ln_v_g,
                 ln_v_b=v_ln_v_b, w_s=v_w_s, b_s=v_b_s, w_mk=v_w_mk, w_mv=v_w_mv, w_out=v_w_out, ln1_g=v_ln1_g,
                 ln1_b=v_ln1_b, w_up=v_w_up, conv_f_w=v_conv_f_w, conv_f_b=v_conv_f_b, w_down=v_w_down, ln2_g=v_ln2_g,
                 ln2_b=v_ln2_b)
    names = list(weights)
    big_names = ["w_in", "w_mk", "w_mv", "w_out", "w_up", "w_down"]
    delta, new_m, new_v = {}, {}, {}
    for n in big_names:
        grads[n], delta[n], new_m[n], new_v[n] = _adamw_layers(weights[n], [big_grads[l][n] for l in range(DEPTH)],
                                                               mom_m[n], mom_v[n], "adamw_" + n)
    small_names = [n for n in names if n not in big_names]
    ds, nms, nvs = _adamw_small([weights[n] for n in small_names], [grads[n] for n in small_names],
                                [mom_m[n] for n in small_names], [mom_v[n] for n in small_names])
    for n, d, nm, nv in zip(small_names, ds, nms, nvs):
        delta[n], new_m[n], new_v[n] = d, nm, nv

    return (loss, grad_x, *[grads[n] for n in names], *[delta[n] for n in names],
            *[new_m[n] for n in names], *[new_v[n] for n in names])
```

```python
import jax
import jax.numpy as jnp
from jax import lax
from jax.experimental import pallas as pl
from jax.experimental.pallas import tpu as pltpu

F32 = jnp.float32
BF16 = jnp.bfloat16

D_MODEL = 1024
DEPTH = 2
CONV_W = 384
GMLP_W = 384
XATTN_W = 256
XATTN_HEADS = 4
HEAD_DIM = 64
IN_W = 1792
CONV_K = 31
CHUNK = 128
N_MEM = 256
D_FF = 2752
FFN_CONV_K = 3
ALPHA = (2.0 * DEPTH) ** 0.25
LN_EPS = 1e-5
ATT_SCALE = 1.0 / 8.0
ADAM_LR, ADAM_B1, ADAM_B2, ADAM_EPS, ADAM_WD, ADAM_STEP = 0.001, 0.9, 0.999, 1e-08, 0.01, 10

N_CHIPS = 4
FF_Q = D_FF // N_CHIPS
FF_QP = 704
FF_H = 2 * FF_QP
FF_P = 4 * FF_QP
LANE = 128
CONV_HALO = 32
FFN_HALO = 8
BF16_ROWS = 16
VMEM_LIMIT = 60 * 1024 * 1024

MESH = pl.DeviceIdType.MESH
ANY = pl.BlockSpec(memory_space=pl.ANY)


def _cparams(sem=None, vmem=VMEM_LIMIT):
    kw = {"vmem_limit_bytes": vmem}
    if sem is not None:
        kw["dimension_semantics"] = sem
    return pltpu.CompilerParams(**kw)


def _row_tile(rows, row_bytes, limit=2 << 20, mult=BF16_ROWS):
    if rows * row_bytes <= limit:
        return rows
    best = None
    for cand in range(mult, rows, mult):
        if rows % cand == 0 and cand * row_bytes <= limit:
            best = cand
    assert best is not None, (rows, row_bytes)
    return best


def _const_spec(shape):
    nd = len(shape)
    return pl.BlockSpec(shape, lambda *_: (0,) * nd)


def _layer_spec(shape, *lead, resident=False):
    nd = len(shape)
    kw = {"pipeline_mode": pl.Buffered(1)} if resident else {}
    return pl.BlockSpec((None,) * len(lead) + tuple(shape), lambda *_: tuple(lead) + (0,) * nd, **kw)


def _sigmoid(x):
    return jax.nn.sigmoid(x)


def _gelu(x):
    return jax.nn.gelu(x)


def _gelu_grad(x):
    c = 0.7978845608028654
    a = 0.044715
    t = jnp.tanh(c * (x + a * x * x * x))
    return 0.5 * (1.0 + t) + 0.5 * x * (1.0 - t * t) * c * (1.0 + 3.0 * a * x * x)


def _ln_fwd(z):
    mu = jnp.mean(z, axis=-1, keepdims=True)
    zc = z - mu
    var = jnp.mean(zc * zc, axis=-1, keepdims=True)
    rstd = lax.rsqrt(var + LN_EPS)
    return zc * rstd, rstd


def _ln_bwd(dxh, xh, rstd):
    m1 = jnp.mean(dxh, axis=-1, keepdims=True)
    m2 = jnp.mean(dxh * xh, axis=-1, keepdims=True)
    return rstd * (dxh - m1 - xh * m2)


def _colsum(a):
    return jnp.sum(a, axis=0, keepdims=True)


def _dot(a, b):
    return jnp.dot(a, b, preferred_element_type=F32)


def _dot_tn(a, b):
    return lax.dot_general(a, b, (((0,), (0,)), ((), ())), preferred_element_type=F32)


def _dot_nt(a, b):
    return lax.dot_general(a, b, (((1,), (1,)), ((), ())), preferred_element_type=F32)


def _softmax_heads(sc):
    ps = []
    for hd in range(XATTN_HEADS):
        s = sc[:, hd * N_MEM:(hd + 1) * N_MEM]
        e = jnp.exp(s - jnp.max(s, axis=-1, keepdims=True))
        ps.append(e / jnp.sum(e, axis=-1, keepdims=True))
    return jnp.concatenate(ps, axis=1)


def _lane_lo(shape):
    return (lax.broadcasted_iota(jnp.int32, shape, len(shape) - 1) % LANE) < HEAD_DIM


def _spatial_mix(vnb, wst_ref, bst_ref, mix_ref, t):
    lo = _lane_lo((CHUNK, LANE))
    for n in range(t // CHUNK):
        rows = slice(n * CHUNK, (n + 1) * CHUNK)
        for j in range(GMLP_W // LANE):
            cols = slice(j * LANE, (j + 1) * LANE)
            r = _dot(wst_ref[j], vnb[rows, cols])
            mix_ref[rows, cols] = jnp.where(lo, r[:CHUNK], r[CHUNK:]) + bst_ref[:, cols]


def _kv_fwd(mem, w_mk, w_mv):
    def body(mem_ref, wk_ref, wv_ref, kt_ref, k_ref, v_ref, vt_ref):
        mb = mem_ref[...].astype(BF16)
        k = _dot(mb, wk_ref[...])
        v = _dot(mb, wv_ref[...])
        col = lax.broadcasted_iota(jnp.int32, (N_MEM, XATTN_W), 1) // HEAD_DIM
        ks = [jnp.where(col == hd, k, 0.0) for hd in range(XATTN_HEADS)]
        vs = [jnp.where(col == hd, v, 0.0) for hd in range(XATTN_HEADS)]
        k_ref[...] = jnp.concatenate(ks, axis=0).astype(BF16)
        v_ref[...] = jnp.concatenate(vs, axis=0).astype(BF16)
        kt_ref[...] = jnp.concatenate([x.T for x in ks], axis=1).astype(BF16)
        vt_ref[...] = jnp.concatenate([x.T for x in vs], axis=1).astype(BF16)

    wide = jax.ShapeDtypeStruct((XATTN_W, XATTN_HEADS * N_MEM), BF16)
    tall = jax.ShapeDtypeStruct((XATTN_HEADS * N_MEM, XATTN_W), BF16)
    wspec = _layer_spec((D_MODEL, XATTN_W), 0)
    return pl.pallas_call(body, name="kv_fwd", grid=(1,),
                          in_specs=[_const_spec((N_MEM, D_MODEL)), wspec, wspec],
                          out_specs=[_const_spec(wide.shape), _const_spec(tall.shape), _const_spec(tall.shape),
                                     _const_spec(wide.shape)],
                          out_shape=(wide, tall, tall, wide), compiler_params=_cparams(("arbitrary",)))(mem, w_mk, w_mv)


def _kv_bwd(mem, dkt_all, dv_all):
    def body(mem_ref, dkt_ref, dv_ref, gk_ref, gv_ref):
        col = lax.broadcasted_iota(jnp.int32, (N_MEM, XATTN_W), 1) // HEAD_DIM
        dk = jnp.zeros((N_MEM, XATTN_W), F32)
        dv = jnp.zeros((N_MEM, XATTN_W), F32)
        for hd in range(XATTN_HEADS):
            dk = dk + jnp.where(col == hd, dkt_ref[:, hd * N_MEM:(hd + 1) * N_MEM].T, 0.0)
            dv = dv + jnp.where(col == hd, dv_ref[hd * N_MEM:(hd + 1) * N_MEM, :], 0.0)
        mb = mem_ref[...].astype(BF16)
        gk_ref[0] = _dot_tn(mb, dk.astype(BF16))
        gv_ref[0] = _dot_tn(mb, dv.astype(BF16))

    out = jax.ShapeDtypeStruct((1, D_MODEL, XATTN_W), F32)
    return pl.pallas_call(body, name="kv_bwd", out_shape=(out, out), compiler_params=_cparams())(mem, dkt_all, dv_all)


def _my_place():
    x, y, c = lax.axis_index("x"), lax.axis_index("y"), lax.axis_index("c")
    chips = [(1 - x, y), (x, 1 - y), (1 - x, 1 - y)]
    return x, y, c, chips


class _SideOp:
    def __init__(self, ins, out_shapes, scratch, make, aliases=None):
        self.ins, self.out_shapes, self.scratch, self.make, self.aliases = list(ins), list(out_shapes), list(scratch), make, dict(aliases or {})


def _call_with_side(body, side_ops, *, name, grid, in_specs, out_specs, out_shape, scratch_shapes, operands, semantics):
    side_ops = list(side_ops or ())
    n_in, n_out, n_scr = len(in_specs), len(out_specs), len(scratch_shapes)
    s_ins = [a for op in side_ops for a in op.ins]
    s_outs = [o for op in side_ops for o in op.out_shapes]
    s_scr = [x for op in side_ops for x in op.scratch]
    aliases, oi, oo = {}, 0, 0
    for op in side_ops:
        for a, b in op.aliases.items():
            aliases[n_in + oi + a] = n_out + oo + b
        oi, oo = oi + len(op.ins), oo + len(op.out_shapes)

    def wrapped(*refs):
        ins, sins = refs[:n_in], refs[n_in:n_in + len(s_ins)]
        base = n_in + len(s_ins)
        outs, souts = refs[base:base + n_out], refs[base + n_out:base + n_out + len(s_outs)]
        base += n_out + len(s_outs)
        scr, sscr = refs[base:base + n_scr], refs[base + n_scr:]
        if side_ops:
            first = pl.program_id(0) == 0
            last = pl.program_id(0) == grid[0] - 1
            for d in range(1, len(grid)):
                first = jnp.logical_and(first, pl.program_id(d) == 0)
                last = jnp.logical_and(last, pl.program_id(d) == grid[d] - 1)
            hooks, a, b, c = [], 0, 0, 0
            for op in side_ops:
                hooks.append(op.make(sins[a:a + len(op.ins)], souts[b:b + len(op.out_shapes)], sscr[c:c + len(op.scratch)]))
                a, b, c = a + len(op.ins), b + len(op.out_shapes), c + len(op.scratch)

            @pl.when(first)
            def _():
                for start, _w in hooks:
                    start()

        body(*ins, *outs, *scr)
        if side_ops:
            @pl.when(last)
            def _():
                for _s, wait in hooks:
                    wait()

    res = pl.pallas_call(
        wrapped, name=name, grid=grid, in_specs=list(in_specs) + [ANY] * len(s_ins),
        out_specs=list(out_specs) + [ANY] * len(s_outs), out_shape=list(out_shape) + s_outs,
        scratch_shapes=list(scratch_shapes) + s_scr, input_output_aliases=aliases,
        compiler_params=_cparams(semantics),
    )(*operands, *s_ins)
    side_res, k = [], n_out
    for op in side_ops:
        side_res.append(list(res[k:k + len(op.out_shapes)]))
        k += len(op.out_shapes)
    return list(res[:n_out]), side_res


def _gather_chips_op(shards):
    n = len(shards)

    def make(ins, outs, scr):
        send_sems, recv_sems, local_sems = scr
        x, y, c, chips = _my_place()
        mej = 2 * x + y

        def remote(tn, k, slot):
            px, py = chips[k]
            return pltpu.make_async_remote_copy(
                src_ref=ins[tn], dst_ref=outs[tn].at[slot], send_sem=send_sems.at[tn, k],
                recv_sem=recv_sems.at[tn, k], device_id=(px, py, c), device_id_type=MESH)

        def local(tn):
            return pltpu.make_async_copy(ins[tn], outs[tn].at[mej], local_sems.at[tn])

        def start():
            for k in range(3):
                for tn in range(n):
                    remote(tn, k, mej).start()
            for tn in range(n):
                local(tn).start()

        def wait():
            for k, (px, py) in enumerate(chips):
                for tn in range(n):
                    remote(tn, k, 2 * px + py).wait_recv()
                    remote(tn, k, mej).wait_send()
            for tn in range(n):
                local(tn).wait()

        return start, wait

    return _SideOp(shards, [jax.ShapeDtypeStruct((N_CHIPS,) + a.shape, a.dtype) for a in shards],
                   [pltpu.SemaphoreType.DMA((n, 3)), pltpu.SemaphoreType.DMA((n, 3)), pltpu.SemaphoreType.DMA((n,))], make)


PA_CONV_B, PA_LNA_G, PA_LNA_B, PA_LNV_G, PA_LNV_B = 0, 1, 2, 3, 4
PD_GIN, PD_BIN, PD_G1, PD_B1, PD_G2, PD_B2 = 0, 1, 2, 3, 4, 5


def _row(ref, r):
    return ref[r:r + 1, :]


def _mixer_fwd(xin, pd, win_t, conv_w, pa, wst, bst, kt_all, v_all, w_out, l, t, side_ops=None):
    s = xin.shape[0]
    nt = s // t

    def body(x_ref, pd_ref, wint_ref, cw_ref, pa_ref, wst_ref, bst_ref, kt_ref, v_ref, wout_ref,
             xh_ref, rstd_ref, h_ref, ac_ref, cat_ref, cbuf, mixbuf, zbuf):
        i = pl.program_id(0)
        x = x_ref[...] * _row(pd_ref, PD_GIN) + _row(pd_ref, PD_BIN)
        h = _dot_nt(x.astype(BF16), wint_ref[...])
        h_ref[...] = h
        a1, a2 = h[:, 0:CONV_W], h[:, CONV_W:2 * CONV_W]
        hu, hv = h[:, 2 * CONV_W:2 * CONV_W + GMLP_W], h[:, 2 * CONV_W + GMLP_W:2 * CONV_W + 2 * GMLP_W]
        q = h[:, IN_W - XATTN_W:]

        @pl.when(i == 0)
        def _():
            cbuf[0:CONV_HALO, :] = jnp.zeros((CONV_HALO, CONV_W), F32)

        cbuf[CONV_HALO:CONV_HALO + t, :] = a1 * _sigmoid(a2)
        ac = jnp.zeros((t, CONV_W), F32) + _row(pa_ref, PA_CONV_B)
        for r in range(8):
            zr = jnp.zeros((t + 8, CONV_W), F32)
            for a in range(4):
                o = 8 * a + r
                if o < CONV_K:
                    k = CONV_K - 1 - o
                    zr = zr + cbuf[CONV_HALO - 8 - 8 * a:CONV_HALO - 8 - 8 * a + t + 8, :] * cw_ref[k:k + 1, :]
            if r == 0:
                ac = ac + zr[8:, :]
            else:
                zbuf[...] = zr
                ac = ac + zbuf[8 - r:8 - r + t, :]
        ac_ref[...] = ac
        cbuf[0:CONV_HALO, :] = cbuf[t:t + CONV_HALO, :]
        xh_a, _ = _ln_fwd(ac)
        an = xh_a * _row(pa_ref, PA_LNA_G) + _row(pa_ref, PA_LNA_B)
        a = an * _sigmoid(an)

        u = _gelu(hu)
        xh_v, _ = _ln_fwd(_gelu(hv))
        vn = xh_v * _row(pa_ref, PA_LNV_G) + _row(pa_ref, PA_LNV_B)
        _spatial_mix(vn.astype(BF16), wst_ref, bst_ref, mixbuf, t)
        g = u * mixbuf[...]

        p = _softmax_heads(_dot(q.astype(BF16), kt_ref[...]) * ATT_SCALE)
        o = _dot(p.astype(BF16), v_ref[...])

        cat = jnp.concatenate([a, g, o], axis=1).astype(BF16)
        cat_ref[...] = cat
        z = ALPHA * x + _dot(cat, wout_ref[...])
        xh, rstd = _ln_fwd(z)
        xh_ref[...] = xh
        rstd_ref[...] = rstd

    tok = lambda w: pl.BlockSpec((t, w), lambda i: (i, 0))
    return _call_with_side(
        body, side_ops, name="mixer_fwd", grid=(nt,),
        in_specs=[tok(D_MODEL), _layer_spec((8, D_MODEL), l), _layer_spec((IN_W, D_MODEL), 0, resident=True),
                  _layer_spec((CONV_HALO, CONV_W), l), _layer_spec((8, CONV_W), l),
                  _layer_spec((3, 2 * CHUNK, CHUNK), l), _layer_spec((CHUNK, GMLP_W), l),
                  _const_spec((XATTN_W, XATTN_HEADS * N_MEM)), _const_spec((XATTN_HEADS * N_MEM, XATTN_W)),
                  _layer_spec((D_MODEL, D_MODEL), 0, resident=True)],
        out_specs=[tok(D_MODEL), tok(1), tok(IN_W), tok(CONV_W), tok(D_MODEL)],
        out_shape=[jax.ShapeDtypeStruct((s, D_MODEL), F32), jax.ShapeDtypeStruct((s, 1), F32),
                   jax.ShapeDtypeStruct((s, IN_W), F32), jax.ShapeDtypeStruct((s, CONV_W), F32),
                   jax.ShapeDtypeStruct((s, D_MODEL), BF16)],
        scratch_shapes=[pltpu.VMEM((t + CONV_HALO, CONV_W), F32), pltpu.VMEM((t, GMLP_W), F32),
                        pltpu.VMEM((t + 8, CONV_W), F32)],
        operands=(xin, pd, win_t, conv_w, pa, wst, bst, kt_all, v_all, w_out), semantics=("arbitrary",))


VD_LN_G, VD_LN_B, VD_LOSS = 0, 1, 2
VA_CONV_B, VA_LNA_G, VA_LNA_B, VA_LNV_G, VA_LNV_B = 0, 1, 2, 3, 4


def _mixer_bwd_d(gz, xh1, rstd1, h, ac, pd, conv_w, pa, wst, wstt, bst, kt_all, k_all, vt_all, w_out, win_t, l, t, side_ops=None):
    s = gz.shape[0]
    nt = s // t

    def body(gz_ref, xh_ref, rstd_ref, h_ref, ac_ref, pd_ref, cw_ref, pa_ref, wst_ref, wstt_ref, bst_ref,
             kt_ref, k_ref, vt_ref, wout_ref, wint_ref,
             dx_ref, dh_ref, dmix_ref, vd_ref, va_ref, dcw_ref, dws_ref, dbs_ref, dkt_ref, dv_ref,
             ebuf, mixbuf, dvnbuf, dbsacc, erbuf):
        i = pl.program_id(0)

        @pl.when(i == 0)
        def _():
            vd_ref[...] = jnp.zeros_like(vd_ref)
            va_ref[...] = jnp.zeros_like(va_ref)
            dcw_ref[...] = jnp.zeros_like(dcw_ref)
            dws_ref[...] = jnp.zeros_like(dws_ref)
            dbs_ref[...] = jnp.zeros_like(dbs_ref)
            dkt_ref[...] = jnp.zeros_like(dkt_ref)
            dv_ref[...] = jnp.zeros_like(dv_ref)
            dbsacc[...] = jnp.zeros_like(dbsacc)
            ebuf[t:t + CONV_HALO, :] = jnp.zeros((CONV_HALO, CONV_W), F32)

        gz_v = gz_ref[...]
        xh = xh_ref[...]
        vd_ref[VD_LN_G:VD_LN_G + 1, :] += _colsum(gz_v * xh)
        vd_ref[VD_LN_B:VD_LN_B + 1, :] += _colsum(gz_v)
        dz = _ln_bwd(gz_v * _row(pd_ref, PD_G1), xh, rstd_ref[...])
        dzb = dz.astype(BF16)
        dmix_ref[...] = dzb
        dcat = _dot_nt(dzb, wout_ref[...])
        d_a, d_g, d_o = dcat[:, 0:CONV_W], dcat[:, CONV_W:CONV_W + GMLP_W], dcat[:, CONV_W + GMLP_W:]

        h = h_ref[...]
        a1, a2 = h[:, 0:CONV_W], h[:, CONV_W:2 * CONV_W]
        hu, hv = h[:, 2 * CONV_W:2 * CONV_W + GMLP_W], h[:, 2 * CONV_W + GMLP_W:2 * CONV_W + 2 * GMLP_W]
        q = h[:, IN_W - XATTN_W:]

        xh_a, rstd_a = _ln_fwd(ac_ref[...])
        an = xh_a * _row(pa_ref, PA_LNA_G) + _row(pa_ref, PA_LNA_B)
        sig = _sigmoid(an)
        d_an = d_a * (sig * (1.0 + an * (1.0 - sig)))
        va_ref[VA_LNA_G:VA_LNA_G + 1, :] += _colsum(d_an * xh_a)
        va_ref[VA_LNA_B:VA_LNA_B + 1, :] += _colsum(d_an)
        dac = _ln_bwd(d_an * _row(pa_ref, PA_LNA_G), xh_a, rstd_a)
        va_ref[VA_CONV_B:VA_CONV_B + 1, :] += _colsum(dac)
        ebuf[0:t, :] = dac
        sg = _sigmoid(a2)
        glu = a1 * sg
        dglu = jnp.zeros((t, CONV_W), F32)
        for r in range(8):
            if r > 0:
                erbuf[...] = ebuf[r:r + t + 24, :]
            src = ebuf if r == 0 else erbuf
            for a in range(4):
                o = 8 * a + r
                if o < CONV_K:
                    k = CONV_K - 1 - o
                    ek = src[8 * a:8 * a + t, :]
                    dglu = dglu + ek * cw_ref[k:k + 1, :]
                    dcw_ref[k:k + 1, :] += _colsum(ek * glu)
        ebuf[t:t + CONV_HALO, :] = ebuf[0:CONV_HALO, :]
        da1 = dglu * sg
        da2 = dglu * a1 * sg * (1.0 - sg)

        u = _gelu(hu)
        xh_v, rstd_v = _ln_fwd(_gelu(hv))
        vn = xh_v * _row(pa_ref, PA_LNV_G) + _row(pa_ref, PA_LNV_B)
        vnb = vn.astype(BF16)
        _spatial_mix(vnb, wst_ref, bst_ref, mixbuf, t)
        dhu = d_g * mixbuf[...] * _gelu_grad(hu)
        dm = d_g * u
        dmb = dm.astype(BF16)
        lo = _lane_lo((CHUNK, LANE))
        for n in range(t // CHUNK):
            rows = slice(n * CHUNK, (n + 1) * CHUNK)
            dbsacc[...] += dm[rows, :]
            for j in range(GMLP_W // LANE):
                cols = slice(j * LANE, (j + 1) * LANE)
                dm_blk = dmb[rows, cols]
                r = _dot(wstt_ref[j], dm_blk)
                dvnbuf[rows, cols] = jnp.where(lo, r[:CHUNK], r[CHUNK:])
                zero = jnp.zeros_like(dm_blk)
                st = jnp.concatenate([jnp.where(lo, dm_blk, zero), jnp.where(lo, zero, dm_blk)], axis=0)
                dws_ref[j] += _dot_nt(st, vnb[rows, cols])
        dvn = dvnbuf[...]
        va_ref[VA_LNV_G:VA_LNV_G + 1, :] += _colsum(dvn * xh_v)
        va_ref[VA_LNV_B:VA_LNV_B + 1, :] += _colsum(dvn)
        dhv = _ln_bwd(dvn * _row(pa_ref, PA_LNV_G), xh_v, rstd_v) * _gelu_grad(hv)

        qb = q.astype(BF16)
        p = _softmax_heads(_dot(qb, kt_ref[...]) * ATT_SCALE)
        dob = d_o.astype(BF16)
        dp = _dot(dob, vt_ref[...])
        dss = []
        for hd in range(XATTN_HEADS):
            cs = slice(hd * N_MEM, (hd + 1) * N_MEM)
            ph, dph = p[:, cs], dp[:, cs]
            dss.append(ph * (dph - jnp.sum(ph * dph, axis=-1, keepdims=True)) * ATT_SCALE)
        dsb = jnp.concatenate(dss, axis=1).astype(BF16)
        dq = _dot(dsb, k_ref[...])
        dkt_ref[...] += _dot_tn(qb, dsb)
        dv_ref[...] += _dot_tn(p.astype(BF16), dob)

        dhb = jnp.concatenate([da1, da2, dhu, dhv, dq], axis=1).astype(BF16)
        dh_ref[...] = dhb
        dx_ref[...] = ALPHA * dz + _dot(dhb, wint_ref[...])

        @pl.when(i == nt - 1)
        def _():
            acc = dbsacc[...]
            head = lax.broadcasted_iota(jnp.int32, (CHUNK, GMLP_W), 1) // HEAD_DIM
            lane = lax.broadcasted_iota(jnp.int32, (CHUNK, LANE), 1)
            out = jnp.zeros((CHUNK, LANE), F32)
            for hd in range(GMLP_W // HEAD_DIM):
                sh = jnp.sum(jnp.where(head == hd, acc, 0.0), axis=1, keepdims=True)
                out = out + jnp.where(lane == hd, sh, 0.0)
            dbs_ref[...] = out

    rev = lambda w: pl.BlockSpec((t, w), lambda i: (nt - 1 - i, 0))
    out_shape = [
        jax.ShapeDtypeStruct((s, D_MODEL), F32), jax.ShapeDtypeStruct((s, IN_W), BF16),
        jax.ShapeDtypeStruct((s, D_MODEL), BF16),
        jax.ShapeDtypeStruct((8, D_MODEL), F32), jax.ShapeDtypeStruct((8, CONV_W), F32),
        jax.ShapeDtypeStruct((CONV_HALO, CONV_W), F32), jax.ShapeDtypeStruct((3, 2 * CHUNK, CHUNK), F32),
        jax.ShapeDtypeStruct((CHUNK, LANE), F32),
        jax.ShapeDtypeStruct((XATTN_W, XATTN_HEADS * N_MEM), F32), jax.ShapeDtypeStruct((XATTN_HEADS * N_MEM, XATTN_W), F32),
    ]
    out_specs = [rev(D_MODEL), rev(IN_W), rev(D_MODEL)] + [_const_spec(o.shape) for o in out_shape[3:]]
    return _call_with_side(
        body, side_ops, name="mixer_bwd_d", grid=(nt,),
        in_specs=[rev(D_MODEL), rev(D_MODEL), rev(1), rev(IN_W), rev(CONV_W),
                  _layer_spec((8, D_MODEL), l), _layer_spec((CONV_HALO, CONV_W), l), _layer_spec((8, CONV_W), l),
                  _layer_spec((3, 2 * CHUNK, CHUNK), l), _layer_spec((3, 2 * CHUNK, CHUNK), l),
                  _layer_spec((CHUNK, GMLP_W), l),
                  _const_spec((XATTN_W, XATTN_HEADS * N_MEM)), _const_spec((XATTN_HEADS * N_MEM, XATTN_W)),
                  _const_spec((XATTN_W, XATTN_HEADS * N_MEM)),
                  _layer_spec((D_MODEL, D_MODEL), 0, resident=True), _layer_spec((IN_W, D_MODEL), 0, resident=True)],
        out_specs=out_specs, out_shape=out_shape,
        scratch_shapes=[pltpu.VMEM((t + CONV_HALO, CONV_W), F32), pltpu.VMEM((t, GMLP_W), F32),
                        pltpu.VMEM((t, GMLP_W), F32), pltpu.VMEM((CHUNK, GMLP_W), F32),
                        pltpu.VMEM((t + 24, CONV_W), F32)],
        operands=(gz, xh1, rstd1, h, ac, pd, conv_w, pa, wst, wstt, bst, kt_all, k_all, vt_all, w_out, win_t),
        semantics=("arbitrary",))


def _mixer_bwd_w(xin, pd, dh, cat, dmix, l, t, side_ops=None):
    s = xin.shape[0]
    nt = s // t

    def body(x_ref, pd_ref, dh_ref, cat_ref, dmix_ref, dwin_ref, dwout_ref):
        @pl.when(pl.program_id(0) == 0)
        def _():
            dwin_ref[...] = jnp.zeros_like(dwin_ref)
            dwout_ref[...] = jnp.zeros_like(dwout_ref)

        xb = (x_ref[...] * _row(pd_ref, PD_GIN) + _row(pd_ref, PD_BIN)).astype(BF16)
        dwin_ref[...] += _dot_tn(dh_ref[...], xb)
        dwout_ref[...] += _dot_tn(cat_ref[...], dmix_ref[...])

    tok = lambda w: pl.BlockSpec((t, w), lambda i: (i, 0))
    return _call_with_side(
        body, side_ops, name="mixer_bwd_w", grid=(nt,),
        in_specs=[tok(D_MODEL), _layer_spec((8, D_MODEL), l), tok(IN_W), tok(D_MODEL), tok(D_MODEL)],
        out_specs=[_layer_spec((IN_W, D_MODEL), 0, resident=True), _layer_spec((D_MODEL, D_MODEL), 0, resident=True)],
        out_shape=[jax.ShapeDtypeStruct((1, IN_W, D_MODEL), F32), jax.ShapeDtypeStruct((1, D_MODEL, D_MODEL), F32)],
        scratch_shapes=[], operands=(xin, pd, dh, cat, dmix), semantics=("arbitrary",))


PF_W0, PF_B = 0, 3


def _ffn_fwd(xh1, pd, wup_t, pf, w_d, l, t, side_ops=None):
    s = xh1.shape[0]
    nt = s // t

    def body(xh_ref, pd_ref, wg_ref, wv_ref, pf_ref, wd_ref, xh2_ref, rstd_ref, upg_ref, upv_ref, fbuf):
        i = pl.program_id(0)

        @pl.when(i == 0)
        def _():
            fbuf[0:FFN_HALO, :] = jnp.zeros((FFN_HALO, FF_P), F32)

        x1 = xh_ref[...] * _row(pd_ref, PD_G1) + _row(pd_ref, PD_B1)
        xb = x1.astype(BF16)
        y = jnp.zeros((t, D_MODEL), F32)
        for hf in range(2):
            cs = slice(hf * FF_H, (hf + 1) * FF_H)
            ug = _dot_nt(xb, wg_ref[cs, :])
            uv = _dot_nt(xb, wv_ref[cs, :])
            upg_ref[:, cs] = ug
            upv_ref[:, cs] = uv
            fbuf[FFN_HALO:FFN_HALO + t, cs] = ug
            gate = jnp.zeros((t, FF_H), F32) + pf_ref[PF_B:PF_B + 1, cs]
            for k in range(FFN_CONV_K):
                off = FFN_HALO - (FFN_CONV_K - 1) + k
                gate = gate + fbuf[off:off + t, cs] * pf_ref[PF_W0 + k:PF_W0 + k + 1, cs]
            fbuf[0:FFN_HALO, cs] = fbuf[t:t + FFN_HALO, cs]
            hm = gate * _sigmoid(gate) * uv
            y = y + _dot(hm.astype(BF16), wd_ref[cs, :])
        xh2, rstd = _ln_fwd(ALPHA * x1 + y)
        xh2_ref[...] = xh2
        rstd_ref[...] = rstd

    tok = lambda w: pl.BlockSpec((t, w), lambda i: (i, 0))
    return _call_with_side(
        body, side_ops, name="ffn_fwd", grid=(nt,),
        in_specs=[tok(D_MODEL), _layer_spec((8, D_MODEL), l),
                  _layer_spec((FF_P, D_MODEL), 0, 0, resident=True), _layer_spec((FF_P, D_MODEL), 0, 1, resident=True),
                  _layer_spec((8, FF_P), l), _layer_spec((FF_P, D_MODEL), 0, resident=True)],
        out_specs=[tok(D_MODEL), tok(1), tok(FF_P), tok(FF_P)],
        out_shape=[jax.ShapeDtypeStruct((s, D_MODEL), F32), jax.ShapeDtypeStruct((s, 1), F32),
                   jax.ShapeDtypeStruct((s, FF_P), F32), jax.ShapeDtypeStruct((s, FF_P), F32)],
        scratch_shapes=[pltpu.VMEM((t + FFN_HALO, FF_P), F32)],
        operands=(xh1, pd, wup_t, wup_t, pf, w_d), semantics=("arbitrary",))


VF_W0, VF_B = 0, 3


def _ffn_bwd_d(gz_or_target, xh2, rstd2, upg, upv, pd, pf, w_d, wup_t, l, t, last, side_ops=None):
    s = xh2.shape[0]
    nt = s // t
    hb = t // FFN_HALO

    def body(gz_ref, xh2_ref, rstd_ref, upg_ref, halo_ref, upv_ref, pd_ref, pf_ref, wd_ref, wg_ref, wv_ref,
             dx_ref, dy_ref, dug_ref, duv_ref, hm_ref, vd_ref, vf_ref, gbuf, ebuf, s1buf, s2buf):
        i = pl.program_id(0)
        first_tile = i == nt - 1

        @pl.when(i == 0)
        def _():
            vd_ref[...] = jnp.zeros_like(vd_ref)
            vf_ref[...] = jnp.zeros_like(vf_ref)
            ebuf[t:t + FFN_HALO, :] = jnp.zeros((FFN_HALO, FF_P), F32)

        xh2_v = xh2_ref[...]
        if last:
            diff = xh2_v * _row(pd_ref, PD_G2) + _row(pd_ref, PD_B2) - gz_ref[...]
            vd_ref[VD_LOSS:VD_LOSS + 1, :] += _colsum(diff * diff)
            gz_v = diff * (1.0 / D_MODEL)
        else:
            gz_v = gz_ref[...]
        vd_ref[VD_LN_G:VD_LN_G + 1, :] += _colsum(gz_v * xh2_v)
        vd_ref[VD_LN_B:VD_LN_B + 1, :] += _colsum(gz_v)
        dz = _ln_bwd(gz_v * _row(pd_ref, PD_G2), xh2_v, rstd_ref[...])
        dyb = dz.astype(BF16)
        dy_ref[...] = dyb
        dx = ALPHA * dz
        for hf in range(2):
            cs = slice(hf * FF_H, (hf + 1) * FF_H)
            ug = upg_ref[:, cs]
            uv = upv_ref[:, cs]
            halo = halo_ref[:, cs]
            gbuf[0:FFN_HALO, :] = jnp.where(first_tile, jnp.zeros_like(halo), halo)
            gbuf[FFN_HALO:FFN_HALO + t, :] = ug
            s1buf[...] = gbuf[FFN_HALO - 1:FFN_HALO - 1 + t, :]
            s2buf[...] = gbuf[FFN_HALO - 2:FFN_HALO - 2 + t, :]
            ug1 = s1buf[...]
            ug2 = s2buf[...]
            gate = (pf_ref[PF_B:PF_B + 1, cs] + ug2 * pf_ref[PF_W0:PF_W0 + 1, cs] + ug1 * pf_ref[PF_W0 + 1:PF_W0 + 2, cs]
                    + ug * pf_ref[PF_W0 + 2:PF_W0 + 3, cs])
            sig = _sigmoid(gate)
            sl = gate * sig
            hm_ref[:, cs] = sl * uv
            dhm = _dot_nt(dyb, wd_ref[cs, :])
            duv = dhm * sl
            dgate = dhm * uv * (sig * (1.0 + gate * (1.0 - sig)))
            vf_ref[VF_B:VF_B + 1, cs] += _colsum(dgate)
            vf_ref[VF_W0:VF_W0 + 1, cs] += _colsum(dgate * ug2)
            vf_ref[VF_W0 + 1:VF_W0 + 2, cs] += _colsum(dgate * ug1)
            vf_ref[VF_W0 + 2:VF_W0 + 3, cs] += _colsum(dgate * ug)
            ebuf[0:t, cs] = dgate
            dug = (ebuf[2:2 + t, cs] * pf_ref[PF_W0:PF_W0 + 1, cs] + ebuf[1:1 + t, cs] * pf_ref[PF_W0 + 1:PF_W0 + 2, cs]
                   + dgate * pf_ref[PF_W0 + 2:PF_W0 + 3, cs])
            ebuf[t:t + FFN_HALO, cs] = ebuf[0:FFN_HALO, cs]
            dugb = dug.astype(BF16)
            duvb = duv.astype(BF16)
            dug_ref[:, cs] = dugb
            duv_ref[:, cs] = duvb
            dx = dx + _dot(dugb, wg_ref[cs, :]) + _dot(duvb, wv_ref[cs, :])
        dx_ref[...] = dx

        if last:
            @pl.when(i == nt - 1)
            def _():
                tot = jnp.sum(vd_ref[VD_LOSS:VD_LOSS + 1, :], axis=1, keepdims=True)
                vd_ref[VD_LOSS:VD_LOSS + 1, :] = jnp.zeros((1, D_MODEL), F32) + tot

    rev = lambda w: pl.BlockSpec((t, w), lambda i: (nt - 1 - i, 0))
    halo_spec = pl.BlockSpec((FFN_HALO, FF_P), lambda i: (jnp.maximum((nt - 1 - i) * hb - 1, 0), 0))
    out_shape = [jax.ShapeDtypeStruct((s, D_MODEL), F32), jax.ShapeDtypeStruct((s, D_MODEL), BF16),
                 jax.ShapeDtypeStruct((s, FF_P), BF16), jax.ShapeDtypeStruct((s, FF_P), BF16),
                 jax.ShapeDtypeStruct((s, FF_P), F32),
                 jax.ShapeDtypeStruct((8, D_MODEL), F32), jax.ShapeDtypeStruct((8, FF_P), F32)]
    return _call_with_side(
        body, side_ops, name="ffn_bwd_d_last" if last else "ffn_bwd_d", grid=(nt,),
        in_specs=[rev(D_MODEL), rev(D_MODEL), rev(1), rev(FF_P), halo_spec, rev(FF_P),
                  _layer_spec((8, D_MODEL), l), _layer_spec((8, FF_P), l),
                  _layer_spec((FF_P, D_MODEL), 0, resident=True),
                  _layer_spec((FF_P, D_MODEL), 0, 0, resident=True), _layer_spec((FF_P, D_MODEL), 0, 1, resident=True)],
        out_specs=[rev(D_MODEL), rev(D_MODEL), rev(FF_P), rev(FF_P), rev(FF_P),
                   _const_spec((8, D_MODEL)), _const_spec((8, FF_P))],
        out_shape=out_shape,
        scratch_shapes=[pltpu.VMEM((t + FFN_HALO, FF_H), F32), pltpu.VMEM((t + FFN_HALO, FF_P), F32),
                        pltpu.VMEM((t, FF_H), F32), pltpu.VMEM((t, FF_H), F32)],
        operands=(gz_or_target, xh2, rstd2, upg, upg, upv, pd, pf, w_d, wup_t, wup_t), semantics=("arbitrary",))


def _ffn_bwd_w(xh1, pd, dy, dug, duv, hm, l, t, side_ops=None):
    s = xh1.shape[0]
    nt = s // t

    def body(xh_ref, pd_ref, dy_ref, dug_ref, duv_ref, hm_ref, dwup_ref, dwd_ref):
        @pl.when(pl.program_id(1) == 0)
        def _():
            dwup_ref[...] = jnp.zeros_like(dwup_ref)
            dwd_ref[...] = jnp.zeros_like(dwd_ref)

        xb = (xh_ref[...] * _row(pd_ref, PD_G1) + _row(pd_ref, PD_B1)).astype(BF16)
        dwup_ref[0] += _dot_tn(dug_ref[...], xb)
        dwup_ref[1] += _dot_tn(duv_ref[...], xb)
        dwd_ref[...] += _dot_tn(hm_ref[...].astype(BF16), dy_ref[...])

    tok = lambda w: pl.BlockSpec((t, w), lambda c, i: (i, 0))
    half = pl.BlockSpec((t, FF_H), lambda c, i: (i, c))
    return _call_with_side(
        body, side_ops, name="ffn_bwd_w", grid=(2, nt),
        in_specs=[tok(D_MODEL), pl.BlockSpec((None, 8, D_MODEL), lambda c, i: (l, 0, 0)), tok(D_MODEL), half, half, half],
        out_specs=[pl.BlockSpec((None, 2, FF_H, D_MODEL), lambda c, i: (0, 0, c, 0), pipeline_mode=pl.Buffered(1)),
                   pl.BlockSpec((None, FF_H, D_MODEL), lambda c, i: (0, c, 0), pipeline_mode=pl.Buffered(1))],
        out_shape=[jax.ShapeDtypeStruct((1, 2, FF_P, D_MODEL), F32), jax.ShapeDtypeStruct((1, FF_P, D_MODEL), F32)],
        scratch_shapes=[], operands=(xh1, pd, dy, dug, duv, hm), semantics=("arbitrary", "arbitrary"))


def _adamw_math(w, g, m, v):
    nm = ADAM_B1 * m + (1.0 - ADAM_B1) * g
    nv = ADAM_B2 * v + (1.0 - ADAM_B2) * (g * g)
    m_hat = nm / (1.0 - ADAM_B1 ** ADAM_STEP)
    v_hat = nv / (1.0 - ADAM_B2 ** ADAM_STEP)
    return -ADAM_LR * (m_hat / (jnp.sqrt(v_hat) + ADAM_EPS) + ADAM_WD * w), nm, nv


def _adamw_layers(w, gs, m, v, name):
    shp = w.shape
    _, rows, cols = shp
    tr = _row_tile(rows, cols * 4, mult=8)
    nb = rows // tr

    def body(w_ref, g0_ref, g1_ref, m_ref, v_ref, g_ref, d_ref, nm_ref, nv_ref):
        g = jnp.where(pl.program_id(0) == 0, g0_ref[...], g1_ref[...])
        g_ref[...] = g
        d_ref[...], nm_ref[...], nv_ref[...] = _adamw_math(w_ref[...], g, m_ref[...], v_ref[...])

    stacked = pl.BlockSpec((tr, cols), lambda l, i: (l * nb + i, 0))
    single = pl.BlockSpec((tr, cols), lambda l, i: (i, 0))
    sh = jax.ShapeDtypeStruct((DEPTH * rows, cols), F32)
    flat = lambda a: a.reshape(DEPTH * rows, cols)
    outs = pl.pallas_call(body, name=name, grid=(DEPTH, nb), in_specs=[stacked, single, single, stacked, stacked],
                          out_specs=[stacked] * 4, out_shape=[sh] * 4,
                          compiler_params=_cparams(("arbitrary", "arbitrary")))(flat(w), gs[0], gs[1], flat(m), flat(v))
    return [o.reshape(shp) for o in outs]


def _adamw_small(ws, gs, ms, vs):
    n = len(ws)

    def body(*refs):
        w_refs, g_refs, m_refs, v_refs = refs[:n], refs[n:2 * n], refs[2 * n:3 * n], refs[3 * n:4 * n]
        d_refs, nm_refs, nv_refs = refs[4 * n:5 * n], refs[5 * n:6 * n], refs[6 * n:7 * n]
        for k in range(n):
            d_refs[k][...], nm_refs[k][...], nv_refs[k][...] = _adamw_math(w_refs[k][...], g_refs[k][...], m_refs[k][...],
                                                                             v_refs[k][...])

    shapes = [jax.ShapeDtypeStruct(w.shape, F32) for w in ws]
    outs = pl.pallas_call(body, name="adamw_small", out_shape=shapes * 3, compiler_params=_cparams())(*ws, *gs, *ms, *vs)
    return outs[:n], outs[n:2 * n], outs[2 * n:]


def _all_gather_chips(tensors, name):
    n = len(tensors)
    halves = [a.shape[1] // 2 for a in tensors]

    def body(*refs):
        x_refs, out_refs = refs[:n], refs[n:2 * n]
        send_sems, recv_sems, local_sems = refs[2 * n:]
        x, y, c, chips = _my_place()
        me, sibling, mej = (x, y, c), (x, y, 1 - c), 2 * x + y

        def rows(tn, px, py, pc):
            return out_refs[tn].at[:, 2 * px + py, pl.ds(pc * halves[tn], halves[tn]), :]

        def copy(tn, k, block, to, src=None):
            return pltpu.make_async_remote_copy(
                src_ref=rows(tn, *block) if src is None else src, dst_ref=rows(tn, *block),
                send_sem=send_sems.at[tn, k], recv_sem=recv_sems.at[tn, k], device_id=to, device_id_type=MESH)

        mine_src = [x_refs[tn].at[:, pl.ds(c * halves[tn], halves[tn]), :] for tn in range(n)]
        mine = [pltpu.make_async_copy(mine_src[tn], rows(tn, *me), local_sems.at[tn]) for tn in range(n)]
        first = []
        for j, chip in enumerate(chips):
            first += [copy(tn, 1 + j, me, (*chip, c), src=mine_src[tn]) for tn in range(n)]
        first += [copy(tn, 0, me, sibling, src=mine_src[tn]) for tn in range(n)]
        for cp in first + mine:
            cp.start()
        passed = []
        for j, chip in enumerate(chips):
            for tn in range(n):
                copy(tn, 1 + j, (*chip, c), me).wait_recv()
                fwd = copy(tn, 4 + j, (*chip, c), sibling)
                fwd.start()
                passed.append(fwd)
        for tn in range(n):
            copy(tn, 0, sibling, me).wait_recv()
            for j, chip in enumerate(chips):
                copy(tn, 4 + j, (*chip, 1 - c), me).wait_recv()
        for cp in first + passed:
            cp.wait_send()
        for cp in mine:
            cp.wait()

    return pl.pallas_call(
        body, name=name,
        out_shape=[jax.ShapeDtypeStruct((a.shape[0], N_CHIPS) + a.shape[1:], a.dtype) for a in tensors],
        in_specs=[ANY] * n, out_specs=[ANY] * n,
        scratch_shapes=[pltpu.SemaphoreType.DMA((n, 7)), pltpu.SemaphoreType.DMA((n, 7)), pltpu.SemaphoreType.DMA((n,))],
    )(*tensors)


def _swap_op(g5s):
    n = len(g5s)

    def make(ins, outs, scr):
        send_sems, recv_sems = scr
        x, y, c, _ = _my_place()

        def copies():
            return [pltpu.make_async_remote_copy(
                src_ref=ins[tn].at[:, :, 1 - c], dst_ref=outs[tn], send_sem=send_sems.at[tn], recv_sem=recv_sems.at[tn],
                device_id=(x, y, 1 - c), device_id_type=MESH) for tn in range(n)]

        def start():
            for cp in copies():
                cp.start()

        def wait():
            for cp in copies():
                cp.wait()

        return start, wait

    return _SideOp(g5s, [jax.ShapeDtypeStruct(g.shape[:2] + g.shape[3:], g.dtype) for g in g5s],
                   [pltpu.SemaphoreType.DMA((n,)), pltpu.SemaphoreType.DMA((n,))], make)


def _scatter_op(parts):
    n = len(parts)

    def make(ins, outs, scr):
        send_sems, recv_sems = scr
        x, y, c, chips = _my_place()

        def copies():
            return [pltpu.make_async_remote_copy(
                src_ref=ins[tn].at[:, 2 * px + py], dst_ref=outs[tn].at[:, k],
                send_sem=send_sems.at[tn, k], recv_sem=recv_sems.at[tn, k],
                device_id=(px, py, c), device_id_type=MESH) for k, (px, py) in enumerate(chips) for tn in range(n)]

        def start():
            for cp in copies():
                cp.start()

        def wait():
            for cp in copies():
                cp.wait()

        return start, wait

    return _SideOp(parts, [jax.ShapeDtypeStruct((p.shape[0], 3) + p.shape[2:], p.dtype) for p in parts],
                   [pltpu.SemaphoreType.DMA((n, 3)), pltpu.SemaphoreType.DMA((n, 3))], make)


def _sgather_op(fs):
    n = len(fs)

    def make(ins, outs, scr):
        send_sems, recv_sems = scr
        x, y, c, _ = _my_place()

        def copy(tn, dst_half):
            return pltpu.make_async_remote_copy(
                src_ref=outs[tn].at[:, c], dst_ref=outs[tn].at[:, dst_half], send_sem=send_sems.at[tn],
                recv_sem=recv_sems.at[tn], device_id=(x, y, 1 - c), device_id_type=MESH)

        def start():
            for tn in range(n):
                copy(tn, c).start()

        def wait():
            for tn in range(n):
                copy(tn, 1 - c).wait_recv()
                copy(tn, c).wait_send()

        return start, wait

    return _SideOp(fs, [jax.ShapeDtypeStruct(f.shape, f.dtype) for f in fs],
                   [pltpu.SemaphoreType.DMA((n,)), pltpu.SemaphoreType.DMA((n,))], make, aliases={tn: tn for tn in range(n)})


def _run_side_ops(ops, name):
    return _call_with_side(lambda: None, ops, name=name, grid=(1,), in_specs=[], out_specs=[], out_shape=[],
                           scratch_shapes=[], operands=(), semantics=("arbitrary",))[1]


def _gather_devices_op(xs):
    def make(ins, outs, scr):
        send_sems, recv_sems, local_sem = scr
        x, y, c, chips = _my_place()
        peers = [(x, y, 1 - c)] + [(px, py, pc) for (px, py) in chips for pc in (c, 1 - c)]
        me = 4 * x + 2 * y + c

        def copy(k, slot):
            return pltpu.make_async_remote_copy(
                src_ref=ins[0], dst_ref=outs[0].at[slot], send_sem=send_sems.at[k], recv_sem=recv_sems.at[k],
                device_id=peers[k], device_id_type=MESH)

        def local():
            return pltpu.make_async_copy(ins[0], outs[0].at[me], local_sem)

        def start():
            for k in range(7):
                copy(k, me).start()
            local().start()

        def wait():
            for k, (px, py, pc) in enumerate(peers):
                copy(k, 4 * px + 2 * py + pc).wait_recv()
                copy(k, me).wait_send()
            local().wait()

        return start, wait

    return _SideOp([xs], [jax.ShapeDtypeStruct((8,) + xs.shape, xs.dtype)],
                   [pltpu.SemaphoreType.DMA((7,)), pltpu.SemaphoreType.DMA((7,)), pltpu.SemaphoreType.DMA], make)


def _add_halves(gs, recvs, place):
    n = len(gs)

    def body(place_ref, *refs):
        g_refs, r_refs, o_refs = refs[:n], refs[n:2 * n], refs[2 * n:]
        for tn in range(n):
            o_refs[tn][...] = (g_refs[tn][...] + r_refs[tn][...]).astype(BF16)

    def gspec(g):
        return pl.BlockSpec((None, None, None) + g.shape[3:], lambda l, j, p: (l, j, p[1], 0, 0))

    def rspec(r):
        return pl.BlockSpec((None, None) + r.shape[2:], lambda l, j, p: (l, j, 0, 0))

    grid_spec = pltpu.PrefetchScalarGridSpec(
        num_scalar_prefetch=1, grid=(gs[0].shape[0], N_CHIPS),
        in_specs=[gspec(g) for g in gs] + [rspec(r) for r in recvs], out_specs=[rspec(r) for r in recvs])
    return pl.pallas_call(body, name="rs_add", grid_spec=grid_spec,
                          out_shape=[jax.ShapeDtypeStruct(r.shape, BF16) for r in recvs],
                          compiler_params=_cparams(("arbitrary", "arbitrary")))(place, *gs, *recvs)


def _sum_slots(parts, slots, place):
    n = len(parts)

    def body(place_ref, *refs):
        p_refs, s_refs, o_refs = refs[:n], refs[n:2 * n], refs[2 * n:]
        for tn in range(n):
            acc = p_refs[tn][...].astype(F32)
            for k in range(3):
                acc = acc + s_refs[tn][k].astype(F32)
            o_refs[tn][...] = acc

    def pspec(p):
        return pl.BlockSpec((None, None) + p.shape[2:], lambda l, pl_: (l, pl_[0], 0, 0))

    def sspec(sl):
        return pl.BlockSpec((None,) + sl.shape[1:], lambda l, pl_: (l, 0, 0, 0))

    def ospec(p):
        return pl.BlockSpec((None, None) + p.shape[2:], lambda l, pl_: (l, pl_[1], 0, 0))

    grid_spec = pltpu.PrefetchScalarGridSpec(
        num_scalar_prefetch=1, grid=(parts[0].shape[0],),
        in_specs=[pspec(p) for p in parts] + [sspec(sl) for sl in slots], out_specs=[ospec(p) for p in parts])
    return pl.pallas_call(body, name="rs_sum", grid_spec=grid_spec,
                          out_shape=[jax.ShapeDtypeStruct((p.shape[0], 2) + p.shape[2:], F32) for p in parts],
                          compiler_params=_cparams(("arbitrary",)))(place, *parts, *slots)


def _sum_devices(gathered, m_per):
    def body(g_ref, o_ref):
        acc = g_ref[0:m_per, :]
        for d in range(1, 8):
            acc = acc + g_ref[d * m_per:(d + 1) * m_per, :]
        o_ref[...] = acc

    return pl.pallas_call(body, name="small_sum", out_shape=jax.ShapeDtypeStruct((m_per, LANE), F32),
                          compiler_params=_cparams())(gathered)


def _pad_ff_cols(a):
    lead = a.shape[:-1]
    n = a.shape[-1] // FF_Q
    a = a.reshape(*lead, n, FF_Q)
    a = jnp.pad(a, [(0, 0)] * len(lead) + [(0, 0), (0, FF_QP - FF_Q)])
    return a.reshape(*lead, n * FF_QP)


def _unpad_ff_cols(a):
    lead = a.shape[:-1]
    n = a.shape[-1] // FF_QP
    return a.reshape(*lead, n, FF_QP)[..., :FF_Q].reshape(*lead, n * FF_Q)


def _pack_small(parts):
    flat = jnp.concatenate([p.reshape(-1) for p in parts])
    return flat.reshape(-1, LANE)


SMALL_SHAPES = [("conv_a_w", (CONV_K, CONV_W)), ("conv_a_b", (CONV_W,)), ("ln_a_g", (CONV_W,)), ("ln_a_b", (CONV_W,)),
                ("ln_v_g", (GMLP_W,)), ("ln_v_b", (GMLP_W,)), ("w_s", (6, CHUNK, CHUNK)), ("b_s", (6, CHUNK)),
                ("ln1_g", (D_MODEL,)), ("ln1_b", (D_MODEL,)), ("conv_f_w", (FFN_CONV_K, D_FF)), ("conv_f_b", (D_FF,)),
                ("ln2_g", (D_MODEL,)), ("ln2_b", (D_MODEL,))]


def _unpack_small(flat2d):
    flat = flat2d.reshape(DEPTH, -1)
    out, o = {}, 0
    for name, shp in SMALL_SHAPES:
        n = 1
        for d in shp:
            n *= d
        out[name] = flat[:, o:o + n].reshape((DEPTH,) + shp)
        o += n
    return out


def _rows8(rows):
    blk = jnp.stack(rows, axis=1)
    return jnp.pad(blk, ((0, 0), (0, 8 - len(rows)), (0, 0)))


def kernel(x, mem, w_in, conv_a_w, conv_a_b, ln_a_g, ln_a_b, ln_v_g, ln_v_b, w_s, b_s, w_mk, w_mv, w_out, ln1_g, ln1_b, w_up, conv_f_w, conv_f_b, w_down, ln2_g, ln2_b, loss_target, m_w_in, m_conv_a_w, m_conv_a_b, m_ln_a_g, m_ln_a_b, m_ln_v_g, m_ln_v_b, m_w_s, m_b_s, m_w_mk, m_w_mv, m_w_out, m_ln1_g, m_ln1_b, m_w_up, m_conv_f_w, m_conv_f_b, m_w_down, m_ln2_g, m_ln2_b, v_w_in, v_conv_a_w, v_conv_a_b, v_ln_a_g, v_ln_a_b, v_ln_v_g, v_ln_v_b, v_w_s, v_b_s, v_w_mk, v_w_mv, v_w_out, v_ln1_g, v_ln1_b, v_w_up, v_conv_f_w, v_conv_f_b, v_w_down, v_ln2_g, v_ln2_b):
    seq = x.shape[1]
    t_fwd = min(512, seq)
    t_bwd = min(256, seq)
    t_wg = min(1024, seq)
    chip = 2 * lax.axis_index("x") + lax.axis_index("y")
    core = lax.axis_index("c")
    place = jnp.stack([chip, core]).astype(jnp.int32)
    x0 = x[0]
    mem0 = mem[0]
    target = loss_target[0]

    sh_in = w_in.transpose(0, 2, 1).astype(BF16)
    sh_mk, sh_mv, sh_out = w_mk.astype(BF16), w_mv.astype(BF16), w_out.astype(BF16)
    sh_up = _pad_ff_cols(w_up).transpose(0, 2, 1).astype(BF16)
    sh_dn = jnp.pad(w_down, ((0, 0), (0, FF_QP - FF_Q), (0, 0))).astype(BF16)

    def mixer_weights(g_in, g_mk, g_mv, g_out):
        return dict(win_t=g_in.reshape(1, IN_W, D_MODEL), wmk=g_mk.reshape(1, D_MODEL, XATTN_W),
                    wmv=g_mv.reshape(1, D_MODEL, XATTN_W), wout=g_out.reshape(1, D_MODEL, D_MODEL))

    def ffn_weights(g_up, g_dn):
        return dict(wup_t=g_up.reshape(1, 2, FF_P, D_MODEL), wdown=g_dn.reshape(1, FF_P, D_MODEL))

    n_ca = conv_a_w.size
    small_w = _pack_small([conv_a_w, conv_f_w, jnp.zeros((2 * 80 * LANE - n_ca - conv_f_w.size,), F32)])[None]
    *g_mixer0, small_g = _all_gather_chips([sh_in[:1], sh_mk[:1], sh_mv[:1], sh_out[:1], small_w], "ag_mixer0")
    wts = [mixer_weights(*g_mixer0), None]
    small_g = small_g.reshape(N_CHIPS, -1)
    conv_a_full = small_g[:, :n_ca].reshape(N_CHIPS, DEPTH, CONV_K, CONV_W // 4).transpose(1, 2, 0, 3).reshape(DEPTH, CONV_K, CONV_W)
    conv_f_full = small_g[:, n_ca:n_ca + conv_f_w.size].reshape(N_CHIPS, DEPTH, FFN_CONV_K, FF_Q).transpose(1, 2, 0, 3).reshape(DEPTH, FFN_CONV_K, D_FF)

    tril = jnp.tril(jnp.ones((CHUNK, CHUNK), dtype=bool))
    ws_m = jnp.where(tril, w_s, 0.0)
    wst = ws_m.reshape(DEPTH, 3, 2 * CHUNK, CHUNK).astype(BF16)
    wstt = ws_m.transpose(0, 1, 3, 2).reshape(DEPTH, 3, 2 * CHUNK, CHUNK).astype(BF16)
    bst = jnp.repeat(b_s.transpose(0, 2, 1), HEAD_DIM, axis=2)
    conv_w = jnp.pad(conv_a_full, ((0, 0), (0, CONV_HALO - CONV_K), (0, 0)))
    pa = _rows8([conv_a_b, ln_a_g, ln_a_b, ln_v_g, ln_v_b])
    gin = jnp.concatenate([jnp.ones((1, D_MODEL), F32), ln2_g[:DEPTH - 1]], axis=0)
    bin_ = jnp.concatenate([jnp.zeros((1, D_MODEL), F32), ln2_b[:DEPTH - 1]], axis=0)
    pd = _rows8([gin, bin_, ln1_g, ln1_b, ln2_g, ln2_b])
    pf = jnp.concatenate([_pad_ff_cols(conv_f_full), _pad_ff_cols(conv_f_b)[:, None, :],
                          jnp.zeros((DEPTH, 8 - FFN_CONV_K - 1, FF_P), F32)], axis=1)

    acts = []
    xin = x0
    for l in range(DEPTH):
        w = wts[l]
        kt_all, k_all, v_all, vt_all = _kv_fwd(mem0, w["wmk"], w["wmv"])
        ops = [_gather_chips_op([sh_up[0], sh_dn[0]])] if l == 0 else None
        (xh1, rstd1, h, ac, cat), side = _mixer_fwd(xin, pd, w["win_t"], conv_w, pa, wst, bst, kt_all, v_all, w["wout"],
                                                    l, t_fwd, ops)
        if l == 0:
            w.update(ffn_weights(*side[0]))
        ops = [_gather_chips_op([sh_in[1], sh_mk[1], sh_mv[1], sh_out[1], sh_up[1], sh_dn[1]])] if l == 0 else None
        (xh2, rstd2, upg, upv), side = _ffn_fwd(xh1, pd, w["wup_t"], pf, w["wdown"], l, t_fwd, ops)
        if l == 0:
            wts[1] = {**mixer_weights(*side[0][:4]), **ffn_weights(*side[0][4:])}
        acts.append(dict(xin=xin, kt_all=kt_all, k_all=k_all, vt_all=vt_all, xh1=xh1, rstd1=rstd1, h=h, ac=ac, cat=cat,
                         xh2=xh2, rstd2=rstd2, upg=upg, upv=upv))
        xin = xh2

    assert DEPTH == 2

    def halves_view(gs):
        return [g.reshape(1, N_CHIPS, 2, g.shape[1] // (2 * N_CHIPS), g.shape[2]) for g in gs]

    small = [None] * DEPTH
    small_packed = [None] * DEPTH
    small_gathered = [None] * DEPTH
    red_layers = [None] * DEPTH
    gz = target
    loss_sum = None
    g5_prev = None
    for l in reversed(range(DEPTH)):
        a, w = acts[l], wts[l]
        last = l == DEPTH - 1
        (dx1, dy, dug, duv, hm, vd2, vf), side = _ffn_bwd_d(
            gz, a["xh2"], a["rstd2"], a["upg"], a["upv"], pd, pf, w["wdown"], w["wup_t"], l, t_bwd, last,
            [_swap_op(g5_prev), _gather_devices_op(small_packed[l + 1])] if g5_prev else None)
        if last:
            loss_sum = vd2[VD_LOSS, 0]
        if g5_prev:
            small_gathered[l + 1] = side[1][0]
        parts_prev = _add_halves(g5_prev, side[0], place) if g5_prev else None
        (gw_up_t, gw_down), side = _ffn_bwd_w(a["xh1"], pd, dy, dug, duv, hm, l, t_wg,
                                              [_scatter_op(parts_prev)] if g5_prev else None)
        halves_prev = _sum_slots(parts_prev, side[0], place) if g5_prev else None
        g5_ffn = halves_view([gw_up_t.reshape(1, 2 * FF_P, D_MODEL), gw_down])
        ops = ([_sgather_op(halves_prev)] if g5_prev else []) + ([_swap_op(g5_ffn)] if l == 0 else [])
        (dx0, dh, dmix, vd1, va, dcw, dws, dbs, dkt, dv), side = _mixer_bwd_d(
            dx1, a["xh1"], a["rstd1"], a["h"], a["ac"], pd, conv_w, pa, wst, wstt, bst,
            a["kt_all"], a["k_all"], a["vt_all"], w["wout"], w["win_t"], l, t_fwd, ops)
        if g5_prev:
            red_layers[l + 1] = side[0]
        parts_ffn = _add_halves(g5_ffn, side[-1], place) if l == 0 else None
        dws6 = jnp.where(tril, dws.reshape(6, CHUNK, CHUNK), 0.0)
        small[l] = [dcw[:CONV_K], va[VA_CONV_B], va[VA_LNA_G], va[VA_LNA_B], va[VA_LNV_G], va[VA_LNV_B], dws6,
                    dbs[:, :6].T, vd1[VD_LN_G], vd1[VD_LN_B],
                    _unpad_ff_cols(vf[VF_W0:VF_W0 + FFN_CONV_K]), _unpad_ff_cols(vf[VF_B]),
                    vd2[VD_LN_G], vd2[VD_LN_B]]
        small_packed[l] = _pack_small(small[l])
        ops = [_scatter_op(parts_ffn), _gather_devices_op(small_packed[l])] if l == 0 else None
        (gw_in_t, gw_out), side = _mixer_bwd_w(a["xin"], pd, dh, a["cat"], dmix, l, t_wg, ops)
        gw_mk, gw_mv = _kv_bwd(mem0, dkt, dv)
        g5_mix = halves_view([gw_in_t, gw_mk, gw_mv, gw_out])
        if l == 0:
            small_gathered[l] = side[1][0]
            halves_ffn = _sum_slots(parts_ffn, side[0], place)
            red_ffn, recv_mix = _run_side_ops([_sgather_op(halves_ffn), _swap_op(g5_mix)], "rs_tail_swap")
            parts_mix = _add_halves(g5_mix, recv_mix, place)
            halves_mix = _sum_slots(parts_mix, _run_side_ops([_scatter_op(parts_mix)], "rs_tail_chips")[0], place)
            red_mix = _run_side_ops([_sgather_op(halves_mix)], "rs_tail_gather")[0]
            red_layers[0] = red_mix + red_ffn
        else:
            g5_prev = g5_mix + g5_ffn
        gz = dx0
    grad_x = gz[None]

    def shard_grads(red):
        r = [f.reshape(-1, f.shape[-1]) for f in red]
        return dict(w_in=r[0].T, w_mk=r[1], w_mv=r[2], w_out=r[3], w_up=_unpad_ff_cols(r[4].T), w_down=r[5][:FF_Q])

    big_grads = [shard_grads(red_layers[l]) for l in range(DEPTH)]

    m_small = small_gathered[0].shape[1]
    small_red = jnp.concatenate([_sum_devices(g.reshape(8 * m_small, LANE), m_small) for g in small_gathered], axis=0)
    sg = _unpack_small(small_red)
    g_conv_a_w = lax.dynamic_slice_in_dim(sg["conv_a_w"], chip * (CONV_W // 4), CONV_W // 4, axis=2)
    g_conv_f_w = lax.dynamic_slice_in_dim(sg["conv_f_w"], chip * FF_Q, FF_Q, axis=2)

    loss = 0.5 / D_MODEL * lax.psum(loss_sum, ("x", "y", "c"))

    grads = dict(conv_a_w=g_conv_a_w, conv_a_b=sg["conv_a_b"], ln_a_g=sg["ln_a_g"], ln_a_b=sg["ln_a_b"],
                 ln_v_g=sg["ln_v_g"], ln_v_b=sg["ln_v_b"], w_s=sg["w_s"], b_s=sg["b_s"], ln1_g=sg["ln1_g"], ln1_b=sg["ln1_b"],
                 conv_f_w=g_conv_f_w, conv_f_b=sg["conv_f_b"], ln2_g=sg["ln2_g"], ln2_b=sg["ln2_b"])
    weights = dict(w_in=w_in, conv_a_w=conv_a_w, conv_a_b=conv_a_b, ln_a_g=ln_a_g, ln_a_b=ln_a_b, ln_v_g=ln_v_g,
                   ln_v_b=ln_v_b, w_s=w_s, b_s=b_s, w_mk=w_mk, w_mv=w_mv, w_out=w_out, ln1_g=ln1_g, ln1_b=ln1_b,
                   w_up=w_up, conv_f_w=conv_f_w, conv_f_b=conv_f_b, w_down=w_down, ln2_g=ln2_g, ln2_b=ln2_b)
    mom_m = dict(w_in=m_w_in, conv_a_w=m_conv_a_w, conv_a_b=m_conv_a_b, ln_a_g=m_ln_a_g, ln_a_b=m_ln_a_b, ln_v_g=m_ln_v_g,
                 ln_v_b=m_ln_v_b, w_s=m_w_s, b_s=m_b_s, w_mk=m_w_mk, w_mv=m_w_mv, w_out=m_w_out, ln1_g=m_ln1_g,
                 ln1_b=m_ln1_b, w_up=m_w_up, conv_f_w=m_conv_f_w, conv_f_b=m_conv_f_b, w_down=m_w_down, ln2_g=m_ln2_g,
                 ln2_b=m_ln2_b)
    mom_v = dict(w_in=v_w_in, conv_a_w=v_conv_a_w, conv_a_b=v_conv_a_b, ln_a_g=v_ln_a_g, ln_a_b=v_ln_a_b, ln_v_g=v_ln_v_g,
                 ln_v_b=v_ln_v_b, w_s=v_w_s, b_s=v_b_s, w_mk=v_w_mk, w_mv=v_w_mv, w_out=v_w_out, ln1_g=v_ln1_g,
                 ln1_b=v_ln1_b, w_up=v_w_up, conv_f_w=v_conv_f_w, conv_f_b=v_conv_f_b, w_down=v_w_down, ln2_g=v_ln2_g,
                 ln2_b=v_ln2_b)
    names = list(weights)
    big_names = ["w_in", "w_mk", "w_mv", "w_out", "w_up", "w_down"]
    delta, new_m, new_v = {}, {}, {}
    for n in big_names:
        grads[n], delta[n], new_m[n], new_v[n] = _adamw_layers(weights[n], [big_grads[l][n] for l in range(DEPTH)],
                                                               mom_m[n], mom_v[n], "adamw_" + n)
    small_names = [n for n in names if n not in big_names]
    ds, nms, nvs = _adamw_small([weights[n] for n in small_names], [grads[n] for n in small_names],
                                [mom_m[n] for n in small_names], [mom_v[n] for n in small_names])
    for n, d, nm, nv in zip(small_names, ds, nms, nvs):
        delta[n], new_m[n], new_v[n] = d, nm, nv

    return (loss, grad_x, *[grads[n] for n in names], *[delta[n] for n in names],
            *[new_m[n] for n in names], *[new_v[n] for n in names])
```

```python
import jax
import jax.numpy as jnp
from jax import lax
from jax.experimental import pallas as pl
from jax.experimental.pallas import tpu as pltpu

F32 = jnp.float32
BF16 = jnp.bfloat16

D_MODEL = 1024
DEPTH = 2
CONV_W = 384
GMLP_W = 384
XATTN_W = 256
XATTN_HEADS = 4
HEAD_DIM = 64
IN_W = 1792
CONV_K = 31
CHUNK = 128
N_MEM = 256
D_FF = 2752
FFN_CONV_K = 3
ALPHA = (2.0 * DEPTH) ** 0.25
LN_EPS = 1e-5
ATT_SCALE = 1.0 / 8.0
ADAM_LR, ADAM_B1, ADAM_B2, ADAM_EPS, ADAM_WD, ADAM_STEP = 0.001, 0.9, 0.999, 1e-08, 0.01, 10

N_CHIPS = 4
FF_Q = D_FF // N_CHIPS
FF_QP = 704
FF_H = 2 * FF_QP
FF_P = 4 * FF_QP
LANE = 128
CONV_HALO = 32
FFN_HALO = 8
BF16_ROWS = 16
VMEM_LIMIT = 60 * 1024 * 1024

MESH = pl.DeviceIdType.MESH
ANY = pl.BlockSpec(memory_space=pl.ANY)


def _cparams(sem=None, vmem=VMEM_LIMIT):
    kw = {"vmem_limit_bytes": vmem}
    if sem is not None:
        kw["dimension_semantics"] = sem
    return pltpu.CompilerParams(**kw)


def _row_tile(rows, row_bytes, limit=2 << 20, mult=BF16_ROWS):
    if rows * row_bytes <= limit:
        return rows
    best = None
    for cand in range(mult, rows, mult):
        if rows % cand == 0 and cand * row_bytes <= limit:
            best = cand
    assert best is not None, (rows, row_bytes)
    return best


def _const_spec(shape):
    nd = len(shape)
    return pl.BlockSpec(shape, lambda *_: (0,) * nd)


def _layer_spec(shape, *lead, resident=False):
    nd = len(shape)
    kw = {"pipeline_mode": pl.Buffered(1)} if resident else {}
    return pl.BlockSpec((None,) * len(lead) + tuple(shape), lambda *_: tuple(lead) + (0,) * nd, **kw)


def _sigmoid(x):
    return jax.nn.sigmoid(x)


def _gelu(x):
    return jax.nn.gelu(x)


def _gelu_and_grad(x):
    c = 0.7978845608028654
    a = 0.044715
    x2 = x * x
    t = jnp.tanh(c * (x + a * x * x2))
    h = 0.5 * (1.0 + t)
    return x * h, h + 0.5 * x * (1.0 - t * t) * c * (1.0 + 3.0 * a * x2)


def _ln_fwd(z):
    mu = jnp.mean(z, axis=-1, keepdims=True)
    zc = z - mu
    var = jnp.mean(zc * zc, axis=-1, keepdims=True)
    rstd = lax.rsqrt(var + LN_EPS)
    return zc * rstd, rstd


def _ln_bwd(dxh, xh, rstd):
    m1 = jnp.mean(dxh, axis=-1, keepdims=True)
    m2 = jnp.mean(dxh * xh, axis=-1, keepdims=True)
    return rstd * (dxh - m1 - xh * m2)


def _colsum(a):
    return jnp.sum(a, axis=0, keepdims=True)


def _dot(a, b):
    return jnp.dot(a, b, preferred_element_type=F32)


def _dot_tn(a, b):
    return lax.dot_general(a, b, (((0,), (0,)), ((), ())), preferred_element_type=F32)


def _dot_nt(a, b):
    return lax.dot_general(a, b, (((1,), (1,)), ((), ())), preferred_element_type=F32)


def _softmax_heads(sc):
    ps = []
    for hd in range(XATTN_HEADS):
        s = sc[:, hd * N_MEM:(hd + 1) * N_MEM]
        e = jnp.exp(s - jnp.max(s, axis=-1, keepdims=True))
        ps.append(e / jnp.sum(e, axis=-1, keepdims=True))
    return jnp.concatenate(ps, axis=1)


def _lane_lo(shape):
    return (lax.broadcasted_iota(jnp.int32, shape, len(shape) - 1) % LANE) < HEAD_DIM


def _spatial_mix(vnb, wst_ref, bst_ref, mix_ref, t):
    lo = _lane_lo((CHUNK, LANE))
    for n in range(t // CHUNK):
        rows = slice(n * CHUNK, (n + 1) * CHUNK)
        for j in range(GMLP_W // LANE):
            cols = slice(j * LANE, (j + 1) * LANE)
            r = _dot(wst_ref[j], vnb[rows, cols])
            mix_ref[rows, cols] = jnp.where(lo, r[:CHUNK], r[CHUNK:]) + bst_ref[:, cols]


def _kv_fwd(mem, w_mk, w_mv):
    def body(mem_ref, wk_ref, wv_ref, kt_ref, k_ref, v_ref, vt_ref):
        mb = mem_ref[...].astype(BF16)
        k = _dot(mb, wk_ref[...])
        v = _dot(mb, wv_ref[...])
        col = lax.broadcasted_iota(jnp.int32, (N_MEM, XATTN_W), 1) // HEAD_DIM
        ks = [jnp.where(col == hd, k, 0.0) for hd in range(XATTN_HEADS)]
        vs = [jnp.where(col == hd, v, 0.0) for hd in range(XATTN_HEADS)]
        k_ref[...] = jnp.concatenate(ks, axis=0).astype(BF16)
        v_ref[...] = jnp.concatenate(vs, axis=0).astype(BF16)
        kt_ref[...] = jnp.concatenate([x.T for x in ks], axis=1).astype(BF16)
        vt_ref[...] = jnp.concatenate([x.T for x in vs], axis=1).astype(BF16)

    wide = jax.ShapeDtypeStruct((XATTN_W, XATTN_HEADS * N_MEM), BF16)
    tall = jax.ShapeDtypeStruct((XATTN_HEADS * N_MEM, XATTN_W), BF16)
    wspec = _layer_spec((D_MODEL, XATTN_W), 0)
    return pl.pallas_call(body, name="kv_fwd", grid=(1,),
                          in_specs=[_const_spec((N_MEM, D_MODEL)), wspec, wspec],
                          out_specs=[_const_spec(wide.shape), _const_spec(tall.shape), _const_spec(tall.shape),
                                     _const_spec(wide.shape)],
                          out_shape=(wide, tall, tall, wide), compiler_params=_cparams(("arbitrary",)))(mem, w_mk, w_mv)


def _kv_bwd(mem, dkt_all, dv_all):
    def body(mem_ref, dkt_ref, dv_ref, gk_ref, gv_ref):
        col = lax.broadcasted_iota(jnp.int32, (N_MEM, XATTN_W), 1) // HEAD_DIM
        dk = jnp.zeros((N_MEM, XATTN_W), F32)
        dv = jnp.zeros((N_MEM, XATTN_W), F32)
        for hd in range(XATTN_HEADS):
            dk = dk + jnp.where(col == hd, dkt_ref[:, hd * N_MEM:(hd + 1) * N_MEM].T, 0.0)
            dv = dv + jnp.where(col == hd, dv_ref[hd * N_MEM:(hd + 1) * N_MEM, :], 0.0)
        mb = mem_ref[...].astype(BF16)
        gk_ref[0] = _dot_tn(mb, dk.astype(BF16))
        gv_ref[0] = _dot_tn(mb, dv.astype(BF16))

    out = jax.ShapeDtypeStruct((1, D_MODEL, XATTN_W), F32)
    return pl.pallas_call(body, name="kv_bwd", out_shape=(out, out), compiler_params=_cparams())(mem, dkt_all, dv_all)


def _my_place():
    x, y, c = lax.axis_index("x"), lax.axis_index("y"), lax.axis_index("c")
    chips = [(1 - x, y), (x, 1 - y), (1 - x, 1 - y)]
    return x, y, c, chips


class _SideOp:
    def __init__(self, ins, out_shapes, scratch, make, aliases=None):
        self.ins, self.out_shapes, self.scratch, self.make, self.aliases = list(ins), list(out_shapes), list(scratch), make, dict(aliases or {})


def _call_with_side(body, side_ops, *, name, grid, in_specs, out_specs, out_shape, scratch_shapes, operands, semantics):
    side_ops = list(side_ops or ())
    n_in, n_out, n_scr = len(in_specs), len(out_specs), len(scratch_shapes)
    s_ins = [a for op in side_ops for a in op.ins]
    s_outs = [o for op in side_ops for o in op.out_shapes]
    s_scr = [x for op in side_ops for x in op.scratch]
    aliases, oi, oo = {}, 0, 0
    for op in side_ops:
        for a, b in op.aliases.items():
            aliases[n_in + oi + a] = n_out + oo + b
        oi, oo = oi + len(op.ins), oo + len(op.out_shapes)

    def wrapped(*refs):
        ins, sins = refs[:n_in], refs[n_in:n_in + len(s_ins)]
        base = n_in + len(s_ins)
        outs, souts = refs[base:base + n_out], refs[base + n_out:base + n_out + len(s_outs)]
        base += n_out + len(s_outs)
        scr, sscr = refs[base:base + n_scr], refs[base + n_scr:]
        if side_ops:
            first = pl.program_id(0) == 0
            last = pl.program_id(0) == grid[0] - 1
            for d in range(1, len(grid)):
                first = jnp.logical_and(first, pl.program_id(d) == 0)
                last = jnp.logical_and(last, pl.program_id(d) == grid[d] - 1)
            hooks, a, b, c = [], 0, 0, 0
            for op in side_ops:
                hooks.append(op.make(sins[a:a + len(op.ins)], souts[b:b + len(op.out_shapes)], sscr[c:c + len(op.scratch)]))
                a, b, c = a + len(op.ins), b + len(op.out_shapes), c + len(op.scratch)

            @pl.when(first)
            def _():
                for start, _w in hooks:
                    start()

        body(*ins, *outs, *scr)
        if side_ops:
            @pl.when(last)
            def _():
                for _s, wait in hooks:
                    wait()

    res = pl.pallas_call(
        wrapped, name=name, grid=grid, in_specs=list(in_specs) + [ANY] * len(s_ins),
        out_specs=list(out_specs) + [ANY] * len(s_outs), out_shape=list(out_shape) + s_outs,
        scratch_shapes=list(scratch_shapes) + s_scr, input_output_aliases=aliases,
        compiler_params=_cparams(semantics),
    )(*operands, *s_ins)
    side_res, k = [], n_out
    for op in side_ops:
        side_res.append(list(res[k:k + len(op.out_shapes)]))
        k += len(op.out_shapes)
    return list(res[:n_out]), side_res


def _gather_chips_op(shards):
    n = len(shards)

    def make(ins, outs, scr):
        send_sems, recv_sems, local_sems = scr
        x, y, c, chips = _my_place()
        mej = 2 * x + y

        def remote(tn, k, slot):
            px, py = chips[k]
            return pltpu.make_async_remote_copy(
                src_ref=ins[tn], dst_ref=outs[tn].at[slot], send_sem=send_sems.at[tn, k],
                recv_sem=recv_sems.at[tn, k], device_id=(px, py, c), device_id_type=MESH)

        def local(tn):
            return pltpu.make_async_copy(ins[tn], outs[tn].at[mej], local_sems.at[tn])

        def start():
            for k in range(3):
                for tn in range(n):
                    remote(tn, k, mej).start()
            for tn in range(n):
                local(tn).start()

        def wait():
            for k, (px, py) in enumerate(chips):
                for tn in range(n):
                    remote(tn, k, 2 * px + py).wait_recv()
                    remote(tn, k, mej).wait_send()
            for tn in range(n):
                local(tn).wait()

        return start, wait

    return _SideOp(shards, [jax.ShapeDtypeStruct((N_CHIPS,) + a.shape, a.dtype) for a in shards],
                   [pltpu.SemaphoreType.DMA((n, 3)), pltpu.SemaphoreType.DMA((n, 3)), pltpu.SemaphoreType.DMA((n,))], make)


PA_CONV_B, PA_LNA_G, PA_LNA_B, PA_LNV_G, PA_LNV_B = 0, 1, 2, 3, 4
PD_GIN, PD_BIN, PD_G1, PD_B1, PD_G2, PD_B2 = 0, 1, 2, 3, 4, 5


def _row(ref, r):
    return ref[r:r + 1, :]


def _mixer_fwd(xin, pd, win_t, conv_w, pa, wst, bst, kt_all, v_all, w_out, l, t, side_ops=None):
    s = xin.shape[0]
    nt = s // t

    def body(x_ref, pd_ref, wint_ref, cw_ref, pa_ref, wst_ref, bst_ref, kt_ref, v_ref, wout_ref,
             xh_ref, rstd_ref, h_ref, ac_ref, cat_ref, mixed_ref, p_ref, cbuf, zbuf):
        i = pl.program_id(0)
        x = x_ref[...] * _row(pd_ref, PD_GIN) + _row(pd_ref, PD_BIN)
        h = _dot_nt(x.astype(BF16), wint_ref[...])
        h_ref[...] = h
        a1, a2 = h[:, 0:CONV_W], h[:, CONV_W:2 * CONV_W]
        hu, hv = h[:, 2 * CONV_W:2 * CONV_W + GMLP_W], h[:, 2 * CONV_W + GMLP_W:2 * CONV_W + 2 * GMLP_W]
        q = h[:, IN_W - XATTN_W:]

        @pl.when(i == 0)
        def _():
            cbuf[0:CONV_HALO, :] = jnp.zeros((CONV_HALO, CONV_W), F32)

        cbuf[CONV_HALO:CONV_HALO + t, :] = a1 * _sigmoid(a2)
        ac = jnp.zeros((t, CONV_W), F32) + _row(pa_ref, PA_CONV_B)
        for r in range(8):
            zr = jnp.zeros((t + 8, CONV_W), F32)
            for a in range(4):
                o = 8 * a + r
                if o < CONV_K:
                    k = CONV_K - 1 - o
                    zr = zr + cbuf[CONV_HALO - 8 - 8 * a:CONV_HALO - 8 - 8 * a + t + 8, :] * cw_ref[k:k + 1, :]
            if r == 0:
                ac = ac + zr[8:, :]
            else:
                zbuf[...] = zr
                ac = ac + zbuf[8 - r:8 - r + t, :]
        ac_ref[...] = ac
        cbuf[0:CONV_HALO, :] = cbuf[t:t + CONV_HALO, :]
        xh_a, _ = _ln_fwd(ac)
        an = xh_a * _row(pa_ref, PA_LNA_G) + _row(pa_ref, PA_LNA_B)
        a = an * _sigmoid(an)

        u = _gelu(hu)
        xh_v, _ = _ln_fwd(_gelu(hv))
        vn = xh_v * _row(pa_ref, PA_LNV_G) + _row(pa_ref, PA_LNV_B)
        _spatial_mix(vn.astype(BF16), wst_ref, bst_ref, mixed_ref, t)
        g = u * mixed_ref[...]

        p = _softmax_heads(_dot(q.astype(BF16), kt_ref[...]) * ATT_SCALE)
        p_ref[...] = p
        o = _dot(p.astype(BF16), v_ref[...])

        cat = jnp.concatenate([a, g, o], axis=1).astype(BF16)
        cat_ref[...] = cat
        z = ALPHA * x + _dot(cat, wout_ref[...])
        xh, rstd = _ln_fwd(z)
        xh_ref[...] = xh
        rstd_ref[...] = rstd

    tok = lambda w: pl.BlockSpec((t, w), lambda i: (i, 0))
    return _call_with_side(
        body, side_ops, name="mixer_fwd", grid=(nt,),
        in_specs=[tok(D_MODEL), _layer_spec((8, D_MODEL), l), _layer_spec((IN_W, D_MODEL), 0, resident=True),
                  _layer_spec((CONV_HALO, CONV_W), l), _layer_spec((8, CONV_W), l),
                  _layer_spec((3, 2 * CHUNK, CHUNK), l), _layer_spec((CHUNK, GMLP_W), l),
                  _const_spec((XATTN_W, XATTN_HEADS * N_MEM)), _const_spec((XATTN_HEADS * N_MEM, XATTN_W)),
                  _layer_spec((D_MODEL, D_MODEL), 0, resident=True)],
        out_specs=[tok(D_MODEL), tok(1), tok(IN_W), tok(CONV_W), tok(D_MODEL), tok(GMLP_W), tok(XATTN_HEADS * N_MEM)],
        out_shape=[jax.ShapeDtypeStruct((s, D_MODEL), F32), jax.ShapeDtypeStruct((s, 1), F32),
                   jax.ShapeDtypeStruct((s, IN_W), F32), jax.ShapeDtypeStruct((s, CONV_W), F32),
                   jax.ShapeDtypeStruct((s, D_MODEL), BF16), jax.ShapeDtypeStruct((s, GMLP_W), F32),
                   jax.ShapeDtypeStruct((s, XATTN_HEADS * N_MEM), F32)],
        scratch_shapes=[pltpu.VMEM((t + CONV_HALO, CONV_W), F32), pltpu.VMEM((t + 8, CONV_W), F32)],
        operands=(xin, pd, win_t, conv_w, pa, wst, bst, kt_all, v_all, w_out), semantics=("arbitrary",))


VD_LN_G, VD_LN_B, VD_LOSS = 0, 1, 2
VA_CONV_B, VA_LNA_G, VA_LNA_B, VA_LNV_G, VA_LNV_B = 0, 1, 2, 3, 4


def _mixer_bwd_d(gz, xh1, rstd1, h, ac, mixed, probs, pd, conv_w, pa, wstt, k_all, vt_all, w_out, win_t, l, t, side_ops=None):
    s = gz.shape[0]
    nt = s // t

    def body(gz_ref, xh_ref, rstd_ref, h_ref, ac_ref, mixed_ref, p_ref, pd_ref, cw_ref, pa_ref, wstt_ref,
             k_ref, vt_ref, wout_ref, wint_ref,
             dx_ref, dh_ref, dmix_ref, vd_ref, va_ref, dcw_ref, dws_ref, dbs_ref, dkt_ref, dv_ref,
             ebuf, dvnbuf, dbsacc, erbuf):
        i = pl.program_id(0)

        @pl.when(i == 0)
        def _():
            vd_ref[...] = jnp.zeros_like(vd_ref)
            va_ref[...] = jnp.zeros_like(va_ref)
            dcw_ref[...] = jnp.zeros_like(dcw_ref)
            dws_ref[...] = jnp.zeros_like(dws_ref)
            dbs_ref[...] = jnp.zeros_like(dbs_ref)
            dkt_ref[...] = jnp.zeros_like(dkt_ref)
            dv_ref[...] = jnp.zeros_like(dv_ref)
            dbsacc[...] = jnp.zeros_like(dbsacc)
            ebuf[t:t + CONV_HALO, :] = jnp.zeros((CONV_HALO, CONV_W), F32)

        gz_v = gz_ref[...]
        xh = xh_ref[...]
        vd_ref[VD_LN_G:VD_LN_G + 1, :] += _colsum(gz_v * xh)
        vd_ref[VD_LN_B:VD_LN_B + 1, :] += _colsum(gz_v)
        dz = _ln_bwd(gz_v * _row(pd_ref, PD_G1), xh, rstd_ref[...])
        dzb = dz.astype(BF16)
        dmix_ref[...] = dzb
        dcat = _dot_nt(dzb, wout_ref[...])
        d_a, d_g, d_o = dcat[:, 0:CONV_W], dcat[:, CONV_W:CONV_W + GMLP_W], dcat[:, CONV_W + GMLP_W:]

        h = h_ref[...]
        a1, a2 = h[:, 0:CONV_W], h[:, CONV_W:2 * CONV_W]
        hu, hv = h[:, 2 * CONV_W:2 * CONV_W + GMLP_W], h[:, 2 * CONV_W + GMLP_W:2 * CONV_W + 2 * GMLP_W]
        q = h[:, IN_W - XATTN_W:]

        xh_a, rstd_a = _ln_fwd(ac_ref[...])
        an = xh_a * _row(pa_ref, PA_LNA_G) + _row(pa_ref, PA_LNA_B)
        sig = _sigmoid(an)
        d_an = d_a * (sig * (1.0 + an * (1.0 - sig)))
        va_ref[VA_LNA_G:VA_LNA_G + 1, :] += _colsum(d_an * xh_a)
        va_ref[VA_LNA_B:VA_LNA_B + 1, :] += _colsum(d_an)
        dac = _ln_bwd(d_an * _row(pa_ref, PA_LNA_G), xh_a, rstd_a)
        va_ref[VA_CONV_B:VA_CONV_B + 1, :] += _colsum(dac)
        ebuf[0:t, :] = dac
        sg = _sigmoid(a2)
        glu = a1 * sg
        dglu = jnp.zeros((t, CONV_W), F32)
        for r in range(8):
            if r > 0:
                erbuf[...] = ebuf[r:r + t + 24, :]
            src = ebuf if r == 0 else erbuf
            for a in range(4):
                o = 8 * a + r
                if o < CONV_K:
                    k = CONV_K - 1 - o
                    ek = src[8 * a:8 * a + t, :]
                    dglu = dglu + ek * cw_ref[k:k + 1, :]
                    dcw_ref[k:k + 1, :] += _colsum(ek * glu)
        ebuf[t:t + CONV_HALO, :] = ebuf[0:CONV_HALO, :]
        da1 = dglu * sg
        da2 = dglu * a1 * sg * (1.0 - sg)

        u, du = _gelu_and_grad(hu)
        vg, dvg_dhv = _gelu_and_grad(hv)
        xh_v, rstd_v = _ln_fwd(vg)
        vn = xh_v * _row(pa_ref, PA_LNV_G) + _row(pa_ref, PA_LNV_B)
        vnb = vn.astype(BF16)
        dhu = d_g * mixed_ref[...] * du
        dm = d_g * u
        dmb = dm.astype(BF16)
        lo = _lane_lo((CHUNK, LANE))
        for n in range(t // CHUNK):
            rows = slice(n * CHUNK, (n + 1) * CHUNK)
            dbsacc[...] += dm[rows, :]
            for j in range(GMLP_W // LANE):
                cols = slice(j * LANE, (j + 1) * LANE)
                dm_blk = dmb[rows, cols]
                r = _dot(wstt_ref[j], dm_blk)
                dvnbuf[rows, cols] = jnp.where(lo, r[:CHUNK], r[CHUNK:])
                zero = jnp.zeros_like(dm_blk)
                st = jnp.concatenate([jnp.where(lo, dm_blk, zero), jnp.where(lo, zero, dm_blk)], axis=0)
                dws_ref[j] += _dot_nt(st, vnb[rows, cols])
        dvn = dvnbuf[...]
        va_ref[VA_LNV_G:VA_LNV_G + 1, :] += _colsum(dvn * xh_v)
        va_ref[VA_LNV_B:VA_LNV_B + 1, :] += _colsum(dvn)
        dhv = _ln_bwd(dvn * _row(pa_ref, PA_LNV_G), xh_v, rstd_v) * dvg_dhv

        qb = q.astype(BF16)
        p = p_ref[...]
        dob = d_o.astype(BF16)
        dp = _dot(dob, vt_ref[...])
        dss = []
        for hd in range(XATTN_HEADS):
            cs = slice(hd * N_MEM, (hd + 1) * N_MEM)
            ph, dph = p[:, cs], dp[:, cs]
            dss.append(ph * (dph - jnp.sum(ph * dph, axis=-1, keepdims=True)) * ATT_SCALE)
        dsb = jnp.concatenate(dss, axis=1).astype(BF16)
        dq = _dot(dsb, k_ref[...])
        dkt_ref[...] += _dot_tn(qb, dsb)
        dv_ref[...] += _dot_tn(p.astype(BF16), dob)

        dhb = jnp.concatenate([da1, da2, dhu, dhv, dq], axis=1).astype(BF16)
        dh_ref[...] = dhb
        dx_ref[...] = ALPHA * dz + _dot(dhb, wint_ref[...])

        @pl.when(i == nt - 1)
        def _():
            acc = dbsacc[...]
            head = lax.broadcasted_iota(jnp.int32, (CHUNK, GMLP_W), 1) // HEAD_DIM
            lane = lax.broadcasted_iota(jnp.int32, (CHUNK, LANE), 1)
            out = jnp.zeros((CHUNK, LANE), F32)
            for hd in range(GMLP_W // HEAD_DIM):
                sh = jnp.sum(jnp.where(head == hd, acc, 0.0), axis=1, keepdims=True)
                out = out + jnp.where(lane == hd, sh, 0.0)
            dbs_ref[...] = out

    rev = lambda w: pl.BlockSpec((t, w), lambda i: (nt - 1 - i, 0))
    out_shape = [
        jax.ShapeDtypeStruct((s, D_MODEL), F32), jax.ShapeDtypeStruct((s, IN_W), BF16),
        jax.ShapeDtypeStruct((s, D_MODEL), BF16),
        jax.ShapeDtypeStruct((8, D_MODEL), F32), jax.ShapeDtypeStruct((8, CONV_W), F32),
        jax.ShapeDtypeStruct((CONV_HALO, CONV_W), F32), jax.ShapeDtypeStruct((3, 2 * CHUNK, CHUNK), F32),
        jax.ShapeDtypeStruct((CHUNK, LANE), F32),
        jax.ShapeDtypeStruct((XATTN_W, XATTN_HEADS * N_MEM), F32), jax.ShapeDtypeStruct((XATTN_HEADS * N_MEM, XATTN_W), F32),
    ]
    out_specs = [rev(D_MODEL), rev(IN_W), rev(D_MODEL)] + [_const_spec(o.shape) for o in out_shape[3:]]
    return _call_with_side(
        body, side_ops, name="mixer_bwd_d", grid=(nt,),
        in_specs=[rev(D_MODEL), rev(D_MODEL), rev(1), rev(IN_W), rev(CONV_W), rev(GMLP_W), rev(XATTN_HEADS * N_MEM),
                  _layer_spec((8, D_MODEL), l), _layer_spec((CONV_HALO, CONV_W), l), _layer_spec((8, CONV_W), l),
                  _layer_spec((3, 2 * CHUNK, CHUNK), l),
                  _const_spec((XATTN_HEADS * N_MEM, XATTN_W)), _const_spec((XATTN_W, XATTN_HEADS * N_MEM)),
                  _layer_spec((D_MODEL, D_MODEL), 0, resident=True), _layer_spec((IN_W, D_MODEL), 0, resident=True)],
        out_specs=out_specs, out_shape=out_shape,
        scratch_shapes=[pltpu.VMEM((t + CONV_HALO, CONV_W), F32), pltpu.VMEM((t, GMLP_W), F32),
                        pltpu.VMEM((CHUNK, GMLP_W), F32), pltpu.VMEM((t + 24, CONV_W), F32)],
        operands=(gz, xh1, rstd1, h, ac, mixed, probs, pd, conv_w, pa, wstt, k_all, vt_all, w_out, win_t),
        semantics=("arbitrary",))


def _mixer_bwd_w(xin, pd, dh, cat, dmix, l, t, side_ops=None):
    s = xin.shape[0]
    nt = s // t

    def body(x_ref, pd_ref, dh_ref, cat_ref, dmix_ref, dwin_ref, dwout_ref):
        @pl.when(pl.program_id(0) == 0)
        def _():
            dwin_ref[...] = jnp.zeros_like(dwin_ref)
            dwout_ref[...] = jnp.zeros_like(dwout_ref)

        xb = (x_ref[...] * _row(pd_ref, PD_GIN) + _row(pd_ref, PD_BIN)).astype(BF16)
        dwin_ref[...] += _dot_tn(dh_ref[...], xb)
        dwout_ref[...] += _dot_tn(cat_ref[...], dmix_ref[...])

    tok = lambda w: pl.BlockSpec((t, w), lambda i: (i, 0))
    return _call_with_side(
        body, side_ops, name="mixer_bwd_w", grid=(nt,),
        in_specs=[tok(D_MODEL), _layer_spec((8, D_MODEL), l), tok(IN_W), tok(D_MODEL), tok(D_MODEL)],
        out_specs=[_layer_spec((IN_W, D_MODEL), 0, resident=True), _layer_spec((D_MODEL, D_MODEL), 0, resident=True)],
        out_shape=[jax.ShapeDtypeStruct((1, IN_W, D_MODEL), F32), jax.ShapeDtypeStruct((1, D_MODEL, D_MODEL), F32)],
        scratch_shapes=[], operands=(xin, pd, dh, cat, dmix), semantics=("arbitrary",))


PF_W0, PF_B = 0, 3


def _ffn_fwd(xh1, pd, wup_t, pf, w_d, l, t, side_ops=None):
    s = xh1.shape[0]
    nt = s // t

    def body(xh_ref, pd_ref, wg_ref, wv_ref, pf_ref, wd_ref, xh2_ref, rstd_ref, upg_ref, upv_ref, fbuf):
        i = pl.program_id(0)

        @pl.when(i == 0)
        def _():
            fbuf[0:FFN_HALO, :] = jnp.zeros((FFN_HALO, FF_P), F32)

        x1 = xh_ref[...] * _row(pd_ref, PD_G1) + _row(pd_ref, PD_B1)
        xb = x1.astype(BF16)
        y = jnp.zeros((t, D_MODEL), F32)
        for hf in range(2):
            cs = slice(hf * FF_H, (hf + 1) * FF_H)
            ug = _dot_nt(xb, wg_ref[cs, :])
            uv = _dot_nt(xb, wv_ref[cs, :])
            upg_ref[:, cs] = ug
            upv_ref[:, cs] = uv
            fbuf[FFN_HALO:FFN_HALO + t, cs] = ug
            gate = jnp.zeros((t, FF_H), F32) + pf_ref[PF_B:PF_B + 1, cs]
            for k in range(FFN_CONV_K):
                off = FFN_HALO - (FFN_CONV_K - 1) + k
                gate = gate + fbuf[off:off + t, cs] * pf_ref[PF_W0 + k:PF_W0 + k + 1, cs]
            fbuf[0:FFN_HALO, cs] = fbuf[t:t + FFN_HALO, cs]
            hm = gate * _sigmoid(gate) * uv
            y = y + _dot(hm.astype(BF16), wd_ref[cs, :])
        xh2, rstd = _ln_fwd(ALPHA * x1 + y)
        xh2_ref[...] = xh2
        rstd_ref[...] = rstd

    tok = lambda w: pl.BlockSpec((t, w), lambda i: (i, 0))
    return _call_with_side(
        body, side_ops, name="ffn_fwd", grid=(nt,),
        in_specs=[tok(D_MODEL), _layer_spec((8, D_MODEL), l),
                  _layer_spec((FF_P, D_MODEL), 0, 0, resident=True), _layer_spec((FF_P, D_MODEL), 0, 1, resident=True),
                  _layer_spec((8, FF_P), l), _layer_spec((FF_P, D_MODEL), 0, resident=True)],
        out_specs=[tok(D_MODEL), tok(1), tok(FF_P), tok(FF_P)],
        out_shape=[jax.ShapeDtypeStruct((s, D_MODEL), F32), jax.ShapeDtypeStruct((s, 1), F32),
                   jax.ShapeDtypeStruct((s, FF_P), F32), jax.ShapeDtypeStruct((s, FF_P), F32)],
        scratch_shapes=[pltpu.VMEM((t + FFN_HALO, FF_P), F32)],
        operands=(xh1, pd, wup_t, wup_t, pf, w_d), semantics=("arbitrary",))


VF_W0, VF_B = 0, 3


def _ffn_bwd_d(gz_or_target, xh2, rstd2, upg, upv, pd, pf, w_d, wup_t, l, t, last, side_ops=None):
    s = xh2.shape[0]
    nt = s // t
    hb = t // FFN_HALO

    def body(gz_ref, xh2_ref, rstd_ref, upg_ref, halo_ref, upv_ref, pd_ref, pf_ref, wd_ref, wg_ref, wv_ref,
             dx_ref, dy_ref, dug_ref, duv_ref, hm_ref, vd_ref, vf_ref, gbuf, ebuf, s1buf, s2buf):
        i = pl.program_id(0)
        first_tile = i == nt - 1

        @pl.when(i == 0)
        def _():
            vd_ref[...] = jnp.zeros_like(vd_ref)
            vf_ref[...] = jnp.zeros_like(vf_ref)
            ebuf[t:t + FFN_HALO, :] = jnp.zeros((FFN_HALO, FF_P), F32)

        xh2_v = xh2_ref[...]
        if last:
            diff = xh2_v * _row(pd_ref, PD_G2) + _row(pd_ref, PD_B2) - gz_ref[...]
            vd_ref[VD_LOSS:VD_LOSS + 1, :] += _colsum(diff * diff)
            gz_v = diff * (1.0 / D_MODEL)
        else:
            gz_v = gz_ref[...]
        vd_ref[VD_LN_G:VD_LN_G + 1, :] += _colsum(gz_v * xh2_v)
        vd_ref[VD_LN_B:VD_LN_B + 1, :] += _colsum(gz_v)
        dz = _ln_bwd(gz_v * _row(pd_ref, PD_G2), xh2_v, rstd_ref[...])
        dyb = dz.astype(BF16)
        dy_ref[...] = dyb
        dx = ALPHA * dz
        for hf in range(2):
            cs = slice(hf * FF_H, (hf + 1) * FF_H)
            ug = upg_ref[:, cs]
            uv = upv_ref[:, cs]
            halo = halo_ref[:, cs]
            gbuf[0:FFN_HALO, :] = jnp.where(first_tile, jnp.zeros_like(halo), halo)
            gbuf[FFN_HALO:FFN_HALO + t, :] = ug
            s1buf[...] = gbuf[FFN_HALO - 1:FFN_HALO - 1 + t, :]
            s2buf[...] = gbuf[FFN_HALO - 2:FFN_HALO - 2 + t, :]
            ug1 = s1buf[...]
            ug2 = s2buf[...]
            gate = (pf_ref[PF_B:PF_B + 1, cs] + ug2 * pf_ref[PF_W0:PF_W0 + 1, cs] + ug1 * pf_ref[PF_W0 + 1:PF_W0 + 2, cs]
                    + ug * pf_ref[PF_W0 + 2:PF_W0 + 3, cs])
            sig = _sigmoid(gate)
            sl = gate * sig
            hm_ref[:, cs] = sl * uv
            dhm = _dot_nt(dyb, wd_ref[cs, :])
            duv = dhm * sl
            dgate = dhm * uv * (sig * (1.0 + gate * (1.0 - sig)))
            vf_ref[VF_B:VF_B + 1, cs] += _colsum(dgate)
            vf_ref[VF_W0:VF_W0 + 1, cs] += _colsum(dgate * ug2)
            vf_ref[VF_W0 + 1:VF_W0 + 2, cs] += _colsum(dgate * ug1)
            vf_ref[VF_W0 + 2:VF_W0 + 3, cs] += _colsum(dgate * ug)
            ebuf[0:t, cs] = dgate
            dug = (ebuf[2:2 + t, cs] * pf_ref[PF_W0:PF_W0 + 1, cs] + ebuf[1:1 + t, cs] * pf_ref[PF_W0 + 1:PF_W0 + 2, cs]
                   + dgate * pf_ref[PF_W0 + 2:PF_W0 + 3, cs])
            ebuf[t:t + FFN_HALO, cs] = ebuf[0:FFN_HALO, cs]
            dugb = dug.astype(BF16)
            duvb = duv.astype(BF16)
            dug_ref[:, cs] = dugb
            duv_ref[:, cs] = duvb
            dx = dx + _dot(dugb, wg_ref[cs, :]) + _dot(duvb, wv_ref[cs, :])
        dx_ref[...] = dx

        if last:
            @pl.when(i == nt - 1)
            def _():
                tot = jnp.sum(vd_ref[VD_LOSS:VD_LOSS + 1, :], axis=1, keepdims=True)
                vd_ref[VD_LOSS:VD_LOSS + 1, :] = jnp.zeros((1, D_MODEL), F32) + tot

    rev = lambda w: pl.BlockSpec((t, w), lambda i: (nt - 1 - i, 0))
    halo_spec = pl.BlockSpec((FFN_HALO, FF_P), lambda i: (jnp.maximum((nt - 1 - i) * hb - 1, 0), 0))
    out_shape = [jax.ShapeDtypeStruct((s, D_MODEL), F32), jax.ShapeDtypeStruct((s, D_MODEL), BF16),
                 jax.ShapeDtypeStruct((s, FF_P), BF16), jax.ShapeDtypeStruct((s, FF_P), BF16),
                 jax.ShapeDtypeStruct((s, FF_P), F32),
                 jax.ShapeDtypeStruct((8, D_MODEL), F32), jax.ShapeDtypeStruct((8, FF_P), F32)]
    return _call_with_side(
        body, side_ops, name="ffn_bwd_d_last" if last else "ffn_bwd_d", grid=(nt,),
        in_specs=[rev(D_MODEL), rev(D_MODEL), rev(1), rev(FF_P), halo_spec, rev(FF_P),
                  _layer_spec((8, D_MODEL), l), _layer_spec((8, FF_P), l),
                  _layer_spec((FF_P, D_MODEL), 0, resident=True),
                  _layer_spec((FF_P, D_MODEL), 0, 0, resident=True), _layer_spec((FF_P, D_MODEL), 0, 1, resident=True)],
        out_specs=[rev(D_MODEL), rev(D_MODEL), rev(FF_P), rev(FF_P), rev(FF_P),
                   _const_spec((8, D_MODEL)), _const_spec((8, FF_P))],
        out_shape=out_shape,
        scratch_shapes=[pltpu.VMEM((t + FFN_HALO, FF_H), F32), pltpu.VMEM((t + FFN_HALO, FF_P), F32),
                        pltpu.VMEM((t, FF_H), F32), pltpu.VMEM((t, FF_H), F32)],
        operands=(gz_or_target, xh2, rstd2, upg, upg, upv, pd, pf, w_d, wup_t, wup_t), semantics=("arbitrary",))


def _ffn_bwd_w(xh1, pd, dy, dug, duv, hm, l, t, side_ops=None):
    s = xh1.shape[0]
    nt = s // t

    def body(xh_ref, pd_ref, dy_ref, dug_ref, duv_ref, hm_ref, dwup_ref, dwd_ref):
        @pl.when(pl.program_id(1) == 0)
        def _():
            dwup_ref[...] = jnp.zeros_like(dwup_ref)
            dwd_ref[...] = jnp.zeros_like(dwd_ref)

        xb = (xh_ref[...] * _row(pd_ref, PD_G1) + _row(pd_ref, PD_B1)).astype(BF16)
        dwup_ref[0] += _dot_tn(dug_ref[...], xb)
        dwup_ref[1] += _dot_tn(duv_ref[...], xb)
        dwd_ref[...] += _dot_tn(hm_ref[...].astype(BF16), dy_ref[...])

    tok = lambda w: pl.BlockSpec((t, w), lambda c, i: (i, 0))
    half = pl.BlockSpec((t, FF_H), lambda c, i: (i, c))
    return _call_with_side(
        body, side_ops, name="ffn_bwd_w", grid=(2, nt),
        in_specs=[tok(D_MODEL), pl.BlockSpec((None, 8, D_MODEL), lambda c, i: (l, 0, 0)), tok(D_MODEL), half, half, half],
        out_specs=[pl.BlockSpec((None, 2, FF_H, D_MODEL), lambda c, i: (0, 0, c, 0), pipeline_mode=pl.Buffered(1)),
                   pl.BlockSpec((None, FF_H, D_MODEL), lambda c, i: (0, c, 0), pipeline_mode=pl.Buffered(1))],
        out_shape=[jax.ShapeDtypeStruct((1, 2, FF_P, D_MODEL), F32), jax.ShapeDtypeStruct((1, FF_P, D_MODEL), F32)],
        scratch_shapes=[], operands=(xh1, pd, dy, dug, duv, hm), semantics=("arbitrary", "arbitrary"))


def _adamw_math(w, g, m, v):
    nm = ADAM_B1 * m + (1.0 - ADAM_B1) * g
    nv = ADAM_B2 * v + (1.0 - ADAM_B2) * (g * g)
    m_hat = nm / (1.0 - ADAM_B1 ** ADAM_STEP)
    v_hat = nv / (1.0 - ADAM_B2 ** ADAM_STEP)
    return -ADAM_LR * (m_hat / (jnp.sqrt(v_hat) + ADAM_EPS) + ADAM_WD * w), nm, nv


def _adamw_layers(w, gs, m, v, name):
    shp = w.shape
    _, rows, cols = shp
    tr = _row_tile(rows, cols * 4, mult=8)
    nb = rows // tr

    def body(w_ref, g0_ref, g1_ref, m_ref, v_ref, g_ref, d_ref, nm_ref, nv_ref):
        g = jnp.where(pl.program_id(0) == 0, g0_ref[...], g1_ref[...])
        g_ref[...] = g
        d_ref[...], nm_ref[...], nv_ref[...] = _adamw_math(w_ref[...], g, m_ref[...], v_ref[...])

    stacked = pl.BlockSpec((tr, cols), lambda l, i: (l * nb + i, 0))
    single = pl.BlockSpec((tr, cols), lambda l, i: (i, 0))
    sh = jax.ShapeDtypeStruct((DEPTH * rows, cols), F32)
    flat = lambda a: a.reshape(DEPTH * rows, cols)
    outs = pl.pallas_call(body, name=name, grid=(DEPTH, nb), in_specs=[stacked, single, single, stacked, stacked],
                          out_specs=[stacked] * 4, out_shape=[sh] * 4,
                          compiler_params=_cparams(("arbitrary", "arbitrary")))(flat(w), gs[0], gs[1], flat(m), flat(v))
    return [o.reshape(shp) for o in outs]


def _adamw_small(ws, gs, ms, vs):
    n = len(ws)

    def body(*refs):
        w_refs, g_refs, m_refs, v_refs = refs[:n], refs[n:2 * n], refs[2 * n:3 * n], refs[3 * n:4 * n]
        d_refs, nm_refs, nv_refs = refs[4 * n:5 * n], refs[5 * n:6 * n], refs[6 * n:7 * n]
        for k in range(n):
            d_refs[k][...], nm_refs[k][...], nv_refs[k][...] = _adamw_math(w_refs[k][...], g_refs[k][...], m_refs[k][...],
                                                                             v_refs[k][...])

    shapes = [jax.ShapeDtypeStruct(w.shape, F32) for w in ws]
    outs = pl.pallas_call(body, name="adamw_small", out_shape=shapes * 3, compiler_params=_cparams())(*ws, *gs, *ms, *vs)
    return outs[:n], outs[n:2 * n], outs[2 * n:]


def _all_gather_chips(tensors, name):
    n = len(tensors)
    halves = [a.shape[1] // 2 for a in tensors]

    def body(*refs):
        x_refs, out_refs = refs[:n], refs[n:2 * n]
        send_sems, recv_sems, local_sems = refs[2 * n:]
        x, y, c, chips = _my_place()
        me, sibling, mej = (x, y, c), (x, y, 1 - c), 2 * x + y

        def rows(tn, px, py, pc):
            return out_refs[tn].at[:, 2 * px + py, pl.ds(pc * halves[tn], halves[tn]), :]

        def copy(tn, k, block, to, src=None):
            return pltpu.make_async_remote_copy(
                src_ref=rows(tn, *block) if src is None else src, dst_ref=rows(tn, *block),
                send_sem=send_sems.at[tn, k], recv_sem=recv_sems.at[tn, k], device_id=to, device_id_type=MESH)

        mine_src = [x_refs[tn].at[:, pl.ds(c * halves[tn], halves[tn]), :] for tn in range(n)]
        mine = [pltpu.make_async_copy(mine_src[tn], rows(tn, *me), local_sems.at[tn]) for tn in range(n)]
        first = []
        for j, chip in enumerate(chips):
            first += [copy(tn, 1 + j, me, (*chip, c), src=mine_src[tn]) for tn in range(n)]
        first += [copy(tn, 0, me, sibling, src=mine_src[tn]) for tn in range(n)]
        for cp in first + mine:
            cp.start()
        passed = []
        for j, chip in enumerate(chips):
            for tn in range(n):
                copy(tn, 1 + j, (*chip, c), me).wait_recv()
                fwd = copy(tn, 4 + j, (*chip, c), sibling)
                fwd.start()
                passed.append(fwd)
        for tn in range(n):
            copy(tn, 0, sibling, me).wait_recv()
            for j, chip in enumerate(chips):
                copy(tn, 4 + j, (*chip, 1 - c), me).wait_recv()
        for cp in first + passed:
            cp.wait_send()
        for cp in mine:
            cp.wait()

    return pl.pallas_call(
        body, name=name,
        out_shape=[jax.ShapeDtypeStruct((a.shape[0], N_CHIPS) + a.shape[1:], a.dtype) for a in tensors],
        in_specs=[ANY] * n, out_specs=[ANY] * n,
        scratch_shapes=[pltpu.SemaphoreType.DMA((n, 7)), pltpu.SemaphoreType.DMA((n, 7)), pltpu.SemaphoreType.DMA((n,))],
    )(*tensors)


def _swap_op(g5s):
    n = len(g5s)

    def make(ins, outs, scr):
        send_sems, recv_sems = scr
        x, y, c, _ = _my_place()

        def copies():
            return [pltpu.make_async_remote_copy(
                src_ref=ins[tn].at[:, :, 1 - c], dst_ref=outs[tn], send_sem=send_sems.at[tn], recv_sem=recv_sems.at[tn],
                device_id=(x, y, 1 - c), device_id_type=MESH) for tn in range(n)]

        def start():
            for cp in copies():
                cp.start()

        def wait():
            for cp in copies():
                cp.wait()

        return start, wait

    return _SideOp(g5s, [jax.ShapeDtypeStruct(g.shape[:2] + g.shape[3:], g.dtype) for g in g5s],
                   [pltpu.SemaphoreType.DMA((n,)), pltpu.SemaphoreType.DMA((n,))], make)


def _scatter_op(parts):
    n = len(parts)

    def make(ins, outs, scr):
        send_sems, recv_sems = scr
        x, y, c, chips = _my_place()

        def copies():
            return [pltpu.make_async_remote_copy(
                src_ref=ins[tn].at[:, 2 * px + py], dst_ref=outs[tn].at[:, k],
                send_sem=send_sems.at[tn, k], recv_sem=recv_sems.at[tn, k],
                device_id=(px, py, c), device_id_type=MESH) for k, (px, py) in enumerate(chips) for tn in range(n)]

        def start():
            for cp in copies():
                cp.start()

        def wait():
            for cp in copies():
                cp.wait()

        return start, wait

    return _SideOp(parts, [jax.ShapeDtypeStruct((p.shape[0], 3) + p.shape[2:], p.dtype) for p in parts],
                   [pltpu.SemaphoreType.DMA((n, 3)), pltpu.SemaphoreType.DMA((n, 3))], make)


def _sgather_op(fs):
    n = len(fs)

    def make(ins, outs, scr):
        send_sems, recv_sems = scr
        x, y, c, _ = _my_place()

        def copy(tn, dst_half):
            return pltpu.make_async_remote_copy(
                src_ref=outs[tn].at[:, c], dst_ref=outs[tn].at[:, dst_half], send_sem=send_sems.at[tn],
                recv_sem=recv_sems.at[tn], device_id=(x, y, 1 - c), device_id_type=MESH)

        def start():
            for tn in range(n):
                copy(tn, c).start()

        def wait():
            for tn in range(n):
                copy(tn, 1 - c).wait_recv()
                copy(tn, c).wait_send()

        return start, wait

    return _SideOp(fs, [jax.ShapeDtypeStruct(f.shape, f.dtype) for f in fs],
                   [pltpu.SemaphoreType.DMA((n,)), pltpu.SemaphoreType.DMA((n,))], make, aliases={tn: tn for tn in range(n)})


def _run_side_ops(ops, name):
    return _call_with_side(lambda: None, ops, name=name, grid=(1,), in_specs=[], out_specs=[], out_shape=[],
                           scratch_shapes=[], operands=(), semantics=("arbitrary",))[1]


def _gather_devices_op(xs):
    def make(ins, outs, scr):
        send_sems, recv_sems, local_sem = scr
        x, y, c, chips = _my_place()
        peers = [(x, y, 1 - c)] + [(px, py, pc) for (px, py) in chips for pc in (c, 1 - c)]
        me = 4 * x + 2 * y + c

        def copy(k, slot):
            return pltpu.make_async_remote_copy(
                src_ref=ins[0], dst_ref=outs[0].at[slot], send_sem=send_sems.at[k], recv_sem=recv_sems.at[k],
                device_id=peers[k], device_id_type=MESH)

        def local():
            return pltpu.make_async_copy(ins[0], outs[0].at[me], local_sem)

        def start():
            for k in range(7):
                copy(k, me).start()
            local().start()

        def wait():
            for k, (px, py, pc) in enumerate(peers):
                copy(k, 4 * px + 2 * py + pc).wait_recv()
                copy(k, me).wait_send()
            local().wait()

        return start, wait

    return _SideOp([xs], [jax.ShapeDtypeStruct((8,) + xs.shape, xs.dtype)],
                   [pltpu.SemaphoreType.DMA((7,)), pltpu.SemaphoreType.DMA((7,)), pltpu.SemaphoreType.DMA], make)


def _add_halves(gs, recvs, place):
    n = len(gs)

    def body(place_ref, *refs):
        g_refs, r_refs, o_refs = refs[:n], refs[n:2 * n], refs[2 * n:]
        for tn in range(n):
            o_refs[tn][...] = (g_refs[tn][...] + r_refs[tn][...]).astype(BF16)

    def gspec(g):
        return pl.BlockSpec((None, None, None) + g.shape[3:], lambda l, j, p: (l, j, p[1], 0, 0))

    def rspec(r):
        return pl.BlockSpec((None, None) + r.shape[2:], lambda l, j, p: (l, j, 0, 0))

    grid_spec = pltpu.PrefetchScalarGridSpec(
        num_scalar_prefetch=1, grid=(gs[0].shape[0], N_CHIPS),
        in_specs=[gspec(g) for g in gs] + [rspec(r) for r in recvs], out_specs=[rspec(r) for r in recvs])
    return pl.pallas_call(body, name="rs_add", grid_spec=grid_spec,
                          out_shape=[jax.ShapeDtypeStruct(r.shape, BF16) for r in recvs],
                          compiler_params=_cparams(("arbitrary", "arbitrary")))(place, *gs, *recvs)


def _sum_slots(parts, slots, place):
    n = len(parts)

    def body(place_ref, *refs):
        p_refs, s_refs, o_refs = refs[:n], refs[n:2 * n], refs[2 * n:]
        for tn in range(n):
            acc = p_refs[tn][...].astype(F32)
            for k in range(3):
                acc = acc + s_refs[tn][k].astype(F32)
            o_refs[tn][...] = acc

    def pspec(p):
        return pl.BlockSpec((None, None) + p.shape[2:], lambda l, pl_: (l, pl_[0], 0, 0))

    def sspec(sl):
        return pl.BlockSpec((None,) + sl.shape[1:], lambda l, pl_: (l, 0, 0, 0))

    def ospec(p):
        return pl.BlockSpec((None, None) + p.shape[2:], lambda l, pl_: (l, pl_[1], 0, 0))

    grid_spec = pltpu.PrefetchScalarGridSpec(
        num_scalar_prefetch=1, grid=(parts[0].shape[0],),
        in_specs=[pspec(p) for p in parts] + [sspec(sl) for sl in slots], out_specs=[ospec(p) for p in parts])
    return pl.pallas_call(body, name="rs_sum", grid_spec=grid_spec,
                          out_shape=[jax.ShapeDtypeStruct((p.shape[0], 2) + p.shape[2:], F32) for p in parts],
                          compiler_params=_cparams(("arbitrary",)))(place, *parts, *slots)


def _sum_devices(gathered, m_per):
    def body(g_ref, o_ref):
        acc = g_ref[0:m_per, :]
        for d in range(1, 8):
            acc = acc + g_ref[d * m_per:(d + 1) * m_per, :]
        o_ref[...] = acc

    return pl.pallas_call(body, name="small_sum", out_shape=jax.ShapeDtypeStruct((m_per, LANE), F32),
                          compiler_params=_cparams())(gathered)


def _pad_ff_cols(a):
    lead = a.shape[:-1]
    n = a.shape[-1] // FF_Q
    a = a.reshape(*lead, n, FF_Q)
    a = jnp.pad(a, [(0, 0)] * len(lead) + [(0, 0), (0, FF_QP - FF_Q)])
    return a.reshape(*lead, n * FF_QP)


def _unpad_ff_cols(a):
    lead = a.shape[:-1]
    n = a.shape[-1] // FF_QP
    return a.reshape(*lead, n, FF_QP)[..., :FF_Q].reshape(*lead, n * FF_Q)


def _pack_small(parts):
    flat = jnp.concatenate([p.reshape(-1) for p in parts])
    return flat.reshape(-1, LANE)


SMALL_SHAPES = [("conv_a_w", (CONV_K, CONV_W)), ("conv_a_b", (CONV_W,)), ("ln_a_g", (CONV_W,)), ("ln_a_b", (CONV_W,)),
                ("ln_v_g", (GMLP_W,)), ("ln_v_b", (GMLP_W,)), ("w_s", (6, CHUNK, CHUNK)), ("b_s", (6, CHUNK)),
                ("ln1_g", (D_MODEL,)), ("ln1_b", (D_MODEL,)), ("conv_f_w", (FFN_CONV_K, D_FF)), ("conv_f_b", (D_FF,)),
                ("ln2_g", (D_MODEL,)), ("ln2_b", (D_MODEL,))]


def _unpack_small(flat2d):
    flat = flat2d.reshape(DEPTH, -1)
    out, o = {}, 0
    for name, shp in SMALL_SHAPES:
        n = 1
        for d in shp:
            n *= d
        out[name] = flat[:, o:o + n].reshape((DEPTH,) + shp)
        o += n
    return out


def _rows8(rows):
    blk = jnp.stack(rows, axis=1)
    return jnp.pad(blk, ((0, 0), (0, 8 - len(rows)), (0, 0)))


def kernel(x, mem, w_in, conv_a_w, conv_a_b, ln_a_g, ln_a_b, ln_v_g, ln_v_b, w_s, b_s, w_mk, w_mv, w_out, ln1_g, ln1_b, w_up, conv_f_w, conv_f_b, w_down, ln2_g, ln2_b, loss_target, m_w_in, m_conv_a_w, m_conv_a_b, m_ln_a_g, m_ln_a_b, m_ln_v_g, m_ln_v_b, m_w_s, m_b_s, m_w_mk, m_w_mv, m_w_out, m_ln1_g, m_ln1_b, m_w_up, m_conv_f_w, m_conv_f_b, m_w_down, m_ln2_g, m_ln2_b, v_w_in, v_conv_a_w, v_conv_a_b, v_ln_a_g, v_ln_a_b, v_ln_v_g, v_ln_v_b, v_w_s, v_b_s, v_w_mk, v_w_mv, v_w_out, v_ln1_g, v_ln1_b, v_w_up, v_conv_f_w, v_conv_f_b, v_w_down, v_ln2_g, v_ln2_b):
    seq = x.shape[1]
    t_fwd = min(512, seq)
    t_bwd = min(256, seq)
    t_wg = min(1024, seq)
    chip = 2 * lax.axis_index("x") + lax.axis_index("y")
    core = lax.axis_index("c")
    place = jnp.stack([chip, core]).astype(jnp.int32)
    x0 = x[0]
    mem0 = mem[0]
    target = loss_target[0]

    sh_in = w_in.transpose(0, 2, 1).astype(BF16)
    sh_mk, sh_mv, sh_out = w_mk.astype(BF16), w_mv.astype(BF16), w_out.astype(BF16)
    sh_up = _pad_ff_cols(w_up).transpose(0, 2, 1).astype(BF16)
    sh_dn = jnp.pad(w_down, ((0, 0), (0, FF_QP - FF_Q), (0, 0))).astype(BF16)

    def mixer_weights(g_in, g_mk, g_mv, g_out):
        return dict(win_t=g_in.reshape(1, IN_W, D_MODEL), wmk=g_mk.reshape(1, D_MODEL, XATTN_W),
                    wmv=g_mv.reshape(1, D_MODEL, XATTN_W), wout=g_out.reshape(1, D_MODEL, D_MODEL))

    def ffn_weights(g_up, g_dn):
        return dict(wup_t=g_up.reshape(1, 2, FF_P, D_MODEL), wdown=g_dn.reshape(1, FF_P, D_MODEL))

    n_ca = conv_a_w.size
    small_w = _pack_small([conv_a_w, conv_f_w, jnp.zeros((2 * 80 * LANE - n_ca - conv_f_w.size,), F32)])[None]
    *g_mixer0, small_g = _all_gather_chips([sh_in[:1], sh_mk[:1], sh_mv[:1], sh_out[:1], small_w], "ag_mixer0")
    wts = [mixer_weights(*g_mixer0), None]
    small_g = small_g.reshape(N_CHIPS, -1)
    conv_a_full = small_g[:, :n_ca].reshape(N_CHIPS, DEPTH, CONV_K, CONV_W // 4).transpose(1, 2, 0, 3).reshape(DEPTH, CONV_K, CONV_W)
    conv_f_full = small_g[:, n_ca:n_ca + conv_f_w.size].reshape(N_CHIPS, DEPTH, FFN_CONV_K, FF_Q).transpose(1, 2, 0, 3).reshape(DEPTH, FFN_CONV_K, D_FF)

    tril = jnp.tril(jnp.ones((CHUNK, CHUNK), dtype=bool))
    ws_m = jnp.where(tril, w_s, 0.0)
    wst = ws_m.reshape(DEPTH, 3, 2 * CHUNK, CHUNK).astype(BF16)
    wstt = ws_m.transpose(0, 1, 3, 2).reshape(DEPTH, 3, 2 * CHUNK, CHUNK).astype(BF16)
    bst = jnp.repeat(b_s.transpose(0, 2, 1), HEAD_DIM, axis=2)
    conv_w = jnp.pad(conv_a_full, ((0, 0), (0, CONV_HALO - CONV_K), (0, 0)))
    pa = _rows8([conv_a_b, ln_a_g, ln_a_b, ln_v_g, ln_v_b])
    gin = jnp.concatenate([jnp.ones((1, D_MODEL), F32), ln2_g[:DEPTH - 1]], axis=0)
    bin_ = jnp.concatenate([jnp.zeros((1, D_MODEL), F32), ln2_b[:DEPTH - 1]], axis=0)
    pd = _rows8([gin, bin_, ln1_g, ln1_b, ln2_g, ln2_b])
    pf = jnp.concatenate([_pad_ff_cols(conv_f_full), _pad_ff_cols(conv_f_b)[:, None, :],
                          jnp.zeros((DEPTH, 8 - FFN_CONV_K - 1, FF_P), F32)], axis=1)

    acts = []
    xin = x0
    for l in range(DEPTH):
        w = wts[l]
        kt_all, k_all, v_all, vt_all = _kv_fwd(mem0, w["wmk"], w["wmv"])
        ops = [_gather_chips_op([sh_up[0], sh_dn[0]])] if l == 0 else None
        (xh1, rstd1, h, ac, cat, mixed, probs), side = _mixer_fwd(xin, pd, w["win_t"], conv_w, pa, wst, bst, kt_all, v_all, w["wout"],
                                                    l, t_fwd, ops)
        if l == 0:
            w.update(ffn_weights(*side[0]))
        ops = [_gather_chips_op([sh_in[1], sh_mk[1], sh_mv[1], sh_out[1], sh_up[1], sh_dn[1]])] if l == 0 else None
        (xh2, rstd2, upg, upv), side = _ffn_fwd(xh1, pd, w["wup_t"], pf, w["wdown"], l, t_fwd, ops)
        if l == 0:
            wts[1] = {**mixer_weights(*side[0][:4]), **ffn_weights(*side[0][4:])}
        acts.append(dict(xin=xin, k_all=k_all, vt_all=vt_all, xh1=xh1, rstd1=rstd1, h=h, ac=ac, cat=cat, mixed=mixed, probs=probs,
                         xh2=xh2, rstd2=rstd2, upg=upg, upv=upv))
        xin = xh2

    assert DEPTH == 2

    def halves_view(gs):
        return [g.reshape(1, N_CHIPS, 2, g.shape[1] // (2 * N_CHIPS), g.shape[2]) for g in gs]

    small = [None] * DEPTH
    small_packed = [None] * DEPTH
    small_gathered = [None] * DEPTH
    red_layers = [None] * DEPTH
    gz = target
    loss_sum = None
    g5_prev = None
    for l in reversed(range(DEPTH)):
        a, w = acts[l], wts[l]
        last = l == DEPTH - 1
        (dx1, dy, dug, duv, hm, vd2, vf), side = _ffn_bwd_d(
            gz, a["xh2"], a["rstd2"], a["upg"], a["upv"], pd, pf, w["wdown"], w["wup_t"], l, t_bwd, last,
            [_swap_op(g5_prev), _gather_devices_op(small_packed[l + 1])] if g5_prev else None)
        if last:
            loss_sum = vd2[VD_LOSS, 0]
        if g5_prev:
            small_gathered[l + 1] = side[1][0]
        parts_prev = _add_halves(g5_prev, side[0], place) if g5_prev else None
        (gw_up_t, gw_down), side = _ffn_bwd_w(a["xh1"], pd, dy, dug, duv, hm, l, t_wg,
                                              [_scatter_op(parts_prev)] if g5_prev else None)
        halves_prev = _sum_slots(parts_prev, side[0], place) if g5_prev else None
        g5_ffn = halves_view([gw_up_t.reshape(1, 2 * FF_P, D_MODEL), gw_down])
        ops = ([_sgather_op(halves_prev)] if g5_prev else []) + ([_swap_op(g5_ffn)] if l == 0 else [])
        (dx0, dh, dmix, vd1, va, dcw, dws, dbs, dkt, dv), side = _mixer_bwd_d(
            dx1, a["xh1"], a["rstd1"], a["h"], a["ac"], a["mixed"], a["probs"], pd, conv_w, pa, wstt,
            a["k_all"], a["vt_all"], w["wout"], w["win_t"], l, t_fwd, ops)
        if g5_prev:
            red_layers[l + 1] = side[0]
        parts_ffn = _add_halves(g5_ffn, side[-1], place) if l == 0 else None
        dws6 = jnp.where(tril, dws.reshape(6, CHUNK, CHUNK), 0.0)
        small[l] = [dcw[:CONV_K], va[VA_CONV_B], va[VA_LNA_G], va[VA_LNA_B], va[VA_LNV_G], va[VA_LNV_B], dws6,
                    dbs[:, :6].T, vd1[VD_LN_G], vd1[VD_LN_B],
                    _unpad_ff_cols(vf[VF_W0:VF_W0 + FFN_CONV_K]), _unpad_ff_cols(vf[VF_B]),
                    vd2[VD_LN_G], vd2[VD_LN_B]]
        small_packed[l] = _pack_small(small[l])
        ops = [_scatter_op(parts_ffn), _gather_devices_op(small_packed[l])] if l == 0 else None
        (gw_in_t, gw_out), side = _mixer_bwd_w(a["xin"], pd, dh, a["cat"], dmix, l, t_wg, ops)
        gw_mk, gw_mv = _kv_bwd(mem0, dkt, dv)
        g5_mix = halves_view([gw_in_t, gw_mk, gw_mv, gw_out])
        if l == 0:
            small_gathered[l] = side[1][0]
            halves_ffn = _sum_slots(parts_ffn, side[0], place)
            red_ffn, recv_mix = _run_side_ops([_sgather_op(halves_ffn), _swap_op(g5_mix)], "rs_tail_swap")
            parts_mix = _add_halves(g5_mix, recv_mix, place)
            halves_mix = _sum_slots(parts_mix, _run_side_ops([_scatter_op(parts_mix)], "rs_tail_chips")[0], place)
            red_mix = _run_side_ops([_sgather_op(halves_mix)], "rs_tail_gather")[0]
            red_layers[0] = red_mix + red_ffn
        else:
            g5_prev = g5_mix + g5_ffn
        gz = dx0
    grad_x = gz[None]

    def shard_grads(red):
        r = [f.reshape(-1, f.shape[-1]) for f in red]
        return dict(w_in=r[0].T, w_mk=r[1], w_mv=r[2], w_out=r[3], w_up=_unpad_ff_cols(r[4].T), w_down=r[5][:FF_Q])

    big_grads = [shard_grads(red_layers[l]) for l in range(DEPTH)]

    m_small = small_gathered[0].shape[1]
    small_red = jnp.concatenate([_sum_devices(g.reshape(8 * m_small, LANE), m_small) for g in small_gathered], axis=0)
    sg = _unpack_small(small_red)
    g_conv_a_w = lax.dynamic_slice_in_dim(sg["conv_a_w"], chip * (CONV_W // 4), CONV_W // 4, axis=2)
    g_conv_f_w = lax.dynamic_slice_in_dim(sg["conv_f_w"], chip * FF_Q, FF_Q, axis=2)

    loss = 0.5 / D_MODEL * lax.psum(loss_sum, ("x", "y", "c"))

    grads = dict(conv_a_w=g_conv_a_w, conv_a_b=sg["conv_a_b"], ln_a_g=sg["ln_a_g"], ln_a_b=sg["ln_a_b"],
                 ln_v_g=sg["ln_v_g"], ln_v_b=sg["ln_v_b"], w_s=sg["w_s"], b_s=sg["b_s"], ln1_g=sg["ln1_g"], ln1_b=sg["ln1_b"],
                 conv_f_w=g_conv_f_w, conv_f_b=sg["conv_f_b"], ln2_g=sg["ln2_g"], ln2_b=sg["ln2_b"])
    weights = dict(w_in=w_in, conv_a_w=conv_a_w, conv_a_b=conv_a_b, ln_a_g=ln_a_g, ln_a_b=ln_a_b, ln_v_g=ln_v_g,
                   ln_v_b=ln_v_b, w_s=w_s, b_s=b_s, w_mk=w_mk, w_mv=w_mv, w_out=w_out, ln1_g=ln1_g, ln1_b=ln1_b,
                   w_up=w_up, conv_f_w=conv_f_w, conv_f_b=conv_f_b, w_down=w_down, ln2_g=ln2_g, ln2_b=ln2_b)
    mom_m = dict(w_in=m_w_in, conv_a_w=m_conv_a_w, conv_a_b=m_conv_a_b, ln_a_g=m_ln_a_g, ln_a_b=m_ln_a_b, ln_v_g=m_ln_v_g,
                 ln_v_b=m_ln_v_b, w_s=m_w_s, b_s=m_b_s, w_mk=m_w_mk, w_mv=m_w_mv, w_out=m_w_out, ln1_g=m_ln1_g,
                 ln1_b=m_ln1_b, w_up=m_w_up, conv_f_w=m_conv_f_w, conv_f_b=m_conv_f_b, w_down=m_w_down, ln2_g=m_ln2_g,
                 ln2_b=m_ln2_b)
    mom_v = dict(w_in=v_w_in, conv_a_w=v_conv_a_w, conv_a_b=v_conv_a_b, ln_a_g=v_ln_a_g, ln_a_b=v_ln_a_b, ln_v_g=v_ln_v_g,
                 ln_v_b=v_ln_v_b, w_s=v_w_s, b_s=v_b_s, w_mk=v_w_mk, w_mv=v_w_mv, w_out=v_w_out, ln1_g=v_ln1_g,
                 ln1_b=v_ln1_b, w_up=v_w_up, conv_f_w=v_conv_f_w, conv_f_b=v_conv_f_b, w_down=v_w_down, ln2_g=v_ln2_g,
                 ln2_b=v_ln2_b)
    names = list(weights)
    big_names = ["w_in", "w_mk", "w_mv", "w_out", "w_up", "w_down"]
    delta, new_m, new_v = {}, {}, {}
    for n in big_names:
        grads[n], delta[n], new_m[n], new_v[n] = _adamw_layers(weights[n], [big_grads[l][n] for l in range(DEPTH)],
                                                               mom_m[n], mom_v[n], "adamw_" + n)
    small_names = [n for n in names if n not in big_names]
    ds, nms, nvs = _adamw_small([weights[n] for n in small_names], [grads[n] for n in small_names],
                                [mom_m[n] for n in small_names], [mom_v[n] for n in small_names])
    for n, d, nm, nv in zip(small_names, ds, nms, nvs):
        delta[n], new_m[n], new_v[n] = d, nm, nv

    return (loss, grad_x, *[grads[n] for n in names], *[delta[n] for n in names],
            *[new_m[n] for n in names], *[new_v[n] for n in names])
```

```python
import jax
import jax.numpy as jnp
from jax import lax
from jax.experimental import pallas as pl
from jax.experimental.pallas import tpu as pltpu

F32 = jnp.float32
BF16 = jnp.bfloat16

D_MODEL = 1024
DEPTH = 2
CONV_W = 384
GMLP_W = 384
XATTN_W = 256
XATTN_HEADS = 4
HEAD_DIM = 64
IN_W = 1792
CONV_K = 31
CHUNK = 128
N_MEM = 256
D_FF = 2752
FFN_CONV_K = 3
ALPHA = (2.0 * DEPTH) ** 0.25
LN_EPS = 1e-5
ATT_SCALE = 1.0 / 8.0
ADAM_LR, ADAM_B1, ADAM_B2, ADAM_EPS, ADAM_WD, ADAM_STEP = 0.001, 0.9, 0.999, 1e-08, 0.01, 10

N_CHIPS = 4
FF_Q = D_FF // N_CHIPS
FF_QP = 704
FF_H = 2 * FF_QP
FF_P = 4 * FF_QP
LANE = 128
CONV_HALO = 32
FFN_HALO = 8
BF16_ROWS = 16
VMEM_LIMIT = 60 * 1024 * 1024

MESH = pl.DeviceIdType.MESH
ANY = pl.BlockSpec(memory_space=pl.ANY)


def _cparams(sem=None, vmem=VMEM_LIMIT):
    kw = {"vmem_limit_bytes": vmem}
    if sem is not None:
        kw["dimension_semantics"] = sem
    return pltpu.CompilerParams(**kw)


def _row_tile(rows, row_bytes, limit=2 << 20, mult=BF16_ROWS):
    if rows * row_bytes <= limit:
        return rows
    best = None
    for cand in range(mult, rows, mult):
        if rows % cand == 0 and cand * row_bytes <= limit:
            best = cand
    assert best is not None, (rows, row_bytes)
    return best


def _const_spec(shape):
    nd = len(shape)
    return pl.BlockSpec(shape, lambda *_: (0,) * nd)


def _layer_spec(shape, *lead, resident=False):
    nd = len(shape)
    kw = {"pipeline_mode": pl.Buffered(1)} if resident else {}
    return pl.BlockSpec((None,) * len(lead) + tuple(shape), lambda *_: tuple(lead) + (0,) * nd, **kw)


def _sigmoid(x):
    return jax.nn.sigmoid(x)


def _gelu(x):
    return jax.nn.gelu(x)


def _gelu_and_grad(x):
    c = 0.7978845608028654
    a = 0.044715
    x2 = x * x
    t = jnp.tanh(c * (x + a * x * x2))
    h = 0.5 * (1.0 + t)
    return x * h, h + 0.5 * x * (1.0 - t * t) * c * (1.0 + 3.0 * a * x2)


def _ln_fwd(z):
    mu = jnp.mean(z, axis=-1, keepdims=True)
    zc = z - mu
    var = jnp.mean(zc * zc, axis=-1, keepdims=True)
    rstd = lax.rsqrt(var + LN_EPS)
    return zc * rstd, rstd


def _ln_bwd(dxh, xh, rstd):
    m1 = jnp.mean(dxh, axis=-1, keepdims=True)
    m2 = jnp.mean(dxh * xh, axis=-1, keepdims=True)
    return rstd * (dxh - m1 - xh * m2)


def _colsum(a):
    return jnp.sum(a, axis=0, keepdims=True)


def _dot(a, b):
    return jnp.dot(a, b, preferred_element_type=F32)


def _dot_tn(a, b):
    return lax.dot_general(a, b, (((0,), (0,)), ((), ())), preferred_element_type=F32)


def _dot_nt(a, b):
    return lax.dot_general(a, b, (((1,), (1,)), ((), ())), preferred_element_type=F32)


def _softmax_heads(sc):
    ps = []
    for hd in range(XATTN_HEADS):
        s = sc[:, hd * N_MEM:(hd + 1) * N_MEM]
        e = jnp.exp(s - jnp.max(s, axis=-1, keepdims=True))
        ps.append(e / jnp.sum(e, axis=-1, keepdims=True))
    return jnp.concatenate(ps, axis=1)


def _lane_lo(shape):
    return (lax.broadcasted_iota(jnp.int32, shape, len(shape) - 1) % LANE) < HEAD_DIM


def _spatial_mix(vnb, wst_ref, bst_ref, mix_ref, t):
    lo = _lane_lo((CHUNK, LANE))
    for n in range(t // CHUNK):
        rows = slice(n * CHUNK, (n + 1) * CHUNK)
        for j in range(GMLP_W // LANE):
            cols = slice(j * LANE, (j + 1) * LANE)
            r = _dot(wst_ref[j], vnb[rows, cols])
            mix_ref[rows, cols] = jnp.where(lo, r[:CHUNK], r[CHUNK:]) + bst_ref[:, cols]


def _kv_fwd(mem, w_mk, w_mv):
    def body(mem_ref, wk_ref, wv_ref, kt_ref, k_ref, v_ref, vt_ref):
        mb = mem_ref[...].astype(BF16)
        k = _dot(mb, wk_ref[...])
        v = _dot(mb, wv_ref[...])
        col = lax.broadcasted_iota(jnp.int32, (N_MEM, XATTN_W), 1) // HEAD_DIM
        ks = [jnp.where(col == hd, k, 0.0) for hd in range(XATTN_HEADS)]
        vs = [jnp.where(col == hd, v, 0.0) for hd in range(XATTN_HEADS)]
        k_ref[...] = jnp.concatenate(ks, axis=0).astype(BF16)
        v_ref[...] = jnp.concatenate(vs, axis=0).astype(BF16)
        kt_ref[...] = jnp.concatenate([x.T for x in ks], axis=1).astype(BF16)
        vt_ref[...] = jnp.concatenate([x.T for x in vs], axis=1).astype(BF16)

    wide = jax.ShapeDtypeStruct((XATTN_W, XATTN_HEADS * N_MEM), BF16)
    tall = jax.ShapeDtypeStruct((XATTN_HEADS * N_MEM, XATTN_W), BF16)
    wspec = _layer_spec((D_MODEL, XATTN_W), 0)
    return pl.pallas_call(body, name="kv_fwd", grid=(1,),
                          in_specs=[_const_spec((N_MEM, D_MODEL)), wspec, wspec],
                          out_specs=[_const_spec(wide.shape), _const_spec(tall.shape), _const_spec(tall.shape),
                                     _const_spec(wide.shape)],
                          out_shape=(wide, tall, tall, wide), compiler_params=_cparams(("arbitrary",)))(mem, w_mk, w_mv)


def _kv_bwd(mem, dkt_all, dv_all):
    def body(mem_ref, dkt_ref, dv_ref, gk_ref, gv_ref):
        col = lax.broadcasted_iota(jnp.int32, (N_MEM, XATTN_W), 1) // HEAD_DIM
        dk = jnp.zeros((N_MEM, XATTN_W), F32)
        dv = jnp.zeros((N_MEM, XATTN_W), F32)
        for hd in range(XATTN_HEADS):
            dk = dk + jnp.where(col == hd, dkt_ref[:, hd * N_MEM:(hd + 1) * N_MEM].T, 0.0)
            dv = dv + jnp.where(col == hd, dv_ref[hd * N_MEM:(hd + 1) * N_MEM, :], 0.0)
        mb = mem_ref[...].astype(BF16)
        gk_ref[0] = _dot_tn(mb, dk.astype(BF16))
        gv_ref[0] = _dot_tn(mb, dv.astype(BF16))

    out = jax.ShapeDtypeStruct((1, D_MODEL, XATTN_W), F32)
    return pl.pallas_call(body, name="kv_bwd", out_shape=(out, out), compiler_params=_cparams())(mem, dkt_all, dv_all)


def _my_place():
    x, y, c = lax.axis_index("x"), lax.axis_index("y"), lax.axis_index("c")
    chips = [(1 - x, y), (x, 1 - y), (1 - x, 1 - y)]
    return x, y, c, chips


class _SideOp:
    def __init__(self, ins, out_shapes, scratch, make, aliases=None):
        self.ins, self.out_shapes, self.scratch, self.make, self.aliases = list(ins), list(out_shapes), list(scratch), make, dict(aliases or {})


def _call_with_side(body, side_ops, *, name, grid, in_specs, out_specs, out_shape, scratch_shapes, operands, semantics):
    side_ops = list(side_ops or ())
    n_in, n_out, n_scr = len(in_specs), len(out_specs), len(scratch_shapes)
    s_ins = [a for op in side_ops for a in op.ins]
    s_outs = [o for op in side_ops for o in op.out_shapes]
    s_scr = [x for op in side_ops for x in op.scratch]
    aliases, oi, oo = {}, 0, 0
    for op in side_ops:
        for a, b in op.aliases.items():
            aliases[n_in + oi + a] = n_out + oo + b
        oi, oo = oi + len(op.ins), oo + len(op.out_shapes)

    def wrapped(*refs):
        ins, sins = refs[:n_in], refs[n_in:n_in + len(s_ins)]
        base = n_in + len(s_ins)
        outs, souts = refs[base:base + n_out], refs[base + n_out:base + n_out + len(s_outs)]
        base += n_out + len(s_outs)
        scr, sscr = refs[base:base + n_scr], refs[base + n_scr:]
        if side_ops:
            first = pl.program_id(0) == 0
            last = pl.program_id(0) == grid[0] - 1
            for d in range(1, len(grid)):
                first = jnp.logical_and(first, pl.program_id(d) == 0)
                last = jnp.logical_and(last, pl.program_id(d) == grid[d] - 1)
            hooks, a, b, c = [], 0, 0, 0
            for op in side_ops:
                hooks.append(op.make(sins[a:a + len(op.ins)], souts[b:b + len(op.out_shapes)], sscr[c:c + len(op.scratch)]))
                a, b, c = a + len(op.ins), b + len(op.out_shapes), c + len(op.scratch)

            @pl.when(first)
            def _():
                for start, _w in hooks:
                    start()

        body(*ins, *outs, *scr)
        if side_ops:
            @pl.when(last)
            def _():
                for _s, wait in hooks:
                    wait()

    res = pl.pallas_call(
        wrapped, name=name, grid=grid, in_specs=list(in_specs) + [ANY] * len(s_ins),
        out_specs=list(out_specs) + [ANY] * len(s_outs), out_shape=list(out_shape) + s_outs,
        scratch_shapes=list(scratch_shapes) + s_scr, input_output_aliases=aliases,
        compiler_params=_cparams(semantics),
    )(*operands, *s_ins)
    side_res, k = [], n_out
    for op in side_ops:
        side_res.append(list(res[k:k + len(op.out_shapes)]))
        k += len(op.out_shapes)
    return list(res[:n_out]), side_res


def _gather_chips_op(shards):
    n = len(shards)

    def make(ins, outs, scr):
        send_sems, recv_sems, local_sems = scr
        x, y, c, chips = _my_place()
        mej = 2 * x + y

        def remote(tn, k, slot):
            px, py = chips[k]
            return pltpu.make_async_remote_copy(
                src_ref=ins[tn], dst_ref=outs[tn].at[slot], send_sem=send_sems.at[tn, k],
                recv_sem=recv_sems.at[tn, k], device_id=(px, py, c), device_id_type=MESH)

        def local(tn):
            return pltpu.make_async_copy(ins[tn], outs[tn].at[mej], local_sems.at[tn])

        def start():
            for k in range(3):
                for tn in range(n):
                    remote(tn, k, mej).start()
            for tn in range(n):
                local(tn).start()

        def wait():
            for k, (px, py) in enumerate(chips):
                for tn in range(n):
                    remote(tn, k, 2 * px + py).wait_recv()
                    remote(tn, k, mej).wait_send()
            for tn in range(n):
                local(tn).wait()

        return start, wait

    return _SideOp(shards, [jax.ShapeDtypeStruct((N_CHIPS,) + a.shape, a.dtype) for a in shards],
                   [pltpu.SemaphoreType.DMA((n, 3)), pltpu.SemaphoreType.DMA((n, 3)), pltpu.SemaphoreType.DMA((n,))], make)


PA_CONV_B, PA_LNA_G, PA_LNA_B, PA_LNV_G, PA_LNV_B = 0, 1, 2, 3, 4
PD_GIN, PD_BIN, PD_G1, PD_B1, PD_G2, PD_B2 = 0, 1, 2, 3, 4, 5


def _row(ref, r):
    return ref[r:r + 1, :]


def _mixer_fwd(xin, pd, win_t, conv_w, pa, wst, bst, kt_all, v_all, w_out, l, t, side_ops=None):
    s = xin.shape[0]
    nt = s // t

    def body(x_ref, pd_ref, wint_ref, cw_ref, pa_ref, wst_ref, bst_ref, kt_ref, v_ref, wout_ref,
             xh_ref, rstd_ref, h_ref, ac_ref, cat_ref, mixed_ref, p_ref, cbuf, zbuf):
        i = pl.program_id(0)
        x = x_ref[...] * _row(pd_ref, PD_GIN) + _row(pd_ref, PD_BIN)
        h = _dot_nt(x.astype(BF16), wint_ref[...])
        h_ref[...] = h
        a1, a2 = h[:, 0:CONV_W], h[:, CONV_W:2 * CONV_W]
        hu, hv = h[:, 2 * CONV_W:2 * CONV_W + GMLP_W], h[:, 2 * CONV_W + GMLP_W:2 * CONV_W + 2 * GMLP_W]
        q = h[:, IN_W - XATTN_W:]

        @pl.when(i == 0)
        def _():
            cbuf[0:CONV_HALO, :] = jnp.zeros((CONV_HALO, CONV_W), F32)

        cbuf[CONV_HALO:CONV_HALO + t, :] = a1 * _sigmoid(a2)
        ac = jnp.zeros((t, CONV_W), F32) + _row(pa_ref, PA_CONV_B)
        for r in range(8):
            zr = jnp.zeros((t + 8, CONV_W), F32)
            for a in range(4):
                o = 8 * a + r
                if o < CONV_K:
                    k = CONV_K - 1 - o
                    zr = zr + cbuf[CONV_HALO - 8 - 8 * a:CONV_HALO - 8 - 8 * a + t + 8, :] * cw_ref[k:k + 1, :]
            if r == 0:
                ac = ac + zr[8:, :]
            else:
                zbuf[...] = zr
                ac = ac + zbuf[8 - r:8 - r + t, :]
        ac_ref[...] = ac
        cbuf[0:CONV_HALO, :] = cbuf[t:t + CONV_HALO, :]
        xh_a, _ = _ln_fwd(ac)
        an = xh_a * _row(pa_ref, PA_LNA_G) + _row(pa_ref, PA_LNA_B)
        a = an * _sigmoid(an)

        u = _gelu(hu)
        xh_v, _ = _ln_fwd(_gelu(hv))
        vn = xh_v * _row(pa_ref, PA_LNV_G) + _row(pa_ref, PA_LNV_B)
        _spatial_mix(vn.astype(BF16), wst_ref, bst_ref, mixed_ref, t)
        g = u * mixed_ref[...]

        p = _softmax_heads(_dot(q.astype(BF16), kt_ref[...]) * ATT_SCALE)
        p_ref[...] = p
        o = _dot(p.astype(BF16), v_ref[...])

        cat = jnp.concatenate([a, g, o], axis=1).astype(BF16)
        cat_ref[...] = cat
        z = ALPHA * x + _dot(cat, wout_ref[...])
        xh, rstd = _ln_fwd(z)
        xh_ref[...] = xh
        rstd_ref[...] = rstd

    tok = lambda w: pl.BlockSpec((t, w), lambda i: (i, 0))
    return _call_with_side(
        body, side_ops, name="mixer_fwd", grid=(nt,),
        in_specs=[tok(D_MODEL), _layer_spec((8, D_MODEL), l), _layer_spec((IN_W, D_MODEL), 0, resident=True),
                  _layer_spec((CONV_HALO, CONV_W), l), _layer_spec((8, CONV_W), l),
                  _layer_spec((3, 2 * CHUNK, CHUNK), l), _layer_spec((CHUNK, GMLP_W), l),
                  _const_spec((XATTN_W, XATTN_HEADS * N_MEM)), _const_spec((XATTN_HEADS * N_MEM, XATTN_W)),
                  _layer_spec((D_MODEL, D_MODEL), 0, resident=True)],
        out_specs=[tok(D_MODEL), tok(1), tok(IN_W), tok(CONV_W), tok(D_MODEL), tok(GMLP_W), tok(XATTN_HEADS * N_MEM)],
        out_shape=[jax.ShapeDtypeStruct((s, D_MODEL), F32), jax.ShapeDtypeStruct((s, 1), F32),
                   jax.ShapeDtypeStruct((s, IN_W), F32), jax.ShapeDtypeStruct((s, CONV_W), F32),
                   jax.ShapeDtypeStruct((s, D_MODEL), BF16), jax.ShapeDtypeStruct((s, GMLP_W), F32),
                   jax.ShapeDtypeStruct((s, XATTN_HEADS * N_MEM), F32)],
        scratch_shapes=[pltpu.VMEM((t + CONV_HALO, CONV_W), F32), pltpu.VMEM((t + 8, CONV_W), F32)],
        operands=(xin, pd, win_t, conv_w, pa, wst, bst, kt_all, v_all, w_out), semantics=("arbitrary",))


VD_LN_G, VD_LN_B, VD_LOSS = 0, 1, 2
VA_CONV_B, VA_LNA_G, VA_LNA_B, VA_LNV_G, VA_LNV_B = 0, 1, 2, 3, 4


def _mixer_bwd_d(gz, xh1, rstd1, h, ac, mixed, probs, pd, conv_w, pa, wstt, k_all, vt_all, w_out, win_t, l, t, side_ops=None):
    s = gz.shape[0]
    nt = s // t

    def body(gz_ref, xh_ref, rstd_ref, h_ref, ac_ref, mixed_ref, p_ref, pd_ref, cw_ref, pa_ref, wstt_ref,
             k_ref, vt_ref, wout_ref, wint_ref,
             dx_ref, dh_ref, dmix_ref, vd_ref, va_ref, dcw_ref, dws_ref, dbs_ref, dkt_ref, dv_ref,
             ebuf, dvnbuf, dbsacc, erbuf):
        i = pl.program_id(0)

        @pl.when(i == 0)
        def _():
            vd_ref[...] = jnp.zeros_like(vd_ref)
            va_ref[...] = jnp.zeros_like(va_ref)
            dcw_ref[...] = jnp.zeros_like(dcw_ref)
            dws_ref[...] = jnp.zeros_like(dws_ref)
            dbs_ref[...] = jnp.zeros_like(dbs_ref)
            dkt_ref[...] = jnp.zeros_like(dkt_ref)
            dv_ref[...] = jnp.zeros_like(dv_ref)
            dbsacc[...] = jnp.zeros_like(dbsacc)
            ebuf[t:t + CONV_HALO, :] = jnp.zeros((CONV_HALO, CONV_W), F32)

        gz_v = gz_ref[...]
        xh = xh_ref[...]
        vd_ref[VD_LN_G:VD_LN_G + 1, :] += _colsum(gz_v * xh)
        vd_ref[VD_LN_B:VD_LN_B + 1, :] += _colsum(gz_v)
        dz = _ln_bwd(gz_v * _row(pd_ref, PD_G1), xh, rstd_ref[...])
        dzb = dz.astype(BF16)
        dmix_ref[...] = dzb
        dcat = _dot_nt(dzb, wout_ref[...])
        d_a, d_g, d_o = dcat[:, 0:CONV_W], dcat[:, CONV_W:CONV_W + GMLP_W], dcat[:, CONV_W + GMLP_W:]

        h = h_ref[...]
        a1, a2 = h[:, 0:CONV_W], h[:, CONV_W:2 * CONV_W]
        hu, hv = h[:, 2 * CONV_W:2 * CONV_W + GMLP_W], h[:, 2 * CONV_W + GMLP_W:2 * CONV_W + 2 * GMLP_W]
        q = h[:, IN_W - XATTN_W:]

        xh_a, rstd_a = _ln_fwd(ac_ref[...])
        an = xh_a * _row(pa_ref, PA_LNA_G) + _row(pa_ref, PA_LNA_B)
        sig = _sigmoid(an)
        d_an = d_a * (sig * (1.0 + an * (1.0 - sig)))
        va_ref[VA_LNA_G:VA_LNA_G + 1, :] += _colsum(d_an * xh_a)
        va_ref[VA_LNA_B:VA_LNA_B + 1, :] += _colsum(d_an)
        dac = _ln_bwd(d_an * _row(pa_ref, PA_LNA_G), xh_a, rstd_a)
        va_ref[VA_CONV_B:VA_CONV_B + 1, :] += _colsum(dac)
        ebuf[0:t, :] = dac
        sg = _sigmoid(a2)
        glu = a1 * sg
        dglu = jnp.zeros((t, CONV_W), F32)
        for r in range(8):
            if r > 0:
                erbuf[...] = ebuf[r:r + t + 24, :]
            src = ebuf if r == 0 else erbuf
            for a in range(4):
                o = 8 * a + r
                if o < CONV_K:
                    k = CONV_K - 1 - o
                    ek = src[8 * a:8 * a + t, :]
                    dglu = dglu + ek * cw_ref[k:k + 1, :]
                    dcw_ref[k:k + 1, :] += _colsum(ek * glu)
        ebuf[t:t + CONV_HALO, :] = ebuf[0:CONV_HALO, :]
        da1 = dglu * sg
        da2 = dglu * a1 * sg * (1.0 - sg)

        u, du = _gelu_and_grad(hu)
        vg, dvg_dhv = _gelu_and_grad(hv)
        xh_v, rstd_v = _ln_fwd(vg)
        vn = xh_v * _row(pa_ref, PA_LNV_G) + _row(pa_ref, PA_LNV_B)
        vnb = vn.astype(BF16)
        dhu = d_g * mixed_ref[...] * du
        dm = d_g * u
        dmb = dm.astype(BF16)
        lo = _lane_lo((CHUNK, LANE))
        for n in range(t // CHUNK):
            rows = slice(n * CHUNK, (n + 1) * CHUNK)
            dbsacc[...] += dm[rows, :]
            for j in range(GMLP_W // LANE):
                cols = slice(j * LANE, (j + 1) * LANE)
                dm_blk = dmb[rows, cols]
                r = _dot(wstt_ref[j], dm_blk)
                dvnbuf[rows, cols] = jnp.where(lo, r[:CHUNK], r[CHUNK:])
                zero = jnp.zeros_like(dm_blk)
                st = jnp.concatenate([jnp.where(lo, dm_blk, zero), jnp.where(lo, zero, dm_blk)], axis=0)
                dws_ref[j] += _dot_nt(st, vnb[rows, cols])
        dvn = dvnbuf[...]
        va_ref[VA_LNV_G:VA_LNV_G + 1, :] += _colsum(dvn * xh_v)
        va_ref[VA_LNV_B:VA_LNV_B + 1, :] += _colsum(dvn)
        dhv = _ln_bwd(dvn * _row(pa_ref, PA_LNV_G), xh_v, rstd_v) * dvg_dhv

        qb = q.astype(BF16)
        p = p_ref[...]
        dob = d_o.astype(BF16)
        dp = _dot(dob, vt_ref[...])
        dss = []
        for hd in range(XATTN_HEADS):
            cs = slice(hd * N_MEM, (hd + 1) * N_MEM)
            ph, dph = p[:, cs], dp[:, cs]
            dss.append(ph * (dph - jnp.sum(ph * dph, axis=-1, keepdims=True)) * ATT_SCALE)
        dsb = jnp.concatenate(dss, axis=1).astype(BF16)
        dq = _dot(dsb, k_ref[...])
        dkt_ref[...] += _dot_tn(qb, dsb)
        dv_ref[...] += _dot_tn(p.astype(BF16), dob)

        dhb = jnp.concatenate([da1, da2, dhu, dhv, dq], axis=1).astype(BF16)
        dh_ref[...] = dhb
        dx_ref[...] = ALPHA * dz + _dot(dhb, wint_ref[...])

        @pl.when(i == nt - 1)
        def _():
            acc = dbsacc[...]
            head = lax.broadcasted_iota(jnp.int32, (CHUNK, GMLP_W), 1) // HEAD_DIM
            lane = lax.broadcasted_iota(jnp.int32, (CHUNK, LANE), 1)
            out = jnp.zeros((CHUNK, LANE), F32)
            for hd in range(GMLP_W // HEAD_DIM):
                sh = jnp.sum(jnp.where(head == hd, acc, 0.0), axis=1, keepdims=True)
                out = out + jnp.where(lane == hd, sh, 0.0)
            dbs_ref[...] = out

    rev = lambda w: pl.BlockSpec((t, w), lambda i: (nt - 1 - i, 0))
    out_shape = [
        jax.ShapeDtypeStruct((s, D_MODEL), F32), jax.ShapeDtypeStruct((s, IN_W), BF16),
        jax.ShapeDtypeStruct((s, D_MODEL), BF16),
        jax.ShapeDtypeStruct((8, D_MODEL), F32), jax.ShapeDtypeStruct((8, CONV_W), F32),
        jax.ShapeDtypeStruct((CONV_HALO, CONV_W), F32), jax.ShapeDtypeStruct((3, 2 * CHUNK, CHUNK), F32),
        jax.ShapeDtypeStruct((CHUNK, LANE), F32),
        jax.ShapeDtypeStruct((XATTN_W, XATTN_HEADS * N_MEM), F32), jax.ShapeDtypeStruct((XATTN_HEADS * N_MEM, XATTN_W), F32),
    ]
    out_specs = [rev(D_MODEL), rev(IN_W), rev(D_MODEL)] + [_const_spec(o.shape) for o in out_shape[3:]]
    return _call_with_side(
        body, side_ops, name="mixer_bwd_d", grid=(nt,),
        in_specs=[rev(D_MODEL), rev(D_MODEL), rev(1), rev(IN_W), rev(CONV_W), rev(GMLP_W), rev(XATTN_HEADS * N_MEM),
                  _layer_spec((8, D_MODEL), l), _layer_spec((CONV_HALO, CONV_W), l), _layer_spec((8, CONV_W), l),
                  _layer_spec((3, 2 * CHUNK, CHUNK), l),
                  _const_spec((XATTN_HEADS * N_MEM, XATTN_W)), _const_spec((XATTN_W, XATTN_HEADS * N_MEM)),
                  _layer_spec((D_MODEL, D_MODEL), 0, resident=True), _layer_spec((IN_W, D_MODEL), 0, resident=True)],
        out_specs=out_specs, out_shape=out_shape,
        scratch_shapes=[pltpu.VMEM((t + CONV_HALO, CONV_W), F32), pltpu.VMEM((t, GMLP_W), F32),
                        pltpu.VMEM((CHUNK, GMLP_W), F32), pltpu.VMEM((t + 24, CONV_W), F32)],
        operands=(gz, xh1, rstd1, h, ac, mixed, probs, pd, conv_w, pa, wstt, k_all, vt_all, w_out, win_t),
        semantics=("arbitrary",))


def _mixer_bwd_w(xin, pd, dh, cat, dmix, l, t, side_ops=None):
    s = xin.shape[0]
    nt = s // t

    def body(x_ref, pd_ref, dh_ref, cat_ref, dmix_ref, dwin_ref, dwout_ref):
        @pl.when(pl.program_id(0) == 0)
        def _():
            dwin_ref[...] = jnp.zeros_like(dwin_ref)
            dwout_ref[...] = jnp.zeros_like(dwout_ref)

        xb = (x_ref[...] * _row(pd_ref, PD_GIN) + _row(pd_ref, PD_BIN)).astype(BF16)
        dwin_ref[...] += _dot_tn(dh_ref[...], xb)
        dwout_ref[...] += _dot_tn(cat_ref[...], dmix_ref[...])

    tok = lambda w: pl.BlockSpec((t, w), lambda i: (i, 0))
    return _call_with_side(
        body, side_ops, name="mixer_bwd_w", grid=(nt,),
        in_specs=[tok(D_MODEL), _layer_spec((8, D_MODEL), l), tok(IN_W), tok(D_MODEL), tok(D_MODEL)],
        out_specs=[_layer_spec((IN_W, D_MODEL), 0, resident=True), _layer_spec((D_MODEL, D_MODEL), 0, resident=True)],
        out_shape=[jax.ShapeDtypeStruct((1, IN_W, D_MODEL), F32), jax.ShapeDtypeStruct((1, D_MODEL, D_MODEL), F32)],
        scratch_shapes=[], operands=(xin, pd, dh, cat, dmix), semantics=("arbitrary",))


PF_W0, PF_B = 0, 3


def _ffn_fwd(xh1, pd, wup_t, pf, w_d, l, t, side_ops=None):
    s = xh1.shape[0]
    nt = s // t

    def body(xh_ref, pd_ref, wg_ref, wv_ref, pf_ref, wd_ref, xh2_ref, rstd_ref, upg_ref, upv_ref, fbuf):
        i = pl.program_id(0)

        @pl.when(i == 0)
        def _():
            fbuf[0:FFN_HALO, :] = jnp.zeros((FFN_HALO, FF_P), F32)

        x1 = xh_ref[...] * _row(pd_ref, PD_G1) + _row(pd_ref, PD_B1)
        xb = x1.astype(BF16)
        y = jnp.zeros((t, D_MODEL), F32)
        for hf in range(2):
            cs = slice(hf * FF_H, (hf + 1) * FF_H)
            ug = _dot_nt(xb, wg_ref[cs, :])
            uv = _dot_nt(xb, wv_ref[cs, :])
            upg_ref[:, cs] = ug
            upv_ref[:, cs] = uv
            fbuf[FFN_HALO:FFN_HALO + t, cs] = ug
            gate = jnp.zeros((t, FF_H), F32) + pf_ref[PF_B:PF_B + 1, cs]
            for k in range(FFN_CONV_K):
                off = FFN_HALO - (FFN_CONV_K - 1) + k
                gate = gate + fbuf[off:off + t, cs] * pf_ref[PF_W0 + k:PF_W0 + k + 1, cs]
            fbuf[0:FFN_HALO, cs] = fbuf[t:t + FFN_HALO, cs]
            hm = gate * _sigmoid(gate) * uv
            y = y + _dot(hm.astype(BF16), wd_ref[cs, :])
        xh2, rstd = _ln_fwd(ALPHA * x1 + y)
        xh2_ref[...] = xh2
        rstd_ref[...] = rstd

    tok = lambda w: pl.BlockSpec((t, w), lambda i: (i, 0))
    return _call_with_side(
        body, side_ops, name="ffn_fwd", grid=(nt,),
        in_specs=[tok(D_MODEL), _layer_spec((8, D_MODEL), l),
                  _layer_spec((FF_P, D_MODEL), 0, 0, resident=True), _layer_spec((FF_P, D_MODEL), 0, 1, resident=True),
                  _layer_spec((8, FF_P), l), _layer_spec((FF_P, D_MODEL), 0, resident=True)],
        out_specs=[tok(D_MODEL), tok(1), tok(FF_P), tok(FF_P)],
        out_shape=[jax.ShapeDtypeStruct((s, D_MODEL), F32), jax.ShapeDtypeStruct((s, 1), F32),
                   jax.ShapeDtypeStruct((s, FF_P), F32), jax.ShapeDtypeStruct((s, FF_P), F32)],
        scratch_shapes=[pltpu.VMEM((t + FFN_HALO, FF_P), F32)],
        operands=(xh1, pd, wup_t, wup_t, pf, w_d), semantics=("arbitrary",))


VF_W0, VF_B = 0, 3


def _ffn_bwd_d(gz_or_target, xh2, rstd2, upg, upv, pd, pf, w_d, wup_t, l, t, last, side_ops=None):
    s = xh2.shape[0]
    nt = s // t
    hb = t // FFN_HALO

    def body(gz_ref, xh2_ref, rstd_ref, upg_ref, halo_ref, upv_ref, pd_ref, pf_ref, wd_ref, wg_ref, wv_ref,
             dx_ref, dy_ref, dug_ref, duv_ref, hm_ref, vd_ref, vf_ref, gbuf, ebuf, s1buf, s2buf):
        i = pl.program_id(0)
        first_tile = i == nt - 1

        @pl.when(i == 0)
        def _():
            vd_ref[...] = jnp.zeros_like(vd_ref)
            vf_ref[...] = jnp.zeros_like(vf_ref)
            ebuf[t:t + FFN_HALO, :] = jnp.zeros((FFN_HALO, FF_P), F32)

        xh2_v = xh2_ref[...]
        if last:
            diff = xh2_v * _row(pd_ref, PD_G2) + _row(pd_ref, PD_B2) - gz_ref[...]
            vd_ref[VD_LOSS:VD_LOSS + 1, :] += _colsum(diff * diff)
            gz_v = diff * (1.0 / D_MODEL)
        else:
            gz_v = gz_ref[...]
        vd_ref[VD_LN_G:VD_LN_G + 1, :] += _colsum(gz_v * xh2_v)
        vd_ref[VD_LN_B:VD_LN_B + 1, :] += _colsum(gz_v)
        dz = _ln_bwd(gz_v * _row(pd_ref, PD_G2), xh2_v, rstd_ref[...])
        dyb = dz.astype(BF16)
        dy_ref[...] = dyb
        dx = ALPHA * dz
        for hf in range(2):
            cs = slice(hf * FF_H, (hf + 1) * FF_H)
            ug = upg_ref[:, cs]
            uv = upv_ref[:, cs]
            halo = halo_ref[:, cs]
            gbuf[0:FFN_HALO, :] = jnp.where(first_tile, jnp.zeros_like(halo), halo)
            gbuf[FFN_HALO:FFN_HALO + t, :] = ug
            s1buf[...] = gbuf[FFN_HALO - 1:FFN_HALO - 1 + t, :]
            s2buf[...] = gbuf[FFN_HALO - 2:FFN_HALO - 2 + t, :]
            ug1 = s1buf[...]
            ug2 = s2buf[...]
            gate = (pf_ref[PF_B:PF_B + 1, cs] + ug2 * pf_ref[PF_W0:PF_W0 + 1, cs] + ug1 * pf_ref[PF_W0 + 1:PF_W0 + 2, cs]
                    + ug * pf_ref[PF_W0 + 2:PF_W0 + 3, cs])
            sig = _sigmoid(gate)
            sl = gate * sig
            hm_ref[:, cs] = sl * uv
            dhm = _dot_nt(dyb, wd_ref[cs, :])
            duv = dhm * sl
            dgate = dhm * uv * (sig * (1.0 + gate * (1.0 - sig)))
            vf_ref[VF_B:VF_B + 1, cs] += _colsum(dgate)
            vf_ref[VF_W0:VF_W0 + 1, cs] += _colsum(dgate * ug2)
            vf_ref[VF_W0 + 1:VF_W0 + 2, cs] += _colsum(dgate * ug1)
            vf_ref[VF_W0 + 2:VF_W0 + 3, cs] += _colsum(dgate * ug)
            ebuf[0:t, cs] = dgate
            dug = (ebuf[2:2 + t, cs] * pf_ref[PF_W0:PF_W0 + 1, cs] + ebuf[1:1 + t, cs] * pf_ref[PF_W0 + 1:PF_W0 + 2, cs]
                   + dgate * pf_ref[PF_W0 + 2:PF_W0 + 3, cs])
            ebuf[t:t + FFN_HALO, cs] = ebuf[0:FFN_HALO, cs]
            dugb = dug.astype(BF16)
            duvb = duv.astype(BF16)
            dug_ref[:, cs] = dugb
            duv_ref[:, cs] = duvb
            dx = dx + _dot(dugb, wg_ref[cs, :]) + _dot(duvb, wv_ref[cs, :])
        dx_ref[...] = dx

        if last:
            @pl.when(i == nt - 1)
            def _():
                tot = jnp.sum(vd_ref[VD_LOSS:VD_LOSS + 1, :], axis=1, keepdims=True)
                vd_ref[VD_LOSS:VD_LOSS + 1, :] = jnp.zeros((1, D_MODEL), F32) + tot

    rev = lambda w: pl.BlockSpec((t, w), lambda i: (nt - 1 - i, 0))
    halo_spec = pl.BlockSpec((FFN_HALO, FF_P), lambda i: (jnp.maximum((nt - 1 - i) * hb - 1, 0), 0))
    out_shape = [jax.ShapeDtypeStruct((s, D_MODEL), F32), jax.ShapeDtypeStruct((s, D_MODEL), BF16),
                 jax.ShapeDtypeStruct((s, FF_P), BF16), jax.ShapeDtypeStruct((s, FF_P), BF16),
                 jax.ShapeDtypeStruct((s, FF_P), F32),
                 jax.ShapeDtypeStruct((8, D_MODEL), F32), jax.ShapeDtypeStruct((8, FF_P), F32)]
    return _call_with_side(
        body, side_ops, name="ffn_bwd_d_last" if last else "ffn_bwd_d", grid=(nt,),
        in_specs=[rev(D_MODEL), rev(D_MODEL), rev(1), rev(FF_P), halo_spec, rev(FF_P),
                  _layer_spec((8, D_MODEL), l), _layer_spec((8, FF_P), l),
                  _layer_spec((FF_P, D_MODEL), 0, resident=True),
                  _layer_spec((FF_P, D_MODEL), 0, 0, resident=True), _layer_spec((FF_P, D_MODEL), 0, 1, resident=True)],
        out_specs=[rev(D_MODEL), rev(D_MODEL), rev(FF_P), rev(FF_P), rev(FF_P),
                   _const_spec((8, D_MODEL)), _const_spec((8, FF_P))],
        out_shape=out_shape,
        scratch_shapes=[pltpu.VMEM((t + FFN_HALO, FF_H), F32), pltpu.VMEM((t + FFN_HALO, FF_P), F32),
                        pltpu.VMEM((t, FF_H), F32), pltpu.VMEM((t, FF_H), F32)],
        operands=(gz_or_target, xh2, rstd2, upg, upg, upv, pd, pf, w_d, wup_t, wup_t), semantics=("arbitrary",))


def _ffn_bwd_w(xh1, pd, dy, dug, duv, hm, l, t, side_ops=None):
    s = xh1.shape[0]
    nt = s // t

    def body(xh_ref, pd_ref, dy_ref, dug_ref, duv_ref, hm_ref, dwup_ref, dwd_ref):
        @pl.when(pl.program_id(1) == 0)
        def _():
            dwup_ref[...] = jnp.zeros_like(dwup_ref)
            dwd_ref[...] = jnp.zeros_like(dwd_ref)

        xb = (xh_ref[...] * _row(pd_ref, PD_G1) + _row(pd_ref, PD_B1)).astype(BF16)
        dwup_ref[0] += _dot_tn(dug_ref[...], xb)
        dwup_ref[1] += _dot_tn(duv_ref[...], xb)
        dwd_ref[...] += _dot_tn(hm_ref[...].astype(BF16), dy_ref[...])

    tok = lambda w: pl.BlockSpec((t, w), lambda c, i: (i, 0))
    half = pl.BlockSpec((t, FF_H), lambda c, i: (i, c))
    return _call_with_side(
        body, side_ops, name="ffn_bwd_w", grid=(2, nt),
        in_specs=[tok(D_MODEL), pl.BlockSpec((None, 8, D_MODEL), lambda c, i: (l, 0, 0)), tok(D_MODEL), half, half, half],
        out_specs=[pl.BlockSpec((None, 2, FF_H, D_MODEL), lambda c, i: (0, 0, c, 0), pipeline_mode=pl.Buffered(1)),
                   pl.BlockSpec((None, FF_H, D_MODEL), lambda c, i: (0, c, 0), pipeline_mode=pl.Buffered(1))],
        out_shape=[jax.ShapeDtypeStruct((1, 2, FF_P, D_MODEL), F32), jax.ShapeDtypeStruct((1, FF_P, D_MODEL), F32)],
        scratch_shapes=[], operands=(xh1, pd, dy, dug, duv, hm), semantics=("arbitrary", "arbitrary"))


def _adamw_math(w, g, m, v):
    nm = ADAM_B1 * m + (1.0 - ADAM_B1) * g
    nv = ADAM_B2 * v + (1.0 - ADAM_B2) * (g * g)
    m_hat = nm / (1.0 - ADAM_B1 ** ADAM_STEP)
    v_hat = nv / (1.0 - ADAM_B2 ** ADAM_STEP)
    return -ADAM_LR * (m_hat / (jnp.sqrt(v_hat) + ADAM_EPS) + ADAM_WD * w), nm, nv


def _adamw_layers(w, gs, m, v, name):
    shp = w.shape
    _, rows, cols = shp
    tr = _row_tile(rows, cols * 4, mult=8)
    nb = rows // tr

    def body(w_ref, g0_ref, g1_ref, m_ref, v_ref, g_ref, d_ref, nm_ref, nv_ref):
        g = jnp.where(pl.program_id(0) == 0, g0_ref[...], g1_ref[...])
        g_ref[...] = g
        d_ref[...], nm_ref[...], nv_ref[...] = _adamw_math(w_ref[...], g, m_ref[...], v_ref[...])

    stacked = pl.BlockSpec((tr, cols), lambda l, i: (l * nb + i, 0))
    single = pl.BlockSpec((tr, cols), lambda l, i: (i, 0))
    sh = jax.ShapeDtypeStruct((DEPTH * rows, cols), F32)
    flat = lambda a: a.reshape(DEPTH * rows, cols)
    outs = pl.pallas_call(body, name=name, grid=(DEPTH, nb), in_specs=[stacked, single, single, stacked, stacked],
                          out_specs=[stacked] * 4, out_shape=[sh] * 4,
                          compiler_params=_cparams(("arbitrary", "arbitrary")))(flat(w), gs[0], gs[1], flat(m), flat(v))
    return [o.reshape(shp) for o in outs]


def _adamw_small(ws, gs, ms, vs):
    n = len(ws)

    def body(*refs):
        w_refs, g_refs, m_refs, v_refs = refs[:n], refs[n:2 * n], refs[2 * n:3 * n], refs[3 * n:4 * n]
        d_refs, nm_refs, nv_refs = refs[4 * n:5 * n], refs[5 * n:6 * n], refs[6 * n:7 * n]
        for k in range(n):
            d_refs[k][...], nm_refs[k][...], nv_refs[k][...] = _adamw_math(w_refs[k][...], g_refs[k][...], m_refs[k][...],
                                                                             v_refs[k][...])

    shapes = [jax.ShapeDtypeStruct(w.shape, F32) for w in ws]
    outs = pl.pallas_call(body, name="adamw_small", out_shape=shapes * 3, compiler_params=_cparams())(*ws, *gs, *ms, *vs)
    return outs[:n], outs[n:2 * n], outs[2 * n:]


def _all_gather_chips(tensors, name):
    n = len(tensors)
    halves = [a.shape[1] // 2 for a in tensors]

    def body(*refs):
        x_refs, out_refs = refs[:n], refs[n:2 * n]
        send_sems, recv_sems, local_sems = refs[2 * n:]
        x, y, c, chips = _my_place()
        me, sibling, mej = (x, y, c), (x, y, 1 - c), 2 * x + y

        def rows(tn, px, py, pc):
            return out_refs[tn].at[:, 2 * px + py, pl.ds(pc * halves[tn], halves[tn]), :]

        def copy(tn, k, block, to, src=None):
            return pltpu.make_async_remote_copy(
                src_ref=rows(tn, *block) if src is None else src, dst_ref=rows(tn, *block),
                send_sem=send_sems.at[tn, k], recv_sem=recv_sems.at[tn, k], device_id=to, device_id_type=MESH)

        mine_src = [x_refs[tn].at[:, pl.ds(c * halves[tn], halves[tn]), :] for tn in range(n)]
        mine = [pltpu.make_async_copy(mine_src[tn], rows(tn, *me), local_sems.at[tn]) for tn in range(n)]
        first = []
        for j, chip in enumerate(chips):
            first += [copy(tn, 1 + j, me, (*chip, c), src=mine_src[tn]) for tn in range(n)]
        first += [copy(tn, 0, me, sibling, src=mine_src[tn]) for tn in range(n)]
        for cp in first + mine:
            cp.start()
        passed = []
        for j, chip in enumerate(chips):
            for tn in range(n):
                copy(tn, 1 + j, (*chip, c), me).wait_recv()
                fwd = copy(tn, 4 + j, (*chip, c), sibling)
                fwd.start()
                passed.append(fwd)
        for tn in range(n):
            copy(tn, 0, sibling, me).wait_recv()
            for j, chip in enumerate(chips):
                copy(tn, 4 + j, (*chip, 1 - c), me).wait_recv()
        for cp in first + passed:
            cp.wait_send()
        for cp in mine:
            cp.wait()

    return pl.pallas_call(
        body, name=name,
        out_shape=[jax.ShapeDtypeStruct((a.shape[0], N_CHIPS) + a.shape[1:], a.dtype) for a in tensors],
        in_specs=[ANY] * n, out_specs=[ANY] * n,
        scratch_shapes=[pltpu.SemaphoreType.DMA((n, 7)), pltpu.SemaphoreType.DMA((n, 7)), pltpu.SemaphoreType.DMA((n,))],
    )(*tensors)


def _swap_op(g5s):
    n = len(g5s)

    def make(ins, outs, scr):
        send_sems, recv_sems = scr
        x, y, c, _ = _my_place()

        def copies():
            return [pltpu.make_async_remote_copy(
                src_ref=ins[tn].at[:, :, 1 - c], dst_ref=outs[tn], send_sem=send_sems.at[tn], recv_sem=recv_sems.at[tn],
                device_id=(x, y, 1 - c), device_id_type=MESH) for tn in range(n)]

        def start():
            for cp in copies():
                cp.start()

        def wait():
            for cp in copies():
                cp.wait()

        return start, wait

    return _SideOp(g5s, [jax.ShapeDtypeStruct(g.shape[:2] + g.shape[3:], g.dtype) for g in g5s],
                   [pltpu.SemaphoreType.DMA((n,)), pltpu.SemaphoreType.DMA((n,))], make)


def _scatter_op(parts):
    n = len(parts)

    def make(ins, outs, scr):
        send_sems, recv_sems = scr
        x, y, c, chips = _my_place()

        def copies():
            return [pltpu.make_async_remote_copy(
                src_ref=ins[tn].at[:, 2 * px + py], dst_ref=outs[tn].at[:, k],
                send_sem=send_sems.at[tn, k], recv_sem=recv_sems.at[tn, k],
                device_id=(px, py, c), device_id_type=MESH) for k, (px, py) in enumerate(chips) for tn in range(n)]

        def start():
            for cp in copies():
                cp.start()

        def wait():
            for cp in copies():
                cp.wait()

        return start, wait

    return _SideOp(parts, [jax.ShapeDtypeStruct((p.shape[0], 3) + p.shape[2:], p.dtype) for p in parts],
                   [pltpu.SemaphoreType.DMA((n, 3)), pltpu.SemaphoreType.DMA((n, 3))], make)


def _sgather_op(fs):
    n = len(fs)

    def make(ins, outs, scr):
        send_sems, recv_sems = scr
        x, y, c, _ = _my_place()

        def copy(tn, dst_half):
            return pltpu.make_async_remote_copy(
                src_ref=outs[tn].at[:, c], dst_ref=outs[tn].at[:, dst_half], send_sem=send_sems.at[tn],
                recv_sem=recv_sems.at[tn], device_id=(x, y, 1 - c), device_id_type=MESH)

        def start():
            for tn in range(n):
                copy(tn, c).start()

        def wait():
            for tn in range(n):
                copy(tn, 1 - c).wait_recv()
                copy(tn, c).wait_send()

        return start, wait

    return _SideOp(fs, [jax.ShapeDtypeStruct(f.shape, f.dtype) for f in fs],
                   [pltpu.SemaphoreType.DMA((n,)), pltpu.SemaphoreType.DMA((n,))], make, aliases={tn: tn for tn in range(n)})


def _run_side_ops(ops, name):
    return _call_with_side(lambda: None, ops, name=name, grid=(1,), in_specs=[], out_specs=[], out_shape=[],
                           scratch_shapes=[], operands=(), semantics=("arbitrary",))[1]


def _gather_devices_op(xs):
    def make(ins, outs, scr):
        send_sems, recv_sems, local_sem = scr
        x, y, c, chips = _my_place()
        peers = [(x, y, 1 - c)] + [(px, py, pc) for (px, py) in chips for pc in (c, 1 - c)]
        me = 4 * x + 2 * y + c

        def copy(k, slot):
            return pltpu.make_async_remote_copy(
                src_ref=ins[0], dst_ref=outs[0].at[slot], send_sem=send_sems.at[k], recv_sem=recv_sems.at[k],
                device_id=peers[k], device_id_type=MESH)

        def local():
            return pltpu.make_async_copy(ins[0], outs[0].at[me], local_sem)

        def start():
            for k in range(7):
                copy(k, me).start()
            local().start()

        def wait():
            for k, (px, py, pc) in enumerate(peers):
                copy(k, 4 * px + 2 * py + pc).wait_recv()
                copy(k, me).wait_send()
            local().wait()

        return start, wait

    return _SideOp([xs], [jax.ShapeDtypeStruct((8,) + xs.shape, xs.dtype)],
                   [pltpu.SemaphoreType.DMA((7,)), pltpu.SemaphoreType.DMA((7,)), pltpu.SemaphoreType.DMA], make)


def _add_halves(gs, recvs, place):
    n = len(gs)

    def body(place_ref, *refs):
        g_refs, r_refs, o_refs = refs[:n], refs[n:2 * n], refs[2 * n:]
        for tn in range(n):
            o_refs[tn][...] = (g_refs[tn][...] + r_refs[tn][...]).astype(BF16)

    def gspec(g):
        return pl.BlockSpec((None, None, None) + g.shape[3:], lambda l, j, p: (l, j, p[1], 0, 0))

    def rspec(r):
        return pl.BlockSpec((None, None) + r.shape[2:], lambda l, j, p: (l, j, 0, 0))

    grid_spec = pltpu.PrefetchScalarGridSpec(
        num_scalar_prefetch=1, grid=(gs[0].shape[0], N_CHIPS),
        in_specs=[gspec(g) for g in gs] + [rspec(r) for r in recvs], out_specs=[rspec(r) for r in recvs])
    return pl.pallas_call(body, name="rs_add", grid_spec=grid_spec,
                          out_shape=[jax.ShapeDtypeStruct(r.shape, BF16) for r in recvs],
                          compiler_params=_cparams(("arbitrary", "arbitrary")))(place, *gs, *recvs)


def _sum_slots(parts, slots, place):
    n = len(parts)

    def body(place_ref, *refs):
        p_refs, s_refs, o_refs = refs[:n], refs[n:2 * n], refs[2 * n:]
        for tn in range(n):
            acc = p_refs[tn][...].astype(F32)
            for k in range(3):
                acc = acc + s_refs[tn][k].astype(F32)
            o_refs[tn][...] = acc

    def pspec(p):
        return pl.BlockSpec((None, None) + p.shape[2:], lambda l, pl_: (l, pl_[0], 0, 0))

    def sspec(sl):
        return pl.BlockSpec((None,) + sl.shape[1:], lambda l, pl_: (l, 0, 0, 0))

    def ospec(p):
        return pl.BlockSpec((None, None) + p.shape[2:], lambda l, pl_: (l, pl_[1], 0, 0))

    grid_spec = pltpu.PrefetchScalarGridSpec(
        num_scalar_prefetch=1, grid=(parts[0].shape[0],),
        in_specs=[pspec(p) for p in parts] + [sspec(sl) for sl in slots], out_specs=[ospec(p) for p in parts])
    return pl.pallas_call(body, name="rs_sum", grid_spec=grid_spec,
                          out_shape=[jax.ShapeDtypeStruct((p.shape[0], 2) + p.shape[2:], F32) for p in parts],
                          compiler_params=_cparams(("arbitrary",)))(place, *parts, *slots)


def _sum_devices(gathered):
    m_per = gathered[0].shape[1]

    def body(*refs):
        o_ref = refs[-1]
        for l, g_ref in enumerate(refs[:-1]):
            acc = g_ref[0]
            for d in range(1, 8):
                acc = acc + g_ref[d]
            o_ref[l * m_per:(l + 1) * m_per, :] = acc

    return pl.pallas_call(body, name="small_sum", out_shape=jax.ShapeDtypeStruct((len(gathered) * m_per, LANE), F32),
                          compiler_params=_cparams())(*gathered)


def _pad_ff_cols(a):
    lead = a.shape[:-1]
    n = a.shape[-1] // FF_Q
    a = a.reshape(*lead, n, FF_Q)
    a = jnp.pad(a, [(0, 0)] * len(lead) + [(0, 0), (0, FF_QP - FF_Q)])
    return a.reshape(*lead, n * FF_QP)


def _unpad_ff_cols(a):
    lead = a.shape[:-1]
    n = a.shape[-1] // FF_QP
    return a.reshape(*lead, n, FF_QP)[..., :FF_Q].reshape(*lead, n * FF_Q)


def _pack_small(parts):
    flat = jnp.concatenate([p.reshape(-1) for p in parts])
    return flat.reshape(-1, LANE)


SMALL_SHAPES = [("conv_a_w", (CONV_K, CONV_W)), ("conv_a_b", (CONV_W,)), ("ln_a_g", (CONV_W,)), ("ln_a_b", (CONV_W,)),
                ("ln_v_g", (GMLP_W,)), ("ln_v_b", (GMLP_W,)), ("w_s", (6, CHUNK, CHUNK)), ("b_s", (6, CHUNK)),
                ("ln1_g", (D_MODEL,)), ("ln1_b", (D_MODEL,)), ("conv_f_w", (FFN_CONV_K, D_FF)), ("conv_f_b", (D_FF,)),
                ("ln2_g", (D_MODEL,)), ("ln2_b", (D_MODEL,))]


def _unpack_small(flat2d):
    flat = flat2d.reshape(DEPTH, -1)
    out, o = {}, 0
    for name, shp in SMALL_SHAPES:
        n = 1
        for d in shp:
            n *= d
        out[name] = flat[:, o:o + n].reshape((DEPTH,) + shp)
        o += n
    return out


def _rows8(rows):
    blk = jnp.stack(rows, axis=1)
    return jnp.pad(blk, ((0, 0), (0, 8 - len(rows)), (0, 0)))


def kernel(x, mem, w_in, conv_a_w, conv_a_b, ln_a_g, ln_a_b, ln_v_g, ln_v_b, w_s, b_s, w_mk, w_mv, w_out, ln1_g, ln1_b, w_up, conv_f_w, conv_f_b, w_down, ln2_g, ln2_b, loss_target, m_w_in, m_conv_a_w, m_conv_a_b, m_ln_a_g, m_ln_a_b, m_ln_v_g, m_ln_v_b, m_w_s, m_b_s, m_w_mk, m_w_mv, m_w_out, m_ln1_g, m_ln1_b, m_w_up, m_conv_f_w, m_conv_f_b, m_w_down, m_ln2_g, m_ln2_b, v_w_in, v_conv_a_w, v_conv_a_b, v_ln_a_g, v_ln_a_b, v_ln_v_g, v_ln_v_b, v_w_s, v_b_s, v_w_mk, v_w_mv, v_w_out, v_ln1_g, v_ln1_b, v_w_up, v_conv_f_w, v_conv_f_b, v_w_down, v_ln2_g, v_ln2_b):
    seq = x.shape[1]
    t_fwd = min(512, seq)
    t_bwd = min(256, seq)
    t_wg = min(1024, seq)
    chip = 2 * lax.axis_index("x") + lax.axis_index("y")
    core = lax.axis_index("c")
    place = jnp.stack([chip, core]).astype(jnp.int32)
    x0 = x[0]
    mem0 = mem[0]
    target = loss_target[0]

    sh_in = w_in.transpose(0, 2, 1).astype(BF16)
    sh_mk, sh_mv, sh_out = w_mk.astype(BF16), w_mv.astype(BF16), w_out.astype(BF16)
    sh_up = _pad_ff_cols(w_up).transpose(0, 2, 1).astype(BF16)
    sh_dn = jnp.pad(w_down, ((0, 0), (0, FF_QP - FF_Q), (0, 0))).astype(BF16)

    def mixer_weights(g_in, g_mk, g_mv, g_out):
        return dict(win_t=g_in.reshape(1, IN_W, D_MODEL), wmk=g_mk.reshape(1, D_MODEL, XATTN_W),
                    wmv=g_mv.reshape(1, D_MODEL, XATTN_W), wout=g_out.reshape(1, D_MODEL, D_MODEL))

    def ffn_weights(g_up, g_dn):
        return dict(wup_t=g_up.reshape(1, 2, FF_P, D_MODEL), wdown=g_dn.reshape(1, FF_P, D_MODEL))

    n_ca = conv_a_w.size
    small_w = _pack_small([conv_a_w, conv_f_w, jnp.zeros((2 * 80 * LANE - n_ca - conv_f_w.size,), F32)])[None]
    *g_mixer0, small_g = _all_gather_chips([sh_in[:1], sh_mk[:1], sh_mv[:1], sh_out[:1], small_w], "ag_mixer0")
    wts = [mixer_weights(*g_mixer0), None]
    small_g = small_g.reshape(N_CHIPS, -1)
    conv_a_full = small_g[:, :n_ca].reshape(N_CHIPS, DEPTH, CONV_K, CONV_W // 4).transpose(1, 2, 0, 3).reshape(DEPTH, CONV_K, CONV_W)
    conv_f_full = small_g[:, n_ca:n_ca + conv_f_w.size].reshape(N_CHIPS, DEPTH, FFN_CONV_K, FF_Q).transpose(1, 2, 0, 3).reshape(DEPTH, FFN_CONV_K, D_FF)

    tril = jnp.tril(jnp.ones((CHUNK, CHUNK), dtype=bool))
    ws_m = jnp.where(tril, w_s, 0.0)
    wst = ws_m.reshape(DEPTH, 3, 2 * CHUNK, CHUNK).astype(BF16)
    wstt = ws_m.transpose(0, 1, 3, 2).reshape(DEPTH, 3, 2 * CHUNK, CHUNK).astype(BF16)
    bst = jnp.repeat(b_s.transpose(0, 2, 1), HEAD_DIM, axis=2)
    conv_w = jnp.pad(conv_a_full, ((0, 0), (0, CONV_HALO - CONV_K), (0, 0)))
    pa = _rows8([conv_a_b, ln_a_g, ln_a_b, ln_v_g, ln_v_b])
    gin = jnp.concatenate([jnp.ones((1, D_MODEL), F32), ln2_g[:DEPTH - 1]], axis=0)
    bin_ = jnp.concatenate([jnp.zeros((1, D_MODEL), F32), ln2_b[:DEPTH - 1]], axis=0)
    pd = _rows8([gin, bin_, ln1_g, ln1_b, ln2_g, ln2_b])
    pf = jnp.concatenate([_pad_ff_cols(conv_f_full), _pad_ff_cols(conv_f_b)[:, None, :],
                          jnp.zeros((DEPTH, 8 - FFN_CONV_K - 1, FF_P), F32)], axis=1)

    acts = []
    xin = x0
    for l in range(DEPTH):
        w = wts[l]
        kt_all, k_all, v_all, vt_all = _kv_fwd(mem0, w["wmk"], w["wmv"])
        ops = [_gather_chips_op([sh_up[0], sh_dn[0]])] if l == 0 else None
        (xh1, rstd1, h, ac, cat, mixed, probs), side = _mixer_fwd(xin, pd, w["win_t"], conv_w, pa, wst, bst, kt_all, v_all, w["wout"],
                                                    l, t_fwd, ops)
        if l == 0:
            w.update(ffn_weights(*side[0]))
        ops = [_gather_chips_op([sh_in[1], sh_mk[1], sh_mv[1], sh_out[1], sh_up[1], sh_dn[1]])] if l == 0 else None
        (xh2, rstd2, upg, upv), side = _ffn_fwd(xh1, pd, w["wup_t"], pf, w["wdown"], l, t_fwd, ops)
        if l == 0:
            wts[1] = {**mixer_weights(*side[0][:4]), **ffn_weights(*side[0][4:])}
        acts.append(dict(xin=xin, k_all=k_all, vt_all=vt_all, xh1=xh1, rstd1=rstd1, h=h, ac=ac, cat=cat, mixed=mixed, probs=probs,
                         xh2=xh2, rstd2=rstd2, upg=upg, upv=upv))
        xin = xh2

    assert DEPTH == 2

    def halves_view(gs):
        return [g.reshape(1, N_CHIPS, 2, g.shape[1] // (2 * N_CHIPS), g.shape[2]) for g in gs]

    small = [None] * DEPTH
    small_packed = [None] * DEPTH
    small_gathered = [None] * DEPTH
    red_layers = [None] * DEPTH
    gz = target
    loss_sum = None
    g5_prev = None
    for l in reversed(range(DEPTH)):
        a, w = acts[l], wts[l]
        last = l == DEPTH - 1
        (dx1, dy, dug, duv, hm, vd2, vf), side = _ffn_bwd_d(
            gz, a["xh2"], a["rstd2"], a["upg"], a["upv"], pd, pf, w["wdown"], w["wup_t"], l, t_bwd, last,
            [_swap_op(g5_prev), _gather_devices_op(small_packed[l + 1])] if g5_prev else None)
        if last:
            loss_sum = vd2[VD_LOSS, 0]
        if g5_prev:
            small_gathered[l + 1] = side[1][0]
        parts_prev = _add_halves(g5_prev, side[0], place) if g5_prev else None
        (gw_up_t, gw_down), side = _ffn_bwd_w(a["xh1"], pd, dy, dug, duv, hm, l, t_wg,
                                              [_scatter_op(parts_prev)] if g5_prev else None)
        halves_prev = _sum_slots(parts_prev, side[0], place) if g5_prev else None
        g5_ffn = halves_view([gw_up_t.reshape(1, 2 * FF_P, D_MODEL), gw_down])
        ops = ([_sgather_op(halves_prev)] if g5_prev else []) + ([_swap_op(g5_ffn)] if l == 0 else [])
        (dx0, dh, dmix, vd1, va, dcw, dws, dbs, dkt, dv), side = _mixer_bwd_d(
            dx1, a["xh1"], a["rstd1"], a["h"], a["ac"], a["mixed"], a["probs"], pd, conv_w, pa, wstt,
            a["k_all"], a["vt_all"], w["wout"], w["win_t"], l, t_fwd, ops)
        if g5_prev:
            red_layers[l + 1] = side[0]
        parts_ffn = _add_halves(g5_ffn, side[-1], place) if l == 0 else None
        dws6 = jnp.where(tril, dws.reshape(6, CHUNK, CHUNK), 0.0)
        small[l] = [dcw[:CONV_K], va[VA_CONV_B], va[VA_LNA_G], va[VA_LNA_B], va[VA_LNV_G], va[VA_LNV_B], dws6,
                    dbs[:, :6].T, vd1[VD_LN_G], vd1[VD_LN_B],
                    _unpad_ff_cols(vf[VF_W0:VF_W0 + FFN_CONV_K]), _unpad_ff_cols(vf[VF_B]),
                    vd2[VD_LN_G], vd2[VD_LN_B]]
        small_packed[l] = _pack_small(small[l])
        ops = [_scatter_op(parts_ffn), _gather_devices_op(small_packed[l])] if l == 0 else None
        (gw_in_t, gw_out), side = _mixer_bwd_w(a["xin"], pd, dh, a["cat"], dmix, l, t_wg, ops)
        gw_mk, gw_mv = _kv_bwd(mem0, dkt, dv)
        g5_mix = halves_view([gw_in_t, gw_mk, gw_mv, gw_out])
        if l == 0:
            small_gathered[l] = side[1][0]
            halves_ffn = _sum_slots(parts_ffn, side[0], place)
            red_ffn, recv_mix = _run_side_ops([_sgather_op(halves_ffn), _swap_op(g5_mix)], "rs_tail_swap")
            parts_mix = _add_halves(g5_mix, recv_mix, place)
            halves_mix = _sum_slots(parts_mix, _run_side_ops([_scatter_op(parts_mix)], "rs_tail_chips")[0], place)
            red_mix = _run_side_ops([_sgather_op(halves_mix)], "rs_tail_gather")[0]
            red_layers[0] = red_mix + red_ffn
        else:
            g5_prev = g5_mix + g5_ffn
        gz = dx0
    grad_x = gz[None]

    def shard_grads(red):
        r = [f.reshape(-1, f.shape[-1]) for f in red]
        return dict(w_in=r[0].T, w_mk=r[1], w_mv=r[2], w_out=r[3], w_up=_unpad_ff_cols(r[4].T), w_down=r[5])

    big_grads = [shard_grads(red_layers[l]) for l in range(DEPTH)]

    small_red = _sum_devices(small_gathered)
    sg = _unpack_small(small_red)
    g_conv_a_w = lax.dynamic_slice_in_dim(sg["conv_a_w"], chip * (CONV_W // 4), CONV_W // 4, axis=2)
    g_conv_f_w = lax.dynamic_slice_in_dim(sg["conv_f_w"], chip * FF_Q, FF_Q, axis=2)

    loss = 0.5 / D_MODEL * lax.psum(loss_sum, ("x", "y", "c"))

    grads = dict(conv_a_w=g_conv_a_w, conv_a_b=sg["conv_a_b"], ln_a_g=sg["ln_a_g"], ln_a_b=sg["ln_a_b"],
                 ln_v_g=sg["ln_v_g"], ln_v_b=sg["ln_v_b"], w_s=sg["w_s"], b_s=sg["b_s"], ln1_g=sg["ln1_g"], ln1_b=sg["ln1_b"],
                 conv_f_w=g_conv_f_w, conv_f_b=sg["conv_f_b"], ln2_g=sg["ln2_g"], ln2_b=sg["ln2_b"])
    weights = dict(w_in=w_in, conv_a_w=conv_a_w, conv_a_b=conv_a_b, ln_a_g=ln_a_g, ln_a_b=ln_a_b, ln_v_g=ln_v_g,
                   ln_v_b=ln_v_b, w_s=w_s, b_s=b_s, w_mk=w_mk, w_mv=w_mv, w_out=w_out, ln1_g=ln1_g, ln1_b=ln1_b,
                   w_up=w_up, conv_f_w=conv_f_w, conv_f_b=conv_f_b, w_down=w_down, ln2_g=ln2_g, ln2_b=ln2_b)
    mom_m = dict(w_in=m_w_in, conv_a_w=m_conv_a_w, conv_a_b=m_conv_a_b, ln_a_g=m_ln_a_g, ln_a_b=m_ln_a_b, ln_v_g=m_ln_v_g,
                 ln_v_b=m_ln_v_b, w_s=m_w_s, b_s=m_b_s, w_mk=m_w_mk, w_mv=m_w_mv, w_out=m_w_out, ln1_g=m_ln1_g,
                 ln1_b=m_ln1_b, w_up=m_w_up, conv_f_w=m_conv_f_w, conv_f_b=m_conv_f_b, w_down=m_w_down, ln2_g=m_ln2_g,
                 ln2_b=m_ln2_b)
    mom_v = dict(w_in=v_w_in, conv_a_w=v_conv_a_w, conv_a_b=v_conv_a_b, ln_a_g=v_ln_a_g, ln_a_b=v_ln_a_b, ln_v_g=v_ln_v_g,
                 ln_v_b=v_ln_v_b, w_s=v_w_s, b_s=v_b_s, w_mk=v_w_mk, w_mv=v_w_mv, w_out=v_w_out, ln1_g=v_ln1_g,
                 ln1_b=v_ln1_b, w_up=v_w_up, conv_f_w=v_conv_f_w, conv_f_b=v_conv_f_b, w_down=v_w_down, ln2_g=v_ln2_g,
                 ln2_b=v_ln2_b)
    names = list(weights)
    big_names = ["w_in", "w_mk", "w_mv", "w_out", "w_up", "w_down"]
    delta, new_m, new_v = {}, {}, {}
    for n in big_names:
        grads[n], delta[n], new_m[n], new_v[n] = _adamw_layers(weights[n], [big_grads[l][n] for l in range(DEPTH)],
                                                               mom_m[n], mom_v[n], "adamw_" + n)
    small_names = [n for n in names if n not in big_names]
    ds, nms, nvs = _adamw_small([weights[n] for n in small_names], [grads[n] for n in small_names],
                                [mom_m[n] for n in small_names], [mom_v[n] for n in small_names])
    for n, d, nm, nv in zip(small_names, ds, nms, nvs):
        delta[n], new_m[n], new_v[n] = d, nm, nv

    return (loss, grad_x, *[grads[n] for n in names], *[delta[n] for n in names],
            *[new_m[n] for n in names], *[new_v[n] for n in names])
```

```python
import jax
import jax.numpy as jnp
from jax import lax
from jax.experimental import pallas as pl
from jax.experimental.pallas import tpu as pltpu

F32 = jnp.float32
BF16 = jnp.bfloat16

D_MODEL = 1024
DEPTH = 2
CONV_W = 384
GMLP_W = 384
XATTN_W = 256
XATTN_HEADS = 4
HEAD_DIM = 64
IN_W = 1792
CONV_K = 31
CHUNK = 128
N_MEM = 256
D_FF = 2752
FFN_CONV_K = 3
ALPHA = (2.0 * DEPTH) ** 0.25
LN_EPS = 1e-5
ATT_SCALE = 1.0 / 8.0
ADAM_LR, ADAM_B1, ADAM_B2, ADAM_EPS, ADAM_WD, ADAM_STEP = 0.001, 0.9, 0.999, 1e-08, 0.01, 10

N_CHIPS = 4
FF_Q = D_FF // N_CHIPS
FF_QP = 704
FF_H = 2 * FF_QP
FF_P = 4 * FF_QP
LANE = 128
CONV_HALO = 32
FFN_HALO = 8
BF16_ROWS = 16
VMEM_LIMIT = 60 * 1024 * 1024

MESH = pl.DeviceIdType.MESH
ANY = pl.BlockSpec(memory_space=pl.ANY)


def _cparams(sem=None, vmem=VMEM_LIMIT):
    kw = {"vmem_limit_bytes": vmem}
    if sem is not None:
        kw["dimension_semantics"] = sem
    return pltpu.CompilerParams(**kw)


def _row_tile(rows, row_bytes, limit=2 << 20, mult=BF16_ROWS):
    if rows * row_bytes <= limit:
        return rows
    best = None
    for cand in range(mult, rows, mult):
        if rows % cand == 0 and cand * row_bytes <= limit:
            best = cand
    assert best is not None, (rows, row_bytes)
    return best


def _const_spec(shape):
    nd = len(shape)
    return pl.BlockSpec(shape, lambda *_: (0,) * nd)


def _layer_spec(shape, *lead, resident=False):
    nd = len(shape)
    kw = {"pipeline_mode": pl.Buffered(1)} if resident else {}
    return pl.BlockSpec((None,) * len(lead) + tuple(shape), lambda *_: tuple(lead) + (0,) * nd, **kw)


def _sigmoid(x):
    return jax.nn.sigmoid(x)


def _gelu(x):
    return jax.nn.gelu(x)


def _gelu_and_grad(x):
    c = 0.7978845608028654
    a = 0.044715
    x2 = x * x
    t = jnp.tanh(c * (x + a * x * x2))
    h = 0.5 * (1.0 + t)
    return x * h, h + 0.5 * x * (1.0 - t * t) * c * (1.0 + 3.0 * a * x2)


def _ln_fwd(z):
    mu = jnp.mean(z, axis=-1, keepdims=True)
    zc = z - mu
    var = jnp.mean(zc * zc, axis=-1, keepdims=True)
    rstd = lax.rsqrt(var + LN_EPS)
    return zc * rstd, rstd


def _ln_bwd(dxh, xh, rstd):
    m1 = jnp.mean(dxh, axis=-1, keepdims=True)
    m2 = jnp.mean(dxh * xh, axis=-1, keepdims=True)
    return rstd * (dxh - m1 - xh * m2)


def _colsum(a):
    return jnp.sum(a, axis=0, keepdims=True)


def _dot(a, b):
    return jnp.dot(a, b, preferred_element_type=F32)


def _dot_tn(a, b):
    return lax.dot_general(a, b, (((0,), (0,)), ((), ())), preferred_element_type=F32)


def _dot_nt(a, b):
    return lax.dot_general(a, b, (((1,), (1,)), ((), ())), preferred_element_type=F32)


def _softmax_heads(sc):
    ps = []
    for hd in range(XATTN_HEADS):
        s = sc[:, hd * N_MEM:(hd + 1) * N_MEM]
        e = jnp.exp(s - jnp.max(s, axis=-1, keepdims=True))
        ps.append(e / jnp.sum(e, axis=-1, keepdims=True))
    return jnp.concatenate(ps, axis=1)


def _lane_lo(shape):
    return (lax.broadcasted_iota(jnp.int32, shape, len(shape) - 1) % LANE) < HEAD_DIM


def _spatial_mix(vnb, wst_ref, bst_ref, mix_ref, t):
    lo = _lane_lo((CHUNK, LANE))
    for n in range(t // CHUNK):
        rows = slice(n * CHUNK, (n + 1) * CHUNK)
        for j in range(GMLP_W // LANE):
            cols = slice(j * LANE, (j + 1) * LANE)
            r = _dot(wst_ref[j], vnb[rows, cols])
            mix_ref[rows, cols] = jnp.where(lo, r[:CHUNK], r[CHUNK:]) + bst_ref[:, cols]


def _kv_fwd(mem, w_mk, w_mv):
    def body(mem_ref, wk_ref, wv_ref, kt_ref, k_ref, v_ref, vt_ref):
        mb = mem_ref[...].astype(BF16)
        k = _dot(mb, wk_ref[...])
        v = _dot(mb, wv_ref[...])
        col = lax.broadcasted_iota(jnp.int32, (N_MEM, XATTN_W), 1) // HEAD_DIM
        ks = [jnp.where(col == hd, k, 0.0) for hd in range(XATTN_HEADS)]
        vs = [jnp.where(col == hd, v, 0.0) for hd in range(XATTN_HEADS)]
        k_ref[...] = jnp.concatenate(ks, axis=0).astype(BF16)
        v_ref[...] = jnp.concatenate(vs, axis=0).astype(BF16)
        kt_ref[...] = jnp.concatenate([x.T for x in ks], axis=1).astype(BF16)
        vt_ref[...] = jnp.concatenate([x.T for x in vs], axis=1).astype(BF16)

    wide = jax.ShapeDtypeStruct((XATTN_W, XATTN_HEADS * N_MEM), BF16)
    tall = jax.ShapeDtypeStruct((XATTN_HEADS * N_MEM, XATTN_W), BF16)
    wspec = _layer_spec((D_MODEL, XATTN_W), 0)
    return pl.pallas_call(body, name="kv_fwd", grid=(1,),
                          in_specs=[_const_spec((N_MEM, D_MODEL)), wspec, wspec],
                          out_specs=[_const_spec(wide.shape), _const_spec(tall.shape), _const_spec(tall.shape),
                                     _const_spec(wide.shape)],
                          out_shape=(wide, tall, tall, wide), compiler_params=_cparams(("arbitrary",)))(mem, w_mk, w_mv)


def _kv_bwd(mem, dkt_all, dv_all):
    def body(mem_ref, dkt_ref, dv_ref, gk_ref, gv_ref):
        col = lax.broadcasted_iota(jnp.int32, (N_MEM, XATTN_W), 1) // HEAD_DIM
        dk = jnp.zeros((N_MEM, XATTN_W), F32)
        dv = jnp.zeros((N_MEM, XATTN_W), F32)
        for hd in range(XATTN_HEADS):
            dk = dk + jnp.where(col == hd, dkt_ref[:, hd * N_MEM:(hd + 1) * N_MEM].T, 0.0)
            dv = dv + jnp.where(col == hd, dv_ref[hd * N_MEM:(hd + 1) * N_MEM, :], 0.0)
        mb = mem_ref[...].astype(BF16)
        gk_ref[0] = _dot_tn(mb, dk.astype(BF16))
        gv_ref[0] = _dot_tn(mb, dv.astype(BF16))

    out = jax.ShapeDtypeStruct((1, D_MODEL, XATTN_W), F32)
    return pl.pallas_call(body, name="kv_bwd", out_shape=(out, out), compiler_params=_cparams())(mem, dkt_all, dv_all)


def _my_place():
    x, y, c = lax.axis_index("x"), lax.axis_index("y"), lax.axis_index("c")
    chips = [(1 - x, y), (x, 1 - y), (1 - x, 1 - y)]
    return x, y, c, chips


class _SideOp:
    def __init__(self, ins, out_shapes, scratch, make, aliases=None):
        self.ins, self.out_shapes, self.scratch, self.make, self.aliases = list(ins), list(out_shapes), list(scratch), make, dict(aliases or {})


def _call_with_side(body, side_ops, *, name, grid, in_specs, out_specs, out_shape, scratch_shapes, operands, semantics):
    side_ops = list(side_ops or ())
    n_in, n_out, n_scr = len(in_specs), len(out_specs), len(scratch_shapes)
    s_ins = [a for op in side_ops for a in op.ins]
    s_outs = [o for op in side_ops for o in op.out_shapes]
    s_scr = [x for op in side_ops for x in op.scratch]
    aliases, oi, oo = {}, 0, 0
    for op in side_ops:
        for a, b in op.aliases.items():
            aliases[n_in + oi + a] = n_out + oo + b
        oi, oo = oi + len(op.ins), oo + len(op.out_shapes)

    def wrapped(*refs):
        ins, sins = refs[:n_in], refs[n_in:n_in + len(s_ins)]
        base = n_in + len(s_ins)
        outs, souts = refs[base:base + n_out], refs[base + n_out:base + n_out + len(s_outs)]
        base += n_out + len(s_outs)
        scr, sscr = refs[base:base + n_scr], refs[base + n_scr:]
        if side_ops:
            first = pl.program_id(0) == 0
            last = pl.program_id(0) == grid[0] - 1
            for d in range(1, len(grid)):
                first = jnp.logical_and(first, pl.program_id(d) == 0)
                last = jnp.logical_and(last, pl.program_id(d) == grid[d] - 1)
            hooks, a, b, c = [], 0, 0, 0
            for op in side_ops:
                hooks.append(op.make(sins[a:a + len(op.ins)], souts[b:b + len(op.out_shapes)], sscr[c:c + len(op.scratch)]))
                a, b, c = a + len(op.ins), b + len(op.out_shapes), c + len(op.scratch)

            @pl.when(first)
            def _():
                for start, _w in hooks:
                    start()

        body(*ins, *outs, *scr)
        if side_ops:
            @pl.when(last)
            def _():
                for _s, wait in hooks:
                    wait()

    res = pl.pallas_call(
        wrapped, name=name, grid=grid, in_specs=list(in_specs) + [ANY] * len(s_ins),
        out_specs=list(out_specs) + [ANY] * len(s_outs), out_shape=list(out_shape) + s_outs,
        scratch_shapes=list(scratch_shapes) + s_scr, input_output_aliases=aliases,
        compiler_params=_cparams(semantics),
    )(*operands, *s_ins)
    side_res, k = [], n_out
    for op in side_ops:
        side_res.append(list(res[k:k + len(op.out_shapes)]))
        k += len(op.out_shapes)
    return list(res[:n_out]), side_res


def _gather_chips_op(shards):
    n = len(shards)

    def make(ins, outs, scr):
        send_sems, recv_sems, local_sems = scr
        x, y, c, chips = _my_place()
        mej = 2 * x + y

        def remote(tn, k, slot):
            px, py = chips[k]
            return pltpu.make_async_remote_copy(
                src_ref=ins[tn], dst_ref=outs[tn].at[slot], send_sem=send_sems.at[tn, k],
                recv_sem=recv_sems.at[tn, k], device_id=(px, py, c), device_id_type=MESH)

        def local(tn):
            return pltpu.make_async_copy(ins[tn], outs[tn].at[mej], local_sems.at[tn])

        def start():
            for k in range(3):
                for tn in range(n):
                    remote(tn, k, mej).start()
            for tn in range(n):
                local(tn).start()

        def wait():
            for k, (px, py) in enumerate(chips):
                for tn in range(n):
                    remote(tn, k, 2 * px + py).wait_recv()
                    remote(tn, k, mej).wait_send()
            for tn in range(n):
                local(tn).wait()

        return start, wait

    return _SideOp(shards, [jax.ShapeDtypeStruct((N_CHIPS,) + a.shape, a.dtype) for a in shards],
                   [pltpu.SemaphoreType.DMA((n, 3)), pltpu.SemaphoreType.DMA((n, 3)), pltpu.SemaphoreType.DMA((n,))], make)


PA_CONV_B, PA_LNA_G, PA_LNA_B, PA_LNV_G, PA_LNV_B = 0, 1, 2, 3, 4
PD_GIN, PD_BIN, PD_G1, PD_B1, PD_G2, PD_B2 = 0, 1, 2, 3, 4, 5


def _row(ref, r):
    return ref[r:r + 1, :]


def _mixer_fwd(xin, pd, win_t, conv_w, pa, wst, bst, kt_all, v_all, w_out, l, t, side_ops=None):
    s = xin.shape[0]
    nt = s // t

    def body(x_ref, pd_ref, wint_ref, cw_ref, pa_ref, wst_ref, bst_ref, kt_ref, v_ref, wout_ref,
             xh_ref, rstd_ref, h_ref, ac_ref, cat_ref, mixed_ref, p_ref, cbuf, zbuf):
        i = pl.program_id(0)
        x = x_ref[...] * _row(pd_ref, PD_GIN) + _row(pd_ref, PD_BIN)
        h = _dot_nt(x.astype(BF16), wint_ref[...])
        h_ref[...] = h
        a1, a2 = h[:, 0:CONV_W], h[:, CONV_W:2 * CONV_W]
        hu, hv = h[:, 2 * CONV_W:2 * CONV_W + GMLP_W], h[:, 2 * CONV_W + GMLP_W:2 * CONV_W + 2 * GMLP_W]
        q = h[:, IN_W - XATTN_W:]

        @pl.when(i == 0)
        def _():
            cbuf[0:CONV_HALO, :] = jnp.zeros((CONV_HALO, CONV_W), F32)

        cbuf[CONV_HALO:CONV_HALO + t, :] = a1 * _sigmoid(a2)
        ac = jnp.zeros((t, CONV_W), F32) + _row(pa_ref, PA_CONV_B)
        for r in range(8):
            zr = jnp.zeros((t + 8, CONV_W), F32)
            for a in range(4):
                o = 8 * a + r
                if o < CONV_K:
                    k = CONV_K - 1 - o
                    zr = zr + cbuf[CONV_HALO - 8 - 8 * a:CONV_HALO - 8 - 8 * a + t + 8, :] * cw_ref[k:k + 1, :]
            if r == 0:
                ac = ac + zr[8:, :]
            else:
                zbuf[...] = zr
                ac = ac + zbuf[8 - r:8 - r + t, :]
        ac_ref[...] = ac
        cbuf[0:CONV_HALO, :] = cbuf[t:t + CONV_HALO, :]
        xh_a, _ = _ln_fwd(ac)
        an = xh_a * _row(pa_ref, PA_LNA_G) + _row(pa_ref, PA_LNA_B)
        a = an * _sigmoid(an)

        u = _gelu(hu)
        xh_v, _ = _ln_fwd(_gelu(hv))
        vn = xh_v * _row(pa_ref, PA_LNV_G) + _row(pa_ref, PA_LNV_B)
        _spatial_mix(vn.astype(BF16), wst_ref, bst_ref, mixed_ref, t)
        g = u * mixed_ref[...]

        p = _softmax_heads(_dot(q.astype(BF16), kt_ref[...]) * ATT_SCALE)
        p_ref[...] = p
        o = _dot(p.astype(BF16), v_ref[...])

        cat = jnp.concatenate([a, g, o], axis=1).astype(BF16)
        cat_ref[...] = cat
        z = ALPHA * x + _dot(cat, wout_ref[...])
        xh, rstd = _ln_fwd(z)
        xh_ref[...] = xh
        rstd_ref[...] = rstd

    tok = lambda w: pl.BlockSpec((t, w), lambda i: (i, 0))
    return _call_with_side(
        body, side_ops, name="mixer_fwd", grid=(nt,),
        in_specs=[tok(D_MODEL), _layer_spec((8, D_MODEL), l), _layer_spec((IN_W, D_MODEL), 0, resident=True),
                  _layer_spec((CONV_HALO, CONV_W), l), _layer_spec((8, CONV_W), l),
                  _layer_spec((3, 2 * CHUNK, CHUNK), l), _layer_spec((CHUNK, GMLP_W), l),
                  _const_spec((XATTN_W, XATTN_HEADS * N_MEM)), _const_spec((XATTN_HEADS * N_MEM, XATTN_W)),
                  _layer_spec((D_MODEL, D_MODEL), 0, resident=True)],
        out_specs=[tok(D_MODEL), tok(1), tok(IN_W), tok(CONV_W), tok(D_MODEL), tok(GMLP_W), tok(XATTN_HEADS * N_MEM)],
        out_shape=[jax.ShapeDtypeStruct((s, D_MODEL), F32), jax.ShapeDtypeStruct((s, 1), F32),
                   jax.ShapeDtypeStruct((s, IN_W), F32), jax.ShapeDtypeStruct((s, CONV_W), F32),
                   jax.ShapeDtypeStruct((s, D_MODEL), BF16), jax.ShapeDtypeStruct((s, GMLP_W), F32),
                   jax.ShapeDtypeStruct((s, XATTN_HEADS * N_MEM), F32)],
        scratch_shapes=[pltpu.VMEM((t + CONV_HALO, CONV_W), F32), pltpu.VMEM((t + 8, CONV_W), F32)],
        operands=(xin, pd, win_t, conv_w, pa, wst, bst, kt_all, v_all, w_out), semantics=("arbitrary",))


VD_LN_G, VD_LN_B, VD_LOSS = 0, 1, 2
VA_CONV_B, VA_LNA_G, VA_LNA_B, VA_LNV_G, VA_LNV_B = 0, 1, 2, 3, 4


def _mixer_bwd_d(gz, xh1, rstd1, h, ac, mixed, probs, pd, conv_w, pa, wstt, k_all, vt_all, w_out, win_t, l, t, side_ops=None):
    s = gz.shape[0]
    nt = s // t

    def body(gz_ref, xh_ref, rstd_ref, h_ref, ac_ref, mixed_ref, p_ref, pd_ref, cw_ref, pa_ref, wstt_ref,
             k_ref, vt_ref, wout_ref, wint_ref,
             dx_ref, dh_ref, dmix_ref, vd_ref, va_ref, dcw_ref, dws_ref, dbs_ref, dkt_ref, dv_ref,
             ebuf, dvnbuf, dbsacc, erbuf):
        i = pl.program_id(0)

        @pl.when(i == 0)
        def _():
            vd_ref[...] = jnp.zeros_like(vd_ref)
            va_ref[...] = jnp.zeros_like(va_ref)
            dcw_ref[...] = jnp.zeros_like(dcw_ref)
            dws_ref[...] = jnp.zeros_like(dws_ref)
            dbs_ref[...] = jnp.zeros_like(dbs_ref)
            dkt_ref[...] = jnp.zeros_like(dkt_ref)
            dv_ref[...] = jnp.zeros_like(dv_ref)
            dbsacc[...] = jnp.zeros_like(dbsacc)
            ebuf[t:t + CONV_HALO, :] = jnp.zeros((CONV_HALO, CONV_W), F32)

        gz_v = gz_ref[...]
        xh = xh_ref[...]
        vd_ref[VD_LN_G:VD_LN_G + 1, :] += _colsum(gz_v * xh)
        vd_ref[VD_LN_B:VD_LN_B + 1, :] += _colsum(gz_v)
        dz = _ln_bwd(gz_v * _row(pd_ref, PD_G1), xh, rstd_ref[...])
        dzb = dz.astype(BF16)
        dmix_ref[...] = dzb
        dcat = _dot_nt(dzb, wout_ref[...])
        d_a, d_g, d_o = dcat[:, 0:CONV_W], dcat[:, CONV_W:CONV_W + GMLP_W], dcat[:, CONV_W + GMLP_W:]

        h = h_ref[...]
        a1, a2 = h[:, 0:CONV_W], h[:, CONV_W:2 * CONV_W]
        hu, hv = h[:, 2 * CONV_W:2 * CONV_W + GMLP_W], h[:, 2 * CONV_W + GMLP_W:2 * CONV_W + 2 * GMLP_W]
        q = h[:, IN_W - XATTN_W:]

        xh_a, rstd_a = _ln_fwd(ac_ref[...])
        an = xh_a * _row(pa_ref, PA_LNA_G) + _row(pa_ref, PA_LNA_B)
        sig = _sigmoid(an)
        d_an = d_a * (sig * (1.0 + an * (1.0 - sig)))
        va_ref[VA_LNA_G:VA_LNA_G + 1, :] += _colsum(d_an * xh_a)
        va_ref[VA_LNA_B:VA_LNA_B + 1, :] += _colsum(d_an)
        dac = _ln_bwd(d_an * _row(pa_ref, PA_LNA_G), xh_a, rstd_a)
        va_ref[VA_CONV_B:VA_CONV_B + 1, :] += _colsum(dac)
        ebuf[0:t, :] = dac
        sg = _sigmoid(a2)
        glu = a1 * sg
        dglu = jnp.zeros((t, CONV_W), F32)
        for r in range(8):
            if r > 0:
                erbuf[...] = ebuf[r:r + t + 24, :]
            src = ebuf if r == 0 else erbuf
            for a in range(4):
                o = 8 * a + r
                if o < CONV_K:
                    k = CONV_K - 1 - o
                    ek = src[8 * a:8 * a + t, :]
                    dglu = dglu + ek * cw_ref[k:k + 1, :]
                    dcw_ref[k:k + 1, :] += _colsum(ek * glu)
        ebuf[t:t + CONV_HALO, :] = ebuf[0:CONV_HALO, :]
        da1 = dglu * sg
        da2 = dglu * a1 * sg * (1.0 - sg)

        u, du = _gelu_and_grad(hu)
        vg, dvg_dhv = _gelu_and_grad(hv)
        xh_v, rstd_v = _ln_fwd(vg)
        vn = xh_v * _row(pa_ref, PA_LNV_G) + _row(pa_ref, PA_LNV_B)
        vnb = vn.astype(BF16)
        dhu = d_g * mixed_ref[...] * du
        dm = d_g * u
        dmb = dm.astype(BF16)
        lo = _lane_lo((CHUNK, LANE))
        for n in range(t // CHUNK):
            rows = slice(n * CHUNK, (n + 1) * CHUNK)
            dbsacc[...] += dm[rows, :]
            for j in range(GMLP_W // LANE):
                cols = slice(j * LANE, (j + 1) * LANE)
                dm_blk = dmb[rows, cols]
                r = _dot(wstt_ref[j], dm_blk)
                dvnbuf[rows, cols] = jnp.where(lo, r[:CHUNK], r[CHUNK:])
                zero = jnp.zeros_like(dm_blk)
                st = jnp.concatenate([jnp.where(lo, dm_blk, zero), jnp.where(lo, zero, dm_blk)], axis=0)
                dws_ref[j] += _dot_nt(st, vnb[rows, cols])
        dvn = dvnbuf[...]
        va_ref[VA_LNV_G:VA_LNV_G + 1, :] += _colsum(dvn * xh_v)
        va_ref[VA_LNV_B:VA_LNV_B + 1, :] += _colsum(dvn)
        dhv = _ln_bwd(dvn * _row(pa_ref, PA_LNV_G), xh_v, rstd_v) * dvg_dhv

        qb = q.astype(BF16)
        p = p_ref[...]
        dob = d_o.astype(BF16)
        dp = _dot(dob, vt_ref[...])
        dss = []
        for hd in range(XATTN_HEADS):
            cs = slice(hd * N_MEM, (hd + 1) * N_MEM)
            ph, dph = p[:, cs], dp[:, cs]
            dss.append(ph * (dph - jnp.sum(ph * dph, axis=-1, keepdims=True)) * ATT_SCALE)
        dsb = jnp.concatenate(dss, axis=1).astype(BF16)
        dq = _dot(dsb, k_ref[...])
        dkt_ref[...] += _dot_tn(qb, dsb)
        dv_ref[...] += _dot_tn(p.astype(BF16), dob)

        dhb = jnp.concatenate([da1, da2, dhu, dhv, dq], axis=1).astype(BF16)
        dh_ref[...] = dhb
        dx_ref[...] = ALPHA * dz + _dot(dhb, wint_ref[...])

        @pl.when(i == nt - 1)
        def _():
            acc = dbsacc[...]
            head = lax.broadcasted_iota(jnp.int32, (CHUNK, GMLP_W), 1) // HEAD_DIM
            lane = lax.broadcasted_iota(jnp.int32, (CHUNK, LANE), 1)
            out = jnp.zeros((CHUNK, LANE), F32)
            for hd in range(GMLP_W // HEAD_DIM):
                sh = jnp.sum(jnp.where(head == hd, acc, 0.0), axis=1, keepdims=True)
                out = out + jnp.where(lane == hd, sh, 0.0)
            dbs_ref[...] = out

    rev = lambda w: pl.BlockSpec((t, w), lambda i: (nt - 1 - i, 0))
    out_shape = [
        jax.ShapeDtypeStruct((s, D_MODEL), F32), jax.ShapeDtypeStruct((s, IN_W), BF16),
        jax.ShapeDtypeStruct((s, D_MODEL), BF16),
        jax.ShapeDtypeStruct((8, D_MODEL), F32), jax.ShapeDtypeStruct((8, CONV_W), F32),
        jax.ShapeDtypeStruct((CONV_HALO, CONV_W), F32), jax.ShapeDtypeStruct((3, 2 * CHUNK, CHUNK), F32),
        jax.ShapeDtypeStruct((CHUNK, LANE), F32),
        jax.ShapeDtypeStruct((XATTN_W, XATTN_HEADS * N_MEM), F32), jax.ShapeDtypeStruct((XATTN_HEADS * N_MEM, XATTN_W), F32),
    ]
    out_specs = [rev(D_MODEL), rev(IN_W), rev(D_MODEL)] + [_const_spec(o.shape) for o in out_shape[3:]]
    return _call_with_side(
        body, side_ops, name="mixer_bwd_d", grid=(nt,),
        in_specs=[rev(D_MODEL), rev(D_MODEL), rev(1), rev(IN_W), rev(CONV_W), rev(GMLP_W), rev(XATTN_HEADS * N_MEM),
                  _layer_spec((8, D_MODEL), l), _layer_spec((CONV_HALO, CONV_W), l), _layer_spec((8, CONV_W), l),
                  _layer_spec((3, 2 * CHUNK, CHUNK), l),
                  _const_spec((XATTN_HEADS * N_MEM, XATTN_W)), _const_spec((XATTN_W, XATTN_HEADS * N_MEM)),
                  _layer_spec((D_MODEL, D_MODEL), 0, resident=True), _layer_spec((IN_W, D_MODEL), 0, resident=True)],
        out_specs=out_specs, out_shape=out_shape,
        scratch_shapes=[pltpu.VMEM((t + CONV_HALO, CONV_W), F32), pltpu.VMEM((t, GMLP_W), F32),
                        pltpu.VMEM((CHUNK, GMLP_W), F32), pltpu.VMEM((t + 24, CONV_W), F32)],
        operands=(gz, xh1, rstd1, h, ac, mixed, probs, pd, conv_w, pa, wstt, k_all, vt_all, w_out, win_t),
        semantics=("arbitrary",))


def _mixer_bwd_w(xin, pd, dh, cat, dmix, l, t, side_ops=None):
    s = xin.shape[0]
    nt = s // t

    def body(x_ref, pd_ref, dh_ref, cat_ref, dmix_ref, dwin_ref, dwout_ref):
        @pl.when(pl.program_id(0) == 0)
        def _():
            dwin_ref[...] = jnp.zeros_like(dwin_ref)
            dwout_ref[...] = jnp.zeros_like(dwout_ref)

        xb = (x_ref[...] * _row(pd_ref, PD_GIN) + _row(pd_ref, PD_BIN)).astype(BF16)
        dwin_ref[...] += _dot_tn(dh_ref[...], xb)
        dwout_ref[...] += _dot_tn(cat_ref[...], dmix_ref[...])

    tok = lambda w: pl.BlockSpec((t, w), lambda i: (i, 0))
    return _call_with_side(
        body, side_ops, name="mixer_bwd_w", grid=(nt,),
        in_specs=[tok(D_MODEL), _layer_spec((8, D_MODEL), l), tok(IN_W), tok(D_MODEL), tok(D_MODEL)],
        out_specs=[_layer_spec((IN_W, D_MODEL), 0, resident=True), _layer_spec((D_MODEL, D_MODEL), 0, resident=True)],
        out_shape=[jax.ShapeDtypeStruct((1, IN_W, D_MODEL), F32), jax.ShapeDtypeStruct((1, D_MODEL, D_MODEL), F32)],
        scratch_shapes=[], operands=(xin, pd, dh, cat, dmix), semantics=("arbitrary",))


PF_W0, PF_B = 0, 3


def _ffn_fwd(xh1, pd, wup_t, pf, w_d, l, t, side_ops=None):
    s = xh1.shape[0]
    nt = s // t

    def body(xh_ref, pd_ref, wg_ref, wv_ref, pf_ref, wd_ref, xh2_ref, rstd_ref, upg_ref, upv_ref, fbuf):
        i = pl.program_id(0)

        @pl.when(i == 0)
        def _():
            fbuf[0:FFN_HALO, :] = jnp.zeros((FFN_HALO, FF_P), F32)

        x1 = xh_ref[...] * _row(pd_ref, PD_G1) + _row(pd_ref, PD_B1)
        xb = x1.astype(BF16)
        y = jnp.zeros((t, D_MODEL), F32)
        for hf in range(2):
            cs = slice(hf * FF_H, (hf + 1) * FF_H)
            ug = _dot_nt(xb, wg_ref[cs, :])
            uv = _dot_nt(xb, wv_ref[cs, :])
            upg_ref[:, cs] = ug
            upv_ref[:, cs] = uv
            fbuf[FFN_HALO:FFN_HALO + t, cs] = ug
            gate = jnp.zeros((t, FF_H), F32) + pf_ref[PF_B:PF_B + 1, cs]
            for k in range(FFN_CONV_K):
                off = FFN_HALO - (FFN_CONV_K - 1) + k
                gate = gate + fbuf[off:off + t, cs] * pf_ref[PF_W0 + k:PF_W0 + k + 1, cs]
            fbuf[0:FFN_HALO, cs] = fbuf[t:t + FFN_HALO, cs]
            hm = gate * _sigmoid(gate) * uv
            y = y + _dot(hm.astype(BF16), wd_ref[cs, :])
        xh2, rstd = _ln_fwd(ALPHA * x1 + y)
        xh2_ref[...] = xh2
        rstd_ref[...] = rstd

    tok = lambda w: pl.BlockSpec((t, w), lambda i: (i, 0))
    return _call_with_side(
        body, side_ops, name="ffn_fwd", grid=(nt,),
        in_specs=[tok(D_MODEL), _layer_spec((8, D_MODEL), l),
                  _layer_spec((FF_P, D_MODEL), 0, 0, resident=True), _layer_spec((FF_P, D_MODEL), 0, 1, resident=True),
                  _layer_spec((8, FF_P), l), _layer_spec((FF_P, D_MODEL), 0, resident=True)],
        out_specs=[tok(D_MODEL), tok(1), tok(FF_P), tok(FF_P)],
        out_shape=[jax.ShapeDtypeStruct((s, D_MODEL), F32), jax.ShapeDtypeStruct((s, 1), F32),
                   jax.ShapeDtypeStruct((s, FF_P), F32), jax.ShapeDtypeStruct((s, FF_P), F32)],
        scratch_shapes=[pltpu.VMEM((t + FFN_HALO, FF_P), F32)],
        operands=(xh1, pd, wup_t, wup_t, pf, w_d), semantics=("arbitrary",))


VF_W0, VF_B = 0, 3


def _ffn_bwd_d(gz_or_target, xh2, rstd2, upg, upv, pd, pf, w_d, wup_t, l, t, last, side_ops=None):
    s = xh2.shape[0]
    nt = s // t
    hb = t // FFN_HALO

    def body(gz_ref, xh2_ref, rstd_ref, upg_ref, halo_ref, upv_ref, pd_ref, pf_ref, wd_ref, wg_ref, wv_ref,
             dx_ref, dy_ref, dug_ref, duv_ref, hm_ref, vd_ref, vf_ref, gbuf, ebuf, s1buf, s2buf):
        i = pl.program_id(0)
        first_tile = i == nt - 1

        @pl.when(i == 0)
        def _():
            vd_ref[...] = jnp.zeros_like(vd_ref)
            vf_ref[...] = jnp.zeros_like(vf_ref)
            ebuf[t:t + FFN_HALO, :] = jnp.zeros((FFN_HALO, FF_P), F32)

        xh2_v = xh2_ref[...]
        if last:
            diff = xh2_v * _row(pd_ref, PD_G2) + _row(pd_ref, PD_B2) - gz_ref[...]
            vd_ref[VD_LOSS:VD_LOSS + 1, :] += _colsum(diff * diff)
            gz_v = diff * (1.0 / D_MODEL)
        else:
            gz_v = gz_ref[...]
        vd_ref[VD_LN_G:VD_LN_G + 1, :] += _colsum(gz_v * xh2_v)
        vd_ref[VD_LN_B:VD_LN_B + 1, :] += _colsum(gz_v)
        dz = _ln_bwd(gz_v * _row(pd_ref, PD_G2), xh2_v, rstd_ref[...])
        dyb = dz.astype(BF16)
        dy_ref[...] = dyb
        dx = ALPHA * dz
        for hf in range(2):
            cs = slice(hf * FF_H, (hf + 1) * FF_H)
            ug = upg_ref[:, cs]
            uv = upv_ref[:, cs]
            halo = halo_ref[:, cs]
            gbuf[0:FFN_HALO, :] = jnp.where(first_tile, jnp.zeros_like(halo), halo)
            gbuf[FFN_HALO:FFN_HALO + t, :] = ug
            s1buf[...] = gbuf[FFN_HALO - 1:FFN_HALO - 1 + t, :]
            s2buf[...] = gbuf[FFN_HALO - 2:FFN_HALO - 2 + t, :]
            ug1 = s1buf[...]
            ug2 = s2buf[...]
            gate = (pf_ref[PF_B:PF_B + 1, cs] + ug2 * pf_ref[PF_W0:PF_W0 + 1, cs] + ug1 * pf_ref[PF_W0 + 1:PF_W0 + 2, cs]
                    + ug * pf_ref[PF_W0 + 2:PF_W0 + 3, cs])
            sig = _sigmoid(gate)
            sl = gate * sig
            hm_ref[:, cs] = sl * uv
            dhm = _dot_nt(dyb, wd_ref[cs, :])
            duv = dhm * sl
            dgate = dhm * uv * (sig * (1.0 + gate * (1.0 - sig)))
            vf_ref[VF_B:VF_B + 1, cs] += _colsum(dgate)
            vf_ref[VF_W0:VF_W0 + 1, cs] += _colsum(dgate * ug2)
            vf_ref[VF_W0 + 1:VF_W0 + 2, cs] += _colsum(dgate * ug1)
            vf_ref[VF_W0 + 2:VF_W0 + 3, cs] += _colsum(dgate * ug)
            ebuf[0:t, cs] = dgate
            dug = (ebuf[2:2 + t, cs] * pf_ref[PF_W0:PF_W0 + 1, cs] + ebuf[1:1 + t, cs] * pf_ref[PF_W0 + 1:PF_W0 + 2, cs]
                   + dgate * pf_ref[PF_W0 + 2:PF_W0 + 3, cs])
            ebuf[t:t + FFN_HALO, cs] = ebuf[0:FFN_HALO, cs]
            dugb = dug.astype(BF16)
            duvb = duv.astype(BF16)
            dug_ref[:, cs] = dugb
            duv_ref[:, cs] = duvb
            dx = dx + _dot(dugb, wg_ref[cs, :]) + _dot(duvb, wv_ref[cs, :])
        dx_ref[...] = dx

        if last:
            @pl.when(i == nt - 1)
            def _():
                tot = jnp.sum(vd_ref[VD_LOSS:VD_LOSS + 1, :], axis=1, keepdims=True)
                vd_ref[VD_LOSS:VD_LOSS + 1, :] = jnp.zeros((1, D_MODEL), F32) + tot

    rev = lambda w: pl.BlockSpec((t, w), lambda i: (nt - 1 - i, 0))
    halo_spec = pl.BlockSpec((FFN_HALO, FF_P), lambda i: (jnp.maximum((nt - 1 - i) * hb - 1, 0), 0))
    out_shape = [jax.ShapeDtypeStruct((s, D_MODEL), F32), jax.ShapeDtypeStruct((s, D_MODEL), BF16),
                 jax.ShapeDtypeStruct((s, FF_P), BF16), jax.ShapeDtypeStruct((s, FF_P), BF16),
                 jax.ShapeDtypeStruct((s, FF_P), F32),
                 jax.ShapeDtypeStruct((8, D_MODEL), F32), jax.ShapeDtypeStruct((8, FF_P), F32)]
    return _call_with_side(
        body, side_ops, name="ffn_bwd_d_last" if last else "ffn_bwd_d", grid=(nt,),
        in_specs=[rev(D_MODEL), rev(D_MODEL), rev(1), rev(FF_P), halo_spec, rev(FF_P),
                  _layer_spec((8, D_MODEL), l), _layer_spec((8, FF_P), l),
                  _layer_spec((FF_P, D_MODEL), 0, resident=True),
                  _layer_spec((FF_P, D_MODEL), 0, 0, resident=True), _layer_spec((FF_P, D_MODEL), 0, 1, resident=True)],
        out_specs=[rev(D_MODEL), rev(D_MODEL), rev(FF_P), rev(FF_P), rev(FF_P),
                   _const_spec((8, D_MODEL)), _const_spec((8, FF_P))],
        out_shape=out_shape,
        scratch_shapes=[pltpu.VMEM((t + FFN_HALO, FF_H), F32), pltpu.VMEM((t + FFN_HALO, FF_P), F32),
                        pltpu.VMEM((t, FF_H), F32), pltpu.VMEM((t, FF_H), F32)],
        operands=(gz_or_target, xh2, rstd2, upg, upg, upv, pd, pf, w_d, wup_t, wup_t), semantics=("arbitrary",))


def _ffn_bwd_w(xh1, pd, dy, dug, duv, hm, l, t, side_ops=None):
    s = xh1.shape[0]
    nt = s // t

    def body(xh_ref, pd_ref, dy_ref, dug_ref, duv_ref, hm_ref, dwup_ref, dwd_ref):
        @pl.when(pl.program_id(1) == 0)
        def _():
            dwup_ref[...] = jnp.zeros_like(dwup_ref)
            dwd_ref[...] = jnp.zeros_like(dwd_ref)

        xb = (xh_ref[...] * _row(pd_ref, PD_G1) + _row(pd_ref, PD_B1)).astype(BF16)
        dwup_ref[0] += _dot_tn(dug_ref[...], xb)
        dwup_ref[1] += _dot_tn(duv_ref[...], xb)
        dwd_ref[...] += _dot_tn(hm_ref[...].astype(BF16), dy_ref[...])

    tok = lambda w: pl.BlockSpec((t, w), lambda c, i: (i, 0))
    half = pl.BlockSpec((t, FF_H), lambda c, i: (i, c))
    return _call_with_side(
        body, side_ops, name="ffn_bwd_w", grid=(2, nt),
        in_specs=[tok(D_MODEL), pl.BlockSpec((None, 8, D_MODEL), lambda c, i: (l, 0, 0)), tok(D_MODEL), half, half, half],
        out_specs=[pl.BlockSpec((None, 2, FF_H, D_MODEL), lambda c, i: (0, 0, c, 0), pipeline_mode=pl.Buffered(1)),
                   pl.BlockSpec((None, FF_H, D_MODEL), lambda c, i: (0, c, 0), pipeline_mode=pl.Buffered(1))],
        out_shape=[jax.ShapeDtypeStruct((1, 2, FF_P, D_MODEL), F32), jax.ShapeDtypeStruct((1, FF_P, D_MODEL), F32)],
        scratch_shapes=[], operands=(xh1, pd, dy, dug, duv, hm), semantics=("arbitrary", "arbitrary"))


def _adamw_math(w, g, m, v):
    nm = ADAM_B1 * m + (1.0 - ADAM_B1) * g
    nv = ADAM_B2 * v + (1.0 - ADAM_B2) * (g * g)
    m_hat = nm / (1.0 - ADAM_B1 ** ADAM_STEP)
    v_hat = nv / (1.0 - ADAM_B2 ** ADAM_STEP)
    return -ADAM_LR * (m_hat / (jnp.sqrt(v_hat) + ADAM_EPS) + ADAM_WD * w), nm, nv


def _adamw_layers(w, gs, m, v, name):
    shp = w.shape
    _, rows, cols = shp
    tr = _row_tile(rows, cols * 4, mult=8)
    nb = rows // tr

    def body(w_ref, g0_ref, g1_ref, m_ref, v_ref, g_ref, d_ref, nm_ref, nv_ref):
        g = jnp.where(pl.program_id(0) == 0, g0_ref[...], g1_ref[...])
        g_ref[...] = g
        d_ref[...], nm_ref[...], nv_ref[...] = _adamw_math(w_ref[...], g, m_ref[...], v_ref[...])

    stacked = pl.BlockSpec((tr, cols), lambda l, i: (l * nb + i, 0))
    single = pl.BlockSpec((tr, cols), lambda l, i: (i, 0))
    sh = jax.ShapeDtypeStruct((DEPTH * rows, cols), F32)
    flat = lambda a: a.reshape(DEPTH * rows, cols)
    outs = pl.pallas_call(body, name=name, grid=(DEPTH, nb), in_specs=[stacked, single, single, stacked, stacked],
                          out_specs=[stacked] * 4, out_shape=[sh] * 4,
                          compiler_params=_cparams(("arbitrary", "arbitrary")))(flat(w), gs[0], gs[1], flat(m), flat(v))
    return [o.reshape(shp) for o in outs]


def _adamw_small(ws, gs, ms, vs):
    n = len(ws)

    def body(*refs):
        w_refs, g_refs, m_refs, v_refs = refs[:n], refs[n:2 * n], refs[2 * n:3 * n], refs[3 * n:4 * n]
        d_refs, nm_refs, nv_refs = refs[4 * n:5 * n], refs[5 * n:6 * n], refs[6 * n:7 * n]
        for k in range(n):
            d_refs[k][...], nm_refs[k][...], nv_refs[k][...] = _adamw_math(w_refs[k][...], g_refs[k][...], m_refs[k][...],
                                                                             v_refs[k][...])

    shapes = [jax.ShapeDtypeStruct(w.shape, F32) for w in ws]
    outs = pl.pallas_call(body, name="adamw_small", out_shape=shapes * 3, compiler_params=_cparams())(*ws, *gs, *ms, *vs)
    return outs[:n], outs[n:2 * n], outs[2 * n:]


def _all_gather_chips(tensors, name):
    n = len(tensors)
    halves = [a.shape[1] // 2 for a in tensors]

    def body(*refs):
        x_refs, out_refs = refs[:n], refs[n:2 * n]
        send_sems, recv_sems, local_sems = refs[2 * n:]
        x, y, c, chips = _my_place()
        me, sibling, mej = (x, y, c), (x, y, 1 - c), 2 * x + y

        def rows(tn, px, py, pc):
            return out_refs[tn].at[:, 2 * px + py, pl.ds(pc * halves[tn], halves[tn]), :]

        def copy(tn, k, block, to, src=None):
            return pltpu.make_async_remote_copy(
                src_ref=rows(tn, *block) if src is None else src, dst_ref=rows(tn, *block),
                send_sem=send_sems.at[tn, k], recv_sem=recv_sems.at[tn, k], device_id=to, device_id_type=MESH)

        mine_src = [x_refs[tn].at[:, pl.ds(c * halves[tn], halves[tn]), :] for tn in range(n)]
        mine = [pltpu.make_async_copy(mine_src[tn], rows(tn, *me), local_sems.at[tn]) for tn in range(n)]
        first = []
        for j, chip in enumerate(chips):
            first += [copy(tn, 1 + j, me, (*chip, c), src=mine_src[tn]) for tn in range(n)]
        first += [copy(tn, 0, me, sibling, src=mine_src[tn]) for tn in range(n)]
        for cp in first + mine:
            cp.start()
        passed = []
        for j, chip in enumerate(chips):
            for tn in range(n):
                copy(tn, 1 + j, (*chip, c), me).wait_recv()
                fwd = copy(tn, 4 + j, (*chip, c), sibling)
                fwd.start()
                passed.append(fwd)
        for tn in range(n):
            copy(tn, 0, sibling, me).wait_recv()
            for j, chip in enumerate(chips):
                copy(tn, 4 + j, (*chip, 1 - c), me).wait_recv()
        for cp in first + passed:
            cp.wait_send()
        for cp in mine:
            cp.wait()

    return pl.pallas_call(
        body, name=name,
        out_shape=[jax.ShapeDtypeStruct((a.shape[0], N_CHIPS) + a.shape[1:], a.dtype) for a in tensors],
        in_specs=[ANY] * n, out_specs=[ANY] * n,
        scratch_shapes=[pltpu.SemaphoreType.DMA((n, 7)), pltpu.SemaphoreType.DMA((n, 7)), pltpu.SemaphoreType.DMA((n,))],
    )(*tensors)


def _swap_op(g5s):
    n = len(g5s)

    def make(ins, outs, scr):
        send_sems, recv_sems = scr
        x, y, c, _ = _my_place()

        def copies():
            return [pltpu.make_async_remote_copy(
                src_ref=ins[tn].at[:, :, 1 - c], dst_ref=outs[tn], send_sem=send_sems.at[tn], recv_sem=recv_sems.at[tn],
                device_id=(x, y, 1 - c), device_id_type=MESH) for tn in range(n)]

        def start():
            for cp in copies():
                cp.start()

        def wait():
            for cp in copies():
                cp.wait()

        return start, wait

    return _SideOp(g5s, [jax.ShapeDtypeStruct(g.shape[:2] + g.shape[3:], g.dtype) for g in g5s],
                   [pltpu.SemaphoreType.DMA((n,)), pltpu.SemaphoreType.DMA((n,))], make)


def _scatter_op(parts):
    n = len(parts)

    def make(ins, outs, scr):
        send_sems, recv_sems = scr
        x, y, c, chips = _my_place()

        def copies():
            return [pltpu.make_async_remote_copy(
                src_ref=ins[tn].at[:, 2 * px + py], dst_ref=outs[tn].at[:, k],
                send_sem=send_sems.at[tn, k], recv_sem=recv_sems.at[tn, k],
                device_id=(px, py, c), device_id_type=MESH) for k, (px, py) in enumerate(chips) for tn in range(n)]

        def start():
            for cp in copies():
                cp.start()

        def wait():
            for cp in copies():
                cp.wait()

        return start, wait

    return _SideOp(parts, [jax.ShapeDtypeStruct((p.shape[0], 3) + p.shape[2:], p.dtype) for p in parts],
                   [pltpu.SemaphoreType.DMA((n, 3)), pltpu.SemaphoreType.DMA((n, 3))], make)


def _sgather_op(fs):
    n = len(fs)

    def make(ins, outs, scr):
        send_sems, recv_sems = scr
        x, y, c, _ = _my_place()

        def copy(tn, dst_half):
            return pltpu.make_async_remote_copy(
                src_ref=outs[tn].at[:, c], dst_ref=outs[tn].at[:, dst_half], send_sem=send_sems.at[tn],
                recv_sem=recv_sems.at[tn], device_id=(x, y, 1 - c), device_id_type=MESH)

        def start():
            for tn in range(n):
                copy(tn, c).start()

        def wait():
            for tn in range(n):
                copy(tn, 1 - c).wait_recv()
                copy(tn, c).wait_send()

        return start, wait

    return _SideOp(fs, [jax.ShapeDtypeStruct(f.shape, f.dtype) for f in fs],
                   [pltpu.SemaphoreType.DMA((n,)), pltpu.SemaphoreType.DMA((n,))], make, aliases={tn: tn for tn in range(n)})


def _run_side_ops(ops, name):
    return _call_with_side(lambda: None, ops, name=name, grid=(1,), in_specs=[], out_specs=[], out_shape=[],
                           scratch_shapes=[], operands=(), semantics=("arbitrary",))[1]


def _gather_devices_op(xs):
    def make(ins, outs, scr):
        send_sems, recv_sems, local_sem = scr
        x, y, c, chips = _my_place()
        peers = [(x, y, 1 - c)] + [(px, py, pc) for (px, py) in chips for pc in (c, 1 - c)]
        me = 4 * x + 2 * y + c

        def copy(k, slot):
            return pltpu.make_async_remote_copy(
                src_ref=ins[0], dst_ref=outs[0].at[slot], send_sem=send_sems.at[k], recv_sem=recv_sems.at[k],
                device_id=peers[k], device_id_type=MESH)

        def local():
            return pltpu.make_async_copy(ins[0], outs[0].at[me], local_sem)

        def start():
            for k in range(7):
                copy(k, me).start()
            local().start()

        def wait():
            for k, (px, py, pc) in enumerate(peers):
                copy(k, 4 * px + 2 * py + pc).wait_recv()
                copy(k, me).wait_send()
            local().wait()

        return start, wait

    return _SideOp([xs], [jax.ShapeDtypeStruct((8,) + xs.shape, xs.dtype)],
                   [pltpu.SemaphoreType.DMA((7,)), pltpu.SemaphoreType.DMA((7,)), pltpu.SemaphoreType.DMA], make)


def _add_halves(gs, recvs, place):
    n = len(gs)

    def body(place_ref, *refs):
        g_refs, r_refs, o_refs = refs[:n], refs[n:2 * n], refs[2 * n:]
        for tn in range(n):
            o_refs[tn][...] = (g_refs[tn][...] + r_refs[tn][...]).astype(BF16)

    def gspec(g):
        return pl.BlockSpec((None, None, None) + g.shape[3:], lambda l, j, p: (l, j, p[1], 0, 0))

    def rspec(r):
        return pl.BlockSpec((None, None) + r.shape[2:], lambda l, j, p: (l, j, 0, 0))

    grid_spec = pltpu.PrefetchScalarGridSpec(
        num_scalar_prefetch=1, grid=(gs[0].shape[0], N_CHIPS),
        in_specs=[gspec(g) for g in gs] + [rspec(r) for r in recvs], out_specs=[rspec(r) for r in recvs])
    return pl.pallas_call(body, name="rs_add", grid_spec=grid_spec,
                          out_shape=[jax.ShapeDtypeStruct(r.shape, BF16) for r in recvs],
                          compiler_params=_cparams(("arbitrary", "arbitrary")))(place, *gs, *recvs)


def _sum_slots(parts, slots, place):
    n = len(parts)

    def body(place_ref, *refs):
        p_refs, s_refs, o_refs = refs[:n], refs[n:2 * n], refs[2 * n:]
        for tn in range(n):
            acc = p_refs[tn][...].astype(F32)
            for k in range(3):
                acc = acc + s_refs[tn][k].astype(F32)
            o_refs[tn][...] = acc

    def pspec(p):
        return pl.BlockSpec((None, None) + p.shape[2:], lambda l, pl_: (l, pl_[0], 0, 0))

    def sspec(sl):
        return pl.BlockSpec((None,) + sl.shape[1:], lambda l, pl_: (l, 0, 0, 0))

    def ospec(p):
        return pl.BlockSpec((None, None) + p.shape[2:], lambda l, pl_: (l, pl_[1], 0, 0))

    grid_spec = pltpu.PrefetchScalarGridSpec(
        num_scalar_prefetch=1, grid=(parts[0].shape[0],),
        in_specs=[pspec(p) for p in parts] + [sspec(sl) for sl in slots], out_specs=[ospec(p) for p in parts])
    return pl.pallas_call(body, name="rs_sum", grid_spec=grid_spec,
                          out_shape=[jax.ShapeDtypeStruct((p.shape[0], 2) + p.shape[2:], F32) for p in parts],
                          compiler_params=_cparams(("arbitrary",)))(place, *parts, *slots)


def _sum_devices(gathered):
    m_per = gathered[0].shape[1]

    def body(*refs):
        o_ref = refs[-1]
        for l, g_ref in enumerate(refs[:-1]):
            acc = g_ref[0]
            for d in range(1, 8):
                acc = acc + g_ref[d]
            o_ref[l * m_per:(l + 1) * m_per, :] = acc

    return pl.pallas_call(body, name="small_sum", out_shape=jax.ShapeDtypeStruct((len(gathered) * m_per, LANE), F32),
                          compiler_params=_cparams())(*gathered)


def _pad_ff_cols(a):
    lead = a.shape[:-1]
    n = a.shape[-1] // FF_Q
    a = a.reshape(*lead, n, FF_Q)
    a = jnp.pad(a, [(0, 0)] * len(lead) + [(0, 0), (0, FF_QP - FF_Q)])
    return a.reshape(*lead, n * FF_QP)


def _unpad_ff_cols(a):
    lead = a.shape[:-1]
    n = a.shape[-1] // FF_QP
    return a.reshape(*lead, n, FF_QP)[..., :FF_Q].reshape(*lead, n * FF_Q)


def _pack_small(parts):
    flat = jnp.concatenate([p.reshape(-1) for p in parts])
    return flat.reshape(-1, LANE)


SMALL_SHAPES = [("conv_a_w", (CONV_K, CONV_W)), ("conv_a_b", (CONV_W,)), ("ln_a_g", (CONV_W,)), ("ln_a_b", (CONV_W,)),
                ("ln_v_g", (GMLP_W,)), ("ln_v_b", (GMLP_W,)), ("w_s", (6, CHUNK, CHUNK)), ("b_s", (6, CHUNK)),
                ("ln1_g", (D_MODEL,)), ("ln1_b", (D_MODEL,)), ("conv_f_w", (FFN_CONV_K, D_FF)), ("conv_f_b", (D_FF,)),
                ("ln2_g", (D_MODEL,)), ("ln2_b", (D_MODEL,))]
SMALL_FLOATS = 128000


def _unpack_small(flat2d):
    flat = flat2d.reshape(DEPTH, -1)
    out, o = {}, 0
    for name, shp in SMALL_SHAPES:
        n = 1
        for d in shp:
            n *= d
        out[name] = flat[:, o:o + n].reshape((DEPTH,) + shp)
        o += n
    return out


def _rows8(rows):
    blk = jnp.stack(rows, axis=1)
    return jnp.pad(blk, ((0, 0), (0, 8 - len(rows)), (0, 0)))


def kernel(x, mem, w_in, conv_a_w, conv_a_b, ln_a_g, ln_a_b, ln_v_g, ln_v_b, w_s, b_s, w_mk, w_mv, w_out, ln1_g, ln1_b, w_up, conv_f_w, conv_f_b, w_down, ln2_g, ln2_b, loss_target, m_w_in, m_conv_a_w, m_conv_a_b, m_ln_a_g, m_ln_a_b, m_ln_v_g, m_ln_v_b, m_w_s, m_b_s, m_w_mk, m_w_mv, m_w_out, m_ln1_g, m_ln1_b, m_w_up, m_conv_f_w, m_conv_f_b, m_w_down, m_ln2_g, m_ln2_b, v_w_in, v_conv_a_w, v_conv_a_b, v_ln_a_g, v_ln_a_b, v_ln_v_g, v_ln_v_b, v_w_s, v_b_s, v_w_mk, v_w_mv, v_w_out, v_ln1_g, v_ln1_b, v_w_up, v_conv_f_w, v_conv_f_b, v_w_down, v_ln2_g, v_ln2_b):
    seq = x.shape[1]
    t_fwd = min(512, seq)
    t_bwd = min(256, seq)
    t_wg = min(1024, seq)
    chip = 2 * lax.axis_index("x") + lax.axis_index("y")
    core = lax.axis_index("c")
    place = jnp.stack([chip, core]).astype(jnp.int32)
    x0 = x[0]
    mem0 = mem[0]
    target = loss_target[0]

    sh_in = w_in.transpose(0, 2, 1).astype(BF16)
    sh_mk, sh_mv, sh_out = w_mk.astype(BF16), w_mv.astype(BF16), w_out.astype(BF16)
    sh_up = _pad_ff_cols(w_up).transpose(0, 2, 1).astype(BF16)
    sh_dn = jnp.pad(w_down, ((0, 0), (0, FF_QP - FF_Q), (0, 0))).astype(BF16)

    def mixer_weights(g_in, g_mk, g_mv, g_out):
        return dict(win_t=g_in.reshape(1, IN_W, D_MODEL), wmk=g_mk.reshape(1, D_MODEL, XATTN_W),
                    wmv=g_mv.reshape(1, D_MODEL, XATTN_W), wout=g_out.reshape(1, D_MODEL, D_MODEL))

    def ffn_weights(g_up, g_dn):
        return dict(wup_t=g_up.reshape(1, 2, FF_P, D_MODEL), wdown=g_dn.reshape(1, FF_P, D_MODEL))

    n_ca = conv_a_w.size
    small_w = _pack_small([conv_a_w, conv_f_w, jnp.zeros((2 * 80 * LANE - n_ca - conv_f_w.size,), F32)])[None]
    *g_mixer0, small_g = _all_gather_chips([sh_in[:1], sh_mk[:1], sh_mv[:1], sh_out[:1], small_w], "ag_mixer0")
    wts = [mixer_weights(*g_mixer0), None]
    small_g = small_g.reshape(N_CHIPS, -1)
    conv_a_full = small_g[:, :n_ca].reshape(N_CHIPS, DEPTH, CONV_K, CONV_W // 4).transpose(1, 2, 0, 3).reshape(DEPTH, CONV_K, CONV_W)
    conv_f_full = small_g[:, n_ca:n_ca + conv_f_w.size].reshape(N_CHIPS, DEPTH, FFN_CONV_K, FF_Q).transpose(1, 2, 0, 3).reshape(DEPTH, FFN_CONV_K, D_FF)

    tril = jnp.tril(jnp.ones((CHUNK, CHUNK), dtype=bool))
    ws_m = jnp.where(tril, w_s, 0.0)
    wst = ws_m.reshape(DEPTH, 3, 2 * CHUNK, CHUNK).astype(BF16)
    wstt = ws_m.transpose(0, 1, 3, 2).reshape(DEPTH, 3, 2 * CHUNK, CHUNK).astype(BF16)
    bst = jnp.repeat(b_s.transpose(0, 2, 1), HEAD_DIM, axis=2)
    conv_w = jnp.pad(conv_a_full, ((0, 0), (0, CONV_HALO - CONV_K), (0, 0)))
    pa = _rows8([conv_a_b, ln_a_g, ln_a_b, ln_v_g, ln_v_b])
    gin = jnp.concatenate([jnp.ones((1, D_MODEL), F32), ln2_g[:DEPTH - 1]], axis=0)
    bin_ = jnp.concatenate([jnp.zeros((1, D_MODEL), F32), ln2_b[:DEPTH - 1]], axis=0)
    pd = _rows8([gin, bin_, ln1_g, ln1_b, ln2_g, ln2_b])
    pf = jnp.concatenate([_pad_ff_cols(conv_f_full), _pad_ff_cols(conv_f_b)[:, None, :],
                          jnp.zeros((DEPTH, 8 - FFN_CONV_K - 1, FF_P), F32)], axis=1)

    acts = []
    xin = x0
    for l in range(DEPTH):
        w = wts[l]
        kt_all, k_all, v_all, vt_all = _kv_fwd(mem0, w["wmk"], w["wmv"])
        ops = [_gather_chips_op([sh_up[0], sh_dn[0]])] if l == 0 else None
        (xh1, rstd1, h, ac, cat, mixed, probs), side = _mixer_fwd(xin, pd, w["win_t"], conv_w, pa, wst, bst, kt_all, v_all, w["wout"],
                                                    l, t_fwd, ops)
        if l == 0:
            w.update(ffn_weights(*side[0]))
        ops = [_gather_chips_op([sh_in[1], sh_mk[1], sh_mv[1], sh_out[1], sh_up[1], sh_dn[1]])] if l == 0 else None
        (xh2, rstd2, upg, upv), side = _ffn_fwd(xh1, pd, w["wup_t"], pf, w["wdown"], l, t_fwd, ops)
        if l == 0:
            wts[1] = {**mixer_weights(*side[0][:4]), **ffn_weights(*side[0][4:])}
        acts.append(dict(xin=xin, k_all=k_all, vt_all=vt_all, xh1=xh1, rstd1=rstd1, h=h, ac=ac, cat=cat, mixed=mixed, probs=probs,
                         xh2=xh2, rstd2=rstd2, upg=upg, upv=upv))
        xin = xh2

    assert DEPTH == 2

    def halves_view(gs):
        return [g.reshape(1, N_CHIPS, 2, g.shape[1] // (2 * N_CHIPS), g.shape[2]) for g in gs]

    small = [None] * DEPTH
    small_packed = [None] * DEPTH
    small_gathered = [None] * DEPTH
    red_layers = [None] * DEPTH
    gz = target
    g5_prev = None
    for l in reversed(range(DEPTH)):
        a, w = acts[l], wts[l]
        last = l == DEPTH - 1
        (dx1, dy, dug, duv, hm, vd2, vf), side = _ffn_bwd_d(
            gz, a["xh2"], a["rstd2"], a["upg"], a["upv"], pd, pf, w["wdown"], w["wup_t"], l, t_bwd, last,
            [_swap_op(g5_prev), _gather_devices_op(small_packed[l + 1])] if g5_prev else None)
        if g5_prev:
            small_gathered[l + 1] = side[1][0]
        parts_prev = _add_halves(g5_prev, side[0], place) if g5_prev else None
        (gw_up_t, gw_down), side = _ffn_bwd_w(a["xh1"], pd, dy, dug, duv, hm, l, t_wg,
                                              [_scatter_op(parts_prev)] if g5_prev else None)
        halves_prev = _sum_slots(parts_prev, side[0], place) if g5_prev else None
        g5_ffn = halves_view([gw_up_t.reshape(1, 2 * FF_P, D_MODEL), gw_down])
        ops = ([_sgather_op(halves_prev)] if g5_prev else []) + ([_swap_op(g5_ffn)] if l == 0 else [])
        (dx0, dh, dmix, vd1, va, dcw, dws, dbs, dkt, dv), side = _mixer_bwd_d(
            dx1, a["xh1"], a["rstd1"], a["h"], a["ac"], a["mixed"], a["probs"], pd, conv_w, pa, wstt,
            a["k_all"], a["vt_all"], w["wout"], w["win_t"], l, t_fwd, ops)
        if g5_prev:
            red_layers[l + 1] = side[0]
        parts_ffn = _add_halves(g5_ffn, side[-1], place) if l == 0 else None
        dws6 = jnp.where(tril, dws.reshape(6, CHUNK, CHUNK), 0.0)
        small[l] = [dcw[:CONV_K], va[VA_CONV_B], va[VA_LNA_G], va[VA_LNA_B], va[VA_LNV_G], va[VA_LNV_B], dws6,
                    dbs[:, :6].T, vd1[VD_LN_G], vd1[VD_LN_B],
                    _unpad_ff_cols(vf[VF_W0:VF_W0 + FFN_CONV_K]), _unpad_ff_cols(vf[VF_B]),
                    vd2[VD_LN_G], vd2[VD_LN_B]]
        small[l].append(vd2[VD_LOSS] if last else jnp.zeros((D_MODEL,), F32))
        small_packed[l] = _pack_small(small[l])
        ops = [_scatter_op(parts_ffn), _gather_devices_op(small_packed[l])] if l == 0 else None
        (gw_in_t, gw_out), side = _mixer_bwd_w(a["xin"], pd, dh, a["cat"], dmix, l, t_wg, ops)
        gw_mk, gw_mv = _kv_bwd(mem0, dkt, dv)
        g5_mix = halves_view([gw_in_t, gw_mk, gw_mv, gw_out])
        if l == 0:
            small_gathered[l] = side[1][0]
            halves_ffn = _sum_slots(parts_ffn, side[0], place)
            red_ffn, recv_mix = _run_side_ops([_sgather_op(halves_ffn), _swap_op(g5_mix)], "rs_tail_swap")
            parts_mix = _add_halves(g5_mix, recv_mix, place)
            halves_mix = _sum_slots(parts_mix, _run_side_ops([_scatter_op(parts_mix)], "rs_tail_chips")[0], place)
            red_mix = _run_side_ops([_sgather_op(halves_mix)], "rs_tail_gather")[0]
            red_layers[0] = red_mix + red_ffn
        else:
            g5_prev = g5_mix + g5_ffn
        gz = dx0
    grad_x = gz[None]

    def shard_grads(red):
        r = [f.reshape(-1, f.shape[-1]) for f in red]
        return dict(w_in=r[0].T, w_mk=r[1], w_mv=r[2], w_out=r[3], w_up=_unpad_ff_cols(r[4].T), w_down=r[5])

    big_grads = [shard_grads(red_layers[l]) for l in range(DEPTH)]

    small_red = _sum_devices(small_gathered)
    sg = _unpack_small(small_red)
    g_conv_a_w = lax.dynamic_slice_in_dim(sg["conv_a_w"], chip * (CONV_W // 4), CONV_W // 4, axis=2)
    g_conv_f_w = lax.dynamic_slice_in_dim(sg["conv_f_w"], chip * FF_Q, FF_Q, axis=2)

    loss = 0.5 / D_MODEL * small_red.reshape(DEPTH, -1)[DEPTH - 1, SMALL_FLOATS]

    grads = dict(conv_a_w=g_conv_a_w, conv_a_b=sg["conv_a_b"], ln_a_g=sg["ln_a_g"], ln_a_b=sg["ln_a_b"],
                 ln_v_g=sg["ln_v_g"], ln_v_b=sg["ln_v_b"], w_s=sg["w_s"], b_s=sg["b_s"], ln1_g=sg["ln1_g"], ln1_b=sg["ln1_b"],
                 conv_f_w=g_conv_f_w, conv_f_b=sg["conv_f_b"], ln2_g=sg["ln2_g"], ln2_b=sg["ln2_b"])
    weights = dict(w_in=w_in, conv_a_w=conv_a_w, conv_a_b=conv_a_b, ln_a_g=ln_a_g, ln_a_b=ln_a_b, ln_v_g=ln_v_g,
                   ln_v_b=ln_v_b, w_s=w_s, b_s=b_s, w_mk=w_mk, w_mv=w_mv, w_out=w_out, ln1_g=ln1_g, ln1_b=ln1_b,
                   w_up=w_up, conv_f_w=conv_f_w, conv_f_b=conv_f_b, w_down=w_down, ln2_g=ln2_g, ln2_b=ln2_b)
    mom_m = dict(w_in=m_w_in, conv_a_w=m_conv_a_w, conv_a_b=m_conv_a_b, ln_a_g=m_ln_a_g, ln_a_b=m_ln_a_b, ln_v_g=m_ln_v_g,
                 ln_v_b=m_ln_v_b, w_s=m_w_s, b_s=m_b_s, w_mk=m_w_mk, w_mv=m_w_mv, w_out=m_w_out, ln1_g=m_ln1_g,
                 ln1_b=m_ln1_b, w_up=m_w_up, conv_f_w=m_conv_f_w, conv_f_b=m_conv_f_b, w_down=m_w_down, ln2_g=m_ln2_g,
                 ln2_b=m_ln2_b)
    mom_v = dict(w_in=v_w_in, conv_a_w=v_conv_a_w, conv_a_b=v_conv_a_b, ln_a_g=v_ln_a_g, ln_a_b=v_ln_a_b, ln_v_g=v_ln_v_g,
                 ln_v_b=v_ln_v_b, w_s=v_w_s, b_s=v_b_s, w_mk=v_w_mk, w_mv=v_w_mv, w_out=v_w_out, ln1_g=v_ln1_g,
                 ln1_b=v_ln1_b, w_up=v_w_up, conv_f_w=v_conv_f_w, conv_f_b=v_conv_f_b, w_down=v_w_down, ln2_g=v_ln2_g,
                 ln2_b=v_ln2_b)
    names = list(weights)
    big_names = ["w_in", "w_mk", "w_mv", "w_out", "w_up", "w_down"]
    delta, new_m, new_v = {}, {}, {}
    for n in big_names:
        grads[n], delta[n], new_m[n], new_v[n] = _adamw_layers(weights[n], [big_grads[l][n] for l in range(DEPTH)],
                                                               mom_m[n], mom_v[n], "adamw_" + n)
    small_names = [n for n in names if n not in big_names]
    ds, nms, nvs = _adamw_small([weights[n] for n in small_names], [grads[n] for n in small_names],
                                [mom_m[n] for n in small_names], [mom_v[n] for n in small_names])
    for n, d, nm, nv in zip(small_names, ds, nms, nvs):
        delta[n], new_m[n], new_v[n] = d, nm, nv

    return (loss, grad_x, *[grads[n] for n in names], *[delta[n] for n in names],
            *[new_m[n] for n in names], *[new_v[n] for n in names])
```

```python
import jax
import jax.numpy as jnp
from jax import lax
from jax.experimental import pallas as pl
from jax.experimental.pallas import tpu as pltpu

F32 = jnp.float32
BF16 = jnp.bfloat16

D_MODEL = 1024
DEPTH = 2
CONV_W = 384
GMLP_W = 384
XATTN_W = 256
XATTN_HEADS = 4
HEAD_DIM = 64
IN_W = 1792
CONV_K = 31
CHUNK = 128
N_MEM = 256
D_FF = 2752
FFN_CONV_K = 3
ALPHA = (2.0 * DEPTH) ** 0.25
LN_EPS = 1e-5
ATT_SCALE = 1.0 / 8.0
ADAM_LR, ADAM_B1, ADAM_B2, ADAM_EPS, ADAM_WD, ADAM_STEP = 0.001, 0.9, 0.999, 1e-08, 0.01, 10

N_CHIPS = 4
FF_Q = D_FF // N_CHIPS
FF_QP = 704
FF_H = 2 * FF_QP
FF_P = 4 * FF_QP
LANE = 128
CONV_HALO = 32
FFN_HALO = 8
BF16_ROWS = 16
VMEM_LIMIT = 60 * 1024 * 1024

MESH = pl.DeviceIdType.MESH
ANY = pl.BlockSpec(memory_space=pl.ANY)


def _cparams(sem=None, vmem=VMEM_LIMIT):
    kw = {"vmem_limit_bytes": vmem}
    if sem is not None:
        kw["dimension_semantics"] = sem
    return pltpu.CompilerParams(**kw)


def _row_tile(rows, row_bytes, limit=2 << 20, mult=BF16_ROWS):
    if rows * row_bytes <= limit:
        return rows
    best = None
    for cand in range(mult, rows, mult):
        if rows % cand == 0 and cand * row_bytes <= limit:
            best = cand
    assert best is not None, (rows, row_bytes)
    return best


def _const_spec(shape):
    nd = len(shape)
    return pl.BlockSpec(shape, lambda *_: (0,) * nd)


def _layer_spec(shape, *lead, resident=False):
    nd = len(shape)
    kw = {"pipeline_mode": pl.Buffered(1)} if resident else {}
    return pl.BlockSpec((None,) * len(lead) + tuple(shape), lambda *_: tuple(lead) + (0,) * nd, **kw)


def _sigmoid(x):
    return jax.nn.sigmoid(x)


def _gelu(x):
    return jax.nn.gelu(x)


def _gelu_and_grad(x):
    c = 0.7978845608028654
    a = 0.044715
    x2 = x * x
    t = jnp.tanh(c * (x + a * x * x2))
    h = 0.5 * (1.0 + t)
    return x * h, h + 0.5 * x * (1.0 - t * t) * c * (1.0 + 3.0 * a * x2)


def _ln_fwd(z):
    mu = jnp.mean(z, axis=-1, keepdims=True)
    zc = z - mu
    var = jnp.mean(zc * zc, axis=-1, keepdims=True)
    rstd = lax.rsqrt(var + LN_EPS)
    return zc * rstd, rstd


def _ln_bwd(dxh, xh, rstd):
    m1 = jnp.mean(dxh, axis=-1, keepdims=True)
    m2 = jnp.mean(dxh * xh, axis=-1, keepdims=True)
    return rstd * (dxh - m1 - xh * m2)


def _colsum(a):
    return jnp.sum(a, axis=0, keepdims=True)


def _dot(a, b):
    return jnp.dot(a, b, preferred_element_type=F32)


def _dot_tn(a, b):
    return lax.dot_general(a, b, (((0,), (0,)), ((), ())), preferred_element_type=F32)


def _dot_nt(a, b):
    return lax.dot_general(a, b, (((1,), (1,)), ((), ())), preferred_element_type=F32)


def _softmax_heads(sc):
    ps = []
    for hd in range(XATTN_HEADS):
        s = sc[:, hd * N_MEM:(hd + 1) * N_MEM]
        e = jnp.exp(s - jnp.max(s, axis=-1, keepdims=True))
        ps.append(e / jnp.sum(e, axis=-1, keepdims=True))
    return jnp.concatenate(ps, axis=1)


def _lane_lo(shape):
    return (lax.broadcasted_iota(jnp.int32, shape, len(shape) - 1) % LANE) < HEAD_DIM


def _spatial_mix(vnb, wst_ref, bst_ref, mix_ref, t):
    lo = _lane_lo((CHUNK, LANE))
    for n in range(t // CHUNK):
        rows = slice(n * CHUNK, (n + 1) * CHUNK)
        for j in range(GMLP_W // LANE):
            cols = slice(j * LANE, (j + 1) * LANE)
            r = _dot(wst_ref[j], vnb[rows, cols])
            mix_ref[rows, cols] = jnp.where(lo, r[:CHUNK], r[CHUNK:]) + bst_ref[:, cols]


def _kv_fwd(mem, w_mk, w_mv):
    def body(mem_ref, wk_ref, wv_ref, kt_ref, k_ref, v_ref, vt_ref):
        mb = mem_ref[...].astype(BF16)
        k = _dot(mb, wk_ref[...])
        v = _dot(mb, wv_ref[...])
        col = lax.broadcasted_iota(jnp.int32, (N_MEM, XATTN_W), 1) // HEAD_DIM
        ks = [jnp.where(col == hd, k, 0.0) for hd in range(XATTN_HEADS)]
        vs = [jnp.where(col == hd, v, 0.0) for hd in range(XATTN_HEADS)]
        k_ref[...] = jnp.concatenate(ks, axis=0).astype(BF16)
        v_ref[...] = jnp.concatenate(vs, axis=0).astype(BF16)
        kt_ref[...] = jnp.concatenate([x.T for x in ks], axis=1).astype(BF16)
        vt_ref[...] = jnp.concatenate([x.T for x in vs], axis=1).astype(BF16)

    wide = jax.ShapeDtypeStruct((XATTN_W, XATTN_HEADS * N_MEM), BF16)
    tall = jax.ShapeDtypeStruct((XATTN_HEADS * N_MEM, XATTN_W), BF16)
    wspec = _layer_spec((D_MODEL, XATTN_W), 0)
    return pl.pallas_call(body, name="kv_fwd", grid=(1,),
                          in_specs=[_const_spec((N_MEM, D_MODEL)), wspec, wspec],
                          out_specs=[_const_spec(wide.shape), _const_spec(tall.shape), _const_spec(tall.shape),
                                     _const_spec(wide.shape)],
                          out_shape=(wide, tall, tall, wide), compiler_params=_cparams(("arbitrary",)))(mem, w_mk, w_mv)


def _kv_bwd(mem, dkt_all, dv_all):
    def body(mem_ref, dkt_ref, dv_ref, gk_ref, gv_ref):
        col = lax.broadcasted_iota(jnp.int32, (N_MEM, XATTN_W), 1) // HEAD_DIM
        dk = jnp.zeros((N_MEM, XATTN_W), F32)
        dv = jnp.zeros((N_MEM, XATTN_W), F32)
        for hd in range(XATTN_HEADS):
            dk = dk + jnp.where(col == hd, dkt_ref[:, hd * N_MEM:(hd + 1) * N_MEM].T, 0.0)
            dv = dv + jnp.where(col == hd, dv_ref[hd * N_MEM:(hd + 1) * N_MEM, :], 0.0)
        mb = mem_ref[...].astype(BF16)
        gk_ref[0] = _dot_tn(mb, dk.astype(BF16))
        gv_ref[0] = _dot_tn(mb, dv.astype(BF16))

    out = jax.ShapeDtypeStruct((1, D_MODEL, XATTN_W), F32)
    return pl.pallas_call(body, name="kv_bwd", out_shape=(out, out), compiler_params=_cparams())(mem, dkt_all, dv_all)


def _my_place():
    x, y, c = lax.axis_index("x"), lax.axis_index("y"), lax.axis_index("c")
    chips = [(1 - x, y), (x, 1 - y), (1 - x, 1 - y)]
    return x, y, c, chips


class _SideOp:
    def __init__(self, ins, out_shapes, scratch, make, aliases=None):
        self.ins, self.out_shapes, self.scratch, self.make, self.aliases = list(ins), list(out_shapes), list(scratch), make, dict(aliases or {})


def _call_with_side(body, side_ops, *, name, grid, in_specs, out_specs, out_shape, scratch_shapes, operands, semantics):
    side_ops = list(side_ops or ())
    n_in, n_out, n_scr = len(in_specs), len(out_specs), len(scratch_shapes)
    s_ins = [a for op in side_ops for a in op.ins]
    s_outs = [o for op in side_ops for o in op.out_shapes]
    s_scr = [x for op in side_ops for x in op.scratch]
    aliases, oi, oo = {}, 0, 0
    for op in side_ops:
        for a, b in op.aliases.items():
            aliases[n_in + oi + a] = n_out + oo + b
        oi, oo = oi + len(op.ins), oo + len(op.out_shapes)

    def wrapped(*refs):
        ins, sins = refs[:n_in], refs[n_in:n_in + len(s_ins)]
        base = n_in + len(s_ins)
        outs, souts = refs[base:base + n_out], refs[base + n_out:base + n_out + len(s_outs)]
        base += n_out + len(s_outs)
        scr, sscr = refs[base:base + n_scr], refs[base + n_scr:]
        if side_ops:
            first = pl.program_id(0) == 0
            last = pl.program_id(0) == grid[0] - 1
            for d in range(1, len(grid)):
                first = jnp.logical_and(first, pl.program_id(d) == 0)
                last = jnp.logical_and(last, pl.program_id(d) == grid[d] - 1)
            hooks, a, b, c = [], 0, 0, 0
            for op in side_ops:
                hooks.append(op.make(sins[a:a + len(op.ins)], souts[b:b + len(op.out_shapes)], sscr[c:c + len(op.scratch)]))
                a, b, c = a + len(op.ins), b + len(op.out_shapes), c + len(op.scratch)

            @pl.when(first)
            def _():
                for start, _w in hooks:
                    start()

        body(*ins, *outs, *scr)
        if side_ops:
            @pl.when(last)
            def _():
                for _s, wait in hooks:
                    wait()

    res = pl.pallas_call(
        wrapped, name=name, grid=grid, in_specs=list(in_specs) + [ANY] * len(s_ins),
        out_specs=list(out_specs) + [ANY] * len(s_outs), out_shape=list(out_shape) + s_outs,
        scratch_shapes=list(scratch_shapes) + s_scr, input_output_aliases=aliases,
        compiler_params=_cparams(semantics),
    )(*operands, *s_ins)
    side_res, k = [], n_out
    for op in side_ops:
        side_res.append(list(res[k:k + len(op.out_shapes)]))
        k += len(op.out_shapes)
    return list(res[:n_out]), side_res


def _gather_chips_op(shards):
    n = len(shards)

    def make(ins, outs, scr):
        send_sems, recv_sems, local_sems = scr
        x, y, c, chips = _my_place()
        mej = 2 * x + y

        def remote(tn, k, slot):
            px, py = chips[k]
            return pltpu.make_async_remote_copy(
                src_ref=ins[tn], dst_ref=outs[tn].at[slot], send_sem=send_sems.at[tn, k],
                recv_sem=recv_sems.at[tn, k], device_id=(px, py, c), device_id_type=MESH)

        def local(tn):
            return pltpu.make_async_copy(ins[tn], outs[tn].at[mej], local_sems.at[tn])

        def start():
            for k in range(3):
                for tn in range(n):
                    remote(tn, k, mej).start()
            for tn in range(n):
                local(tn).start()

        def wait():
            for k, (px, py) in enumerate(chips):
                for tn in range(n):
                    remote(tn, k, 2 * px + py).wait_recv()
                    remote(tn, k, mej).wait_send()
            for tn in range(n):
                local(tn).wait()

        return start, wait

    return _SideOp(shards, [jax.ShapeDtypeStruct((N_CHIPS,) + a.shape, a.dtype) for a in shards],
                   [pltpu.SemaphoreType.DMA((n, 3)), pltpu.SemaphoreType.DMA((n, 3)), pltpu.SemaphoreType.DMA((n,))], make)


PA_CONV_B, PA_LNA_G, PA_LNA_B, PA_LNV_G, PA_LNV_B = 0, 1, 2, 3, 4
PD_GIN, PD_BIN, PD_G1, PD_B1, PD_G2, PD_B2 = 0, 1, 2, 3, 4, 5


def _row(ref, r):
    return ref[r:r + 1, :]


def _mixer_fwd(xin, pd, win_t, conv_w, pa, wst, bst, kt_all, v_all, w_out, l, t, side_ops=None):
    s = xin.shape[0]
    nt = s // t

    def body(x_ref, pd_ref, wint_ref, cw_ref, pa_ref, wst_ref, bst_ref, kt_ref, v_ref, wout_ref,
             xh_ref, rstd_ref, h_ref, ac_ref, cat_ref, mixed_ref, p_ref, cbuf, zbuf):
        i = pl.program_id(0)
        x = x_ref[...] * _row(pd_ref, PD_GIN) + _row(pd_ref, PD_BIN)
        h = _dot_nt(x.astype(BF16), wint_ref[...])
        h_ref[...] = h
        a1, a2 = h[:, 0:CONV_W], h[:, CONV_W:2 * CONV_W]
        hu, hv = h[:, 2 * CONV_W:2 * CONV_W + GMLP_W], h[:, 2 * CONV_W + GMLP_W:2 * CONV_W + 2 * GMLP_W]
        q = h[:, IN_W - XATTN_W:]

        @pl.when(i == 0)
        def _():
            cbuf[0:CONV_HALO, :] = jnp.zeros((CONV_HALO, CONV_W), F32)

        cbuf[CONV_HALO:CONV_HALO + t, :] = a1 * _sigmoid(a2)
        ac = jnp.zeros((t, CONV_W), F32) + _row(pa_ref, PA_CONV_B)
        for r in range(8):
            zr = jnp.zeros((t + 8, CONV_W), F32)
            for a in range(4):
                o = 8 * a + r
                if o < CONV_K:
                    k = CONV_K - 1 - o
                    zr = zr + cbuf[CONV_HALO - 8 - 8 * a:CONV_HALO - 8 - 8 * a + t + 8, :] * cw_ref[k:k + 1, :]
            if r == 0:
                ac = ac + zr[8:, :]
            else:
                zbuf[...] = zr
                ac = ac + zbuf[8 - r:8 - r + t, :]
        ac_ref[...] = ac
        cbuf[0:CONV_HALO, :] = cbuf[t:t + CONV_HALO, :]
        xh_a, _ = _ln_fwd(ac)
        an = xh_a * _row(pa_ref, PA_LNA_G) + _row(pa_ref, PA_LNA_B)
        a = an * _sigmoid(an)

        u = _gelu(hu)
        xh_v, _ = _ln_fwd(_gelu(hv))
        vn = xh_v * _row(pa_ref, PA_LNV_G) + _row(pa_ref, PA_LNV_B)
        _spatial_mix(vn.astype(BF16), wst_ref, bst_ref, mixed_ref, t)
        g = u * mixed_ref[...]

        p = _softmax_heads(_dot(q.astype(BF16), kt_ref[...]) * ATT_SCALE)
        p_ref[...] = p
        o = _dot(p.astype(BF16), v_ref[...])

        cat = jnp.concatenate([a, g, o], axis=1).astype(BF16)
        cat_ref[...] = cat
        z = ALPHA * x + _dot(cat, wout_ref[...])
        xh, rstd = _ln_fwd(z)
        xh_ref[...] = xh
        rstd_ref[...] = rstd

    tok = lambda w: pl.BlockSpec((t, w), lambda i: (i, 0))
    return _call_with_side(
        body, side_ops, name="mixer_fwd", grid=(nt,),
        in_specs=[tok(D_MODEL), _layer_spec((8, D_MODEL), l), _layer_spec((IN_W, D_MODEL), 0, resident=True),
                  _layer_spec((CONV_HALO, CONV_W), l), _layer_spec((8, CONV_W), l),
                  _layer_spec((3, 2 * CHUNK, CHUNK), l), _layer_spec((CHUNK, GMLP_W), l),
                  _const_spec((XATTN_W, XATTN_HEADS * N_MEM)), _const_spec((XATTN_HEADS * N_MEM, XATTN_W)),
                  _layer_spec((D_MODEL, D_MODEL), 0, resident=True)],
        out_specs=[tok(D_MODEL), tok(1), tok(IN_W), tok(CONV_W), tok(D_MODEL), tok(GMLP_W), tok(XATTN_HEADS * N_MEM)],
        out_shape=[jax.ShapeDtypeStruct((s, D_MODEL), F32), jax.ShapeDtypeStruct((s, 1), F32),
                   jax.ShapeDtypeStruct((s, IN_W), F32), jax.ShapeDtypeStruct((s, CONV_W), F32),
                   jax.ShapeDtypeStruct((s, D_MODEL), BF16), jax.ShapeDtypeStruct((s, GMLP_W), F32),
                   jax.ShapeDtypeStruct((s, XATTN_HEADS * N_MEM), F32)],
        scratch_shapes=[pltpu.VMEM((t + CONV_HALO, CONV_W), F32), pltpu.VMEM((t + 8, CONV_W), F32)],
        operands=(xin, pd, win_t, conv_w, pa, wst, bst, kt_all, v_all, w_out), semantics=("arbitrary",))


VD_LN_G, VD_LN_B, VD_LOSS = 0, 1, 2
VA_CONV_B, VA_LNA_G, VA_LNA_B, VA_LNV_G, VA_LNV_B = 0, 1, 2, 3, 4


def _mixer_bwd_d(gz, xh1, rstd1, h, ac, mixed, probs, pd, conv_w, pa, wstt, k_all, vt_all, w_out, win_t, l, t, side_ops=None):
    s = gz.shape[0]
    nt = s // t

    def body(gz_ref, xh_ref, rstd_ref, h_ref, ac_ref, mixed_ref, p_ref, pd_ref, cw_ref, pa_ref, wstt_ref,
             k_ref, vt_ref, wout_ref, wint_ref,
             dx_ref, dh_ref, dmix_ref, vd_ref, va_ref, dcw_ref, dws_ref, dbs_ref, dkt_ref, dv_ref,
             ebuf, dvnbuf, dbsacc, erbuf):
        i = pl.program_id(0)

        @pl.when(i == 0)
        def _():
            vd_ref[...] = jnp.zeros_like(vd_ref)
            va_ref[...] = jnp.zeros_like(va_ref)
            dcw_ref[...] = jnp.zeros_like(dcw_ref)
            dws_ref[...] = jnp.zeros_like(dws_ref)
            dbs_ref[...] = jnp.zeros_like(dbs_ref)
            dkt_ref[...] = jnp.zeros_like(dkt_ref)
            dv_ref[...] = jnp.zeros_like(dv_ref)
            dbsacc[...] = jnp.zeros_like(dbsacc)
            ebuf[t:t + CONV_HALO, :] = jnp.zeros((CONV_HALO, CONV_W), F32)

        gz_v = gz_ref[...]
        xh = xh_ref[...]
        vd_ref[VD_LN_G:VD_LN_G + 1, :] += _colsum(gz_v * xh)
        vd_ref[VD_LN_B:VD_LN_B + 1, :] += _colsum(gz_v)
        dz = _ln_bwd(gz_v * _row(pd_ref, PD_G1), xh, rstd_ref[...])
        dzb = dz.astype(BF16)
        dmix_ref[...] = dzb
        dcat = _dot_nt(dzb, wout_ref[...])
        d_a, d_g, d_o = dcat[:, 0:CONV_W], dcat[:, CONV_W:CONV_W + GMLP_W], dcat[:, CONV_W + GMLP_W:]

        h = h_ref[...]
        a1, a2 = h[:, 0:CONV_W], h[:, CONV_W:2 * CONV_W]
        hu, hv = h[:, 2 * CONV_W:2 * CONV_W + GMLP_W], h[:, 2 * CONV_W + GMLP_W:2 * CONV_W + 2 * GMLP_W]
        q = h[:, IN_W - XATTN_W:]

        xh_a, rstd_a = _ln_fwd(ac_ref[...])
        an = xh_a * _row(pa_ref, PA_LNA_G) + _row(pa_ref, PA_LNA_B)
        sig = _sigmoid(an)
        d_an = d_a * (sig * (1.0 + an * (1.0 - sig)))
        va_ref[VA_LNA_G:VA_LNA_G + 1, :] += _colsum(d_an * xh_a)
        va_ref[VA_LNA_B:VA_LNA_B + 1, :] += _colsum(d_an)
        dac = _ln_bwd(d_an * _row(pa_ref, PA_LNA_G), xh_a, rstd_a)
        va_ref[VA_CONV_B:VA_CONV_B + 1, :] += _colsum(dac)
        ebuf[0:t, :] = dac
        sg = _sigmoid(a2)
        glu = a1 * sg
        dglu = jnp.zeros((t, CONV_W), F32)
        for r in range(8):
            if r > 0:
                erbuf[...] = ebuf[r:r + t + 24, :]
            src = ebuf if r == 0 else erbuf
            for a in range(4):
                o = 8 * a + r
                if o < CONV_K:
                    k = CONV_K - 1 - o
                    ek = src[8 * a:8 * a + t, :]
                    dglu = dglu + ek * cw_ref[k:k + 1, :]
                    dcw_ref[k:k + 1, :] += _colsum(ek * glu)
        ebuf[t:t + CONV_HALO, :] = ebuf[0:CONV_HALO, :]
        da1 = dglu * sg
        da2 = dglu * a1 * sg * (1.0 - sg)

        u, du = _gelu_and_grad(hu)
        vg, dvg_dhv = _gelu_and_grad(hv)
        xh_v, rstd_v = _ln_fwd(vg)
        vn = xh_v * _row(pa_ref, PA_LNV_G) + _row(pa_ref, PA_LNV_B)
        vnb = vn.astype(BF16)
        dhu = d_g * mixed_ref[...] * du
        dm = d_g * u
        dmb = dm.astype(BF16)
        lo = _lane_lo((CHUNK, LANE))
        for n in range(t // CHUNK):
            rows = slice(n * CHUNK, (n + 1) * CHUNK)
            dbsacc[...] += dm[rows, :]
            for j in range(GMLP_W // LANE):
                cols = slice(j * LANE, (j + 1) * LANE)
                dm_blk = dmb[rows, cols]
                r = _dot(wstt_ref[j], dm_blk)
                dvnbuf[rows, cols] = jnp.where(lo, r[:CHUNK], r[CHUNK:])
                zero = jnp.zeros_like(dm_blk)
                st = jnp.concatenate([jnp.where(lo, dm_blk, zero), jnp.where(lo, zero, dm_blk)], axis=0)
                dws_ref[j] += _dot_nt(st, vnb[rows, cols])
        dvn = dvnbuf[...]
        va_ref[VA_LNV_G:VA_LNV_G + 1, :] += _colsum(dvn * xh_v)
        va_ref[VA_LNV_B:VA_LNV_B + 1, :] += _colsum(dvn)
        dhv = _ln_bwd(dvn * _row(pa_ref, PA_LNV_G), xh_v, rstd_v) * dvg_dhv

        qb = q.astype(BF16)
        p = p_ref[...]
        dob = d_o.astype(BF16)
        dp = _dot(dob, vt_ref[...])
        dss = []
        for hd in range(XATTN_HEADS):
            cs = slice(hd * N_MEM, (hd + 1) * N_MEM)
            ph, dph = p[:, cs], dp[:, cs]
            dss.append(ph * (dph - jnp.sum(ph * dph, axis=-1, keepdims=True)) * ATT_SCALE)
        dsb = jnp.concatenate(dss, axis=1).astype(BF16)
        dq = _dot(dsb, k_ref[...])
        dkt_ref[...] += _dot_tn(qb, dsb)
        dv_ref[...] += _dot_tn(p.astype(BF16), dob)

        dhb = jnp.concatenate([da1, da2, dhu, dhv, dq], axis=1).astype(BF16)
        dh_ref[...] = dhb
        dx_ref[...] = ALPHA * dz + _dot(dhb, wint_ref[...])

        @pl.when(i == nt - 1)
        def _():
            acc = dbsacc[...]
            head = lax.broadcasted_iota(jnp.int32, (CHUNK, GMLP_W), 1) // HEAD_DIM
            lane = lax.broadcasted_iota(jnp.int32, (CHUNK, LANE), 1)
            out = jnp.zeros((CHUNK, LANE), F32)
            for hd in range(GMLP_W // HEAD_DIM):
                sh = jnp.sum(jnp.where(head == hd, acc, 0.0), axis=1, keepdims=True)
                out = out + jnp.where(lane == hd, sh, 0.0)
            dbs_ref[...] = out

    rev = lambda w: pl.BlockSpec((t, w), lambda i: (nt - 1 - i, 0))
    out_shape = [
        jax.ShapeDtypeStruct((s, D_MODEL), F32), jax.ShapeDtypeStruct((s, IN_W), BF16),
        jax.ShapeDtypeStruct((s, D_MODEL), BF16),
        jax.ShapeDtypeStruct((8, D_MODEL), F32), jax.ShapeDtypeStruct((8, CONV_W), F32),
        jax.ShapeDtypeStruct((CONV_HALO, CONV_W), F32), jax.ShapeDtypeStruct((3, 2 * CHUNK, CHUNK), F32),
        jax.ShapeDtypeStruct((CHUNK, LANE), F32),
        jax.ShapeDtypeStruct((XATTN_W, XATTN_HEADS * N_MEM), F32), jax.ShapeDtypeStruct((XATTN_HEADS * N_MEM, XATTN_W), F32),
    ]
    out_specs = [rev(D_MODEL), rev(IN_W), rev(D_MODEL)] + [_const_spec(o.shape) for o in out_shape[3:]]
    return _call_with_side(
        body, side_ops, name="mixer_bwd_d", grid=(nt,),
        in_specs=[rev(D_MODEL), rev(D_MODEL), rev(1), rev(IN_W), rev(CONV_W), rev(GMLP_W), rev(XATTN_HEADS * N_MEM),
                  _layer_spec((8, D_MODEL), l), _layer_spec((CONV_HALO, CONV_W), l), _layer_spec((8, CONV_W), l),
                  _layer_spec((3, 2 * CHUNK, CHUNK), l),
                  _const_spec((XATTN_HEADS * N_MEM, XATTN_W)), _const_spec((XATTN_W, XATTN_HEADS * N_MEM)),
                  _layer_spec((D_MODEL, D_MODEL), 0, resident=True), _layer_spec((IN_W, D_MODEL), 0, resident=True)],
        out_specs=out_specs, out_shape=out_shape,
        scratch_shapes=[pltpu.VMEM((t + CONV_HALO, CONV_W), F32), pltpu.VMEM((t, GMLP_W), F32),
                        pltpu.VMEM((CHUNK, GMLP_W), F32), pltpu.VMEM((t + 24, CONV_W), F32)],
        operands=(gz, xh1, rstd1, h, ac, mixed, probs, pd, conv_w, pa, wstt, k_all, vt_all, w_out, win_t),
        semantics=("arbitrary",))


def _mixer_bwd_w(xin, pd, dh, cat, dmix, l, t, side_ops=None):
    s = xin.shape[0]
    nt = s // t

    def body(x_ref, pd_ref, dh_ref, cat_ref, dmix_ref, dwin_ref, dwout_ref):
        @pl.when(pl.program_id(0) == 0)
        def _():
            dwin_ref[...] = jnp.zeros_like(dwin_ref)
            dwout_ref[...] = jnp.zeros_like(dwout_ref)

        xb = (x_ref[...] * _row(pd_ref, PD_GIN) + _row(pd_ref, PD_BIN)).astype(BF16)
        dwin_ref[...] += _dot_tn(dh_ref[...], xb)
        dwout_ref[...] += _dot_tn(cat_ref[...], dmix_ref[...])

    tok = lambda w: pl.BlockSpec((t, w), lambda i: (i, 0))
    return _call_with_side(
        body, side_ops, name="mixer_bwd_w", grid=(nt,),
        in_specs=[tok(D_MODEL), _layer_spec((8, D_MODEL), l), tok(IN_W), tok(D_MODEL), tok(D_MODEL)],
        out_specs=[_layer_spec((IN_W, D_MODEL), 0, resident=True), _layer_spec((D_MODEL, D_MODEL), 0, resident=True)],
        out_shape=[jax.ShapeDtypeStruct((1, IN_W, D_MODEL), F32), jax.ShapeDtypeStruct((1, D_MODEL, D_MODEL), F32)],
        scratch_shapes=[], operands=(xin, pd, dh, cat, dmix), semantics=("arbitrary",))


PF_W0, PF_B = 0, 3


def _ffn_fwd(xh1, pd, wup_t, pf, w_d, l, t, side_ops=None):
    s = xh1.shape[0]
    nt = s // t

    def body(xh_ref, pd_ref, wg_ref, wv_ref, pf_ref, wd_ref, xh2_ref, rstd_ref, upg_ref, upv_ref, fbuf):
        i = pl.program_id(0)

        @pl.when(i == 0)
        def _():
            fbuf[0:FFN_HALO, :] = jnp.zeros((FFN_HALO, FF_P), F32)

        x1 = xh_ref[...] * _row(pd_ref, PD_G1) + _row(pd_ref, PD_B1)
        xb = x1.astype(BF16)
        y = jnp.zeros((t, D_MODEL), F32)
        for hf in range(2):
            cs = slice(hf * FF_H, (hf + 1) * FF_H)
            ug = _dot_nt(xb, wg_ref[cs, :])
            uv = _dot_nt(xb, wv_ref[cs, :])
            upg_ref[:, cs] = ug
            upv_ref[:, cs] = uv
            fbuf[FFN_HALO:FFN_HALO + t, cs] = ug
            gate = jnp.zeros((t, FF_H), F32) + pf_ref[PF_B:PF_B + 1, cs]
            for k in range(FFN_CONV_K):
                off = FFN_HALO - (FFN_CONV_K - 1) + k
                gate = gate + fbuf[off:off + t, cs] * pf_ref[PF_W0 + k:PF_W0 + k + 1, cs]
            fbuf[0:FFN_HALO, cs] = fbuf[t:t + FFN_HALO, cs]
            hm = gate * _sigmoid(gate) * uv
            y = y + _dot(hm.astype(BF16), wd_ref[cs, :])
        xh2, rstd = _ln_fwd(ALPHA * x1 + y)
        xh2_ref[...] = xh2
        rstd_ref[...] = rstd

    tok = lambda w: pl.BlockSpec((t, w), lambda i: (i, 0))
    return _call_with_side(
        body, side_ops, name="ffn_fwd", grid=(nt,),
        in_specs=[tok(D_MODEL), _layer_spec((8, D_MODEL), l),
                  _layer_spec((FF_P, D_MODEL), 0, 0, resident=True), _layer_spec((FF_P, D_MODEL), 0, 1, resident=True),
                  _layer_spec((8, FF_P), l), _layer_spec((FF_P, D_MODEL), 0, resident=True)],
        out_specs=[tok(D_MODEL), tok(1), tok(FF_P), tok(FF_P)],
        out_shape=[jax.ShapeDtypeStruct((s, D_MODEL), F32), jax.ShapeDtypeStruct((s, 1), F32),
                   jax.ShapeDtypeStruct((s, FF_P), F32), jax.ShapeDtypeStruct((s, FF_P), F32)],
        scratch_shapes=[pltpu.VMEM((t + FFN_HALO, FF_P), F32)],
        operands=(xh1, pd, wup_t, wup_t, pf, w_d), semantics=("arbitrary",))


VF_W0, VF_B = 0, 3


def _ffn_bwd_d(gz_or_target, xh2, rstd2, upg, upv, pd, pf, w_d, wup_t, l, t, last, side_ops=None):
    s = xh2.shape[0]
    nt = s // t
    hb = t // FFN_HALO

    def body(gz_ref, xh2_ref, rstd_ref, upg_ref, halo_ref, upv_ref, pd_ref, pf_ref, wd_ref, wg_ref, wv_ref,
             dx_ref, dy_ref, dug_ref, duv_ref, hm_ref, vd_ref, vf_ref, gbuf, ebuf, s1buf, s2buf):
        i = pl.program_id(0)
        first_tile = i == nt - 1

        @pl.when(i == 0)
        def _():
            vd_ref[...] = jnp.zeros_like(vd_ref)
            vf_ref[...] = jnp.zeros_like(vf_ref)
            ebuf[t:t + FFN_HALO, :] = jnp.zeros((FFN_HALO, FF_P), F32)

        xh2_v = xh2_ref[...]
        if last:
            diff = xh2_v * _row(pd_ref, PD_G2) + _row(pd_ref, PD_B2) - gz_ref[...]
            vd_ref[VD_LOSS:VD_LOSS + 1, :] += _colsum(diff * diff)
            gz_v = diff * (1.0 / D_MODEL)
        else:
            gz_v = gz_ref[...]
        vd_ref[VD_LN_G:VD_LN_G + 1, :] += _colsum(gz_v * xh2_v)
        vd_ref[VD_LN_B:VD_LN_B + 1, :] += _colsum(gz_v)
        dz = _ln_bwd(gz_v * _row(pd_ref, PD_G2), xh2_v, rstd_ref[...])
        dyb = dz.astype(BF16)
        dy_ref[...] = dyb
        dx = ALPHA * dz
        for hf in range(2):
            cs = slice(hf * FF_H, (hf + 1) * FF_H)
            ug = upg_ref[:, cs]
            uv = upv_ref[:, cs]
            halo = halo_ref[:, cs]
            gbuf[0:FFN_HALO, :] = jnp.where(first_tile, jnp.zeros_like(halo), halo)
            gbuf[FFN_HALO:FFN_HALO + t, :] = ug
            s1buf[...] = gbuf[FFN_HALO - 1:FFN_HALO - 1 + t, :]
            s2buf[...] = gbuf[FFN_HALO - 2:FFN_HALO - 2 + t, :]
            ug1 = s1buf[...]
            ug2 = s2buf[...]
            gate = (pf_ref[PF_B:PF_B + 1, cs] + ug2 * pf_ref[PF_W0:PF_W0 + 1, cs] + ug1 * pf_ref[PF_W0 + 1:PF_W0 + 2, cs]
                    + ug * pf_ref[PF_W0 + 2:PF_W0 + 3, cs])
            sig = _sigmoid(gate)
            sl = gate * sig
            hm_ref[:, cs] = sl * uv
            dhm = _dot_nt(dyb, wd_ref[cs, :])
            duv = dhm * sl
            dgate = dhm * uv * (sig * (1.0 + gate * (1.0 - sig)))
            vf_ref[VF_B:VF_B + 1, cs] += _colsum(dgate)
            vf_ref[VF_W0:VF_W0 + 1, cs] += _colsum(dgate * ug2)
            vf_ref[VF_W0 + 1:VF_W0 + 2, cs] += _colsum(dgate * ug1)
            vf_ref[VF_W0 + 2:VF_W0 + 3, cs] += _colsum(dgate * ug)
            ebuf[0:t, cs] = dgate
            dug = (ebuf[2:2 + t, cs] * pf_ref[PF_W0:PF_W0 + 1, cs] + ebuf[1:1 + t, cs] * pf_ref[PF_W0 + 1:PF_W0 + 2, cs]
                   + dgate * pf_ref[PF_W0 + 2:PF_W0 + 3, cs])
            ebuf[t:t + FFN_HALO, cs] = ebuf[0:FFN_HALO, cs]
            dugb = dug.astype(BF16)
            duvb = duv.astype(BF16)
            dug_ref[:, cs] = dugb
            duv_ref[:, cs] = duvb
            dx = dx + _dot(dugb, wg_ref[cs, :]) + _dot(duvb, wv_ref[cs, :])
        dx_ref[...] = dx

        if last:
            @pl.when(i == nt - 1)
            def _():
                tot = jnp.sum(vd_ref[VD_LOSS:VD_LOSS + 1, :], axis=1, keepdims=True)
                vd_ref[VD_LOSS:VD_LOSS + 1, :] = jnp.zeros((1, D_MODEL), F32) + tot

    rev = lambda w: pl.BlockSpec((t, w), lambda i: (nt - 1 - i, 0))
    halo_spec = pl.BlockSpec((FFN_HALO, FF_P), lambda i: (jnp.maximum((nt - 1 - i) * hb - 1, 0), 0))
    out_shape = [jax.ShapeDtypeStruct((s, D_MODEL), F32), jax.ShapeDtypeStruct((s, D_MODEL), BF16),
                 jax.ShapeDtypeStruct((s, FF_P), BF16), jax.ShapeDtypeStruct((s, FF_P), BF16),
                 jax.ShapeDtypeStruct((s, FF_P), F32),
                 jax.ShapeDtypeStruct((8, D_MODEL), F32), jax.ShapeDtypeStruct((8, FF_P), F32)]
    return _call_with_side(
        body, side_ops, name="ffn_bwd_d_last" if last else "ffn_bwd_d", grid=(nt,),
        in_specs=[rev(D_MODEL), rev(D_MODEL), rev(1), rev(FF_P), halo_spec, rev(FF_P),
                  _layer_spec((8, D_MODEL), l), _layer_spec((8, FF_P), l),
                  _layer_spec((FF_P, D_MODEL), 0, resident=True),
                  _layer_spec((FF_P, D_MODEL), 0, 0, resident=True), _layer_spec((FF_P, D_MODEL), 0, 1, resident=True)],
        out_specs=[rev(D_MODEL), rev(D_MODEL), rev(FF_P), rev(FF_P), rev(FF_P),
                   _const_spec((8, D_MODEL)), _const_spec((8, FF_P))],
        out_shape=out_shape,
        scratch_shapes=[pltpu.VMEM((t + FFN_HALO, FF_H), F32), pltpu.VMEM((t + FFN_HALO, FF_P), F32),
                        pltpu.VMEM((t, FF_H), F32), pltpu.VMEM((t, FF_H), F32)],
        operands=(gz_or_target, xh2, rstd2, upg, upg, upv, pd, pf, w_d, wup_t, wup_t), semantics=("arbitrary",))


def _ffn_bwd_w(xh1, pd, dy, dug, duv, hm, l, t, side_ops=None):
    s = xh1.shape[0]
    nt = s // t

    def body(xh_ref, pd_ref, dy_ref, dug_ref, duv_ref, hm_ref, dwup_ref, dwd_ref):
        @pl.when(pl.program_id(1) == 0)
        def _():
            dwup_ref[...] = jnp.zeros_like(dwup_ref)
            dwd_ref[...] = jnp.zeros_like(dwd_ref)

        xb = (xh_ref[...] * _row(pd_ref, PD_G1) + _row(pd_ref, PD_B1)).astype(BF16)
        dwup_ref[0] += _dot_tn(dug_ref[...], xb)
        dwup_ref[1] += _dot_tn(duv_ref[...], xb)
        dwd_ref[...] += _dot_tn(hm_ref[...].astype(BF16), dy_ref[...])

    tok = lambda w: pl.BlockSpec((t, w), lambda c, i: (i, 0))
    half = pl.BlockSpec((t, FF_H), lambda c, i: (i, c))
    return _call_with_side(
        body, side_ops, name="ffn_bwd_w", grid=(2, nt),
        in_specs=[tok(D_MODEL), pl.BlockSpec((None, 8, D_MODEL), lambda c, i: (l, 0, 0)), tok(D_MODEL), half, half, half],
        out_specs=[pl.BlockSpec((None, 2, FF_H, D_MODEL), lambda c, i: (0, 0, c, 0), pipeline_mode=pl.Buffered(1)),
                   pl.BlockSpec((None, FF_H, D_MODEL), lambda c, i: (0, c, 0), pipeline_mode=pl.Buffered(1))],
        out_shape=[jax.ShapeDtypeStruct((1, 2, FF_P, D_MODEL), F32), jax.ShapeDtypeStruct((1, FF_P, D_MODEL), F32)],
        scratch_shapes=[], operands=(xh1, pd, dy, dug, duv, hm), semantics=("arbitrary", "arbitrary"))


def _adamw_math(w, g, m, v):
    nm = ADAM_B1 * m + (1.0 - ADAM_B1) * g
    nv = ADAM_B2 * v + (1.0 - ADAM_B2) * (g * g)
    m_hat = nm / (1.0 - ADAM_B1 ** ADAM_STEP)
    v_hat = nv / (1.0 - ADAM_B2 ** ADAM_STEP)
    return -ADAM_LR * (m_hat / (jnp.sqrt(v_hat) + ADAM_EPS) + ADAM_WD * w), nm, nv


def _adamw_layers(w, gs, m, v, name):
    shp = w.shape
    _, rows, cols = shp
    tr = _row_tile(rows, cols * 4, mult=8)
    nb = rows // tr

    def body(w_ref, g0_ref, g1_ref, m_ref, v_ref, g_ref, d_ref, nm_ref, nv_ref):
        g = jnp.where(pl.program_id(0) == 0, g0_ref[...], g1_ref[...])
        g_ref[...] = g
        d_ref[...], nm_ref[...], nv_ref[...] = _adamw_math(w_ref[...], g, m_ref[...], v_ref[...])

    stacked = pl.BlockSpec((tr, cols), lambda l, i: (l * nb + i, 0))
    single = pl.BlockSpec((tr, cols), lambda l, i: (i, 0))
    sh = jax.ShapeDtypeStruct((DEPTH * rows, cols), F32)
    flat = lambda a: a.reshape(DEPTH * rows, cols)
    outs = pl.pallas_call(body, name=name, grid=(DEPTH, nb), in_specs=[stacked, single, single, stacked, stacked],
                          out_specs=[stacked] * 4, out_shape=[sh] * 4,
                          compiler_params=_cparams(("arbitrary", "arbitrary")))(flat(w), gs[0], gs[1], flat(m), flat(v))
    return [o.reshape(shp) for o in outs]


def _adamw_small(ws, gs, ms, vs):
    n = len(ws)

    def body(*refs):
        w_refs, g_refs, m_refs, v_refs = refs[:n], refs[n:2 * n], refs[2 * n:3 * n], refs[3 * n:4 * n]
        d_refs, nm_refs, nv_refs = refs[4 * n:5 * n], refs[5 * n:6 * n], refs[6 * n:7 * n]
        for k in range(n):
            d_refs[k][...], nm_refs[k][...], nv_refs[k][...] = _adamw_math(w_refs[k][...], g_refs[k][...], m_refs[k][...],
                                                                             v_refs[k][...])

    shapes = [jax.ShapeDtypeStruct(w.shape, F32) for w in ws]
    outs = pl.pallas_call(body, name="adamw_small", out_shape=shapes * 3, compiler_params=_cparams())(*ws, *gs, *ms, *vs)
    return outs[:n], outs[n:2 * n], outs[2 * n:]


def _all_gather_chips(tensors, name):
    n = len(tensors)
    halves = [a.shape[1] // 2 for a in tensors]

    def body(*refs):
        x_refs, out_refs = refs[:n], refs[n:2 * n]
        send_sems, recv_sems, local_sems = refs[2 * n:]
        x, y, c, chips = _my_place()
        me, sibling, mej = (x, y, c), (x, y, 1 - c), 2 * x + y

        def rows(tn, px, py, pc):
            return out_refs[tn].at[:, 2 * px + py, pl.ds(pc * halves[tn], halves[tn]), :]

        def copy(tn, k, block, to, src=None):
            return pltpu.make_async_remote_copy(
                src_ref=rows(tn, *block) if src is None else src, dst_ref=rows(tn, *block),
                send_sem=send_sems.at[tn, k], recv_sem=recv_sems.at[tn, k], device_id=to, device_id_type=MESH)

        mine_src = [x_refs[tn].at[:, pl.ds(c * halves[tn], halves[tn]), :] for tn in range(n)]
        mine = [pltpu.make_async_copy(mine_src[tn], rows(tn, *me), local_sems.at[tn]) for tn in range(n)]
        first = []
        for j, chip in enumerate(chips):
            first += [copy(tn, 1 + j, me, (*chip, c), src=mine_src[tn]) for tn in range(n)]
        first += [copy(tn, 0, me, sibling, src=mine_src[tn]) for tn in range(n)]
        for cp in first + mine:
            cp.start()
        passed = []
        for j, chip in enumerate(chips):
            for tn in range(n):
                copy(tn, 1 + j, (*chip, c), me).wait_recv()
                fwd = copy(tn, 4 + j, (*chip, c), sibling)
                fwd.start()
                passed.append(fwd)
        for tn in range(n):
            copy(tn, 0, sibling, me).wait_recv()
            for j, chip in enumerate(chips):
                copy(tn, 4 + j, (*chip, 1 - c), me).wait_recv()
        for cp in first + passed:
            cp.wait_send()
        for cp in mine:
            cp.wait()

    return pl.pallas_call(
        body, name=name,
        out_shape=[jax.ShapeDtypeStruct((a.shape[0], N_CHIPS) + a.shape[1:], a.dtype) for a in tensors],
        in_specs=[ANY] * n, out_specs=[ANY] * n,
        scratch_shapes=[pltpu.SemaphoreType.DMA((n, 7)), pltpu.SemaphoreType.DMA((n, 7)), pltpu.SemaphoreType.DMA((n,))],
    )(*tensors)


def _swap_op(g5s):
    n = len(g5s)

    def make(ins, outs, scr):
        send_sems, recv_sems = scr
        x, y, c, _ = _my_place()

        def copies():
            return [pltpu.make_async_remote_copy(
                src_ref=ins[tn].at[:, :, 1 - c], dst_ref=outs[tn], send_sem=send_sems.at[tn], recv_sem=recv_sems.at[tn],
                device_id=(x, y, 1 - c), device_id_type=MESH) for tn in range(n)]

        def start():
            for cp in copies():
                cp.start()

        def wait():
            for cp in copies():
                cp.wait()

        return start, wait

    return _SideOp(g5s, [jax.ShapeDtypeStruct(g.shape[:2] + g.shape[3:], g.dtype) for g in g5s],
                   [pltpu.SemaphoreType.DMA((n,)), pltpu.SemaphoreType.DMA((n,))], make)


def _scatter_op(parts):
    n = len(parts)

    def make(ins, outs, scr):
        send_sems, recv_sems = scr
        x, y, c, chips = _my_place()

        def copies():
            return [pltpu.make_async_remote_copy(
                src_ref=ins[tn].at[:, 2 * px + py], dst_ref=outs[tn].at[:, k],
                send_sem=send_sems.at[tn, k], recv_sem=recv_sems.at[tn, k],
                device_id=(px, py, c), device_id_type=MESH) for k, (px, py) in enumerate(chips) for tn in range(n)]

        def start():
            for cp in copies():
                cp.start()

        def wait():
            for cp in copies():
                cp.wait()

        return start, wait

    return _SideOp(parts, [jax.ShapeDtypeStruct((p.shape[0], 3) + p.shape[2:], p.dtype) for p in parts],
                   [pltpu.SemaphoreType.DMA((n, 3)), pltpu.SemaphoreType.DMA((n, 3))], make)


def _sgather_op(fs):
    n = len(fs)

    def make(ins, outs, scr):
        send_sems, recv_sems = scr
        x, y, c, _ = _my_place()

        def copy(tn, dst_half):
            return pltpu.make_async_remote_copy(
                src_ref=outs[tn].at[:, c], dst_ref=outs[tn].at[:, dst_half], send_sem=send_sems.at[tn],
                recv_sem=recv_sems.at[tn], device_id=(x, y, 1 - c), device_id_type=MESH)

        def start():
            for tn in range(n):
                copy(tn, c).start()

        def wait():
            for tn in range(n):
                copy(tn, 1 - c).wait_recv()
                copy(tn, c).wait_send()

        return start, wait

    return _SideOp(fs, [jax.ShapeDtypeStruct(f.shape, f.dtype) for f in fs],
                   [pltpu.SemaphoreType.DMA((n,)), pltpu.SemaphoreType.DMA((n,))], make, aliases={tn: tn for tn in range(n)})


def _reduce_group_in_vmem(g5s, side_ops):
    n = len(g5s)
    dims = [g.shape[3:] for g in g5s]

    def body(*refs):
        g_refs, out_refs = refs[:n], refs[n:2 * n]
        scr = refs[2 * n:]
        own, recv, part, slots, outv = (scr[k * n:(k + 1) * n] for k in range(5))
        s1, r1, s2, r2, s3, r3, lsem = scr[5 * n:]
        x, y, c, chips = _my_place()
        mej, sibling = 2 * x + y, (x, y, 1 - c)

        loads = [pltpu.make_async_copy(g_refs[t].at[0, :, c], own[t], lsem.at[t]) for t in range(n)]
        swaps = [pltpu.make_async_remote_copy(src_ref=g_refs[t].at[0, :, 1 - c], dst_ref=recv[t], send_sem=s1.at[t],
                                              recv_sem=r1.at[t], device_id=sibling, device_id_type=MESH) for t in range(n)]
        for cp in loads + swaps:
            cp.start()
        for cp in loads + swaps:
            cp.wait()
        for t in range(n):
            part[t][...] = (own[t][...] + recv[t][...]).astype(BF16)

        sends = [pltpu.make_async_remote_copy(src_ref=part[t].at[2 * px + py], dst_ref=slots[t].at[k], send_sem=s2.at[t, k],
                                              recv_sem=r2.at[t, k], device_id=(px, py, c), device_id_type=MESH)
                 for k, (px, py) in enumerate(chips) for t in range(n)]
        for cp in sends:
            cp.start()
        for cp in sends:
            cp.wait()
        for t in range(n):
            acc = part[t][pl.ds(mej, 1)].astype(F32)
            for k in range(3):
                acc = acc + slots[t][k:k + 1].astype(F32)
            outv[t][pl.ds(c, 1)] = acc

        def back(t, half):
            return pltpu.make_async_remote_copy(src_ref=outv[t].at[c], dst_ref=outv[t].at[half], send_sem=s3.at[t],
                                                recv_sem=r3.at[t], device_id=sibling, device_id_type=MESH)

        for t in range(n):
            back(t, c).start()
        for t in range(n):
            back(t, 1 - c).wait_recv()
            back(t, c).wait_send()
        for t in range(n):
            pltpu.sync_copy(outv[t], out_refs[t].at[0])

    vm = lambda shape, dt: pltpu.VMEM(shape, dt)
    scratch = ([vm((N_CHIPS,) + d, F32) for d in dims] + [vm((N_CHIPS,) + d, F32) for d in dims]
               + [vm((N_CHIPS,) + d, BF16) for d in dims] + [vm((3,) + d, BF16) for d in dims] + [vm((2,) + d, F32) for d in dims]
               + [pltpu.SemaphoreType.DMA((n,)), pltpu.SemaphoreType.DMA((n,)), pltpu.SemaphoreType.DMA((n, 3)),
                  pltpu.SemaphoreType.DMA((n, 3)), pltpu.SemaphoreType.DMA((n,)), pltpu.SemaphoreType.DMA((n,)),
                  pltpu.SemaphoreType.DMA((n,))])
    return _call_with_side(body, side_ops, name="rs_tail_fused", grid=(1,), in_specs=[ANY] * n, out_specs=[ANY] * n,
                           out_shape=[jax.ShapeDtypeStruct((1, 2) + d, F32) for d in dims], scratch_shapes=scratch,
                           operands=tuple(g5s), semantics=("arbitrary",))


def _gather_devices_op(xs):
    def make(ins, outs, scr):
        send_sems, recv_sems, local_sem = scr
        x, y, c, chips = _my_place()
        peers = [(x, y, 1 - c)] + [(px, py, pc) for (px, py) in chips for pc in (c, 1 - c)]
        me = 4 * x + 2 * y + c

        def copy(k, slot):
            return pltpu.make_async_remote_copy(
                src_ref=ins[0], dst_ref=outs[0].at[slot], send_sem=send_sems.at[k], recv_sem=recv_sems.at[k],
                device_id=peers[k], device_id_type=MESH)

        def local():
            return pltpu.make_async_copy(ins[0], outs[0].at[me], local_sem)

        def start():
            for k in range(7):
                copy(k, me).start()
            local().start()

        def wait():
            for k, (px, py, pc) in enumerate(peers):
                copy(k, 4 * px + 2 * py + pc).wait_recv()
                copy(k, me).wait_send()
            local().wait()

        return start, wait

    return _SideOp([xs], [jax.ShapeDtypeStruct((8,) + xs.shape, xs.dtype)],
                   [pltpu.SemaphoreType.DMA((7,)), pltpu.SemaphoreType.DMA((7,)), pltpu.SemaphoreType.DMA], make)


def _add_halves(gs, recvs, place):
    n = len(gs)

    def body(place_ref, *refs):
        g_refs, r_refs, o_refs = refs[:n], refs[n:2 * n], refs[2 * n:]
        for tn in range(n):
            o_refs[tn][...] = (g_refs[tn][...] + r_refs[tn][...]).astype(BF16)

    def gspec(g):
        return pl.BlockSpec((None, None, None) + g.shape[3:], lambda l, j, p: (l, j, p[1], 0, 0))

    def rspec(r):
        return pl.BlockSpec((None, None) + r.shape[2:], lambda l, j, p: (l, j, 0, 0))

    grid_spec = pltpu.PrefetchScalarGridSpec(
        num_scalar_prefetch=1, grid=(gs[0].shape[0], N_CHIPS),
        in_specs=[gspec(g) for g in gs] + [rspec(r) for r in recvs], out_specs=[rspec(r) for r in recvs])
    return pl.pallas_call(body, name="rs_add", grid_spec=grid_spec,
                          out_shape=[jax.ShapeDtypeStruct(r.shape, BF16) for r in recvs],
                          compiler_params=_cparams(("arbitrary", "arbitrary")))(place, *gs, *recvs)


def _sum_slots(parts, slots, place):
    n = len(parts)

    def body(place_ref, *refs):
        p_refs, s_refs, o_refs = refs[:n], refs[n:2 * n], refs[2 * n:]
        for tn in range(n):
            acc = p_refs[tn][...].astype(F32)
            for k in range(3):
                acc = acc + s_refs[tn][k].astype(F32)
            o_refs[tn][...] = acc

    def pspec(p):
        return pl.BlockSpec((None, None) + p.shape[2:], lambda l, pl_: (l, pl_[0], 0, 0))

    def sspec(sl):
        return pl.BlockSpec((None,) + sl.shape[1:], lambda l, pl_: (l, 0, 0, 0))

    def ospec(p):
        return pl.BlockSpec((None, None) + p.shape[2:], lambda l, pl_: (l, pl_[1], 0, 0))

    grid_spec = pltpu.PrefetchScalarGridSpec(
        num_scalar_prefetch=1, grid=(parts[0].shape[0],),
        in_specs=[pspec(p) for p in parts] + [sspec(sl) for sl in slots], out_specs=[ospec(p) for p in parts])
    return pl.pallas_call(body, name="rs_sum", grid_spec=grid_spec,
                          out_shape=[jax.ShapeDtypeStruct((p.shape[0], 2) + p.shape[2:], F32) for p in parts],
                          compiler_params=_cparams(("arbitrary",)))(place, *parts, *slots)


def _sum_devices(gathered):
    m_per = gathered[0].shape[1]

    def body(*refs):
        o_ref = refs[-1]
        for l, g_ref in enumerate(refs[:-1]):
            acc = g_ref[0]
            for d in range(1, 8):
                acc = acc + g_ref[d]
            o_ref[l * m_per:(l + 1) * m_per, :] = acc

    return pl.pallas_call(body, name="small_sum", out_shape=jax.ShapeDtypeStruct((len(gathered) * m_per, LANE), F32),
                          compiler_params=_cparams())(*gathered)


def _pad_ff_cols(a):
    lead = a.shape[:-1]
    n = a.shape[-1] // FF_Q
    a = a.reshape(*lead, n, FF_Q)
    a = jnp.pad(a, [(0, 0)] * len(lead) + [(0, 0), (0, FF_QP - FF_Q)])
    return a.reshape(*lead, n * FF_QP)


def _unpad_ff_cols(a):
    lead = a.shape[:-1]
    n = a.shape[-1] // FF_QP
    return a.reshape(*lead, n, FF_QP)[..., :FF_Q].reshape(*lead, n * FF_Q)


def _pack_small(parts):
    flat = jnp.concatenate([p.reshape(-1) for p in parts])
    return flat.reshape(-1, LANE)


SMALL_SHAPES = [("conv_a_w", (CONV_K, CONV_W)), ("conv_a_b", (CONV_W,)), ("ln_a_g", (CONV_W,)), ("ln_a_b", (CONV_W,)),
                ("ln_v_g", (GMLP_W,)), ("ln_v_b", (GMLP_W,)), ("w_s", (6, CHUNK, CHUNK)), ("b_s", (6, CHUNK)),
                ("ln1_g", (D_MODEL,)), ("ln1_b", (D_MODEL,)), ("conv_f_w", (FFN_CONV_K, D_FF)), ("conv_f_b", (D_FF,)),
                ("ln2_g", (D_MODEL,)), ("ln2_b", (D_MODEL,))]
SMALL_FLOATS = 128000


def _unpack_small(flat2d):
    flat = flat2d.reshape(DEPTH, -1)
    out, o = {}, 0
    for name, shp in SMALL_SHAPES:
        n = 1
        for d in shp:
            n *= d
        out[name] = flat[:, o:o + n].reshape((DEPTH,) + shp)
        o += n
    return out


def _rows8(rows):
    blk = jnp.stack(rows, axis=1)
    return jnp.pad(blk, ((0, 0), (0, 8 - len(rows)), (0, 0)))


def kernel(x, mem, w_in, conv_a_w, conv_a_b, ln_a_g, ln_a_b, ln_v_g, ln_v_b, w_s, b_s, w_mk, w_mv, w_out, ln1_g, ln1_b, w_up, conv_f_w, conv_f_b, w_down, ln2_g, ln2_b, loss_target, m_w_in, m_conv_a_w, m_conv_a_b, m_ln_a_g, m_ln_a_b, m_ln_v_g, m_ln_v_b, m_w_s, m_b_s, m_w_mk, m_w_mv, m_w_out, m_ln1_g, m_ln1_b, m_w_up, m_conv_f_w, m_conv_f_b, m_w_down, m_ln2_g, m_ln2_b, v_w_in, v_conv_a_w, v_conv_a_b, v_ln_a_g, v_ln_a_b, v_ln_v_g, v_ln_v_b, v_w_s, v_b_s, v_w_mk, v_w_mv, v_w_out, v_ln1_g, v_ln1_b, v_w_up, v_conv_f_w, v_conv_f_b, v_w_down, v_ln2_g, v_ln2_b):
    seq = x.shape[1]
    t_fwd = min(512, seq)
    t_bwd = min(256, seq)
    t_wg = min(1024, seq)
    chip = 2 * lax.axis_index("x") + lax.axis_index("y")
    core = lax.axis_index("c")
    place = jnp.stack([chip, core]).astype(jnp.int32)
    x0 = x[0]
    mem0 = mem[0]
    target = loss_target[0]

    sh_in = w_in.transpose(0, 2, 1).astype(BF16)
    sh_mk, sh_mv, sh_out = w_mk.astype(BF16), w_mv.astype(BF16), w_out.astype(BF16)
    sh_up = _pad_ff_cols(w_up).transpose(0, 2, 1).astype(BF16)
    sh_dn = jnp.pad(w_down, ((0, 0), (0, FF_QP - FF_Q), (0, 0))).astype(BF16)

    def mixer_weights(g_in, g_mk, g_mv, g_out):
        return dict(win_t=g_in.reshape(1, IN_W, D_MODEL), wmk=g_mk.reshape(1, D_MODEL, XATTN_W),
                    wmv=g_mv.reshape(1, D_MODEL, XATTN_W), wout=g_out.reshape(1, D_MODEL, D_MODEL))

    def ffn_weights(g_up, g_dn):
        return dict(wup_t=g_up.reshape(1, 2, FF_P, D_MODEL), wdown=g_dn.reshape(1, FF_P, D_MODEL))

    n_ca = conv_a_w.size
    small_w = _pack_small([conv_a_w, conv_f_w, jnp.zeros((2 * 80 * LANE - n_ca - conv_f_w.size,), F32)])[None]
    *g_mixer0, small_g = _all_gather_chips([sh_in[:1], sh_mk[:1], sh_mv[:1], sh_out[:1], small_w], "ag_mixer0")
    wts = [mixer_weights(*g_mixer0), None]
    small_g = small_g.reshape(N_CHIPS, -1)
    conv_a_full = small_g[:, :n_ca].reshape(N_CHIPS, DEPTH, CONV_K, CONV_W // 4).transpose(1, 2, 0, 3).reshape(DEPTH, CONV_K, CONV_W)
    conv_f_full = small_g[:, n_ca:n_ca + conv_f_w.size].reshape(N_CHIPS, DEPTH, FFN_CONV_K, FF_Q).transpose(1, 2, 0, 3).reshape(DEPTH, FFN_CONV_K, D_FF)

    tril = jnp.tril(jnp.ones((CHUNK, CHUNK), dtype=bool))
    ws_m = jnp.where(tril, w_s, 0.0)
    wst = ws_m.reshape(DEPTH, 3, 2 * CHUNK, CHUNK).astype(BF16)
    wstt = ws_m.transpose(0, 1, 3, 2).reshape(DEPTH, 3, 2 * CHUNK, CHUNK).astype(BF16)
    bst = jnp.repeat(b_s.transpose(0, 2, 1), HEAD_DIM, axis=2)
    conv_w = jnp.pad(conv_a_full, ((0, 0), (0, CONV_HALO - CONV_K), (0, 0)))
    pa = _rows8([conv_a_b, ln_a_g, ln_a_b, ln_v_g, ln_v_b])
    gin = jnp.concatenate([jnp.ones((1, D_MODEL), F32), ln2_g[:DEPTH - 1]], axis=0)
    bin_ = jnp.concatenate([jnp.zeros((1, D_MODEL), F32), ln2_b[:DEPTH - 1]], axis=0)
    pd = _rows8([gin, bin_, ln1_g, ln1_b, ln2_g, ln2_b])
    pf = jnp.concatenate([_pad_ff_cols(conv_f_full), _pad_ff_cols(conv_f_b)[:, None, :],
                          jnp.zeros((DEPTH, 8 - FFN_CONV_K - 1, FF_P), F32)], axis=1)

    acts = []
    xin = x0
    for l in range(DEPTH):
        w = wts[l]
        kt_all, k_all, v_all, vt_all = _kv_fwd(mem0, w["wmk"], w["wmv"])
        ops = [_gather_chips_op([sh_up[0], sh_dn[0]])] if l == 0 else None
        (xh1, rstd1, h, ac, cat, mixed, probs), side = _mixer_fwd(xin, pd, w["win_t"], conv_w, pa, wst, bst, kt_all, v_all, w["wout"],
                                                    l, t_fwd, ops)
        if l == 0:
            w.update(ffn_weights(*side[0]))
        ops = [_gather_chips_op([sh_in[1], sh_mk[1], sh_mv[1], sh_out[1], sh_up[1], sh_dn[1]])] if l == 0 else None
        (xh2, rstd2, upg, upv), side = _ffn_fwd(xh1, pd, w["wup_t"], pf, w["wdown"], l, t_fwd, ops)
        if l == 0:
            wts[1] = {**mixer_weights(*side[0][:4]), **ffn_weights(*side[0][4:])}
        acts.append(dict(xin=xin, k_all=k_all, vt_all=vt_all, xh1=xh1, rstd1=rstd1, h=h, ac=ac, cat=cat, mixed=mixed, probs=probs,
                         xh2=xh2, rstd2=rstd2, upg=upg, upv=upv))
        xin = xh2

    assert DEPTH == 2

    def halves_view(gs):
        return [g.reshape(1, N_CHIPS, 2, g.shape[1] // (2 * N_CHIPS), g.shape[2]) for g in gs]

    small = [None] * DEPTH
    small_packed = [None] * DEPTH
    small_gathered = [None] * DEPTH
    red_layers = [None] * DEPTH
    gz = target
    g5_prev = None
    for l in reversed(range(DEPTH)):
        a, w = acts[l], wts[l]
        last = l == DEPTH - 1
        (dx1, dy, dug, duv, hm, vd2, vf), side = _ffn_bwd_d(
            gz, a["xh2"], a["rstd2"], a["upg"], a["upv"], pd, pf, w["wdown"], w["wup_t"], l, t_bwd, last,
            [_swap_op(g5_prev), _gather_devices_op(small_packed[l + 1])] if g5_prev else None)
        if g5_prev:
            small_gathered[l + 1] = side[1][0]
        parts_prev = _add_halves(g5_prev, side[0], place) if g5_prev else None
        (gw_up_t, gw_down), side = _ffn_bwd_w(a["xh1"], pd, dy, dug, duv, hm, l, t_wg,
                                              [_scatter_op(parts_prev)] if g5_prev else None)
        halves_prev = _sum_slots(parts_prev, side[0], place) if g5_prev else None
        g5_ffn = halves_view([gw_up_t.reshape(1, 2 * FF_P, D_MODEL), gw_down])
        ops = ([_sgather_op(halves_prev)] if g5_prev else []) + ([_swap_op(g5_ffn)] if l == 0 else [])
        (dx0, dh, dmix, vd1, va, dcw, dws, dbs, dkt, dv), side = _mixer_bwd_d(
            dx1, a["xh1"], a["rstd1"], a["h"], a["ac"], a["mixed"], a["probs"], pd, conv_w, pa, wstt,
            a["k_all"], a["vt_all"], w["wout"], w["win_t"], l, t_fwd, ops)
        if g5_prev:
            red_layers[l + 1] = side[0]
        parts_ffn = _add_halves(g5_ffn, side[-1], place) if l == 0 else None
        dws6 = jnp.where(tril, dws.reshape(6, CHUNK, CHUNK), 0.0)
        small[l] = [dcw[:CONV_K], va[VA_CONV_B], va[VA_LNA_G], va[VA_LNA_B], va[VA_LNV_G], va[VA_LNV_B], dws6,
                    dbs[:, :6].T, vd1[VD_LN_G], vd1[VD_LN_B],
                    _unpad_ff_cols(vf[VF_W0:VF_W0 + FFN_CONV_K]), _unpad_ff_cols(vf[VF_B]),
                    vd2[VD_LN_G], vd2[VD_LN_B]]
        small[l].append(vd2[VD_LOSS] if last else jnp.zeros((D_MODEL,), F32))
        small_packed[l] = _pack_small(small[l])
        ops = [_scatter_op(parts_ffn), _gather_devices_op(small_packed[l])] if l == 0 else None
        (gw_in_t, gw_out), side = _mixer_bwd_w(a["xin"], pd, dh, a["cat"], dmix, l, t_wg, ops)
        gw_mk, gw_mv = _kv_bwd(mem0, dkt, dv)
        g5_mix = halves_view([gw_in_t, gw_mk, gw_mv, gw_out])
        if l == 0:
            small_gathered[l] = side[1][0]
            halves_ffn = _sum_slots(parts_ffn, side[0], place)
            red_mix, side = _reduce_group_in_vmem(g5_mix, [_sgather_op(halves_ffn)])
            red_layers[0] = red_mix + side[0]
        else:
            g5_prev = g5_mix + g5_ffn
        gz = dx0
    grad_x = gz[None]

    def shard_grads(red):
        r = [f.reshape(-1, f.shape[-1]) for f in red]
        return dict(w_in=r[0].T, w_mk=r[1], w_mv=r[2], w_out=r[3], w_up=_unpad_ff_cols(r[4].T), w_down=r[5])

    big_grads = [shard_grads(red_layers[l]) for l in range(DEPTH)]

    small_red = _sum_devices(small_gathered)
    sg = _unpack_small(small_red)
    g_conv_a_w = lax.dynamic_slice_in_dim(sg["conv_a_w"], chip * (CONV_W // 4), CONV_W // 4, axis=2)
    g_conv_f_w = lax.dynamic_slice_in_dim(sg["conv_f_w"], chip * FF_Q, FF_Q, axis=2)

    loss = 0.5 / D_MODEL * small_red.reshape(DEPTH, -1)[DEPTH - 1, SMALL_FLOATS]

    grads = dict(conv_a_w=g_conv_a_w, conv_a_b=sg["conv_a_b"], ln_a_g=sg["ln_a_g"], ln_a_b=sg["ln_a_b"],
                 ln_v_g=sg["ln_v_g"], ln_v_b=sg["ln_v_b"], w_s=sg["w_s"], b_s=sg["b_s"], ln1_g=sg["ln1_g"], ln1_b=sg["ln1_b"],
                 conv_f_w=g_conv_f_w, conv_f_b=sg["conv_f_b"], ln2_g=sg["ln2_g"], ln2_b=sg["ln2_b"])
    weights = dict(w_in=w_in, conv_a_w=conv_a_w, conv_a_b=conv_a_b, ln_a_g=ln_a_g, ln_a_b=ln_a_b, ln_v_g=ln_v_g,
                   ln_v_b=ln_v_b, w_s=w_s, b_s=b_s, w_mk=w_mk, w_mv=w_mv, w_out=w_out, ln1_g=ln1_g, ln1_b=ln1_b,
                   w_up=w_up, conv_f_w=conv_f_w, conv_f_b=conv_f_b, w_down=w_down, ln2_g=ln2_g, ln2_b=ln2_b)
    mom_m = dict(w_in=m_w_in, conv_a_w=m_conv_a_w, conv_a_b=m_conv_a_b, ln_a_g=m_ln_a_g, ln_a_b=m_ln_a_b, ln_v_g=m_ln_v_g,
                 ln_v_b=m_ln_v_b, w_s=m_w_s, b_s=m_b_s, w_mk=m_w_mk, w_mv=m_w_mv, w_out=m_w_out, ln1_g=m_ln1_g,
                 ln1_b=m_ln1_b, w_up=m_w_up, conv_f_w=m_conv_f_w, conv_f_b=m_conv_f_b, w_down=m_w_down, ln2_g=m_ln2_g,
                 ln2_b=m_ln2_b)
    mom_v = dict(w_in=v_w_in, conv_a_w=v_conv_a_w, conv_a_b=v_conv_a_b, ln_a_g=v_ln_a_g, ln_a_b=v_ln_a_b, ln_v_g=v_ln_v_g,
                 ln_v_b=v_ln_v_b, w_s=v_w_s, b_s=v_b_s, w_mk=v_w_mk, w_mv=v_w_mv, w_out=v_w_out, ln1_g=v_ln1_g,
                 ln1_b=v_ln1_b, w_up=v_w_up, conv_f_w=v_conv_f_w, conv_f_b=v_conv_f_b, w_down=v_w_down, ln2_g=v_ln2_g,
                 ln2_b=v_ln2_b)
    names = list(weights)
    big_names = ["w_in", "w_mk", "w_mv", "w_out", "w_up", "w_down"]
    delta, new_m, new_v = {}, {}, {}
    for n in big_names:
        grads[n], delta[n], new_m[n], new_v[n] = _adamw_layers(weights[n], [big_grads[l][n] for l in range(DEPTH)],
                                                               mom_m[n], mom_v[n], "adamw_" + n)
    small_names = [n for n in names if n not in big_names]
    ds, nms, nvs = _adamw_small([weights[n] for n in small_names], [grads[n] for n in small_names],
                                [mom_m[n] for n in small_names], [mom_v[n] for n in small_names])
    for n, d, nm, nv in zip(small_names, ds, nms, nvs):
        delta[n], new_m[n], new_v[n] = d, nm, nv

    return (loss, grad_x, *[grads[n] for n in names], *[delta[n] for n in names],
            *[new_m[n] for n in names], *[new_v[n] for n in names])
```

```python
import jax
import jax.numpy as jnp
from jax import lax
from jax.experimental import pallas as pl
from jax.experimental.pallas import tpu as pltpu

F32 = jnp.float32
BF16 = jnp.bfloat16

D_MODEL = 1024
DEPTH = 2
CONV_W = 384
GMLP_W = 384
XATTN_W = 256
XATTN_HEADS = 4
HEAD_DIM = 64
IN_W = 1792
CONV_K = 31
CHUNK = 128
N_MEM = 256
D_FF = 2752
FFN_CONV_K = 3
ALPHA = (2.0 * DEPTH) ** 0.25
LN_EPS = 1e-5
ATT_SCALE = 1.0 / 8.0
ADAM_LR, ADAM_B1, ADAM_B2, ADAM_EPS, ADAM_WD, ADAM_STEP = 0.001, 0.9, 0.999, 1e-08, 0.01, 10

N_CHIPS = 4
FF_Q = D_FF // N_CHIPS
FF_QP = 704
FF_H = 2 * FF_QP
FF_P = 4 * FF_QP
LANE = 128
CONV_HALO = 32
FFN_HALO = 8
BF16_ROWS = 16
VMEM_LIMIT = 60 * 1024 * 1024

MESH = pl.DeviceIdType.MESH
ANY = pl.BlockSpec(memory_space=pl.ANY)


def _cparams(sem=None, vmem=VMEM_LIMIT):
    kw = {"vmem_limit_bytes": vmem}
    if sem is not None:
        kw["dimension_semantics"] = sem
    return pltpu.CompilerParams(**kw)


def _row_tile(rows, row_bytes, limit=2 << 20, mult=BF16_ROWS):
    if rows * row_bytes <= limit:
        return rows
    best = None
    for cand in range(mult, rows, mult):
        if rows % cand == 0 and cand * row_bytes <= limit:
            best = cand
    assert best is not None, (rows, row_bytes)
    return best


def _const_spec(shape):
    nd = len(shape)
    return pl.BlockSpec(shape, lambda *_: (0,) * nd)


def _layer_spec(shape, *lead, resident=False):
    nd = len(shape)
    kw = {"pipeline_mode": pl.Buffered(1)} if resident else {}
    return pl.BlockSpec((None,) * len(lead) + tuple(shape), lambda *_: tuple(lead) + (0,) * nd, **kw)


def _sigmoid(x):
    return jax.nn.sigmoid(x)


def _gelu(x):
    return jax.nn.gelu(x)


def _gelu_and_grad(x):
    c = 0.7978845608028654
    a = 0.044715
    x2 = x * x
    t = jnp.tanh(c * (x + a * x * x2))
    h = 0.5 * (1.0 + t)
    return x * h, h + 0.5 * x * (1.0 - t * t) * c * (1.0 + 3.0 * a * x2)


def _ln_fwd(z):
    mu = jnp.mean(z, axis=-1, keepdims=True)
    zc = z - mu
    var = jnp.mean(zc * zc, axis=-1, keepdims=True)
    rstd = lax.rsqrt(var + LN_EPS)
    return zc * rstd, rstd


def _ln_bwd(dxh, xh, rstd):
    m1 = jnp.mean(dxh, axis=-1, keepdims=True)
    m2 = jnp.mean(dxh * xh, axis=-1, keepdims=True)
    return rstd * (dxh - m1 - xh * m2)


def _colsum(a):
    return jnp.sum(a, axis=0, keepdims=True)


def _dot(a, b):
    return jnp.dot(a, b, preferred_element_type=F32)


def _dot_tn(a, b):
    return lax.dot_general(a, b, (((0,), (0,)), ((), ())), preferred_element_type=F32)


def _dot_nt(a, b):
    return lax.dot_general(a, b, (((1,), (1,)), ((), ())), preferred_element_type=F32)


def _softmax_heads(sc):
    ps = []
    for hd in range(XATTN_HEADS):
        s = sc[:, hd * N_MEM:(hd + 1) * N_MEM]
        e = jnp.exp(s - jnp.max(s, axis=-1, keepdims=True))
        ps.append(e / jnp.sum(e, axis=-1, keepdims=True))
    return jnp.concatenate(ps, axis=1)


def _lane_lo(shape):
    return (lax.broadcasted_iota(jnp.int32, shape, len(shape) - 1) % LANE) < HEAD_DIM


def _spatial_mix(vnb, wst_ref, bst_ref, mix_ref, t):
    lo = _lane_lo((CHUNK, LANE))
    for n in range(t // CHUNK):
        rows = slice(n * CHUNK, (n + 1) * CHUNK)
        for j in range(GMLP_W // LANE):
            cols = slice(j * LANE, (j + 1) * LANE)
            r = _dot(wst_ref[j], vnb[rows, cols])
            mix_ref[rows, cols] = jnp.where(lo, r[:CHUNK], r[CHUNK:]) + bst_ref[:, cols]


def _kv_fwd(mem, w_mk, w_mv):
    def body(mem_ref, wk_ref, wv_ref, kt_ref, k_ref, v_ref, vt_ref):
        mb = mem_ref[...].astype(BF16)
        k = _dot(mb, wk_ref[...])
        v = _dot(mb, wv_ref[...])
        col = lax.broadcasted_iota(jnp.int32, (N_MEM, XATTN_W), 1) // HEAD_DIM
        ks = [jnp.where(col == hd, k, 0.0) for hd in range(XATTN_HEADS)]
        vs = [jnp.where(col == hd, v, 0.0) for hd in range(XATTN_HEADS)]
        k_ref[...] = jnp.concatenate(ks, axis=0).astype(BF16)
        v_ref[...] = jnp.concatenate(vs, axis=0).astype(BF16)
        kt_ref[...] = jnp.concatenate([x.T for x in ks], axis=1).astype(BF16)
        vt_ref[...] = jnp.concatenate([x.T for x in vs], axis=1).astype(BF16)

    wide = jax.ShapeDtypeStruct((XATTN_W, XATTN_HEADS * N_MEM), BF16)
    tall = jax.ShapeDtypeStruct((XATTN_HEADS * N_MEM, XATTN_W), BF16)
    wspec = _layer_spec((D_MODEL, XATTN_W), 0)
    return pl.pallas_call(body, name="kv_fwd", grid=(1,),
                          in_specs=[_const_spec((N_MEM, D_MODEL)), wspec, wspec],
                          out_specs=[_const_spec(wide.shape), _const_spec(tall.shape), _const_spec(tall.shape),
                                     _const_spec(wide.shape)],
                          out_shape=(wide, tall, tall, wide), compiler_params=_cparams(("arbitrary",)))(mem, w_mk, w_mv)


def _kv_bwd(mem, dkt_all, dv_all):
    def body(mem_ref, dkt_ref, dv_ref, gk_ref, gv_ref):
        col = lax.broadcasted_iota(jnp.int32, (N_MEM, XATTN_W), 1) // HEAD_DIM
        dk = jnp.zeros((N_MEM, XATTN_W), F32)
        dv = jnp.zeros((N_MEM, XATTN_W), F32)
        for hd in range(XATTN_HEADS):
            dk = dk + jnp.where(col == hd, dkt_ref[:, hd * N_MEM:(hd + 1) * N_MEM].T, 0.0)
            dv = dv + jnp.where(col == hd, dv_ref[hd * N_MEM:(hd + 1) * N_MEM, :], 0.0)
        mb = mem_ref[...].astype(BF16)
        gk_ref[0] = _dot_tn(mb, dk.astype(BF16))
        gv_ref[0] = _dot_tn(mb, dv.astype(BF16))

    out = jax.ShapeDtypeStruct((1, D_MODEL, XATTN_W), F32)
    return pl.pallas_call(body, name="kv_bwd", out_shape=(out, out), compiler_params=_cparams())(mem, dkt_all, dv_all)


def _my_place():
    x, y, c = lax.axis_index("x"), lax.axis_index("y"), lax.axis_index("c")
    chips = [(1 - x, y), (x, 1 - y), (1 - x, 1 - y)]
    return x, y, c, chips


class _SideOp:
    def __init__(self, ins, out_shapes, scratch, make, aliases=None):
        self.ins, self.out_shapes, self.scratch, self.make, self.aliases = list(ins), list(out_shapes), list(scratch), make, dict(aliases or {})


def _call_with_side(body, side_ops, *, name, grid, in_specs, out_specs, out_shape, scratch_shapes, operands, semantics):
    side_ops = list(side_ops or ())
    n_in, n_out, n_scr = len(in_specs), len(out_specs), len(scratch_shapes)
    s_ins = [a for op in side_ops for a in op.ins]
    s_outs = [o for op in side_ops for o in op.out_shapes]
    s_scr = [x for op in side_ops for x in op.scratch]
    aliases, oi, oo = {}, 0, 0
    for op in side_ops:
        for a, b in op.aliases.items():
            aliases[n_in + oi + a] = n_out + oo + b
        oi, oo = oi + len(op.ins), oo + len(op.out_shapes)

    def wrapped(*refs):
        ins, sins = refs[:n_in], refs[n_in:n_in + len(s_ins)]
        base = n_in + len(s_ins)
        outs, souts = refs[base:base + n_out], refs[base + n_out:base + n_out + len(s_outs)]
        base += n_out + len(s_outs)
        scr, sscr = refs[base:base + n_scr], refs[base + n_scr:]
        if side_ops:
            first = pl.program_id(0) == 0
            last = pl.program_id(0) == grid[0] - 1
            for d in range(1, len(grid)):
                first = jnp.logical_and(first, pl.program_id(d) == 0)
                last = jnp.logical_and(last, pl.program_id(d) == grid[d] - 1)
            hooks, a, b, c = [], 0, 0, 0
            for op in side_ops:
                hooks.append(op.make(sins[a:a + len(op.ins)], souts[b:b + len(op.out_shapes)], sscr[c:c + len(op.scratch)]))
                a, b, c = a + len(op.ins), b + len(op.out_shapes), c + len(op.scratch)

            @pl.when(first)
            def _():
                for start, _w in hooks:
                    start()

        body(*ins, *outs, *scr)
        if side_ops:
            @pl.when(last)
            def _():
                for _s, wait in hooks:
                    wait()

    res = pl.pallas_call(
        wrapped, name=name, grid=grid, in_specs=list(in_specs) + [ANY] * len(s_ins),
        out_specs=list(out_specs) + [ANY] * len(s_outs), out_shape=list(out_shape) + s_outs,
        scratch_shapes=list(scratch_shapes) + s_scr, input_output_aliases=aliases,
        compiler_params=_cparams(semantics),
    )(*operands, *s_ins)
    side_res, k = [], n_out
    for op in side_ops:
        side_res.append(list(res[k:k + len(op.out_shapes)]))
        k += len(op.out_shapes)
    return list(res[:n_out]), side_res


def _gather_chips_op(shards):
    n = len(shards)

    def make(ins, outs, scr):
        send_sems, recv_sems, local_sems = scr
        x, y, c, chips = _my_place()
        mej = 2 * x + y

        def remote(tn, k, slot):
            px, py = chips[k]
            return pltpu.make_async_remote_copy(
                src_ref=ins[tn], dst_ref=outs[tn].at[slot], send_sem=send_sems.at[tn, k],
                recv_sem=recv_sems.at[tn, k], device_id=(px, py, c), device_id_type=MESH)

        def local(tn):
            return pltpu.make_async_copy(ins[tn], outs[tn].at[mej], local_sems.at[tn])

        def start():
            for k in range(3):
                for tn in range(n):
                    remote(tn, k, mej).start()
            for tn in range(n):
                local(tn).start()

        def wait():
            for k, (px, py) in enumerate(chips):
                for tn in range(n):
                    remote(tn, k, 2 * px + py).wait_recv()
                    remote(tn, k, mej).wait_send()
            for tn in range(n):
                local(tn).wait()

        return start, wait

    return _SideOp(shards, [jax.ShapeDtypeStruct((N_CHIPS,) + a.shape, a.dtype) for a in shards],
                   [pltpu.SemaphoreType.DMA((n, 3)), pltpu.SemaphoreType.DMA((n, 3)), pltpu.SemaphoreType.DMA((n,))], make)


PA_CONV_B, PA_LNA_G, PA_LNA_B, PA_LNV_G, PA_LNV_B = 0, 1, 2, 3, 4
PD_GIN, PD_BIN, PD_G1, PD_B1, PD_G2, PD_B2 = 0, 1, 2, 3, 4, 5


def _row(ref, r):
    return ref[r:r + 1, :]


def _mixer_fwd(xin, pd, win_t, conv_w, pa, wst, bst, kt_all, v_all, w_out, l, t, side_ops=None):
    s = xin.shape[0]
    nt = s // t

    def body(x_ref, pd_ref, wint_ref, cw_ref, pa_ref, wst_ref, bst_ref, kt_ref, v_ref, wout_ref,
             xh_ref, rstd_ref, h_ref, ac_ref, cat_ref, mixed_ref, p_ref, cbuf, zbuf):
        i = pl.program_id(0)
        x = x_ref[...] * _row(pd_ref, PD_GIN) + _row(pd_ref, PD_BIN)
        h = _dot_nt(x.astype(BF16), wint_ref[...])
        h_ref[...] = h
        a1, a2 = h[:, 0:CONV_W], h[:, CONV_W:2 * CONV_W]
        hu, hv = h[:, 2 * CONV_W:2 * CONV_W + GMLP_W], h[:, 2 * CONV_W + GMLP_W:2 * CONV_W + 2 * GMLP_W]
        q = h[:, IN_W - XATTN_W:]

        @pl.when(i == 0)
        def _():
            cbuf[0:CONV_HALO, :] = jnp.zeros((CONV_HALO, CONV_W), F32)

        cbuf[CONV_HALO:CONV_HALO + t, :] = a1 * _sigmoid(a2)
        ac = jnp.zeros((t, CONV_W), F32) + _row(pa_ref, PA_CONV_B)
        for r in range(8):
            zr = jnp.zeros((t + 8, CONV_W), F32)
            for a in range(4):
                o = 8 * a + r
                if o < CONV_K:
                    k = CONV_K - 1 - o
                    zr = zr + cbuf[CONV_HALO - 8 - 8 * a:CONV_HALO - 8 - 8 * a + t + 8, :] * cw_ref[k:k + 1, :]
            if r == 0:
                ac = ac + zr[8:, :]
            else:
                zbuf[...] = zr
                ac = ac + zbuf[8 - r:8 - r + t, :]
        ac_ref[...] = ac
        cbuf[0:CONV_HALO, :] = cbuf[t:t + CONV_HALO, :]
        xh_a, _ = _ln_fwd(ac)
        an = xh_a * _row(pa_ref, PA_LNA_G) + _row(pa_ref, PA_LNA_B)
        a = an * _sigmoid(an)

        u = _gelu(hu)
        xh_v, _ = _ln_fwd(_gelu(hv))
        vn = xh_v * _row(pa_ref, PA_LNV_G) + _row(pa_ref, PA_LNV_B)
        _spatial_mix(vn.astype(BF16), wst_ref, bst_ref, mixed_ref, t)
        g = u * mixed_ref[...]

        p = _softmax_heads(_dot(q.astype(BF16), kt_ref[...]) * ATT_SCALE)
        p_ref[...] = p
        o = _dot(p.astype(BF16), v_ref[...])

        cat = jnp.concatenate([a, g, o], axis=1).astype(BF16)
        cat_ref[...] = cat
        z = ALPHA * x + _dot(cat, wout_ref[...])
        xh, rstd = _ln_fwd(z)
        xh_ref[...] = xh
        rstd_ref[...] = rstd

    tok = lambda w: pl.BlockSpec((t, w), lambda i: (i, 0))
    return _call_with_side(
        body, side_ops, name="mixer_fwd", grid=(nt,),
        in_specs=[tok(D_MODEL), _layer_spec((8, D_MODEL), l), _layer_spec((IN_W, D_MODEL), 0, resident=True),
                  _layer_spec((CONV_HALO, CONV_W), l), _layer_spec((8, CONV_W), l),
                  _layer_spec((3, 2 * CHUNK, CHUNK), l), _layer_spec((CHUNK, GMLP_W), l),
                  _const_spec((XATTN_W, XATTN_HEADS * N_MEM)), _const_spec((XATTN_HEADS * N_MEM, XATTN_W)),
                  _layer_spec((D_MODEL, D_MODEL), 0, resident=True)],
        out_specs=[tok(D_MODEL), tok(1), tok(IN_W), tok(CONV_W), tok(D_MODEL), tok(GMLP_W), tok(XATTN_HEADS * N_MEM)],
        out_shape=[jax.ShapeDtypeStruct((s, D_MODEL), F32), jax.ShapeDtypeStruct((s, 1), F32),
                   jax.ShapeDtypeStruct((s, IN_W), F32), jax.ShapeDtypeStruct((s, CONV_W), F32),
                   jax.ShapeDtypeStruct((s, D_MODEL), BF16), jax.ShapeDtypeStruct((s, GMLP_W), F32),
                   jax.ShapeDtypeStruct((s, XATTN_HEADS * N_MEM), F32)],
        scratch_shapes=[pltpu.VMEM((t + CONV_HALO, CONV_W), F32), pltpu.VMEM((t + 8, CONV_W), F32)],
        operands=(xin, pd, win_t, conv_w, pa, wst, bst, kt_all, v_all, w_out), semantics=("arbitrary",))


VD_LN_G, VD_LN_B, VD_LOSS = 0, 1, 2
VA_CONV_B, VA_LNA_G, VA_LNA_B, VA_LNV_G, VA_LNV_B = 0, 1, 2, 3, 4


def _mixer_bwd_d(gz, xh1, rstd1, h, ac, mixed, probs, pd, conv_w, pa, wstt, k_all, vt_all, w_out, win_t, l, t, side_ops=None):
    s = gz.shape[0]
    nt = s // t

    def body(gz_ref, xh_ref, rstd_ref, h_ref, ac_ref, mixed_ref, p_ref, pd_ref, cw_ref, pa_ref, wstt_ref,
             k_ref, vt_ref, wout_ref, wint_ref,
             dx_ref, dh_ref, dmix_ref, vd_ref, va_ref, dcw_ref, dws_ref, dbs_ref, dkt_ref, dv_ref,
             ebuf, dvnbuf, dbsacc, erbuf):
        i = pl.program_id(0)

        @pl.when(i == 0)
        def _():
            vd_ref[...] = jnp.zeros_like(vd_ref)
            va_ref[...] = jnp.zeros_like(va_ref)
            dcw_ref[...] = jnp.zeros_like(dcw_ref)
            dws_ref[...] = jnp.zeros_like(dws_ref)
            dbs_ref[...] = jnp.zeros_like(dbs_ref)
            dkt_ref[...] = jnp.zeros_like(dkt_ref)
            dv_ref[...] = jnp.zeros_like(dv_ref)
            dbsacc[...] = jnp.zeros_like(dbsacc)
            ebuf[t:t + CONV_HALO, :] = jnp.zeros((CONV_HALO, CONV_W), F32)

        gz_v = gz_ref[...]
        xh = xh_ref[...]
        vd_ref[VD_LN_G:VD_LN_G + 1, :] += _colsum(gz_v * xh)
        vd_ref[VD_LN_B:VD_LN_B + 1, :] += _colsum(gz_v)
        dz = _ln_bwd(gz_v * _row(pd_ref, PD_G1), xh, rstd_ref[...])
        dzb = dz.astype(BF16)
        dmix_ref[...] = dzb
        dcat = _dot_nt(dzb, wout_ref[...])
        d_a, d_g, d_o = dcat[:, 0:CONV_W], dcat[:, CONV_W:CONV_W + GMLP_W], dcat[:, CONV_W + GMLP_W:]

        h = h_ref[...]
        a1, a2 = h[:, 0:CONV_W], h[:, CONV_W:2 * CONV_W]
        hu, hv = h[:, 2 * CONV_W:2 * CONV_W + GMLP_W], h[:, 2 * CONV_W + GMLP_W:2 * CONV_W + 2 * GMLP_W]
        q = h[:, IN_W - XATTN_W:]

        xh_a, rstd_a = _ln_fwd(ac_ref[...])
        an = xh_a * _row(pa_ref, PA_LNA_G) + _row(pa_ref, PA_LNA_B)
        sig = _sigmoid(an)
        d_an = d_a * (sig * (1.0 + an * (1.0 - sig)))
        va_ref[VA_LNA_G:VA_LNA_G + 1, :] += _colsum(d_an * xh_a)
        va_ref[VA_LNA_B:VA_LNA_B + 1, :] += _colsum(d_an)
        dac = _ln_bwd(d_an * _row(pa_ref, PA_LNA_G), xh_a, rstd_a)
        va_ref[VA_CONV_B:VA_CONV_B + 1, :] += _colsum(dac)
        ebuf[0:t, :] = dac
        sg = _sigmoid(a2)
        glu = a1 * sg
        dglu = jnp.zeros((t, CONV_W), F32)
        for r in range(8):
            if r > 0:
                erbuf[...] = ebuf[r:r + t + 24, :]
            src = ebuf if r == 0 else erbuf
            for a in range(4):
                o = 8 * a + r
                if o < CONV_K:
                    k = CONV_K - 1 - o
                    ek = src[8 * a:8 * a + t, :]
                    dglu = dglu + ek * cw_ref[k:k + 1, :]
                    dcw_ref[k:k + 1, :] += _colsum(ek * glu)
        ebuf[t:t + CONV_HALO, :] = ebuf[0:CONV_HALO, :]
        da1 = dglu * sg
        da2 = dglu * a1 * sg * (1.0 - sg)

        u, du = _gelu_and_grad(hu)
        vg, dvg_dhv = _gelu_and_grad(hv)
        xh_v, rstd_v = _ln_fwd(vg)
        vn = xh_v * _row(pa_ref, PA_LNV_G) + _row(pa_ref, PA_LNV_B)
        vnb = vn.astype(BF16)
        dhu = d_g * mixed_ref[...] * du
        dm = d_g * u
        dmb = dm.astype(BF16)
        lo = _lane_lo((CHUNK, LANE))
        for n in range(t // CHUNK):
            rows = slice(n * CHUNK, (n + 1) * CHUNK)
            dbsacc[...] += dm[rows, :]
            for j in range(GMLP_W // LANE):
                cols = slice(j * LANE, (j + 1) * LANE)
                dm_blk = dmb[rows, cols]
                r = _dot(wstt_ref[j], dm_blk)
                dvnbuf[rows, cols] = jnp.where(lo, r[:CHUNK], r[CHUNK:])
                zero = jnp.zeros_like(dm_blk)
                st = jnp.concatenate([jnp.where(lo, dm_blk, zero), jnp.where(lo, zero, dm_blk)], axis=0)
                dws_ref[j] += _dot_nt(st, vnb[rows, cols])
        dvn = dvnbuf[...]
        va_ref[VA_LNV_G:VA_LNV_G + 1, :] += _colsum(dvn * xh_v)
        va_ref[VA_LNV_B:VA_LNV_B + 1, :] += _colsum(dvn)
        dhv = _ln_bwd(dvn * _row(pa_ref, PA_LNV_G), xh_v, rstd_v) * dvg_dhv

        qb = q.astype(BF16)
        p = p_ref[...]
        dob = d_o.astype(BF16)
        dp = _dot(dob, vt_ref[...])
        dss = []
        for hd in range(XATTN_HEADS):
            cs = slice(hd * N_MEM, (hd + 1) * N_MEM)
            ph, dph = p[:, cs], dp[:, cs]
            dss.append(ph * (dph - jnp.sum(ph * dph, axis=-1, keepdims=True)) * ATT_SCALE)
        dsb = jnp.concatenate(dss, axis=1).astype(BF16)
        dq = _dot(dsb, k_ref[...])
        dkt_ref[...] += _dot_tn(qb, dsb)
        dv_ref[...] += _dot_tn(p.astype(BF16), dob)

        dhb = jnp.concatenate([da1, da2, dhu, dhv, dq], axis=1).astype(BF16)
        dh_ref[...] = dhb
        dx_ref[...] = ALPHA * dz + _dot(dhb, wint_ref[...])

        @pl.when(i == nt - 1)
        def _():
            acc = dbsacc[...]
            head = lax.broadcasted_iota(jnp.int32, (CHUNK, GMLP_W), 1) // HEAD_DIM
            lane = lax.broadcasted_iota(jnp.int32, (CHUNK, LANE), 1)
            out = jnp.zeros((CHUNK, LANE), F32)
            for hd in range(GMLP_W // HEAD_DIM):
                sh = jnp.sum(jnp.where(head == hd, acc, 0.0), axis=1, keepdims=True)
                out = out + jnp.where(lane == hd, sh, 0.0)
            dbs_ref[...] = out

    rev = lambda w: pl.BlockSpec((t, w), lambda i: (nt - 1 - i, 0))
    out_shape = [
        jax.ShapeDtypeStruct((s, D_MODEL), F32), jax.ShapeDtypeStruct((s, IN_W), BF16),
        jax.ShapeDtypeStruct((s, D_MODEL), BF16),
        jax.ShapeDtypeStruct((8, D_MODEL), F32), jax.ShapeDtypeStruct((8, CONV_W), F32),
        jax.ShapeDtypeStruct((CONV_HALO, CONV_W), F32), jax.ShapeDtypeStruct((3, 2 * CHUNK, CHUNK), F32),
        jax.ShapeDtypeStruct((CHUNK, LANE), F32),
        jax.ShapeDtypeStruct((XATTN_W, XATTN_HEADS * N_MEM), F32), jax.ShapeDtypeStruct((XATTN_HEADS * N_MEM, XATTN_W), F32),
    ]
    out_specs = [rev(D_MODEL), rev(IN_W), rev(D_MODEL)] + [_const_spec(o.shape) for o in out_shape[3:]]
    return _call_with_side(
        body, side_ops, name="mixer_bwd_d", grid=(nt,),
        in_specs=[rev(D_MODEL), rev(D_MODEL), rev(1), rev(IN_W), rev(CONV_W), rev(GMLP_W), rev(XATTN_HEADS * N_MEM),
                  _layer_spec((8, D_MODEL), l), _layer_spec((CONV_HALO, CONV_W), l), _layer_spec((8, CONV_W), l),
                  _layer_spec((3, 2 * CHUNK, CHUNK), l),
                  _const_spec((XATTN_HEADS * N_MEM, XATTN_W)), _const_spec((XATTN_W, XATTN_HEADS * N_MEM)),
                  _layer_spec((D_MODEL, D_MODEL), 0, resident=True), _layer_spec((IN_W, D_MODEL), 0, resident=True)],
        out_specs=out_specs, out_shape=out_shape,
        scratch_shapes=[pltpu.VMEM((t + CONV_HALO, CONV_W), F32), pltpu.VMEM((t, GMLP_W), F32),
                        pltpu.VMEM((CHUNK, GMLP_W), F32), pltpu.VMEM((t + 24, CONV_W), F32)],
        operands=(gz, xh1, rstd1, h, ac, mixed, probs, pd, conv_w, pa, wstt, k_all, vt_all, w_out, win_t),
        semantics=("arbitrary",))


def _mixer_bwd_w(xin, pd, dh, cat, dmix, l, t, side_ops=None):
    s = xin.shape[0]
    nt = s // t

    def body(x_ref, pd_ref, dh_ref, cat_ref, dmix_ref, dwin_ref, dwout_ref):
        @pl.when(pl.program_id(0) == 0)
        def _():
            dwin_ref[...] = jnp.zeros_like(dwin_ref)
            dwout_ref[...] = jnp.zeros_like(dwout_ref)

        xb = (x_ref[...] * _row(pd_ref, PD_GIN) + _row(pd_ref, PD_BIN)).astype(BF16)
        dwin_ref[...] += _dot_tn(dh_ref[...], xb)
        dwout_ref[...] += _dot_tn(cat_ref[...], dmix_ref[...])

    tok = lambda w: pl.BlockSpec((t, w), lambda i: (i, 0))
    return _call_with_side(
        body, side_ops, name="mixer_bwd_w", grid=(nt,),
        in_specs=[tok(D_MODEL), _layer_spec((8, D_MODEL), l), tok(IN_W), tok(D_MODEL), tok(D_MODEL)],
        out_specs=[_layer_spec((IN_W, D_MODEL), 0, resident=True), _layer_spec((D_MODEL, D_MODEL), 0, resident=True)],
        out_shape=[jax.ShapeDtypeStruct((1, IN_W, D_MODEL), F32), jax.ShapeDtypeStruct((1, D_MODEL, D_MODEL), F32)],
        scratch_shapes=[], operands=(xin, pd, dh, cat, dmix), semantics=("arbitrary",))


PF_W0, PF_B = 0, 3


def _ffn_fwd(xh1, pd, wup_t, pf, w_d, l, t, side_ops=None):
    s = xh1.shape[0]
    nt = s // t

    def body(xh_ref, pd_ref, wg_ref, wv_ref, pf_ref, wd_ref, xh2_ref, rstd_ref, upg_ref, upv_ref, fbuf):
        i = pl.program_id(0)

        @pl.when(i == 0)
        def _():
            fbuf[0:FFN_HALO, :] = jnp.zeros((FFN_HALO, FF_P), F32)

        x1 = xh_ref[...] * _row(pd_ref, PD_G1) + _row(pd_ref, PD_B1)
        xb = x1.astype(BF16)
        y = jnp.zeros((t, D_MODEL), F32)
        for hf in range(2):
            cs = slice(hf * FF_H, (hf + 1) * FF_H)
            ug = _dot_nt(xb, wg_ref[cs, :])
            uv = _dot_nt(xb, wv_ref[cs, :])
            upg_ref[:, cs] = ug
            upv_ref[:, cs] = uv
            fbuf[FFN_HALO:FFN_HALO + t, cs] = ug
            gate = jnp.zeros((t, FF_H), F32) + pf_ref[PF_B:PF_B + 1, cs]
            for k in range(FFN_CONV_K):
                off = FFN_HALO - (FFN_CONV_K - 1) + k
                gate = gate + fbuf[off:off + t, cs] * pf_ref[PF_W0 + k:PF_W0 + k + 1, cs]
            fbuf[0:FFN_HALO, cs] = fbuf[t:t + FFN_HALO, cs]
            hm = gate * _sigmoid(gate) * uv
            y = y + _dot(hm.astype(BF16), wd_ref[cs, :])
        xh2, rstd = _ln_fwd(ALPHA * x1 + y)
        xh2_ref[...] = xh2
        rstd_ref[...] = rstd

    tok = lambda w: pl.BlockSpec((t, w), lambda i: (i, 0))
    return _call_with_side(
        body, side_ops, name="ffn_fwd", grid=(nt,),
        in_specs=[tok(D_MODEL), _layer_spec((8, D_MODEL), l),
                  _layer_spec((FF_P, D_MODEL), 0, 0, resident=True), _layer_spec((FF_P, D_MODEL), 0, 1, resident=True),
                  _layer_spec((8, FF_P), l), _layer_spec((FF_P, D_MODEL), 0, resident=True)],
        out_specs=[tok(D_MODEL), tok(1), tok(FF_P), tok(FF_P)],
        out_shape=[jax.ShapeDtypeStruct((s, D_MODEL), F32), jax.ShapeDtypeStruct((s, 1), F32),
                   jax.ShapeDtypeStruct((s, FF_P), F32), jax.ShapeDtypeStruct((s, FF_P), F32)],
        scratch_shapes=[pltpu.VMEM((t + FFN_HALO, FF_P), F32)],
        operands=(xh1, pd, wup_t, wup_t, pf, w_d), semantics=("arbitrary",))


VF_W0, VF_B = 0, 3


def _ffn_bwd_d(gz_or_target, xh2, rstd2, upg, upv, pd, pf, w_d, wup_t, l, t, last, side_ops=None):
    s = xh2.shape[0]
    nt = s // t
    hb = t // FFN_HALO

    def body(gz_ref, xh2_ref, rstd_ref, upg_ref, halo_ref, upv_ref, pd_ref, pf_ref, wd_ref, wg_ref, wv_ref,
             dx_ref, dy_ref, dug_ref, duv_ref, hm_ref, vd_ref, vf_ref, gbuf, ebuf, s1buf, s2buf):
        i = pl.program_id(0)
        first_tile = i == nt - 1

        @pl.when(i == 0)
        def _():
            vd_ref[...] = jnp.zeros_like(vd_ref)
            vf_ref[...] = jnp.zeros_like(vf_ref)
            ebuf[t:t + FFN_HALO, :] = jnp.zeros((FFN_HALO, FF_P), F32)

        xh2_v = xh2_ref[...]
        if last:
            diff = xh2_v * _row(pd_ref, PD_G2) + _row(pd_ref, PD_B2) - gz_ref[...]
            vd_ref[VD_LOSS:VD_LOSS + 1, :] += _colsum(diff * diff)
            gz_v = diff * (1.0 / D_MODEL)
        else:
            gz_v = gz_ref[...]
        vd_ref[VD_LN_G:VD_LN_G + 1, :] += _colsum(gz_v * xh2_v)
        vd_ref[VD_LN_B:VD_LN_B + 1, :] += _colsum(gz_v)
        dz = _ln_bwd(gz_v * _row(pd_ref, PD_G2), xh2_v, rstd_ref[...])
        dyb = dz.astype(BF16)
        dy_ref[...] = dyb
        dx = ALPHA * dz
        for hf in range(2):
            cs = slice(hf * FF_H, (hf + 1) * FF_H)
            ug = upg_ref[:, cs]
            uv = upv_ref[:, cs]
            halo = halo_ref[:, cs]
            gbuf[0:FFN_HALO, :] = jnp.where(first_tile, jnp.zeros_like(halo), halo)
            gbuf[FFN_HALO:FFN_HALO + t, :] = ug
            s1buf[...] = gbuf[FFN_HALO - 1:FFN_HALO - 1 + t, :]
            s2buf[...] = gbuf[FFN_HALO - 2:FFN_HALO - 2 + t, :]
            ug1 = s1buf[...]
            ug2 = s2buf[...]
            gate = (pf_ref[PF_B:PF_B + 1, cs] + ug2 * pf_ref[PF_W0:PF_W0 + 1, cs] + ug1 * pf_ref[PF_W0 + 1:PF_W0 + 2, cs]
                    + ug * pf_ref[PF_W0 + 2:PF_W0 + 3, cs])
            sig = _sigmoid(gate)
            sl = gate * sig
            hm_ref[:, cs] = sl * uv
            dhm = _dot_nt(dyb, wd_ref[cs, :])
            duv = dhm * sl
            dgate = dhm * uv * (sig * (1.0 + gate * (1.0 - sig)))
            vf_ref[VF_B:VF_B + 1, cs] += _colsum(dgate)
            vf_ref[VF_W0:VF_W0 + 1, cs] += _colsum(dgate * ug2)
            vf_ref[VF_W0 + 1:VF_W0 + 2, cs] += _colsum(dgate * ug1)
            vf_ref[VF_W0 + 2:VF_W0 + 3, cs] += _colsum(dgate * ug)
            ebuf[0:t, cs] = dgate
            dug = (ebuf[2:2 + t, cs] * pf_ref[PF_W0:PF_W0 + 1, cs] + ebuf[1:1 + t, cs] * pf_ref[PF_W0 + 1:PF_W0 + 2, cs]
                   + dgate * pf_ref[PF_W0 + 2:PF_W0 + 3, cs])
            ebuf[t:t + FFN_HALO, cs] = ebuf[0:FFN_HALO, cs]
            dugb = dug.astype(BF16)
            duvb = duv.astype(BF16)
            dug_ref[:, cs] = dugb
            duv_ref[:, cs] = duvb
            dx = dx + _dot(dugb, wg_ref[cs, :]) + _dot(duvb, wv_ref[cs, :])
        dx_ref[...] = dx

        if last:
            @pl.when(i == nt - 1)
            def _():
                tot = jnp.sum(vd_ref[VD_LOSS:VD_LOSS + 1, :], axis=1, keepdims=True)
                vd_ref[VD_LOSS:VD_LOSS + 1, :] = jnp.zeros((1, D_MODEL), F32) + tot

    rev = lambda w: pl.BlockSpec((t, w), lambda i: (nt - 1 - i, 0))
    halo_spec = pl.BlockSpec((FFN_HALO, FF_P), lambda i: (jnp.maximum((nt - 1 - i) * hb - 1, 0), 0))
    out_shape = [jax.ShapeDtypeStruct((s, D_MODEL), F32), jax.ShapeDtypeStruct((s, D_MODEL), BF16),
                 jax.ShapeDtypeStruct((s, FF_P), BF16), jax.ShapeDtypeStruct((s, FF_P), BF16),
                 jax.ShapeDtypeStruct((s, FF_P), F32),
                 jax.ShapeDtypeStruct((8, D_MODEL), F32), jax.ShapeDtypeStruct((8, FF_P), F32)]
    return _call_with_side(
        body, side_ops, name="ffn_bwd_d_last" if last else "ffn_bwd_d", grid=(nt,),
        in_specs=[rev(D_MODEL), rev(D_MODEL), rev(1), rev(FF_P), halo_spec, rev(FF_P),
                  _layer_spec((8, D_MODEL), l), _layer_spec((8, FF_P), l),
                  _layer_spec((FF_P, D_MODEL), 0, resident=True),
                  _layer_spec((FF_P, D_MODEL), 0, 0, resident=True), _layer_spec((FF_P, D_MODEL), 0, 1, resident=True)],
        out_specs=[rev(D_MODEL), rev(D_MODEL), rev(FF_P), rev(FF_P), rev(FF_P),
                   _const_spec((8, D_MODEL)), _const_spec((8, FF_P))],
        out_shape=out_shape,
        scratch_shapes=[pltpu.VMEM((t + FFN_HALO, FF_H), F32), pltpu.VMEM((t + FFN_HALO, FF_P), F32),
                        pltpu.VMEM((t, FF_H), F32), pltpu.VMEM((t, FF_H), F32)],
        operands=(gz_or_target, xh2, rstd2, upg, upg, upv, pd, pf, w_d, wup_t, wup_t), semantics=("arbitrary",))


def _ffn_bwd_w(xh1, pd, dy, dug, duv, hm, l, t, side_ops=None):
    s = xh1.shape[0]
    nt = s // t

    def body(xh_ref, pd_ref, dy_ref, dug_ref, duv_ref, hm_ref, dwup_ref, dwd_ref):
        @pl.when(pl.program_id(1) == 0)
        def _():
            dwup_ref[...] = jnp.zeros_like(dwup_ref)
            dwd_ref[...] = jnp.zeros_like(dwd_ref)

        xb = (xh_ref[...] * _row(pd_ref, PD_G1) + _row(pd_ref, PD_B1)).astype(BF16)
        dwup_ref[0] += _dot_tn(dug_ref[...], xb)
        dwup_ref[1] += _dot_tn(duv_ref[...], xb)
        dwd_ref[...] += _dot_tn(hm_ref[...].astype(BF16), dy_ref[...])

    tok = lambda w: pl.BlockSpec((t, w), lambda c, i: (i, 0))
    half = pl.BlockSpec((t, FF_H), lambda c, i: (i, c))
    return _call_with_side(
        body, side_ops, name="ffn_bwd_w", grid=(2, nt),
        in_specs=[tok(D_MODEL), pl.BlockSpec((None, 8, D_MODEL), lambda c, i: (l, 0, 0)), tok(D_MODEL), half, half, half],
        out_specs=[pl.BlockSpec((None, 2, FF_H, D_MODEL), lambda c, i: (0, 0, c, 0), pipeline_mode=pl.Buffered(1)),
                   pl.BlockSpec((None, FF_H, D_MODEL), lambda c, i: (0, c, 0), pipeline_mode=pl.Buffered(1))],
        out_shape=[jax.ShapeDtypeStruct((1, 2, FF_P, D_MODEL), F32), jax.ShapeDtypeStruct((1, FF_P, D_MODEL), F32)],
        scratch_shapes=[], operands=(xh1, pd, dy, dug, duv, hm), semantics=("arbitrary", "arbitrary"))


def _adamw_math(w, g, m, v):
    nm = ADAM_B1 * m + (1.0 - ADAM_B1) * g
    nv = ADAM_B2 * v + (1.0 - ADAM_B2) * (g * g)
    m_hat = nm / (1.0 - ADAM_B1 ** ADAM_STEP)
    v_hat = nv / (1.0 - ADAM_B2 ** ADAM_STEP)
    return -ADAM_LR * (m_hat / (jnp.sqrt(v_hat) + ADAM_EPS) + ADAM_WD * w), nm, nv


def _adamw_layers(w, gs, m, v, name):
    shp = w.shape
    _, rows, cols = shp
    tr = _row_tile(rows, cols * 4, mult=8)
    nb = rows // tr

    def body(w_ref, g0_ref, g1_ref, m_ref, v_ref, g_ref, d_ref, nm_ref, nv_ref):
        g = jnp.where(pl.program_id(0) == 0, g0_ref[...], g1_ref[...])
        g_ref[...] = g
        d_ref[...], nm_ref[...], nv_ref[...] = _adamw_math(w_ref[...], g, m_ref[...], v_ref[...])

    stacked = pl.BlockSpec((tr, cols), lambda l, i: (l * nb + i, 0))
    single = pl.BlockSpec((tr, cols), lambda l, i: (i, 0))
    sh = jax.ShapeDtypeStruct((DEPTH * rows, cols), F32)
    flat = lambda a: a.reshape(DEPTH * rows, cols)
    outs = pl.pallas_call(body, name=name, grid=(DEPTH, nb), in_specs=[stacked, single, single, stacked, stacked],
                          out_specs=[stacked] * 4, out_shape=[sh] * 4,
                          compiler_params=_cparams(("arbitrary", "arbitrary")))(flat(w), gs[0], gs[1], flat(m), flat(v))
    return [o.reshape(shp) for o in outs]


def _adamw_small(ws, gs, ms, vs):
    n = len(ws)

    def body(*refs):
        w_refs, g_refs, m_refs, v_refs = refs[:n], refs[n:2 * n], refs[2 * n:3 * n], refs[3 * n:4 * n]
        d_refs, nm_refs, nv_refs = refs[4 * n:5 * n], refs[5 * n:6 * n], refs[6 * n:7 * n]
        for k in range(n):
            d_refs[k][...], nm_refs[k][...], nv_refs[k][...] = _adamw_math(w_refs[k][...], g_refs[k][...], m_refs[k][...],
                                                                             v_refs[k][...])

    shapes = [jax.ShapeDtypeStruct(w.shape, F32) for w in ws]
    outs = pl.pallas_call(body, name="adamw_small", out_shape=shapes * 3, compiler_params=_cparams())(*ws, *gs, *ms, *vs)
    return outs[:n], outs[n:2 * n], outs[2 * n:]


def _all_gather_chips(tensors, name):
    n = len(tensors)
    halves = [a.shape[1] // 2 for a in tensors]

    def body(*refs):
        x_refs, out_refs = refs[:n], refs[n:2 * n]
        send_sems, recv_sems, local_sems = refs[2 * n:]
        x, y, c, chips = _my_place()
        me, sibling, mej = (x, y, c), (x, y, 1 - c), 2 * x + y

        def rows(tn, px, py, pc):
            return out_refs[tn].at[:, 2 * px + py, pl.ds(pc * halves[tn], halves[tn]), :]

        def copy(tn, k, block, to, src=None):
            return pltpu.make_async_remote_copy(
                src_ref=rows(tn, *block) if src is None else src, dst_ref=rows(tn, *block),
                send_sem=send_sems.at[tn, k], recv_sem=recv_sems.at[tn, k], device_id=to, device_id_type=MESH)

        mine_src = [x_refs[tn].at[:, pl.ds(c * halves[tn], halves[tn]), :] for tn in range(n)]
        mine = [pltpu.make_async_copy(mine_src[tn], rows(tn, *me), local_sems.at[tn]) for tn in range(n)]
        first = []
        for j, chip in enumerate(chips):
            first += [copy(tn, 1 + j, me, (*chip, c), src=mine_src[tn]) for tn in range(n)]
        first += [copy(tn, 0, me, sibling, src=mine_src[tn]) for tn in range(n)]
        for cp in first + mine:
            cp.start()
        passed = []
        for j, chip in enumerate(chips):
            for tn in range(n):
                copy(tn, 1 + j, (*chip, c), me).wait_recv()
                fwd = copy(tn, 4 + j, (*chip, c), sibling)
                fwd.start()
                passed.append(fwd)
        for tn in range(n):
            copy(tn, 0, sibling, me).wait_recv()
            for j, chip in enumerate(chips):
                copy(tn, 4 + j, (*chip, 1 - c), me).wait_recv()
        for cp in first + passed:
            cp.wait_send()
        for cp in mine:
            cp.wait()

    return pl.pallas_call(
        body, name=name,
        out_shape=[jax.ShapeDtypeStruct((a.shape[0], N_CHIPS) + a.shape[1:], a.dtype) for a in tensors],
        in_specs=[ANY] * n, out_specs=[ANY] * n,
        scratch_shapes=[pltpu.SemaphoreType.DMA((n, 7)), pltpu.SemaphoreType.DMA((n, 7)), pltpu.SemaphoreType.DMA((n,))],
    )(*tensors)


def _swap_op(g5s):
    n = len(g5s)

    def make(ins, outs, scr):
        send_sems, recv_sems = scr
        x, y, c, _ = _my_place()

        def copies():
            return [pltpu.make_async_remote_copy(
                src_ref=ins[tn].at[:, :, 1 - c], dst_ref=outs[tn], send_sem=send_sems.at[tn], recv_sem=recv_sems.at[tn],
                device_id=(x, y, 1 - c), device_id_type=MESH) for tn in range(n)]

        def start():
            for cp in copies():
                cp.start()

        def wait():
            for cp in copies():
                cp.wait()

        return start, wait

    return _SideOp(g5s, [jax.ShapeDtypeStruct(g.shape[:2] + g.shape[3:], g.dtype) for g in g5s],
                   [pltpu.SemaphoreType.DMA((n,)), pltpu.SemaphoreType.DMA((n,))], make)


def _scatter_op(parts):
    n = len(parts)

    def make(ins, outs, scr):
        send_sems, recv_sems = scr
        x, y, c, chips = _my_place()

        def copies():
            return [pltpu.make_async_remote_copy(
                src_ref=ins[tn].at[:, 2 * px + py], dst_ref=outs[tn].at[:, k],
                send_sem=send_sems.at[tn, k], recv_sem=recv_sems.at[tn, k],
                device_id=(px, py, c), device_id_type=MESH) for k, (px, py) in enumerate(chips) for tn in range(n)]

        def start():
            for cp in copies():
                cp.start()

        def wait():
            for cp in copies():
                cp.wait()

        return start, wait

    return _SideOp(parts, [jax.ShapeDtypeStruct((p.shape[0], 3) + p.shape[2:], p.dtype) for p in parts],
                   [pltpu.SemaphoreType.DMA((n, 3)), pltpu.SemaphoreType.DMA((n, 3))], make)


def _sgather_op(fs):
    n = len(fs)

    def make(ins, outs, scr):
        send_sems, recv_sems = scr
        x, y, c, _ = _my_place()

        def copy(tn, dst_half):
            return pltpu.make_async_remote_copy(
                src_ref=outs[tn].at[:, c], dst_ref=outs[tn].at[:, dst_half], send_sem=send_sems.at[tn],
                recv_sem=recv_sems.at[tn], device_id=(x, y, 1 - c), device_id_type=MESH)

        def start():
            for tn in range(n):
                copy(tn, c).start()

        def wait():
            for tn in range(n):
                copy(tn, 1 - c).wait_recv()
                copy(tn, c).wait_send()

        return start, wait

    return _SideOp(fs, [jax.ShapeDtypeStruct(f.shape, f.dtype) for f in fs],
                   [pltpu.SemaphoreType.DMA((n,)), pltpu.SemaphoreType.DMA((n,))], make, aliases={tn: tn for tn in range(n)})


def _reduce_group_in_vmem(g5s, parts2, slots2):
    n, n2 = len(g5s), len(parts2)
    dims = [g.shape[3:] for g in g5s]
    dims2 = [p.shape[2:] for p in parts2]

    def body(*refs):
        g_refs, p2_refs, sl2_refs = refs[:n], refs[n:n + n2], refs[n + n2:n + 2 * n2]
        out_refs = refs[n + 2 * n2:2 * n + 3 * n2]
        scr = refs[2 * n + 3 * n2:]
        own, recv, part, slots = (scr[k * n:(k + 1) * n] for k in range(4))
        scr = scr[4 * n:]
        outv, own2, got2 = scr[:n + n2], scr[n + n2:n + 2 * n2], scr[n + 2 * n2:n + 3 * n2]
        s1, r1, s2, r2, s3, r3, lsem, lsem2 = scr[n + 3 * n2:]
        x, y, c, chips = _my_place()
        mej, sibling = 2 * x + y, (x, y, 1 - c)
        loads2 = ([pltpu.make_async_copy(p2_refs[t].at[0, pl.ds(mej, 1)], own2[t], lsem2.at[t, 0]) for t in range(n2)]
                  + [pltpu.make_async_copy(sl2_refs[t].at[0], got2[t], lsem2.at[t, 1]) for t in range(n2)])
        for cp in loads2:
            cp.start()

        loads = [pltpu.make_async_copy(g_refs[t].at[0, :, c], own[t], lsem.at[t]) for t in range(n)]
        swaps = [pltpu.make_async_remote_copy(src_ref=g_refs[t].at[0, :, 1 - c], dst_ref=recv[t], send_sem=s1.at[t],
                                              recv_sem=r1.at[t], device_id=sibling, device_id_type=MESH) for t in range(n)]
        for cp in loads + swaps:
            cp.start()
        for cp in loads + swaps:
            cp.wait()
        for t in range(n):
            part[t][...] = (own[t][...] + recv[t][...]).astype(BF16)

        sends = [pltpu.make_async_remote_copy(src_ref=part[t].at[2 * px + py], dst_ref=slots[t].at[k], send_sem=s2.at[t, k],
                                              recv_sem=r2.at[t, k], device_id=(px, py, c), device_id_type=MESH)
                 for k, (px, py) in enumerate(chips) for t in range(n)]
        for cp in sends:
            cp.start()
        for cp in sends:
            cp.wait()
        for t in range(n):
            acc = part[t][pl.ds(mej, 1)].astype(F32)
            for k in range(3):
                acc = acc + slots[t][k:k + 1].astype(F32)
            outv[t][pl.ds(c, 1)] = acc
        for cp in loads2:
            cp.wait()
        for t in range(n2):
            acc = own2[t][...].astype(F32)
            for k in range(3):
                acc = acc + got2[t][k:k + 1].astype(F32)
            outv[n + t][pl.ds(c, 1)] = acc

        def back(t, half):
            return pltpu.make_async_remote_copy(src_ref=outv[t].at[c], dst_ref=outv[t].at[half], send_sem=s3.at[t],
                                                recv_sem=r3.at[t], device_id=sibling, device_id_type=MESH)

        for t in range(n + n2):
            back(t, c).start()
        for t in range(n + n2):
            back(t, 1 - c).wait_recv()
            back(t, c).wait_send()
        for t in range(n + n2):
            pltpu.sync_copy(outv[t], out_refs[t].at[0])

    vm = lambda shape, dt: pltpu.VMEM(shape, dt)
    dma = pltpu.SemaphoreType.DMA
    scratch = ([vm((N_CHIPS,) + d, F32) for d in dims] + [vm((N_CHIPS,) + d, F32) for d in dims]
               + [vm((N_CHIPS,) + d, BF16) for d in dims] + [vm((3,) + d, BF16) for d in dims]
               + [vm((2,) + d, F32) for d in dims + dims2] + [vm((1,) + d, BF16) for d in dims2] + [vm((3,) + d, BF16) for d in dims2]
               + [dma((n,)), dma((n,)), dma((n, 3)), dma((n, 3)), dma((n + n2,)), dma((n + n2,)), dma((n,)), dma((n2, 2))])
    nio = n + 2 * n2
    return pl.pallas_call(body, name="rs_tail_fused", in_specs=[ANY] * nio, out_specs=[ANY] * (n + n2),
                          out_shape=[jax.ShapeDtypeStruct((1, 2) + d, F32) for d in dims + dims2], scratch_shapes=scratch,
                          compiler_params=_cparams())(*g5s, *parts2, *slots2)


def _gather_devices_op(xs):
    def make(ins, outs, scr):
        send_sems, recv_sems, local_sem = scr
        x, y, c, chips = _my_place()
        peers = [(x, y, 1 - c)] + [(px, py, pc) for (px, py) in chips for pc in (c, 1 - c)]
        me = 4 * x + 2 * y + c

        def copy(k, slot):
            return pltpu.make_async_remote_copy(
                src_ref=ins[0], dst_ref=outs[0].at[slot], send_sem=send_sems.at[k], recv_sem=recv_sems.at[k],
                device_id=peers[k], device_id_type=MESH)

        def local():
            return pltpu.make_async_copy(ins[0], outs[0].at[me], local_sem)

        def start():
            for k in range(7):
                copy(k, me).start()
            local().start()

        def wait():
            for k, (px, py, pc) in enumerate(peers):
                copy(k, 4 * px + 2 * py + pc).wait_recv()
                copy(k, me).wait_send()
            local().wait()

        return start, wait

    return _SideOp([xs], [jax.ShapeDtypeStruct((8,) + xs.shape, xs.dtype)],
                   [pltpu.SemaphoreType.DMA((7,)), pltpu.SemaphoreType.DMA((7,)), pltpu.SemaphoreType.DMA], make)


def _add_halves(gs, recvs, place):
    n = len(gs)

    def body(place_ref, *refs):
        g_refs, r_refs, o_refs = refs[:n], refs[n:2 * n], refs[2 * n:]
        for tn in range(n):
            o_refs[tn][...] = (g_refs[tn][...] + r_refs[tn][...]).astype(BF16)

    def gspec(g):
        return pl.BlockSpec((None, None, None) + g.shape[3:], lambda l, j, p: (l, j, p[1], 0, 0))

    def rspec(r):
        return pl.BlockSpec((None, None) + r.shape[2:], lambda l, j, p: (l, j, 0, 0))

    grid_spec = pltpu.PrefetchScalarGridSpec(
        num_scalar_prefetch=1, grid=(gs[0].shape[0], N_CHIPS),
        in_specs=[gspec(g) for g in gs] + [rspec(r) for r in recvs], out_specs=[rspec(r) for r in recvs])
    return pl.pallas_call(body, name="rs_add", grid_spec=grid_spec,
                          out_shape=[jax.ShapeDtypeStruct(r.shape, BF16) for r in recvs],
                          compiler_params=_cparams(("arbitrary", "arbitrary")))(place, *gs, *recvs)


def _sum_slots(parts, slots, place):
    n = len(parts)

    def body(place_ref, *refs):
        p_refs, s_refs, o_refs = refs[:n], refs[n:2 * n], refs[2 * n:]
        for tn in range(n):
            acc = p_refs[tn][...].astype(F32)
            for k in range(3):
                acc = acc + s_refs[tn][k].astype(F32)
            o_refs[tn][...] = acc

    def pspec(p):
        return pl.BlockSpec((None, None) + p.shape[2:], lambda l, pl_: (l, pl_[0], 0, 0))

    def sspec(sl):
        return pl.BlockSpec((None,) + sl.shape[1:], lambda l, pl_: (l, 0, 0, 0))

    def ospec(p):
        return pl.BlockSpec((None, None) + p.shape[2:], lambda l, pl_: (l, pl_[1], 0, 0))

    grid_spec = pltpu.PrefetchScalarGridSpec(
        num_scalar_prefetch=1, grid=(parts[0].shape[0],),
        in_specs=[pspec(p) for p in parts] + [sspec(sl) for sl in slots], out_specs=[ospec(p) for p in parts])
    return pl.pallas_call(body, name="rs_sum", grid_spec=grid_spec,
                          out_shape=[jax.ShapeDtypeStruct((p.shape[0], 2) + p.shape[2:], F32) for p in parts],
                          compiler_params=_cparams(("arbitrary",)))(place, *parts, *slots)


def _sum_devices(gathered):
    m_per = gathered[0].shape[1]

    def body(*refs):
        o_ref = refs[-1]
        for l, g_ref in enumerate(refs[:-1]):
            acc = g_ref[0]
            for d in range(1, 8):
                acc = acc + g_ref[d]
            o_ref[l * m_per:(l + 1) * m_per, :] = acc

    return pl.pallas_call(body, name="small_sum", out_shape=jax.ShapeDtypeStruct((len(gathered) * m_per, LANE), F32),
                          compiler_params=_cparams())(*gathered)


def _pad_ff_cols(a):
    lead = a.shape[:-1]
    n = a.shape[-1] // FF_Q
    a = a.reshape(*lead, n, FF_Q)
    a = jnp.pad(a, [(0, 0)] * len(lead) + [(0, 0), (0, FF_QP - FF_Q)])
    return a.reshape(*lead, n * FF_QP)


def _unpad_ff_cols(a):
    lead = a.shape[:-1]
    n = a.shape[-1] // FF_QP
    return a.reshape(*lead, n, FF_QP)[..., :FF_Q].reshape(*lead, n * FF_Q)


def _pack_small(parts):
    flat = jnp.concatenate([p.reshape(-1) for p in parts])
    return flat.reshape(-1, LANE)


SMALL_SHAPES = [("conv_a_w", (CONV_K, CONV_W)), ("conv_a_b", (CONV_W,)), ("ln_a_g", (CONV_W,)), ("ln_a_b", (CONV_W,)),
                ("ln_v_g", (GMLP_W,)), ("ln_v_b", (GMLP_W,)), ("w_s", (6, CHUNK, CHUNK)), ("b_s", (6, CHUNK)),
                ("ln1_g", (D_MODEL,)), ("ln1_b", (D_MODEL,)), ("conv_f_w", (FFN_CONV_K, D_FF)), ("conv_f_b", (D_FF,)),
                ("ln2_g", (D_MODEL,)), ("ln2_b", (D_MODEL,))]
SMALL_FLOATS = 128000


def _unpack_small(flat2d):
    flat = flat2d.reshape(DEPTH, -1)
    out, o = {}, 0
    for name, shp in SMALL_SHAPES:
        n = 1
        for d in shp:
            n *= d
        out[name] = flat[:, o:o + n].reshape((DEPTH,) + shp)
        o += n
    return out


def _rows8(rows):
    blk = jnp.stack(rows, axis=1)
    return jnp.pad(blk, ((0, 0), (0, 8 - len(rows)), (0, 0)))


def kernel(x, mem, w_in, conv_a_w, conv_a_b, ln_a_g, ln_a_b, ln_v_g, ln_v_b, w_s, b_s, w_mk, w_mv, w_out, ln1_g, ln1_b, w_up, conv_f_w, conv_f_b, w_down, ln2_g, ln2_b, loss_target, m_w_in, m_conv_a_w, m_conv_a_b, m_ln_a_g, m_ln_a_b, m_ln_v_g, m_ln_v_b, m_w_s, m_b_s, m_w_mk, m_w_mv, m_w_out, m_ln1_g, m_ln1_b, m_w_up, m_conv_f_w, m_conv_f_b, m_w_down, m_ln2_g, m_ln2_b, v_w_in, v_conv_a_w, v_conv_a_b, v_ln_a_g, v_ln_a_b, v_ln_v_g, v_ln_v_b, v_w_s, v_b_s, v_w_mk, v_w_mv, v_w_out, v_ln1_g, v_ln1_b, v_w_up, v_conv_f_w, v_conv_f_b, v_w_down, v_ln2_g, v_ln2_b):
    seq = x.shape[1]
    t_fwd = min(512, seq)
    t_bwd = min(256, seq)
    t_wg = min(1024, seq)
    chip = 2 * lax.axis_index("x") + lax.axis_index("y")
    core = lax.axis_index("c")
    place = jnp.stack([chip, core]).astype(jnp.int32)
    x0 = x[0]
    mem0 = mem[0]
    target = loss_target[0]

    sh_in = w_in.transpose(0, 2, 1).astype(BF16)
    sh_mk, sh_mv, sh_out = w_mk.astype(BF16), w_mv.astype(BF16), w_out.astype(BF16)
    sh_up = _pad_ff_cols(w_up).transpose(0, 2, 1).astype(BF16)
    sh_dn = jnp.pad(w_down, ((0, 0), (0, FF_QP - FF_Q), (0, 0))).astype(BF16)

    def mixer_weights(g_in, g_mk, g_mv, g_out):
        return dict(win_t=g_in.reshape(1, IN_W, D_MODEL), wmk=g_mk.reshape(1, D_MODEL, XATTN_W),
                    wmv=g_mv.reshape(1, D_MODEL, XATTN_W), wout=g_out.reshape(1, D_MODEL, D_MODEL))

    def ffn_weights(g_up, g_dn):
        return dict(wup_t=g_up.reshape(1, 2, FF_P, D_MODEL), wdown=g_dn.reshape(1, FF_P, D_MODEL))

    n_ca = conv_a_w.size
    small_w = _pack_small([conv_a_w, conv_f_w, jnp.zeros((2 * 80 * LANE - n_ca - conv_f_w.size,), F32)])[None]
    *g_mixer0, small_g = _all_gather_chips([sh_in[:1], sh_mk[:1], sh_mv[:1], sh_out[:1], small_w], "ag_mixer0")
    wts = [mixer_weights(*g_mixer0), None]
    small_g = small_g.reshape(N_CHIPS, -1)
    conv_a_full = small_g[:, :n_ca].reshape(N_CHIPS, DEPTH, CONV_K, CONV_W // 4).transpose(1, 2, 0, 3).reshape(DEPTH, CONV_K, CONV_W)
    conv_f_full = small_g[:, n_ca:n_ca + conv_f_w.size].reshape(N_CHIPS, DEPTH, FFN_CONV_K, FF_Q).transpose(1, 2, 0, 3).reshape(DEPTH, FFN_CONV_K, D_FF)

    tril = jnp.tril(jnp.ones((CHUNK, CHUNK), dtype=bool))
    ws_m = jnp.where(tril, w_s, 0.0)
    wst = ws_m.reshape(DEPTH, 3, 2 * CHUNK, CHUNK).astype(BF16)
    wstt = ws_m.transpose(0, 1, 3, 2).reshape(DEPTH, 3, 2 * CHUNK, CHUNK).astype(BF16)
    bst = jnp.repeat(b_s.transpose(0, 2, 1), HEAD_DIM, axis=2)
    conv_w = jnp.pad(conv_a_full, ((0, 0), (0, CONV_HALO - CONV_K), (0, 0)))
    pa = _rows8([conv_a_b, ln_a_g, ln_a_b, ln_v_g, ln_v_b])
    gin = jnp.concatenate([jnp.ones((1, D_MODEL), F32), ln2_g[:DEPTH - 1]], axis=0)
    bin_ = jnp.concatenate([jnp.zeros((1, D_MODEL), F32), ln2_b[:DEPTH - 1]], axis=0)
    pd = _rows8([gin, bin_, ln1_g, ln1_b, ln2_g, ln2_b])
    pf = jnp.concatenate([_pad_ff_cols(conv_f_full), _pad_ff_cols(conv_f_b)[:, None, :],
                          jnp.zeros((DEPTH, 8 - FFN_CONV_K - 1, FF_P), F32)], axis=1)

    acts = []
    xin = x0
    for l in range(DEPTH):
        w = wts[l]
        kt_all, k_all, v_all, vt_all = _kv_fwd(mem0, w["wmk"], w["wmv"])
        ops = [_gather_chips_op([sh_up[0], sh_dn[0]])] if l == 0 else None
        (xh1, rstd1, h, ac, cat, mixed, probs), side = _mixer_fwd(xin, pd, w["win_t"], conv_w, pa, wst, bst, kt_all, v_all, w["wout"],
                                                    l, t_fwd, ops)
        if l == 0:
            w.update(ffn_weights(*side[0]))
        ops = [_gather_chips_op([sh_in[1], sh_mk[1], sh_mv[1], sh_out[1], sh_up[1], sh_dn[1]])] if l == 0 else None
        (xh2, rstd2, upg, upv), side = _ffn_fwd(xh1, pd, w["wup_t"], pf, w["wdown"], l, t_fwd, ops)
        if l == 0:
            wts[1] = {**mixer_weights(*side[0][:4]), **ffn_weights(*side[0][4:])}
        acts.append(dict(xin=xin, k_all=k_all, vt_all=vt_all, xh1=xh1, rstd1=rstd1, h=h, ac=ac, cat=cat, mixed=mixed, probs=probs,
                         xh2=xh2, rstd2=rstd2, upg=upg, upv=upv))
        xin = xh2

    assert DEPTH == 2

    def halves_view(gs):
        return [g.reshape(1, N_CHIPS, 2, g.shape[1] // (2 * N_CHIPS), g.shape[2]) for g in gs]

    small = [None] * DEPTH
    small_packed = [None] * DEPTH
    small_gathered = [None] * DEPTH
    red_layers = [None] * DEPTH
    gz = target
    g5_prev = None
    for l in reversed(range(DEPTH)):
        a, w = acts[l], wts[l]
        last = l == DEPTH - 1
        (dx1, dy, dug, duv, hm, vd2, vf), side = _ffn_bwd_d(
            gz, a["xh2"], a["rstd2"], a["upg"], a["upv"], pd, pf, w["wdown"], w["wup_t"], l, t_bwd, last,
            [_swap_op(g5_prev), _gather_devices_op(small_packed[l + 1])] if g5_prev else None)
        if g5_prev:
            small_gathered[l + 1] = side[1][0]
        parts_prev = _add_halves(g5_prev, side[0], place) if g5_prev else None
        (gw_up_t, gw_down), side = _ffn_bwd_w(a["xh1"], pd, dy, dug, duv, hm, l, t_wg,
                                              [_scatter_op(parts_prev)] if g5_prev else None)
        halves_prev = _sum_slots(parts_prev, side[0], place) if g5_prev else None
        g5_ffn = halves_view([gw_up_t.reshape(1, 2 * FF_P, D_MODEL), gw_down])
        ops = ([_sgather_op(halves_prev)] if g5_prev else []) + ([_swap_op(g5_ffn)] if l == 0 else [])
        (dx0, dh, dmix, vd1, va, dcw, dws, dbs, dkt, dv), side = _mixer_bwd_d(
            dx1, a["xh1"], a["rstd1"], a["h"], a["ac"], a["mixed"], a["probs"], pd, conv_w, pa, wstt,
            a["k_all"], a["vt_all"], w["wout"], w["win_t"], l, t_fwd, ops)
        if g5_prev:
            red_layers[l + 1] = side[0]
        parts_ffn = _add_halves(g5_ffn, side[-1], place) if l == 0 else None
        dws6 = jnp.where(tril, dws.reshape(6, CHUNK, CHUNK), 0.0)
        small[l] = [dcw[:CONV_K], va[VA_CONV_B], va[VA_LNA_G], va[VA_LNA_B], va[VA_LNV_G], va[VA_LNV_B], dws6,
                    dbs[:, :6].T, vd1[VD_LN_G], vd1[VD_LN_B],
                    _unpad_ff_cols(vf[VF_W0:VF_W0 + FFN_CONV_K]), _unpad_ff_cols(vf[VF_B]),
                    vd2[VD_LN_G], vd2[VD_LN_B]]
        small[l].append(vd2[VD_LOSS] if last else jnp.zeros((D_MODEL,), F32))
        small_packed[l] = _pack_small(small[l])
        ops = [_scatter_op(parts_ffn), _gather_devices_op(small_packed[l])] if l == 0 else None
        (gw_in_t, gw_out), side = _mixer_bwd_w(a["xin"], pd, dh, a["cat"], dmix, l, t_wg, ops)
        gw_mk, gw_mv = _kv_bwd(mem0, dkt, dv)
        g5_mix = halves_view([gw_in_t, gw_mk, gw_mv, gw_out])
        if l == 0:
            small_gathered[l] = side[1][0]
            red_layers[0] = _reduce_group_in_vmem(g5_mix, parts_ffn, side[0])
        else:
            g5_prev = g5_mix + g5_ffn
        gz = dx0
    grad_x = gz[None]

    def shard_grads(red):
        r = [f.reshape(-1, f.shape[-1]) for f in red]
        return dict(w_in=r[0].T, w_mk=r[1], w_mv=r[2], w_out=r[3], w_up=_unpad_ff_cols(r[4].T), w_down=r[5])

    big_grads = [shard_grads(red_layers[l]) for l in range(DEPTH)]

    small_red = _sum_devices(small_gathered)
    sg = _unpack_small(small_red)
    g_conv_a_w = lax.dynamic_slice_in_dim(sg["conv_a_w"], chip * (CONV_W // 4), CONV_W // 4, axis=2)
    g_conv_f_w = lax.dynamic_slice_in_dim(sg["conv_f_w"], chip * FF_Q, FF_Q, axis=2)

    loss = 0.5 / D_MODEL * small_red.reshape(DEPTH, -1)[DEPTH - 1, SMALL_FLOATS]

    grads = dict(conv_a_w=g_conv_a_w, conv_a_b=sg["conv_a_b"], ln_a_g=sg["ln_a_g"], ln_a_b=sg["ln_a_b"],
                 ln_v_g=sg["ln_v_g"], ln_v_b=sg["ln_v_b"], w_s=sg["w_s"], b_s=sg["b_s"], ln1_g=sg["ln1_g"], ln1_b=sg["ln1_b"],
                 conv_f_w=g_conv_f_w, conv_f_b=sg["conv_f_b"], ln2_g=sg["ln2_g"], ln2_b=sg["ln2_b"])
    weights = dict(w_in=w_in, conv_a_w=conv_a_w, conv_a_b=conv_a_b, ln_a_g=ln_a_g, ln_a_b=ln_a_b, ln_v_g=ln_v_g,
                   ln_v_b=ln_v_b, w_s=w_s, b_s=b_s, w_mk=w_mk, w_mv=w_mv, w_out=w_out, ln1_g=ln1_g, ln1_b=ln1_b,
                   w_up=w_up, conv_f_w=conv_f_w, conv_f_b=conv_f_b, w_down=w_down, ln2_g=ln2_g, ln2_b=ln2_b)
    mom_m = dict(w_in=m_w_in, conv_a_w=m_conv_a_w, conv_a_b=m_conv_a_b, ln_a_g=m_ln_a_g, ln_a_b=m_ln_a_b, ln_v_g=m_ln_v_g,
                 ln_v_b=m_ln_v_b, w_s=m_w_s, b_s=m_b_s, w_mk=m_w_mk, w_mv=m_w_mv, w_out=m_w_out, ln1_g=m_ln1_g,
                 ln1_b=m_ln1_b, w_up=m_w_up, conv_f_w=m_conv_f_w, conv_f_b=m_conv_f_b, w_down=m_w_down, ln2_g=m_ln2_g,
                 ln2_b=m_ln2_b)
    mom_v = dict(w_in=v_w_in, conv_a_w=v_conv_a_w, conv_a_b=v_conv_a_b, ln_a_g=v_ln_a_g, ln_a_b=v_ln_a_b, ln_v_g=v_ln_v_g,
                 ln_v_b=v_ln_v_b, w_s=v_w_s, b_s=v_b_s, w_mk=v_w_mk, w_mv=v_w_mv, w_out=v_w_out, ln1_g=v_ln1_g,
                 ln1_b=v_ln1_b, w_up=v_w_up, conv_f_w=v_conv_f_w, conv_f_b=v_conv_f_b, w_down=v_w_down, ln2_g=v_ln2_g,
                 ln2_b=v_ln2_b)
    names = list(weights)
    big_names = ["w_in", "w_mk", "w_mv", "w_out", "w_up", "w_down"]
    delta, new_m, new_v = {}, {}, {}
    for n in big_names:
        grads[n], delta[n], new_m[n], new_v[n] = _adamw_layers(weights[n], [big_grads[l][n] for l in range(DEPTH)],
                                                               mom_m[n], mom_v[n], "adamw_" + n)
    small_names = [n for n in names if n not in big_names]
    ds, nms, nvs = _adamw_small([weights[n] for n in small_names], [grads[n] for n in small_names],
                                [mom_m[n] for n in small_names], [mom_v[n] for n in small_names])
    for n, d, nm, nv in zip(small_names, ds, nms, nvs):
        delta[n], new_m[n], new_v[n] = d, nm, nv

    return (loss, grad_x, *[grads[n] for n in names], *[delta[n] for n in names],
            *[new_m[n] for n in names], *[new_v[n] for n in names])
```

```python
import jax
import jax.numpy as jnp
from jax import lax
from jax.experimental import pallas as pl
from jax.experimental.pallas import tpu as pltpu

F32 = jnp.float32
BF16 = jnp.bfloat16

D_MODEL = 1024
DEPTH = 2
CONV_W = 384
GMLP_W = 384
XATTN_W = 256
XATTN_HEADS = 4
HEAD_DIM = 64
IN_W = 1792
CONV_K = 31
CHUNK = 128
N_MEM = 256
D_FF = 2752
FFN_CONV_K = 3
ALPHA = (2.0 * DEPTH) ** 0.25
LN_EPS = 1e-5
ATT_SCALE = 1.0 / 8.0
ADAM_LR, ADAM_B1, ADAM_B2, ADAM_EPS, ADAM_WD, ADAM_STEP = 0.001, 0.9, 0.999, 1e-08, 0.01, 10

N_CHIPS = 4
FF_Q = D_FF // N_CHIPS
FF_QP = 704
FF_H = 2 * FF_QP
FF_P = 4 * FF_QP
LANE = 128
CONV_HALO = 32
FFN_HALO = 8
BF16_ROWS = 16
VMEM_LIMIT = 60 * 1024 * 1024

MESH = pl.DeviceIdType.MESH
ANY = pl.BlockSpec(memory_space=pl.ANY)


def _cparams(sem=None, vmem=VMEM_LIMIT):
    kw = {"vmem_limit_bytes": vmem}
    if sem is not None:
        kw["dimension_semantics"] = sem
    return pltpu.CompilerParams(**kw)


def _row_tile(rows, row_bytes, limit=2 << 20, mult=BF16_ROWS):
    if rows * row_bytes <= limit:
        return rows
    best = None
    for cand in range(mult, rows, mult):
        if rows % cand == 0 and cand * row_bytes <= limit:
            best = cand
    assert best is not None, (rows, row_bytes)
    return best


def _const_spec(shape):
    nd = len(shape)
    return pl.BlockSpec(shape, lambda *_: (0,) * nd)


def _layer_spec(shape, *lead, resident=False):
    nd = len(shape)
    kw = {"pipeline_mode": pl.Buffered(1)} if resident else {}
    return pl.BlockSpec((None,) * len(lead) + tuple(shape), lambda *_: tuple(lead) + (0,) * nd, **kw)


def _sigmoid(x):
    return jax.nn.sigmoid(x)


def _gelu(x):
    return jax.nn.gelu(x)


def _gelu_and_grad(x):
    c = 0.7978845608028654
    a = 0.044715
    x2 = x * x
    t = jnp.tanh(c * (x + a * x * x2))
    h = 0.5 * (1.0 + t)
    return x * h, h + 0.5 * x * (1.0 - t * t) * c * (1.0 + 3.0 * a * x2)


def _ln_fwd(z):
    mu = jnp.mean(z, axis=-1, keepdims=True)
    zc = z - mu
    var = jnp.mean(zc * zc, axis=-1, keepdims=True)
    rstd = lax.rsqrt(var + LN_EPS)
    return zc * rstd, rstd


def _ln_bwd(dxh, xh, rstd):
    m1 = jnp.mean(dxh, axis=-1, keepdims=True)
    m2 = jnp.mean(dxh * xh, axis=-1, keepdims=True)
    return rstd * (dxh - m1 - xh * m2)


def _colsum(a):
    return jnp.sum(a, axis=0, keepdims=True)


def _dot(a, b):
    return jnp.dot(a, b, preferred_element_type=F32)


def _dot_tn(a, b):
    return lax.dot_general(a, b, (((0,), (0,)), ((), ())), preferred_element_type=F32)


def _dot_nt(a, b):
    return lax.dot_general(a, b, (((1,), (1,)), ((), ())), preferred_element_type=F32)


def _softmax_heads(sc):
    ps = []
    for hd in range(XATTN_HEADS):
        s = sc[:, hd * N_MEM:(hd + 1) * N_MEM]
        e = jnp.exp(s - jnp.max(s, axis=-1, keepdims=True))
        ps.append(e / jnp.sum(e, axis=-1, keepdims=True))
    return jnp.concatenate(ps, axis=1)


def _lane_lo(shape):
    return (lax.broadcasted_iota(jnp.int32, shape, len(shape) - 1) % LANE) < HEAD_DIM


def _spatial_mix(vnb, wst_ref, bst_ref, mix_ref, t):
    lo = _lane_lo((CHUNK, LANE))
    for n in range(t // CHUNK):
        rows = slice(n * CHUNK, (n + 1) * CHUNK)
        for j in range(GMLP_W // LANE):
            cols = slice(j * LANE, (j + 1) * LANE)
            r = _dot(wst_ref[j], vnb[rows, cols])
            mix_ref[rows, cols] = jnp.where(lo, r[:CHUNK], r[CHUNK:]) + bst_ref[:, cols]


def _kv_fwd(mem, w_mk, w_mv):
    def body(mem_ref, wk_ref, wv_ref, kt_ref, k_ref, v_ref, vt_ref):
        mb = mem_ref[...].astype(BF16)
        k = _dot(mb, wk_ref[...])
        v = _dot(mb, wv_ref[...])
        col = lax.broadcasted_iota(jnp.int32, (N_MEM, XATTN_W), 1) // HEAD_DIM
        ks = [jnp.where(col == hd, k, 0.0) for hd in range(XATTN_HEADS)]
        vs = [jnp.where(col == hd, v, 0.0) for hd in range(XATTN_HEADS)]
        k_ref[...] = jnp.concatenate(ks, axis=0).astype(BF16)
        v_ref[...] = jnp.concatenate(vs, axis=0).astype(BF16)
        kt_ref[...] = jnp.concatenate([x.T for x in ks], axis=1).astype(BF16)
        vt_ref[...] = jnp.concatenate([x.T for x in vs], axis=1).astype(BF16)

    wide = jax.ShapeDtypeStruct((XATTN_W, XATTN_HEADS * N_MEM), BF16)
    tall = jax.ShapeDtypeStruct((XATTN_HEADS * N_MEM, XATTN_W), BF16)
    wspec = _layer_spec((D_MODEL, XATTN_W), 0)
    return pl.pallas_call(body, name="kv_fwd", grid=(1,),
                          in_specs=[_const_spec((N_MEM, D_MODEL)), wspec, wspec],
                          out_specs=[_const_spec(wide.shape), _const_spec(tall.shape), _const_spec(tall.shape),
                                     _const_spec(wide.shape)],
                          out_shape=(wide, tall, tall, wide), compiler_params=_cparams(("arbitrary",)))(mem, w_mk, w_mv)


def _kv_bwd(mem, dkt_all, dv_all):
    def body(mem_ref, dkt_ref, dv_ref, gk_ref, gv_ref):
        col = lax.broadcasted_iota(jnp.int32, (N_MEM, XATTN_W), 1) // HEAD_DIM
        dk = jnp.zeros((N_MEM, XATTN_W), F32)
        dv = jnp.zeros((N_MEM, XATTN_W), F32)
        for hd in range(XATTN_HEADS):
            dk = dk + jnp.where(col == hd, dkt_ref[:, hd * N_MEM:(hd + 1) * N_MEM].T, 0.0)
            dv = dv + jnp.where(col == hd, dv_ref[hd * N_MEM:(hd + 1) * N_MEM, :], 0.0)
        mb = mem_ref[...].astype(BF16)
        gk_ref[0] = _dot_tn(mb, dk.astype(BF16))
        gv_ref[0] = _dot_tn(mb, dv.astype(BF16))

    out = jax.ShapeDtypeStruct((1, D_MODEL, XATTN_W), F32)
    return pl.pallas_call(body, name="kv_bwd", out_shape=(out, out), compiler_params=_cparams())(mem, dkt_all, dv_all)


def _my_place():
    x, y, c = lax.axis_index("x"), lax.axis_index("y"), lax.axis_index("c")
    chips = [(1 - x, y), (x, 1 - y), (1 - x, 1 - y)]
    return x, y, c, chips


class _SideOp:
    def __init__(self, ins, out_shapes, scratch, make, aliases=None):
        self.ins, self.out_shapes, self.scratch, self.make, self.aliases = list(ins), list(out_shapes), list(scratch), make, dict(aliases or {})


def _call_with_side(body, side_ops, *, name, grid, in_specs, out_specs, out_shape, scratch_shapes, operands, semantics):
    side_ops = list(side_ops or ())
    n_in, n_out, n_scr = len(in_specs), len(out_specs), len(scratch_shapes)
    s_ins = [a for op in side_ops for a in op.ins]
    s_outs = [o for op in side_ops for o in op.out_shapes]
    s_scr = [x for op in side_ops for x in op.scratch]
    aliases, oi, oo = {}, 0, 0
    for op in side_ops:
        for a, b in op.aliases.items():
            aliases[n_in + oi + a] = n_out + oo + b
        oi, oo = oi + len(op.ins), oo + len(op.out_shapes)

    def wrapped(*refs):
        ins, sins = refs[:n_in], refs[n_in:n_in + len(s_ins)]
        base = n_in + len(s_ins)
        outs, souts = refs[base:base + n_out], refs[base + n_out:base + n_out + len(s_outs)]
        base += n_out + len(s_outs)
        scr, sscr = refs[base:base + n_scr], refs[base + n_scr:]
        if side_ops:
            first = pl.program_id(0) == 0
            last = pl.program_id(0) == grid[0] - 1
            for d in range(1, len(grid)):
                first = jnp.logical_and(first, pl.program_id(d) == 0)
                last = jnp.logical_and(last, pl.program_id(d) == grid[d] - 1)
            hooks, a, b, c = [], 0, 0, 0
            for op in side_ops:
                hooks.append(op.make(sins[a:a + len(op.ins)], souts[b:b + len(op.out_shapes)], sscr[c:c + len(op.scratch)]))
                a, b, c = a + len(op.ins), b + len(op.out_shapes), c + len(op.scratch)

            @pl.when(first)
            def _():
                for start, _w in hooks:
                    start()

        body(*ins, *outs, *scr)
        if side_ops:
            @pl.when(last)
            def _():
                for _s, wait in hooks:
                    wait()

    res = pl.pallas_call(
        wrapped, name=name, grid=grid, in_specs=list(in_specs) + [ANY] * len(s_ins),
        out_specs=list(out_specs) + [ANY] * len(s_outs), out_shape=list(out_shape) + s_outs,
        scratch_shapes=list(scratch_shapes) + s_scr, input_output_aliases=aliases,
        compiler_params=_cparams(semantics),
    )(*operands, *s_ins)
    side_res, k = [], n_out
    for op in side_ops:
        side_res.append(list(res[k:k + len(op.out_shapes)]))
        k += len(op.out_shapes)
    return list(res[:n_out]), side_res


def _gather_chips_op(shards):
    n = len(shards)

    def make(ins, outs, scr):
        send_sems, recv_sems, local_sems = scr
        x, y, c, chips = _my_place()
        mej = 2 * x + y

        def remote(tn, k, slot):
            px, py = chips[k]
            return pltpu.make_async_remote_copy(
                src_ref=ins[tn], dst_ref=outs[tn].at[slot], send_sem=send_sems.at[tn, k],
                recv_sem=recv_sems.at[tn, k], device_id=(px, py, c), device_id_type=MESH)

        def local(tn):
            return pltpu.make_async_copy(ins[tn], outs[tn].at[mej], local_sems.at[tn])

        def start():
            for k in range(3):
                for tn in range(n):
                    remote(tn, k, mej).start()
            for tn in range(n):
                local(tn).start()

        def wait():
            for k, (px, py) in enumerate(chips):
                for tn in range(n):
                    remote(tn, k, 2 * px + py).wait_recv()
                    remote(tn, k, mej).wait_send()
            for tn in range(n):
                local(tn).wait()

        return start, wait

    return _SideOp(shards, [jax.ShapeDtypeStruct((N_CHIPS,) + a.shape, a.dtype) for a in shards],
                   [pltpu.SemaphoreType.DMA((n, 3)), pltpu.SemaphoreType.DMA((n, 3)), pltpu.SemaphoreType.DMA((n,))], make)


PA_CONV_B, PA_LNA_G, PA_LNA_B, PA_LNV_G, PA_LNV_B = 0, 1, 2, 3, 4
PD_GIN, PD_BIN, PD_G1, PD_B1, PD_G2, PD_B2 = 0, 1, 2, 3, 4, 5


def _row(ref, r):
    return ref[r:r + 1, :]


def _mixer_fwd(xin, pd, win_t, conv_w, pa, wst, bst, kt_all, v_all, w_out, l, t, side_ops=None):
    s = xin.shape[0]
    nt = s // t

    def body(x_ref, pd_ref, wint_ref, cw_ref, pa_ref, wst_ref, bst_ref, kt_ref, v_ref, wout_ref,
             xh_ref, rstd_ref, h_ref, ac_ref, cat_ref, mixed_ref, p_ref, cbuf, zbuf):
        i = pl.program_id(0)
        x = x_ref[...] * _row(pd_ref, PD_GIN) + _row(pd_ref, PD_BIN)
        h = _dot_nt(x.astype(BF16), wint_ref[...])
        h_ref[...] = h
        a1, a2 = h[:, 0:CONV_W], h[:, CONV_W:2 * CONV_W]
        hu, hv = h[:, 2 * CONV_W:2 * CONV_W + GMLP_W], h[:, 2 * CONV_W + GMLP_W:2 * CONV_W + 2 * GMLP_W]
        q = h[:, IN_W - XATTN_W:]

        @pl.when(i == 0)
        def _():
            cbuf[0:CONV_HALO, :] = jnp.zeros((CONV_HALO, CONV_W), F32)

        cbuf[CONV_HALO:CONV_HALO + t, :] = a1 * _sigmoid(a2)
        ac = jnp.zeros((t, CONV_W), F32) + _row(pa_ref, PA_CONV_B)
        for r in range(8):
            zr = jnp.zeros((t + 8, CONV_W), F32)
            for a in range(4):
                o = 8 * a + r
                if o < CONV_K:
                    k = CONV_K - 1 - o
                    zr = zr + cbuf[CONV_HALO - 8 - 8 * a:CONV_HALO - 8 - 8 * a + t + 8, :] * cw_ref[k:k + 1, :]
            if r == 0:
                ac = ac + zr[8:, :]
            else:
                zbuf[...] = zr
                ac = ac + zbuf[8 - r:8 - r + t, :]
        ac_ref[...] = ac
        cbuf[0:CONV_HALO, :] = cbuf[t:t + CONV_HALO, :]
        xh_a, _ = _ln_fwd(ac)
        an = xh_a * _row(pa_ref, PA_LNA_G) + _row(pa_ref, PA_LNA_B)
        a = an * _sigmoid(an)

        u = _gelu(hu)
        xh_v, _ = _ln_fwd(_gelu(hv))
        vn = xh_v * _row(pa_ref, PA_LNV_G) + _row(pa_ref, PA_LNV_B)
        _spatial_mix(vn.astype(BF16), wst_ref, bst_ref, mixed_ref, t)
        g = u * mixed_ref[...]

        p = _softmax_heads(_dot(q.astype(BF16), kt_ref[...]) * ATT_SCALE)
        p_ref[...] = p
        o = _dot(p.astype(BF16), v_ref[...])

        cat = jnp.concatenate([a, g, o], axis=1).astype(BF16)
        cat_ref[...] = cat
        z = ALPHA * x + _dot(cat, wout_ref[...])
        xh, rstd = _ln_fwd(z)
        xh_ref[...] = xh
        rstd_ref[...] = rstd

    tok = lambda w: pl.BlockSpec((t, w), lambda i: (i, 0))
    return _call_with_side(
        body, side_ops, name="mixer_fwd", grid=(nt,),
        in_specs=[tok(D_MODEL), _layer_spec((8, D_MODEL), l), _layer_spec((IN_W, D_MODEL), 0, resident=True),
                  _layer_spec((CONV_HALO, CONV_W), l), _layer_spec((8, CONV_W), l),
                  _layer_spec((3, 2 * CHUNK, CHUNK), l), _layer_spec((CHUNK, GMLP_W), l),
                  _const_spec((XATTN_W, XATTN_HEADS * N_MEM)), _const_spec((XATTN_HEADS * N_MEM, XATTN_W)),
                  _layer_spec((D_MODEL, D_MODEL), 0, resident=True)],
        out_specs=[tok(D_MODEL), tok(1), tok(IN_W), tok(CONV_W), tok(D_MODEL), tok(GMLP_W), tok(XATTN_HEADS * N_MEM)],
        out_shape=[jax.ShapeDtypeStruct((s, D_MODEL), F32), jax.ShapeDtypeStruct((s, 1), F32),
                   jax.ShapeDtypeStruct((s, IN_W), F32), jax.ShapeDtypeStruct((s, CONV_W), F32),
                   jax.ShapeDtypeStruct((s, D_MODEL), BF16), jax.ShapeDtypeStruct((s, GMLP_W), F32),
                   jax.ShapeDtypeStruct((s, XATTN_HEADS * N_MEM), F32)],
        scratch_shapes=[pltpu.VMEM((t + CONV_HALO, CONV_W), F32), pltpu.VMEM((t + 8, CONV_W), F32)],
        operands=(xin, pd, win_t, conv_w, pa, wst, bst, kt_all, v_all, w_out), semantics=("arbitrary",))


VD_LN_G, VD_LN_B, VD_LOSS = 0, 1, 2
VA_CONV_B, VA_LNA_G, VA_LNA_B, VA_LNV_G, VA_LNV_B = 0, 1, 2, 3, 4


def _mixer_bwd_d(gz, xh1, rstd1, h, ac, mixed, probs, pd, conv_w, pa, wstt, k_all, vt_all, w_out, win_t, l, t, side_ops=None):
    s = gz.shape[0]
    nt = s // t

    def body(gz_ref, xh_ref, rstd_ref, h_ref, ac_ref, mixed_ref, p_ref, pd_ref, cw_ref, pa_ref, wstt_ref,
             k_ref, vt_ref, wout_ref, wint_ref,
             dx_ref, dh_ref, dmix_ref, vd_ref, va_ref, dcw_ref, dws_ref, dbs_ref, dkt_ref, dv_ref,
             ebuf, dvnbuf, dbsacc, erbuf):
        i = pl.program_id(0)

        @pl.when(i == 0)
        def _():
            vd_ref[...] = jnp.zeros_like(vd_ref)
            va_ref[...] = jnp.zeros_like(va_ref)
            dcw_ref[...] = jnp.zeros_like(dcw_ref)
            dws_ref[...] = jnp.zeros_like(dws_ref)
            dbs_ref[...] = jnp.zeros_like(dbs_ref)
            dkt_ref[...] = jnp.zeros_like(dkt_ref)
            dv_ref[...] = jnp.zeros_like(dv_ref)
            dbsacc[...] = jnp.zeros_like(dbsacc)
            ebuf[t:t + CONV_HALO, :] = jnp.zeros((CONV_HALO, CONV_W), F32)

        gz_v = gz_ref[...]
        xh = xh_ref[...]
        vd_ref[VD_LN_G:VD_LN_G + 1, :] += _colsum(gz_v * xh)
        vd_ref[VD_LN_B:VD_LN_B + 1, :] += _colsum(gz_v)
        dz = _ln_bwd(gz_v * _row(pd_ref, PD_G1), xh, rstd_ref[...])
        dzb = dz.astype(BF16)
        dmix_ref[...] = dzb
        dcat = _dot_nt(dzb, wout_ref[...])
        d_a, d_g, d_o = dcat[:, 0:CONV_W], dcat[:, CONV_W:CONV_W + GMLP_W], dcat[:, CONV_W + GMLP_W:]

        h = h_ref[...]
        a1, a2 = h[:, 0:CONV_W], h[:, CONV_W:2 * CONV_W]
        hu, hv = h[:, 2 * CONV_W:2 * CONV_W + GMLP_W], h[:, 2 * CONV_W + GMLP_W:2 * CONV_W + 2 * GMLP_W]
        q = h[:, IN_W - XATTN_W:]

        xh_a, rstd_a = _ln_fwd(ac_ref[...])
        an = xh_a * _row(pa_ref, PA_LNA_G) + _row(pa_ref, PA_LNA_B)
        sig = _sigmoid(an)
        d_an = d_a * (sig * (1.0 + an * (1.0 - sig)))
        va_ref[VA_LNA_G:VA_LNA_G + 1, :] += _colsum(d_an * xh_a)
        va_ref[VA_LNA_B:VA_LNA_B + 1, :] += _colsum(d_an)
        dac = _ln_bwd(d_an * _row(pa_ref, PA_LNA_G), xh_a, rstd_a)
        va_ref[VA_CONV_B:VA_CONV_B + 1, :] += _colsum(dac)
        ebuf[0:t, :] = dac
        sg = _sigmoid(a2)
        glu = a1 * sg
        dglu = jnp.zeros((t, CONV_W), F32)
        for r in range(8):
            if r > 0:
                erbuf[...] = ebuf[r:r + t + 24, :]
            src = ebuf if r == 0 else erbuf
            for a in range(4):
                o = 8 * a + r
                if o < CONV_K:
                    k = CONV_K - 1 - o
                    ek = src[8 * a:8 * a + t, :]
                    dglu = dglu + ek * cw_ref[k:k + 1, :]
                    dcw_ref[k:k + 1, :] += _colsum(ek * glu)
        ebuf[t:t + CONV_HALO, :] = ebuf[0:CONV_HALO, :]
        da1 = dglu * sg
        da2 = dglu * a1 * sg * (1.0 - sg)

        u, du = _gelu_and_grad(hu)
        vg, dvg_dhv = _gelu_and_grad(hv)
        xh_v, rstd_v = _ln_fwd(vg)
        vn = xh_v * _row(pa_ref, PA_LNV_G) + _row(pa_ref, PA_LNV_B)
        vnb = vn.astype(BF16)
        dhu = d_g * mixed_ref[...] * du
        dm = d_g * u
        dmb = dm.astype(BF16)
        lo = _lane_lo((CHUNK, LANE))
        for n in range(t // CHUNK):
            rows = slice(n * CHUNK, (n + 1) * CHUNK)
            dbsacc[...] += dm[rows, :]
            for j in range(GMLP_W // LANE):
                cols = slice(j * LANE, (j + 1) * LANE)
                dm_blk = dmb[rows, cols]
                r = _dot(wstt_ref[j], dm_blk)
                dvnbuf[rows, cols] = jnp.where(lo, r[:CHUNK], r[CHUNK:])
                zero = jnp.zeros_like(dm_blk)
                st = jnp.concatenate([jnp.where(lo, dm_blk, zero), jnp.where(lo, zero, dm_blk)], axis=0)
                dws_ref[j] += _dot_nt(st, vnb[rows, cols])
        dvn = dvnbuf[...]
        va_ref[VA_LNV_G:VA_LNV_G + 1, :] += _colsum(dvn * xh_v)
        va_ref[VA_LNV_B:VA_LNV_B + 1, :] += _colsum(dvn)
        dhv = _ln_bwd(dvn * _row(pa_ref, PA_LNV_G), xh_v, rstd_v) * dvg_dhv

        qb = q.astype(BF16)
        p = p_ref[...]
        dob = d_o.astype(BF16)
        dp = _dot(dob, vt_ref[...])
        dss = []
        for hd in range(XATTN_HEADS):
            cs = slice(hd * N_MEM, (hd + 1) * N_MEM)
            ph, dph = p[:, cs], dp[:, cs]
            dss.append(ph * (dph - jnp.sum(ph * dph, axis=-1, keepdims=True)) * ATT_SCALE)
        dsb = jnp.concatenate(dss, axis=1).astype(BF16)
        dq = _dot(dsb, k_ref[...])
        dkt_ref[...] += _dot_tn(qb, dsb)
        dv_ref[...] += _dot_tn(p.astype(BF16), dob)

        dhb = jnp.concatenate([da1, da2, dhu, dhv, dq], axis=1).astype(BF16)
        dh_ref[...] = dhb
        dx_ref[...] = ALPHA * dz + _dot(dhb, wint_ref[...])

        @pl.when(i == nt - 1)
        def _():
            acc = dbsacc[...]
            head = lax.broadcasted_iota(jnp.int32, (CHUNK, GMLP_W), 1) // HEAD_DIM
            lane = lax.broadcasted_iota(jnp.int32, (CHUNK, LANE), 1)
            out = jnp.zeros((CHUNK, LANE), F32)
            for hd in range(GMLP_W // HEAD_DIM):
                sh = jnp.sum(jnp.where(head == hd, acc, 0.0), axis=1, keepdims=True)
                out = out + jnp.where(lane == hd, sh, 0.0)
            dbs_ref[...] = out

    rev = lambda w: pl.BlockSpec((t, w), lambda i: (nt - 1 - i, 0))
    out_shape = [
        jax.ShapeDtypeStruct((s, D_MODEL), F32), jax.ShapeDtypeStruct((s, IN_W), BF16),
        jax.ShapeDtypeStruct((s, D_MODEL), BF16),
        jax.ShapeDtypeStruct((8, D_MODEL), F32), jax.ShapeDtypeStruct((8, CONV_W), F32),
        jax.ShapeDtypeStruct((CONV_HALO, CONV_W), F32), jax.ShapeDtypeStruct((3, 2 * CHUNK, CHUNK), F32),
        jax.ShapeDtypeStruct((CHUNK, LANE), F32),
        jax.ShapeDtypeStruct((XATTN_W, XATTN_HEADS * N_MEM), F32), jax.ShapeDtypeStruct((XATTN_HEADS * N_MEM, XATTN_W), F32),
    ]
    out_specs = [rev(D_MODEL), rev(IN_W), rev(D_MODEL)] + [_const_spec(o.shape) for o in out_shape[3:]]
    return _call_with_side(
        body, side_ops, name="mixer_bwd_d", grid=(nt,),
        in_specs=[rev(D_MODEL), rev(D_MODEL), rev(1), rev(IN_W), rev(CONV_W), rev(GMLP_W), rev(XATTN_HEADS * N_MEM),
                  _layer_spec((8, D_MODEL), l), _layer_spec((CONV_HALO, CONV_W), l), _layer_spec((8, CONV_W), l),
                  _layer_spec((3, 2 * CHUNK, CHUNK), l),
                  _const_spec((XATTN_HEADS * N_MEM, XATTN_W)), _const_spec((XATTN_W, XATTN_HEADS * N_MEM)),
                  _layer_spec((D_MODEL, D_MODEL), 0, resident=True), _layer_spec((IN_W, D_MODEL), 0, resident=True)],
        out_specs=out_specs, out_shape=out_shape,
        scratch_shapes=[pltpu.VMEM((t + CONV_HALO, CONV_W), F32), pltpu.VMEM((t, GMLP_W), F32),
                        pltpu.VMEM((CHUNK, GMLP_W), F32), pltpu.VMEM((t + 24, CONV_W), F32)],
        operands=(gz, xh1, rstd1, h, ac, mixed, probs, pd, conv_w, pa, wstt, k_all, vt_all, w_out, win_t),
        semantics=("arbitrary",))


def _mixer_bwd_w(xin, pd, dh, cat, dmix, l, t, side_ops=None):
    s = xin.shape[0]
    nt = s // t

    def body(x_ref, pd_ref, dh_ref, cat_ref, dmix_ref, dwin_ref, dwout_ref):
        @pl.when(pl.program_id(0) == 0)
        def _():
            dwin_ref[...] = jnp.zeros_like(dwin_ref)
            dwout_ref[...] = jnp.zeros_like(dwout_ref)

        xb = (x_ref[...] * _row(pd_ref, PD_GIN) + _row(pd_ref, PD_BIN)).astype(BF16)
        dwin_ref[...] += _dot_tn(dh_ref[...], xb)
        dwout_ref[...] += _dot_tn(cat_ref[...], dmix_ref[...])

    tok = lambda w: pl.BlockSpec((t, w), lambda i: (i, 0))
    return _call_with_side(
        body, side_ops, name="mixer_bwd_w", grid=(nt,),
        in_specs=[tok(D_MODEL), _layer_spec((8, D_MODEL), l), tok(IN_W), tok(D_MODEL), tok(D_MODEL)],
        out_specs=[_layer_spec((IN_W, D_MODEL), 0, resident=True), _layer_spec((D_MODEL, D_MODEL), 0, resident=True)],
        out_shape=[jax.ShapeDtypeStruct((1, IN_W, D_MODEL), F32), jax.ShapeDtypeStruct((1, D_MODEL, D_MODEL), F32)],
        scratch_shapes=[], operands=(xin, pd, dh, cat, dmix), semantics=("arbitrary",))


PF_W0, PF_B = 0, 3


def _ffn_fwd(xh1, pd, wup_t, pf, w_d, l, t, side_ops=None):
    s = xh1.shape[0]
    nt = s // t

    def body(xh_ref, pd_ref, wg_ref, wv_ref, pf_ref, wd_ref, xh2_ref, rstd_ref, upg_ref, upv_ref, fbuf):
        i = pl.program_id(0)

        @pl.when(i == 0)
        def _():
            fbuf[0:FFN_HALO, :] = jnp.zeros((FFN_HALO, FF_P), F32)

        x1 = xh_ref[...] * _row(pd_ref, PD_G1) + _row(pd_ref, PD_B1)
        xb = x1.astype(BF16)
        y = jnp.zeros((t, D_MODEL), F32)
        for hf in range(2):
            cs = slice(hf * FF_H, (hf + 1) * FF_H)
            ug = _dot_nt(xb, wg_ref[cs, :])
            uv = _dot_nt(xb, wv_ref[cs, :])
            upg_ref[:, cs] = ug
            upv_ref[:, cs] = uv
            fbuf[FFN_HALO:FFN_HALO + t, cs] = ug
            gate = jnp.zeros((t, FF_H), F32) + pf_ref[PF_B:PF_B + 1, cs]
            for k in range(FFN_CONV_K):
                off = FFN_HALO - (FFN_CONV_K - 1) + k
                gate = gate + fbuf[off:off + t, cs] * pf_ref[PF_W0 + k:PF_W0 + k + 1, cs]
            fbuf[0:FFN_HALO, cs] = fbuf[t:t + FFN_HALO, cs]
            hm = gate * _sigmoid(gate) * uv
            y = y + _dot(hm.astype(BF16), wd_ref[cs, :])
        xh2, rstd = _ln_fwd(ALPHA * x1 + y)
        xh2_ref[...] = xh2
        rstd_ref[...] = rstd

    tok = lambda w: pl.BlockSpec((t, w), lambda i: (i, 0))
    return _call_with_side(
        body, side_ops, name="ffn_fwd", grid=(nt,),
        in_specs=[tok(D_MODEL), _layer_spec((8, D_MODEL), l),
                  _layer_spec((FF_P, D_MODEL), 0, 0, resident=True), _layer_spec((FF_P, D_MODEL), 0, 1, resident=True),
                  _layer_spec((8, FF_P), l), _layer_spec((FF_P, D_MODEL), 0, resident=True)],
        out_specs=[tok(D_MODEL), tok(1), tok(FF_P), tok(FF_P)],
        out_shape=[jax.ShapeDtypeStruct((s, D_MODEL), F32), jax.ShapeDtypeStruct((s, 1), F32),
                   jax.ShapeDtypeStruct((s, FF_P), F32), jax.ShapeDtypeStruct((s, FF_P), F32)],
        scratch_shapes=[pltpu.VMEM((t + FFN_HALO, FF_P), F32)],
        operands=(xh1, pd, wup_t, wup_t, pf, w_d), semantics=("arbitrary",))


VF_W0, VF_B = 0, 3


def _ffn_bwd_d(gz_or_target, xh2, rstd2, upg, upv, pd, pf, w_d, wup_t, l, t, last, side_ops=None):
    s = xh2.shape[0]
    nt = s // t
    hb = t // FFN_HALO

    def body(gz_ref, xh2_ref, rstd_ref, upg_ref, halo_ref, upv_ref, pd_ref, pf_ref, wd_ref, wg_ref, wv_ref,
             dx_ref, dy_ref, dug_ref, duv_ref, hm_ref, vd_ref, vf_ref, gbuf, ebuf, s1buf, s2buf):
        i = pl.program_id(0)
        first_tile = i == nt - 1

        @pl.when(i == 0)
        def _():
            vd_ref[...] = jnp.zeros_like(vd_ref)
            vf_ref[...] = jnp.zeros_like(vf_ref)
            ebuf[t:t + FFN_HALO, :] = jnp.zeros((FFN_HALO, FF_P), F32)

        xh2_v = xh2_ref[...]
        if last:
            diff = xh2_v * _row(pd_ref, PD_G2) + _row(pd_ref, PD_B2) - gz_ref[...]
            vd_ref[VD_LOSS:VD_LOSS + 1, :] += _colsum(diff * diff)
            gz_v = diff * (1.0 / D_MODEL)
        else:
            gz_v = gz_ref[...]
        vd_ref[VD_LN_G:VD_LN_G + 1, :] += _colsum(gz_v * xh2_v)
        vd_ref[VD_LN_B:VD_LN_B + 1, :] += _colsum(gz_v)
        dz = _ln_bwd(gz_v * _row(pd_ref, PD_G2), xh2_v, rstd_ref[...])
        dyb = dz.astype(BF16)
        dy_ref[...] = dyb
        dx = ALPHA * dz
        for hf in range(2):
            cs = slice(hf * FF_H, (hf + 1) * FF_H)
            ug = upg_ref[:, cs]
            uv = upv_ref[:, cs]
            halo = halo_ref[:, cs]
            gbuf[0:FFN_HALO, :] = jnp.where(first_tile, jnp.zeros_like(halo), halo)
            gbuf[FFN_HALO:FFN_HALO + t, :] = ug
            s1buf[...] = gbuf[FFN_HALO - 1:FFN_HALO - 1 + t, :]
            s2buf[...] = gbuf[FFN_HALO - 2:FFN_HALO - 2 + t, :]
            ug1 = s1buf[...]
            ug2 = s2buf[...]
            gate = (pf_ref[PF_B:PF_B + 1, cs] + ug2 * pf_ref[PF_W0:PF_W0 + 1, cs] + ug1 * pf_ref[PF_W0 + 1:PF_W0 + 2, cs]
                    + ug * pf_ref[PF_W0 + 2:PF_W0 + 3, cs])
            sig = _sigmoid(gate)
            sl = gate * sig
            hm_ref[:, cs] = sl * uv
            dhm = _dot_nt(dyb, wd_ref[cs, :])
            duv = dhm * sl
            dgate = dhm * uv * (sig * (1.0 + gate * (1.0 - sig)))
            vf_ref[VF_B:VF_B + 1, cs] += _colsum(dgate)
            vf_ref[VF_W0:VF_W0 + 1, cs] += _colsum(dgate * ug2)
            vf_ref[VF_W0 + 1:VF_W0 + 2, cs] += _colsum(dgate * ug1)
            vf_ref[VF_W0 + 2:VF_W0 + 3, cs] += _colsum(dgate * ug)
            ebuf[0:t, cs] = dgate
            dug = (ebuf[2:2 + t, cs] * pf_ref[PF_W0:PF_W0 + 1, cs] + ebuf[1:1 + t, cs] * pf_ref[PF_W0 + 1:PF_W0 + 2, cs]
                   + dgate * pf_ref[PF_W0 + 2:PF_W0 + 3, cs])
            ebuf[t:t + FFN_HALO, cs] = ebuf[0:FFN_HALO, cs]
            dugb = dug.astype(BF16)
            duvb = duv.astype(BF16)
            dug_ref[:, cs] = dugb
            duv_ref[:, cs] = duvb
            dx = dx + _dot(dugb, wg_ref[cs, :]) + _dot(duvb, wv_ref[cs, :])
        dx_ref[...] = dx

        if last:
            @pl.when(i == nt - 1)
            def _():
                tot = jnp.sum(vd_ref[VD_LOSS:VD_LOSS + 1, :], axis=1, keepdims=True)
                vd_ref[VD_LOSS:VD_LOSS + 1, :] = jnp.zeros((1, D_MODEL), F32) + tot

    rev = lambda w: pl.BlockSpec((t, w), lambda i: (nt - 1 - i, 0))
    halo_spec = pl.BlockSpec((FFN_HALO, FF_P), lambda i: (jnp.maximum((nt - 1 - i) * hb - 1, 0), 0))
    out_shape = [jax.ShapeDtypeStruct((s, D_MODEL), F32), jax.ShapeDtypeStruct((s, D_MODEL), BF16),
                 jax.ShapeDtypeStruct((s, FF_P), BF16), jax.ShapeDtypeStruct((s, FF_P), BF16),
                 jax.ShapeDtypeStruct((s, FF_P), F32),
                 jax.ShapeDtypeStruct((8, D_MODEL), F32), jax.ShapeDtypeStruct((8, FF_P), F32)]
    return _call_with_side(
        body, side_ops, name="ffn_bwd_d_last" if last else "ffn_bwd_d", grid=(nt,),
        in_specs=[rev(D_MODEL), rev(D_MODEL), rev(1), rev(FF_P), halo_spec, rev(FF_P),
                  _layer_spec((8, D_MODEL), l), _layer_spec((8, FF_P), l),
                  _layer_spec((FF_P, D_MODEL), 0, resident=True),
                  _layer_spec((FF_P, D_MODEL), 0, 0, resident=True), _layer_spec((FF_P, D_MODEL), 0, 1, resident=True)],
        out_specs=[rev(D_MODEL), rev(D_MODEL), rev(FF_P), rev(FF_P), rev(FF_P),
                   _const_spec((8, D_MODEL)), _const_spec((8, FF_P))],
        out_shape=out_shape,
        scratch_shapes=[pltpu.VMEM((t + FFN_HALO, FF_H), F32), pltpu.VMEM((t + FFN_HALO, FF_P), F32),
                        pltpu.VMEM((t, FF_H), F32), pltpu.VMEM((t, FF_H), F32)],
        operands=(gz_or_target, xh2, rstd2, upg, upg, upv, pd, pf, w_d, wup_t, wup_t), semantics=("arbitrary",))


def _ffn_bwd_w(xh1, pd, dy, dug, duv, hm, l, t, side_ops=None):
    s = xh1.shape[0]
    nt = s // t

    def body(xh_ref, pd_ref, dy_ref, dug_ref, duv_ref, hm_ref, dwup_ref, dwd_ref):
        @pl.when(pl.program_id(1) == 0)
        def _():
            dwup_ref[...] = jnp.zeros_like(dwup_ref)
            dwd_ref[...] = jnp.zeros_like(dwd_ref)

        xb = (xh_ref[...] * _row(pd_ref, PD_G1) + _row(pd_ref, PD_B1)).astype(BF16)
        dwup_ref[0] += _dot_tn(dug_ref[...], xb)
        dwup_ref[1] += _dot_tn(duv_ref[...], xb)
        dwd_ref[...] += _dot_tn(hm_ref[...].astype(BF16), dy_ref[...])

    tok = lambda w: pl.BlockSpec((t, w), lambda c, i: (i, 0))
    half = pl.BlockSpec((t, FF_H), lambda c, i: (i, c))
    return _call_with_side(
        body, side_ops, name="ffn_bwd_w", grid=(2, nt),
        in_specs=[tok(D_MODEL), pl.BlockSpec((None, 8, D_MODEL), lambda c, i: (l, 0, 0)), tok(D_MODEL), half, half, half],
        out_specs=[pl.BlockSpec((None, 2, FF_H, D_MODEL), lambda c, i: (0, 0, c, 0), pipeline_mode=pl.Buffered(1)),
                   pl.BlockSpec((None, FF_H, D_MODEL), lambda c, i: (0, c, 0), pipeline_mode=pl.Buffered(1))],
        out_shape=[jax.ShapeDtypeStruct((1, 2, FF_P, D_MODEL), F32), jax.ShapeDtypeStruct((1, FF_P, D_MODEL), F32)],
        scratch_shapes=[], operands=(xh1, pd, dy, dug, duv, hm), semantics=("arbitrary", "arbitrary"))


def _adamw_math(w, g, m, v):
    nm = ADAM_B1 * m + (1.0 - ADAM_B1) * g
    nv = ADAM_B2 * v + (1.0 - ADAM_B2) * (g * g)
    m_hat = nm / (1.0 - ADAM_B1 ** ADAM_STEP)
    v_hat = nv / (1.0 - ADAM_B2 ** ADAM_STEP)
    return -ADAM_LR * (m_hat / (jnp.sqrt(v_hat) + ADAM_EPS) + ADAM_WD * w), nm, nv


def _adamw_layers(w, gs, m, v, name):
    shp = w.shape
    _, rows, cols = shp
    tr = _row_tile(rows, cols * 4, mult=8)
    nb = rows // tr

    def body(w_ref, g0_ref, g1_ref, m_ref, v_ref, g_ref, d_ref, nm_ref, nv_ref):
        g = jnp.where(pl.program_id(0) == 0, g0_ref[...], g1_ref[...])
        g_ref[...] = g
        d_ref[...], nm_ref[...], nv_ref[...] = _adamw_math(w_ref[...], g, m_ref[...], v_ref[...])

    stacked = pl.BlockSpec((tr, cols), lambda l, i: (l * nb + i, 0))
    single = pl.BlockSpec((tr, cols), lambda l, i: (i, 0))
    sh = jax.ShapeDtypeStruct((DEPTH * rows, cols), F32)
    flat = lambda a: a.reshape(DEPTH * rows, cols)
    outs = pl.pallas_call(body, name=name, grid=(DEPTH, nb), in_specs=[stacked, single, single, stacked, stacked],
                          out_specs=[stacked] * 4, out_shape=[sh] * 4,
                          compiler_params=_cparams(("arbitrary", "arbitrary")))(flat(w), gs[0], gs[1], flat(m), flat(v))
    return [o.reshape(shp) for o in outs]


def _adamw_small(ws, gs, ms, vs):
    n = len(ws)

    def body(*refs):
        w_refs, g_refs, m_refs, v_refs = refs[:n], refs[n:2 * n], refs[2 * n:3 * n], refs[3 * n:4 * n]
        d_refs, nm_refs, nv_refs = refs[4 * n:5 * n], refs[5 * n:6 * n], refs[6 * n:7 * n]
        for k in range(n):
            d_refs[k][...], nm_refs[k][...], nv_refs[k][...] = _adamw_math(w_refs[k][...], g_refs[k][...], m_refs[k][...],
                                                                             v_refs[k][...])

    shapes = [jax.ShapeDtypeStruct(w.shape, F32) for w in ws]
    outs = pl.pallas_call(body, name="adamw_small", out_shape=shapes * 3, compiler_params=_cparams())(*ws, *gs, *ms, *vs)
    return outs[:n], outs[n:2 * n], outs[2 * n:]


def _all_gather_chips(tensors, name):
    n = len(tensors)
    halves = [a.shape[1] // 2 for a in tensors]

    def body(*refs):
        x_refs, out_refs = refs[:n], refs[n:2 * n]
        send_sems, recv_sems, local_sems = refs[2 * n:]
        x, y, c, chips = _my_place()
        me, sibling, mej = (x, y, c), (x, y, 1 - c), 2 * x + y

        def rows(tn, px, py, pc):
            return out_refs[tn].at[:, 2 * px + py, pl.ds(pc * halves[tn], halves[tn]), :]

        def copy(tn, k, block, to, src=None):
            return pltpu.make_async_remote_copy(
                src_ref=rows(tn, *block) if src is None else src, dst_ref=rows(tn, *block),
                send_sem=send_sems.at[tn, k], recv_sem=recv_sems.at[tn, k], device_id=to, device_id_type=MESH)

        mine_src = [x_refs[tn].at[:, pl.ds(c * halves[tn], halves[tn]), :] for tn in range(n)]
        mine = [pltpu.make_async_copy(mine_src[tn], rows(tn, *me), local_sems.at[tn]) for tn in range(n)]
        first = []
        for j, chip in enumerate(chips):
            first += [copy(tn, 1 + j, me, (*chip, c), src=mine_src[tn]) for tn in range(n)]
        first += [copy(tn, 0, me, sibling, src=mine_src[tn]) for tn in range(n)]
        for cp in first + mine:
            cp.start()
        passed = []
        for j, chip in enumerate(chips):
            for tn in range(n):
                copy(tn, 1 + j, (*chip, c), me).wait_recv()
                fwd = copy(tn, 4 + j, (*chip, c), sibling)
                fwd.start()
                passed.append(fwd)
        for tn in range(n):
            copy(tn, 0, sibling, me).wait_recv()
            for j, chip in enumerate(chips):
                copy(tn, 4 + j, (*chip, 1 - c), me).wait_recv()
        for cp in first + passed:
            cp.wait_send()
        for cp in mine:
            cp.wait()

    return pl.pallas_call(
        body, name=name,
        out_shape=[jax.ShapeDtypeStruct((a.shape[0], N_CHIPS) + a.shape[1:], a.dtype) for a in tensors],
        in_specs=[ANY] * n, out_specs=[ANY] * n,
        scratch_shapes=[pltpu.SemaphoreType.DMA((n, 7)), pltpu.SemaphoreType.DMA((n, 7)), pltpu.SemaphoreType.DMA((n,))],
    )(*tensors)


def _swap_op(g5s):
    n = len(g5s)

    def make(ins, outs, scr):
        send_sems, recv_sems = scr
        x, y, c, _ = _my_place()

        def copies():
            return [pltpu.make_async_remote_copy(
                src_ref=ins[tn].at[:, :, 1 - c], dst_ref=outs[tn], send_sem=send_sems.at[tn], recv_sem=recv_sems.at[tn],
                device_id=(x, y, 1 - c), device_id_type=MESH) for tn in range(n)]

        def start():
            for cp in copies():
                cp.start()

        def wait():
            for cp in copies():
                cp.wait()

        return start, wait

    return _SideOp(g5s, [jax.ShapeDtypeStruct(g.shape[:2] + g.shape[3:], g.dtype) for g in g5s],
                   [pltpu.SemaphoreType.DMA((n,)), pltpu.SemaphoreType.DMA((n,))], make)


def _scatter_op(parts):
    n = len(parts)

    def make(ins, outs, scr):
        send_sems, recv_sems = scr
        x, y, c, chips = _my_place()

        def copies():
            return [pltpu.make_async_remote_copy(
                src_ref=ins[tn].at[:, 2 * px + py], dst_ref=outs[tn].at[:, k],
                send_sem=send_sems.at[tn, k], recv_sem=recv_sems.at[tn, k],
                device_id=(px, py, c), device_id_type=MESH) for k, (px, py) in enumerate(chips) for tn in range(n)]

        def start():
            for cp in copies():
                cp.start()

        def wait():
            for cp in copies():
                cp.wait()

        return start, wait

    return _SideOp(parts, [jax.ShapeDtypeStruct((p.shape[0], 3) + p.shape[2:], p.dtype) for p in parts],
                   [pltpu.SemaphoreType.DMA((n, 3)), pltpu.SemaphoreType.DMA((n, 3))], make)


def _sgather_op(fs):
    n = len(fs)

    def make(ins, outs, scr):
        send_sems, recv_sems = scr
        x, y, c, _ = _my_place()

        def copy(tn, dst_half):
            return pltpu.make_async_remote_copy(
                src_ref=outs[tn].at[:, c], dst_ref=outs[tn].at[:, dst_half], send_sem=send_sems.at[tn],
                recv_sem=recv_sems.at[tn], device_id=(x, y, 1 - c), device_id_type=MESH)

        def start():
            for tn in range(n):
                copy(tn, c).start()

        def wait():
            for tn in range(n):
                copy(tn, 1 - c).wait_recv()
                copy(tn, c).wait_send()

        return start, wait

    return _SideOp(fs, [jax.ShapeDtypeStruct(f.shape, f.dtype) for f in fs],
                   [pltpu.SemaphoreType.DMA((n,)), pltpu.SemaphoreType.DMA((n,))], make, aliases={tn: tn for tn in range(n)})


def _reduce_group_in_vmem(g5s, parts2, slots2):
    n, n2 = len(g5s), len(parts2)
    dims = [g.shape[3:] for g in g5s]
    dims2 = [p.shape[2:] for p in parts2]

    def body(*refs):
        g_refs, p2_refs, sl2_refs = refs[:n], refs[n:n + n2], refs[n + n2:n + 2 * n2]
        out_refs = refs[n + 2 * n2:2 * n + 3 * n2]
        scr = refs[2 * n + 3 * n2:]
        own, recv, part, slots = (scr[k * n:(k + 1) * n] for k in range(4))
        scr = scr[4 * n:]
        outv, own2, got2 = scr[:n + n2], scr[n + n2:n + 2 * n2], scr[n + 2 * n2:n + 3 * n2]
        s1, r1, s2, r2, s3, r3, lsem, lsem2 = scr[n + 3 * n2:]
        x, y, c, chips = _my_place()
        mej, sibling = 2 * x + y, (x, y, 1 - c)
        loads2 = ([pltpu.make_async_copy(p2_refs[t].at[0, pl.ds(mej, 1)], own2[t], lsem2.at[t, 0]) for t in range(n2)]
                  + [pltpu.make_async_copy(sl2_refs[t].at[0], got2[t], lsem2.at[t, 1]) for t in range(n2)])
        for cp in loads2:
            cp.start()

        loads = [pltpu.make_async_copy(g_refs[t].at[0, :, c], own[t], lsem.at[t]) for t in range(n)]
        swaps = [pltpu.make_async_remote_copy(src_ref=g_refs[t].at[0, :, 1 - c], dst_ref=recv[t], send_sem=s1.at[t],
                                              recv_sem=r1.at[t], device_id=sibling, device_id_type=MESH) for t in range(n)]
        for cp in loads + swaps:
            cp.start()
        for cp in loads + swaps:
            cp.wait()
        sends = []
        for t in range(n):
            part[t][...] = (own[t][...] + recv[t][...]).astype(BF16)
            mine = [pltpu.make_async_remote_copy(src_ref=part[t].at[2 * px + py], dst_ref=slots[t].at[k], send_sem=s2.at[t, k],
                                                 recv_sem=r2.at[t, k], device_id=(px, py, c), device_id_type=MESH)
                    for k, (px, py) in enumerate(chips)]
            for cp in mine:
                cp.start()
            sends += mine
        for cp in sends:
            cp.wait()
        for t in range(n):
            acc = part[t][pl.ds(mej, 1)].astype(F32)
            for k in range(3):
                acc = acc + slots[t][k:k + 1].astype(F32)
            outv[t][pl.ds(c, 1)] = acc
        for cp in loads2:
            cp.wait()
        for t in range(n2):
            acc = own2[t][...].astype(F32)
            for k in range(3):
                acc = acc + got2[t][k:k + 1].astype(F32)
            outv[n + t][pl.ds(c, 1)] = acc

        def back(t, half):
            return pltpu.make_async_remote_copy(src_ref=outv[t].at[c], dst_ref=outv[t].at[half], send_sem=s3.at[t],
                                                recv_sem=r3.at[t], device_id=sibling, device_id_type=MESH)

        for t in range(n + n2):
            back(t, c).start()
        for t in range(n + n2):
            back(t, 1 - c).wait_recv()
            back(t, c).wait_send()
        for t in range(n + n2):
            pltpu.sync_copy(outv[t], out_refs[t].at[0])

    vm = lambda shape, dt: pltpu.VMEM(shape, dt)
    dma = pltpu.SemaphoreType.DMA
    scratch = ([vm((N_CHIPS,) + d, F32) for d in dims] + [vm((N_CHIPS,) + d, F32) for d in dims]
               + [vm((N_CHIPS,) + d, BF16) for d in dims] + [vm((3,) + d, BF16) for d in dims]
               + [vm((2,) + d, F32) for d in dims + dims2] + [vm((1,) + d, BF16) for d in dims2] + [vm((3,) + d, BF16) for d in dims2]
               + [dma((n,)), dma((n,)), dma((n, 3)), dma((n, 3)), dma((n + n2,)), dma((n + n2,)), dma((n,)), dma((n2, 2))])
    nio = n + 2 * n2
    return pl.pallas_call(body, name="rs_tail_fused", in_specs=[ANY] * nio, out_specs=[ANY] * (n + n2),
                          out_shape=[jax.ShapeDtypeStruct((1, 2) + d, F32) for d in dims + dims2], scratch_shapes=scratch,
                          compiler_params=_cparams())(*g5s, *parts2, *slots2)


def _gather_devices_op(xs):
    def make(ins, outs, scr):
        send_sems, recv_sems, local_sem = scr
        x, y, c, chips = _my_place()
        peers = [(x, y, 1 - c)] + [(px, py, pc) for (px, py) in chips for pc in (c, 1 - c)]
        me = 4 * x + 2 * y + c

        def copy(k, slot):
            return pltpu.make_async_remote_copy(
                src_ref=ins[0], dst_ref=outs[0].at[slot], send_sem=send_sems.at[k], recv_sem=recv_sems.at[k],
                device_id=peers[k], device_id_type=MESH)

        def local():
            return pltpu.make_async_copy(ins[0], outs[0].at[me], local_sem)

        def start():
            for k in range(7):
                copy(k, me).start()
            local().start()

        def wait():
            for k, (px, py, pc) in enumerate(peers):
                copy(k, 4 * px + 2 * py + pc).wait_recv()
                copy(k, me).wait_send()
            local().wait()

        return start, wait

    return _SideOp([xs], [jax.ShapeDtypeStruct((8,) + xs.shape, xs.dtype)],
                   [pltpu.SemaphoreType.DMA((7,)), pltpu.SemaphoreType.DMA((7,)), pltpu.SemaphoreType.DMA], make)


def _add_halves(gs, recvs, place):
    n = len(gs)

    def body(place_ref, *refs):
        g_refs, r_refs, o_refs = refs[:n], refs[n:2 * n], refs[2 * n:]
        for tn in range(n):
            o_refs[tn][...] = (g_refs[tn][...] + r_refs[tn][...]).astype(BF16)

    def gspec(g):
        return pl.BlockSpec((None, None, None) + g.shape[3:], lambda l, j, p: (l, j, p[1], 0, 0))

    def rspec(r):
        return pl.BlockSpec((None, None) + r.shape[2:], lambda l, j, p: (l, j, 0, 0))

    grid_spec = pltpu.PrefetchScalarGridSpec(
        num_scalar_prefetch=1, grid=(gs[0].shape[0], N_CHIPS),
        in_specs=[gspec(g) for g in gs] + [rspec(r) for r in recvs], out_specs=[rspec(r) for r in recvs])
    return pl.pallas_call(body, name="rs_add", grid_spec=grid_spec,
                          out_shape=[jax.ShapeDtypeStruct(r.shape, BF16) for r in recvs],
                          compiler_params=_cparams(("arbitrary", "arbitrary")))(place, *gs, *recvs)


def _sum_slots(parts, slots, place):
    n = len(parts)

    def body(place_ref, *refs):
        p_refs, s_refs, o_refs = refs[:n], refs[n:2 * n], refs[2 * n:]
        for tn in range(n):
            acc = p_refs[tn][...].astype(F32)
            for k in range(3):
                acc = acc + s_refs[tn][k].astype(F32)
            o_refs[tn][...] = acc

    def pspec(p):
        return pl.BlockSpec((None, None) + p.shape[2:], lambda l, pl_: (l, pl_[0], 0, 0))

    def sspec(sl):
        return pl.BlockSpec((None,) + sl.shape[1:], lambda l, pl_: (l, 0, 0, 0))

    def ospec(p):
        return pl.BlockSpec((None, None) + p.shape[2:], lambda l, pl_: (l, pl_[1], 0, 0))

    grid_spec = pltpu.PrefetchScalarGridSpec(
        num_scalar_prefetch=1, grid=(parts[0].shape[0],),
        in_specs=[pspec(p) for p in parts] + [sspec(sl) for sl in slots], out_specs=[ospec(p) for p in parts])
    return pl.pallas_call(body, name="rs_sum", grid_spec=grid_spec,
                          out_shape=[jax.ShapeDtypeStruct((p.shape[0], 2) + p.shape[2:], F32) for p in parts],
                          compiler_params=_cparams(("arbitrary",)))(place, *parts, *slots)


def _sum_devices(gathered):
    m_per = gathered[0].shape[1]

    def body(*refs):
        o_ref = refs[-1]
        for l, g_ref in enumerate(refs[:-1]):
            acc = g_ref[0]
            for d in range(1, 8):
                acc = acc + g_ref[d]
            o_ref[l * m_per:(l + 1) * m_per, :] = acc

    return pl.pallas_call(body, name="small_sum", out_shape=jax.ShapeDtypeStruct((len(gathered) * m_per, LANE), F32),
                          compiler_params=_cparams())(*gathered)


def _pad_ff_cols(a):
    lead = a.shape[:-1]
    n = a.shape[-1] // FF_Q
    a = a.reshape(*lead, n, FF_Q)
    a = jnp.pad(a, [(0, 0)] * len(lead) + [(0, 0), (0, FF_QP - FF_Q)])
    return a.reshape(*lead, n * FF_QP)


def _unpad_ff_cols(a):
    lead = a.shape[:-1]
    n = a.shape[-1] // FF_QP
    return a.reshape(*lead, n, FF_QP)[..., :FF_Q].reshape(*lead, n * FF_Q)


def _pack_small(parts):
    flat = jnp.concatenate([p.reshape(-1) for p in parts])
    return flat.reshape(-1, LANE)


SMALL_SHAPES = [("conv_a_w", (CONV_K, CONV_W)), ("conv_a_b", (CONV_W,)), ("ln_a_g", (CONV_W,)), ("ln_a_b", (CONV_W,)),
                ("ln_v_g", (GMLP_W,)), ("ln_v_b", (GMLP_W,)), ("w_s", (6, CHUNK, CHUNK)), ("b_s", (6, CHUNK)),
                ("ln1_g", (D_MODEL,)), ("ln1_b", (D_MODEL,)), ("conv_f_w", (FFN_CONV_K, D_FF)), ("conv_f_b", (D_FF,)),
                ("ln2_g", (D_MODEL,)), ("ln2_b", (D_MODEL,))]
SMALL_FLOATS = 128000


def _unpack_small(flat2d):
    flat = flat2d.reshape(DEPTH, -1)
    out, o = {}, 0
    for name, shp in SMALL_SHAPES:
        n = 1
        for d in shp:
            n *= d
        out[name] = flat[:, o:o + n].reshape((DEPTH,) + shp)
        o += n
    return out


def _rows8(rows):
    blk = jnp.stack(rows, axis=1)
    return jnp.pad(blk, ((0, 0), (0, 8 - len(rows)), (0, 0)))


def kernel(x, mem, w_in, conv_a_w, conv_a_b, ln_a_g, ln_a_b, ln_v_g, ln_v_b, w_s, b_s, w_mk, w_mv, w_out, ln1_g, ln1_b, w_up, conv_f_w, conv_f_b, w_down, ln2_g, ln2_b, loss_target, m_w_in, m_conv_a_w, m_conv_a_b, m_ln_a_g, m_ln_a_b, m_ln_v_g, m_ln_v_b, m_w_s, m_b_s, m_w_mk, m_w_mv, m_w_out, m_ln1_g, m_ln1_b, m_w_up, m_conv_f_w, m_conv_f_b, m_w_down, m_ln2_g, m_ln2_b, v_w_in, v_conv_a_w, v_conv_a_b, v_ln_a_g, v_ln_a_b, v_ln_v_g, v_ln_v_b, v_w_s, v_b_s, v_w_mk, v_w_mv, v_w_out, v_ln1_g, v_ln1_b, v_w_up, v_conv_f_w, v_conv_f_b, v_w_down, v_ln2_g, v_ln2_b):
    seq = x.shape[1]
    t_fwd = min(512, seq)
    t_bwd = min(256, seq)
    t_wg = min(1024, seq)
    chip = 2 * lax.axis_index("x") + lax.axis_index("y")
    core = lax.axis_index("c")
    place = jnp.stack([chip, core]).astype(jnp.int32)
    x0 = x[0]
    mem0 = mem[0]
    target = loss_target[0]

    sh_in = w_in.transpose(0, 2, 1).astype(BF16)
    sh_mk, sh_mv, sh_out = w_mk.astype(BF16), w_mv.astype(BF16), w_out.astype(BF16)
    sh_up = _pad_ff_cols(w_up).transpose(0, 2, 1).astype(BF16)
    sh_dn = jnp.pad(w_down, ((0, 0), (0, FF_QP - FF_Q), (0, 0))).astype(BF16)

    def mixer_weights(g_in, g_mk, g_mv, g_out):
        return dict(win_t=g_in.reshape(1, IN_W, D_MODEL), wmk=g_mk.reshape(1, D_MODEL, XATTN_W),
                    wmv=g_mv.reshape(1, D_MODEL, XATTN_W), wout=g_out.reshape(1, D_MODEL, D_MODEL))

    def ffn_weights(g_up, g_dn):
        return dict(wup_t=g_up.reshape(1, 2, FF_P, D_MODEL), wdown=g_dn.reshape(1, FF_P, D_MODEL))

    n_ca = conv_a_w.size
    small_w = _pack_small([conv_a_w, conv_f_w, jnp.zeros((2 * 80 * LANE - n_ca - conv_f_w.size,), F32)])[None]
    *g_mixer0, small_g = _all_gather_chips([sh_in[:1], sh_mk[:1], sh_mv[:1], sh_out[:1], small_w], "ag_mixer0")
    wts = [mixer_weights(*g_mixer0), None]
    small_g = small_g.reshape(N_CHIPS, -1)
    conv_a_full = small_g[:, :n_ca].reshape(N_CHIPS, DEPTH, CONV_K, CONV_W // 4).transpose(1, 2, 0, 3).reshape(DEPTH, CONV_K, CONV_W)
    conv_f_full = small_g[:, n_ca:n_ca + conv_f_w.size].reshape(N_CHIPS, DEPTH, FFN_CONV_K, FF_Q).transpose(1, 2, 0, 3).reshape(DEPTH, FFN_CONV_K, D_FF)

    tril = jnp.tril(jnp.ones((CHUNK, CHUNK), dtype=bool))
    ws_m = jnp.where(tril, w_s, 0.0)
    wst = ws_m.reshape(DEPTH, 3, 2 * CHUNK, CHUNK).astype(BF16)
    wstt = ws_m.transpose(0, 1, 3, 2).reshape(DEPTH, 3, 2 * CHUNK, CHUNK).astype(BF16)
    bst = jnp.repeat(b_s.transpose(0, 2, 1), HEAD_DIM, axis=2)
    conv_w = jnp.pad(conv_a_full, ((0, 0), (0, CONV_HALO - CONV_K), (0, 0)))
    pa = _rows8([conv_a_b, ln_a_g, ln_a_b, ln_v_g, ln_v_b])
    gin = jnp.concatenate([jnp.ones((1, D_MODEL), F32), ln2_g[:DEPTH - 1]], axis=0)
    bin_ = jnp.concatenate([jnp.zeros((1, D_MODEL), F32), ln2_b[:DEPTH - 1]], axis=0)
    pd = _rows8([gin, bin_, ln1_g, ln1_b, ln2_g, ln2_b])
    pf = jnp.concatenate([_pad_ff_cols(conv_f_full), _pad_ff_cols(conv_f_b)[:, None, :],
                          jnp.zeros((DEPTH, 8 - FFN_CONV_K - 1, FF_P), F32)], axis=1)

    acts = []
    xin = x0
    for l in range(DEPTH):
        w = wts[l]
        kt_all, k_all, v_all, vt_all = _kv_fwd(mem0, w["wmk"], w["wmv"])
        ops = [_gather_chips_op([sh_up[0], sh_dn[0]])] if l == 0 else None
        (xh1, rstd1, h, ac, cat, mixed, probs), side = _mixer_fwd(xin, pd, w["win_t"], conv_w, pa, wst, bst, kt_all, v_all, w["wout"],
                                                    l, t_fwd, ops)
        if l == 0:
            w.update(ffn_weights(*side[0]))
        ops = [_gather_chips_op([sh_in[1], sh_mk[1], sh_mv[1], sh_out[1], sh_up[1], sh_dn[1]])] if l == 0 else None
        (xh2, rstd2, upg, upv), side = _ffn_fwd(xh1, pd, w["wup_t"], pf, w["wdown"], l, t_fwd, ops)
        if l == 0:
            wts[1] = {**mixer_weights(*side[0][:4]), **ffn_weights(*side[0][4:])}
        acts.append(dict(xin=xin, k_all=k_all, vt_all=vt_all, xh1=xh1, rstd1=rstd1, h=h, ac=ac, cat=cat, mixed=mixed, probs=probs,
                         xh2=xh2, rstd2=rstd2, upg=upg, upv=upv))
        xin = xh2

    assert DEPTH == 2

    def halves_view(gs):
        return [g.reshape(1, N_CHIPS, 2, g.shape[1] // (2 * N_CHIPS), g.shape[2]) for g in gs]

    small = [None] * DEPTH
    small_packed = [None] * DEPTH
    small_gathered = [None] * DEPTH
    red_layers = [None] * DEPTH
    gz = target
    g5_prev = None
    for l in reversed(range(DEPTH)):
        a, w = acts[l], wts[l]
        last = l == DEPTH - 1
        (dx1, dy, dug, duv, hm, vd2, vf), side = _ffn_bwd_d(
            gz, a["xh2"], a["rstd2"], a["upg"], a["upv"], pd, pf, w["wdown"], w["wup_t"], l, t_bwd, last,
            [_swap_op(g5_prev), _gather_devices_op(small_packed[l + 1])] if g5_prev else None)
        if g5_prev:
            small_gathered[l + 1] = side[1][0]
        parts_prev = _add_halves(g5_prev, side[0], place) if g5_prev else None
        (gw_up_t, gw_down), side = _ffn_bwd_w(a["xh1"], pd, dy, dug, duv, hm, l, t_wg,
                                              [_scatter_op(parts_prev)] if g5_prev else None)
        halves_prev = _sum_slots(parts_prev, side[0], place) if g5_prev else None
        g5_ffn = halves_view([gw_up_t.reshape(1, 2 * FF_P, D_MODEL), gw_down])
        ops = ([_sgather_op(halves_prev)] if g5_prev else []) + ([_swap_op(g5_ffn)] if l == 0 else [])
        (dx0, dh, dmix, vd1, va, dcw, dws, dbs, dkt, dv), side = _mixer_bwd_d(
            dx1, a["xh1"], a["rstd1"], a["h"], a["ac"], a["mixed"], a["probs"], pd, conv_w, pa, wstt,
            a["k_all"], a["vt_all"], w["wout"], w["win_t"], l, t_fwd, ops)
        if g5_prev:
            red_layers[l + 1] = side[0]
        parts_ffn = _add_halves(g5_ffn, side[-1], place) if l == 0 else None
        dws6 = jnp.where(tril, dws.reshape(6, CHUNK, CHUNK), 0.0)
        small[l] = [dcw[:CONV_K], va[VA_CONV_B], va[VA_LNA_G], va[VA_LNA_B], va[VA_LNV_G], va[VA_LNV_B], dws6,
                    dbs[:, :6].T, vd1[VD_LN_G], vd1[VD_LN_B],
                    _unpad_ff_cols(vf[VF_W0:VF_W0 + FFN_CONV_K]), _unpad_ff_cols(vf[VF_B]),
                    vd2[VD_LN_G], vd2[VD_LN_B]]
        small[l].append(vd2[VD_LOSS] if last else jnp.zeros((D_MODEL,), F32))
        small_packed[l] = _pack_small(small[l])
        ops = [_scatter_op(parts_ffn), _gather_devices_op(small_packed[l])] if l == 0 else None
        (gw_in_t, gw_out), side = _mixer_bwd_w(a["xin"], pd, dh, a["cat"], dmix, l, t_wg, ops)
        gw_mk, gw_mv = _kv_bwd(mem0, dkt, dv)
        g5_mix = halves_view([gw_in_t, gw_mk, gw_mv, gw_out])
        if l == 0:
            small_gathered[l] = side[1][0]
            red_layers[0] = _reduce_group_in_vmem(g5_mix, parts_ffn, side[0])
        else:
            g5_prev = g5_mix + g5_ffn
        gz = dx0
    grad_x = gz[None]

    def shard_grads(red):
        r = [f.reshape(-1, f.shape[-1]) for f in red]
        return dict(w_in=r[0].T, w_mk=r[1], w_mv=r[2], w_out=r[3], w_up=_unpad_ff_cols(r[4].T), w_down=r[5])

    big_grads = [shard_grads(red_layers[l]) for l in range(DEPTH)]

    small_red = _sum_devices(small_gathered)
    sg = _unpack_small(small_red)
    g_conv_a_w = lax.dynamic_slice_in_dim(sg["conv_a_w"], chip * (CONV_W // 4), CONV_W // 4, axis=2)
    g_conv_f_w = lax.dynamic_slice_in_dim(sg["conv_f_w"], chip * FF_Q, FF_Q, axis=2)

    loss = 0.5 / D_MODEL * small_red.reshape(DEPTH, -1)[DEPTH - 1, SMALL_FLOATS]

    grads = dict(conv_a_w=g_conv_a_w, conv_a_b=sg["conv_a_b"], ln_a_g=sg["ln_a_g"], ln_a_b=sg["ln_a_b"],
                 ln_v_g=sg["ln_v_g"], ln_v_b=sg["ln_v_b"], w_s=sg["w_s"], b_s=sg["b_s"], ln1_g=sg["ln1_g"], ln1_b=sg["ln1_b"],
                 conv_f_w=g_conv_f_w, conv_f_b=sg["conv_f_b"], ln2_g=sg["ln2_g"], ln2_b=sg["ln2_b"])
    weights = dict(w_in=w_in, conv_a_w=conv_a_w, conv_a_b=conv_a_b, ln_a_g=ln_a_g, ln_a_b=ln_a_b, ln_v_g=ln_v_g,
                   ln_v_b=ln_v_b, w_s=w_s, b_s=b_s, w_mk=w_mk, w_mv=w_mv, w_out=w_out, ln1_g=ln1_g, ln1_b=ln1_b,
                   w_up=w_up, conv_f_w=conv_f_w, conv_f_b=conv_f_b, w_down=w_down, ln2_g=ln2_g, ln2_b=ln2_b)
    mom_m = dict(w_in=m_w_in, conv_a_w=m_conv_a_w, conv_a_b=m_conv_a_b, ln_a_g=m_ln_a_g, ln_a_b=m_ln_a_b, ln_v_g=m_ln_v_g,
                 ln_v_b=m_ln_v_b, w_s=m_w_s, b_s=m_b_s, w_mk=m_w_mk, w_mv=m_w_mv, w_out=m_w_out, ln1_g=m_ln1_g,
                 ln1_b=m_ln1_b, w_up=m_w_up, conv_f_w=m_conv_f_w, conv_f_b=m_conv_f_b, w_down=m_w_down, ln2_g=m_ln2_g,
                 ln2_b=m_ln2_b)
    mom_v = dict(w_in=v_w_in, conv_a_w=v_conv_a_w, conv_a_b=v_conv_a_b, ln_a_g=v_ln_a_g, ln_a_b=v_ln_a_b, ln_v_g=v_ln_v_g,
                 ln_v_b=v_ln_v_b, w_s=v_w_s, b_s=v_b_s, w_mk=v_w_mk, w_mv=v_w_mv, w_out=v_w_out, ln1_g=v_ln1_g,
                 ln1_b=v_ln1_b, w_up=v_w_up, conv_f_w=v_conv_f_w, conv_f_b=v_conv_f_b, w_down=v_w_down, ln2_g=v_ln2_g,
                 ln2_b=v_ln2_b)
    names = list(weights)
    big_names = ["w_in", "w_mk", "w_mv", "w_out", "w_up", "w_down"]
    delta, new_m, new_v = {}, {}, {}
    for n in big_names:
        grads[n], delta[n], new_m[n], new_v[n] = _adamw_layers(weights[n], [big_grads[l][n] for l in range(DEPTH)],
                                                               mom_m[n], mom_v[n], "adamw_" + n)
    small_names = [n for n in names if n not in big_names]
    ds, nms, nvs = _adamw_small([weights[n] for n in small_names], [grads[n] for n in small_names],
                                [mom_m[n] for n in small_names], [mom_v[n] for n in small_names])
    for n, d, nm, nv in zip(small_names, ds, nms, nvs):
        delta[n], new_m[n], new_v[n] = d, nm, nv

    return (loss, grad_x, *[grads[n] for n in names], *[delta[n] for n in names],
            *[new_m[n] for n in names], *[new_v[n] for n in names])
```
